```python
import math
import jax
import jax.numpy as jnp
from jax import lax
import numpy as np

D_MODEL = 1024
BATCH = 8
SEQ = 2048
DEPTH = 2

GRID_W = 64
CTX_LEN = 256
Q_BLOCK = 128
ROPE_BASE = 10000.0
EPS = 1e-6
DA_HEADS = 4
DA_DIM = 32
DA_VDIM = 64
S5_GROUPS = 16
S5_CH = 16
S5_STATE = 64
S5_WIDTH = S5_GROUPS * S5_CH
MLA_HEADS = 4
MLA_NOPE = 32
MLA_ROPE = 16
MLA_VDIM = 64
MLA_Q_RANK = 192
MLA_KV_RANK = 128
RW_HEADS = 4
RW_DIM = 64
RW_WIDTH = RW_HEADS * RW_DIM
RW_DECAY_RANK = 32
RW_ICL_RANK = 32
RW_GATE_RANK = 64
RW_LN_EPS = 64e-5
N_BRANCH = 4
BRANCH_WIDTH = 256
N_EXPERTS = 16
N_GROUPS = 4
EXPERTS_PER_GROUP = N_EXPERTS // N_GROUPS
TOP_K = 2
D_FF = 512

IN_SIZES = (2 * DA_HEADS * DA_DIM, 2 * DA_HEADS * DA_DIM, DA_HEADS * DA_VDIM, S5_WIDTH,
            MLA_Q_RANK, MLA_KV_RANK, MLA_ROPE, RW_WIDTH, RW_WIDTH, RW_WIDTH, RW_WIDTH,
            N_BRANCH * D_MODEL)
IN_OFFSETS = tuple(int(o) for o in np.cumsum(IN_SIZES)[:-1])
N_IN = int(sum(IN_SIZES))

kernel_name = 'hybrid_prefix_dit_block'


def rms_norm(x, g, eps=EPS):
    xf = x.astype(jnp.float32)
    xf = xf * lax.rsqrt(jnp.mean(jnp.square(xf), axis=-1, keepdims=True) + eps)
    return (xf * g.astype(jnp.float32)).astype(x.dtype)


def modulate(x, g, shift, scale):
    return rms_norm(x, g) * (1 + scale) + shift


def axial_rope_tables(n_tokens, rot_dim):
    rows = n_tokens // GRID_W
    row = jnp.repeat(jnp.arange(rows, dtype=jnp.float32), GRID_W)
    col = jnp.tile(jnp.arange(GRID_W, dtype=jnp.float32), rows)
    n_freq = rot_dim // 4
    inv_freq = ROPE_BASE ** (-jnp.arange(n_freq, dtype=jnp.float32) / n_freq)
    ang = jnp.concatenate([row[:, None] * inv_freq, col[:, None] * inv_freq], axis=-1)
    return jnp.cos(ang), jnp.sin(ang)


def apply_rope(x, cos, sin):
    half = x.shape[-1] // 2
    x1, x2 = x[..., :half], x[..., half:]
    c = cos[:, None, :].astype(x.dtype)
    s = sin[:, None, :].astype(x.dtype)
    return jnp.concatenate([x1 * c - x2 * s, x2 * c + x1 * s], axis=-1)


def attend(q, k, v, scale):
    b, m, h, lq, d = q.shape
    nb = lq // Q_BLOCK
    qb = jnp.moveaxis(q.reshape(b, m, h, nb, Q_BLOCK, d), 3, 0)

    def block(qi):
        s = jnp.einsum('bmhqd,bmhkd->bmhqk', qi, k).astype(jnp.float32) * scale
        p = jax.nn.softmax(s, axis=-1).astype(v.dtype)
        return jnp.einsum('bmhqk,bhkd->bmhqd', p, v)

    o = lax.map(block, qb)
    return jnp.moveaxis(o, 0, 3).reshape(b, m, h, lq, v.shape[-1])


def diff_attention(q_l, k_l, v_l, q_c, k_c, v_c, cos, sin, qk_g, lam, subln_g, lambda_init, need_ctx):
    def prep(q, k, v, rotate):
        b, n, _ = q.shape
        q = rms_norm(q.reshape(b, n, 2 * DA_HEADS, DA_DIM), qk_g[0])
        k = rms_norm(k.reshape(b, n, 2 * DA_HEADS, DA_DIM), qk_g[1])
        if rotate:
            q, k = apply_rope(q, cos, sin), apply_rope(k, cos, sin)
        q = q.reshape(b, n, DA_HEADS, 2, DA_DIM).transpose(0, 3, 2, 1, 4)
        k = k.reshape(b, n, DA_HEADS, 2, DA_DIM).transpose(0, 3, 2, 1, 4)
        v = v.reshape(b, n, DA_HEADS, DA_VDIM).transpose(0, 2, 1, 3)
        return q, k, v

    ql, kl, vl = prep(q_l, k_l, v_l, True)
    qc, kc, vc = prep(q_c, k_c, v_c, False)
    lam32 = lam.astype(jnp.float32)
    lmbda = jnp.exp(jnp.sum(lam32[0] * lam32[1])) - jnp.exp(jnp.sum(lam32[2] * lam32[3])) + lambda_init
    scale = DA_DIM ** -0.5

    def combine(o):
        o = o[:, 0] - lmbda.astype(o.dtype) * o[:, 1]
        o = rms_norm(o, subln_g) * (1 - lambda_init)
        b, h, n, dv = o.shape
        return o.transpose(0, 2, 1, 3).reshape(b, n, h * dv)

    k_all = jnp.concatenate([kl, kc], axis=3)
    v_all = jnp.concatenate([vl, vc], axis=2)
    y_lat = combine(attend(ql, k_all, v_all, scale))
    y_ctx = combine(attend(qc, kc, vc, scale)) if need_ctx else None
    return y_lat, y_ctx


def _linear_combine(e1, e2):
    a1, b1 = e1
    a2, b2 = e2
    return a2 * a1, a2 * b1 + b2


def s5_scan(u, lam_re, lam_im, log_dt, b_re, b_im, h0, reverse):
    lam = lax.complex(lam_re.astype(jnp.float32), lam_im.astype(jnp.float32))
    dt = jnp.exp(log_dt.astype(jnp.float32))[:, None]
    a_bar = jnp.exp(lam * dt)
    b_mat = lax.complex(b_re.astype(jnp.float32), b_im.astype(jnp.float32))
    b_bar = ((a_bar - 1.0) / lam)[..., None] * b_mat
    bu = jnp.einsum('bngc,gpc->bngp', u.astype(jnp.complex64), b_bar)
    if h0 is not None:
        edge = -1 if reverse else 0
        bu = bu.at[:, edge].add(a_bar * h0)
    a = jnp.broadcast_to(a_bar, bu.shape)
    _, h = lax.associative_scan(_linear_combine, (a, bu), reverse=reverse, axis=1)
    return h


def s5_readout(h, c_re, c_im):
    return (jnp.einsum('bngp,gcp->bngc', h.real, c_re.astype(jnp.float32))
            - jnp.einsum('bngp,gcp->bngc', h.imag, c_im.astype(jnp.float32)))


def s5_mixer(u_l, u_c, lam_re, lam_im, log_dt, b_re, b_im, c_re, c_im, d, w_glu, b_glu, need_ctx):
    out_dtype = u_l.dtype

    def groups(u):
        return u.astype(jnp.float32).reshape(u.shape[0], u.shape[1], S5_GROUPS, S5_CH)

    ul, uc = groups(u_l), groups(u_c)
    ys_l = [d * ul]
    ys_c = [d * uc]
    for dr in range(2):
        rev = dr == 1
        prm = (lam_re[dr], lam_im[dr], log_dt[dr], b_re[dr], b_im[dr])
        h_c = s5_scan(uc, *prm, None, rev)
        h_final = h_c[:, 0] if rev else h_c[:, -1]
        h_l = s5_scan(ul, *prm, h_final, rev)
        ys_l.append(s5_readout(h_l, c_re[dr], c_im[dr]))
        if need_ctx:
            ys_c.append(s5_readout(h_c, c_re[dr], c_im[dr]))

    def glu(ys):
        y = sum(ys)
        b, n = y.shape[:2]
        z = jax.nn.gelu(y.reshape(b, n, S5_WIDTH))
        return (z * jax.nn.sigmoid(z @ w_glu + b_glu)).astype(out_dtype)

    return glu(ys_l), (glu(ys_c) if need_ctx else None)


def mla(cq_l, ckv_l, kr_l, cq_c, ckv_c, kr_c, cos, sin, cq_g, ckv_g, w_uq, w_ukv, qk_g, need_ctx):
    def prep(cq, ckv, kr, rotate):
        b, n, _ = cq.shape
        q = (rms_norm(cq, cq_g) @ w_uq).reshape(b, n, MLA_HEADS, MLA_NOPE + MLA_ROPE)
        kv = (rms_norm(ckv, ckv_g) @ w_ukv).reshape(b, n, MLA_HEADS, MLA_NOPE + MLA_VDIM)
        k_nope, v = kv[..., :MLA_NOPE], kv[..., MLA_NOPE:]
        k_rope = jnp.broadcast_to(kr[:, :, None, :], (b, n, MLA_HEADS, MLA_ROPE))
        k = jnp.concatenate([k_nope, k_rope], axis=-1)
        q = rms_norm(q, qk_g[0])
        k = rms_norm(k, qk_g[1])
        if rotate:
            q = jnp.concatenate([q[..., :MLA_NOPE], apply_rope(q[..., MLA_NOPE:], cos, sin)], axis=-1)
            k = jnp.concatenate([k[..., :MLA_NOPE], apply_rope(k[..., MLA_NOPE:], cos, sin)], axis=-1)
        return (q.transpose(0, 2, 1, 3)[:, None], k.transpose(0, 2, 1, 3)[:, None],
                v.transpose(0, 2, 1, 3))

    ql, kl, vl = prep(cq_l, ckv_l, kr_l, True)
    qc, kc, vc = prep(cq_c, ckv_c, kr_c, False)
    scale = (MLA_NOPE + MLA_ROPE) ** -0.5

    def heads_out(o):
        o = o[:, 0]
        b, h, n, dv = o.shape
        return o.transpose(0, 2, 1, 3).reshape(b, n, h * dv)

    k_all = jnp.concatenate([kl, kc], axis=3)
    v_all = jnp.concatenate([vl, vc], axis=2)
    y_lat = heads_out(attend(ql, k_all, v_all, scale))
    y_ctx = heads_out(attend(qc, kc, vc, scale)) if need_ctx else None
    return y_lat, y_ctx


def centred_shift(x):
    pad = jnp.pad(x, ((0, 0), (1, 1), (0, 0)))
    return 0.5 * (pad[:, :-2] + pad[:, 2:])


def rwkv_direction(xd, k, w0, w1, w2, a0, a1, a2, k_a):
    w_raw = w0 + jnp.tanh(xd @ w1) @ w2
    decay = jnp.exp(-jnp.exp(-jax.nn.softplus(-w_raw) - 0.5))
    a = jax.nn.sigmoid(a0 + (xd @ a1) @ a2)
    k_mod = k * (1 + (a - 1) * k_a)
    return decay, a, k_mod


def rwkv_scan(s0, r, w, k, v, kk, a, reverse):
    xs = tuple(jnp.moveaxis(t, 1, 0) for t in (r, w, k, v, kk, a))

    def step(s, inp):
        r_t, w_t, k_t, v_t, kk_t, a_t = inp
        sa = jnp.einsum('bhvk,bhk->bhv', s, -kk_t)
        s = (s * w_t[:, :, None, :] + sa[..., None] * (kk_t * a_t)[:, :, None, :]
             + v_t[..., None] * k_t[:, :, None, :])
        return s, jnp.einsum('bhvk,bhk->bhv', s, r_t)

    s_final, ys = lax.scan(step, s0, xs, reverse=reverse)
    return s_final, jnp.moveaxis(ys, 0, 1)


def rwkv_bonus(r, k, v, r_k):
    return jnp.sum(r * k * r_k, axis=-1, keepdims=True) * v


def rwkv_mixer(r_l, k_l, v_l, d_l, r_c, k_c, v_c, d_c, mu, w0, w1, w2, a0, a1, a2, g1, g2,
               k_k, k_a, r_k, ln_g, ln_b, need_ctx):
    out_dtype = r_l.dtype

    def streams(*ss):
        return [(s + (centred_shift(s) - s) * mu[i]).astype(jnp.float32) for i, s in enumerate(ss)]

    def heads(t):
        return t.reshape(t.shape[0], t.shape[1], RW_HEADS, RW_DIM)

    def removal_key(k):
        kk = heads(k * k_k)
        return kk / jnp.maximum(jnp.sqrt(jnp.sum(kk * kk, axis=-1, keepdims=True)), 1e-12)

    rl, kl, vl, xl = streams(r_l, k_l, v_l, d_l)
    rc, kc, vc, xc = streams(r_c, k_c, v_c, d_c)
    kk_l, kk_c = removal_key(kl), removal_key(kc)
    s0 = jnp.zeros((rc.shape[0], RW_HEADS, RW_DIM, RW_DIM), jnp.float32)
    ys_l, bon_l, ys_c, bon_c = [], [], [], []
    for dr in range(2):
        rev = dr == 1
        prm = (w0[dr], w1[dr], w2[dr], a0[dr], a1[dr], a2[dr], k_a)
        dec_c, a_c, km_c = rwkv_direction(xc, kc, *prm)
        s_c, y_c = rwkv_scan(s0, heads(rc), heads(dec_c), heads(km_c), heads(vc), kk_c, heads(a_c), rev)
        dec_l, a_l, km_l = rwkv_direction(xl, kl, *prm)
        _, y_l = rwkv_scan(s_c, heads(rl), heads(dec_l), heads(km_l), heads(vl), kk_l, heads(a_l), rev)
        ys_l.append(y_l)
        bon_l.append(rwkv_bonus(heads(rl), heads(km_l), heads(vl), r_k))
        if need_ctx:
            ys_c.append(y_c)
            bon_c.append(rwkv_bonus(heads(rc), heads(km_c), heads(vc), r_k))

    def finish(ys, bons, xd):
        y = sum(ys)
        mean = jnp.mean(y, axis=-1, keepdims=True)
        var = jnp.mean(jnp.square(y - mean), axis=-1, keepdims=True)
        y = (y - mean) * lax.rsqrt(var + RW_LN_EPS)
        b, n = y.shape[:2]
        y = y.reshape(b, n, RW_WIDTH) * ln_g + ln_b + sum(bons).reshape(b, n, RW_WIDTH)
        g = jax.nn.sigmoid(xd @ g1) @ g2
        return (y * g).astype(out_dtype)

    y_lat = finish(ys_l, bon_l, xl)
    y_ctx = finish(ys_c, bon_c, xc) if need_ctx else None
    return y_lat, y_ctx


def merge_branches(branches, gate_cols, w_branch_l, w_out_l):
    b, n, _ = gate_cols.shape
    gates = jax.nn.sigmoid(gate_cols.reshape(b, n, N_BRANCH, D_MODEL))
    merged = sum(gates[:, :, i] * (y @ w_branch_l[i]) for i, y in enumerate(branches))
    return merged @ w_out_l


def moe(h, router_w, router_bias, w_gate, w_up, w_down):
    t = h.shape[0]
    scores = jax.nn.sigmoid((h @ router_w).astype(jnp.float32))
    biased = (scores + router_bias.astype(jnp.float32)).reshape(t, N_GROUPS, EXPERTS_PER_GROUP)
    group_score = jnp.sum(lax.top_k(biased, TOP_K)[0], axis=-1)
    in_group = jax.nn.one_hot(jnp.argmax(group_score, axis=-1), N_GROUPS, dtype=jnp.bool_)
    masked = jnp.where(in_group[..., None], biased, -jnp.inf).reshape(t, N_EXPERTS)
    _, idx = lax.top_k(masked, TOP_K)
    w = jnp.take_along_axis(scores, idx, axis=-1)
    w = w / jnp.sum(w, axis=-1, keepdims=True)
    combine = jnp.sum(jax.nn.one_hot(idx, N_EXPERTS, dtype=jnp.float32) * w[..., None], axis=1).astype(h.dtype)
    out = 0
    for e in range(N_EXPERTS):
        act = jax.nn.silu(h @ w_gate[e]) * (h @ w_up[e])
        out = out + combine[:, e:e + 1] * (act @ w_down[e])
    return out


def setup_inputs(seed: int = 0) -> dict:
    key = jax.random.key(seed)
    keys = iter(jax.random.split(key, 64))
    f32 = jnp.float32

    def nrm(shape, scale):
        return scale * jax.random.normal(next(keys), shape, f32)

    def unif(shape, lo, hi):
        return jax.random.uniform(next(keys), shape, f32, lo, hi)

    def gain(shape):
        return 1.0 + nrm(shape, 0.05)

    dm = D_MODEL
    g, p, ch = S5_GROUPS, S5_STATE, S5_CH
    e = N_EXPERTS
    n_idx = jnp.arange(p, dtype=f32)
    return {
        'x': nrm((BATCH, SEQ, dm), 1.0),
        'c': nrm((BATCH, dm), 1.0),
        'ctx': nrm((BATCH, CTX_LEN, dm), 1.0),
        'c_ctx': nrm((dm,), 1.0),
        'w_ada': nrm((DEPTH, dm, 6 * dm), 0.5 * dm ** -0.5),
        'b_ada': nrm((DEPTH, 6 * dm), 0.02),
        'norm_mix_g': gain((DEPTH, dm)),
        'norm_ffn_g': gain((DEPTH, dm)),
        'w_in': nrm((DEPTH, dm, N_IN), dm ** -0.5),
        'da_qk_norm_g': gain((DEPTH, 2, DA_DIM)),
        'da_lambda': nrm((DEPTH, 4, DA_DIM), 0.1),
        'da_subln_g': gain((DEPTH, DA_VDIM)),
        's5_lam_re': -0.5 + nrm((DEPTH, 2, g, p), 0.01),
        's5_lam_im': jnp.broadcast_to(math.pi * n_idx, (DEPTH, 2, g, p)),
        's5_log_dt': unif((DEPTH, 2, g), math.log(1e-3), math.log(1e-1)),
        's5_b_re': nrm((DEPTH, 2, g, p, ch), (2 * ch) ** -0.5),
        's5_b_im': nrm((DEPTH, 2, g, p, ch), (2 * ch) ** -0.5),
        's5_c_re': nrm((DEPTH, 2, g, ch, p), p ** -0.5),
        's5_c_im': nrm((DEPTH, 2, g, ch, p), p ** -0.5),
        's5_d': nrm((DEPTH, g, ch), 1.0),
        's5_w_glu': nrm((DEPTH, S5_WIDTH, S5_WIDTH), S5_WIDTH ** -0.5),
        's5_b_glu': nrm((DEPTH, S5_WIDTH), 0.02),
        'mla_cq_norm_g': gain((DEPTH, MLA_Q_RANK)),
        'mla_ckv_norm_g': gain((DEPTH, MLA_KV_RANK)),
        'mla_w_uq': nrm((DEPTH, MLA_Q_RANK, MLA_HEADS * (MLA_NOPE + MLA_ROPE)), MLA_Q_RANK ** -0.5),
        'mla_w_ukv': nrm((DEPTH, MLA_KV_RANK, MLA_HEADS * (MLA_NOPE + MLA_VDIM)), MLA_KV_RANK ** -0.5),
        'mla_qk_norm_g': gain((DEPTH, 2, MLA_NOPE + MLA_ROPE)),
        'rw_mu': unif((DEPTH, 4, RW_WIDTH), 0.0, 1.0),
        'rw_w0': nrm((DEPTH, 2, RW_WIDTH), 1.0),
        'rw_w1': nrm((DEPTH, 2, RW_WIDTH, RW_DECAY_RANK), RW_WIDTH ** -0.5),
        'rw_w2': nrm((DEPTH, 2, RW_DECAY_RANK, RW_WIDTH), 0.1 * RW_DECAY_RANK ** -0.5),
        'rw_a0': nrm((DEPTH, 2, RW_WIDTH), 0.5),
        'rw_a1': nrm((DEPTH, 2, RW_WIDTH, RW_ICL_RANK), RW_WIDTH ** -0.5),
        'rw_a2': nrm((DEPTH, 2, RW_ICL_RANK, RW_WIDTH), 0.1 * RW_ICL_RANK ** -0.5),
        'rw_g1': nrm((DEPTH, RW_WIDTH, RW_GATE_RANK), RW_WIDTH ** -0.5),
        'rw_g2': nrm((DEPTH, RW_GATE_RANK, RW_WIDTH), RW_GATE_RANK ** -0.5),
        'rw_k_k': 0.85 + nrm((DEPTH, RW_WIDTH), 0.05),
        'rw_k_a': gain((DEPTH, RW_WIDTH)),
        'rw_r_k': nrm((DEPTH, RW_HEADS, RW_DIM), 0.1),
        'rw_ln_g': gain((DEPTH, RW_WIDTH)),
        'rw_ln_b': nrm((DEPTH, RW_WIDTH), 0.02),
        'w_branch': nrm((DEPTH, N_BRANCH, BRANCH_WIDTH, dm), BRANCH_WIDTH ** -0.5),
        'w_out': nrm((DEPTH, dm, dm), dm ** -0.5),
        'router_w': nrm((dm, e), dm ** -0.5),
        'router_bias': nrm((e,), 0.01),
        'exp_w_gate': nrm((DEPTH, e, dm, D_FF), dm ** -0.5),
        'exp_w_up': nrm((DEPTH, e, dm, D_FF), dm ** -0.5),
        'exp_w_down': nrm((DEPTH, e, D_FF, dm), D_FF ** -0.5),
    }


def reference(x, c, ctx, c_ctx, w_ada, b_ada, norm_mix_g, norm_ffn_g, w_in,
              da_qk_norm_g, da_lambda, da_subln_g,
              s5_lam_re, s5_lam_im, s5_log_dt, s5_b_re, s5_b_im, s5_c_re, s5_c_im, s5_d, s5_w_glu, s5_b_glu,
              mla_cq_norm_g, mla_ckv_norm_g, mla_w_uq, mla_w_ukv, mla_qk_norm_g,
              rw_mu, rw_w0, rw_w1, rw_w2, rw_a0, rw_a1, rw_a2, rw_g1, rw_g2, rw_k_k, rw_k_a, rw_r_k,
              rw_ln_g, rw_ln_b,
              w_branch, w_out, router_w, router_bias, exp_w_gate, exp_w_up, exp_w_down):
    b, n_lat, _ = x.shape
    cos_da, sin_da = axial_rope_tables(n_lat, DA_DIM)
    cos_mla, sin_mla = axial_rope_tables(n_lat, MLA_ROPE)
    cx = ctx
    for l in range(DEPTH):
        need_ctx = l < DEPTH - 1
        lambda_init = 0.8 - 0.6 * math.exp(-0.3 * l)
        mod = (jax.nn.silu(c) @ w_ada[l] + b_ada[l]).reshape(b, 6, 1, D_MODEL)
        mod_c = (jax.nn.silu(c_ctx) @ w_ada[l] + b_ada[l]).reshape(6, D_MODEL)

        h_l = modulate(x, norm_mix_g[l], mod[:, 0], mod[:, 1])
        h_c = modulate(cx, norm_mix_g[l], mod_c[0], mod_c[1])
        p_l = jnp.split(h_l @ w_in[l], IN_OFFSETS, axis=-1)
        p_c = jnp.split(h_c @ w_in[l], IN_OFFSETS, axis=-1)
        ya_l, ya_c = diff_attention(p_l[0], p_l[1], p_l[2], p_c[0], p_c[1], p_c[2], cos_da, sin_da,
                                    da_qk_norm_g[l], da_lambda[l], da_subln_g[l], lambda_init, need_ctx)
        yb_l, yb_c = s5_mixer(p_l[3], p_c[3], s5_lam_re[l], s5_lam_im[l], s5_log_dt[l], s5_b_re[l],
                              s5_b_im[l], s5_c_re[l], s5_c_im[l], s5_d[l], s5_w_glu[l], s5_b_glu[l], need_ctx)
        yc_l, yc_c = mla(p_l[4], p_l[5], p_l[6], p_c[4], p_c[5], p_c[6], cos_mla, sin_mla,
                         mla_cq_norm_g[l], mla_ckv_norm_g[l], mla_w_uq[l], mla_w_ukv[l], mla_qk_norm_g[l],
                         need_ctx)
        yd_l, yd_c = rwkv_mixer(p_l[7], p_l[8], p_l[9], p_l[10], p_c[7], p_c[8], p_c[9], p_c[10],
                                rw_mu[l], rw_w0[l], rw_w1[l], rw_w2[l], rw_a0[l], rw_a1[l], rw_a2[l],
                                rw_g1[l], rw_g2[l], rw_k_k[l], rw_k_a[l], rw_r_k[l], rw_ln_g[l], rw_ln_b[l],
                                need_ctx)
        x = x + mod[:, 2] * merge_branches([ya_l, yb_l, yc_l, yd_l], p_l[11], w_branch[l], w_out[l])
        if need_ctx:
            cx = cx + mod_c[2] * merge_branches([ya_c, yb_c, yc_c, yd_c], p_c[11], w_branch[l], w_out[l])

        f_l = modulate(x, norm_ffn_g[l], mod[:, 3], mod[:, 4])
        if need_ctx:
            f_c = modulate(cx, norm_ffn_g[l], mod_c[3], mod_c[4])
            toks = jnp.concatenate([f_l.reshape(-1, D_MODEL), f_c.reshape(-1, D_MODEL)], axis=0)
            y = moe(toks, router_w, router_bias, exp_w_gate[l], exp_w_up[l], exp_w_down[l])
            x = x + mod[:, 5] * y[: b * n_lat].reshape(x.shape)
            cx = cx + mod_c[5] * y[b * n_lat:].reshape(cx.shape)
        else:
            y = moe(f_l.reshape(-1, D_MODEL), router_w, router_bias, exp_w_gate[l], exp_w_up[l], exp_w_down[l])
            x = x + mod[:, 5] * y.reshape(x.shape)
    return x
```

```python
import functools
import math

import numpy as np
import jax
import jax.numpy as jnp
from jax import lax
from jax.experimental import pallas as pl
from jax.experimental.pallas import tpu as pltpu

F32 = jnp.float32
BF16 = jnp.bfloat16

D_MODEL = 1024
GRID_W = 64
ROPE_BASE = 10000.0
EPS = 1e-6
DA_HEADS, DA_DIM, DA_VDIM = 4, 32, 64
S5_GROUPS, S5_CH, S5_STATE = 16, 16, 64
MLA_HEADS, MLA_NOPE, MLA_ROPE, MLA_VDIM = 4, 32, 16, 64
MLA_Q_RANK, MLA_KV_RANK = 192, 128
MLA_HEAD_PAD = 64
RW_HEADS, RW_DIM = 4, 64
RW_LN_EPS = 64e-5
N_BRANCH, BRANCH_WIDTH = 4, 256
N_EXPERTS, N_GROUPS, EXPERTS_PER_GROUP = 16, 4, 4
D_FF = 512
MIX_W = 256

S5_CHUNK = 32
RW_TCHUNK = 128

_DA_W, _S5_W, _MLA_W, _RW_W = 768, 256, 640, 1024
_MIX_COLS = _DA_W + _S5_W + _MLA_W + _RW_W

V7X_VMEM_BYTES = 64 * 2**20
_VMEM_LIMIT = V7X_VMEM_BYTES - 8 * 2**20


def _cparams(*sem):
    return pltpu.CompilerParams(dimension_semantics=sem, vmem_limit_bytes=_VMEM_LIMIT)


def _split_dot(x, w, terms=2):
    acc = None
    rem = x
    for i in range(terms):
        part = rem.astype(BF16)
        d = jnp.dot(part, w, preferred_element_type=F32)
        acc = d if acc is None else acc + d
        if i + 1 < terms:
            rem = rem - part.astype(F32)
    return acc


def _modulate(x, g, shift, scale):
    xn = x * lax.rsqrt(jnp.mean(x * x, axis=-1, keepdims=True) + EPS)
    return xn * g * (1.0 + scale) + shift


def _sigmoid(x):
    return 1.0 / (1.0 + jnp.exp(-x))


def _group_rms(x, ones_bd, inv_n, gain):
    ms = _split_dot(x * x, ones_bd) * inv_n
    return x * lax.rsqrt(ms + EPS) * gain


def _rope(x, cos_t, sin_t, half, period, first_end):
    n = x.shape[1]
    lane = lax.broadcasted_iota(jnp.int32, x.shape, 1)
    up = pltpu.roll(x, n - half, axis=1)
    down = pltpu.roll(x, half, axis=1)
    partner = jnp.where((lane & (period - 1)) < first_end, up, down)
    return x * cos_t + partner * sin_t


def _ada_kernel(c_ref, w_ref, b_ref, o_ref):
    c = c_ref[...]
    s = c * _sigmoid(c)
    o_ref[0] = jnp.dot(s.astype(BF16), w_ref[0].astype(BF16), preferred_element_type=F32) + b_ref[0]


def _ada_call(cc, w_ada, b_ada):
    depth, dm, n = w_ada.shape
    tn = 1536
    return pl.pallas_call(
        _ada_kernel,
        grid=(depth, n // tn),
        in_specs=[
            pl.BlockSpec(cc.shape, lambda l, j: (0, 0)),
            pl.BlockSpec((1, dm, tn), lambda l, j: (l, 0, j)),
            pl.BlockSpec((1, 1, tn), lambda l, j: (l, 0, j)),
        ],
        out_specs=pl.BlockSpec((1, cc.shape[0], tn), lambda l, j: (l, 0, j)),
        out_shape=jax.ShapeDtypeStruct((depth, cc.shape[0], n), F32),
        compiler_params=_cparams("parallel", "parallel"),
        name="ada_mod",
    )(cc, w_ada, b_ada.reshape(depth, 1, n))


def _inproj_kernel(x_ref, mod_ref, g_ref, w_ref, da_ref, s5_ref, mla_ref, rw_ref):
    h = _modulate(x_ref[...], g_ref[...], mod_ref[0, 0:1, :], mod_ref[0, 1:2, :])
    acc = jnp.dot(h.astype(BF16), w_ref[...], preferred_element_type=F32)
    da_ref[...] = acc[:, 0:_DA_W]
    s5_ref[...] = acc[:, _DA_W:_DA_W + _S5_W]
    mla_ref[...] = acc[:, _DA_W + _S5_W:_DA_W + _S5_W + _MLA_W]
    rw_ref[...] = acc[:, _DA_W + _S5_W + _MLA_W:_MIX_COLS]


def _seg_map(tiles_per_batch, n_batch):
    return lambda i: (jnp.minimum(i // tiles_per_batch, n_batch), 0, 0)


def _inproj_call(x_all, mod, g, w_mix, tm, tiles_per_batch, n_batch):
    t = x_all.shape[0]
    widths = (_DA_W, _S5_W, _MLA_W, _RW_W)
    return pl.pallas_call(
        _inproj_kernel,
        grid=(t // tm,),
        in_specs=[
            pl.BlockSpec((tm, D_MODEL), lambda i: (i, 0)),
            pl.BlockSpec((1, 6, D_MODEL), _seg_map(tiles_per_batch, n_batch)),
            pl.BlockSpec((1, D_MODEL), lambda i: (0, 0)),
            pl.BlockSpec((D_MODEL, _MIX_COLS), lambda i: (0, 0)),
        ],
        out_specs=[pl.BlockSpec((tm, w), lambda i: (i, 0)) for w in widths],
        out_shape=[jax.ShapeDtypeStruct((t, w), F32) for w in widths],
        compiler_params=_cparams("parallel"),
        name="in_proj",
    )(x_all, mod, g, w_mix)


def _qkprep_kernel(da_ref, mla_ref, cda_ref, sda_ref, cml_ref, sml_ref, g32_ref, g64_ref,
                   gda_ref, gml_ref, cqg_ref, ckvg_ref, wuq_ref, wuk_ref, wuv_ref,
                   qd_ref, kd_ref, vd_ref, qm_ref, km_ref, vm_ref):
    g32 = g32_ref[...]
    g64 = g64_ref[...]
    cda, sda = cda_ref[...], sda_ref[...]
    for src, gi, dst in ((0, 0, qd_ref), (1, 1, kd_ref)):
        x = da_ref[:, src * MIX_W:(src + 1) * MIX_W]
        y = _group_rms(x, g32, 1.0 / DA_DIM, gda_ref[gi:gi + 1, :])
        dst[...] = _rope(y, cda, sda, DA_DIM // 2, DA_DIM, DA_DIM // 2).astype(BF16)
    vd_ref[...] = da_ref[:, 2 * MIX_W:3 * MIX_W].astype(BF16)

    cml, sml = cml_ref[...], sml_ref[...]
    cq = mla_ref[:, 0:256]
    cqn = cq * lax.rsqrt(jnp.sum(cq * cq, axis=-1, keepdims=True) * (1.0 / MLA_Q_RANK) + EPS) * cqg_ref[...]
    q = jnp.dot(cqn.astype(BF16), wuq_ref[...], preferred_element_type=F32)
    ckv = mla_ref[:, 256:384]
    ckvn = ckv * lax.rsqrt(jnp.mean(ckv * ckv, axis=-1, keepdims=True) + EPS) * ckvg_ref[...]
    ckvb = ckvn.astype(BF16)
    k = jnp.dot(ckvb, wuk_ref[...], preferred_element_type=F32) + mla_ref[:, 384:640]
    vm_ref[...] = jnp.dot(ckvb, wuv_ref[...], preferred_element_type=F32).astype(BF16)
    inv_n = 1.0 / (MLA_NOPE + MLA_ROPE)
    half = MLA_ROPE // 2
    for x, gi, dst in ((q, 0, qm_ref), (k, 1, km_ref)):
        y = _group_rms(x, g64, inv_n, gml_ref[gi:gi + 1, :])
        dst[...] = _rope(y, cml, sml, half, MLA_HEAD_PAD, MLA_NOPE + half).astype(BF16)


def _qkprep_call(da, mla, tabs, consts, tm, tiles_per_seq, n_lat_tiles):
    t = da.shape[0]
    tab_map = lambda i: (jnp.where(i < n_lat_tiles, i % tiles_per_seq, tiles_per_seq), 0)
    full = lambda a: pl.BlockSpec(a.shape, lambda i: (0,) * a.ndim)
    in_specs = [pl.BlockSpec((tm, _DA_W), lambda i: (i, 0)), pl.BlockSpec((tm, _MLA_W), lambda i: (i, 0))]
    in_specs += [pl.BlockSpec((tm, MIX_W), tab_map) for _ in tabs]
    in_specs += [full(a) for a in consts]
    return pl.pallas_call(
        _qkprep_kernel,
        grid=(t // tm,),
        in_specs=in_specs,
        out_specs=[pl.BlockSpec((tm, MIX_W), lambda i: (i, 0)) for _ in range(6)],
        out_shape=[jax.ShapeDtypeStruct((t, MIX_W), BF16) for _ in range(6)],
        compiler_params=_cparams("parallel"),
        name="qk_prep",
    )(da, mla, *tabs, *consts)


def _softmax_parts(q, kt):
    s = jnp.dot(q, kt, preferred_element_type=F32)
    p = jnp.exp(s - jnp.max(s, axis=-1, keepdims=True))
    return p, 1.0 / jnp.sum(p, axis=-1, keepdims=True)


def _diff_attn_kernel(q_ref, kt_ref, v_ref, lam_ref, g_ref, o_ref, *, post_scale):
    lam = lam_ref[...]
    for h in range(DA_HEADS):
        p0, r0 = _softmax_parts(q_ref[0, 2 * h], kt_ref[0, 2 * h])
        p1, r1 = _softmax_parts(q_ref[0, 2 * h + 1], kt_ref[0, 2 * h + 1])
        d = p0 * r0 - p1 * (r1 * lam)
        o = jnp.dot(d.astype(BF16), v_ref[0, h], preferred_element_type=F32)
        o = o * lax.rsqrt(jnp.mean(o * o, axis=-1, keepdims=True) + EPS) * g_ref[...] * post_scale
        o_ref[0, h] = o.astype(BF16)


def _mla_attn_kernel(q_ref, kt_ref, v_ref, o_ref):
    for h in range(MLA_HEADS):
        p, r = _softmax_parts(q_ref[0, h], kt_ref[0, h])
        o = jnp.dot(p.astype(BF16), v_ref[0, h], preferred_element_type=F32) * r
        o_ref[0, h] = o.astype(BF16)


def _attention(q, k, v, n_qh, dk, extra, kernel, name):
    b, nq, _ = q.shape
    nk = k.shape[1]
    n_vh, dv = 4, 64
    qh = q.reshape(b, nq, n_qh, dk).transpose(0, 2, 1, 3)
    kt = k.reshape(b, nk, n_qh, dk).transpose(0, 2, 3, 1)
    vh = v.reshape(b, nk, n_vh, dv).transpose(0, 2, 1, 3)
    tq = min(256, nq)
    in_specs = [
        pl.BlockSpec((1, n_qh, tq, dk), lambda i, j: (i, 0, j, 0)),
        pl.BlockSpec((1, n_qh, dk, nk), lambda i, j: (i, 0, 0, 0)),
        pl.BlockSpec((1, n_vh, nk, dv), lambda i, j: (i, 0, 0, 0)),
    ] + [pl.BlockSpec(a.shape, lambda i, j, nd=a.ndim: (0,) * nd) for a in extra]
    o = pl.pallas_call(
        kernel,
        grid=(b, nq // tq),
        in_specs=in_specs,
        out_specs=pl.BlockSpec((1, n_vh, tq, dv), lambda i, j: (i, 0, j, 0)),
        out_shape=jax.ShapeDtypeStruct((b, n_vh, nq, dv), BF16),
        compiler_params=_cparams("parallel", "parallel"),
        name=name,
    )(qh, kt, vh, *extra)
    return o.transpose(0, 2, 1, 3).reshape(b, nq, n_vh * dv)


def _s5_kernel(u_ref, m_ref, bs_ref, cs_ref, a_ref, y_ref, s_scr, h_scr, *, n_chunks, n_batch):
    u = u_ref[0, 0]
    s_scr[...] = jnp.dot(u, bs_ref[0, 0], preferred_element_type=F32)
    a1 = a_ref[0, 0, 0:1, :]
    a2 = a_ref[0, 0, 1:2, :]

    def step(k, h):
        rows = pl.ds(pl.multiple_of(k * n_batch, n_batch), n_batch)
        h_scr[rows, :] = h
        return a1 * h + a2 * pltpu.roll(h, S5_STATE, axis=1) + s_scr[rows, :]

    lax.fori_loop(0, n_chunks, step, jnp.zeros((n_batch, 2 * S5_STATE), F32))
    y = jnp.dot(u, m_ref[0, 0], preferred_element_type=F32)
    y_ref[0, 0] = y + _split_dot(h_scr[...], cs_ref[0, 0])


def _s5_mats(lam_re, lam_im, log_dt, b_re, b_im, c_re, c_im):
    hp = lax.Precision.HIGHEST
    L = S5_CHUNK
    dt = jnp.exp(log_dt.astype(F32))[..., None]
    z_re, z_im = lam_re.astype(F32) * dt, lam_im.astype(F32) * dt
    lam = lax.complex(lam_re.astype(F32), lam_im.astype(F32))
    a_bar = jnp.exp(lax.complex(z_re, z_im))
    bb = ((a_bar - 1.0) / lam)[..., None] * lax.complex(b_re.astype(F32), b_im.astype(F32))
    j = jnp.arange(L + 1, dtype=F32)[:, None, None, None]
    mag = jnp.exp(z_re[None] * j)
    pw_re, pw_im = mag * jnp.cos(z_im[None] * j), mag * jnp.sin(z_im[None] * j)
    bb_re, bb_im = jnp.real(bb), jnp.imag(bb)
    cre, cim = c_re.astype(F32), c_im.astype(F32)
    x_re = pw_re[..., None] * bb_re[None] - pw_im[..., None] * bb_im[None]
    x_im = pw_re[..., None] * bb_im[None] + pw_im[..., None] * bb_re[None]
    kern = (jnp.einsum('dgcp,jdgpe->dgjce', cre, x_re[:L], precision=hp)
            - jnp.einsum('dgcp,jdgpe->dgjce', cim, x_im[:L], precision=hp))
    s_idx = jnp.arange(L)[:, None]
    t_idx = jnp.arange(L)[None, :]
    lag = t_idx - s_idx
    m = jnp.where((lag >= 0)[None, None, :, :, None, None], kern[:, :, jnp.clip(lag, 0, L - 1)], 0.0)
    m = m.transpose(0, 1, 2, 5, 3, 4).reshape(2, S5_GROUPS, L * S5_CH, L * S5_CH)
    rev = jnp.arange(L - 1, -1, -1)
    bs = jnp.concatenate([x_re[rev], x_im[rev]], axis=3)
    bs = bs.transpose(1, 2, 0, 4, 3).reshape(2, S5_GROUPS, L * S5_CH, 2 * S5_STATE)
    ca_re = cre[None] * pw_re[1:, :, :, None, :] - cim[None] * pw_im[1:, :, :, None, :]
    ca_im = cre[None] * pw_im[1:, :, :, None, :] + cim[None] * pw_re[1:, :, :, None, :]
    cs = jnp.concatenate([ca_re, -ca_im], axis=4)
    cs = cs.transpose(1, 2, 4, 0, 3).reshape(2, S5_GROUPS, 2 * S5_STATE, L * S5_CH)
    al_re, al_im = pw_re[L], pw_im[L]
    a12 = jnp.stack([jnp.concatenate([al_re, al_re], -1), jnp.concatenate([-al_im, al_im], -1)], axis=2)
    return m.astype(BF16), bs.astype(BF16), cs.astype(BF16), a12


def _seq_orders(lat, ctx):
    fwd = jnp.concatenate([ctx, lat], axis=1)
    rev = jnp.concatenate([ctx[:, ::-1], lat[:, ::-1]], axis=1)
    return jnp.stack([fwd, rev])


def _unorder_sum(y, n_ctx):
    lat = y[0][:, n_ctx:] + y[1][:, n_ctx:][:, ::-1]
    ctx = y[0][:, :n_ctx] + y[1][:, :n_ctx][:, ::-1]
    return lat, ctx


def _s5_scan(u_lat, u_ctx, mats):
    m, bs, cs, a12 = mats
    b, n_lat, w = u_lat.shape
    n_ctx = u_ctx.shape[1]
    n_tot = n_lat + n_ctx
    L = S5_CHUNK
    nch = n_tot // L
    rows = nch * b
    u = _seq_orders(u_lat, u_ctx).astype(BF16)
    u = u.reshape(2, b, nch, L, S5_GROUPS, S5_CH).transpose(0, 4, 2, 1, 3, 5).reshape(2, S5_GROUPS, rows, L * S5_CH)
    blk = lambda r, c: pl.BlockSpec((1, 1, r, c), lambda d, g: (d, g, 0, 0))
    y = pl.pallas_call(
        functools.partial(_s5_kernel, n_chunks=nch, n_batch=b),
        grid=(2, S5_GROUPS),
        in_specs=[blk(rows, L * S5_CH), blk(L * S5_CH, L * S5_CH), blk(L * S5_CH, 2 * S5_STATE),
                  blk(2 * S5_STATE, L * S5_CH), blk(2, 2 * S5_STATE)],
        out_specs=blk(rows, L * S5_CH),
        out_shape=jax.ShapeDtypeStruct((2, S5_GROUPS, rows, L * S5_CH), F32),
        scratch_shapes=[pltpu.VMEM((rows, 2 * S5_STATE), F32), pltpu.VMEM((rows, 2 * S5_STATE), F32)],
        compiler_params=_cparams("parallel", "parallel"),
        name="s5_scan",
    )(u, m, bs, cs, a12)
    y = y.reshape(2, S5_GROUPS, nch, b, L, S5_CH).transpose(0, 3, 2, 4, 1, 5).reshape(2, b, n_tot, w)
    return _unorder_sum(y, n_ctx)


def _s5_glu_kernel(u_ref, y_ref, d_ref, w_ref, b_ref, o_ref):
    y = d_ref[...] * u_ref[...] + y_ref[...]
    z = 0.5 * y * (1.0 + jnp.tanh(math.sqrt(2.0 / math.pi) * (y + 0.044715 * (y * y * y))))
    gate = _sigmoid(jnp.dot(z.astype(BF16), w_ref[...], preferred_element_type=F32) + b_ref[...])
    o_ref[...] = (z * gate).astype(BF16)


def _s5_glu_call(u, y, d, w, bias, tm):
    t = y.shape[0]
    row = pl.BlockSpec((tm, MIX_W), lambda i: (i, 0))
    one = pl.BlockSpec((1, MIX_W), lambda i: (0, 0))
    return pl.pallas_call(
        _s5_glu_kernel,
        grid=(t // tm,),
        in_specs=[row, row, one, pl.BlockSpec((MIX_W, MIX_W), lambda i: (0, 0)), one],
        out_specs=row,
        out_shape=jax.ShapeDtypeStruct((t, MIX_W), BF16),
        compiler_params=_cparams("parallel"),
        name="s5_glu",
    )(u, y, d, w, bias)


def _rw_pre_kernel(x_ref, prev_ref, next_ref, mu_ref, g64_ref, kk_g_ref, ka_ref, rk_ref,
                   w0_ref, w1_ref, w2_ref, a0_ref, a1_ref, a2_ref, g1_ref, g2_ref,
                   r_ref, v_ref, kk_ref, wd_ref, kka_ref, km_ref, bon_ref, gate_ref):
    x = x_ref[...]
    n = x.shape[0]
    row = lax.broadcasted_iota(jnp.int32, x.shape, 0)
    left = jnp.where(row == 0, prev_ref[0], pltpu.roll(x, 1, axis=0))
    right = jnp.where(row == n - 1, next_ref[0], pltpu.roll(x, n - 1, axis=0))
    x = x + (0.5 * (left + right) - x) * mu_ref[...]
    r, k, v, xd = (x[:, i * MIX_W:(i + 1) * MIX_W] for i in range(4))
    g64 = g64_ref[...]
    kscaled = k * kk_g_ref[...]
    kk = kscaled / jnp.maximum(jnp.sqrt(_split_dot(kscaled * kscaled, g64)), 1e-12)
    xdb = xd.astype(BF16)
    r_ref[...] = r
    v_ref[...] = v
    kk_ref[...] = kk
    km_sum = None
    for d in range(2):
        lo = jnp.tanh(jnp.dot(xdb, w1_ref[d], preferred_element_type=F32))
        w_raw = w0_ref[d] + jnp.dot(lo.astype(BF16), w2_ref[d], preferred_element_type=F32)
        wd_ref[d] = jnp.exp(-_sigmoid(w_raw) * math.exp(-0.5))
        ar = jnp.dot(xdb, a1_ref[d], preferred_element_type=F32)
        a = _sigmoid(a0_ref[d] + jnp.dot(ar.astype(BF16), a2_ref[d], preferred_element_type=F32))
        km = k * (1.0 + (a - 1.0) * ka_ref[...])
        kka_ref[d] = kk * a
        km_ref[d] = km
        km_sum = km if km_sum is None else km_sum + km
    bon_ref[...] = _split_dot(r * km_sum * rk_ref[...], g64) * v
    gr = _sigmoid(jnp.dot(xdb, g1_ref[...], preferred_element_type=F32))
    gate_ref[...] = jnp.dot(gr.astype(BF16), g2_ref[...], preferred_element_type=F32)


def _rw_pre_call(rw, prev, nxt, consts, tr):
    t = rw.shape[0]
    nt = t // tr
    full = lambda a: pl.BlockSpec(a.shape, lambda i, nd=a.ndim: (0,) * nd)
    row = pl.BlockSpec((tr, MIX_W), lambda i: (i, 0))
    row2 = pl.BlockSpec((2, tr, MIX_W), lambda i: (0, i, 0))
    halo = pl.BlockSpec((1, 1, _RW_W), lambda i: (i, 0, 0))
    sd = jax.ShapeDtypeStruct((t, MIX_W), F32)
    sd2 = jax.ShapeDtypeStruct((2, t, MIX_W), F32)
    return pl.pallas_call(
        _rw_pre_kernel,
        grid=(nt,),
        in_specs=[pl.BlockSpec((tr, _RW_W), lambda i: (i, 0)), halo, halo] + [full(a) for a in consts],
        out_specs=[row, row, row, row2, row2, row2, row, row],
        out_shape=[sd, sd, sd, sd2, sd2, sd2, sd, sd],
        compiler_params=_cparams("parallel"),
        name="rwkv_pre",
    )(rw, prev, nxt, *consts)


def _rw_scan_kernel(r_ref, v_ref, kk_ref, w_ref, kka_ref, km_ref, g64_ref, eye_ref, y_ref, s_scr,
                    *, n_pairs, n_batch, n_steps):
    @pl.when(pl.program_id(0) == 0)
    def _():
        s_scr[...] = jnp.zeros_like(s_scr)

    g64 = g64_ref[...]
    eye4 = eye_ref[...]

    def step(t, carry):
        for p in range(n_pairs):
            d, b = divmod(p, n_batch)
            rowof = lambda ref: ref[d, b, pl.ds(t, 1), :]
            s = s_scr[p]
            sa = _split_dot(s * rowof(kk_ref), g64)
            vt = _split_dot(eye4 * rowof(v_ref), g64)
            s = s * rowof(w_ref) - sa * rowof(kka_ref) + vt * rowof(km_ref)
            s_scr[p] = s
            yb = _split_dot(s * rowof(r_ref), g64)
            y_ref[d, b, pl.ds(t, 1), :] = jnp.sum(yb * eye4, axis=0, keepdims=True)
        return carry

    lax.fori_loop(0, n_steps, step, 0)


def _rw_scan_call(seqs, g64, eye4, tc):
    _, b, n_tot, w = seqs[0].shape
    blk = pl.BlockSpec((2, b, tc, w), lambda i: (0, 0, i, 0))
    return pl.pallas_call(
        functools.partial(_rw_scan_kernel, n_pairs=2 * b, n_batch=b, n_steps=tc),
        grid=(n_tot // tc,),
        in_specs=[blk] * 6 + [pl.BlockSpec(g64.shape, lambda i: (0, 0)), pl.BlockSpec(eye4.shape, lambda i: (0, 0))],
        out_specs=blk,
        out_shape=jax.ShapeDtypeStruct((2, b, n_tot, w), F32),
        scratch_shapes=[pltpu.VMEM((2 * b, RW_DIM, w), F32)],
        compiler_params=_cparams("arbitrary"),
        name="rwkv_scan",
    )(*seqs, g64, eye4)


def _rw_fin_kernel(y_ref, bon_ref, gate_ref, g64_ref, lng_ref, lnb_ref, o_ref):
    y = y_ref[...]
    g64 = g64_ref[...]
    mean = _split_dot(y, g64) * (1.0 / RW_DIM)
    c = y - mean
    var = _split_dot(c * c, g64) * (1.0 / RW_DIM)
    out = c * lax.rsqrt(var + RW_LN_EPS) * lng_ref[...] + lnb_ref[...] + bon_ref[...]
    o_ref[...] = (out * gate_ref[...]).astype(BF16)


def _rw_fin_call(y, bon, gate, g64, lng, lnb, tm):
    t = y.shape[0]
    row = pl.BlockSpec((tm, MIX_W), lambda i: (i, 0))
    one = pl.BlockSpec((1, MIX_W), lambda i: (0, 0))
    return pl.pallas_call(
        _rw_fin_kernel,
        grid=(t // tm,),
        in_specs=[row, row, row, pl.BlockSpec(g64.shape, lambda i: (0, 0)), one, one],
        out_specs=row,
        out_shape=jax.ShapeDtypeStruct((t, MIX_W), BF16),
        compiler_params=_cparams("parallel"),
        name="rwkv_finish",
    )(y, bon, gate, g64, lng, lnb)


def _merge_kernel(x_ref, mod_ref, g_ref, wg_ref, ya_ref, yb_ref, yc_ref, yd_ref, wb_ref, wo_ref, o_ref):
    x = x_ref[...]
    h = _modulate(x, g_ref[...], mod_ref[0, 0:1, :], mod_ref[0, 1:2, :]).astype(BF16)
    merged = None
    for i, y_ref in enumerate((ya_ref, yb_ref, yc_ref, yd_ref)):
        gate = _sigmoid(jnp.dot(h, wg_ref[:, i * D_MODEL:(i + 1) * D_MODEL], preferred_element_type=F32))
        term = gate * jnp.dot(y_ref[...], wb_ref[i], preferred_element_type=F32)
        merged = term if merged is None else merged + term
    out = jnp.dot(merged.astype(BF16), wo_ref[...], preferred_element_type=F32)
    o_ref[...] = x + mod_ref[0, 2:3, :] * out


def _merge_call(x_all, mod, g, w_gate, ys, w_branch, w_out, n_rows, tm, tiles_per_batch, n_batch):
    row = lambda w: pl.BlockSpec((tm, w), lambda i: (i, 0))
    full = lambda a: pl.BlockSpec(a.shape, lambda i, nd=a.ndim: (0,) * nd)
    return pl.pallas_call(
        _merge_kernel,
        grid=(n_rows // tm,),
        in_specs=[row(D_MODEL), pl.BlockSpec((1, 6, D_MODEL), _seg_map(tiles_per_batch, n_batch)), full(g),
                  full(w_gate)] + [row(MIX_W)] * 4 + [full(w_branch), full(w_out)],
        out_specs=row(D_MODEL),
        out_shape=jax.ShapeDtypeStruct((n_rows, D_MODEL), F32),
        compiler_params=_cparams("parallel"),
        name="merge_out",
    )(x_all, mod, g, w_gate, *ys, w_branch, w_out)


def _router_kernel(x_ref, mod_ref, g_ref, wh_ref, wl_ref, bias_ref, f_ref, comb_ref):
    f = _modulate(x_ref[...], g_ref[...], mod_ref[0, 3:4, :], mod_ref[0, 4:5, :])
    fh = f.astype(BF16)
    f_ref[...] = fh
    fl = (f - fh.astype(F32)).astype(BF16)
    nt = (((1,), (1,)), ((), ()))
    wh, wl = wh_ref[...], wl_ref[...]
    logits = (lax.dot_general(wh, fh, nt, preferred_element_type=F32)
              + lax.dot_general(wh, fl, nt, preferred_element_type=F32)
              + lax.dot_general(wl, fh, nt, preferred_element_type=F32))
    scores = _sigmoid(logits)
    biased = scores + bias_ref[...]
    sc = [scores[e:e + 1, :] for e in range(N_EXPERTS)]
    bi = [biased[e:e + 1, :] for e in range(N_EXPERTS)]
    group_score = []
    for g in range(N_GROUPS):
        a, b, c, d = bi[4 * g:4 * g + 4]
        m1, n1, m2, n2 = jnp.maximum(a, b), jnp.minimum(a, b), jnp.maximum(c, d), jnp.minimum(c, d)
        group_score.append(jnp.maximum(m1, m2) + jnp.maximum(jnp.minimum(m1, m2), jnp.maximum(n1, n2)))
    def first_argmax(vals):
        top = functools.reduce(jnp.maximum, vals)
        seen, hot = None, []
        for v in vals:
            h = v == top
            if seen is not None:
                h = jnp.logical_and(h, jnp.logical_not(seen))
            seen = h if seen is None else jnp.logical_or(seen, h)
            hot.append(h)
        return hot

    in_group = first_argmax(group_score)
    masked = [jnp.where(in_group[e // EXPERTS_PER_GROUP], bi[e], -jnp.inf) for e in range(N_EXPERTS)]
    hot1 = first_argmax(masked)
    hot2 = first_argmax([jnp.where(h, -jnp.inf, v) for h, v in zip(hot1, masked)])
    w1 = functools.reduce(jnp.add, [jnp.where(h, s, 0.0) for h, s in zip(hot1, sc)])
    w2 = functools.reduce(jnp.add, [jnp.where(h, s, 0.0) for h, s in zip(hot2, sc)])
    inv_tot = 1.0 / (w1 + w2)
    for e in range(N_EXPERTS):
        comb_ref[e:e + 1, :] = (jnp.where(hot1[e], w1, 0.0) + jnp.where(hot2[e], w2, 0.0)) * inv_tot


def _router_call(x, mod, g, wh, wl, bias, tm, tiles_per_batch, n_batch):
    t = x.shape[0]
    full = lambda a: pl.BlockSpec(a.shape, lambda i, nd=a.ndim: (0,) * nd)
    return pl.pallas_call(
        _router_kernel,
        grid=(t // tm,),
        in_specs=[pl.BlockSpec((tm, D_MODEL), lambda i: (i, 0)),
                  pl.BlockSpec((1, 6, D_MODEL), _seg_map(tiles_per_batch, n_batch)), full(g), full(wh), full(wl), full(bias)],
        out_specs=[pl.BlockSpec((tm, D_MODEL), lambda i: (i, 0)), pl.BlockSpec((N_EXPERTS, tm), lambda i: (0, i))],
        out_shape=[jax.ShapeDtypeStruct((t, D_MODEL), BF16), jax.ShapeDtypeStruct((N_EXPERTS, t), F32)],
        compiler_params=_cparams("parallel"),
        name="moe_router",
    )(x, mod, g, wh, wl, bias)


def _moe_kernel(f_ref, comb_ref, wg_ref, wu_ref, wd_ref, x_ref, mod_ref, o_ref, acc_ref):
    e = pl.program_id(1)

    @pl.when(e == 0)
    def _():
        acc_ref[...] = jnp.zeros_like(acc_ref)

    f = f_ref[...]
    gate = jnp.dot(f, wg_ref[0], preferred_element_type=F32)
    up = jnp.dot(f, wu_ref[0], preferred_element_type=F32)
    act = (gate * _sigmoid(gate) * up).astype(BF16)
    down = jnp.dot(act, wd_ref[0], preferred_element_type=F32)
    comb = comb_ref[...]
    lane = lax.broadcasted_iota(jnp.int32, comb.shape, 1)
    c_e = jnp.sum(jnp.where(lane == e, comb, 0.0), axis=1, keepdims=True)
    acc_ref[...] += c_e * down

    @pl.when(e == N_EXPERTS - 1)
    def _():
        o_ref[...] = x_ref[...] + mod_ref[0, 5:6, :] * acc_ref[...]


def _moe_call(f, comb, wg, wu, wd, x, mod, tm, tiles_per_batch, n_batch):
    t = f.shape[0]
    seg = _seg_map(tiles_per_batch, n_batch)
    return pl.pallas_call(
        _moe_kernel,
        grid=(t // tm, N_EXPERTS),
        in_specs=[pl.BlockSpec((tm, D_MODEL), lambda i, e: (i, 0)),
                  pl.BlockSpec((tm, N_EXPERTS), lambda i, e: (i, 0)),
                  pl.BlockSpec((1, D_MODEL, D_FF), lambda i, e: (e, 0, 0)),
                  pl.BlockSpec((1, D_MODEL, D_FF), lambda i, e: (e, 0, 0)),
                  pl.BlockSpec((1, D_FF, D_MODEL), lambda i, e: (e, 0, 0)),
                  pl.BlockSpec((tm, D_MODEL), lambda i, e: (i, 0)),
                  pl.BlockSpec((1, 6, D_MODEL), lambda i, e: seg(i))],
        out_specs=pl.BlockSpec((tm, D_MODEL), lambda i, e: (i, 0)),
        out_shape=jax.ShapeDtypeStruct((t, D_MODEL), F32),
        scratch_shapes=[pltpu.VMEM((tm, D_MODEL), F32)],
        compiler_params=_cparams("parallel", "arbitrary"),
        name="moe_experts",
    )(f, comb, wg, wu, wd, x, mod)


def _block_ones(n, group):
    i = np.arange(n) // group
    return jnp.asarray(i[:, None] == i[None, :], dtype=BF16)


def _rope_tables(n_lat, n_pad):
    rows = n_lat // GRID_W
    row = jnp.repeat(jnp.arange(rows, dtype=F32), GRID_W)
    col = jnp.tile(jnp.arange(GRID_W, dtype=F32), rows)

    def angles(rot_dim):
        n_freq = rot_dim // 4
        inv_freq = ROPE_BASE ** (-jnp.arange(n_freq, dtype=F32) / n_freq)
        ang = jnp.concatenate([row[:, None] * inv_freq, col[:, None] * inv_freq], axis=-1)
        return jnp.cos(ang), jnp.sin(ang)

    c, s = angles(DA_DIM)
    cda = jnp.tile(jnp.concatenate([c, c], -1), (1, 2 * DA_HEADS))
    sda = jnp.tile(jnp.concatenate([-s, s], -1), (1, 2 * DA_HEADS))
    c, s = angles(MLA_ROPE)
    one = jnp.ones((n_lat, MLA_NOPE), F32)
    pad = MLA_HEAD_PAD - MLA_NOPE - MLA_ROPE
    cml = jnp.tile(jnp.concatenate([one, c, c, jnp.ones((n_lat, pad), F32)], -1), (1, MLA_HEADS))
    sml = jnp.tile(jnp.concatenate([0 * one, -s, s, jnp.zeros((n_lat, pad), F32)], -1), (1, MLA_HEADS))
    ident = lambda t, v: jnp.concatenate([t, jnp.full((n_pad, MIX_W), v, F32)], axis=0)
    return ident(cda, 1.0), ident(sda, 0.0), ident(cml, 1.0), ident(sml, 0.0)


def _pad_heads(w, n_heads, src_w, lo, hi, dst_w=MLA_HEAD_PAD):
    w = w.reshape(w.shape[0], n_heads, src_w)[:, :, lo:hi]
    w = jnp.pad(w, ((0, 0), (0, 0), (0, dst_w - (hi - lo))))
    return w.reshape(w.shape[0], n_heads * dst_w)


def _mix_weight(w_in_l):
    w = w_in_l
    kr = w[:, 1344:1360]
    z = lambda n: jnp.zeros((D_MODEL, n), w.dtype)
    kr_wide = jnp.concatenate([jnp.concatenate([z(MLA_NOPE), kr, z(MLA_HEAD_PAD - MLA_NOPE - MLA_ROPE)], 1)] * MLA_HEADS, 1)
    return jnp.concatenate([w[:, 0:1024], w[:, 1024:1216], z(64), w[:, 1216:1344], kr_wide, w[:, 1360:2384]], axis=1).astype(BF16)


def _pick_tile(n, pref):
    t = min(pref, n)
    assert n % t == 0, (n, t)
    return t


def kernel(x, c, ctx, c_ctx, w_ada, b_ada, norm_mix_g, norm_ffn_g, w_in, da_qk_norm_g, da_lambda, da_subln_g, s5_lam_re, s5_lam_im, s5_log_dt, s5_b_re, s5_b_im, s5_c_re, s5_c_im, s5_d, s5_w_glu, s5_b_glu, mla_cq_norm_g, mla_ckv_norm_g, mla_w_uq, mla_w_ukv, mla_qk_norm_g, rw_mu, rw_w0, rw_w1, rw_w2, rw_a0, rw_a1, rw_a2, rw_g1, rw_g2, rw_k_k, rw_k_a, rw_r_k, rw_ln_g, rw_ln_b, w_branch, w_out, router_w, router_bias, exp_w_gate, exp_w_up, exp_w_down):
    b, n_lat, dm = x.shape
    n_ctx = ctx.shape[1]
    depth = w_ada.shape[0]
    assert dm == D_MODEL
    t_lat, t_ctx = b * n_lat, b * n_ctx
    t_all = t_lat + t_ctx
    tm = _pick_tile(math.gcd(n_lat, t_ctx), 512)
    tr = _pick_tile(math.gcd(n_lat, n_ctx), 256)
    tm_moe = _pick_tile(math.gcd(n_lat, t_ctx), 1024)
    tpb, tpb_moe = n_lat // tm, n_lat // tm_moe

    g32 = _block_ones(MIX_W, DA_DIM)
    g64 = _block_ones(MIX_W, RW_DIM)
    eye4 = jnp.tile(jnp.eye(RW_DIM, dtype=F32), (1, RW_HEADS))
    tabs = _rope_tables(n_lat, tm)
    row = lambda v: v.reshape(1, -1).astype(F32)

    cc = jnp.zeros((16, dm), F32).at[:b].set(c).at[b].set(c_ctx)
    mod_all = _ada_call(cc, w_ada, b_ada)
    x_all = jnp.concatenate([x.reshape(t_lat, dm), ctx.reshape(t_ctx, dm)], axis=0)

    rw_halo_tiles = t_all // tr
    tile_id = np.arange(rw_halo_tiles)
    lat_tiles, tps = t_lat // tr, n_lat // tr
    seq_start = np.where(tile_id < lat_tiles, tile_id % tps == 0, True)[:, None]
    seq_end = np.where(tile_id < lat_tiles, tile_id % tps == tps - 1, True)[:, None]

    wr_hi = router_w.T.astype(BF16)
    wr_lo = (router_w.T - wr_hi.astype(F32)).astype(BF16)
    r_bias = router_bias.reshape(N_EXPERTS, 1).astype(F32)

    for l in range(depth):
        need_ctx = l < depth - 1
        lambda_init = 0.8 - 0.6 * math.exp(-0.3 * l)
        mod = mod_all[l, :b + 1].reshape(b + 1, 6, dm)
        da, s5u, mla, rw = _inproj_call(x_all, mod, row(norm_mix_g[l]), _mix_weight(w_in[l]), tm, tpb, b)

        gda = jnp.stack([jnp.tile(da_qk_norm_g[l, 0], 2 * DA_HEADS) * DA_DIM ** -0.5, jnp.tile(da_qk_norm_g[l, 1], 2 * DA_HEADS)])
        mla_pad = MLA_HEAD_PAD - MLA_NOPE - MLA_ROPE
        gml = jnp.stack([jnp.tile(jnp.pad(mla_qk_norm_g[l, 0], (0, mla_pad)), MLA_HEADS) * (MLA_NOPE + MLA_ROPE) ** -0.5,
                         jnp.tile(jnp.pad(mla_qk_norm_g[l, 1], (0, mla_pad)), MLA_HEADS)])
        wuq = jnp.pad(_pad_heads(mla_w_uq[l], MLA_HEADS, MLA_NOPE + MLA_ROPE, 0, MLA_NOPE + MLA_ROPE), ((0, 64), (0, 0))).astype(BF16)
        wuk = _pad_heads(mla_w_ukv[l], MLA_HEADS, MLA_NOPE + MLA_VDIM, 0, MLA_NOPE).astype(BF16)
        wuv = _pad_heads(mla_w_ukv[l], MLA_HEADS, MLA_NOPE + MLA_VDIM, MLA_NOPE, MLA_NOPE + MLA_VDIM).astype(BF16)
        consts = (g32, g64, gda.astype(F32), gml.astype(F32), row(jnp.pad(mla_cq_norm_g[l], (0, 64))), row(mla_ckv_norm_g[l]),
                  wuq, wuk, wuv)
        qd, kd, vd, qm, km, vm = _qkprep_call(da, mla, tabs, consts, tm, n_lat // tm, t_lat // tm)

        lam32 = da_lambda[l].astype(F32)
        lmbda = (jnp.exp(jnp.sum(lam32[0] * lam32[1])) - jnp.exp(jnp.sum(lam32[2] * lam32[3])) + lambda_init).reshape(1, 1)
        da_kernel = functools.partial(_diff_attn_kernel, post_scale=1.0 - lambda_init)
        da_extra = (lmbda, row(da_subln_g[l]))
        lat = lambda a: a[:t_lat].reshape(b, n_lat, -1)
        cx = lambda a: a[t_lat:].reshape(b, n_ctx, -1)
        kv_all = lambda a: jnp.concatenate([lat(a), cx(a)], axis=1)
        ya = [_attention(lat(qd), kv_all(kd), kv_all(vd), 2 * DA_HEADS, DA_DIM, da_extra, da_kernel, "diff_attn_lat").reshape(t_lat, MIX_W)]
        yc = [_attention(lat(qm), kv_all(km), kv_all(vm), MLA_HEADS, MLA_HEAD_PAD, (), _mla_attn_kernel, "mla_attn_lat").reshape(t_lat, MIX_W)]
        if need_ctx:
            ya.append(_attention(cx(qd), cx(kd), cx(vd), 2 * DA_HEADS, DA_DIM, da_extra, da_kernel, "diff_attn_ctx").reshape(t_ctx, MIX_W))
            yc.append(_attention(cx(qm), cx(km), cx(vm), MLA_HEADS, MLA_HEAD_PAD, (), _mla_attn_kernel, "mla_attn_ctx").reshape(t_ctx, MIX_W))
        ya, yc = jnp.concatenate(ya, axis=0), jnp.concatenate(yc, axis=0)

        mats = _s5_mats(s5_lam_re[l], s5_lam_im[l], s5_log_dt[l], s5_b_re[l], s5_b_im[l], s5_c_re[l], s5_c_im[l])
        ys_lat, ys_ctx = _s5_scan(lat(s5u), cx(s5u), mats)
        ys = jnp.concatenate([ys_lat.reshape(t_lat, MIX_W), ys_ctx.reshape(t_ctx, MIX_W)], axis=0)
        yb = _s5_glu_call(s5u, ys, row(s5_d[l]), s5_w_glu[l].astype(BF16), row(s5_b_glu[l]), tm)

        prev = jnp.concatenate([jnp.zeros((1, _RW_W), F32), rw[tr - 1::tr][:-1]], axis=0)
        nxt = jnp.concatenate([rw[0::tr][1:], jnp.zeros((1, _RW_W), F32)], axis=0)
        prev = jnp.where(seq_start, 0.0, prev).reshape(rw_halo_tiles, 1, _RW_W)
        nxt = jnp.where(seq_end, 0.0, nxt).reshape(rw_halo_tiles, 1, _RW_W)
        bf = lambda a: a.astype(BF16)
        pre_consts = (row(rw_mu[l]), g64, row(rw_k_k[l]), row(rw_k_a[l]), row(rw_r_k[l]),
                      rw_w0[l].reshape(2, 1, MIX_W), bf(rw_w1[l]), bf(rw_w2[l]),
                      rw_a0[l].reshape(2, 1, MIX_W), bf(rw_a1[l]), bf(rw_a2[l]), bf(rw_g1[l]), bf(rw_g2[l]))
        r_, v_, kk_, wd_, kka_, km_, bon, gate = _rw_pre_call(rw, prev, nxt, pre_consts, tr)
        shared = lambda a: _seq_orders(lat(a), cx(a))
        per_dir = lambda a: jnp.stack([_seq_orders(lat(a[0]), cx(a[0]))[0], _seq_orders(lat(a[1]), cx(a[1]))[1]])
        seqs = (shared(r_), shared(v_), shared(kk_), per_dir(wd_), per_dir(kka_), per_dir(km_))
        tc = _pick_tile(n_lat + n_ctx, RW_TCHUNK)
        yr_lat, yr_ctx = _unorder_sum(_rw_scan_call(seqs, g64, eye4, tc), n_ctx)
        yr = jnp.concatenate([yr_lat.reshape(t_lat, MIX_W), yr_ctx.reshape(t_ctx, MIX_W)], axis=0)
        yd = _rw_fin_call(yr, bon, gate, g64, row(rw_ln_g[l]), row(rw_ln_b[l]), tm)

        n_rows = t_all if need_ctx else t_lat
        x_mid = _merge_call(x_all, mod, row(norm_mix_g[l]), w_in[l][:, 2384:].astype(BF16), (ya, yb, yc, yd),
                            w_branch[l].astype(BF16), w_out[l].astype(BF16), n_rows, tm, tpb, b)
        f, comb_t = _router_call(x_mid, mod, row(norm_ffn_g[l]), wr_hi, wr_lo, r_bias, tm, tpb, b)
        x_all = _moe_call(f, comb_t.T, exp_w_gate[l].astype(BF16), exp_w_up[l].astype(BF16), exp_w_down[l].astype(BF16),
                          x_mid, mod, tm_moe, tpb_moe, b)
    return x_all[:t_lat].reshape(b, n_lat, dm)
```

```python
import functools
import math

import numpy as np
import jax
import jax.numpy as jnp
from jax import lax
from jax.experimental import pallas as pl
from jax.experimental.pallas import tpu as pltpu

F32 = jnp.float32
BF16 = jnp.bfloat16

D_MODEL = 1024
GRID_W = 64
ROPE_BASE = 10000.0
EPS = 1e-6
DA_HEADS, DA_DIM, DA_VDIM = 4, 32, 64
S5_GROUPS, S5_CH, S5_STATE = 16, 16, 64
MLA_HEADS, MLA_NOPE, MLA_ROPE, MLA_VDIM = 4, 32, 16, 64
MLA_Q_RANK, MLA_KV_RANK = 192, 128
MLA_HEAD_PAD = 64
RW_HEADS, RW_DIM = 4, 64
RW_LN_EPS = 64e-5
N_BRANCH = 4
N_EXPERTS, N_GROUPS, EXPERTS_PER_GROUP = 16, 4, 4
D_FF = 512
MIX_W = 256

S5_CHUNK = 8
S5_FLAT = S5_CHUNK * MIX_W
S5_STATE_W = S5_GROUPS * S5_STATE
RW_TCHUNK = 128
TOKEN_TILE = 256
MOE_TILE = 1152

_DA_W, _S5_W, _MLA_W, _RW_W = 768, 256, 640, 1024
_MIX_COLS = _DA_W + _S5_W + _MLA_W + _RW_W

V7X_VMEM_BYTES = 64 * 2**20
_VMEM_LIMIT = V7X_VMEM_BYTES - 8 * 2**20


def _cparams(*sem):
    return pltpu.CompilerParams(dimension_semantics=sem, vmem_limit_bytes=_VMEM_LIMIT)


def _full(a):
    return pl.BlockSpec(a.shape, lambda *_, nd=a.ndim: (0,) * nd)


def _split_dot(x, w, terms=2):
    acc = None
    rem = x
    for i in range(terms):
        part = rem.astype(BF16)
        d = jnp.dot(part, w, preferred_element_type=F32)
        acc = d if acc is None else acc + d
        if i + 1 < terms:
            rem = rem - part.astype(F32)
    return acc


def _split_dot_k(x, w2):
    hi = x.astype(BF16)
    lo = (x - hi.astype(F32)).astype(BF16)
    return jnp.dot(jnp.concatenate([hi, lo], axis=1), w2, preferred_element_type=F32)


def _modulate(x, g, shift, scale):
    xn = x * lax.rsqrt(jnp.mean(x * x, axis=-1, keepdims=True) + EPS)
    return xn * g * (1.0 + scale) + shift


def _sigmoid(x):
    return 1.0 / (1.0 + jnp.exp(-x))


def _group_rms(x, ones_bd, inv_n, gain):
    ms = _split_dot(x * x, ones_bd) * inv_n
    return x * lax.rsqrt(ms + EPS) * gain


def _lane_partner(x, half, period, first_end):
    n = x.shape[1]
    lane = lax.broadcasted_iota(jnp.int32, x.shape, 1)
    up = pltpu.roll(x, n - half, axis=1)
    down = pltpu.roll(x, half, axis=1)
    return jnp.where((lane & (period - 1)) < first_end, up, down)


def _rope(x, cos_t, sin_t, half, period, first_end):
    return x * cos_t + _lane_partner(x, half, period, first_end) * sin_t


def _ada_kernel(c_ref, w_ref, b_ref, o_ref):
    c = c_ref[...]
    s = c * _sigmoid(c)
    o_ref[0] = jnp.dot(s.astype(BF16), w_ref[0].astype(BF16), preferred_element_type=F32) + b_ref[0]


def _ada_call(cc, w_ada, b_ada):
    depth, dm, n = w_ada.shape
    tn = n // 4
    return pl.pallas_call(
        _ada_kernel,
        grid=(depth, n // tn),
        in_specs=[
            pl.BlockSpec(cc.shape, lambda l, j: (0, 0)),
            pl.BlockSpec((1, dm, tn), lambda l, j: (l, 0, j)),
            pl.BlockSpec((1, 1, tn), lambda l, j: (l, 0, j)),
        ],
        out_specs=pl.BlockSpec((1, cc.shape[0], tn), lambda l, j: (l, 0, j)),
        out_shape=jax.ShapeDtypeStruct((depth, cc.shape[0], n), F32),
        compiler_params=_cparams("parallel", "parallel"),
        name="ada_mod",
    )(cc, w_ada, b_ada.reshape(depth, 1, n))


class _Layout:
    def __init__(self, n_batch, n_ctx, n_lat):
        t = TOKEN_TILE
        assert n_ctx % t == 0 and n_lat % t == 0
        self.b, self.n_ctx, self.n_lat, self.n_tot = n_batch, n_ctx, n_lat, n_ctx + n_lat
        self.t = t
        self.ctx_tiles, self.lat_tiles, self.seq_tiles = n_ctx // t, n_lat // t, (n_ctx + n_lat) // t

    def rows(self, with_ctx):
        return self.b * (self.n_tot if with_ctx else self.n_lat)

    def n_tiles(self, with_ctx):
        return self.b * (self.seq_tiles if with_ctx else self.lat_tiles)

    def src_tile(self, with_ctx):
        if with_ctx:
            return lambda i: i
        return lambda i: (i // self.lat_tiles) * self.seq_tiles + i % self.lat_tiles + self.ctx_tiles

    def mod_row(self, with_ctx):
        if with_ctx:
            return lambda i: jnp.where(i % self.seq_tiles < self.ctx_tiles, self.b, i // self.seq_tiles)
        return lambda i: i // self.lat_tiles


def _inproj_kernel(x_ref, mod_ref, g_ref, w_ref, da_ref, s5_ref, mla_ref, rw_ref):
    h = _modulate(x_ref[...], g_ref[...], mod_ref[0, 0:1, :], mod_ref[0, 1:2, :])
    acc = jnp.dot(h.astype(BF16), w_ref[...], preferred_element_type=F32)
    da_ref[...] = acc[:, 0:_DA_W]
    s5_ref[...] = acc[:, _DA_W:_DA_W + _S5_W]
    mla_ref[...] = acc[:, _DA_W + _S5_W:_DA_W + _S5_W + _MLA_W]
    rw_ref[...] = acc[:, _DA_W + _S5_W + _MLA_W:_MIX_COLS]


def _inproj_call(lay, x_all, mod, g, w_mix):
    t, tm = x_all.shape[0], lay.t
    widths = (_DA_W, _S5_W, _MLA_W, _RW_W)
    mrow = lay.mod_row(True)
    return pl.pallas_call(
        _inproj_kernel,
        grid=(t // tm,),
        in_specs=[
            pl.BlockSpec((tm, D_MODEL), lambda i: (i, 0)),
            pl.BlockSpec((1, 6, D_MODEL), lambda i: (mrow(i), 0, 0)),
            _full(g), _full(w_mix),
        ],
        out_specs=[pl.BlockSpec((tm, w), lambda i: (i, 0)) for w in widths],
        out_shape=[jax.ShapeDtypeStruct((t, w), F32) for w in widths],
        compiler_params=_cparams("parallel"),
        name="in_proj",
    )(x_all, mod, g, w_mix)


def _qkprep_kernel(da_ref, mla_ref, cda_ref, sda_ref, cml_ref, sml_ref, g32_ref, g64_ref,
                   gda_ref, gml_ref, cqg_ref, ckvg_ref, wuq_ref, wuk_ref, wuv_ref,
                   qd_ref, kd_ref, vd_ref, qm_ref, km_ref, vm_ref):
    g32 = g32_ref[...]
    g64 = g64_ref[...]
    cda, sda = cda_ref[...], sda_ref[...]
    q = _group_rms(da_ref[:, 0:MIX_W], g32, 1.0 / DA_DIM, gda_ref[0:1, :])
    qd_ref[...] = _rope(q, cda, sda, DA_DIM // 2, DA_DIM, DA_DIM // 2).astype(BF16)
    k = _group_rms(da_ref[:, MIX_W:2 * MIX_W], g32, 1.0 / DA_DIM, gda_ref[1:2, :])
    kd_ref[0] = _rope(k, cda, sda, DA_DIM // 2, DA_DIM, DA_DIM // 2).T.astype(BF16)
    vd_ref[...] = da_ref[:, 2 * MIX_W:3 * MIX_W].astype(BF16)

    cml, sml = cml_ref[...], sml_ref[...]
    cq = mla_ref[:, 0:256]
    cqn = cq * lax.rsqrt(jnp.sum(cq * cq, axis=-1, keepdims=True) * (1.0 / MLA_Q_RANK) + EPS) * cqg_ref[...]
    q = jnp.dot(cqn.astype(BF16), wuq_ref[...], preferred_element_type=F32)
    ckv = mla_ref[:, 256:384]
    ckvn = ckv * lax.rsqrt(jnp.mean(ckv * ckv, axis=-1, keepdims=True) + EPS) * ckvg_ref[...]
    ckvb = ckvn.astype(BF16)
    k = jnp.dot(ckvb, wuk_ref[...], preferred_element_type=F32) + mla_ref[:, 384:640]
    vm_ref[...] = jnp.dot(ckvb, wuv_ref[...], preferred_element_type=F32).astype(BF16)
    inv_n = 1.0 / (MLA_NOPE + MLA_ROPE)
    half = MLA_ROPE // 2
    q = _group_rms(q, g64, inv_n, gml_ref[0:1, :])
    qm_ref[...] = _rope(q, cml, sml, half, MLA_HEAD_PAD, MLA_NOPE + half).astype(BF16)
    k = _group_rms(k, g64, inv_n, gml_ref[1:2, :])
    km_ref[0] = _rope(k, cml, sml, half, MLA_HEAD_PAD, MLA_NOPE + half).T.astype(BF16)


def _qkprep_call(lay, da, mla, tabs, consts):
    t, tm = da.shape[0], lay.t
    st = lay.seq_tiles
    row = pl.BlockSpec((tm, MIX_W), lambda i: (i, 0))
    key_t = pl.BlockSpec((1, MIX_W, tm), lambda i: (i // st, 0, i % st))
    in_specs = [pl.BlockSpec((tm, _DA_W), lambda i: (i, 0)), pl.BlockSpec((tm, _MLA_W), lambda i: (i, 0))]
    in_specs += [pl.BlockSpec((tm, MIX_W), lambda i: (i % st, 0)) for _ in tabs]
    in_specs += [_full(a) for a in consts]
    tok = jax.ShapeDtypeStruct((t, MIX_W), BF16)
    keys = jax.ShapeDtypeStruct((lay.b, MIX_W, lay.n_tot), BF16)
    return pl.pallas_call(
        _qkprep_kernel,
        grid=(t // tm,),
        in_specs=in_specs,
        out_specs=[row, key_t, row, row, key_t, row],
        out_shape=[tok, keys, tok, tok, keys, tok],
        compiler_params=_cparams("parallel"),
        name="qk_prep",
    )(da, mla, *tabs, *consts)


def _softmax_parts(q, kt):
    s = jnp.dot(q, kt, preferred_element_type=F32)
    p = jnp.exp(s - jnp.max(s, axis=-1, keepdims=True))
    return p, 1.0 / jnp.sum(p, axis=-1, keepdims=True)


def _attn_heads(q, kt_ref, v_ref, nk, diff, lam):
    lane = lax.broadcasted_iota(jnp.int32, (q.shape[0], MIX_W), 1)
    v = v_ref[0, 0:nk, :]
    acc = jnp.zeros((q.shape[0], MIX_W), F32)
    for h in range(DA_HEADS):
        if diff:
            e0, e1 = 2 * h * DA_DIM, (2 * h + 1) * DA_DIM
            p0, r0 = _softmax_parts(q[:, e0:e0 + DA_DIM], kt_ref[0, e0:e0 + DA_DIM, 0:nk])
            p1, r1 = _softmax_parts(q[:, e1:e1 + DA_DIM], kt_ref[0, e1:e1 + DA_DIM, 0:nk])
            o = jnp.dot((p0 * r0 - p1 * (r1 * lam)).astype(BF16), v, preferred_element_type=F32)
        else:
            e0 = h * MLA_HEAD_PAD
            p, r = _softmax_parts(q[:, e0:e0 + MLA_HEAD_PAD], kt_ref[0, e0:e0 + MLA_HEAD_PAD, 0:nk])
            o = jnp.dot(p.astype(BF16), v, preferred_element_type=F32) * r
        in_head = jnp.logical_and(lane >= h * DA_VDIM, lane < (h + 1) * DA_VDIM)
        acc = jnp.where(in_head, o, acc)
    return acc


def _attn_kernel(q_ref, kt_ref, v_ref, lam_ref, gain_ref, g64_ref, o_ref, *, diff, n_ctx, n_tot, ctx_tiles):
    q = q_ref[...]
    lam = lam_ref[...]

    def run(nk):
        o = _attn_heads(q, kt_ref, v_ref, nk, diff, lam)
        if diff:
            o = _group_rms(o, g64_ref[...], 1.0 / DA_VDIM, gain_ref[...])
        o_ref[...] = o.astype(BF16)

    if ctx_tiles:
        is_ctx = pl.program_id(1) < ctx_tiles
        pl.when(is_ctx)(lambda: run(n_ctx))
        pl.when(jnp.logical_not(is_ctx))(lambda: run(n_tot))
    else:
        run(n_tot)


def _attention(lay, q, kt, v, extra, diff, with_ctx, name):
    tq = lay.t
    tiles = lay.seq_tiles if with_ctx else lay.lat_tiles
    off = 0 if with_ctx else lay.ctx_tiles
    v3 = v.reshape(lay.b, lay.n_tot, MIX_W)
    kern = functools.partial(_attn_kernel, diff=diff, n_ctx=lay.n_ctx, n_tot=lay.n_tot,
                             ctx_tiles=lay.ctx_tiles if with_ctx else 0)
    return pl.pallas_call(
        kern,
        grid=(lay.b, tiles),
        in_specs=[
            pl.BlockSpec((tq, MIX_W), lambda b, j: (b * lay.seq_tiles + j + off, 0)),
            pl.BlockSpec((1, MIX_W, lay.n_tot), lambda b, j: (b, 0, 0)),
            pl.BlockSpec((1, lay.n_tot, MIX_W), lambda b, j: (b, 0, 0)),
        ] + [_full(a) for a in extra],
        out_specs=pl.BlockSpec((tq, MIX_W), lambda b, j: (b * tiles + j, 0)),
        out_shape=jax.ShapeDtypeStruct((lay.rows(with_ctx), MIX_W), BF16),
        compiler_params=_cparams("parallel", "parallel"),
        name=name,
    )(q, kt, v3, *extra)


def _s5_proj_kernel(u_ref, bre_ref, bim_ref, sre_ref, sim_ref):
    u = u_ref[...].astype(BF16)
    sre_ref[0] = jnp.dot(u, bre_ref[0], preferred_element_type=F32)
    sim_ref[0] = jnp.dot(u, bim_ref[0], preferred_element_type=F32)


def _s5_rec_kernel(sre_ref, sim_ref, are_ref, aim_ref, hre_ref, him_ref, *, n_batch, n_chunks, ctx_chunks):
    rev = pl.program_id(0) == 1
    ar, ai = are_ref[0], aim_ref[0]
    sre, sim, hre, him = sre_ref.at[0], sim_ref.at[0], hre_ref.at[0], him_ref.at[0]

    def step(i, carry):
        hr, hi = carry
        k_rev = jnp.where(i < ctx_chunks, ctx_chunks - 1 - i, n_chunks - 1 + ctx_chunks - i)
        k = jnp.where(rev, k_rev, i)
        rows = pl.ds(k, n_batch, stride=n_chunks)
        hre[rows, :] = hr
        him[rows, :] = hi
        return ar * hr - ai * hi + sre[rows, :], ar * hi + ai * hr + sim[rows, :]

    zero = jnp.zeros((n_batch, 128), F32)
    lax.fori_loop(0, n_chunks, step, (zero, zero))


def _s5_out_kernel(u_ref, hre_ref, him_ref, m_ref, cre_ref, cim_ref, y_ref):
    y = jnp.dot(u_ref[...].astype(BF16), m_ref[0], preferred_element_type=F32)
    y_ref[0] = y + _split_dot(hre_ref[0], cre_ref[0]) + _split_dot(him_ref[0], cim_ref[0])


def _s5_mats(lam_re, lam_im, log_dt, b_re, b_im, c_re, c_im):
    hp = lax.Precision.HIGHEST
    L, G, P, CH = S5_CHUNK, S5_GROUPS, S5_STATE, S5_CH
    lr, li = lam_re.astype(F32), lam_im.astype(F32)
    dt = jnp.exp(log_dt.astype(F32))[..., None]
    zr, zi = lr * dt, li * dt
    j = jnp.arange(L + 1, dtype=F32)[:, None, None, None]
    mag = jnp.exp(zr[None] * j)
    pw_re, pw_im = mag * jnp.cos(zi[None] * j), mag * jnp.sin(zi[None] * j)
    nr, ni = pw_re[1] - 1.0, pw_im[1]
    den = lr * lr + li * li
    cr, ci = (nr * lr + ni * li) / den, (ni * lr - nr * li) / den
    bre, bim = b_re.astype(F32), b_im.astype(F32)
    bb_re = cr[..., None] * bre - ci[..., None] * bim
    bb_im = cr[..., None] * bim + ci[..., None] * bre
    x_re = pw_re[..., None] * bb_re[None] - pw_im[..., None] * bb_im[None]
    x_im = pw_re[..., None] * bb_im[None] + pw_im[..., None] * bb_re[None]
    cre, cim = c_re.astype(F32), c_im.astype(F32)
    kern = (jnp.einsum('dgcp,jdgpe->dgjce', cre, x_re[:L], precision=hp)
            - jnp.einsum('dgcp,jdgpe->dgjce', cim, x_im[:L], precision=hp))
    eye = jnp.eye(G, dtype=F32)
    s_idx, t_idx = jnp.arange(L)[:, None], jnp.arange(L)[None, :]
    fwd_pow, rev_pow = jnp.arange(L - 1, -1, -1), jnp.arange(L)
    m, b_r, b_i, c_r, c_i = [], [], [], [], []
    for d in range(2):
        lag = (t_idx - s_idx) if d == 0 else (s_idx - t_idx)
        kst = jnp.where((lag >= 0)[None, :, :, None, None], kern[d][:, jnp.clip(lag, 0, L - 1)], 0.0)
        kst = kst.transpose(1, 0, 4, 2, 3)
        m.append((kst[:, :, :, :, None, :] * eye[None, :, None, None, :, None]).reshape(L * G * CH, L * G * CH))
        pw = fwd_pow if d == 0 else rev_pow
        for src, dst in ((x_re, b_r), (x_im, b_i)):
            xs = src[pw, d].transpose(0, 1, 3, 2)
            dst.append((xs[:, :, :, None, :] * eye[None, :, None, :, None]).reshape(L * G * CH, G * P))
        q = jnp.arange(1, L + 1) if d == 0 else jnp.arange(L, 0, -1)
        ca_re = cre[d][None] * pw_re[q, d][:, :, None, :] - cim[d][None] * pw_im[q, d][:, :, None, :]
        ca_im = cre[d][None] * pw_im[q, d][:, :, None, :] + cim[d][None] * pw_re[q, d][:, :, None, :]
        for src, dst in ((ca_re, c_r), (-ca_im, c_i)):
            cs = src.transpose(1, 3, 0, 2)
            dst.append((cs[:, :, :, None, :] * eye[:, None, None, :, None]).reshape(G * P, L * G * CH))
    stack = lambda xs: jnp.stack(xs).astype(BF16)
    a_re, a_im = pw_re[L].reshape(2, 1, G * P), pw_im[L].reshape(2, 1, G * P)
    return stack(m), stack(b_r), stack(b_i), stack(c_r), stack(c_i), a_re, a_im


def _s5_scan(lay, u, mats):
    m, b_r, b_i, c_r, c_i, a_re, a_im = mats
    n_chunks = lay.n_tot // S5_CHUNK
    rows = lay.b * n_chunks
    tr = min(lay.t, rows)
    u2 = u.reshape(rows, S5_FLAT)
    wspec = lambda a: pl.BlockSpec((1,) + a.shape[1:], lambda d, i: (d, 0, 0))
    state = jax.ShapeDtypeStruct((2, rows, S5_STATE_W), F32)
    sblk = pl.BlockSpec((1, tr, S5_STATE_W), lambda d, i: (d, i, 0))
    ublk = pl.BlockSpec((tr, S5_FLAT), lambda d, i: (i, 0))
    s_re, s_im = pl.pallas_call(
        _s5_proj_kernel,
        grid=(2, rows // tr),
        in_specs=[ublk, wspec(b_r), wspec(b_i)],
        out_specs=[sblk, sblk],
        out_shape=[state, state],
        compiler_params=_cparams("parallel", "parallel"),
        name="s5_proj",
    )(u2, b_r, b_i)
    col = pl.BlockSpec((1, rows, 128), lambda d, j: (d, 0, j))
    acol = pl.BlockSpec((1, 1, 128), lambda d, j: (d, 0, j))
    h_re, h_im = pl.pallas_call(
        functools.partial(_s5_rec_kernel, n_batch=lay.b, n_chunks=n_chunks, ctx_chunks=lay.n_ctx // S5_CHUNK),
        grid=(2, S5_STATE_W // 128),
        in_specs=[col, col, acol, acol],
        out_specs=[col, col],
        out_shape=[state, state],
        compiler_params=_cparams("parallel", "parallel"),
        name="s5_rec",
    )(s_re, s_im, a_re, a_im)
    y = pl.pallas_call(
        _s5_out_kernel,
        grid=(2, rows // tr),
        in_specs=[ublk, sblk, sblk, wspec(m), wspec(c_r), wspec(c_i)],
        out_specs=pl.BlockSpec((1, tr, S5_FLAT), lambda d, i: (d, i, 0)),
        out_shape=jax.ShapeDtypeStruct((2, rows, S5_FLAT), F32),
        compiler_params=_cparams("parallel", "parallel"),
        name="s5_out",
    )(u2, h_re, h_im, m, c_r, c_i)
    return y.reshape(2, lay.b * lay.n_tot, MIX_W)


def _s5_glu_kernel(u_ref, y_ref, d_ref, w_ref, b_ref, o_ref):
    y = d_ref[...] * u_ref[...] + y_ref[0] + y_ref[1]
    z = 0.5 * y * (1.0 + jnp.tanh(math.sqrt(2.0 / math.pi) * (y + 0.044715 * (y * y * y))))
    gate = _sigmoid(jnp.dot(z.astype(BF16), w_ref[...], preferred_element_type=F32) + b_ref[...])
    o_ref[...] = (z * gate).astype(BF16)


def _s5_glu_call(u, y2, d, w, bias, tm):
    t = u.shape[0]
    row = pl.BlockSpec((tm, MIX_W), lambda i: (i, 0))
    return pl.pallas_call(
        _s5_glu_kernel,
        grid=(t // tm,),
        in_specs=[row, pl.BlockSpec((2, tm, MIX_W), lambda i: (0, i, 0)), _full(d), _full(w), _full(bias)],
        out_specs=row,
        out_shape=jax.ShapeDtypeStruct((t, MIX_W), BF16),
        compiler_params=_cparams("parallel"),
        name="s5_glu",
    )(u, y2, d, w, bias)


def _rw_pre_kernel(x_ref, prev_ref, next_ref, mu_ref, g64_ref, kk_g_ref, ka_ref, rk_ref,
                   w0_ref, w1_ref, w2_ref, a0_ref, a1_ref, a2_ref, g1_ref, g2_ref,
                   r_ref, va_ref, vb_ref, kk_ref, wd_ref, kka_ref, km_ref, bon_ref, gate_ref,
                   *, seq_tiles, ctx_tiles):
    x = x_ref[...]
    n = x.shape[0]
    j = pl.program_id(0) % seq_tiles
    starts = jnp.logical_or(j == 0, j == ctx_tiles)
    ends = jnp.logical_or(j == ctx_tiles - 1, j == seq_tiles - 1)
    prev_row = jnp.where(starts, 0.0, prev_ref[0, 7:8, :])
    next_row = jnp.where(ends, 0.0, next_ref[0, 0:1, :])
    row = lax.broadcasted_iota(jnp.int32, x.shape, 0)
    left = jnp.where(row == 0, prev_row, pltpu.roll(x, 1, axis=0))
    right = jnp.where(row == n - 1, next_row, pltpu.roll(x, n - 1, axis=0))
    x = x + (0.5 * (left + right) - x) * mu_ref[...]
    r, k, v, xd = (x[:, i * MIX_W:(i + 1) * MIX_W] for i in range(4))
    g64 = g64_ref[...]
    kscaled = k * kk_g_ref[...]
    kk = kscaled / jnp.maximum(jnp.sqrt(_split_dot(kscaled * kscaled, g64)), 1e-12)
    xdb = xd.astype(BF16)
    r_ref[...] = r
    kk_ref[...] = kk
    va = v.astype(BF16).astype(F32)
    va_ref[...] = va
    vb_ref[...] = _lane_partner((v - va).astype(BF16).astype(F32), RW_DIM // 2, RW_DIM, RW_DIM // 2)
    km_sum = None
    for d in range(2):
        lo = jnp.tanh(jnp.dot(xdb, w1_ref[d], preferred_element_type=F32))
        w_raw = w0_ref[d] + jnp.dot(lo.astype(BF16), w2_ref[d], preferred_element_type=F32)
        wd_ref[d] = jnp.exp(-_sigmoid(w_raw) * math.exp(-0.5))
        ar = jnp.dot(xdb, a1_ref[d], preferred_element_type=F32)
        a = _sigmoid(a0_ref[d] + jnp.dot(ar.astype(BF16), a2_ref[d], preferred_element_type=F32))
        km = k * (1.0 + (a - 1.0) * ka_ref[...])
        kka_ref[d] = kk * a
        km_ref[d] = km
        km_sum = km if km_sum is None else km_sum + km
    bon_ref[...] = _split_dot(r * km_sum * rk_ref[...], g64) * v
    gr = _sigmoid(jnp.dot(xdb, g1_ref[...], preferred_element_type=F32))
    gate_ref[...] = jnp.dot(gr.astype(BF16), g2_ref[...], preferred_element_type=F32)


def _rw_pre_call(lay, rw, consts):
    t, tr = rw.shape[0], lay.t
    nt = t // tr
    g8 = tr // 8
    rw8 = rw.reshape(t // 8, 8, _RW_W)
    row = pl.BlockSpec((tr, MIX_W), lambda i: (i, 0))
    row2 = pl.BlockSpec((2, tr, MIX_W), lambda i: (0, i, 0))
    sd = jax.ShapeDtypeStruct((t, MIX_W), F32)
    sd2 = jax.ShapeDtypeStruct((2, t, MIX_W), F32)
    return pl.pallas_call(
        functools.partial(_rw_pre_kernel, seq_tiles=lay.seq_tiles, ctx_tiles=lay.ctx_tiles),
        grid=(nt,),
        in_specs=[pl.BlockSpec((tr, _RW_W), lambda i: (i, 0)),
                  pl.BlockSpec((1, 8, _RW_W), lambda i: (jnp.maximum(i * g8 - 1, 0), 0, 0)),
                  pl.BlockSpec((1, 8, _RW_W), lambda i: (jnp.minimum((i + 1) * g8, t // 8 - 1), 0, 0))]
                 + [_full(a) for a in consts],
        out_specs=[row, row, row, row, row2, row2, row2, row, row],
        out_shape=[sd, sd, sd, sd, sd2, sd2, sd2, sd, sd],
        compiler_params=_cparams("parallel"),
        name="rwkv_pre",
    )(rw, rw8, rw8, *consts)


def _rw_scan_kernel(rf, vaf, vbf, kkf, rr, var, vbr, kkr, wf, wr, kaf, kar, kmf, kmr, g_ref, g2_ref, ea_ref, eb_ref,
                    yf_ref, yr_ref, s_scr, *, n_batch, n_steps):
    @pl.when(pl.program_id(0) == 0)
    def _():
        s_scr[...] = jnp.zeros_like(s_scr)

    g, g2 = g_ref[...], g2_ref[...]
    ea, eb = ea_ref[...], eb_ref[...]
    m = 2 * n_batch * RW_DIM

    def step(t, carry):
        tr = n_steps - 1 - t
        both = lambda a, b: jnp.concatenate([a[:, pl.ds(t, 1), :], b[:, pl.ds(tr, 1), :]], axis=0)
        both_d = lambda a, b: jnp.concatenate([a[0, :, pl.ds(t, 1), :], b[0, :, pl.ds(tr, 1), :]], axis=0)
        kk, r, va, vb = both(kkf, kkr), both(rf, rr), both(vaf, var), both(vbf, vbr)
        w, ka, km = both_d(wf, wr), both_d(kaf, kar), both_d(kmf, kmr)
        s = s_scr[...]
        sa = _split_dot_k((s * kk).reshape(m, MIX_W), g2).reshape(s.shape)
        vt = jnp.dot((ea * va + eb * vb).astype(BF16).reshape(m, MIX_W), g, preferred_element_type=F32).reshape(s.shape)
        s = s * w - sa * ka + vt * km
        s_scr[...] = s
        yb = jnp.dot((s * r).astype(BF16).reshape(m, MIX_W), g, preferred_element_type=F32).reshape(s.shape)
        y = jnp.sum(yb * ea, axis=1, keepdims=True)
        yf_ref[:, pl.ds(t, 1), :] = y[:n_batch]
        yr_ref[:, pl.ds(tr, 1), :] = y[n_batch:]
        return carry

    lax.fori_loop(0, n_steps, step, 0, unroll=2)


def _rw_scan_call(lay, shared, perdir, consts):
    b, n_tot, tc = lay.b, lay.n_tot, RW_TCHUNK
    assert lay.n_ctx % tc == 0 and lay.n_lat % tc == 0
    nch, cch = n_tot // tc, lay.n_ctx // tc
    rev_chunk = lambda i: jnp.where(i < cch, cch - 1 - i, nch - 1 + cch - i)
    blk_f = pl.BlockSpec((b, tc, MIX_W), lambda i: (0, i, 0))
    blk_r = pl.BlockSpec((b, tc, MIX_W), lambda i: (0, rev_chunk(i), 0))
    blk_f4 = pl.BlockSpec((1, b, tc, MIX_W), lambda i: (0, 0, i, 0))
    blk_r4 = pl.BlockSpec((1, b, tc, MIX_W), lambda i: (1, 0, rev_chunk(i), 0))
    sh = [a.reshape(b, n_tot, MIX_W) for a in shared]
    pd = [a.reshape(2, b, n_tot, MIX_W) for a in perdir]
    yf, yr = pl.pallas_call(
        functools.partial(_rw_scan_kernel, n_batch=b, n_steps=tc),
        grid=(nch,),
        in_specs=[blk_f] * 4 + [blk_r] * 4 + [blk_f4, blk_r4] * 3 + [_full(a) for a in consts],
        out_specs=[blk_f, blk_r],
        out_shape=[jax.ShapeDtypeStruct((b, n_tot, MIX_W), F32)] * 2,
        scratch_shapes=[pltpu.VMEM((2 * b, RW_DIM, MIX_W), F32)],
        compiler_params=_cparams("arbitrary"),
        name="rwkv_scan",
    )(*sh, *sh, pd[0], pd[0], pd[1], pd[1], pd[2], pd[2], *consts)
    return yf.reshape(b * n_tot, MIX_W), yr.reshape(b * n_tot, MIX_W)


def _rw_fin_kernel(yf_ref, yr_ref, bon_ref, gate_ref, g64_ref, lng_ref, lnb_ref, o_ref):
    y = yf_ref[...] + yr_ref[...]
    g64 = g64_ref[...]
    mean = _split_dot(y, g64) * (1.0 / RW_DIM)
    c = y - mean
    var = _split_dot(c * c, g64) * (1.0 / RW_DIM)
    out = c * lax.rsqrt(var + RW_LN_EPS) * lng_ref[...] + lnb_ref[...] + bon_ref[...]
    o_ref[...] = (out * gate_ref[...]).astype(BF16)


def _rw_fin_call(yf, yr, bon, gate, g64, lng, lnb, tm):
    t = yf.shape[0]
    row = pl.BlockSpec((tm, MIX_W), lambda i: (i, 0))
    return pl.pallas_call(
        _rw_fin_kernel,
        grid=(t // tm,),
        in_specs=[row, row, row, row, _full(g64), _full(lng), _full(lnb)],
        out_specs=row,
        out_shape=jax.ShapeDtypeStruct((t, MIX_W), BF16),
        compiler_params=_cparams("parallel"),
        name="rwkv_finish",
    )(yf, yr, bon, gate, g64, lng, lnb)


def _merge_kernel(x_ref, mod_ref, g_ref, wg_ref, ya_ref, yb_ref, yc_ref, yd_ref, wb_ref, wo_ref, o_ref):
    x = x_ref[...]
    h = _modulate(x, g_ref[...], mod_ref[0, 0:1, :], mod_ref[0, 1:2, :]).astype(BF16)
    merged = None
    for i, y_ref in enumerate((ya_ref, yb_ref, yc_ref, yd_ref)):
        gate = _sigmoid(jnp.dot(h, wg_ref[:, i * D_MODEL:(i + 1) * D_MODEL], preferred_element_type=F32))
        term = gate * jnp.dot(y_ref[...], wb_ref[i], preferred_element_type=F32)
        merged = term if merged is None else merged + term
    out = jnp.dot(merged.astype(BF16), wo_ref[...], preferred_element_type=F32)
    o_ref[...] = x + mod_ref[0, 2:3, :] * out


def _merge_call(lay, with_ctx, x_all, mod, g, w_gate, ya, yb, yc, yd, w_branch, w_out):
    tm = lay.t
    src, mrow = lay.src_tile(with_ctx), lay.mod_row(with_ctx)
    full_row = lambda w: pl.BlockSpec((tm, w), lambda i: (src(i), 0))
    out_row = lambda w: pl.BlockSpec((tm, w), lambda i: (i, 0))
    return pl.pallas_call(
        _merge_kernel,
        grid=(lay.n_tiles(with_ctx),),
        in_specs=[full_row(D_MODEL), pl.BlockSpec((1, 6, D_MODEL), lambda i: (mrow(i), 0, 0)), _full(g), _full(w_gate),
                  out_row(MIX_W), full_row(MIX_W), out_row(MIX_W), full_row(MIX_W), _full(w_branch), _full(w_out)],
        out_specs=out_row(D_MODEL),
        out_shape=jax.ShapeDtypeStruct((lay.rows(with_ctx), D_MODEL), F32),
        compiler_params=_cparams("parallel"),
        name="merge_out",
    )(x_all, mod, g, w_gate, ya, yb, yc, yd, w_branch, w_out)


def _router_kernel(x_ref, mod_ref, g_ref, wh_ref, wl_ref, bias_ref, f_ref, comb_ref):
    f = _modulate(x_ref[...], g_ref[...], mod_ref[0, 3:4, :], mod_ref[0, 4:5, :])
    fh = f.astype(BF16)
    f_ref[...] = fh
    fl = (f - fh.astype(F32)).astype(BF16)
    nt = (((1,), (1,)), ((), ()))
    wh, wl = wh_ref[...], wl_ref[...]
    logits = (lax.dot_general(wh, fh, nt, preferred_element_type=F32)
              + lax.dot_general(wh, fl, nt, preferred_element_type=F32)
              + lax.dot_general(wl, fh, nt, preferred_element_type=F32))
    scores = _sigmoid(logits)
    biased = scores + bias_ref[...]
    sc = [scores[e:e + 1, :] for e in range(N_EXPERTS)]
    bi = [biased[e:e + 1, :] for e in range(N_EXPERTS)]
    group_score = []
    for g in range(N_GROUPS):
        a, b, c, d = bi[4 * g:4 * g + 4]
        m1, n1, m2, n2 = jnp.maximum(a, b), jnp.minimum(a, b), jnp.maximum(c, d), jnp.minimum(c, d)
        group_score.append(jnp.maximum(m1, m2) + jnp.maximum(jnp.minimum(m1, m2), jnp.maximum(n1, n2)))

    def first_argmax(vals):
        top = functools.reduce(jnp.maximum, vals)
        seen, hot = None, []
        for v in vals:
            h = v == top
            if seen is not None:
                h = jnp.logical_and(h, jnp.logical_not(seen))
            seen = h if seen is None else jnp.logical_or(seen, h)
            hot.append(h)
        return hot

    in_group = first_argmax(group_score)
    masked = [jnp.where(in_group[e // EXPERTS_PER_GROUP], bi[e], -jnp.inf) for e in range(N_EXPERTS)]
    hot1 = first_argmax(masked)
    hot2 = first_argmax([jnp.where(h, -jnp.inf, v) for h, v in zip(hot1, masked)])
    w1 = functools.reduce(jnp.add, [jnp.where(h, s, 0.0) for h, s in zip(hot1, sc)])
    w2 = functools.reduce(jnp.add, [jnp.where(h, s, 0.0) for h, s in zip(hot2, sc)])
    inv_tot = 1.0 / (w1 + w2)
    for e in range(N_EXPERTS):
        comb_ref[e:e + 1, :] = (jnp.where(hot1[e], w1, 0.0) + jnp.where(hot2[e], w2, 0.0)) * inv_tot


def _router_call(lay, with_ctx, x, mod, g, wh, wl, bias):
    t, tm = x.shape[0], lay.t
    mrow = lay.mod_row(with_ctx)
    return pl.pallas_call(
        _router_kernel,
        grid=(t // tm,),
        in_specs=[pl.BlockSpec((tm, D_MODEL), lambda i: (i, 0)),
                  pl.BlockSpec((1, 6, D_MODEL), lambda i: (mrow(i), 0, 0)), _full(g), _full(wh), _full(wl), _full(bias)],
        out_specs=[pl.BlockSpec((tm, D_MODEL), lambda i: (i, 0)), pl.BlockSpec((N_EXPERTS, tm), lambda i: (0, i))],
        out_shape=[jax.ShapeDtypeStruct((t, D_MODEL), BF16), jax.ShapeDtypeStruct((N_EXPERTS, t), F32)],
        compiler_params=_cparams("parallel"),
        name="moe_router",
    )(x, mod, g, wh, wl, bias)


def _moe_kernel(f_ref, comb_ref, wg_ref, wu_ref, wd_ref, x_ref, modb_ref, modc_ref, o_ref, acc_ref,
                *, ctx_rows, tiles_per_seq):
    e = pl.program_id(1)

    @pl.when(e == 0)
    def _():
        acc_ref[...] = jnp.zeros_like(acc_ref)

    f = f_ref[...]
    gate = jnp.dot(f, wg_ref[0], preferred_element_type=F32)
    up = jnp.dot(f, wu_ref[0], preferred_element_type=F32)
    act = (gate * _sigmoid(gate) * up).astype(BF16)
    down = jnp.dot(act, wd_ref[0], preferred_element_type=F32)
    comb = comb_ref[...]
    lane = lax.broadcasted_iota(jnp.int32, comb.shape, 1)
    c_e = jnp.sum(jnp.where(lane == e, comb, 0.0), axis=1, keepdims=True)
    acc_ref[...] += c_e * down

    @pl.when(e == N_EXPERTS - 1)
    def _():
        res_gate = modb_ref[0, 5:6, :]
        if ctx_rows:
            row = lax.broadcasted_iota(jnp.int32, acc_ref.shape, 0)
            first = pl.program_id(0) % tiles_per_seq == 0
            res_gate = jnp.where(jnp.logical_and(first, row < ctx_rows), modc_ref[0, 5:6, :], res_gate)
        o_ref[...] = x_ref[...] + res_gate * acc_ref[...]


def _moe_call(lay, with_ctx, f, comb, wg, wu, wd, x, mod):
    t = f.shape[0]
    seq = lay.n_tot if with_ctx else lay.n_lat
    tm = MOE_TILE if seq % MOE_TILE == 0 else math.gcd(seq, 1024)
    tps = seq // tm
    ctx_rows = lay.n_ctx if with_ctx else 0
    assert ctx_rows <= tm
    wspec = lambda a: pl.BlockSpec((1,) + a.shape[1:], lambda i, e: (e, 0, 0))
    tok = lambda w: pl.BlockSpec((tm, w), lambda i, e: (i, 0))
    return pl.pallas_call(
        functools.partial(_moe_kernel, ctx_rows=ctx_rows, tiles_per_seq=tps),
        grid=(t // tm, N_EXPERTS),
        in_specs=[tok(D_MODEL), tok(N_EXPERTS), wspec(wg), wspec(wu), wspec(wd), tok(D_MODEL),
                  pl.BlockSpec((1, 6, D_MODEL), lambda i, e: (i // tps, 0, 0)),
                  pl.BlockSpec((1, 6, D_MODEL), lambda i, e: (lay.b, 0, 0))],
        out_specs=tok(D_MODEL),
        out_shape=jax.ShapeDtypeStruct((t, D_MODEL), F32),
        scratch_shapes=[pltpu.VMEM((tm, D_MODEL), F32)],
        compiler_params=_cparams("parallel", "arbitrary"),
        name="moe_experts",
    )(f, comb, wg, wu, wd, x, mod, mod)


def _block_ones(n, group):
    i = np.arange(n) // group
    return jnp.asarray(i[:, None] == i[None, :], dtype=BF16)


def _rope_tables(n_ctx, n_lat):
    rows = n_lat // GRID_W
    row = jnp.repeat(jnp.arange(rows, dtype=F32), GRID_W)
    col = jnp.tile(jnp.arange(GRID_W, dtype=F32), rows)

    def angles(rot_dim):
        n_freq = rot_dim // 4
        inv_freq = ROPE_BASE ** (-jnp.arange(n_freq, dtype=F32) / n_freq)
        ang = jnp.concatenate([row[:, None] * inv_freq, col[:, None] * inv_freq], axis=-1)
        return jnp.cos(ang), jnp.sin(ang)

    c, s = angles(DA_DIM)
    cda = jnp.tile(jnp.concatenate([c, c], -1), (1, 2 * DA_HEADS))
    sda = jnp.tile(jnp.concatenate([-s, s], -1), (1, 2 * DA_HEADS))
    c, s = angles(MLA_ROPE)
    one = jnp.ones((n_lat, MLA_NOPE), F32)
    pad = MLA_HEAD_PAD - MLA_NOPE - MLA_ROPE
    cml = jnp.tile(jnp.concatenate([one, c, c, jnp.ones((n_lat, pad), F32)], -1), (1, MLA_HEADS))
    sml = jnp.tile(jnp.concatenate([0 * one, -s, s, jnp.zeros((n_lat, pad), F32)], -1), (1, MLA_HEADS))
    ident = lambda t, v: jnp.concatenate([jnp.full((n_ctx, MIX_W), v, F32), t], axis=0)
    return ident(cda, 1.0), ident(sda, 0.0), ident(cml, 1.0), ident(sml, 0.0)


def _pad_heads(w, n_heads, src_w, lo, hi, dst_w=MLA_HEAD_PAD):
    w = w.reshape(w.shape[0], n_heads, src_w)[:, :, lo:hi]
    w = jnp.pad(w, ((0, 0), (0, 0), (0, dst_w - (hi - lo))))
    return w.reshape(w.shape[0], n_heads * dst_w)


def _mix_weight(w_in_l):
    w = w_in_l
    kr = w[:, 1344:1360]
    z = lambda n: jnp.zeros((D_MODEL, n), w.dtype)
    kr_wide = jnp.concatenate([jnp.concatenate([z(MLA_NOPE), kr, z(MLA_HEAD_PAD - MLA_NOPE - MLA_ROPE)], 1)] * MLA_HEADS, 1)
    return jnp.concatenate([w[:, 0:1024], w[:, 1024:1216], z(64), w[:, 1216:1344], kr_wide, w[:, 1360:2384]], axis=1).astype(BF16)


def kernel(x, c, ctx, c_ctx, w_ada, b_ada, norm_mix_g, norm_ffn_g, w_in, da_qk_norm_g, da_lambda, da_subln_g, s5_lam_re, s5_lam_im, s5_log_dt, s5_b_re, s5_b_im, s5_c_re, s5_c_im, s5_d, s5_w_glu, s5_b_glu, mla_cq_norm_g, mla_ckv_norm_g, mla_w_uq, mla_w_ukv, mla_qk_norm_g, rw_mu, rw_w0, rw_w1, rw_w2, rw_a0, rw_a1, rw_a2, rw_g1, rw_g2, rw_k_k, rw_k_a, rw_r_k, rw_ln_g, rw_ln_b, w_branch, w_out, router_w, router_bias, exp_w_gate, exp_w_up, exp_w_down):
    b, n_lat, dm = x.shape
    n_ctx = ctx.shape[1]
    depth = w_ada.shape[0]
    assert dm == D_MODEL
    lay = _Layout(b, n_ctx, n_lat)
    t_all = b * lay.n_tot
    tm_big = 2 * lay.t

    g32 = _block_ones(MIX_W, DA_DIM)
    g64 = _block_ones(MIX_W, RW_DIM)
    g64x2 = jnp.concatenate([g64, g64], axis=0)
    lane = np.arange(MIX_W) % RW_DIM
    vrow = np.arange(RW_DIM)[:, None]
    eye_a = jnp.asarray(lane[None, :] == vrow, F32)
    eye_b = jnp.asarray(lane[None, :] == (vrow + RW_DIM // 2) % RW_DIM, F32)
    tabs = _rope_tables(n_ctx, n_lat)
    row = lambda v: v.reshape(1, -1).astype(F32)
    bf = lambda a: a.astype(BF16)

    cc = jnp.zeros((16, dm), F32).at[:b].set(c).at[b].set(c_ctx)
    mod_all = _ada_call(cc, w_ada, b_ada)
    x_all = jnp.concatenate([ctx, x], axis=1).reshape(t_all, dm)

    wr_hi = router_w.T.astype(BF16)
    wr_lo = (router_w.T - wr_hi.astype(F32)).astype(BF16)
    r_bias = router_bias.reshape(N_EXPERTS, 1).astype(F32)

    for l in range(depth):
        need_ctx = l < depth - 1
        lambda_init = 0.8 - 0.6 * math.exp(-0.3 * l)
        mod = mod_all[l, :b + 1].reshape(b + 1, 6, dm)
        g_mix = row(norm_mix_g[l])
        da, s5u, mla, rw = _inproj_call(lay, x_all, mod, g_mix, _mix_weight(w_in[l]))

        gda = jnp.stack([jnp.tile(da_qk_norm_g[l, 0], 2 * DA_HEADS) * DA_DIM ** -0.5, jnp.tile(da_qk_norm_g[l, 1], 2 * DA_HEADS)])
        mla_pad = MLA_HEAD_PAD - MLA_NOPE - MLA_ROPE
        gml = jnp.stack([jnp.tile(jnp.pad(mla_qk_norm_g[l, 0], (0, mla_pad)), MLA_HEADS) * (MLA_NOPE + MLA_ROPE) ** -0.5,
                         jnp.tile(jnp.pad(mla_qk_norm_g[l, 1], (0, mla_pad)), MLA_HEADS)])
        wuq = bf(jnp.pad(_pad_heads(mla_w_uq[l], MLA_HEADS, MLA_NOPE + MLA_ROPE, 0, MLA_NOPE + MLA_ROPE), ((0, 64), (0, 0))))
        wuk = bf(_pad_heads(mla_w_ukv[l], MLA_HEADS, MLA_NOPE + MLA_VDIM, 0, MLA_NOPE))
        wuv = bf(_pad_heads(mla_w_ukv[l], MLA_HEADS, MLA_NOPE + MLA_VDIM, MLA_NOPE, MLA_NOPE + MLA_VDIM))
        consts = (g32, g64, gda.astype(F32), gml.astype(F32), row(jnp.pad(mla_cq_norm_g[l], (0, 64))), row(mla_ckv_norm_g[l]),
                  wuq, wuk, wuv)
        qd, kdt, vd, qm, kmt, vm = _qkprep_call(lay, da, mla, tabs, consts)

        lam32 = da_lambda[l].astype(F32)
        lmbda = (jnp.exp(jnp.sum(lam32[0] * lam32[1])) - jnp.exp(jnp.sum(lam32[2] * lam32[3])) + lambda_init).reshape(1, 1)
        subln = row(jnp.tile(da_subln_g[l], DA_HEADS) * (1.0 - lambda_init))
        ya = _attention(lay, qd, kdt, vd, (lmbda, subln, g64), True, need_ctx, "diff_attn")
        yc = _attention(lay, qm, kmt, vm, (lmbda, subln, g64), False, need_ctx, "mla_attn")

        mats = _s5_mats(s5_lam_re[l], s5_lam_im[l], s5_log_dt[l], s5_b_re[l], s5_b_im[l], s5_c_re[l], s5_c_im[l])
        yb = _s5_glu_call(s5u, _s5_scan(lay, s5u, mats), row(s5_d[l]), bf(s5_w_glu[l]), row(s5_b_glu[l]), tm_big)

        pre_consts = (row(rw_mu[l]), g64, row(rw_k_k[l]), row(rw_k_a[l]), row(rw_r_k[l]),
                      rw_w0[l].reshape(2, 1, MIX_W), bf(rw_w1[l]), bf(rw_w2[l]),
                      rw_a0[l].reshape(2, 1, MIX_W), bf(rw_a1[l]), bf(rw_a2[l]), bf(rw_g1[l]), bf(rw_g2[l]))
        r_, va_, vb_, kk_, wd_, kka_, km_, bon, gate = _rw_pre_call(lay, rw, pre_consts)
        yf, yr = _rw_scan_call(lay, (r_, va_, vb_, kk_), (wd_, kka_, km_), (g64, g64x2, eye_a, eye_b))
        yd = _rw_fin_call(yf, yr, bon, gate, g64, row(rw_ln_g[l]), row(rw_ln_b[l]), tm_big)

        x_mid = _merge_call(lay, need_ctx, x_all, mod, g_mix, bf(w_in[l][:, 2384:]), ya, yb, yc, yd,
                            bf(w_branch[l]), bf(w_out[l]))
        f, comb_t = _router_call(lay, need_ctx, x_mid, mod, row(norm_ffn_g[l]), wr_hi, wr_lo, r_bias)
        x_all = _moe_call(lay, need_ctx, f, comb_t.T, bf(exp_w_gate[l]), bf(exp_w_up[l]), bf(exp_w_down[l]), x_mid, mod)
    return x_all.reshape(b, n_lat, dm)
```

```python
import functools
import math

import numpy as np
import jax
import jax.numpy as jnp
from jax import lax
from jax.experimental import pallas as pl
from jax.experimental.pallas import tpu as pltpu

F32 = jnp.float32
BF16 = jnp.bfloat16

D_MODEL = 1024
GRID_W = 64
ROPE_BASE = 10000.0
EPS = 1e-6
DA_HEADS, DA_DIM, DA_VDIM = 4, 32, 64
S5_GROUPS, S5_CH, S5_STATE = 16, 16, 64
MLA_HEADS, MLA_NOPE, MLA_ROPE, MLA_VDIM = 4, 32, 16, 64
MLA_Q_RANK, MLA_KV_RANK = 192, 128
MLA_HEAD_PAD = 64
RW_HEADS, RW_DIM = 4, 64
RW_LN_EPS = 64e-5
N_BRANCH = 4
N_EXPERTS, N_GROUPS, EXPERTS_PER_GROUP = 16, 4, 4
D_FF = 512
MIX_W = 256

S5_CHUNK = 8
S5_FLAT = S5_CHUNK * MIX_W
S5_STATE_W = S5_GROUPS * S5_STATE
RW_TCHUNK = 128
TOKEN_TILE = 256
MOE_TILE = 1152

_DA_W, _S5_W, _MLA_W, _RW_W = 768, 256, 640, 1024
_MIX_COLS = _DA_W + _S5_W + _MLA_W + _RW_W

V7X_VMEM_BYTES = 64 * 2**20
_VMEM_LIMIT = V7X_VMEM_BYTES - 8 * 2**20


def _cparams(*sem):
    return pltpu.CompilerParams(dimension_semantics=sem, vmem_limit_bytes=_VMEM_LIMIT)


def _full(a):
    return pl.BlockSpec(a.shape, lambda *_, nd=a.ndim: (0,) * nd)


def _split_dot(x, w, terms=2):
    acc = None
    rem = x
    for i in range(terms):
        part = rem.astype(BF16)
        d = jnp.dot(part, w, preferred_element_type=F32)
        acc = d if acc is None else acc + d
        if i + 1 < terms:
            rem = rem - part.astype(F32)
    return acc


def _modulate(x, g, shift, scale):
    xn = x * lax.rsqrt(jnp.mean(x * x, axis=-1, keepdims=True) + EPS)
    return xn * g * (1.0 + scale) + shift


def _sigmoid(x):
    return 1.0 / (1.0 + jnp.exp(-x))


def _group_rms(x, ones_bd, inv_n, gain):
    ms = _split_dot(x * x, ones_bd) * inv_n
    return x * lax.rsqrt(ms + EPS) * gain


def _lane_partner(x, half, period, first_end):
    n = x.shape[1]
    lane = lax.broadcasted_iota(jnp.int32, x.shape, 1)
    up = pltpu.roll(x, n - half, axis=1)
    down = pltpu.roll(x, half, axis=1)
    return jnp.where((lane & (period - 1)) < first_end, up, down)


def _rope(x, cos_t, sin_t, half, period, first_end):
    return x * cos_t + _lane_partner(x, half, period, first_end) * sin_t


def _ada_kernel(c_ref, w_ref, b_ref, o_ref):
    c = c_ref[...]
    s = c * _sigmoid(c)
    o_ref[0] = jnp.dot(s.astype(BF16), w_ref[0].astype(BF16), preferred_element_type=F32) + b_ref[0]


def _ada_call(cc, w_ada, b_ada):
    depth, dm, n = w_ada.shape
    tn = n // 4
    return pl.pallas_call(
        _ada_kernel,
        grid=(depth, n // tn),
        in_specs=[
            pl.BlockSpec(cc.shape, lambda l, j: (0, 0)),
            pl.BlockSpec((1, dm, tn), lambda l, j: (l, 0, j)),
            pl.BlockSpec((1, 1, tn), lambda l, j: (l, 0, j)),
        ],
        out_specs=pl.BlockSpec((1, cc.shape[0], tn), lambda l, j: (l, 0, j)),
        out_shape=jax.ShapeDtypeStruct((depth, cc.shape[0], n), F32),
        compiler_params=_cparams("parallel", "parallel"),
        name="ada_mod",
    )(cc, w_ada, b_ada.reshape(depth, 1, n))


class _Layout:
    def __init__(self, n_batch, n_ctx, n_lat):
        t = TOKEN_TILE
        assert n_ctx % t == 0 and n_lat % t == 0
        self.b, self.n_ctx, self.n_lat, self.n_tot = n_batch, n_ctx, n_lat, n_ctx + n_lat
        self.t = t
        self.ctx_tiles, self.lat_tiles, self.seq_tiles = n_ctx // t, n_lat // t, (n_ctx + n_lat) // t

    def rows(self, with_ctx):
        return self.b * (self.n_tot if with_ctx else self.n_lat)

    def n_tiles(self, with_ctx):
        return self.b * (self.seq_tiles if with_ctx else self.lat_tiles)

    def src_tile(self, with_ctx):
        if with_ctx:
            return lambda i: i
        return lambda i: (i // self.lat_tiles) * self.seq_tiles + i % self.lat_tiles + self.ctx_tiles

    def mod_row(self, with_ctx):
        if with_ctx:
            return lambda i: jnp.where(i % self.seq_tiles < self.ctx_tiles, self.b, i // self.seq_tiles)
        return lambda i: i // self.lat_tiles


def _inproj_kernel(x_ref, mod_ref, g_ref, w_ref, da_ref, s5a_ref, s5b_ref, mla_ref, rw_ref):
    h = _modulate(x_ref[...], g_ref[...], mod_ref[0, 0:1, :], mod_ref[0, 1:2, :])
    acc = jnp.dot(h.astype(BF16), w_ref[...], preferred_element_type=F32)
    da_ref[...] = acc[:, 0:_DA_W]
    s5a_ref[...] = acc[:, _DA_W:_DA_W + _S5_W // 2]
    s5b_ref[...] = acc[:, _DA_W + _S5_W // 2:_DA_W + _S5_W]
    mla_ref[...] = acc[:, _DA_W + _S5_W:_DA_W + _S5_W + _MLA_W]
    rw_ref[...] = acc[:, _DA_W + _S5_W + _MLA_W:_MIX_COLS]


def _inproj_call(lay, x_all, mod, g, w_mix):
    t, tm = x_all.shape[0], lay.t
    widths = (_DA_W, _S5_W // 2, _S5_W // 2, _MLA_W, _RW_W)
    mrow = lay.mod_row(True)
    return pl.pallas_call(
        _inproj_kernel,
        grid=(t // tm,),
        in_specs=[
            pl.BlockSpec((tm, D_MODEL), lambda i: (i, 0)),
            pl.BlockSpec((1, 6, D_MODEL), lambda i: (mrow(i), 0, 0)),
            _full(g), _full(w_mix),
        ],
        out_specs=[pl.BlockSpec((tm, w), lambda i: (i, 0)) for w in widths],
        out_shape=[jax.ShapeDtypeStruct((t, w), F32) for w in widths],
        compiler_params=_cparams("parallel"),
        name="in_proj",
    )(x_all, mod, g, w_mix)


def _qkprep_kernel(da_ref, mla_ref, cda_ref, sda_ref, cml_ref, sml_ref, g32_ref, g64_ref,
                   gda_ref, gml_ref, cqg_ref, ckvg_ref, wuq_ref, wuk_ref, wuv_ref,
                   qd_ref, kd_ref, vd_ref, qm_ref, km_ref, vm_ref):
    g32 = g32_ref[...]
    g64 = g64_ref[...]
    cda, sda = cda_ref[...], sda_ref[...]
    q = _group_rms(da_ref[:, 0:MIX_W], g32, 1.0 / DA_DIM, gda_ref[0:1, :])
    qd_ref[...] = _rope(q, cda, sda, DA_DIM // 2, DA_DIM, DA_DIM // 2).astype(BF16)
    k = _group_rms(da_ref[:, MIX_W:2 * MIX_W], g32, 1.0 / DA_DIM, gda_ref[1:2, :])
    kd_ref[0] = _rope(k, cda, sda, DA_DIM // 2, DA_DIM, DA_DIM // 2).T.astype(BF16)
    vd_ref[...] = da_ref[:, 2 * MIX_W:3 * MIX_W].astype(BF16)

    cml, sml = cml_ref[...], sml_ref[...]
    cq = mla_ref[:, 0:256]
    cqn = cq * lax.rsqrt(jnp.sum(cq * cq, axis=-1, keepdims=True) * (1.0 / MLA_Q_RANK) + EPS) * cqg_ref[...]
    q = jnp.dot(cqn.astype(BF16), wuq_ref[...], preferred_element_type=F32)
    ckv = mla_ref[:, 256:384]
    ckvn = ckv * lax.rsqrt(jnp.mean(ckv * ckv, axis=-1, keepdims=True) + EPS) * ckvg_ref[...]
    ckvb = ckvn.astype(BF16)
    k = jnp.dot(ckvb, wuk_ref[...], preferred_element_type=F32) + mla_ref[:, 384:640]
    vm_ref[...] = jnp.dot(ckvb, wuv_ref[...], preferred_element_type=F32).astype(BF16)
    inv_n = 1.0 / (MLA_NOPE + MLA_ROPE)
    half = MLA_ROPE // 2
    q = _group_rms(q, g64, inv_n, gml_ref[0:1, :])
    qm_ref[...] = _rope(q, cml, sml, half, MLA_HEAD_PAD, MLA_NOPE + half).astype(BF16)
    k = _group_rms(k, g64, inv_n, gml_ref[1:2, :])
    km_ref[0] = _rope(k, cml, sml, half, MLA_HEAD_PAD, MLA_NOPE + half).T.astype(BF16)


def _qkprep_call(lay, da, mla, tabs, consts):
    t, tm = da.shape[0], lay.t
    st = lay.seq_tiles
    row = pl.BlockSpec((tm, MIX_W), lambda i: (i, 0))
    key_t = pl.BlockSpec((1, MIX_W, tm), lambda i: (i // st, 0, i % st))
    in_specs = [pl.BlockSpec((tm, _DA_W), lambda i: (i, 0)), pl.BlockSpec((tm, _MLA_W), lambda i: (i, 0))]
    in_specs += [pl.BlockSpec((tm, MIX_W), lambda i: (i % st, 0)) for _ in tabs]
    in_specs += [_full(a) for a in consts]
    tok = jax.ShapeDtypeStruct((t, MIX_W), BF16)
    keys = jax.ShapeDtypeStruct((lay.b, MIX_W, lay.n_tot), BF16)
    return pl.pallas_call(
        _qkprep_kernel,
        grid=(t // tm,),
        in_specs=in_specs,
        out_specs=[row, key_t, row, row, key_t, row],
        out_shape=[tok, keys, tok, tok, keys, tok],
        compiler_params=_cparams("parallel"),
        name="qk_prep",
    )(da, mla, *tabs, *consts)


def _softmax_parts(q, kt):
    s = jnp.dot(q, kt, preferred_element_type=F32)
    p = jnp.exp(s - jnp.max(s, axis=-1, keepdims=True))
    return p, 1.0 / jnp.sum(p, axis=-1, keepdims=True)


def _attn_heads(q, kt_ref, v_ref, nk, diff, lam):
    lane = lax.broadcasted_iota(jnp.int32, (q.shape[0], MIX_W), 1)
    v = v_ref[0, 0:nk, :]
    acc = jnp.zeros((q.shape[0], MIX_W), F32)
    for h in range(DA_HEADS):
        if diff:
            e0, e1 = 2 * h * DA_DIM, (2 * h + 1) * DA_DIM
            p0, r0 = _softmax_parts(q[:, e0:e0 + DA_DIM], kt_ref[0, e0:e0 + DA_DIM, 0:nk])
            p1, r1 = _softmax_parts(q[:, e1:e1 + DA_DIM], kt_ref[0, e1:e1 + DA_DIM, 0:nk])
            o = jnp.dot((p0 * r0 - p1 * (r1 * lam)).astype(BF16), v, preferred_element_type=F32)
        else:
            e0 = h * MLA_HEAD_PAD
            p, r = _softmax_parts(q[:, e0:e0 + MLA_HEAD_PAD], kt_ref[0, e0:e0 + MLA_HEAD_PAD, 0:nk])
            o = jnp.dot(p.astype(BF16), v, preferred_element_type=F32) * r
        in_head = jnp.logical_and(lane >= h * DA_VDIM, lane < (h + 1) * DA_VDIM)
        acc = jnp.where(in_head, o, acc)
    return acc


def _attn_kernel(q_ref, kt_ref, v_ref, lam_ref, gain_ref, g64_ref, o_ref, *, diff, n_ctx, n_tot, ctx_tiles):
    q = q_ref[...]
    lam = lam_ref[...]

    def run(nk):
        o = _attn_heads(q, kt_ref, v_ref, nk, diff, lam)
        if diff:
            o = _group_rms(o, g64_ref[...], 1.0 / DA_VDIM, gain_ref[...])
        o_ref[...] = o.astype(BF16)

    if ctx_tiles:
        is_ctx = pl.program_id(1) < ctx_tiles
        pl.when(is_ctx)(lambda: run(n_ctx))
        pl.when(jnp.logical_not(is_ctx))(lambda: run(n_tot))
    else:
        run(n_tot)


def _attention(lay, q, kt, v, extra, diff, with_ctx, name):
    tq = lay.t
    tiles = lay.seq_tiles if with_ctx else lay.lat_tiles
    off = 0 if with_ctx else lay.ctx_tiles
    v3 = v.reshape(lay.b, lay.n_tot, MIX_W)
    kern = functools.partial(_attn_kernel, diff=diff, n_ctx=lay.n_ctx, n_tot=lay.n_tot,
                             ctx_tiles=lay.ctx_tiles if with_ctx else 0)
    return pl.pallas_call(
        kern,
        grid=(lay.b, tiles),
        in_specs=[
            pl.BlockSpec((tq, MIX_W), lambda b, j: (b * lay.seq_tiles + j + off, 0)),
            pl.BlockSpec((1, MIX_W, lay.n_tot), lambda b, j: (b, 0, 0)),
            pl.BlockSpec((1, lay.n_tot, MIX_W), lambda b, j: (b, 0, 0)),
        ] + [_full(a) for a in extra],
        out_specs=pl.BlockSpec((tq, MIX_W), lambda b, j: (b * tiles + j, 0)),
        out_shape=jax.ShapeDtypeStruct((lay.rows(with_ctx), MIX_W), BF16),
        compiler_params=_cparams("parallel", "parallel"),
        name=name,
    )(q, kt, v3, *extra)


def _chunk_rows(ua_ref, ub_ref):
    n = ua_ref.shape[0] // S5_CHUNK
    parts = []
    for s in range(S5_CHUNK):
        rows = pl.ds(s, n, stride=S5_CHUNK)
        parts += [ua_ref[rows, :], ub_ref[rows, :]]
    return jnp.concatenate(parts, axis=1).astype(BF16)


def _s5_proj_kernel(ua_ref, ub_ref, bre_ref, bim_ref, sre_ref, sim_ref):
    u = _chunk_rows(ua_ref, ub_ref)
    sre_ref[0] = jnp.dot(u, bre_ref[0], preferred_element_type=F32)
    sim_ref[0] = jnp.dot(u, bim_ref[0], preferred_element_type=F32)


def _s5_rec_kernel(sre_ref, sim_ref, are_ref, aim_ref, hre_ref, him_ref, *, n_batch, n_chunks, ctx_chunks):
    rev = pl.program_id(0) == 1
    ar, ai = are_ref[0], aim_ref[0]
    sre, sim, hre, him = sre_ref.at[0], sim_ref.at[0], hre_ref.at[0], him_ref.at[0]

    def step(i, carry):
        hr, hi = carry
        k_rev = jnp.where(i < ctx_chunks, ctx_chunks - 1 - i, n_chunks - 1 + ctx_chunks - i)
        k = jnp.where(rev, k_rev, i)
        rows = pl.ds(k, n_batch, stride=n_chunks)
        hre[rows, :] = hr
        him[rows, :] = hi
        return ar * hr - ai * hi + sre[rows, :], ar * hi + ai * hr + sim[rows, :]

    zero = jnp.zeros((n_batch, 128), F32)
    lax.fori_loop(0, n_chunks, step, (zero, zero))


def _s5_out_kernel(ua_ref, ub_ref, hre_ref, him_ref, m_ref, cre_ref, cim_ref, ya_ref, yb_ref):
    y = jnp.dot(_chunk_rows(ua_ref, ub_ref), m_ref[0], preferred_element_type=F32)
    y = y + _split_dot(hre_ref[0], cre_ref[0]) + _split_dot(him_ref[0], cim_ref[0])
    n = y.shape[0]
    ya, yb = ya_ref.at[0], yb_ref.at[0]
    for s in range(S5_CHUNK):
        rows = pl.ds(s, n, stride=S5_CHUNK)
        ya[rows, :] = y[:, s * MIX_W:s * MIX_W + 128]
        yb[rows, :] = y[:, s * MIX_W + 128:(s + 1) * MIX_W]


def _s5_mats(lam_re, lam_im, log_dt, b_re, b_im, c_re, c_im):
    hp = lax.Precision.HIGHEST
    L, G, P, CH = S5_CHUNK, S5_GROUPS, S5_STATE, S5_CH
    lr, li = lam_re.astype(F32), lam_im.astype(F32)
    dt = jnp.exp(log_dt.astype(F32))[..., None]
    zr, zi = lr * dt, li * dt
    j = jnp.arange(L + 1, dtype=F32)[:, None, None, None]
    mag = jnp.exp(zr[None] * j)
    pw_re, pw_im = mag * jnp.cos(zi[None] * j), mag * jnp.sin(zi[None] * j)
    nr, ni = pw_re[1] - 1.0, pw_im[1]
    den = lr * lr + li * li
    cr, ci = (nr * lr + ni * li) / den, (ni * lr - nr * li) / den
    bre, bim = b_re.astype(F32), b_im.astype(F32)
    bb_re = cr[..., None] * bre - ci[..., None] * bim
    bb_im = cr[..., None] * bim + ci[..., None] * bre
    x_re = pw_re[..., None] * bb_re[None] - pw_im[..., None] * bb_im[None]
    x_im = pw_re[..., None] * bb_im[None] + pw_im[..., None] * bb_re[None]
    cre, cim = c_re.astype(F32), c_im.astype(F32)
    kern = (jnp.einsum('dgcp,jdgpe->dgjce', cre, x_re[:L], precision=hp)
            - jnp.einsum('dgcp,jdgpe->dgjce', cim, x_im[:L], precision=hp))
    eye = jnp.eye(G, dtype=F32)
    s_idx, t_idx = jnp.arange(L)[:, None], jnp.arange(L)[None, :]
    fwd_pow, rev_pow = jnp.arange(L - 1, -1, -1), jnp.arange(L)
    m, b_r, b_i, c_r, c_i = [], [], [], [], []
    for d in range(2):
        lag = (t_idx - s_idx) if d == 0 else (s_idx - t_idx)
        kst = jnp.where((lag >= 0)[None, :, :, None, None], kern[d][:, jnp.clip(lag, 0, L - 1)], 0.0)
        kst = kst.transpose(1, 0, 4, 2, 3)
        m.append((kst[:, :, :, :, None, :] * eye[None, :, None, None, :, None]).reshape(L * G * CH, L * G * CH))
        pw = fwd_pow if d == 0 else rev_pow
        for src, dst in ((x_re, b_r), (x_im, b_i)):
            xs = src[pw, d].transpose(0, 1, 3, 2)
            dst.append((xs[:, :, :, None, :] * eye[None, :, None, :, None]).reshape(L * G * CH, G * P))
        q = jnp.arange(1, L + 1) if d == 0 else jnp.arange(L, 0, -1)
        ca_re = cre[d][None] * pw_re[q, d][:, :, None, :] - cim[d][None] * pw_im[q, d][:, :, None, :]
        ca_im = cre[d][None] * pw_im[q, d][:, :, None, :] + cim[d][None] * pw_re[q, d][:, :, None, :]
        for src, dst in ((ca_re, c_r), (-ca_im, c_i)):
            cs = src.transpose(1, 3, 0, 2)
            dst.append((cs[:, :, :, None, :] * eye[:, None, None, :, None]).reshape(G * P, L * G * CH))
    stack = lambda xs: jnp.stack(xs).astype(BF16)
    a_re, a_im = pw_re[L].reshape(2, 1, G * P), pw_im[L].reshape(2, 1, G * P)
    return stack(m), stack(b_r), stack(b_i), stack(c_r), stack(c_i), a_re, a_im


def _s5_scan(lay, ua, ub, mats):
    m, b_r, b_i, c_r, c_i, a_re, a_im = mats
    n_chunks = lay.n_tot // S5_CHUNK
    rows = lay.b * n_chunks
    tr = min(lay.t, rows)
    tok = tr * S5_CHUNK
    half = MIX_W // 2
    wspec = lambda a: pl.BlockSpec((1,) + a.shape[1:], lambda d, i: (d, 0, 0))
    state = jax.ShapeDtypeStruct((2, rows, S5_STATE_W), F32)
    sblk = pl.BlockSpec((1, tr, S5_STATE_W), lambda d, i: (d, i, 0))
    ublk = pl.BlockSpec((tok, half), lambda d, i: (i, 0))
    s_re, s_im = pl.pallas_call(
        _s5_proj_kernel,
        grid=(2, rows // tr),
        in_specs=[ublk, ublk, wspec(b_r), wspec(b_i)],
        out_specs=[sblk, sblk],
        out_shape=[state, state],
        compiler_params=_cparams("parallel", "parallel"),
        name="s5_proj",
    )(ua, ub, b_r, b_i)
    col = pl.BlockSpec((1, rows, 128), lambda d, j: (d, 0, j))
    acol = pl.BlockSpec((1, 1, 128), lambda d, j: (d, 0, j))
    h_re, h_im = pl.pallas_call(
        functools.partial(_s5_rec_kernel, n_batch=lay.b, n_chunks=n_chunks, ctx_chunks=lay.n_ctx // S5_CHUNK),
        grid=(2, S5_STATE_W // 128),
        in_specs=[col, col, acol, acol],
        out_specs=[col, col],
        out_shape=[state, state],
        compiler_params=_cparams("parallel", "parallel"),
        name="s5_rec",
    )(s_re, s_im, a_re, a_im)
    yblk = pl.BlockSpec((1, tok, half), lambda d, i: (d, i, 0))
    yshape = jax.ShapeDtypeStruct((2, lay.b * lay.n_tot, half), F32)
    return pl.pallas_call(
        _s5_out_kernel,
        grid=(2, rows // tr),
        in_specs=[ublk, ublk, sblk, sblk, wspec(m), wspec(c_r), wspec(c_i)],
        out_specs=[yblk, yblk],
        out_shape=[yshape, yshape],
        compiler_params=_cparams("parallel", "parallel"),
        name="s5_out",
    )(ua, ub, h_re, h_im, m, c_r, c_i)


def _s5_glu_kernel(ua_ref, ub_ref, ya_ref, yb_ref, d_ref, w_ref, b_ref, o_ref):
    u = jnp.concatenate([ua_ref[...], ub_ref[...]], axis=1)
    y = d_ref[...] * u + jnp.concatenate([ya_ref[0] + ya_ref[1], yb_ref[0] + yb_ref[1]], axis=1)
    z = 0.5 * y * (1.0 + jnp.tanh(math.sqrt(2.0 / math.pi) * (y + 0.044715 * (y * y * y))))
    gate = _sigmoid(jnp.dot(z.astype(BF16), w_ref[...], preferred_element_type=F32) + b_ref[...])
    o_ref[...] = (z * gate).astype(BF16)


def _s5_glu_call(ua, ub, ya, yb, d, w, bias, tm):
    t, half = ua.shape
    urow = pl.BlockSpec((tm, half), lambda i: (i, 0))
    yrow = pl.BlockSpec((2, tm, half), lambda i: (0, i, 0))
    return pl.pallas_call(
        _s5_glu_kernel,
        grid=(t // tm,),
        in_specs=[urow, urow, yrow, yrow, _full(d), _full(w), _full(bias)],
        out_specs=pl.BlockSpec((tm, MIX_W), lambda i: (i, 0)),
        out_shape=jax.ShapeDtypeStruct((t, MIX_W), BF16),
        compiler_params=_cparams("parallel"),
        name="s5_glu",
    )(ua, ub, ya, yb, d, w, bias)


def _rw_pre_kernel(x_ref, prev_ref, next_ref, mu_ref, g64_ref, kk_g_ref, ka_ref, rk_ref,
                   w0_ref, w1_ref, w2_ref, a0_ref, a1_ref, a2_ref, g1_ref, g2_ref,
                   r_ref, va_ref, vb_ref, kk_ref, wd_ref, kka_ref, km_ref, bon_ref, gate_ref,
                   *, seq_tiles, ctx_tiles):
    x = x_ref[...]
    n = x.shape[0]
    j = pl.program_id(0) % seq_tiles
    starts = jnp.logical_or(j == 0, j == ctx_tiles)
    ends = jnp.logical_or(j == ctx_tiles - 1, j == seq_tiles - 1)
    prev_row = jnp.where(starts, 0.0, prev_ref[0, 7:8, :])
    next_row = jnp.where(ends, 0.0, next_ref[0, 0:1, :])
    row = lax.broadcasted_iota(jnp.int32, x.shape, 0)
    left = jnp.where(row == 0, prev_row, pltpu.roll(x, 1, axis=0))
    right = jnp.where(row == n - 1, next_row, pltpu.roll(x, n - 1, axis=0))
    x = x + (0.5 * (left + right) - x) * mu_ref[...]
    r, k, v, xd = (x[:, i * MIX_W:(i + 1) * MIX_W] for i in range(4))
    g64 = g64_ref[...]
    kscaled = k * kk_g_ref[...]
    kk = kscaled / jnp.maximum(jnp.sqrt(_split_dot(kscaled * kscaled, g64)), 1e-12)
    xdb = xd.astype(BF16)
    r_ref[...] = r
    kk_ref[...] = kk
    va = v.astype(BF16).astype(F32)
    va_ref[...] = va
    vb_ref[...] = _lane_partner((v - va).astype(BF16).astype(F32), RW_DIM // 2, RW_DIM, RW_DIM // 2)
    km_sum = None
    for d in range(2):
        lo = jnp.tanh(jnp.dot(xdb, w1_ref[d], preferred_element_type=F32))
        w_raw = w0_ref[d] + jnp.dot(lo.astype(BF16), w2_ref[d], preferred_element_type=F32)
        wd_ref[d] = jnp.exp(-_sigmoid(w_raw) * math.exp(-0.5))
        ar = jnp.dot(xdb, a1_ref[d], preferred_element_type=F32)
        a = _sigmoid(a0_ref[d] + jnp.dot(ar.astype(BF16), a2_ref[d], preferred_element_type=F32))
        km = k * (1.0 + (a - 1.0) * ka_ref[...])
        kka_ref[d] = kk * a
        km_ref[d] = km
        km_sum = km if km_sum is None else km_sum + km
    bon_ref[...] = _split_dot(r * km_sum * rk_ref[...], g64) * v
    gr = _sigmoid(jnp.dot(xdb, g1_ref[...], preferred_element_type=F32))
    gate_ref[...] = jnp.dot(gr.astype(BF16), g2_ref[...], preferred_element_type=F32)


def _rw_pre_call(lay, rw, consts):
    t, tr = rw.shape[0], lay.t
    nt = t // tr
    g8 = tr // 8
    rw8 = rw.reshape(t // 8, 8, _RW_W)
    row = pl.BlockSpec((tr, MIX_W), lambda i: (i, 0))
    row2 = pl.BlockSpec((2, tr, MIX_W), lambda i: (0, i, 0))
    sd = jax.ShapeDtypeStruct((t, MIX_W), F32)
    sd2 = jax.ShapeDtypeStruct((2, t, MIX_W), F32)
    return pl.pallas_call(
        functools.partial(_rw_pre_kernel, seq_tiles=lay.seq_tiles, ctx_tiles=lay.ctx_tiles),
        grid=(nt,),
        in_specs=[pl.BlockSpec((tr, _RW_W), lambda i: (i, 0)),
                  pl.BlockSpec((1, 8, _RW_W), lambda i: (jnp.maximum(i * g8 - 1, 0), 0, 0)),
                  pl.BlockSpec((1, 8, _RW_W), lambda i: (jnp.minimum((i + 1) * g8, t // 8 - 1), 0, 0))]
                 + [_full(a) for a in consts],
        out_specs=[row, row, row, row, row2, row2, row2, row, row],
        out_shape=[sd, sd, sd, sd, sd2, sd2, sd2, sd, sd],
        compiler_params=_cparams("parallel"),
        name="rwkv_pre",
    )(rw, rw8, rw8, *consts)


def _rw_scan_kernel(rf, vaf, vbf, kkf, rr, var, vbr, kkr, wf, wr, kaf, kar, kmf, kmr, g_ref, ea_ref, eb_ref,
                    yf_ref, yr_ref, s_scr, *, n_batch, n_steps):
    @pl.when(pl.program_id(0) == 0)
    def _():
        s_scr[...] = jnp.zeros_like(s_scr)

    g = g_ref[...]
    ea, eb = ea_ref[...], eb_ref[...]
    ea16, eb16 = ea.astype(BF16), eb.astype(BF16)
    m = 2 * n_batch * RW_DIM
    block_sum = lambda x: jnp.dot(x.reshape(m, MIX_W), g, preferred_element_type=F32)

    def step(t, carry):
        tr = n_steps - 1 - t
        both = lambda a, b: jnp.concatenate([a[:, pl.ds(t, 1), :], b[:, pl.ds(tr, 1), :]], axis=0)
        both_d = lambda a, b: jnp.concatenate([a[0, :, pl.ds(t, 1), :], b[0, :, pl.ds(tr, 1), :]], axis=0)
        kk, r, va, vb = both(kkf, kkr), both(rf, rr), both(vaf, var), both(vbf, vbr)
        w, ka, km = both_d(wf, wr), both_d(kaf, kar), both_d(kmf, kmr)
        s = s_scr[...]
        sa = block_sum((s * kk).astype(BF16)).reshape(s.shape)
        vt = block_sum(ea16 * va.astype(BF16) + eb16 * vb.astype(BF16)).reshape(s.shape)
        s = s * w - sa * ka + vt * km
        s_scr[...] = s
        yb = block_sum((s * r).astype(BF16)).reshape(s.shape)
        y = jnp.sum(yb * ea, axis=1, keepdims=True)
        yf_ref[:, pl.ds(t, 1), :] = y[:n_batch]
        yr_ref[:, pl.ds(tr, 1), :] = y[n_batch:]
        return carry

    lax.fori_loop(0, n_steps, step, 0, unroll=2)


def _rwkv_branch(lay, rw, pre_consts, g64, lng, lnb):
    r_, va_, vb_, kk_, wd_, kka_, km_, bon, gate = _rw_pre_call(lay, rw, pre_consts)
    lane = np.arange(MIX_W) % RW_DIM
    vrow = np.arange(RW_DIM)[:, None]
    eye_a = jnp.asarray(lane[None, :] == vrow, F32)
    eye_b = jnp.asarray(lane[None, :] == (vrow + RW_DIM // 2) % RW_DIM, F32)
    consts = (g64, eye_a, eye_b)
    yf, yr = _rw_scan_call(lay, (r_, va_, vb_, kk_), (wd_, kka_, km_), consts)
    return _rw_fin_call(yf, yr, bon, gate, g64, lng, lnb, 2 * lay.t)


def _rw_scan_call(lay, shared, perdir, consts):
    b, n_tot, tc = lay.b, lay.n_tot, RW_TCHUNK
    assert lay.n_ctx % tc == 0 and lay.n_lat % tc == 0
    nch, cch = n_tot // tc, lay.n_ctx // tc
    rev_chunk = lambda i: jnp.where(i < cch, cch - 1 - i, nch - 1 + cch - i)
    blk_f = pl.BlockSpec((b, tc, MIX_W), lambda i: (0, i, 0))
    blk_r = pl.BlockSpec((b, tc, MIX_W), lambda i: (0, rev_chunk(i), 0))
    blk_f4 = pl.BlockSpec((1, b, tc, MIX_W), lambda i: (0, 0, i, 0))
    blk_r4 = pl.BlockSpec((1, b, tc, MIX_W), lambda i: (1, 0, rev_chunk(i), 0))
    sh = [a.reshape(b, n_tot, MIX_W) for a in shared]
    pd = [a.reshape(2, b, n_tot, MIX_W) for a in perdir]
    yf, yr = pl.pallas_call(
        functools.partial(_rw_scan_kernel, n_batch=b, n_steps=tc),
        grid=(nch,),
        in_specs=[blk_f] * 4 + [blk_r] * 4 + [blk_f4, blk_r4] * 3 + [_full(a) for a in consts],
        out_specs=[blk_f, blk_r],
        out_shape=[jax.ShapeDtypeStruct((b, n_tot, MIX_W), F32)] * 2,
        scratch_shapes=[pltpu.VMEM((2 * b, RW_DIM, MIX_W), F32)],
        compiler_params=_cparams("arbitrary"),
        name="rwkv_scan",
    )(*sh, *sh, pd[0], pd[0], pd[1], pd[1], pd[2], pd[2], *consts)
    return yf.reshape(b * n_tot, MIX_W), yr.reshape(b * n_tot, MIX_W)


def _rw_fin_kernel(yf_ref, yr_ref, bon_ref, gate_ref, g64_ref, lng_ref, lnb_ref, o_ref):
    y = yf_ref[...] + yr_ref[...]
    g64 = g64_ref[...]
    mean = _split_dot(y, g64) * (1.0 / RW_DIM)
    c = y - mean
    var = _split_dot(c * c, g64) * (1.0 / RW_DIM)
    out = c * lax.rsqrt(var + RW_LN_EPS) * lng_ref[...] + lnb_ref[...] + bon_ref[...]
    o_ref[...] = (out * gate_ref[...]).astype(BF16)


def _rw_fin_call(yf, yr, bon, gate, g64, lng, lnb, tm):
    t = yf.shape[0]
    row = pl.BlockSpec((tm, MIX_W), lambda i: (i, 0))
    return pl.pallas_call(
        _rw_fin_kernel,
        grid=(t // tm,),
        in_specs=[row, row, row, row, _full(g64), _full(lng), _full(lnb)],
        out_specs=row,
        out_shape=jax.ShapeDtypeStruct((t, MIX_W), BF16),
        compiler_params=_cparams("parallel"),
        name="rwkv_finish",
    )(yf, yr, bon, gate, g64, lng, lnb)


def _merge_kernel(x_ref, mod_ref, g_ref, wg_ref, ya_ref, yb_ref, yc_ref, yd_ref, wb_ref, wo_ref, o_ref):
    x = x_ref[...]
    h = _modulate(x, g_ref[...], mod_ref[0, 0:1, :], mod_ref[0, 1:2, :]).astype(BF16)
    merged = None
    for i, y_ref in enumerate((ya_ref, yb_ref, yc_ref, yd_ref)):
        gate = _sigmoid(jnp.dot(h, wg_ref[:, i * D_MODEL:(i + 1) * D_MODEL], preferred_element_type=F32))
        term = gate * jnp.dot(y_ref[...], wb_ref[i], preferred_element_type=F32)
        merged = term if merged is None else merged + term
    out = jnp.dot(merged.astype(BF16), wo_ref[...], preferred_element_type=F32)
    o_ref[...] = x + mod_ref[0, 2:3, :] * out


def _merge_call(lay, with_ctx, x_all, mod, g, w_gate, ya, yb, yc, yd, w_branch, w_out):
    tm = lay.t
    src, mrow = lay.src_tile(with_ctx), lay.mod_row(with_ctx)
    full_row = lambda w: pl.BlockSpec((tm, w), lambda i: (src(i), 0))
    out_row = lambda w: pl.BlockSpec((tm, w), lambda i: (i, 0))
    return pl.pallas_call(
        _merge_kernel,
        grid=(lay.n_tiles(with_ctx),),
        in_specs=[full_row(D_MODEL), pl.BlockSpec((1, 6, D_MODEL), lambda i: (mrow(i), 0, 0)), _full(g), _full(w_gate),
                  out_row(MIX_W), full_row(MIX_W), out_row(MIX_W), full_row(MIX_W), _full(w_branch), _full(w_out)],
        out_specs=out_row(D_MODEL),
        out_shape=jax.ShapeDtypeStruct((lay.rows(with_ctx), D_MODEL), F32),
        compiler_params=_cparams("parallel"),
        name="merge_out",
    )(x_all, mod, g, w_gate, ya, yb, yc, yd, w_branch, w_out)


def _router_kernel(x_ref, mod_ref, g_ref, wh_ref, wl_ref, bias_ref, f_ref, comb_ref):
    f = _modulate(x_ref[...], g_ref[...], mod_ref[0, 3:4, :], mod_ref[0, 4:5, :])
    fh = f.astype(BF16)
    f_ref[...] = fh
    fl = (f - fh.astype(F32)).astype(BF16)
    nt = (((1,), (1,)), ((), ()))
    wh, wl = wh_ref[...], wl_ref[...]
    logits = (lax.dot_general(wh, fh, nt, preferred_element_type=F32)
              + lax.dot_general(wh, fl, nt, preferred_element_type=F32)
              + lax.dot_general(wl, fh, nt, preferred_element_type=F32))
    scores = _sigmoid(logits)
    biased = scores + bias_ref[...]
    sc = [scores[e:e + 1, :] for e in range(N_EXPERTS)]
    bi = [biased[e:e + 1, :] for e in range(N_EXPERTS)]
    group_score = []
    for g in range(N_GROUPS):
        a, b, c, d = bi[4 * g:4 * g + 4]
        m1, n1, m2, n2 = jnp.maximum(a, b), jnp.minimum(a, b), jnp.maximum(c, d), jnp.minimum(c, d)
        group_score.append(jnp.maximum(m1, m2) + jnp.maximum(jnp.minimum(m1, m2), jnp.maximum(n1, n2)))

    def first_argmax(vals):
        top = functools.reduce(jnp.maximum, vals)
        seen, hot = None, []
        for v in vals:
            h = v == top
            if seen is not None:
                h = jnp.logical_and(h, jnp.logical_not(seen))
            seen = h if seen is None else jnp.logical_or(seen, h)
            hot.append(h)
        return hot

    in_group = first_argmax(group_score)
    masked = [jnp.where(in_group[e // EXPERTS_PER_GROUP], bi[e], -jnp.inf) for e in range(N_EXPERTS)]
    hot1 = first_argmax(masked)
    hot2 = first_argmax([jnp.where(h, -jnp.inf, v) for h, v in zip(hot1, masked)])
    w1 = functools.reduce(jnp.add, [jnp.where(h, s, 0.0) for h, s in zip(hot1, sc)])
    w2 = functools.reduce(jnp.add, [jnp.where(h, s, 0.0) for h, s in zip(hot2, sc)])
    inv_tot = 1.0 / (w1 + w2)
    for e in range(N_EXPERTS):
        comb_ref[e:e + 1, :] = (jnp.where(hot1[e], w1, 0.0) + jnp.where(hot2[e], w2, 0.0)) * inv_tot


def _router_call(lay, with_ctx, x, mod, g, wh, wl, bias):
    t, tm = x.shape[0], lay.t
    mrow = lay.mod_row(with_ctx)
    return pl.pallas_call(
        _router_kernel,
        grid=(t // tm,),
        in_specs=[pl.BlockSpec((tm, D_MODEL), lambda i: (i, 0)),
                  pl.BlockSpec((1, 6, D_MODEL), lambda i: (mrow(i), 0, 0)), _full(g), _full(wh), _full(wl), _full(bias)],
        out_specs=[pl.BlockSpec((tm, D_MODEL), lambda i: (i, 0)), pl.BlockSpec((N_EXPERTS, tm), lambda i: (0, i))],
        out_shape=[jax.ShapeDtypeStruct((t, D_MODEL), BF16), jax.ShapeDtypeStruct((N_EXPERTS, t), F32)],
        compiler_params=_cparams("parallel"),
        name="moe_router",
    )(x, mod, g, wh, wl, bias)


def _moe_kernel(f_ref, comb_ref, wg_ref, wu_ref, wd_ref, x_ref, modb_ref, modc_ref, o_ref, acc_ref,
                *, ctx_rows, tiles_per_seq):
    e = pl.program_id(1)

    @pl.when(e == 0)
    def _():
        acc_ref[...] = jnp.zeros_like(acc_ref)

    f = f_ref[...]
    gate = jnp.dot(f, wg_ref[0], preferred_element_type=F32)
    up = jnp.dot(f, wu_ref[0], preferred_element_type=F32)
    act = (gate * _sigmoid(gate) * up).astype(BF16)
    down = jnp.dot(act, wd_ref[0], preferred_element_type=F32)
    comb = comb_ref[...]
    lane = lax.broadcasted_iota(jnp.int32, comb.shape, 1)
    c_e = jnp.sum(jnp.where(lane == e, comb, 0.0), axis=1, keepdims=True)
    acc_ref[...] += c_e * down

    @pl.when(e == N_EXPERTS - 1)
    def _():
        res_gate = modb_ref[0, 5:6, :]
        if ctx_rows:
            row = lax.broadcasted_iota(jnp.int32, acc_ref.shape, 0)
            first = pl.program_id(0) % tiles_per_seq == 0
            res_gate = jnp.where(jnp.logical_and(first, row < ctx_rows), modc_ref[0, 5:6, :], res_gate)
        o_ref[...] = x_ref[...] + res_gate * acc_ref[...]


def _moe_call(lay, with_ctx, f, comb, wg, wu, wd, x, mod):
    t = f.shape[0]
    seq = lay.n_tot if with_ctx else lay.n_lat
    tm = MOE_TILE if seq % MOE_TILE == 0 else math.gcd(seq, 1024)
    tps = seq // tm
    ctx_rows = lay.n_ctx if with_ctx else 0
    assert ctx_rows <= tm
    wspec = lambda a: pl.BlockSpec((1,) + a.shape[1:], lambda i, e: (e, 0, 0))
    tok = lambda w: pl.BlockSpec((tm, w), lambda i, e: (i, 0))
    return pl.pallas_call(
        functools.partial(_moe_kernel, ctx_rows=ctx_rows, tiles_per_seq=tps),
        grid=(t // tm, N_EXPERTS),
        in_specs=[tok(D_MODEL), tok(N_EXPERTS), wspec(wg), wspec(wu), wspec(wd), tok(D_MODEL),
                  pl.BlockSpec((1, 6, D_MODEL), lambda i, e: (i // tps, 0, 0)),
                  pl.BlockSpec((1, 6, D_MODEL), lambda i, e: (lay.b, 0, 0))],
        out_specs=tok(D_MODEL),
        out_shape=jax.ShapeDtypeStruct((t, D_MODEL), F32),
        scratch_shapes=[pltpu.VMEM((tm, D_MODEL), F32)],
        compiler_params=_cparams("parallel", "arbitrary"),
        name="moe_experts",
    )(f, comb, wg, wu, wd, x, mod, mod)


def _block_ones(n, group):
    i = np.arange(n) // group
    return jnp.asarray(i[:, None] == i[None, :], dtype=BF16)


def _rope_tables(n_ctx, n_lat):
    rows = n_lat // GRID_W
    row = jnp.repeat(jnp.arange(rows, dtype=F32), GRID_W)
    col = jnp.tile(jnp.arange(GRID_W, dtype=F32), rows)

    def angles(rot_dim):
        n_freq = rot_dim // 4
        inv_freq = ROPE_BASE ** (-jnp.arange(n_freq, dtype=F32) / n_freq)
        ang = jnp.concatenate([row[:, None] * inv_freq, col[:, None] * inv_freq], axis=-1)
        return jnp.cos(ang), jnp.sin(ang)

    c, s = angles(DA_DIM)
    cda = jnp.tile(jnp.concatenate([c, c], -1), (1, 2 * DA_HEADS))
    sda = jnp.tile(jnp.concatenate([-s, s], -1), (1, 2 * DA_HEADS))
    c, s = angles(MLA_ROPE)
    one = jnp.ones((n_lat, MLA_NOPE), F32)
    pad = MLA_HEAD_PAD - MLA_NOPE - MLA_ROPE
    cml = jnp.tile(jnp.concatenate([one, c, c, jnp.ones((n_lat, pad), F32)], -1), (1, MLA_HEADS))
    sml = jnp.tile(jnp.concatenate([0 * one, -s, s, jnp.zeros((n_lat, pad), F32)], -1), (1, MLA_HEADS))
    ident = lambda t, v: jnp.concatenate([jnp.full((n_ctx, MIX_W), v, F32), t], axis=0)
    return ident(cda, 1.0), ident(sda, 0.0), ident(cml, 1.0), ident(sml, 0.0)


def _pad_heads(w, n_heads, src_w, lo, hi, dst_w=MLA_HEAD_PAD):
    w = w.reshape(w.shape[0], n_heads, src_w)[:, :, lo:hi]
    w = jnp.pad(w, ((0, 0), (0, 0), (0, dst_w - (hi - lo))))
    return w.reshape(w.shape[0], n_heads * dst_w)


def _mix_weight(w_in_l):
    w = w_in_l
    kr = w[:, 1344:1360]
    z = lambda n: jnp.zeros((D_MODEL, n), w.dtype)
    kr_wide = jnp.concatenate([jnp.concatenate([z(MLA_NOPE), kr, z(MLA_HEAD_PAD - MLA_NOPE - MLA_ROPE)], 1)] * MLA_HEADS, 1)
    return jnp.concatenate([w[:, 0:1024], w[:, 1024:1216], z(64), w[:, 1216:1344], kr_wide, w[:, 1360:2384]], axis=1).astype(BF16)


def kernel(x, c, ctx, c_ctx, w_ada, b_ada, norm_mix_g, norm_ffn_g, w_in, da_qk_norm_g, da_lambda, da_subln_g, s5_lam_re, s5_lam_im, s5_log_dt, s5_b_re, s5_b_im, s5_c_re, s5_c_im, s5_d, s5_w_glu, s5_b_glu, mla_cq_norm_g, mla_ckv_norm_g, mla_w_uq, mla_w_ukv, mla_qk_norm_g, rw_mu, rw_w0, rw_w1, rw_w2, rw_a0, rw_a1, rw_a2, rw_g1, rw_g2, rw_k_k, rw_k_a, rw_r_k, rw_ln_g, rw_ln_b, w_branch, w_out, router_w, router_bias, exp_w_gate, exp_w_up, exp_w_down):
    b, n_lat, dm = x.shape
    n_ctx = ctx.shape[1]
    depth = w_ada.shape[0]
    assert dm == D_MODEL
    lay = _Layout(b, n_ctx, n_lat)
    t_all = b * lay.n_tot
    tm_big = 2 * lay.t

    g32 = _block_ones(MIX_W, DA_DIM)
    g64 = _block_ones(MIX_W, RW_DIM)
    tabs = _rope_tables(n_ctx, n_lat)
    row = lambda v: v.reshape(1, -1).astype(F32)
    bf = lambda a: a.astype(BF16)

    cc = jnp.zeros((16, dm), F32).at[:b].set(c).at[b].set(c_ctx)
    mod_all = _ada_call(cc, w_ada, b_ada)
    x_all = jnp.concatenate([ctx, x], axis=1).reshape(t_all, dm)

    wr_hi = router_w.T.astype(BF16)
    wr_lo = (router_w.T - wr_hi.astype(F32)).astype(BF16)
    r_bias = router_bias.reshape(N_EXPERTS, 1).astype(F32)

    for l in range(depth):
        need_ctx = l < depth - 1
        lambda_init = 0.8 - 0.6 * math.exp(-0.3 * l)
        mod = mod_all[l, :b + 1].reshape(b + 1, 6, dm)
        g_mix = row(norm_mix_g[l])
        da, s5a, s5b, mla, rw = _inproj_call(lay, x_all, mod, g_mix, _mix_weight(w_in[l]))

        gda = jnp.stack([jnp.tile(da_qk_norm_g[l, 0], 2 * DA_HEADS) * DA_DIM ** -0.5, jnp.tile(da_qk_norm_g[l, 1], 2 * DA_HEADS)])
        mla_pad = MLA_HEAD_PAD - MLA_NOPE - MLA_ROPE
        gml = jnp.stack([jnp.tile(jnp.pad(mla_qk_norm_g[l, 0], (0, mla_pad)), MLA_HEADS) * (MLA_NOPE + MLA_ROPE) ** -0.5,
                         jnp.tile(jnp.pad(mla_qk_norm_g[l, 1], (0, mla_pad)), MLA_HEADS)])
        wuq = bf(jnp.pad(_pad_heads(mla_w_uq[l], MLA_HEADS, MLA_NOPE + MLA_ROPE, 0, MLA_NOPE + MLA_ROPE), ((0, 64), (0, 0))))
        wuk = bf(_pad_heads(mla_w_ukv[l], MLA_HEADS, MLA_NOPE + MLA_VDIM, 0, MLA_NOPE))
        wuv = bf(_pad_heads(mla_w_ukv[l], MLA_HEADS, MLA_NOPE + MLA_VDIM, MLA_NOPE, MLA_NOPE + MLA_VDIM))
        consts = (g32, g64, gda.astype(F32), gml.astype(F32), row(jnp.pad(mla_cq_norm_g[l], (0, 64))), row(mla_ckv_norm_g[l]),
                  wuq, wuk, wuv)
        qd, kdt, vd, qm, kmt, vm = _qkprep_call(lay, da, mla, tabs, consts)

        lam32 = da_lambda[l].astype(F32)
        lmbda = (jnp.exp(jnp.sum(lam32[0] * lam32[1])) - jnp.exp(jnp.sum(lam32[2] * lam32[3])) + lambda_init).reshape(1, 1)
        subln = row(jnp.tile(da_subln_g[l], DA_HEADS) * (1.0 - lambda_init))
        ya = _attention(lay, qd, kdt, vd, (lmbda, subln, g64), True, need_ctx, "diff_attn")
        yc = _attention(lay, qm, kmt, vm, (lmbda, subln, g64), False, need_ctx, "mla_attn")

        mats = _s5_mats(s5_lam_re[l], s5_lam_im[l], s5_log_dt[l], s5_b_re[l], s5_b_im[l], s5_c_re[l], s5_c_im[l])
        ys_a, ys_b = _s5_scan(lay, s5a, s5b, mats)
        yb = _s5_glu_call(s5a, s5b, ys_a, ys_b, row(s5_d[l]), bf(s5_w_glu[l]), row(s5_b_glu[l]), tm_big)

        pre_consts = (row(rw_mu[l]), g64, row(rw_k_k[l]), row(rw_k_a[l]), row(rw_r_k[l]),
                      rw_w0[l].reshape(2, 1, MIX_W), bf(rw_w1[l]), bf(rw_w2[l]),
                      rw_a0[l].reshape(2, 1, MIX_W), bf(rw_a1[l]), bf(rw_a2[l]), bf(rw_g1[l]), bf(rw_g2[l]))
        yd = _rwkv_branch(lay, rw, pre_consts, g64, row(rw_ln_g[l]), row(rw_ln_b[l]))

        x_mid = _merge_call(lay, need_ctx, x_all, mod, g_mix, bf(w_in[l][:, 2384:]), ya, yb, yc, yd,
                            bf(w_branch[l]), bf(w_out[l]))
        f, comb_t = _router_call(lay, need_ctx, x_mid, mod, row(norm_ffn_g[l]), wr_hi, wr_lo, r_bias)
        x_all = _moe_call(lay, need_ctx, f, comb_t.T, bf(exp_w_gate[l]), bf(exp_w_up[l]), bf(exp_w_down[l]), x_mid, mod)
    return x_all.reshape(b, n_lat, dm)
```

```python
import functools
import math

import numpy as np
import jax
import jax.numpy as jnp
from jax import lax
from jax.experimental import pallas as pl
from jax.experimental.pallas import tpu as pltpu

F32 = jnp.float32
BF16 = jnp.bfloat16

D_MODEL = 1024
GRID_W = 64
ROPE_BASE = 10000.0
EPS = 1e-6
DA_HEADS, DA_DIM, DA_VDIM = 4, 32, 64
S5_GROUPS, S5_CH, S5_STATE = 16, 16, 64
MLA_HEADS, MLA_NOPE, MLA_ROPE, MLA_VDIM = 4, 32, 16, 64
MLA_Q_RANK, MLA_KV_RANK = 192, 128
MLA_HEAD_PAD = 64
RW_HEADS, RW_DIM = 4, 64
RW_LN_EPS = 64e-5
N_BRANCH = 4
N_EXPERTS, N_GROUPS, EXPERTS_PER_GROUP = 16, 4, 4
D_FF = 512
MIX_W = 256

S5_CHUNK = 8
S5_FLAT = S5_CHUNK * MIX_W
S5_STATE_W = S5_GROUPS * S5_STATE
RW_TCHUNK = 128
TOKEN_TILE = 256
MOE_TILE = 1152

_DA_W, _S5_W, _MLA_W, _RW_W = 768, 256, 640, 1024
_MIX_COLS = _DA_W + _S5_W + _MLA_W + _RW_W

V7X_VMEM_BYTES = 64 * 2**20
_VMEM_LIMIT = V7X_VMEM_BYTES - 8 * 2**20


def _cparams(*sem):
    return pltpu.CompilerParams(dimension_semantics=sem, vmem_limit_bytes=_VMEM_LIMIT)


def _full(a):
    return pl.BlockSpec(a.shape, lambda *_, nd=a.ndim: (0,) * nd)


def _split_dot(x, w, terms=2):
    acc = None
    rem = x
    for i in range(terms):
        part = rem.astype(BF16)
        d = jnp.dot(part, w, preferred_element_type=F32)
        acc = d if acc is None else acc + d
        if i + 1 < terms:
            rem = rem - part.astype(F32)
    return acc


def _modulate(x, g, shift, scale):
    xn = x * lax.rsqrt(jnp.mean(x * x, axis=-1, keepdims=True) + EPS)
    return xn * g * (1.0 + scale) + shift


def _sigmoid(x):
    return 1.0 / (1.0 + jnp.exp(-x))


def _group_rms(x, ones_bd, inv_n, gain):
    ms = _split_dot(x * x, ones_bd) * inv_n
    return x * lax.rsqrt(ms + EPS) * gain


def _lane_partner(x, half, period, first_end):
    n = x.shape[1]
    lane = lax.broadcasted_iota(jnp.int32, x.shape, 1)
    up = pltpu.roll(x, n - half, axis=1)
    down = pltpu.roll(x, half, axis=1)
    return jnp.where((lane & (period - 1)) < first_end, up, down)


def _rope(x, cos_t, sin_t, half, period, first_end):
    return x * cos_t + _lane_partner(x, half, period, first_end) * sin_t


def _ada_kernel(c_ref, w_ref, b_ref, o_ref):
    c = c_ref[...]
    s = c * _sigmoid(c)
    o_ref[0] = jnp.dot(s.astype(BF16), w_ref[0].astype(BF16), preferred_element_type=F32) + b_ref[0]


def _ada_call(cc, w_ada, b_ada):
    depth, dm, n = w_ada.shape
    tn = n // 4
    return pl.pallas_call(
        _ada_kernel,
        grid=(depth, n // tn),
        in_specs=[
            pl.BlockSpec(cc.shape, lambda l, j: (0, 0)),
            pl.BlockSpec((1, dm, tn), lambda l, j: (l, 0, j)),
            pl.BlockSpec((1, 1, tn), lambda l, j: (l, 0, j)),
        ],
        out_specs=pl.BlockSpec((1, cc.shape[0], tn), lambda l, j: (l, 0, j)),
        out_shape=jax.ShapeDtypeStruct((depth, cc.shape[0], n), F32),
        compiler_params=_cparams("parallel", "parallel"),
        name="ada_mod",
    )(cc, w_ada, b_ada.reshape(depth, 1, n))


class _Layout:
    def __init__(self, n_batch, n_ctx, n_lat):
        t = TOKEN_TILE
        assert n_ctx % t == 0 and n_lat % t == 0
        self.b, self.n_ctx, self.n_lat, self.n_tot = n_batch, n_ctx, n_lat, n_ctx + n_lat
        self.t = t
        self.ctx_tiles, self.lat_tiles, self.seq_tiles = n_ctx // t, n_lat // t, (n_ctx + n_lat) // t

    def rows(self, with_ctx):
        return self.b * (self.n_tot if with_ctx else self.n_lat)

    def n_tiles(self, with_ctx):
        return self.b * (self.seq_tiles if with_ctx else self.lat_tiles)

    def src_tile(self, with_ctx):
        if with_ctx:
            return lambda i: i
        return lambda i: (i // self.lat_tiles) * self.seq_tiles + i % self.lat_tiles + self.ctx_tiles

    def mod_row(self, with_ctx):
        if with_ctx:
            return lambda i: jnp.where(i % self.seq_tiles < self.ctx_tiles, self.b, i // self.seq_tiles)
        return lambda i: i // self.lat_tiles


def _inproj_kernel(x_ref, mod_ref, g_ref, w_ref, da_ref, s5a_ref, s5b_ref, mla_ref, rw_ref):
    h = _modulate(x_ref[...], g_ref[...], mod_ref[0, 0:1, :], mod_ref[0, 1:2, :])
    acc = jnp.dot(h.astype(BF16), w_ref[...], preferred_element_type=F32)
    da_ref[...] = acc[:, 0:_DA_W]
    s5a_ref[...] = acc[:, _DA_W:_DA_W + _S5_W // 2]
    s5b_ref[...] = acc[:, _DA_W + _S5_W // 2:_DA_W + _S5_W]
    mla_ref[...] = acc[:, _DA_W + _S5_W:_DA_W + _S5_W + _MLA_W]
    rw_ref[...] = acc[:, _DA_W + _S5_W + _MLA_W:_MIX_COLS]


def _inproj_call(lay, x_all, mod, g, w_mix):
    t, tm = x_all.shape[0], lay.t
    widths = (_DA_W, _S5_W // 2, _S5_W // 2, _MLA_W, _RW_W)
    mrow = lay.mod_row(True)
    return pl.pallas_call(
        _inproj_kernel,
        grid=(t // tm,),
        in_specs=[
            pl.BlockSpec((tm, D_MODEL), lambda i: (i, 0)),
            pl.BlockSpec((1, 6, D_MODEL), lambda i: (mrow(i), 0, 0)),
            _full(g), _full(w_mix),
        ],
        out_specs=[pl.BlockSpec((tm, w), lambda i: (i, 0)) for w in widths],
        out_shape=[jax.ShapeDtypeStruct((t, w), F32) for w in widths],
        compiler_params=_cparams("parallel"),
        name="in_proj",
    )(x_all, mod, g, w_mix)


def _qkprep_kernel(da_ref, mla_ref, cda_ref, sda_ref, cml_ref, sml_ref, g32_ref, g64_ref,
                   gda_ref, gml_ref, cqg_ref, ckvg_ref, wuq_ref, wuk_ref, wuv_ref,
                   qd_ref, kd_ref, vd_ref, qm_ref, km_ref, vm_ref):
    g32 = g32_ref[...]
    g64 = g64_ref[...]
    cda, sda = cda_ref[...], sda_ref[...]
    q = _group_rms(da_ref[:, 0:MIX_W], g32, 1.0 / DA_DIM, gda_ref[0:1, :])
    qd_ref[...] = _rope(q, cda, sda, DA_DIM // 2, DA_DIM, DA_DIM // 2).astype(BF16)
    k = _group_rms(da_ref[:, MIX_W:2 * MIX_W], g32, 1.0 / DA_DIM, gda_ref[1:2, :])
    kd_ref[0] = _rope(k, cda, sda, DA_DIM // 2, DA_DIM, DA_DIM // 2).T.astype(BF16)
    vd_ref[...] = da_ref[:, 2 * MIX_W:3 * MIX_W].astype(BF16)

    cml, sml = cml_ref[...], sml_ref[...]
    cq = mla_ref[:, 0:256]
    cqn = cq * lax.rsqrt(jnp.sum(cq * cq, axis=-1, keepdims=True) * (1.0 / MLA_Q_RANK) + EPS) * cqg_ref[...]
    q = jnp.dot(cqn.astype(BF16), wuq_ref[...], preferred_element_type=F32)
    ckv = mla_ref[:, 256:384]
    ckvn = ckv * lax.rsqrt(jnp.mean(ckv * ckv, axis=-1, keepdims=True) + EPS) * ckvg_ref[...]
    ckvb = ckvn.astype(BF16)
    k = jnp.dot(ckvb, wuk_ref[...], preferred_element_type=F32) + mla_ref[:, 384:640]
    vm_ref[...] = jnp.dot(ckvb, wuv_ref[...], preferred_element_type=F32).astype(BF16)
    inv_n = 1.0 / (MLA_NOPE + MLA_ROPE)
    half = MLA_ROPE // 2
    q = _group_rms(q, g64, inv_n, gml_ref[0:1, :])
    qm_ref[...] = _rope(q, cml, sml, half, MLA_HEAD_PAD, MLA_NOPE + half).astype(BF16)
    k = _group_rms(k, g64, inv_n, gml_ref[1:2, :])
    km_ref[0] = _rope(k, cml, sml, half, MLA_HEAD_PAD, MLA_NOPE + half).T.astype(BF16)


def _qkprep_call(lay, da, mla, tabs, consts):
    t, tm = da.shape[0], lay.t
    st = lay.seq_tiles
    row = pl.BlockSpec((tm, MIX_W), lambda i: (i, 0))
    key_t = pl.BlockSpec((1, MIX_W, tm), lambda i: (i // st, 0, i % st))
    in_specs = [pl.BlockSpec((tm, _DA_W), lambda i: (i, 0)), pl.BlockSpec((tm, _MLA_W), lambda i: (i, 0))]
    in_specs += [pl.BlockSpec((tm, MIX_W), lambda i: (i % st, 0)) for _ in tabs]
    in_specs += [_full(a) for a in consts]
    tok = jax.ShapeDtypeStruct((t, MIX_W), BF16)
    keys = jax.ShapeDtypeStruct((lay.b, MIX_W, lay.n_tot), BF16)
    return pl.pallas_call(
        _qkprep_kernel,
        grid=(t // tm,),
        in_specs=in_specs,
        out_specs=[row, key_t, row, row, key_t, row],
        out_shape=[tok, keys, tok, tok, keys, tok],
        compiler_params=_cparams("parallel"),
        name="qk_prep",
    )(da, mla, *tabs, *consts)


def _softmax_parts(q, kt):
    s = jnp.dot(q, kt, preferred_element_type=F32)
    p = jnp.exp2(s - jnp.max(s, axis=-1, keepdims=True))
    return p, 1.0 / jnp.sum(p, axis=-1, keepdims=True)


def _attn_heads(q, kt_ref, v_ref, nk, diff, lam):
    lane = lax.broadcasted_iota(jnp.int32, (q.shape[0], MIX_W), 1)
    v = v_ref[0, 0:nk, :]
    acc = jnp.zeros((q.shape[0], MIX_W), F32)
    for h in range(DA_HEADS):
        if diff:
            e0, e1 = 2 * h * DA_DIM, (2 * h + 1) * DA_DIM
            p0, r0 = _softmax_parts(q[:, e0:e0 + DA_DIM], kt_ref[0, e0:e0 + DA_DIM, 0:nk])
            p1, r1 = _softmax_parts(q[:, e1:e1 + DA_DIM], kt_ref[0, e1:e1 + DA_DIM, 0:nk])
            o = jnp.dot((p0 * r0 - p1 * (r1 * lam)).astype(BF16), v, preferred_element_type=F32)
        else:
            e0 = h * MLA_HEAD_PAD
            p, r = _softmax_parts(q[:, e0:e0 + MLA_HEAD_PAD], kt_ref[0, e0:e0 + MLA_HEAD_PAD, 0:nk])
            o = jnp.dot(p.astype(BF16), v, preferred_element_type=F32) * r
        in_head = jnp.logical_and(lane >= h * DA_VDIM, lane < (h + 1) * DA_VDIM)
        acc = jnp.where(in_head, o, acc)
    return acc


def _attn_kernel(q_ref, kt_ref, v_ref, lam_ref, gain_ref, g64_ref, o_ref, *, diff, n_ctx, n_tot, ctx_tiles):
    q = q_ref[...]
    lam = lam_ref[...]

    def run(nk):
        o = _attn_heads(q, kt_ref, v_ref, nk, diff, lam)
        if diff:
            o = _group_rms(o, g64_ref[...], 1.0 / DA_VDIM, gain_ref[...])
        o_ref[...] = o.astype(BF16)

    if ctx_tiles:
        is_ctx = pl.program_id(1) < ctx_tiles
        pl.when(is_ctx)(lambda: run(n_ctx))
        pl.when(jnp.logical_not(is_ctx))(lambda: run(n_tot))
    else:
        run(n_tot)


def _attention(lay, q, kt, v, extra, diff, with_ctx, name):
    tq = lay.t
    tiles = lay.seq_tiles if with_ctx else lay.lat_tiles
    off = 0 if with_ctx else lay.ctx_tiles
    v3 = v.reshape(lay.b, lay.n_tot, MIX_W)
    kern = functools.partial(_attn_kernel, diff=diff, n_ctx=lay.n_ctx, n_tot=lay.n_tot,
                             ctx_tiles=lay.ctx_tiles if with_ctx else 0)
    return pl.pallas_call(
        kern,
        grid=(lay.b, tiles),
        in_specs=[
            pl.BlockSpec((tq, MIX_W), lambda b, j: (b * lay.seq_tiles + j + off, 0)),
            pl.BlockSpec((1, MIX_W, lay.n_tot), lambda b, j: (b, 0, 0)),
            pl.BlockSpec((1, lay.n_tot, MIX_W), lambda b, j: (b, 0, 0)),
        ] + [_full(a) for a in extra],
        out_specs=pl.BlockSpec((tq, MIX_W), lambda b, j: (b * tiles + j, 0)),
        out_shape=jax.ShapeDtypeStruct((lay.rows(with_ctx), MIX_W), BF16),
        compiler_params=_cparams("parallel", "parallel"),
        name=name,
    )(q, kt, v3, *extra)


def _chunk_rows(ua_ref, ub_ref):
    n = ua_ref.shape[0] // S5_CHUNK
    parts = []
    for s in range(S5_CHUNK):
        rows = pl.ds(s, n, stride=S5_CHUNK)
        parts += [ua_ref[rows, :], ub_ref[rows, :]]
    return jnp.concatenate(parts, axis=1).astype(BF16)


def _s5_proj_kernel(ua_ref, ub_ref, bre_ref, bim_ref, sre_ref, sim_ref):
    u = _chunk_rows(ua_ref, ub_ref)
    sre_ref[0] = jnp.dot(u, bre_ref[0], preferred_element_type=F32)
    sim_ref[0] = jnp.dot(u, bim_ref[0], preferred_element_type=F32)


def _s5_rec_kernel(sre_ref, sim_ref, are_ref, aim_ref, hre_ref, him_ref, *, n_batch, n_chunks, ctx_chunks):
    rev = pl.program_id(0) == 1
    ar, ai = are_ref[0], aim_ref[0]
    sre, sim, hre, him = sre_ref.at[0], sim_ref.at[0], hre_ref.at[0], him_ref.at[0]

    def step(i, carry):
        hr, hi = carry
        k_rev = jnp.where(i < ctx_chunks, ctx_chunks - 1 - i, n_chunks - 1 + ctx_chunks - i)
        k = jnp.where(rev, k_rev, i)
        rows = pl.ds(k, n_batch, stride=n_chunks)
        hre[rows, :] = hr
        him[rows, :] = hi
        return ar * hr - ai * hi + sre[rows, :], ar * hi + ai * hr + sim[rows, :]

    zero = jnp.zeros((n_batch, 128), F32)
    lax.fori_loop(0, n_chunks, step, (zero, zero))


def _s5_out_kernel(ua_ref, ub_ref, hre_ref, him_ref, m_ref, cre_ref, cim_ref, ya_ref, yb_ref):
    y = jnp.dot(_chunk_rows(ua_ref, ub_ref), m_ref[0], preferred_element_type=F32)
    y = y + _split_dot(hre_ref[0], cre_ref[0]) + _split_dot(him_ref[0], cim_ref[0])
    n = y.shape[0]
    ya, yb = ya_ref.at[0], yb_ref.at[0]
    for s in range(S5_CHUNK):
        rows = pl.ds(s, n, stride=S5_CHUNK)
        ya[rows, :] = y[:, s * MIX_W:s * MIX_W + 128]
        yb[rows, :] = y[:, s * MIX_W + 128:(s + 1) * MIX_W]


def _s5_mats(lam_re, lam_im, log_dt, b_re, b_im, c_re, c_im):
    hp = lax.Precision.HIGHEST
    L, G, P, CH = S5_CHUNK, S5_GROUPS, S5_STATE, S5_CH
    lr, li = lam_re.astype(F32), lam_im.astype(F32)
    dt = jnp.exp(log_dt.astype(F32))[..., None]
    zr, zi = lr * dt, li * dt
    j = jnp.arange(L + 1, dtype=F32)[:, None, None, None]
    mag = jnp.exp(zr[None] * j)
    pw_re, pw_im = mag * jnp.cos(zi[None] * j), mag * jnp.sin(zi[None] * j)
    nr, ni = pw_re[1] - 1.0, pw_im[1]
    den = lr * lr + li * li
    cr, ci = (nr * lr + ni * li) / den, (ni * lr - nr * li) / den
    bre, bim = b_re.astype(F32), b_im.astype(F32)
    bb_re = cr[..., None] * bre - ci[..., None] * bim
    bb_im = cr[..., None] * bim + ci[..., None] * bre
    x_re = pw_re[..., None] * bb_re[None] - pw_im[..., None] * bb_im[None]
    x_im = pw_re[..., None] * bb_im[None] + pw_im[..., None] * bb_re[None]
    cre, cim = c_re.astype(F32), c_im.astype(F32)
    kern = (jnp.einsum('dgcp,jdgpe->dgjce', cre, x_re[:L], precision=hp)
            - jnp.einsum('dgcp,jdgpe->dgjce', cim, x_im[:L], precision=hp))
    eye = jnp.eye(G, dtype=F32)
    kbd = (kern.transpose(0, 2, 1, 4, 3)[:, :, :, :, None, :] * eye[None, None, :, None, :, None]).astype(BF16)
    kbd = kbd.reshape(2, L, G * CH, G * CH)
    blockdiag = lambda x: (x[:, :, :, :, None, :] * eye[None, None, :, None, :, None]).astype(BF16)
    bbd_re = blockdiag(x_re.transpose(1, 0, 2, 4, 3)).reshape(2, L + 1, G * CH, G * P)
    bbd_im = blockdiag(x_im.transpose(1, 0, 2, 4, 3)).reshape(2, L + 1, G * CH, G * P)
    ca_re = cre[:, None] * pw_re.transpose(1, 0, 2, 3)[:, :, :, None, :] - cim[:, None] * pw_im.transpose(1, 0, 2, 3)[:, :, :, None, :]
    ca_im = cre[:, None] * pw_im.transpose(1, 0, 2, 3)[:, :, :, None, :] + cim[:, None] * pw_re.transpose(1, 0, 2, 3)[:, :, :, None, :]
    cbd_re = blockdiag(ca_re.transpose(0, 1, 2, 4, 3)).reshape(2, L + 1, G * P, G * CH)
    cbd_im = blockdiag((-ca_im).transpose(0, 1, 2, 4, 3)).reshape(2, L + 1, G * P, G * CH)
    s_idx, t_idx = np.arange(L)[:, None], np.arange(L)[None, :]
    m, b_r, b_i, c_r, c_i = [], [], [], [], []
    for d in range(2):
        lag = (t_idx - s_idx) if d == 0 else (s_idx - t_idx)
        blocks = jnp.where(jnp.asarray(lag >= 0)[:, :, None, None], kbd[d][np.clip(lag, 0, L - 1)], 0)
        m.append(blocks.transpose(0, 2, 1, 3).reshape(L * G * CH, L * G * CH))
        pw = np.arange(L - 1, -1, -1) if d == 0 else np.arange(L)
        b_r.append(bbd_re[d][pw].reshape(L * G * CH, G * P))
        b_i.append(bbd_im[d][pw].reshape(L * G * CH, G * P))
        q = np.arange(1, L + 1) if d == 0 else np.arange(L, 0, -1)
        c_r.append(cbd_re[d][q].transpose(1, 0, 2).reshape(G * P, L * G * CH))
        c_i.append(cbd_im[d][q].transpose(1, 0, 2).reshape(G * P, L * G * CH))
    a_re, a_im = pw_re[L].reshape(2, 1, G * P), pw_im[L].reshape(2, 1, G * P)
    return jnp.stack(m), jnp.stack(b_r), jnp.stack(b_i), jnp.stack(c_r), jnp.stack(c_i), a_re, a_im


def _s5_scan(lay, ua, ub, mats):
    m, b_r, b_i, c_r, c_i, a_re, a_im = mats
    n_chunks = lay.n_tot // S5_CHUNK
    rows = lay.b * n_chunks
    tr = min(lay.t, rows)
    tok = tr * S5_CHUNK
    half = MIX_W // 2
    wspec = lambda a: pl.BlockSpec((1,) + a.shape[1:], lambda d, i: (d, 0, 0))
    state = jax.ShapeDtypeStruct((2, rows, S5_STATE_W), F32)
    sblk = pl.BlockSpec((1, tr, S5_STATE_W), lambda d, i: (d, i, 0))
    ublk = pl.BlockSpec((tok, half), lambda d, i: (i, 0))
    s_re, s_im = pl.pallas_call(
        _s5_proj_kernel,
        grid=(2, rows // tr),
        in_specs=[ublk, ublk, wspec(b_r), wspec(b_i)],
        out_specs=[sblk, sblk],
        out_shape=[state, state],
        compiler_params=_cparams("parallel", "parallel"),
        name="s5_proj",
    )(ua, ub, b_r, b_i)
    col = pl.BlockSpec((1, rows, 128), lambda d, j: (d, 0, j))
    acol = pl.BlockSpec((1, 1, 128), lambda d, j: (d, 0, j))
    h_re, h_im = pl.pallas_call(
        functools.partial(_s5_rec_kernel, n_batch=lay.b, n_chunks=n_chunks, ctx_chunks=lay.n_ctx // S5_CHUNK),
        grid=(2, S5_STATE_W // 128),
        in_specs=[col, col, acol, acol],
        out_specs=[col, col],
        out_shape=[state, state],
        compiler_params=_cparams("parallel", "parallel"),
        name="s5_rec",
    )(s_re, s_im, a_re, a_im)
    yblk = pl.BlockSpec((1, tok, half), lambda d, i: (d, i, 0))
    yshape = jax.ShapeDtypeStruct((2, lay.b * lay.n_tot, half), F32)
    return pl.pallas_call(
        _s5_out_kernel,
        grid=(2, rows // tr),
        in_specs=[ublk, ublk, sblk, sblk, wspec(m), wspec(c_r), wspec(c_i)],
        out_specs=[yblk, yblk],
        out_shape=[yshape, yshape],
        compiler_params=_cparams("parallel", "parallel"),
        name="s5_out",
    )(ua, ub, h_re, h_im, m, c_r, c_i)


def _s5_glu_kernel(ua_ref, ub_ref, ya_ref, yb_ref, d_ref, w_ref, b_ref, o_ref):
    u = jnp.concatenate([ua_ref[...], ub_ref[...]], axis=1)
    y = d_ref[...] * u + jnp.concatenate([ya_ref[0] + ya_ref[1], yb_ref[0] + yb_ref[1]], axis=1)
    z = 0.5 * y * (1.0 + jnp.tanh(math.sqrt(2.0 / math.pi) * (y + 0.044715 * (y * y * y))))
    gate = _sigmoid(jnp.dot(z.astype(BF16), w_ref[...], preferred_element_type=F32) + b_ref[...])
    o_ref[...] = (z * gate).astype(BF16)


def _s5_glu_call(ua, ub, ya, yb, d, w, bias, tm):
    t, half = ua.shape
    urow = pl.BlockSpec((tm, half), lambda i: (i, 0))
    yrow = pl.BlockSpec((2, tm, half), lambda i: (0, i, 0))
    return pl.pallas_call(
        _s5_glu_kernel,
        grid=(t // tm,),
        in_specs=[urow, urow, yrow, yrow, _full(d), _full(w), _full(bias)],
        out_specs=pl.BlockSpec((tm, MIX_W), lambda i: (i, 0)),
        out_shape=jax.ShapeDtypeStruct((t, MIX_W), BF16),
        compiler_params=_cparams("parallel"),
        name="s5_glu",
    )(ua, ub, ya, yb, d, w, bias)


def _rw_pre_kernel(x_ref, prev_ref, next_ref, mu_ref, g64_ref, kk_g_ref, ka_ref, rk_ref,
                   w0_ref, w1_ref, w2_ref, a0_ref, a1_ref, a2_ref, g1_ref, g2_ref,
                   r_ref, va_ref, vb_ref, kk_ref, wd_ref, kka_ref, km_ref, bon_ref, gate_ref,
                   *, seq_tiles, ctx_tiles):
    x = x_ref[...]
    n = x.shape[0]
    j = pl.program_id(0) % seq_tiles
    starts = jnp.logical_or(j == 0, j == ctx_tiles)
    ends = jnp.logical_or(j == ctx_tiles - 1, j == seq_tiles - 1)
    prev_row = jnp.where(starts, 0.0, prev_ref[0, 7:8, :])
    next_row = jnp.where(ends, 0.0, next_ref[0, 0:1, :])
    row = lax.broadcasted_iota(jnp.int32, x.shape, 0)
    left = jnp.where(row == 0, prev_row, pltpu.roll(x, 1, axis=0))
    right = jnp.where(row == n - 1, next_row, pltpu.roll(x, n - 1, axis=0))
    x = x + (0.5 * (left + right) - x) * mu_ref[...]
    r, k, v, xd = (x[:, i * MIX_W:(i + 1) * MIX_W] for i in range(4))
    g64 = g64_ref[...]
    kscaled = k * kk_g_ref[...]
    kk = kscaled / jnp.maximum(jnp.sqrt(_split_dot(kscaled * kscaled, g64)), 1e-12)
    xdb = xd.astype(BF16)
    r_ref[...] = r
    kk_ref[...] = kk
    va = v.astype(BF16).astype(F32)
    va_ref[...] = va
    vb_ref[...] = _lane_partner((v - va).astype(BF16).astype(F32), RW_DIM // 2, RW_DIM, RW_DIM // 2)
    km_sum = None
    for d in range(2):
        lo = jnp.tanh(jnp.dot(xdb, w1_ref[d], preferred_element_type=F32))
        w_raw = w0_ref[d] + jnp.dot(lo.astype(BF16), w2_ref[d], preferred_element_type=F32)
        wd_ref[d] = jnp.exp(-_sigmoid(w_raw) * math.exp(-0.5))
        ar = jnp.dot(xdb, a1_ref[d], preferred_element_type=F32)
        a = _sigmoid(a0_ref[d] + jnp.dot(ar.astype(BF16), a2_ref[d], preferred_element_type=F32))
        km = k * (1.0 + (a - 1.0) * ka_ref[...])
        kka_ref[d] = kk * a
        km_ref[d] = km
        km_sum = km if km_sum is None else km_sum + km
    bon_ref[...] = _split_dot(r * km_sum * rk_ref[...], g64) * v
    gr = _sigmoid(jnp.dot(xdb, g1_ref[...], preferred_element_type=F32))
    gate_ref[...] = jnp.dot(gr.astype(BF16), g2_ref[...], preferred_element_type=F32)


def _rw_pre_call(lay, rw, consts):
    t, tr = rw.shape[0], lay.t
    nt = t // tr
    g8 = tr // 8
    rw8 = rw.reshape(t // 8, 8, _RW_W)
    row = pl.BlockSpec((tr, MIX_W), lambda i: (i, 0))
    row2 = pl.BlockSpec((2, tr, MIX_W), lambda i: (0, i, 0))
    sd = jax.ShapeDtypeStruct((t, MIX_W), F32)
    sd2 = jax.ShapeDtypeStruct((2, t, MIX_W), F32)
    return pl.pallas_call(
        functools.partial(_rw_pre_kernel, seq_tiles=lay.seq_tiles, ctx_tiles=lay.ctx_tiles),
        grid=(nt,),
        in_specs=[pl.BlockSpec((tr, _RW_W), lambda i: (i, 0)),
                  pl.BlockSpec((1, 8, _RW_W), lambda i: (jnp.maximum(i * g8 - 1, 0), 0, 0)),
                  pl.BlockSpec((1, 8, _RW_W), lambda i: (jnp.minimum((i + 1) * g8, t // 8 - 1), 0, 0))]
                 + [_full(a) for a in consts],
        out_specs=[row, row, row, row, row2, row2, row2, row, row],
        out_shape=[sd, sd, sd, sd, sd2, sd2, sd2, sd, sd],
        compiler_params=_cparams("parallel"),
        name="rwkv_pre",
    )(rw, rw8, rw8, *consts)


def _rw_scan_kernel(rf, vaf, vbf, kkf, rr, var, vbr, kkr, wf, wr, kaf, kar, kmf, kmr, g_ref, ea_ref, eb_ref,
                    yf_ref, yr_ref, s_scr, *, n_batch, n_steps):
    @pl.when(pl.program_id(0) == 0)
    def _():
        s_scr[...] = jnp.zeros_like(s_scr)

    g = g_ref[...]
    ea, eb = ea_ref[...], eb_ref[...]
    ea16, eb16 = ea.astype(BF16), eb.astype(BF16)
    m = 2 * n_batch * RW_DIM
    block_sum = lambda x: jnp.dot(x.reshape(m, MIX_W), g, preferred_element_type=F32)

    def step(t, carry):
        tr = n_steps - 1 - t
        both = lambda a, b: jnp.concatenate([a[:, pl.ds(t, 1), :], b[:, pl.ds(tr, 1), :]], axis=0)
        both_d = lambda a, b: jnp.concatenate([a[0, :, pl.ds(t, 1), :], b[0, :, pl.ds(tr, 1), :]], axis=0)
        kk, r, va, vb = both(kkf, kkr), both(rf, rr), both(vaf, var), both(vbf, vbr)
        w, ka, km = both_d(wf, wr), both_d(kaf, kar), both_d(kmf, kmr)
        s = s_scr[...]
        sa = block_sum((s * kk).astype(BF16)).reshape(s.shape)
        vt = block_sum(ea16 * va.astype(BF16) + eb16 * vb.astype(BF16)).reshape(s.shape)
        s = s * w - sa * ka + vt * km
        s_scr[...] = s
        yb = block_sum((s * r).astype(BF16)).reshape(s.shape)
        y = jnp.sum(yb * ea, axis=1, keepdims=True)
        yf_ref[:, pl.ds(t, 1), :] = y[:n_batch]
        yr_ref[:, pl.ds(tr, 1), :] = y[n_batch:]
        return carry

    lax.fori_loop(0, n_steps, step, 0, unroll=2)


def _rwkv_branch(lay, rw, pre_consts, g64, lng, lnb):
    r_, va_, vb_, kk_, wd_, kka_, km_, bon, gate = _rw_pre_call(lay, rw, pre_consts)
    lane = np.arange(MIX_W) % RW_DIM
    vrow = np.arange(RW_DIM)[:, None]
    eye_a = jnp.asarray(lane[None, :] == vrow, F32)
    eye_b = jnp.asarray(lane[None, :] == (vrow + RW_DIM // 2) % RW_DIM, F32)
    consts = (g64, eye_a, eye_b)
    yf, yr = _rw_scan_call(lay, (r_, va_, vb_, kk_), (wd_, kka_, km_), consts)
    return _rw_fin_call(yf, yr, bon, gate, g64, lng, lnb, 2 * lay.t)


def _rw_scan_call(lay, shared, perdir, consts):
    b, n_tot, tc = lay.b, lay.n_tot, RW_TCHUNK
    assert lay.n_ctx % tc == 0 and lay.n_lat % tc == 0
    nch, cch = n_tot // tc, lay.n_ctx // tc
    rev_chunk = lambda i: jnp.where(i < cch, cch - 1 - i, nch - 1 + cch - i)
    blk_f = pl.BlockSpec((b, tc, MIX_W), lambda i: (0, i, 0))
    blk_r = pl.BlockSpec((b, tc, MIX_W), lambda i: (0, rev_chunk(i), 0))
    blk_f4 = pl.BlockSpec((1, b, tc, MIX_W), lambda i: (0, 0, i, 0))
    blk_r4 = pl.BlockSpec((1, b, tc, MIX_W), lambda i: (1, 0, rev_chunk(i), 0))
    sh = [a.reshape(b, n_tot, MIX_W) for a in shared]
    pd = [a.reshape(2, b, n_tot, MIX_W) for a in perdir]
    yf, yr = pl.pallas_call(
        functools.partial(_rw_scan_kernel, n_batch=b, n_steps=tc),
        grid=(nch,),
        in_specs=[blk_f] * 4 + [blk_r] * 4 + [blk_f4, blk_r4] * 3 + [_full(a) for a in consts],
        out_specs=[blk_f, blk_r],
        out_shape=[jax.ShapeDtypeStruct((b, n_tot, MIX_W), F32)] * 2,
        scratch_shapes=[pltpu.VMEM((2 * b, RW_DIM, MIX_W), F32)],
        compiler_params=_cparams("arbitrary"),
        name="rwkv_scan",
    )(*sh, *sh, pd[0], pd[0], pd[1], pd[1], pd[2], pd[2], *consts)
    return yf.reshape(b * n_tot, MIX_W), yr.reshape(b * n_tot, MIX_W)


def _rw_fin_kernel(yf_ref, yr_ref, bon_ref, gate_ref, g64_ref, lng_ref, lnb_ref, o_ref):
    y = yf_ref[...] + yr_ref[...]
    g64 = g64_ref[...]
    mean = _split_dot(y, g64) * (1.0 / RW_DIM)
    c = y - mean
    var = _split_dot(c * c, g64) * (1.0 / RW_DIM)
    out = c * lax.rsqrt(var + RW_LN_EPS) * lng_ref[...] + lnb_ref[...] + bon_ref[...]
    o_ref[...] = (out * gate_ref[...]).astype(BF16)


def _rw_fin_call(yf, yr, bon, gate, g64, lng, lnb, tm):
    t = yf.shape[0]
    row = pl.BlockSpec((tm, MIX_W), lambda i: (i, 0))
    return pl.pallas_call(
        _rw_fin_kernel,
        grid=(t // tm,),
        in_specs=[row, row, row, row, _full(g64), _full(lng), _full(lnb)],
        out_specs=row,
        out_shape=jax.ShapeDtypeStruct((t, MIX_W), BF16),
        compiler_params=_cparams("parallel"),
        name="rwkv_finish",
    )(yf, yr, bon, gate, g64, lng, lnb)


def _merge_kernel(x_ref, mod_ref, g_ref, wg_ref, ya_ref, yb_ref, yc_ref, yd_ref, wb_ref, wo_ref, o_ref):
    x = x_ref[...]
    h = _modulate(x, g_ref[...], mod_ref[0, 0:1, :], mod_ref[0, 1:2, :]).astype(BF16)
    merged = None
    for i, y_ref in enumerate((ya_ref, yb_ref, yc_ref, yd_ref)):
        gate = _sigmoid(jnp.dot(h, wg_ref[:, i * D_MODEL:(i + 1) * D_MODEL], preferred_element_type=F32))
        term = gate * jnp.dot(y_ref[...], wb_ref[i], preferred_element_type=F32)
        merged = term if merged is None else merged + term
    out = jnp.dot(merged.astype(BF16), wo_ref[...], preferred_element_type=F32)
    o_ref[...] = x + mod_ref[0, 2:3, :] * out


def _merge_call(lay, with_ctx, x_all, mod, g, w_gate, ya, yb, yc, yd, w_branch, w_out):
    tm = lay.t
    src, mrow = lay.src_tile(with_ctx), lay.mod_row(with_ctx)
    full_row = lambda w: pl.BlockSpec((tm, w), lambda i: (src(i), 0))
    out_row = lambda w: pl.BlockSpec((tm, w), lambda i: (i, 0))
    return pl.pallas_call(
        _merge_kernel,
        grid=(lay.n_tiles(with_ctx),),
        in_specs=[full_row(D_MODEL), pl.BlockSpec((1, 6, D_MODEL), lambda i: (mrow(i), 0, 0)), _full(g), _full(w_gate),
                  out_row(MIX_W), full_row(MIX_W), out_row(MIX_W), full_row(MIX_W), _full(w_branch), _full(w_out)],
        out_specs=out_row(D_MODEL),
        out_shape=jax.ShapeDtypeStruct((lay.rows(with_ctx), D_MODEL), F32),
        compiler_params=_cparams("parallel"),
        name="merge_out",
    )(x_all, mod, g, w_gate, ya, yb, yc, yd, w_branch, w_out)


def _router_kernel(x_ref, mod_ref, g_ref, wh_ref, wl_ref, bias_ref, f_ref, comb_ref):
    f = _modulate(x_ref[...], g_ref[...], mod_ref[0, 3:4, :], mod_ref[0, 4:5, :])
    fh = f.astype(BF16)
    f_ref[...] = fh
    fl = (f - fh.astype(F32)).astype(BF16)
    nt = (((1,), (1,)), ((), ()))
    wh, wl = wh_ref[...], wl_ref[...]
    logits = (lax.dot_general(wh, fh, nt, preferred_element_type=F32)
              + lax.dot_general(wh, fl, nt, preferred_element_type=F32)
              + lax.dot_general(wl, fh, nt, preferred_element_type=F32))
    scores = _sigmoid(logits)
    biased = scores + bias_ref[...]
    sc = [scores[e:e + 1, :] for e in range(N_EXPERTS)]
    bi = [biased[e:e + 1, :] for e in range(N_EXPERTS)]
    group_score = []
    for g in range(N_GROUPS):
        a, b, c, d = bi[4 * g:4 * g + 4]
        m1, n1, m2, n2 = jnp.maximum(a, b), jnp.minimum(a, b), jnp.maximum(c, d), jnp.minimum(c, d)
        group_score.append(jnp.maximum(m1, m2) + jnp.maximum(jnp.minimum(m1, m2), jnp.maximum(n1, n2)))

    def first_argmax(vals):
        top = functools.reduce(jnp.maximum, vals)
        seen, hot = None, []
        for v in vals:
            h = v == top
            if seen is not None:
                h = jnp.logical_and(h, jnp.logical_not(seen))
            seen = h if seen is None else jnp.logical_or(seen, h)
            hot.append(h)
        return hot

    in_group = first_argmax(group_score)
    masked = [jnp.where(in_group[e // EXPERTS_PER_GROUP], bi[e], -jnp.inf) for e in range(N_EXPERTS)]
    hot1 = first_argmax(masked)
    hot2 = first_argmax([jnp.where(h, -jnp.inf, v) for h, v in zip(hot1, masked)])
    w1 = functools.reduce(jnp.add, [jnp.where(h, s, 0.0) for h, s in zip(hot1, sc)])
    w2 = functools.reduce(jnp.add, [jnp.where(h, s, 0.0) for h, s in zip(hot2, sc)])
    inv_tot = 1.0 / (w1 + w2)
    for e in range(N_EXPERTS):
        comb_ref[e:e + 1, :] = (jnp.where(hot1[e], w1, 0.0) + jnp.where(hot2[e], w2, 0.0)) * inv_tot


def _router_call(lay, with_ctx, x, mod, g, wh, wl, bias):
    t, tm = x.shape[0], lay.t
    mrow = lay.mod_row(with_ctx)
    return pl.pallas_call(
        _router_kernel,
        grid=(t // tm,),
        in_specs=[pl.BlockSpec((tm, D_MODEL), lambda i: (i, 0)),
                  pl.BlockSpec((1, 6, D_MODEL), lambda i: (mrow(i), 0, 0)), _full(g), _full(wh), _full(wl), _full(bias)],
        out_specs=[pl.BlockSpec((tm, D_MODEL), lambda i: (i, 0)), pl.BlockSpec((N_EXPERTS, tm), lambda i: (0, i))],
        out_shape=[jax.ShapeDtypeStruct((t, D_MODEL), BF16), jax.ShapeDtypeStruct((N_EXPERTS, t), F32)],
        compiler_params=_cparams("parallel"),
        name="moe_router",
    )(x, mod, g, wh, wl, bias)


def _moe_kernel(f_ref, comb_ref, wg_ref, wu_ref, wd_ref, x_ref, modb_ref, modc_ref, o_ref, acc_ref,
                *, ctx_rows, tiles_per_seq):
    e = pl.program_id(1)

    @pl.when(e == 0)
    def _():
        acc_ref[...] = jnp.zeros_like(acc_ref)

    f = f_ref[...]
    gate = jnp.dot(f, wg_ref[0], preferred_element_type=F32)
    up = jnp.dot(f, wu_ref[0], preferred_element_type=F32)
    act = (gate * _sigmoid(gate) * up).astype(BF16)
    down = jnp.dot(act, wd_ref[0], preferred_element_type=F32)
    comb = comb_ref[...]
    lane = lax.broadcasted_iota(jnp.int32, comb.shape, 1)
    c_e = jnp.sum(jnp.where(lane == e, comb, 0.0), axis=1, keepdims=True)
    acc_ref[...] += c_e * down

    @pl.when(e == N_EXPERTS - 1)
    def _():
        res_gate = modb_ref[0, 5:6, :]
        if ctx_rows:
            row = lax.broadcasted_iota(jnp.int32, acc_ref.shape, 0)
            first = pl.program_id(0) % tiles_per_seq == 0
            res_gate = jnp.where(jnp.logical_and(first, row < ctx_rows), modc_ref[0, 5:6, :], res_gate)
        o_ref[...] = x_ref[...] + res_gate * acc_ref[...]


def _moe_call(lay, with_ctx, f, comb, wg, wu, wd, x, mod):
    t = f.shape[0]
    seq = lay.n_tot if with_ctx else lay.n_lat
    tm = MOE_TILE if seq % MOE_TILE == 0 else math.gcd(seq, 1024)
    tps = seq // tm
    ctx_rows = lay.n_ctx if with_ctx else 0
    assert ctx_rows <= tm
    wspec = lambda a: pl.BlockSpec((1,) + a.shape[1:], lambda i, e: (e, 0, 0))
    tok = lambda w: pl.BlockSpec((tm, w), lambda i, e: (i, 0))
    return pl.pallas_call(
        functools.partial(_moe_kernel, ctx_rows=ctx_rows, tiles_per_seq=tps),
        grid=(t // tm, N_EXPERTS),
        in_specs=[tok(D_MODEL), tok(N_EXPERTS), wspec(wg), wspec(wu), wspec(wd), tok(D_MODEL),
                  pl.BlockSpec((1, 6, D_MODEL), lambda i, e: (i // tps, 0, 0)),
                  pl.BlockSpec((1, 6, D_MODEL), lambda i, e: (lay.b, 0, 0))],
        out_specs=tok(D_MODEL),
        out_shape=jax.ShapeDtypeStruct((t, D_MODEL), F32),
        scratch_shapes=[pltpu.VMEM((tm, D_MODEL), F32)],
        compiler_params=_cparams("parallel", "arbitrary"),
        name="moe_experts",
    )(f, comb, wg, wu, wd, x, mod, mod)


def _block_ones(n, group):
    i = np.arange(n) // group
    return jnp.asarray(i[:, None] == i[None, :], dtype=BF16)


def _rope_tables(n_ctx, n_lat):
    rows = n_lat // GRID_W
    row = jnp.repeat(jnp.arange(rows, dtype=F32), GRID_W)
    col = jnp.tile(jnp.arange(GRID_W, dtype=F32), rows)

    def angles(rot_dim):
        n_freq = rot_dim // 4
        inv_freq = ROPE_BASE ** (-jnp.arange(n_freq, dtype=F32) / n_freq)
        ang = jnp.concatenate([row[:, None] * inv_freq, col[:, None] * inv_freq], axis=-1)
        return jnp.cos(ang), jnp.sin(ang)

    c, s = angles(DA_DIM)
    cda = jnp.tile(jnp.concatenate([c, c], -1), (1, 2 * DA_HEADS))
    sda = jnp.tile(jnp.concatenate([-s, s], -1), (1, 2 * DA_HEADS))
    c, s = angles(MLA_ROPE)
    one = jnp.ones((n_lat, MLA_NOPE), F32)
    pad = MLA_HEAD_PAD - MLA_NOPE - MLA_ROPE
    cml = jnp.tile(jnp.concatenate([one, c, c, jnp.ones((n_lat, pad), F32)], -1), (1, MLA_HEADS))
    sml = jnp.tile(jnp.concatenate([0 * one, -s, s, jnp.zeros((n_lat, pad), F32)], -1), (1, MLA_HEADS))
    ident = lambda t, v: jnp.concatenate([jnp.full((n_ctx, MIX_W), v, F32), t], axis=0)
    return ident(cda, 1.0), ident(sda, 0.0), ident(cml, 1.0), ident(sml, 0.0)


def _pad_heads(w, n_heads, src_w, lo, hi, dst_w=MLA_HEAD_PAD):
    w = w.reshape(w.shape[0], n_heads, src_w)[:, :, lo:hi]
    w = jnp.pad(w, ((0, 0), (0, 0), (0, dst_w - (hi - lo))))
    return w.reshape(w.shape[0], n_heads * dst_w)


def _mix_weight(w_in_l):
    w = w_in_l
    kr = w[:, 1344:1360]
    z = lambda n: jnp.zeros((D_MODEL, n), w.dtype)
    kr_wide = jnp.concatenate([jnp.concatenate([z(MLA_NOPE), kr, z(MLA_HEAD_PAD - MLA_NOPE - MLA_ROPE)], 1)] * MLA_HEADS, 1)
    return jnp.concatenate([w[:, 0:1024], w[:, 1024:1216], z(64), w[:, 1216:1344], kr_wide, w[:, 1360:2384]], axis=1).astype(BF16)


def kernel(x, c, ctx, c_ctx, w_ada, b_ada, norm_mix_g, norm_ffn_g, w_in, da_qk_norm_g, da_lambda, da_subln_g, s5_lam_re, s5_lam_im, s5_log_dt, s5_b_re, s5_b_im, s5_c_re, s5_c_im, s5_d, s5_w_glu, s5_b_glu, mla_cq_norm_g, mla_ckv_norm_g, mla_w_uq, mla_w_ukv, mla_qk_norm_g, rw_mu, rw_w0, rw_w1, rw_w2, rw_a0, rw_a1, rw_a2, rw_g1, rw_g2, rw_k_k, rw_k_a, rw_r_k, rw_ln_g, rw_ln_b, w_branch, w_out, router_w, router_bias, exp_w_gate, exp_w_up, exp_w_down):
    b, n_lat, dm = x.shape
    n_ctx = ctx.shape[1]
    depth = w_ada.shape[0]
    assert dm == D_MODEL
    lay = _Layout(b, n_ctx, n_lat)
    t_all = b * lay.n_tot
    tm_big = 2 * lay.t

    g32 = _block_ones(MIX_W, DA_DIM)
    g64 = _block_ones(MIX_W, RW_DIM)
    tabs = _rope_tables(n_ctx, n_lat)
    row = lambda v: v.reshape(1, -1).astype(F32)
    bf = lambda a: a.astype(BF16)

    cc = jnp.zeros((16, dm), F32).at[:b].set(c).at[b].set(c_ctx)
    mod_all = _ada_call(cc, w_ada, b_ada)
    x_all = jnp.concatenate([ctx, x], axis=1).reshape(t_all, dm)

    wr_hi = router_w.T.astype(BF16)
    wr_lo = (router_w.T - wr_hi.astype(F32)).astype(BF16)
    r_bias = router_bias.reshape(N_EXPERTS, 1).astype(F32)

    for l in range(depth):
        need_ctx = l < depth - 1
        lambda_init = 0.8 - 0.6 * math.exp(-0.3 * l)
        mod = mod_all[l, :b + 1].reshape(b + 1, 6, dm)
        g_mix = row(norm_mix_g[l])
        da, s5a, s5b, mla, rw = _inproj_call(lay, x_all, mod, g_mix, _mix_weight(w_in[l]))

        log2e = math.log2(math.e)
        gda = jnp.stack([jnp.tile(da_qk_norm_g[l, 0], 2 * DA_HEADS) * (DA_DIM ** -0.5 * log2e), jnp.tile(da_qk_norm_g[l, 1], 2 * DA_HEADS)])
        mla_pad = MLA_HEAD_PAD - MLA_NOPE - MLA_ROPE
        gml = jnp.stack([jnp.tile(jnp.pad(mla_qk_norm_g[l, 0], (0, mla_pad)), MLA_HEADS) * ((MLA_NOPE + MLA_ROPE) ** -0.5 * log2e),
                         jnp.tile(jnp.pad(mla_qk_norm_g[l, 1], (0, mla_pad)), MLA_HEADS)])
        wuq = bf(jnp.pad(_pad_heads(mla_w_uq[l], MLA_HEADS, MLA_NOPE + MLA_ROPE, 0, MLA_NOPE + MLA_ROPE), ((0, 64), (0, 0))))
        wuk = bf(_pad_heads(mla_w_ukv[l], MLA_HEADS, MLA_NOPE + MLA_VDIM, 0, MLA_NOPE))
        wuv = bf(_pad_heads(mla_w_ukv[l], MLA_HEADS, MLA_NOPE + MLA_VDIM, MLA_NOPE, MLA_NOPE + MLA_VDIM))
        consts = (g32, g64, gda.astype(F32), gml.astype(F32), row(jnp.pad(mla_cq_norm_g[l], (0, 64))), row(mla_ckv_norm_g[l]),
                  wuq, wuk, wuv)
        qd, kdt, vd, qm, kmt, vm = _qkprep_call(lay, da, mla, tabs, consts)

        lam32 = da_lambda[l].astype(F32)
        lmbda = (jnp.exp(jnp.sum(lam32[0] * lam32[1])) - jnp.exp(jnp.sum(lam32[2] * lam32[3])) + lambda_init).reshape(1, 1)
        subln = row(jnp.tile(da_subln_g[l], DA_HEADS) * (1.0 - lambda_init))
        ya = _attention(lay, qd, kdt, vd, (lmbda, subln, g64), True, need_ctx, "diff_attn")
        yc = _attention(lay, qm, kmt, vm, (lmbda, subln, g64), False, need_ctx, "mla_attn")

        mats = _s5_mats(s5_lam_re[l], s5_lam_im[l], s5_log_dt[l], s5_b_re[l], s5_b_im[l], s5_c_re[l], s5_c_im[l])
        ys_a, ys_b = _s5_scan(lay, s5a, s5b, mats)
        yb = _s5_glu_call(s5a, s5b, ys_a, ys_b, row(s5_d[l]), bf(s5_w_glu[l]), row(s5_b_glu[l]), tm_big)

        pre_consts = (row(rw_mu[l]), g64, row(rw_k_k[l]), row(rw_k_a[l]), row(rw_r_k[l]),
                      rw_w0[l].reshape(2, 1, MIX_W), bf(rw_w1[l]), bf(rw_w2[l]),
                      rw_a0[l].reshape(2, 1, MIX_W), bf(rw_a1[l]), bf(rw_a2[l]), bf(rw_g1[l]), bf(rw_g2[l]))
        yd = _rwkv_branch(lay, rw, pre_consts, g64, row(rw_ln_g[l]), row(rw_ln_b[l]))

        x_mid = _merge_call(lay, need_ctx, x_all, mod, g_mix, bf(w_in[l][:, 2384:]), ya, yb, yc, yd,
                            bf(w_branch[l]), bf(w_out[l]))
        f, comb_t = _router_call(lay, need_ctx, x_mid, mod, row(norm_ffn_g[l]), wr_hi, wr_lo, r_bias)
        x_all = _moe_call(lay, need_ctx, f, comb_t.T, bf(exp_w_gate[l]), bf(exp_w_up[l]), bf(exp_w_down[l]), x_mid, mod)
    return x_all.reshape(b, n_lat, dm)
```

```python
import functools
import math

import numpy as np
import jax
import jax.numpy as jnp
from jax import lax
from jax.experimental import pallas as pl
from jax.experimental.pallas import tpu as pltpu

F32 = jnp.float32
BF16 = jnp.bfloat16

D_MODEL = 1024
GRID_W = 64
ROPE_BASE = 10000.0
EPS = 1e-6
DA_HEADS, DA_DIM, DA_VDIM = 4, 32, 64
S5_GROUPS, S5_CH, S5_STATE = 16, 16, 64
MLA_HEADS, MLA_NOPE, MLA_ROPE, MLA_VDIM = 4, 32, 16, 64
MLA_Q_RANK, MLA_KV_RANK = 192, 128
MLA_HEAD_PAD = 64
RW_HEADS, RW_DIM = 4, 64
RW_LN_EPS = 64e-5
N_BRANCH = 4
N_EXPERTS, N_GROUPS, EXPERTS_PER_GROUP = 16, 4, 4
D_FF = 512
MIX_W = 256

S5_CHUNK = 8
S5_FLAT = S5_CHUNK * MIX_W
S5_STATE_W = S5_GROUPS * S5_STATE
RW_CHUNK = 16
RW_TILE = 128
_NT = (((1,), (1,)), ((), ()))
TOKEN_TILE = 256
MOE_TILE = 1152

_DA_W, _S5_W, _MLA_W, _RW_W = 768, 256, 640, 1024
_MIX_COLS = _DA_W + _S5_W + _MLA_W + _RW_W

V7X_VMEM_BYTES = 64 * 2**20
_VMEM_LIMIT = V7X_VMEM_BYTES - 8 * 2**20


def _cparams(*sem):
    return pltpu.CompilerParams(dimension_semantics=sem, vmem_limit_bytes=_VMEM_LIMIT)


def _full(a):
    return pl.BlockSpec(a.shape, lambda *_, nd=a.ndim: (0,) * nd)


def _split_dot(x, w, terms=2):
    acc = None
    rem = x
    for i in range(terms):
        part = rem.astype(BF16)
        d = jnp.dot(part, w, preferred_element_type=F32)
        acc = d if acc is None else acc + d
        if i + 1 < terms:
            rem = rem - part.astype(F32)
    return acc


def _modulate(x, g, shift, scale):
    xn = x * lax.rsqrt(jnp.mean(x * x, axis=-1, keepdims=True) + EPS)
    return xn * g * (1.0 + scale) + shift


def _sigmoid(x):
    return 1.0 / (1.0 + jnp.exp(-x))


def _group_rms(x, ones_bd, inv_n, gain):
    ms = _split_dot(x * x, ones_bd) * inv_n
    return x * lax.rsqrt(ms + EPS) * gain


def _lane_partner(x, half, period, first_end):
    n = x.shape[1]
    lane = lax.broadcasted_iota(jnp.int32, x.shape, 1)
    up = pltpu.roll(x, n - half, axis=1)
    down = pltpu.roll(x, half, axis=1)
    return jnp.where((lane & (period - 1)) < first_end, up, down)


def _rope(x, cos_t, sin_t, half, period, first_end):
    return x * cos_t + _lane_partner(x, half, period, first_end) * sin_t


def _ada_kernel(c_ref, w_ref, b_ref, o_ref):
    c = c_ref[...]
    s = c * _sigmoid(c)
    o_ref[0] = jnp.dot(s.astype(BF16), w_ref[0].astype(BF16), preferred_element_type=F32) + b_ref[0]


def _ada_call(cc, w_ada, b_ada):
    depth, dm, n = w_ada.shape
    tn = n // 4
    return pl.pallas_call(
        _ada_kernel,
        grid=(depth, n // tn),
        in_specs=[
            pl.BlockSpec(cc.shape, lambda l, j: (0, 0)),
            pl.BlockSpec((1, dm, tn), lambda l, j: (l, 0, j)),
            pl.BlockSpec((1, 1, tn), lambda l, j: (l, 0, j)),
        ],
        out_specs=pl.BlockSpec((1, cc.shape[0], tn), lambda l, j: (l, 0, j)),
        out_shape=jax.ShapeDtypeStruct((depth, cc.shape[0], n), F32),
        compiler_params=_cparams("parallel", "parallel"),
        name="ada_mod",
    )(cc, w_ada, b_ada.reshape(depth, 1, n))


class _Layout:
    def __init__(self, n_batch, n_ctx, n_lat):
        t = TOKEN_TILE
        assert n_ctx % t == 0 and n_lat % t == 0
        self.b, self.n_ctx, self.n_lat, self.n_tot = n_batch, n_ctx, n_lat, n_ctx + n_lat
        self.t = t
        self.ctx_tiles, self.lat_tiles, self.seq_tiles = n_ctx // t, n_lat // t, (n_ctx + n_lat) // t

    def rows(self, with_ctx):
        return self.b * (self.n_tot if with_ctx else self.n_lat)

    def n_tiles(self, with_ctx):
        return self.b * (self.seq_tiles if with_ctx else self.lat_tiles)

    def src_tile(self, with_ctx):
        if with_ctx:
            return lambda i: i
        return lambda i: (i // self.lat_tiles) * self.seq_tiles + i % self.lat_tiles + self.ctx_tiles

    def mod_row(self, with_ctx):
        if with_ctx:
            return lambda i: jnp.where(i % self.seq_tiles < self.ctx_tiles, self.b, i // self.seq_tiles)
        return lambda i: i // self.lat_tiles


def _inproj_kernel(x_ref, mod_ref, g_ref, w_ref, da_ref, s5a_ref, s5b_ref, mla_ref, rw_ref):
    h = _modulate(x_ref[...], g_ref[...], mod_ref[0, 0:1, :], mod_ref[0, 1:2, :])
    acc = jnp.dot(h.astype(BF16), w_ref[...], preferred_element_type=F32)
    da_ref[...] = acc[:, 0:_DA_W]
    s5a_ref[...] = acc[:, _DA_W:_DA_W + _S5_W // 2]
    s5b_ref[...] = acc[:, _DA_W + _S5_W // 2:_DA_W + _S5_W]
    mla_ref[...] = acc[:, _DA_W + _S5_W:_DA_W + _S5_W + _MLA_W]
    rw_ref[...] = acc[:, _DA_W + _S5_W + _MLA_W:_MIX_COLS]


def _inproj_call(lay, x_all, mod, g, w_mix):
    t, tm = x_all.shape[0], lay.t
    widths = (_DA_W, _S5_W // 2, _S5_W // 2, _MLA_W, _RW_W)
    mrow = lay.mod_row(True)
    return pl.pallas_call(
        _inproj_kernel,
        grid=(t // tm,),
        in_specs=[
            pl.BlockSpec((tm, D_MODEL), lambda i: (i, 0)),
            pl.BlockSpec((1, 6, D_MODEL), lambda i: (mrow(i), 0, 0)),
            _full(g), _full(w_mix),
        ],
        out_specs=[pl.BlockSpec((tm, w), lambda i: (i, 0)) for w in widths],
        out_shape=[jax.ShapeDtypeStruct((t, w), F32) for w in widths],
        compiler_params=_cparams("parallel"),
        name="in_proj",
    )(x_all, mod, g, w_mix)


def _qkprep_kernel(da_ref, mla_ref, cda_ref, sda_ref, cml_ref, sml_ref, g32_ref, g64_ref,
                   gda_ref, gml_ref, cqg_ref, ckvg_ref, wuq_ref, wuk_ref, wuv_ref,
                   qd_ref, kd_ref, vd_ref, qm_ref, km_ref, vm_ref):
    g32 = g32_ref[...]
    g64 = g64_ref[...]
    cda, sda = cda_ref[...], sda_ref[...]
    q = _group_rms(da_ref[:, 0:MIX_W], g32, 1.0 / DA_DIM, gda_ref[0:1, :])
    qd_ref[...] = _rope(q, cda, sda, DA_DIM // 2, DA_DIM, DA_DIM // 2).astype(BF16)
    k = _group_rms(da_ref[:, MIX_W:2 * MIX_W], g32, 1.0 / DA_DIM, gda_ref[1:2, :])
    kd_ref[0] = _rope(k, cda, sda, DA_DIM // 2, DA_DIM, DA_DIM // 2).T.astype(BF16)
    vd_ref[...] = da_ref[:, 2 * MIX_W:3 * MIX_W].astype(BF16)

    cml, sml = cml_ref[...], sml_ref[...]
    cq = mla_ref[:, 0:256]
    cqn = cq * lax.rsqrt(jnp.sum(cq * cq, axis=-1, keepdims=True) * (1.0 / MLA_Q_RANK) + EPS) * cqg_ref[...]
    q = jnp.dot(cqn.astype(BF16), wuq_ref[...], preferred_element_type=F32)
    ckv = mla_ref[:, 256:384]
    ckvn = ckv * lax.rsqrt(jnp.mean(ckv * ckv, axis=-1, keepdims=True) + EPS) * ckvg_ref[...]
    ckvb = ckvn.astype(BF16)
    k = jnp.dot(ckvb, wuk_ref[...], preferred_element_type=F32) + mla_ref[:, 384:640]
    vm_ref[...] = jnp.dot(ckvb, wuv_ref[...], preferred_element_type=F32).astype(BF16)
    inv_n = 1.0 / (MLA_NOPE + MLA_ROPE)
    half = MLA_ROPE // 2
    q = _group_rms(q, g64, inv_n, gml_ref[0:1, :])
    qm_ref[...] = _rope(q, cml, sml, half, MLA_HEAD_PAD, MLA_NOPE + half).astype(BF16)
    k = _group_rms(k, g64, inv_n, gml_ref[1:2, :])
    km_ref[0] = _rope(k, cml, sml, half, MLA_HEAD_PAD, MLA_NOPE + half).T.astype(BF16)


def _qkprep_call(lay, da, mla, tabs, consts):
    t, tm = da.shape[0], lay.t
    st = lay.seq_tiles
    row = pl.BlockSpec((tm, MIX_W), lambda i: (i, 0))
    key_t = pl.BlockSpec((1, MIX_W, tm), lambda i: (i // st, 0, i % st))
    in_specs = [pl.BlockSpec((tm, _DA_W), lambda i: (i, 0)), pl.BlockSpec((tm, _MLA_W), lambda i: (i, 0))]
    in_specs += [pl.BlockSpec((tm, MIX_W), lambda i: (i % st, 0)) for _ in tabs]
    in_specs += [_full(a) for a in consts]
    tok = jax.ShapeDtypeStruct((t, MIX_W), BF16)
    keys = jax.ShapeDtypeStruct((lay.b, MIX_W, lay.n_tot), BF16)
    return pl.pallas_call(
        _qkprep_kernel,
        grid=(t // tm,),
        in_specs=in_specs,
        out_specs=[row, key_t, row, row, key_t, row],
        out_shape=[tok, keys, tok, tok, keys, tok],
        compiler_params=_cparams("parallel"),
        name="qk_prep",
    )(da, mla, *tabs, *consts)


def _softmax_parts(q, kt):
    s = jnp.dot(q, kt, preferred_element_type=F32)
    p = jnp.exp2(s - jnp.max(s, axis=-1, keepdims=True))
    return p, 1.0 / jnp.sum(p, axis=-1, keepdims=True)


def _attn_heads(q, kt_ref, v_ref, nk, diff, lam):
    lane = lax.broadcasted_iota(jnp.int32, (q.shape[0], MIX_W), 1)
    v = v_ref[0, 0:nk, :]
    acc = jnp.zeros((q.shape[0], MIX_W), F32)
    for h in range(DA_HEADS):
        if diff:
            e0, e1 = 2 * h * DA_DIM, (2 * h + 1) * DA_DIM
            p0, r0 = _softmax_parts(q[:, e0:e0 + DA_DIM], kt_ref[0, e0:e0 + DA_DIM, 0:nk])
            p1, r1 = _softmax_parts(q[:, e1:e1 + DA_DIM], kt_ref[0, e1:e1 + DA_DIM, 0:nk])
            o = jnp.dot((p0 * r0 - p1 * (r1 * lam)).astype(BF16), v, preferred_element_type=F32)
        else:
            e0 = h * MLA_HEAD_PAD
            p, r = _softmax_parts(q[:, e0:e0 + MLA_HEAD_PAD], kt_ref[0, e0:e0 + MLA_HEAD_PAD, 0:nk])
            o = jnp.dot(p.astype(BF16), v, preferred_element_type=F32) * r
        in_head = jnp.logical_and(lane >= h * DA_VDIM, lane < (h + 1) * DA_VDIM)
        acc = jnp.where(in_head, o, acc)
    return acc


def _attn_kernel(q_ref, kt_ref, v_ref, lam_ref, gain_ref, g64_ref, o_ref, *, diff, n_ctx, n_tot, ctx_tiles):
    q = q_ref[...]
    lam = lam_ref[...]

    def run(nk):
        o = _attn_heads(q, kt_ref, v_ref, nk, diff, lam)
        if diff:
            o = _group_rms(o, g64_ref[...], 1.0 / DA_VDIM, gain_ref[...])
        o_ref[...] = o.astype(BF16)

    if ctx_tiles:
        is_ctx = pl.program_id(1) < ctx_tiles
        pl.when(is_ctx)(lambda: run(n_ctx))
        pl.when(jnp.logical_not(is_ctx))(lambda: run(n_tot))
    else:
        run(n_tot)


def _attention(lay, q, kt, v, extra, diff, with_ctx, name):
    tq = lay.t
    tiles = lay.seq_tiles if with_ctx else lay.lat_tiles
    off = 0 if with_ctx else lay.ctx_tiles
    v3 = v.reshape(lay.b, lay.n_tot, MIX_W)
    kern = functools.partial(_attn_kernel, diff=diff, n_ctx=lay.n_ctx, n_tot=lay.n_tot,
                             ctx_tiles=lay.ctx_tiles if with_ctx else 0)
    return pl.pallas_call(
        kern,
        grid=(lay.b, tiles),
        in_specs=[
            pl.BlockSpec((tq, MIX_W), lambda b, j: (b * lay.seq_tiles + j + off, 0)),
            pl.BlockSpec((1, MIX_W, lay.n_tot), lambda b, j: (b, 0, 0)),
            pl.BlockSpec((1, lay.n_tot, MIX_W), lambda b, j: (b, 0, 0)),
        ] + [_full(a) for a in extra],
        out_specs=pl.BlockSpec((tq, MIX_W), lambda b, j: (b * tiles + j, 0)),
        out_shape=jax.ShapeDtypeStruct((lay.rows(with_ctx), MIX_W), BF16),
        compiler_params=_cparams("parallel", "parallel"),
        name=name,
    )(q, kt, v3, *extra)


def _chunk_rows(ua_ref, ub_ref):
    n = ua_ref.shape[0] // S5_CHUNK
    parts = []
    for s in range(S5_CHUNK):
        rows = pl.ds(s, n, stride=S5_CHUNK)
        parts += [ua_ref[rows, :], ub_ref[rows, :]]
    return jnp.concatenate(parts, axis=1).astype(BF16)


def _s5_proj_kernel(ua_ref, ub_ref, bre_ref, bim_ref, sre_ref, sim_ref):
    u = _chunk_rows(ua_ref, ub_ref)
    sre_ref[0] = jnp.dot(u, bre_ref[0], preferred_element_type=F32)
    sim_ref[0] = jnp.dot(u, bim_ref[0], preferred_element_type=F32)


def _s5_rec_kernel(sre_ref, sim_ref, are_ref, aim_ref, hre_ref, him_ref, *, n_batch, n_chunks, ctx_chunks):
    rev = pl.program_id(0) == 1
    ar, ai = are_ref[0], aim_ref[0]
    sre, sim, hre, him = sre_ref.at[0], sim_ref.at[0], hre_ref.at[0], him_ref.at[0]

    def step(i, carry):
        hr, hi = carry
        k_rev = jnp.where(i < ctx_chunks, ctx_chunks - 1 - i, n_chunks - 1 + ctx_chunks - i)
        k = jnp.where(rev, k_rev, i)
        rows = pl.ds(k, n_batch, stride=n_chunks)
        hre[rows, :] = hr
        him[rows, :] = hi
        return ar * hr - ai * hi + sre[rows, :], ar * hi + ai * hr + sim[rows, :]

    zero = jnp.zeros((n_batch, 128), F32)
    lax.fori_loop(0, n_chunks, step, (zero, zero))


def _s5_out_kernel(ua_ref, ub_ref, hre_ref, him_ref, m_ref, cre_ref, cim_ref, ya_ref, yb_ref):
    y = jnp.dot(_chunk_rows(ua_ref, ub_ref), m_ref[0], preferred_element_type=F32)
    y = y + _split_dot(hre_ref[0], cre_ref[0]) + _split_dot(him_ref[0], cim_ref[0])
    n = y.shape[0]
    ya, yb = ya_ref.at[0], yb_ref.at[0]
    for s in range(S5_CHUNK):
        rows = pl.ds(s, n, stride=S5_CHUNK)
        ya[rows, :] = y[:, s * MIX_W:s * MIX_W + 128]
        yb[rows, :] = y[:, s * MIX_W + 128:(s + 1) * MIX_W]


def _s5_mats(lam_re, lam_im, log_dt, b_re, b_im, c_re, c_im):
    hp = lax.Precision.HIGHEST
    L, G, P, CH = S5_CHUNK, S5_GROUPS, S5_STATE, S5_CH
    lr, li = lam_re.astype(F32), lam_im.astype(F32)
    dt = jnp.exp(log_dt.astype(F32))[..., None]
    zr, zi = lr * dt, li * dt
    j = jnp.arange(L + 1, dtype=F32)[:, None, None, None]
    mag = jnp.exp(zr[None] * j)
    pw_re, pw_im = mag * jnp.cos(zi[None] * j), mag * jnp.sin(zi[None] * j)
    nr, ni = pw_re[1] - 1.0, pw_im[1]
    den = lr * lr + li * li
    cr, ci = (nr * lr + ni * li) / den, (ni * lr - nr * li) / den
    bre, bim = b_re.astype(F32), b_im.astype(F32)
    bb_re = cr[..., None] * bre - ci[..., None] * bim
    bb_im = cr[..., None] * bim + ci[..., None] * bre
    x_re = pw_re[..., None] * bb_re[None] - pw_im[..., None] * bb_im[None]
    x_im = pw_re[..., None] * bb_im[None] + pw_im[..., None] * bb_re[None]
    cre, cim = c_re.astype(F32), c_im.astype(F32)
    kern = (jnp.einsum('dgcp,jdgpe->dgjce', cre, x_re[:L], precision=hp)
            - jnp.einsum('dgcp,jdgpe->dgjce', cim, x_im[:L], precision=hp))
    def blockdiag(x):
        a, b = x.shape[-2:]
        cat = x.reshape(x.shape[:2] + (G * a, b))
        spread = jnp.asarray(np.tile(np.eye(b, dtype=np.float32), (1, G)))
        mask = jnp.asarray(np.kron(np.eye(G, dtype=np.float32), np.ones((a, b), np.float32)))
        return (jnp.einsum('djrb,bc->djrc', cat, spread, precision=hp) * mask).astype(BF16)

    kbd = blockdiag(kern.transpose(0, 2, 1, 4, 3))
    bbd_re = blockdiag(x_re.transpose(1, 0, 2, 4, 3))
    bbd_im = blockdiag(x_im.transpose(1, 0, 2, 4, 3))
    pwt_re, pwt_im = pw_re.transpose(1, 0, 2, 3)[:, :, :, :, None], pw_im.transpose(1, 0, 2, 3)[:, :, :, :, None]
    cret, cimt = cre.transpose(0, 1, 3, 2)[:, None], cim.transpose(0, 1, 3, 2)[:, None]
    cbd_re = blockdiag(cret * pwt_re - cimt * pwt_im)
    cbd_im = blockdiag(-(cret * pwt_im + cimt * pwt_re))
    s_idx, t_idx = np.arange(L)[:, None], np.arange(L)[None, :]
    m, b_r, b_i, c_r, c_i = [], [], [], [], []
    for d in range(2):
        lag = (t_idx - s_idx) if d == 0 else (s_idx - t_idx)
        blocks = jnp.where(jnp.asarray(lag >= 0)[:, :, None, None], kbd[d][np.clip(lag, 0, L - 1)], 0)
        m.append(blocks.transpose(0, 2, 1, 3).reshape(L * G * CH, L * G * CH))
        pw = np.arange(L - 1, -1, -1) if d == 0 else np.arange(L)
        b_r.append(bbd_re[d][pw].reshape(L * G * CH, G * P))
        b_i.append(bbd_im[d][pw].reshape(L * G * CH, G * P))
        q = np.arange(1, L + 1) if d == 0 else np.arange(L, 0, -1)
        c_r.append(cbd_re[d][q].transpose(1, 0, 2).reshape(G * P, L * G * CH))
        c_i.append(cbd_im[d][q].transpose(1, 0, 2).reshape(G * P, L * G * CH))
    a_re, a_im = pw_re[L].reshape(2, 1, G * P), pw_im[L].reshape(2, 1, G * P)
    return jnp.stack(m), jnp.stack(b_r), jnp.stack(b_i), jnp.stack(c_r), jnp.stack(c_i), a_re, a_im


def _s5_scan(lay, ua, ub, mats):
    m, b_r, b_i, c_r, c_i, a_re, a_im = mats
    n_chunks = lay.n_tot // S5_CHUNK
    rows = lay.b * n_chunks
    tr = min(lay.t, rows)
    tok = tr * S5_CHUNK
    half = MIX_W // 2
    wspec = lambda a: pl.BlockSpec((1,) + a.shape[1:], lambda d, i: (d, 0, 0))
    state = jax.ShapeDtypeStruct((2, rows, S5_STATE_W), F32)
    sblk = pl.BlockSpec((1, tr, S5_STATE_W), lambda d, i: (d, i, 0))
    ublk = pl.BlockSpec((tok, half), lambda d, i: (i, 0))
    s_re, s_im = pl.pallas_call(
        _s5_proj_kernel,
        grid=(2, rows // tr),
        in_specs=[ublk, ublk, wspec(b_r), wspec(b_i)],
        out_specs=[sblk, sblk],
        out_shape=[state, state],
        compiler_params=_cparams("parallel", "parallel"),
        name="s5_proj",
    )(ua, ub, b_r, b_i)
    col = pl.BlockSpec((1, rows, 128), lambda d, j: (d, 0, j))
    acol = pl.BlockSpec((1, 1, 128), lambda d, j: (d, 0, j))
    h_re, h_im = pl.pallas_call(
        functools.partial(_s5_rec_kernel, n_batch=lay.b, n_chunks=n_chunks, ctx_chunks=lay.n_ctx // S5_CHUNK),
        grid=(2, S5_STATE_W // 128),
        in_specs=[col, col, acol, acol],
        out_specs=[col, col],
        out_shape=[state, state],
        compiler_params=_cparams("parallel", "parallel"),
        name="s5_rec",
    )(s_re, s_im, a_re, a_im)
    yblk = pl.BlockSpec((1, tok, half), lambda d, i: (d, i, 0))
    yshape = jax.ShapeDtypeStruct((2, lay.b * lay.n_tot, half), F32)
    return pl.pallas_call(
        _s5_out_kernel,
        grid=(2, rows // tr),
        in_specs=[ublk, ublk, sblk, sblk, wspec(m), wspec(c_r), wspec(c_i)],
        out_specs=[yblk, yblk],
        out_shape=[yshape, yshape],
        compiler_params=_cparams("parallel", "parallel"),
        name="s5_out",
    )(ua, ub, h_re, h_im, m, c_r, c_i)


def _s5_glu_kernel(ua_ref, ub_ref, ya_ref, yb_ref, d_ref, w_ref, b_ref, o_ref):
    u = jnp.concatenate([ua_ref[...], ub_ref[...]], axis=1)
    y = d_ref[...] * u + jnp.concatenate([ya_ref[0] + ya_ref[1], yb_ref[0] + yb_ref[1]], axis=1)
    z = 0.5 * y * (1.0 + jnp.tanh(math.sqrt(2.0 / math.pi) * (y + 0.044715 * (y * y * y))))
    gate = _sigmoid(jnp.dot(z.astype(BF16), w_ref[...], preferred_element_type=F32) + b_ref[...])
    o_ref[...] = (z * gate).astype(BF16)


def _s5_glu_call(ua, ub, ya, yb, d, w, bias, tm):
    t, half = ua.shape
    urow = pl.BlockSpec((tm, half), lambda i: (i, 0))
    yrow = pl.BlockSpec((2, tm, half), lambda i: (0, i, 0))
    return pl.pallas_call(
        _s5_glu_kernel,
        grid=(t // tm,),
        in_specs=[urow, urow, yrow, yrow, _full(d), _full(w), _full(bias)],
        out_specs=pl.BlockSpec((tm, MIX_W), lambda i: (i, 0)),
        out_shape=jax.ShapeDtypeStruct((t, MIX_W), BF16),
        compiler_params=_cparams("parallel"),
        name="s5_glu",
    )(ua, ub, ya, yb, d, w, bias)


def _rw_pre_kernel(x_ref, prev_ref, next_ref, mu_ref, g64_ref, kk_g_ref, ka_ref, rk_ref,
                   w0_ref, w1_ref, w2_ref, a0_ref, a1_ref, a2_ref, g1_ref, g2_ref,
                   r_ref, v_ref, kk_ref, lw_ref, kka_ref, km_ref, bon_ref, gate_ref,
                   *, seq_tiles, ctx_tiles):
    x = x_ref[...]
    n = x.shape[0]
    j = pl.program_id(0) % seq_tiles
    starts = jnp.logical_or(j == 0, j == ctx_tiles)
    ends = jnp.logical_or(j == ctx_tiles - 1, j == seq_tiles - 1)
    prev_row = jnp.where(starts, 0.0, prev_ref[0, 7:8, :])
    next_row = jnp.where(ends, 0.0, next_ref[0, 0:1, :])
    row = lax.broadcasted_iota(jnp.int32, x.shape, 0)
    left = jnp.where(row == 0, prev_row, pltpu.roll(x, 1, axis=0))
    right = jnp.where(row == n - 1, next_row, pltpu.roll(x, n - 1, axis=0))
    x = x + (0.5 * (left + right) - x) * mu_ref[...]
    r, k, v, xd = (x[:, i * MIX_W:(i + 1) * MIX_W] for i in range(4))
    g64 = g64_ref[...]
    kscaled = k * kk_g_ref[...]
    kk = kscaled / jnp.maximum(jnp.sqrt(_split_dot(kscaled * kscaled, g64)), 1e-12)
    xdb = xd.astype(BF16)
    r_ref[...] = r
    v_ref[...] = v
    kk_ref[...] = kk
    km_sum = None
    for d in range(2):
        lo = jnp.tanh(jnp.dot(xdb, w1_ref[d], preferred_element_type=F32))
        w_raw = w0_ref[d] + jnp.dot(lo.astype(BF16), w2_ref[d], preferred_element_type=F32)
        lw_ref[d] = -_sigmoid(w_raw) * math.exp(-0.5)
        ar = jnp.dot(xdb, a1_ref[d], preferred_element_type=F32)
        a = _sigmoid(a0_ref[d] + jnp.dot(ar.astype(BF16), a2_ref[d], preferred_element_type=F32))
        km = k * (1.0 + (a - 1.0) * ka_ref[...])
        kka_ref[d] = kk * a
        km_ref[d] = km
        km_sum = km if km_sum is None else km_sum + km
    bon_ref[...] = _split_dot(r * km_sum * rk_ref[...], g64) * v
    gr = _sigmoid(jnp.dot(xdb, g1_ref[...], preferred_element_type=F32))
    gate_ref[...] = jnp.dot(gr.astype(BF16), g2_ref[...], preferred_element_type=F32)


def _rw_pre_call(lay, rw, consts):
    t, tr = rw.shape[0], lay.t
    nt = t // tr
    g8 = tr // 8
    rw8 = rw.reshape(t // 8, 8, _RW_W)
    row = pl.BlockSpec((tr, MIX_W), lambda i: (i, 0))
    row2 = pl.BlockSpec((2, tr, MIX_W), lambda i: (0, i, 0))
    sd = jax.ShapeDtypeStruct((t, MIX_W), F32)
    sd2 = jax.ShapeDtypeStruct((2, t, MIX_W), F32)
    return pl.pallas_call(
        functools.partial(_rw_pre_kernel, seq_tiles=lay.seq_tiles, ctx_tiles=lay.ctx_tiles),
        grid=(nt,),
        in_specs=[pl.BlockSpec((tr, _RW_W), lambda i: (i, 0)),
                  pl.BlockSpec((1, 8, _RW_W), lambda i: (jnp.maximum(i * g8 - 1, 0), 0, 0)),
                  pl.BlockSpec((1, 8, _RW_W), lambda i: (jnp.minimum((i + 1) * g8, t // 8 - 1), 0, 0))]
                 + [_full(a) for a in consts],
        out_specs=[row, row, row, row2, row2, row2, row, row],
        out_shape=[sd, sd, sd, sd2, sd2, sd2, sd, sd],
        compiler_params=_cparams("parallel"),
        name="rwkv_pre",
    )(rw, rw8, rw8, *consts)


def _head_masks(shape, lane_axis, seg):
    lane = lax.broadcasted_iota(jnp.int32, shape, lane_axis)
    return [jnp.logical_and(lane >= h * seg, lane < (h + 1) * seg) for h in range(RW_HEADS)]


def _rw_prep_kernel(r_ref, kk_ref, v_ref, lw_ref, ka_ref, km_ref, perm_ref, permt_ref, g_ref, eye_ref,
                    br_ref, ck_ref, uvt_ref, y0_ref, pc_ref, *, rev):
    C, NC = RW_CHUNK, RW_TILE // RW_CHUNK
    perm, permt, g64, eye4 = perm_ref[...], permt_ref[...], g_ref[...], eye_ref[...]
    nat = jnp.concatenate([r_ref[0], kk_ref[0], v_ref[0], lw_ref[0, 0], ka_ref[0, 0], km_ref[0, 0]], axis=1)
    hi = nat.astype(BF16)
    lo = (nat - hi.astype(F32)).astype(BF16)
    pm = jnp.dot(perm, hi, preferred_element_type=F32) + jnp.dot(perm, lo, preferred_element_type=F32)
    r, kk, v, lw, ka, km = (pm[:, i * MIX_W:(i + 1) * MIX_W] for i in range(6))
    slab = lambda x, j: x[j * NC:(j + 1) * NC, :]
    order = list(range(C))[::-1] if rev else list(range(C))
    pos = {j: i for i, j in enumerate(order)}
    cum, run = {}, None
    for j in order:
        run = slab(lw, j) if run is None else run + slab(lw, j)
        cum[j] = run
    tot = run
    bh, ch, kh, rh, cp, kp, vv = {}, {}, {}, {}, {}, {}, {}
    for j in range(C):
        e_inv, e_end = jnp.exp(-cum[j]), jnp.exp(tot - cum[j])
        bh[j] = -slab(kk, j) * jnp.exp(cum[j] - slab(lw, j))
        ch[j], kh[j] = slab(ka, j) * e_inv, slab(km, j) * e_inv
        rh[j] = slab(r, j) * jnp.exp(cum[j])
        cp[j], kp[j] = slab(ka, j) * e_end, slab(km, j) * e_end
        vv[j] = slab(v, j)
    strict = [(t, s) for t in order for s in order if pos[s] < pos[t]]
    incl = [(t, s) for t in order for s in order if pos[s] <= pos[t]]
    def head_dots(lhs, rhs, pairs):
        prods = jnp.concatenate([lhs[t] * rhs[s] for t, s in pairs], axis=0).astype(BF16)
        gram = jnp.dot(prods, g64, preferred_element_type=F32)
        return {p: gram[i * NC:(i + 1) * NC, :] for i, p in enumerate(pairs)}

    acb, akb = head_dots(bh, ch, strict), head_dots(bh, kh, strict)
    mcr, mkr = head_dots(rh, ch, incl), head_dots(rh, kh, incl)
    bt, u0 = {}, {}
    for t in order:
        b_acc, u_acc = bh[t], jnp.zeros_like(bh[t])
        for s in order:
            if pos[s] < pos[t]:
                b_acc = b_acc + acb[(t, s)] * bt[s]
                u_acc = u_acc + akb[(t, s)] * vv[s] + acb[(t, s)] * u0[s]
        bt[t], u0[t] = b_acc, u_acc
    rt, y0 = {}, {}
    for t in order:
        r_acc, y_acc = rh[t], jnp.zeros_like(rh[t])
        for s in order:
            if pos[s] <= pos[t]:
                r_acc = r_acc + mcr[(t, s)] * bt[s]
                y_acc = y_acc + mcr[(t, s)] * u0[s] + mkr[(t, s)] * vv[s]
        rt[t], y0[t] = r_acc, y_acc
    stackp = lambda dct: jnp.concatenate([dct[j] for j in range(C)], axis=0)
    b16 = lambda x: x.astype(BF16)
    y0p, u0p = stackp(y0), stackp(u0)
    y0h, u0h = b16(y0p), b16(u0p)
    cat = jnp.concatenate([b16(stackp(bt)), b16(stackp(rt)), b16(stackp(cp)), b16(stackp(kp)),
                           y0h, b16(y0p - y0h.astype(F32)), u0h, b16(u0p - u0h.astype(F32)), b16(stackp(vv))], axis=1)
    natural = jnp.dot(permt, cat, preferred_element_type=F32)
    seg = lambda i: natural[:, i * MIX_W:(i + 1) * MIX_W]
    btn, rtn, cpn, kpn = b16(seg(0)), b16(seg(1)), b16(seg(2)), b16(seg(3))
    y0_ref[0] = seg(4) + seg(5)
    u0h_n, u0l_n, vn = b16(seg(6)), b16(seg(7)), b16(seg(8))
    hm = _head_masks((C, MIX_W), 1, RW_DIM)
    zero = jnp.zeros((C, MIX_W), BF16)
    zh, zl = [], []
    for c in range(NC):
        rows = slice(c * C, (c + 1) * C)
        br_ref[0, c, 0:C, :] = btn[rows]
        br_ref[0, c, C:2 * C, :] = rtn[rows]
        ck_ref[0, c, 0:C, :] = cpn[rows]
        ck_ref[0, c, C:2 * C, :] = kpn[rows]
        for h in range(RW_HEADS):
            zh += [jnp.where(hm[h], u0h_n[rows], zero), jnp.where(hm[h], vn[rows], zero)]
            zl += [jnp.where(hm[h], u0l_n[rows], zero), zero]
    uvt = (lax.dot_general(eye4, jnp.concatenate(zh, axis=0), _NT, preferred_element_type=F32)
           + lax.dot_general(eye4, jnp.concatenate(zl, axis=0), _NT, preferred_element_type=F32))
    for c in range(NC):
        uvt_ref[0, c] = uvt[:, c * 2 * C * RW_HEADS:(c + 1) * 2 * C * RW_HEADS]
    pc_ref[0] = jnp.exp(tot)


def _rw_prep_call(lay, shared, perdir, consts, rev):
    b, n_tot, tt = lay.b, lay.n_tot, RW_TILE
    nck = n_tot // RW_CHUNK
    cpt = tt // RW_CHUNK
    d = 1 if rev else 0
    sh = [a.reshape(b, n_tot, MIX_W) for a in shared]
    pd = [a.reshape(2, b, n_tot, MIX_W) for a in perdir]
    tok = pl.BlockSpec((1, tt, MIX_W), lambda i, j: (i, j, 0))
    tok_d = pl.BlockSpec((1, 1, tt, MIX_W), lambda i, j: (d, i, j, 0))
    rows32 = pl.BlockSpec((1, cpt, 2 * RW_CHUNK, MIX_W), lambda i, j: (i, j, 0, 0))
    return pl.pallas_call(
        functools.partial(_rw_prep_kernel, rev=rev),
        grid=(b, n_tot // tt),
        in_specs=[tok] * 3 + [tok_d] * 3 + [_full(a) for a in consts],
        out_specs=[rows32, rows32,
                   pl.BlockSpec((1, cpt, RW_DIM, 2 * RW_CHUNK * RW_HEADS), lambda i, j: (i, j, 0, 0)),
                   tok,
                   pl.BlockSpec((1, cpt, MIX_W), lambda i, j: (i, j, 0))],
        out_shape=[jax.ShapeDtypeStruct((b, nck, 2 * RW_CHUNK, MIX_W), BF16),
                   jax.ShapeDtypeStruct((b, nck, 2 * RW_CHUNK, MIX_W), BF16),
                   jax.ShapeDtypeStruct((b, nck, RW_DIM, 2 * RW_CHUNK * RW_HEADS), F32),
                   jax.ShapeDtypeStruct((b, n_tot, MIX_W), F32),
                   jax.ShapeDtypeStruct((b, nck, MIX_W), F32)],
        compiler_params=_cparams("parallel", "parallel"),
        name="rwkv_prep_rev" if rev else "rwkv_prep_fwd",
    )(*sh, *pd, *consts)


def _rw_scan_kernel(brf, ckf, uvtf, pcf, brr, ckr, uvtr, pcr, ytf_ref, ytr_ref, s_scr, *, n_batch):
    @pl.when(pl.program_id(0) == 0)
    def _():
        s_scr[...] = jnp.zeros_like(s_scr)

    cpt = RW_TILE // RW_CHUNK
    hm = _head_masks((2 * RW_CHUNK, MIX_W), 1, RW_DIM)
    lane = lax.broadcasted_iota(jnp.int32, (RW_DIM, 2 * RW_CHUNK * RW_HEADS), 1)
    is_u = (lane & (2 * RW_CHUNK - 1)) < RW_CHUNK
    per_head = lambda x: jnp.concatenate([jnp.where(m, x, jnp.zeros_like(x)) for m in hm], axis=0)

    def step(c, carry):
        for p in range(2 * n_batch):
            d, b = divmod(p, n_batch)
            br_ref, ck_ref, uvt_ref, pc_ref, yt_ref = (brf, ckf, uvtf, pcf, ytf_ref) if d == 0 else (brr, ckr, uvtr, pcr, ytr_ref)
            cc = c if d == 0 else cpt - 1 - c
            s = s_scr[p]
            shi = s.astype(BF16)
            slo = (s - shi.astype(F32)).astype(BF16)
            w2 = lax.dot_general(jnp.concatenate([shi, slo], axis=0), per_head(br_ref[b, cc]), _NT, preferred_element_type=F32)
            w = w2[:RW_DIM] + w2[RW_DIM:]
            yt_ref[b, cc] = w
            uvt = uvt_ref[b, cc]
            lhs = jnp.where(is_u, w + uvt, uvt).astype(BF16)
            s_scr[p] = s * pc_ref[b, pl.ds(cc, 1), :] + jnp.dot(lhs, per_head(ck_ref[b, cc]), preferred_element_type=F32)
        return carry

    lax.fori_loop(0, cpt, step, 0)


def _rw_scan_call(lay, fwd, rev):
    b, n_tot, tt = lay.b, lay.n_tot, RW_TILE
    assert lay.n_ctx % tt == 0 and lay.n_lat % tt == 0
    nt, ct = n_tot // tt, lay.n_ctx // tt
    cpt = tt // RW_CHUNK
    rev_tile = lambda i: jnp.where(i < ct, ct - 1 - i, nt - 1 + ct - i)

    def specs(tile):
        return [pl.BlockSpec((b, cpt, 2 * RW_CHUNK, MIX_W), lambda i: (0, tile(i), 0, 0)),
                pl.BlockSpec((b, cpt, 2 * RW_CHUNK, MIX_W), lambda i: (0, tile(i), 0, 0)),
                pl.BlockSpec((b, cpt, RW_DIM, 2 * RW_CHUNK * RW_HEADS), lambda i: (0, tile(i), 0, 0)),
                pl.BlockSpec((b, cpt, MIX_W), lambda i: (0, tile(i), 0))]

    ident = lambda i: i
    yt = jax.ShapeDtypeStruct((b, n_tot // RW_CHUNK, RW_DIM, 2 * RW_CHUNK * RW_HEADS), F32)
    return pl.pallas_call(
        functools.partial(_rw_scan_kernel, n_batch=b),
        grid=(nt,),
        in_specs=specs(ident) + specs(rev_tile),
        out_specs=[specs(ident)[2], specs(rev_tile)[2]],
        out_shape=[yt, yt],
        scratch_shapes=[pltpu.VMEM((2 * b, RW_DIM, MIX_W), F32)],
        compiler_params=_cparams("arbitrary"),
        name="rwkv_scan",
    )(*fwd, *rev)


def _rw_fin_kernel(ytf_ref, ytr_ref, y0f_ref, y0r_ref, bon_ref, gate_ref, asel_ref, g64_ref, lng_ref, lnb_ref, o_ref):
    cpt = RW_TILE // RW_CHUNK
    asel = asel_ref[...]
    width = cpt * 2 * RW_CHUNK * RW_HEADS
    lane = lax.broadcasted_iota(jnp.int32, (RW_DIM, width), 1)
    lane_head = jnp.bitwise_and(jnp.right_shift(lane, 5), RW_HEADS - 1)

    def base(yt_ref):
        yt = jnp.concatenate([yt_ref[0, c] for c in range(cpt)], axis=1)
        hi = yt.astype(BF16)
        lo = (yt - hi.astype(F32)).astype(BF16)
        rows = lambda x: jnp.concatenate([jnp.where(lane_head == h, x, jnp.zeros_like(x)) for h in range(RW_HEADS)], axis=0)
        return (lax.dot_general(asel, rows(hi), _NT, preferred_element_type=F32)
                + lax.dot_general(asel, rows(lo), _NT, preferred_element_type=F32))

    y = base(ytf_ref) + y0f_ref[0] + base(ytr_ref) + y0r_ref[0]
    g64 = g64_ref[...]
    mean = _split_dot(y, g64) * (1.0 / RW_DIM)
    c = y - mean
    var = _split_dot(c * c, g64) * (1.0 / RW_DIM)
    out = c * lax.rsqrt(var + RW_LN_EPS) * lng_ref[...] + lnb_ref[...] + bon_ref[...]
    o_ref[...] = (out * gate_ref[...]).astype(BF16)


def _rw_fin_call(lay, ytf, ytr, y0f, y0r, bon, gate, consts):
    b, n_tot, tt = lay.b, lay.n_tot, RW_TILE
    nt = n_tot // tt
    cpt = tt // RW_CHUNK
    ytb = pl.BlockSpec((1, cpt, RW_DIM, 2 * RW_CHUNK * RW_HEADS), lambda i, j: (i, j, 0, 0))
    y0b = pl.BlockSpec((1, tt, MIX_W), lambda i, j: (i, j, 0))
    row = pl.BlockSpec((tt, MIX_W), lambda i, j: (i * nt + j, 0))
    return pl.pallas_call(
        _rw_fin_kernel,
        grid=(b, nt),
        in_specs=[ytb, ytb, y0b, y0b, row, row] + [_full(a) for a in consts],
        out_specs=row,
        out_shape=jax.ShapeDtypeStruct((b * n_tot, MIX_W), BF16),
        compiler_params=_cparams("parallel", "parallel"),
        name="rwkv_finish",
    )(ytf, ytr, y0f, y0r, bon, gate, *consts)


def _rw_constants():
    c, nc = RW_CHUNK, RW_TILE // RW_CHUNK
    perm = np.zeros((RW_TILE, RW_TILE), np.float32)
    for ci in range(nc):
        for j in range(c):
            perm[j * nc + ci, ci * c + j] = 1.0
    lane = np.arange(MIX_W) % RW_DIM
    eye4 = (lane[None, :] == np.arange(RW_DIM)[:, None]).astype(np.float32)
    lanes = np.arange(nc * 2 * c * RW_HEADS)
    lane_chunk, lane_tok = lanes // (2 * c * RW_HEADS), lanes % (2 * c)
    t = np.arange(RW_TILE)
    asel = ((lane_chunk[None, :] == (t // c)[:, None]) & (lane_tok[None, :] == (c + t % c)[:, None])).astype(np.float32)
    as16 = lambda a: jnp.asarray(a, BF16)
    return as16(perm), as16(perm.T), as16(eye4), as16(asel)


def _rwkv_branch(lay, rw, pre_consts, g64, lng, lnb):
    r_, v_, kk_, lw_, kka_, km_, bon, gate = _rw_pre_call(lay, rw, pre_consts)
    perm, permt, eye4, asel = _rw_constants()
    prep_consts = (perm, permt, g64, eye4)
    fwd = _rw_prep_call(lay, (r_, kk_, v_), (lw_, kka_, km_), prep_consts, False)
    rev = _rw_prep_call(lay, (r_, kk_, v_), (lw_, kka_, km_), prep_consts, True)
    pick = lambda o: (o[0], o[1], o[2], o[4])
    ytf, ytr = _rw_scan_call(lay, pick(fwd), pick(rev))
    return _rw_fin_call(lay, ytf, ytr, fwd[3], rev[3], bon, gate, (asel, g64, lng, lnb))


def _merge_kernel(x_ref, mod_ref, g_ref, wg_ref, ya_ref, yb_ref, yc_ref, yd_ref, wb_ref, wo_ref, o_ref):
    x = x_ref[...]
    h = _modulate(x, g_ref[...], mod_ref[0, 0:1, :], mod_ref[0, 1:2, :]).astype(BF16)
    merged = None
    for i, y_ref in enumerate((ya_ref, yb_ref, yc_ref, yd_ref)):
        gate = _sigmoid(jnp.dot(h, wg_ref[:, i * D_MODEL:(i + 1) * D_MODEL], preferred_element_type=F32))
        term = gate * jnp.dot(y_ref[...], wb_ref[i], preferred_element_type=F32)
        merged = term if merged is None else merged + term
    out = jnp.dot(merged.astype(BF16), wo_ref[...], preferred_element_type=F32)
    o_ref[...] = x + mod_ref[0, 2:3, :] * out


def _merge_call(lay, with_ctx, x_all, mod, g, w_gate, ya, yb, yc, yd, w_branch, w_out):
    tm = lay.t
    src, mrow = lay.src_tile(with_ctx), lay.mod_row(with_ctx)
    full_row = lambda w: pl.BlockSpec((tm, w), lambda i: (src(i), 0))
    out_row = lambda w: pl.BlockSpec((tm, w), lambda i: (i, 0))
    return pl.pallas_call(
        _merge_kernel,
        grid=(lay.n_tiles(with_ctx),),
        in_specs=[full_row(D_MODEL), pl.BlockSpec((1, 6, D_MODEL), lambda i: (mrow(i), 0, 0)), _full(g), _full(w_gate),
                  out_row(MIX_W), full_row(MIX_W), out_row(MIX_W), full_row(MIX_W), _full(w_branch), _full(w_out)],
        out_specs=out_row(D_MODEL),
        out_shape=jax.ShapeDtypeStruct((lay.rows(with_ctx), D_MODEL), F32),
        compiler_params=_cparams("parallel"),
        name="merge_out",
    )(x_all, mod, g, w_gate, ya, yb, yc, yd, w_branch, w_out)


def _router_kernel(x_ref, mod_ref, g_ref, wh_ref, wl_ref, bias_ref, f_ref, comb_ref):
    f = _modulate(x_ref[...], g_ref[...], mod_ref[0, 3:4, :], mod_ref[0, 4:5, :])
    fh = f.astype(BF16)
    f_ref[...] = fh
    fl = (f - fh.astype(F32)).astype(BF16)
    nt = (((1,), (1,)), ((), ()))
    wh, wl = wh_ref[...], wl_ref[...]
    logits = (lax.dot_general(wh, fh, nt, preferred_element_type=F32)
              + lax.dot_general(wh, fl, nt, preferred_element_type=F32)
              + lax.dot_general(wl, fh, nt, preferred_element_type=F32))
    scores = _sigmoid(logits)
    biased = scores + bias_ref[...]
    sc = [scores[e:e + 1, :] for e in range(N_EXPERTS)]
    bi = [biased[e:e + 1, :] for e in range(N_EXPERTS)]
    group_score = []
    for g in range(N_GROUPS):
        a, b, c, d = bi[4 * g:4 * g + 4]
        m1, n1, m2, n2 = jnp.maximum(a, b), jnp.minimum(a, b), jnp.maximum(c, d), jnp.minimum(c, d)
        group_score.append(jnp.maximum(m1, m2) + jnp.maximum(jnp.minimum(m1, m2), jnp.maximum(n1, n2)))

    def first_argmax(vals):
        top = functools.reduce(jnp.maximum, vals)
        seen, hot = None, []
        for v in vals:
            h = v == top
            if seen is not None:
                h = jnp.logical_and(h, jnp.logical_not(seen))
            seen = h if seen is None else jnp.logical_or(seen, h)
            hot.append(h)
        return hot

    in_group = first_argmax(group_score)
    masked = [jnp.where(in_group[e // EXPERTS_PER_GROUP], bi[e], -jnp.inf) for e in range(N_EXPERTS)]
    hot1 = first_argmax(masked)
    hot2 = first_argmax([jnp.where(h, -jnp.inf, v) for h, v in zip(hot1, masked)])
    w1 = functools.reduce(jnp.add, [jnp.where(h, s, 0.0) for h, s in zip(hot1, sc)])
    w2 = functools.reduce(jnp.add, [jnp.where(h, s, 0.0) for h, s in zip(hot2, sc)])
    inv_tot = 1.0 / (w1 + w2)
    for e in range(N_EXPERTS):
        comb_ref[e:e + 1, :] = (jnp.where(hot1[e], w1, 0.0) + jnp.where(hot2[e], w2, 0.0)) * inv_tot


def _router_call(lay, with_ctx, x, mod, g, wh, wl, bias):
    t, tm = x.shape[0], lay.t
    mrow = lay.mod_row(with_ctx)
    return pl.pallas_call(
        _router_kernel,
        grid=(t // tm,),
        in_specs=[pl.BlockSpec((tm, D_MODEL), lambda i: (i, 0)),
                  pl.BlockSpec((1, 6, D_MODEL), lambda i: (mrow(i), 0, 0)), _full(g), _full(wh), _full(wl), _full(bias)],
        out_specs=[pl.BlockSpec((tm, D_MODEL), lambda i: (i, 0)), pl.BlockSpec((N_EXPERTS, tm), lambda i: (0, i))],
        out_shape=[jax.ShapeDtypeStruct((t, D_MODEL), BF16), jax.ShapeDtypeStruct((N_EXPERTS, t), F32)],
        compiler_params=_cparams("parallel"),
        name="moe_router",
    )(x, mod, g, wh, wl, bias)


def _moe_kernel(f_ref, comb_ref, wg_ref, wu_ref, wd_ref, x_ref, modb_ref, modc_ref, o_ref, acc_ref,
                *, ctx_rows, tiles_per_seq):
    e = pl.program_id(1)

    @pl.when(e == 0)
    def _():
        acc_ref[...] = jnp.zeros_like(acc_ref)

    f = f_ref[...]
    gate = jnp.dot(f, wg_ref[0], preferred_element_type=F32)
    up = jnp.dot(f, wu_ref[0], preferred_element_type=F32)
    act = (gate * _sigmoid(gate) * up).astype(BF16)
    down = jnp.dot(act, wd_ref[0], preferred_element_type=F32)
    comb = comb_ref[...]
    lane = lax.broadcasted_iota(jnp.int32, comb.shape, 1)
    c_e = jnp.sum(jnp.where(lane == e, comb, 0.0), axis=1, keepdims=True)
    acc_ref[...] += c_e * down

    @pl.when(e == N_EXPERTS - 1)
    def _():
        res_gate = modb_ref[0, 5:6, :]
        if ctx_rows:
            row = lax.broadcasted_iota(jnp.int32, acc_ref.shape, 0)
            first = pl.program_id(0) % tiles_per_seq == 0
            res_gate = jnp.where(jnp.logical_and(first, row < ctx_rows), modc_ref[0, 5:6, :], res_gate)
        o_ref[...] = x_ref[...] + res_gate * acc_ref[...]


def _moe_call(lay, with_ctx, f, comb, wg, wu, wd, x, mod):
    t = f.shape[0]
    seq = lay.n_tot if with_ctx else lay.n_lat
    tm = MOE_TILE if seq % MOE_TILE == 0 else math.gcd(seq, 1024)
    tps = seq // tm
    ctx_rows = lay.n_ctx if with_ctx else 0
    assert ctx_rows <= tm
    wspec = lambda a: pl.BlockSpec((1,) + a.shape[1:], lambda i, e: (e, 0, 0))
    tok = lambda w: pl.BlockSpec((tm, w), lambda i, e: (i, 0))
    return pl.pallas_call(
        functools.partial(_moe_kernel, ctx_rows=ctx_rows, tiles_per_seq=tps),
        grid=(t // tm, N_EXPERTS),
        in_specs=[tok(D_MODEL), tok(N_EXPERTS), wspec(wg), wspec(wu), wspec(wd), tok(D_MODEL),
                  pl.BlockSpec((1, 6, D_MODEL), lambda i, e: (i // tps, 0, 0)),
                  pl.BlockSpec((1, 6, D_MODEL), lambda i, e: (lay.b, 0, 0))],
        out_specs=tok(D_MODEL),
        out_shape=jax.ShapeDtypeStruct((t, D_MODEL), F32),
        scratch_shapes=[pltpu.VMEM((tm, D_MODEL), F32)],
        compiler_params=_cparams("parallel", "arbitrary"),
        name="moe_experts",
    )(f, comb, wg, wu, wd, x, mod, mod)


def _block_ones(n, group):
    i = np.arange(n) // group
    return jnp.asarray(i[:, None] == i[None, :], dtype=BF16)


def _rope_tables(n_ctx, n_lat):
    rows = n_lat // GRID_W
    row = jnp.repeat(jnp.arange(rows, dtype=F32), GRID_W)
    col = jnp.tile(jnp.arange(GRID_W, dtype=F32), rows)

    def angles(rot_dim):
        n_freq = rot_dim // 4
        inv_freq = ROPE_BASE ** (-jnp.arange(n_freq, dtype=F32) / n_freq)
        ang = jnp.concatenate([row[:, None] * inv_freq, col[:, None] * inv_freq], axis=-1)
        return jnp.cos(ang), jnp.sin(ang)

    c, s = angles(DA_DIM)
    cda = jnp.tile(jnp.concatenate([c, c], -1), (1, 2 * DA_HEADS))
    sda = jnp.tile(jnp.concatenate([-s, s], -1), (1, 2 * DA_HEADS))
    c, s = angles(MLA_ROPE)
    one = jnp.ones((n_lat, MLA_NOPE), F32)
    pad = MLA_HEAD_PAD - MLA_NOPE - MLA_ROPE
    cml = jnp.tile(jnp.concatenate([one, c, c, jnp.ones((n_lat, pad), F32)], -1), (1, MLA_HEADS))
    sml = jnp.tile(jnp.concatenate([0 * one, -s, s, jnp.zeros((n_lat, pad), F32)], -1), (1, MLA_HEADS))
    ident = lambda t, v: jnp.concatenate([jnp.full((n_ctx, MIX_W), v, F32), t], axis=0)
    return ident(cda, 1.0), ident(sda, 0.0), ident(cml, 1.0), ident(sml, 0.0)


def _pad_heads(w, n_heads, src_w, lo, hi, dst_w=MLA_HEAD_PAD):
    w = w.reshape(w.shape[0], n_heads, src_w)[:, :, lo:hi]
    w = jnp.pad(w, ((0, 0), (0, 0), (0, dst_w - (hi - lo))))
    return w.reshape(w.shape[0], n_heads * dst_w)


def _mix_weight(w_in_l):
    w = w_in_l
    kr = w[:, 1344:1360]
    z = lambda n: jnp.zeros((D_MODEL, n), w.dtype)
    kr_wide = jnp.concatenate([jnp.concatenate([z(MLA_NOPE), kr, z(MLA_HEAD_PAD - MLA_NOPE - MLA_ROPE)], 1)] * MLA_HEADS, 1)
    return jnp.concatenate([w[:, 0:1024], w[:, 1024:1216], z(64), w[:, 1216:1344], kr_wide, w[:, 1360:2384]], axis=1).astype(BF16)


def kernel(x, c, ctx, c_ctx, w_ada, b_ada, norm_mix_g, norm_ffn_g, w_in, da_qk_norm_g, da_lambda, da_subln_g, s5_lam_re, s5_lam_im, s5_log_dt, s5_b_re, s5_b_im, s5_c_re, s5_c_im, s5_d, s5_w_glu, s5_b_glu, mla_cq_norm_g, mla_ckv_norm_g, mla_w_uq, mla_w_ukv, mla_qk_norm_g, rw_mu, rw_w0, rw_w1, rw_w2, rw_a0, rw_a1, rw_a2, rw_g1, rw_g2, rw_k_k, rw_k_a, rw_r_k, rw_ln_g, rw_ln_b, w_branch, w_out, router_w, router_bias, exp_w_gate, exp_w_up, exp_w_down):
    b, n_lat, dm = x.shape
    n_ctx = ctx.shape[1]
    depth = w_ada.shape[0]
    assert dm == D_MODEL
    lay = _Layout(b, n_ctx, n_lat)
    t_all = b * lay.n_tot
    tm_big = 2 * lay.t

    g32 = _block_ones(MIX_W, DA_DIM)
    g64 = _block_ones(MIX_W, RW_DIM)
    tabs = _rope_tables(n_ctx, n_lat)
    row = lambda v: v.reshape(1, -1).astype(F32)
    bf = lambda a: a.astype(BF16)

    cc = jnp.zeros((16, dm), F32).at[:b].set(c).at[b].set(c_ctx)
    mod_all = _ada_call(cc, w_ada, b_ada)
    x_all = jnp.concatenate([ctx, x], axis=1).reshape(t_all, dm)

    wr_hi = router_w.T.astype(BF16)
    wr_lo = (router_w.T - wr_hi.astype(F32)).astype(BF16)
    r_bias = router_bias.reshape(N_EXPERTS, 1).astype(F32)

    for l in range(depth):
        need_ctx = l < depth - 1
        lambda_init = 0.8 - 0.6 * math.exp(-0.3 * l)
        mod = mod_all[l, :b + 1].reshape(b + 1, 6, dm)
        g_mix = row(norm_mix_g[l])
        da, s5a, s5b, mla, rw = _inproj_call(lay, x_all, mod, g_mix, _mix_weight(w_in[l]))

        log2e = math.log2(math.e)
        gda = jnp.stack([jnp.tile(da_qk_norm_g[l, 0], 2 * DA_HEADS) * (DA_DIM ** -0.5 * log2e), jnp.tile(da_qk_norm_g[l, 1], 2 * DA_HEADS)])
        mla_pad = MLA_HEAD_PAD - MLA_NOPE - MLA_ROPE
        gml = jnp.stack([jnp.tile(jnp.pad(mla_qk_norm_g[l, 0], (0, mla_pad)), MLA_HEADS) * ((MLA_NOPE + MLA_ROPE) ** -0.5 * log2e),
                         jnp.tile(jnp.pad(mla_qk_norm_g[l, 1], (0, mla_pad)), MLA_HEADS)])
        wuq = bf(jnp.pad(_pad_heads(mla_w_uq[l], MLA_HEADS, MLA_NOPE + MLA_ROPE, 0, MLA_NOPE + MLA_ROPE), ((0, 64), (0, 0))))
        wuk = bf(_pad_heads(mla_w_ukv[l], MLA_HEADS, MLA_NOPE + MLA_VDIM, 0, MLA_NOPE))
        wuv = bf(_pad_heads(mla_w_ukv[l], MLA_HEADS, MLA_NOPE + MLA_VDIM, MLA_NOPE, MLA_NOPE + MLA_VDIM))
        consts = (g32, g64, gda.astype(F32), gml.astype(F32), row(jnp.pad(mla_cq_norm_g[l], (0, 64))), row(mla_ckv_norm_g[l]),
                  wuq, wuk, wuv)
        qd, kdt, vd, qm, kmt, vm = _qkprep_call(lay, da, mla, tabs, consts)

        lam32 = da_lambda[l].astype(F32)
        lmbda = (jnp.exp(jnp.sum(lam32[0] * lam32[1])) - jnp.exp(jnp.sum(lam32[2] * lam32[3])) + lambda_init).reshape(1, 1)
        subln = row(jnp.tile(da_subln_g[l], DA_HEADS) * (1.0 - lambda_init))
        ya = _attention(lay, qd, kdt, vd, (lmbda, subln, g64), True, need_ctx, "diff_attn")
        yc = _attention(lay, qm, kmt, vm, (lmbda, subln, g64), False, need_ctx, "mla_attn")

        mats = _s5_mats(s5_lam_re[l], s5_lam_im[l], s5_log_dt[l], s5_b_re[l], s5_b_im[l], s5_c_re[l], s5_c_im[l])
        ys_a, ys_b = _s5_scan(lay, s5a, s5b, mats)
        yb = _s5_glu_call(s5a, s5b, ys_a, ys_b, row(s5_d[l]), bf(s5_w_glu[l]), row(s5_b_glu[l]), tm_big)

        pre_consts = (row(rw_mu[l]), g64, row(rw_k_k[l]), row(rw_k_a[l]), row(rw_r_k[l]),
                      rw_w0[l].reshape(2, 1, MIX_W), bf(rw_w1[l]), bf(rw_w2[l]),
                      rw_a0[l].reshape(2, 1, MIX_W), bf(rw_a1[l]), bf(rw_a2[l]), bf(rw_g1[l]), bf(rw_g2[l]))
        yd = _rwkv_branch(lay, rw, pre_consts, g64, row(rw_ln_g[l]), row(rw_ln_b[l]))

        x_mid = _merge_call(lay, need_ctx, x_all, mod, g_mix, bf(w_in[l][:, 2384:]), ya, yb, yc, yd,
                            bf(w_branch[l]), bf(w_out[l]))
        f, comb_t = _router_call(lay, need_ctx, x_mid, mod, row(norm_ffn_g[l]), wr_hi, wr_lo, r_bias)
        x_all = _moe_call(lay, need_ctx, f, comb_t.T, bf(exp_w_gate[l]), bf(exp_w_up[l]), bf(exp_w_down[l]), x_mid, mod)
    return x_all.reshape(b, n_lat, dm)
```

```python
import functools
import math

import numpy as np
import jax
import jax.numpy as jnp
from jax import lax
from jax.experimental import pallas as pl
from jax.experimental.pallas import tpu as pltpu

F32 = jnp.float32
BF16 = jnp.bfloat16

D_MODEL = 1024
GRID_W = 64
ROPE_BASE = 10000.0
EPS = 1e-6
DA_HEADS, DA_DIM, DA_VDIM = 4, 32, 64
S5_GROUPS, S5_CH, S5_STATE = 16, 16, 64
MLA_HEADS, MLA_NOPE, MLA_ROPE, MLA_VDIM = 4, 32, 16, 64
MLA_Q_RANK, MLA_KV_RANK = 192, 128
MLA_HEAD_PAD = 64
RW_HEADS, RW_DIM = 4, 64
RW_LN_EPS = 64e-5
N_BRANCH = 4
N_EXPERTS, N_GROUPS, EXPERTS_PER_GROUP = 16, 4, 4
D_FF = 512
MIX_W = 256

S5_CHUNK = 8
S5_FLAT = S5_CHUNK * MIX_W
S5_STATE_W = S5_GROUPS * S5_STATE
RW_CHUNK = 16
RW_TILE = 128
RW_PREP_TILES = 2
_NT = (((1,), (1,)), ((), ()))
TOKEN_TILE = 256
MOE_TILE = 1152
MOE_BLOCK = 384
MOE_OVER = 128

_DA_W, _S5_W, _MLA_W, _RW_W = 768, 256, 640, 1024
_MIX_COLS = _DA_W + _S5_W + _MLA_W + _RW_W

V7X_VMEM_BYTES = 64 * 2**20
_VMEM_LIMIT = V7X_VMEM_BYTES - 8 * 2**20


def _cparams(*sem):
    return pltpu.CompilerParams(dimension_semantics=sem, vmem_limit_bytes=_VMEM_LIMIT)


def _full(a):
    return pl.BlockSpec(a.shape, lambda *_, nd=a.ndim: (0,) * nd)


def _split_dot(x, w, terms=2):
    acc = None
    rem = x
    for i in range(terms):
        part = rem.astype(BF16)
        d = jnp.dot(part, w, preferred_element_type=F32)
        acc = d if acc is None else acc + d
        if i + 1 < terms:
            rem = rem - part.astype(F32)
    return acc


def _split_dot_rhs(w, x):
    hi = x.astype(BF16)
    lo = (x - hi.astype(F32)).astype(BF16)
    return jnp.dot(w, hi, preferred_element_type=F32) + jnp.dot(w, lo, preferred_element_type=F32)


def _modulate(x, g, shift, scale):
    xn = x * lax.rsqrt(jnp.mean(x * x, axis=-1, keepdims=True) + EPS)
    return xn * g * (1.0 + scale) + shift


def _sigmoid(x):
    return 1.0 / (1.0 + jnp.exp(-x))


def _group_rms(x, ones_bd, inv_n, gain):
    ms = _split_dot(x * x, ones_bd) * inv_n
    return x * lax.rsqrt(ms + EPS) * gain


def _lane_partner(x, half, period, first_end):
    n = x.shape[1]
    lane = lax.broadcasted_iota(jnp.int32, x.shape, 1)
    up = pltpu.roll(x, n - half, axis=1)
    down = pltpu.roll(x, half, axis=1)
    return jnp.where((lane & (period - 1)) < first_end, up, down)


def _rope(x, cos_t, sin_t, half, period, first_end):
    return x * cos_t + _lane_partner(x, half, period, first_end) * sin_t


def _ada_kernel(c_ref, w_ref, b_ref, o_ref):
    c = c_ref[...]
    s = c * _sigmoid(c)
    o_ref[0] = jnp.dot(s.astype(BF16), w_ref[0].astype(BF16), preferred_element_type=F32) + b_ref[0]


def _ada_call(cc, w_ada, b_ada):
    depth, dm, n = w_ada.shape
    tn = n // 4
    return pl.pallas_call(
        _ada_kernel,
        grid=(depth, n // tn),
        in_specs=[
            pl.BlockSpec(cc.shape, lambda l, j: (0, 0)),
            pl.BlockSpec((1, dm, tn), lambda l, j: (l, 0, j)),
            pl.BlockSpec((1, 1, tn), lambda l, j: (l, 0, j)),
        ],
        out_specs=pl.BlockSpec((1, cc.shape[0], tn), lambda l, j: (l, 0, j)),
        out_shape=jax.ShapeDtypeStruct((depth, cc.shape[0], n), F32),
        compiler_params=_cparams("parallel", "parallel"),
        name="ada_mod",
    )(cc, w_ada, b_ada.reshape(depth, 1, n))


class _Layout:
    def __init__(self, n_batch, n_ctx, n_lat):
        t = TOKEN_TILE
        assert n_ctx % t == 0 and n_lat % t == 0
        self.b, self.n_ctx, self.n_lat, self.n_tot = n_batch, n_ctx, n_lat, n_ctx + n_lat
        self.t = t
        self.ctx_tiles, self.lat_tiles, self.seq_tiles = n_ctx // t, n_lat // t, (n_ctx + n_lat) // t

    def rows(self, with_ctx):
        return self.b * (self.n_tot if with_ctx else self.n_lat)

    def n_tiles(self, with_ctx):
        return self.b * (self.seq_tiles if with_ctx else self.lat_tiles)

    def src_tile(self, with_ctx):
        if with_ctx:
            return lambda i: i
        return lambda i: (i // self.lat_tiles) * self.seq_tiles + i % self.lat_tiles + self.ctx_tiles

    def mod_row(self, with_ctx):
        if with_ctx:
            return lambda i: jnp.where(i % self.seq_tiles < self.ctx_tiles, self.b, i // self.seq_tiles)
        return lambda i: i // self.lat_tiles


def _inproj_kernel(x_ref, mod_ref, g_ref, w_ref, da_ref, s5a_ref, s5b_ref, mla_ref, rw_ref):
    h = _modulate(x_ref[...], g_ref[...], mod_ref[0, 0:1, :], mod_ref[0, 1:2, :])
    acc = jnp.dot(h.astype(BF16), w_ref[...], preferred_element_type=F32)
    da_ref[...] = acc[:, 0:_DA_W]
    s5a_ref[...] = acc[:, _DA_W:_DA_W + _S5_W // 2]
    s5b_ref[...] = acc[:, _DA_W + _S5_W // 2:_DA_W + _S5_W]
    mla_ref[...] = acc[:, _DA_W + _S5_W:_DA_W + _S5_W + _MLA_W]
    rw_ref[...] = acc[:, _DA_W + _S5_W + _MLA_W:_MIX_COLS]


def _inproj_call(lay, x_all, mod, g, w_mix):
    t, tm = x_all.shape[0], lay.t
    widths = (_DA_W, _S5_W // 2, _S5_W // 2, _MLA_W, _RW_W)
    mrow = lay.mod_row(True)
    return pl.pallas_call(
        _inproj_kernel,
        grid=(t // tm,),
        in_specs=[
            pl.BlockSpec((tm, D_MODEL), lambda i: (i, 0)),
            pl.BlockSpec((1, 6, D_MODEL), lambda i: (mrow(i), 0, 0)),
            _full(g), _full(w_mix),
        ],
        out_specs=[pl.BlockSpec((tm, w), lambda i: (i, 0)) for w in widths],
        out_shape=[jax.ShapeDtypeStruct((t, w), F32) for w in widths],
        compiler_params=_cparams("parallel"),
        name="in_proj",
    )(x_all, mod, g, w_mix)


def _qkprep_kernel(da_ref, mla_ref, cda_ref, sda_ref, cml_ref, sml_ref, g32_ref, g64_ref,
                   gda_ref, gml_ref, cqg_ref, ckvg_ref, wuq_ref, wuk_ref, wuv_ref,
                   qd_ref, kd_ref, vd_ref, qm_ref, km_ref, vm_ref):
    g32 = g32_ref[...]
    g64 = g64_ref[...]
    cda, sda = cda_ref[...], sda_ref[...]
    q = _group_rms(da_ref[:, 0:MIX_W], g32, 1.0 / DA_DIM, gda_ref[0:1, :])
    qd_ref[...] = _rope(q, cda, sda, DA_DIM // 2, DA_DIM, DA_DIM // 2).astype(BF16)
    k = _group_rms(da_ref[:, MIX_W:2 * MIX_W], g32, 1.0 / DA_DIM, gda_ref[1:2, :])
    kd_ref[0] = _rope(k, cda, sda, DA_DIM // 2, DA_DIM, DA_DIM // 2).T.astype(BF16)
    vd_ref[...] = da_ref[:, 2 * MIX_W:3 * MIX_W].astype(BF16)

    cml, sml = cml_ref[...], sml_ref[...]
    cq = mla_ref[:, 0:256]
    cqn = cq * lax.rsqrt(jnp.sum(cq * cq, axis=-1, keepdims=True) * (1.0 / MLA_Q_RANK) + EPS) * cqg_ref[...]
    q = jnp.dot(cqn.astype(BF16), wuq_ref[...], preferred_element_type=F32)
    ckv = mla_ref[:, 256:384]
    ckvn = ckv * lax.rsqrt(jnp.mean(ckv * ckv, axis=-1, keepdims=True) + EPS) * ckvg_ref[...]
    ckvb = ckvn.astype(BF16)
    k = jnp.dot(ckvb, wuk_ref[...], preferred_element_type=F32) + mla_ref[:, 384:640]
    vm_ref[...] = jnp.dot(ckvb, wuv_ref[...], preferred_element_type=F32).astype(BF16)
    inv_n = 1.0 / (MLA_NOPE + MLA_ROPE)
    half = MLA_ROPE // 2
    q = _group_rms(q, g64, inv_n, gml_ref[0:1, :])
    qm_ref[...] = _rope(q, cml, sml, half, MLA_HEAD_PAD, MLA_NOPE + half).astype(BF16)
    k = _group_rms(k, g64, inv_n, gml_ref[1:2, :])
    km_ref[0] = _rope(k, cml, sml, half, MLA_HEAD_PAD, MLA_NOPE + half).T.astype(BF16)


def _qkprep_call(lay, da, mla, tabs, consts):
    t, tm = da.shape[0], lay.t
    st = lay.seq_tiles
    row = pl.BlockSpec((tm, MIX_W), lambda i: (i, 0))
    key_t = pl.BlockSpec((1, MIX_W, tm), lambda i: (i // st, 0, i % st))
    in_specs = [pl.BlockSpec((tm, _DA_W), lambda i: (i, 0)), pl.BlockSpec((tm, _MLA_W), lambda i: (i, 0))]
    in_specs += [pl.BlockSpec((tm, MIX_W), lambda i: (i % st, 0)) for _ in tabs]
    in_specs += [_full(a) for a in consts]
    tok = jax.ShapeDtypeStruct((t, MIX_W), BF16)
    keys = jax.ShapeDtypeStruct((lay.b, MIX_W, lay.n_tot), BF16)
    return pl.pallas_call(
        _qkprep_kernel,
        grid=(t // tm,),
        in_specs=in_specs,
        out_specs=[row, key_t, row, row, key_t, row],
        out_shape=[tok, keys, tok, tok, keys, tok],
        compiler_params=_cparams("parallel"),
        name="qk_prep",
    )(da, mla, *tabs, *consts)


def _softmax_parts(q, kt):
    s = jnp.dot(q, kt, preferred_element_type=F32)
    p = jnp.exp2(s - jnp.max(s, axis=-1, keepdims=True))
    return p, 1.0 / jnp.sum(p, axis=-1, keepdims=True)


def _attn_heads(q, kt_ref, v_ref, nk, diff, lam):
    lane = lax.broadcasted_iota(jnp.int32, (q.shape[0], MIX_W), 1)
    v = v_ref[0, 0:nk, :]
    acc = jnp.zeros((q.shape[0], MIX_W), F32)
    for h in range(DA_HEADS):
        if diff:
            e0, e1 = 2 * h * DA_DIM, (2 * h + 1) * DA_DIM
            p0, r0 = _softmax_parts(q[:, e0:e0 + DA_DIM], kt_ref[0, e0:e0 + DA_DIM, 0:nk])
            p1, r1 = _softmax_parts(q[:, e1:e1 + DA_DIM], kt_ref[0, e1:e1 + DA_DIM, 0:nk])
            o = jnp.dot((p0 * r0 - p1 * (r1 * lam)).astype(BF16), v, preferred_element_type=F32)
        else:
            e0 = h * MLA_HEAD_PAD
            p, r = _softmax_parts(q[:, e0:e0 + MLA_HEAD_PAD], kt_ref[0, e0:e0 + MLA_HEAD_PAD, 0:nk])
            o = jnp.dot(p.astype(BF16), v, preferred_element_type=F32) * r
        in_head = jnp.logical_and(lane >= h * DA_VDIM, lane < (h + 1) * DA_VDIM)
        acc = jnp.where(in_head, o, acc)
    return acc


def _attn_kernel(q_ref, kt_ref, v_ref, lam_ref, gain_ref, g64_ref, o_ref, *, diff, n_ctx, n_tot, ctx_tiles):
    q = q_ref[...]
    lam = lam_ref[...]

    def run(nk):
        o = _attn_heads(q, kt_ref, v_ref, nk, diff, lam)
        if diff:
            o = _group_rms(o, g64_ref[...], 1.0 / DA_VDIM, gain_ref[...])
        o_ref[...] = o.astype(BF16)

    if ctx_tiles:
        is_ctx = pl.program_id(1) < ctx_tiles
        pl.when(is_ctx)(lambda: run(n_ctx))
        pl.when(jnp.logical_not(is_ctx))(lambda: run(n_tot))
    else:
        run(n_tot)


def _attention(lay, q, kt, v, extra, diff, with_ctx, name):
    tq = lay.t
    tiles = lay.seq_tiles if with_ctx else lay.lat_tiles
    off = 0 if with_ctx else lay.ctx_tiles
    v3 = v.reshape(lay.b, lay.n_tot, MIX_W)
    kern = functools.partial(_attn_kernel, diff=diff, n_ctx=lay.n_ctx, n_tot=lay.n_tot,
                             ctx_tiles=lay.ctx_tiles if with_ctx else 0)
    return pl.pallas_call(
        kern,
        grid=(lay.b, tiles),
        in_specs=[
            pl.BlockSpec((tq, MIX_W), lambda b, j: (b * lay.seq_tiles + j + off, 0)),
            pl.BlockSpec((1, MIX_W, lay.n_tot), lambda b, j: (b, 0, 0)),
            pl.BlockSpec((1, lay.n_tot, MIX_W), lambda b, j: (b, 0, 0)),
        ] + [_full(a) for a in extra],
        out_specs=pl.BlockSpec((tq, MIX_W), lambda b, j: (b * tiles + j, 0)),
        out_shape=jax.ShapeDtypeStruct((lay.rows(with_ctx), MIX_W), BF16),
        compiler_params=_cparams("parallel", "parallel"),
        name=name,
    )(q, kt, v3, *extra)


def _chunk_rows(ua_ref, ub_ref):
    n = ua_ref.shape[0] // S5_CHUNK
    parts = []
    for s in range(S5_CHUNK):
        rows = pl.ds(s, n, stride=S5_CHUNK)
        parts += [ua_ref[rows, :], ub_ref[rows, :]]
    return jnp.concatenate(parts, axis=1).astype(BF16)


def _s5_proj_kernel(ua_ref, ub_ref, bre_ref, bim_ref, sre_ref, sim_ref):
    u = _chunk_rows(ua_ref, ub_ref)
    sre_ref[0] = jnp.dot(u, bre_ref[0], preferred_element_type=F32)
    sim_ref[0] = jnp.dot(u, bim_ref[0], preferred_element_type=F32)


def _s5_rec_kernel(sre_ref, sim_ref, are_ref, aim_ref, hre_ref, him_ref, *, n_batch, n_chunks, ctx_chunks):
    rev = pl.program_id(0) == 1
    ar, ai = are_ref[0], aim_ref[0]
    sre, sim, hre, him = sre_ref.at[0], sim_ref.at[0], hre_ref.at[0], him_ref.at[0]

    def step(i, carry):
        hr, hi = carry
        k_rev = jnp.where(i < ctx_chunks, ctx_chunks - 1 - i, n_chunks - 1 + ctx_chunks - i)
        k = jnp.where(rev, k_rev, i)
        rows = pl.ds(k, n_batch, stride=n_chunks)
        hre[rows, :] = hr
        him[rows, :] = hi
        return ar * hr - ai * hi + sre[rows, :], ar * hi + ai * hr + sim[rows, :]

    zero = jnp.zeros((n_batch, 128), F32)
    lax.fori_loop(0, n_chunks, step, (zero, zero))


def _s5_out_kernel(ua_ref, ub_ref, hre_ref, him_ref, m_ref, cre_ref, cim_ref, ya_ref, yb_ref):
    y = jnp.dot(_chunk_rows(ua_ref, ub_ref), m_ref[0], preferred_element_type=F32)
    y = y + _split_dot(hre_ref[0], cre_ref[0]) + _split_dot(him_ref[0], cim_ref[0])
    n = y.shape[0]
    ya, yb = ya_ref.at[0], yb_ref.at[0]
    for s in range(S5_CHUNK):
        rows = pl.ds(s, n, stride=S5_CHUNK)
        ya[rows, :] = y[:, s * MIX_W:s * MIX_W + 128]
        yb[rows, :] = y[:, s * MIX_W + 128:(s + 1) * MIX_W]


def _s5_mats(lam_re, lam_im, log_dt, b_re, b_im, c_re, c_im):
    hp = lax.Precision.HIGHEST
    L, G, P, CH = S5_CHUNK, S5_GROUPS, S5_STATE, S5_CH
    lr, li = lam_re.astype(F32), lam_im.astype(F32)
    dt = jnp.exp(log_dt.astype(F32))[..., None]
    zr, zi = lr * dt, li * dt
    j = jnp.arange(L + 1, dtype=F32)[:, None, None, None]
    mag = jnp.exp(zr[None] * j)
    pw_re, pw_im = mag * jnp.cos(zi[None] * j), mag * jnp.sin(zi[None] * j)
    nr, ni = pw_re[1] - 1.0, pw_im[1]
    den = lr * lr + li * li
    cr, ci = (nr * lr + ni * li) / den, (ni * lr - nr * li) / den
    bre, bim = b_re.astype(F32), b_im.astype(F32)
    bb_re = cr[..., None] * bre - ci[..., None] * bim
    bb_im = cr[..., None] * bim + ci[..., None] * bre
    x_re = pw_re[..., None] * bb_re[None] - pw_im[..., None] * bb_im[None]
    x_im = pw_re[..., None] * bb_im[None] + pw_im[..., None] * bb_re[None]
    cre, cim = c_re.astype(F32), c_im.astype(F32)
    kern = (jnp.einsum('dgcp,jdgpe->dgjce', cre, x_re[:L], precision=hp)
            - jnp.einsum('dgcp,jdgpe->dgjce', cim, x_im[:L], precision=hp))
    def blockdiag(x):
        a, b = x.shape[-2:]
        cat = x.reshape(x.shape[:2] + (G * a, b))
        spread = jnp.asarray(np.tile(np.eye(b, dtype=np.float32), (1, G)))
        mask = jnp.asarray(np.kron(np.eye(G, dtype=np.float32), np.ones((a, b), np.float32)))
        return (jnp.einsum('djrb,bc->djrc', cat, spread, precision=hp) * mask).astype(BF16)

    kbd = blockdiag(kern.transpose(0, 2, 1, 4, 3))
    bbd_re = blockdiag(x_re.transpose(1, 0, 2, 4, 3))
    bbd_im = blockdiag(x_im.transpose(1, 0, 2, 4, 3))
    pwt_re, pwt_im = pw_re.transpose(1, 0, 2, 3)[:, :, :, :, None], pw_im.transpose(1, 0, 2, 3)[:, :, :, :, None]
    cret, cimt = cre.transpose(0, 1, 3, 2)[:, None], cim.transpose(0, 1, 3, 2)[:, None]
    cbd_re = blockdiag(cret * pwt_re - cimt * pwt_im)
    cbd_im = blockdiag(-(cret * pwt_im + cimt * pwt_re))
    s_idx, t_idx = np.arange(L)[:, None], np.arange(L)[None, :]
    m, b_r, b_i, c_r, c_i = [], [], [], [], []
    for d in range(2):
        lag = (t_idx - s_idx) if d == 0 else (s_idx - t_idx)
        blocks = jnp.where(jnp.asarray(lag >= 0)[:, :, None, None], kbd[d][np.clip(lag, 0, L - 1)], 0)
        m.append(blocks.transpose(0, 2, 1, 3).reshape(L * G * CH, L * G * CH))
        pw = np.arange(L - 1, -1, -1) if d == 0 else np.arange(L)
        b_r.append(bbd_re[d][pw].reshape(L * G * CH, G * P))
        b_i.append(bbd_im[d][pw].reshape(L * G * CH, G * P))
        q = np.arange(1, L + 1) if d == 0 else np.arange(L, 0, -1)
        c_r.append(cbd_re[d][q].transpose(1, 0, 2).reshape(G * P, L * G * CH))
        c_i.append(cbd_im[d][q].transpose(1, 0, 2).reshape(G * P, L * G * CH))
    a_re, a_im = pw_re[L].reshape(2, 1, G * P), pw_im[L].reshape(2, 1, G * P)
    return jnp.stack(m), jnp.stack(b_r), jnp.stack(b_i), jnp.stack(c_r), jnp.stack(c_i), a_re, a_im


def _s5_scan(lay, ua, ub, mats):
    m, b_r, b_i, c_r, c_i, a_re, a_im = mats
    n_chunks = lay.n_tot // S5_CHUNK
    rows = lay.b * n_chunks
    tr = min(lay.t, rows)
    tok = tr * S5_CHUNK
    half = MIX_W // 2
    wspec = lambda a: pl.BlockSpec((1,) + a.shape[1:], lambda d, i: (d, 0, 0))
    state = jax.ShapeDtypeStruct((2, rows, S5_STATE_W), F32)
    sblk = pl.BlockSpec((1, tr, S5_STATE_W), lambda d, i: (d, i, 0))
    ublk = pl.BlockSpec((tok, half), lambda d, i: (i, 0))
    s_re, s_im = pl.pallas_call(
        _s5_proj_kernel,
        grid=(2, rows // tr),
        in_specs=[ublk, ublk, wspec(b_r), wspec(b_i)],
        out_specs=[sblk, sblk],
        out_shape=[state, state],
        compiler_params=_cparams("parallel", "parallel"),
        name="s5_proj",
    )(ua, ub, b_r, b_i)
    col = pl.BlockSpec((1, rows, 128), lambda d, j: (d, 0, j))
    acol = pl.BlockSpec((1, 1, 128), lambda d, j: (d, 0, j))
    h_re, h_im = pl.pallas_call(
        functools.partial(_s5_rec_kernel, n_batch=lay.b, n_chunks=n_chunks, ctx_chunks=lay.n_ctx // S5_CHUNK),
        grid=(2, S5_STATE_W // 128),
        in_specs=[col, col, acol, acol],
        out_specs=[col, col],
        out_shape=[state, state],
        compiler_params=_cparams("parallel", "parallel"),
        name="s5_rec",
    )(s_re, s_im, a_re, a_im)
    yblk = pl.BlockSpec((1, tok, half), lambda d, i: (d, i, 0))
    yshape = jax.ShapeDtypeStruct((2, lay.b * lay.n_tot, half), F32)
    return pl.pallas_call(
        _s5_out_kernel,
        grid=(2, rows // tr),
        in_specs=[ublk, ublk, sblk, sblk, wspec(m), wspec(c_r), wspec(c_i)],
        out_specs=[yblk, yblk],
        out_shape=[yshape, yshape],
        compiler_params=_cparams("parallel", "parallel"),
        name="s5_out",
    )(ua, ub, h_re, h_im, m, c_r, c_i)


def _s5_glu_kernel(ua_ref, ub_ref, ya_ref, yb_ref, d_ref, w_ref, b_ref, o_ref):
    u = jnp.concatenate([ua_ref[...], ub_ref[...]], axis=1)
    y = d_ref[...] * u + jnp.concatenate([ya_ref[0] + ya_ref[1], yb_ref[0] + yb_ref[1]], axis=1)
    z = 0.5 * y * (1.0 + jnp.tanh(math.sqrt(2.0 / math.pi) * (y + 0.044715 * (y * y * y))))
    gate = _sigmoid(jnp.dot(z.astype(BF16), w_ref[...], preferred_element_type=F32) + b_ref[...])
    o_ref[...] = (z * gate).astype(BF16)


def _s5_glu_call(ua, ub, ya, yb, d, w, bias, tm):
    t, half = ua.shape
    urow = pl.BlockSpec((tm, half), lambda i: (i, 0))
    yrow = pl.BlockSpec((2, tm, half), lambda i: (0, i, 0))
    return pl.pallas_call(
        _s5_glu_kernel,
        grid=(t // tm,),
        in_specs=[urow, urow, yrow, yrow, _full(d), _full(w), _full(bias)],
        out_specs=pl.BlockSpec((tm, MIX_W), lambda i: (i, 0)),
        out_shape=jax.ShapeDtypeStruct((t, MIX_W), BF16),
        compiler_params=_cparams("parallel"),
        name="s5_glu",
    )(ua, ub, ya, yb, d, w, bias)


def _rw_pre_kernel(x_ref, prev_ref, next_ref, mu_ref, g64_ref, kk_g_ref, ka_ref, rk_ref,
                   w0_ref, w1_ref, w2_ref, a0_ref, a1_ref, a2_ref, g1_ref, g2_ref,
                   r_ref, v_ref, kk_ref, lw_ref, kka_ref, km_ref, bon_ref, gate_ref,
                   *, seq_tiles, ctx_tiles):
    x = x_ref[...]
    n = x.shape[0]
    j = pl.program_id(0) % seq_tiles
    starts = jnp.logical_or(j == 0, j == ctx_tiles)
    ends = jnp.logical_or(j == ctx_tiles - 1, j == seq_tiles - 1)
    prev_row = jnp.where(starts, 0.0, prev_ref[0, 7:8, :])
    next_row = jnp.where(ends, 0.0, next_ref[0, 0:1, :])
    row = lax.broadcasted_iota(jnp.int32, x.shape, 0)
    left = jnp.where(row == 0, prev_row, pltpu.roll(x, 1, axis=0))
    right = jnp.where(row == n - 1, next_row, pltpu.roll(x, n - 1, axis=0))
    x = x + (0.5 * (left + right) - x) * mu_ref[...]
    r, k, v, xd = (x[:, i * MIX_W:(i + 1) * MIX_W] for i in range(4))
    g64 = g64_ref[...]
    kscaled = k * kk_g_ref[...]
    kk = kscaled / jnp.maximum(jnp.sqrt(_split_dot(kscaled * kscaled, g64)), 1e-12)
    xdb = xd.astype(BF16)
    r_ref[...] = r
    v_ref[...] = v
    kk_ref[...] = kk
    km_sum = None
    for d in range(2):
        lo = jnp.tanh(jnp.dot(xdb, w1_ref[d], preferred_element_type=F32))
        w_raw = w0_ref[d] + jnp.dot(lo.astype(BF16), w2_ref[d], preferred_element_type=F32)
        lw_ref[d] = -_sigmoid(w_raw) * math.exp(-0.5)
        ar = jnp.dot(xdb, a1_ref[d], preferred_element_type=F32)
        a = _sigmoid(a0_ref[d] + jnp.dot(ar.astype(BF16), a2_ref[d], preferred_element_type=F32))
        km = k * (1.0 + (a - 1.0) * ka_ref[...])
        kka_ref[d] = kk * a
        km_ref[d] = km
        km_sum = km if km_sum is None else km_sum + km
    bon_ref[...] = _split_dot(r * km_sum * rk_ref[...], g64) * v
    gr = _sigmoid(jnp.dot(xdb, g1_ref[...], preferred_element_type=F32))
    gate_ref[...] = jnp.dot(gr.astype(BF16), g2_ref[...], preferred_element_type=F32)


def _rw_pre_call(lay, rw, consts):
    t, tr = rw.shape[0], lay.t
    nt = t // tr
    g8 = tr // 8
    rw8 = rw.reshape(t // 8, 8, _RW_W)
    row = pl.BlockSpec((tr, MIX_W), lambda i: (i, 0))
    row2 = pl.BlockSpec((2, tr, MIX_W), lambda i: (0, i, 0))
    sd = jax.ShapeDtypeStruct((t, MIX_W), F32)
    sd2 = jax.ShapeDtypeStruct((2, t, MIX_W), F32)
    return pl.pallas_call(
        functools.partial(_rw_pre_kernel, seq_tiles=lay.seq_tiles, ctx_tiles=lay.ctx_tiles),
        grid=(nt,),
        in_specs=[pl.BlockSpec((tr, _RW_W), lambda i: (i, 0)),
                  pl.BlockSpec((1, 8, _RW_W), lambda i: (jnp.maximum(i * g8 - 1, 0), 0, 0)),
                  pl.BlockSpec((1, 8, _RW_W), lambda i: (jnp.minimum((i + 1) * g8, t // 8 - 1), 0, 0))]
                 + [_full(a) for a in consts],
        out_specs=[row, row, row, row2, row2, row2, row, row],
        out_shape=[sd, sd, sd, sd2, sd2, sd2, sd, sd],
        compiler_params=_cparams("parallel"),
        name="rwkv_pre",
    )(rw, rw8, rw8, *consts)


def _head_masks(shape, lane_axis, seg):
    lane = lax.broadcasted_iota(jnp.int32, shape, lane_axis)
    return [jnp.logical_and(lane >= h * seg, lane < (h + 1) * seg) for h in range(RW_HEADS)]


def _rw_prep_kernel(*refs, rev):
    for sub in range(RW_PREP_TILES):
        _rw_prep_tile(sub, *refs, rev=rev)


def _rw_prep_tile(sub, r_ref, kk_ref, v_ref, lw_ref, ka_ref, km_ref, perm_ref, permt_ref, g_ref, eye_ref,
                  br_ref, ck_ref, uvt_ref, y0_ref, pc_ref, *, rev):
    C, NC = RW_CHUNK, RW_TILE // RW_CHUNK
    perm, permt, g64, eye4 = perm_ref[...], permt_ref[...], g_ref[...], eye_ref[...]
    tok = slice(sub * RW_TILE, (sub + 1) * RW_TILE)
    nat = jnp.concatenate([r_ref[0, tok, :], kk_ref[0, tok, :], v_ref[0, tok, :],
                           lw_ref[0, 0, tok, :], ka_ref[0, 0, tok, :], km_ref[0, 0, tok, :]], axis=1)
    hi = nat.astype(BF16)
    lo = (nat - hi.astype(F32)).astype(BF16)
    pm = jnp.dot(perm, hi, preferred_element_type=F32) + jnp.dot(perm, lo, preferred_element_type=F32)
    r, kk, v, lw, ka, km = (pm[:, i * MIX_W:(i + 1) * MIX_W] for i in range(6))
    slab = lambda x, j: x[j * NC:(j + 1) * NC, :]
    order = list(range(C))[::-1] if rev else list(range(C))
    pos = {j: i for i, j in enumerate(order)}
    cum, run = {}, None
    for j in order:
        run = slab(lw, j) if run is None else run + slab(lw, j)
        cum[j] = run
    tot = run
    bh, ch, kh, rh, cp, kp, vv = {}, {}, {}, {}, {}, {}, {}
    for j in range(C):
        e_inv, e_end = jnp.exp(-cum[j]), jnp.exp(tot - cum[j])
        bh[j] = -slab(kk, j) * jnp.exp(cum[j] - slab(lw, j))
        ch[j], kh[j] = slab(ka, j) * e_inv, slab(km, j) * e_inv
        rh[j] = slab(r, j) * jnp.exp(cum[j])
        cp[j], kp[j] = slab(ka, j) * e_end, slab(km, j) * e_end
        vv[j] = slab(v, j)
    strict = [(t, s) for t in order for s in order if pos[s] < pos[t]]
    incl = [(t, s) for t in order for s in order if pos[s] <= pos[t]]
    def head_dots(lhs, rhs, pairs):
        prods = jnp.concatenate([lhs[t] * rhs[s] for t, s in pairs], axis=0).astype(BF16)
        gram = jnp.dot(prods, g64, preferred_element_type=F32)
        return {p: gram[i * NC:(i + 1) * NC, :] for i, p in enumerate(pairs)}

    acb, akb = head_dots(bh, ch, strict), head_dots(bh, kh, strict)
    mcr, mkr = head_dots(rh, ch, incl), head_dots(rh, kh, incl)
    bt, u0 = {}, {}
    for t in order:
        b_acc, u_acc = bh[t], jnp.zeros_like(bh[t])
        for s in order:
            if pos[s] < pos[t]:
                b_acc = b_acc + acb[(t, s)] * bt[s]
                u_acc = u_acc + akb[(t, s)] * vv[s] + acb[(t, s)] * u0[s]
        bt[t], u0[t] = b_acc, u_acc
    rt, y0 = {}, {}
    for t in order:
        r_acc, y_acc = rh[t], jnp.zeros_like(rh[t])
        for s in order:
            if pos[s] <= pos[t]:
                r_acc = r_acc + mcr[(t, s)] * bt[s]
                y_acc = y_acc + mcr[(t, s)] * u0[s] + mkr[(t, s)] * vv[s]
        rt[t], y0[t] = r_acc, y_acc
    stackp = lambda dct: jnp.concatenate([dct[j] for j in range(C)], axis=0)
    b16 = lambda x: x.astype(BF16)
    y0p, u0p = stackp(y0), stackp(u0)
    y0h, u0h = b16(y0p), b16(u0p)
    cat = jnp.concatenate([b16(stackp(bt)), b16(stackp(rt)), b16(stackp(cp)), b16(stackp(kp)),
                           y0h, b16(y0p - y0h.astype(F32)), u0h, b16(u0p - u0h.astype(F32)), b16(stackp(vv))], axis=1)
    natural = jnp.dot(permt, cat, preferred_element_type=F32)
    seg = lambda i: natural[:, i * MIX_W:(i + 1) * MIX_W]
    btn, rtn, cpn, kpn = b16(seg(0)), b16(seg(1)), b16(seg(2)), b16(seg(3))
    y0_ref[0, tok, :] = seg(4) + seg(5)
    u0h_n, u0l_n, vn = b16(seg(6)), b16(seg(7)), b16(seg(8))
    hm = _head_masks((C, MIX_W), 1, RW_DIM)
    zero = jnp.zeros((C, MIX_W), BF16)
    zh, zl = [], []
    for c in range(NC):
        rows = slice(c * C, (c + 1) * C)
        br_ref[0, sub * NC + c, 0:C, :] = btn[rows]
        br_ref[0, sub * NC + c, C:2 * C, :] = rtn[rows]
        ck_ref[0, sub * NC + c, 0:C, :] = cpn[rows]
        ck_ref[0, sub * NC + c, C:2 * C, :] = kpn[rows]
        for h in range(RW_HEADS):
            zh += [jnp.where(hm[h], u0h_n[rows], zero), jnp.where(hm[h], vn[rows], zero)]
            zl += [jnp.where(hm[h], u0l_n[rows], zero), zero]
    uvt = (lax.dot_general(eye4, jnp.concatenate(zh, axis=0), _NT, preferred_element_type=F32)
           + lax.dot_general(eye4, jnp.concatenate(zl, axis=0), _NT, preferred_element_type=F32))
    for c in range(NC):
        uvt_ref[0, sub * NC + c] = uvt[:, c * 2 * C * RW_HEADS:(c + 1) * 2 * C * RW_HEADS]
    pc_ref[0, sub * NC:(sub + 1) * NC, :] = jnp.exp(tot)


def _rw_prep_call(lay, shared, perdir, consts, rev):
    b, n_tot, tt = lay.b, lay.n_tot, RW_TILE * RW_PREP_TILES
    assert n_tot % tt == 0
    nck = n_tot // RW_CHUNK
    cpt = tt // RW_CHUNK
    d = 1 if rev else 0
    sh = [a.reshape(b, n_tot, MIX_W) for a in shared]
    pd = [a.reshape(2, b, n_tot, MIX_W) for a in perdir]
    tok = pl.BlockSpec((1, tt, MIX_W), lambda i, j: (i, j, 0))
    tok_d = pl.BlockSpec((1, 1, tt, MIX_W), lambda i, j: (d, i, j, 0))
    rows32 = pl.BlockSpec((1, cpt, 2 * RW_CHUNK, MIX_W), lambda i, j: (i, j, 0, 0))
    return pl.pallas_call(
        functools.partial(_rw_prep_kernel, rev=rev),
        grid=(b, n_tot // tt),
        in_specs=[tok] * 3 + [tok_d] * 3 + [_full(a) for a in consts],
        out_specs=[rows32, rows32,
                   pl.BlockSpec((1, cpt, RW_DIM, 2 * RW_CHUNK * RW_HEADS), lambda i, j: (i, j, 0, 0)),
                   tok,
                   pl.BlockSpec((1, cpt, MIX_W), lambda i, j: (i, j, 0))],
        out_shape=[jax.ShapeDtypeStruct((b, nck, 2 * RW_CHUNK, MIX_W), BF16),
                   jax.ShapeDtypeStruct((b, nck, 2 * RW_CHUNK, MIX_W), BF16),
                   jax.ShapeDtypeStruct((b, nck, RW_DIM, 2 * RW_CHUNK * RW_HEADS), F32),
                   jax.ShapeDtypeStruct((b, n_tot, MIX_W), F32),
                   jax.ShapeDtypeStruct((b, nck, MIX_W), F32)],
        compiler_params=_cparams("parallel", "parallel"),
        name="rwkv_prep_rev" if rev else "rwkv_prep_fwd",
    )(*sh, *pd, *consts)


def _rw_scan_kernel(brf, ckf, uvtf, pcf, brr, ckr, uvtr, pcr, ytf_ref, ytr_ref, s_scr, *, n_batch):
    @pl.when(pl.program_id(0) == 0)
    def _():
        s_scr[...] = jnp.zeros_like(s_scr)

    cpt = RW_TILE // RW_CHUNK
    hm = _head_masks((2 * RW_CHUNK, MIX_W), 1, RW_DIM)
    lane = lax.broadcasted_iota(jnp.int32, (RW_DIM, 2 * RW_CHUNK * RW_HEADS), 1)
    is_u = (lane & (2 * RW_CHUNK - 1)) < RW_CHUNK
    per_head = lambda x: jnp.concatenate([jnp.where(m, x, jnp.zeros_like(x)) for m in hm], axis=0)

    def step(c, carry):
        for p in range(2 * n_batch):
            d, b = divmod(p, n_batch)
            br_ref, ck_ref, uvt_ref, pc_ref, yt_ref = (brf, ckf, uvtf, pcf, ytf_ref) if d == 0 else (brr, ckr, uvtr, pcr, ytr_ref)
            cc = c if d == 0 else cpt - 1 - c
            s = s_scr[p]
            shi = s.astype(BF16)
            slo = (s - shi.astype(F32)).astype(BF16)
            w2 = lax.dot_general(jnp.concatenate([shi, slo], axis=0), per_head(br_ref[b, cc]), _NT, preferred_element_type=F32)
            w = w2[:RW_DIM] + w2[RW_DIM:]
            yt_ref[b, cc] = w
            uvt = uvt_ref[b, cc]
            lhs = jnp.where(is_u, w + uvt, uvt).astype(BF16)
            s_scr[p] = s * pc_ref[b, pl.ds(cc, 1), :] + jnp.dot(lhs, per_head(ck_ref[b, cc]), preferred_element_type=F32)
        return carry

    lax.fori_loop(0, cpt, step, 0)


def _rw_scan_call(lay, fwd, rev):
    b, n_tot, tt = lay.b, lay.n_tot, RW_TILE
    assert lay.n_ctx % tt == 0 and lay.n_lat % tt == 0
    nt, ct = n_tot // tt, lay.n_ctx // tt
    cpt = tt // RW_CHUNK
    rev_tile = lambda i: jnp.where(i < ct, ct - 1 - i, nt - 1 + ct - i)

    def specs(tile):
        return [pl.BlockSpec((b, cpt, 2 * RW_CHUNK, MIX_W), lambda i: (0, tile(i), 0, 0)),
                pl.BlockSpec((b, cpt, 2 * RW_CHUNK, MIX_W), lambda i: (0, tile(i), 0, 0)),
                pl.BlockSpec((b, cpt, RW_DIM, 2 * RW_CHUNK * RW_HEADS), lambda i: (0, tile(i), 0, 0)),
                pl.BlockSpec((b, cpt, MIX_W), lambda i: (0, tile(i), 0))]

    ident = lambda i: i
    yt = jax.ShapeDtypeStruct((b, n_tot // RW_CHUNK, RW_DIM, 2 * RW_CHUNK * RW_HEADS), F32)
    return pl.pallas_call(
        functools.partial(_rw_scan_kernel, n_batch=b),
        grid=(nt,),
        in_specs=specs(ident) + specs(rev_tile),
        out_specs=[specs(ident)[2], specs(rev_tile)[2]],
        out_shape=[yt, yt],
        scratch_shapes=[pltpu.VMEM((2 * b, RW_DIM, MIX_W), F32)],
        compiler_params=_cparams("arbitrary"),
        name="rwkv_scan",
    )(*fwd, *rev)


def _rw_fin_kernel(ytf_ref, ytr_ref, y0f_ref, y0r_ref, bon_ref, gate_ref, asel_ref, g64_ref, lng_ref, lnb_ref, o_ref):
    cpt = RW_TILE // RW_CHUNK
    asel = asel_ref[...]
    width = cpt * 2 * RW_CHUNK * RW_HEADS
    lane = lax.broadcasted_iota(jnp.int32, (RW_DIM, width), 1)
    lane_head = jnp.bitwise_and(jnp.right_shift(lane, 5), RW_HEADS - 1)

    def base(yt_ref):
        yt = jnp.concatenate([yt_ref[0, c] for c in range(cpt)], axis=1).astype(BF16)
        rows = jnp.concatenate([jnp.where(lane_head == h, yt, jnp.zeros_like(yt)) for h in range(RW_HEADS)], axis=0)
        return lax.dot_general(asel, rows, _NT, preferred_element_type=F32)

    y = base(ytf_ref) + y0f_ref[0] + base(ytr_ref) + y0r_ref[0]
    g64 = g64_ref[...]
    mean = _split_dot(y, g64) * (1.0 / RW_DIM)
    c = y - mean
    var = _split_dot(c * c, g64) * (1.0 / RW_DIM)
    out = c * lax.rsqrt(var + RW_LN_EPS) * lng_ref[...] + lnb_ref[...] + bon_ref[...]
    o_ref[...] = (out * gate_ref[...]).astype(BF16)


def _rw_fin_call(lay, ytf, ytr, y0f, y0r, bon, gate, consts):
    b, n_tot, tt = lay.b, lay.n_tot, RW_TILE
    nt = n_tot // tt
    cpt = tt // RW_CHUNK
    ytb = pl.BlockSpec((1, cpt, RW_DIM, 2 * RW_CHUNK * RW_HEADS), lambda i, j: (i, j, 0, 0))
    y0b = pl.BlockSpec((1, tt, MIX_W), lambda i, j: (i, j, 0))
    row = pl.BlockSpec((tt, MIX_W), lambda i, j: (i * nt + j, 0))
    return pl.pallas_call(
        _rw_fin_kernel,
        grid=(b, nt),
        in_specs=[ytb, ytb, y0b, y0b, row, row] + [_full(a) for a in consts],
        out_specs=row,
        out_shape=jax.ShapeDtypeStruct((b * n_tot, MIX_W), BF16),
        compiler_params=_cparams("parallel", "parallel"),
        name="rwkv_finish",
    )(ytf, ytr, y0f, y0r, bon, gate, *consts)


def _rw_constants():
    c, nc = RW_CHUNK, RW_TILE // RW_CHUNK
    perm = np.zeros((RW_TILE, RW_TILE), np.float32)
    for ci in range(nc):
        for j in range(c):
            perm[j * nc + ci, ci * c + j] = 1.0
    lane = np.arange(MIX_W) % RW_DIM
    eye4 = (lane[None, :] == np.arange(RW_DIM)[:, None]).astype(np.float32)
    lanes = np.arange(nc * 2 * c * RW_HEADS)
    lane_chunk, lane_tok = lanes // (2 * c * RW_HEADS), lanes % (2 * c)
    t = np.arange(RW_TILE)
    asel = ((lane_chunk[None, :] == (t // c)[:, None]) & (lane_tok[None, :] == (c + t % c)[:, None])).astype(np.float32)
    as16 = lambda a: jnp.asarray(a, BF16)
    return as16(perm), as16(perm.T), as16(eye4), as16(asel)


def _rwkv_branch(lay, rw, pre_consts, g64, lng, lnb):
    r_, v_, kk_, lw_, kka_, km_, bon, gate = _rw_pre_call(lay, rw, pre_consts)
    perm, permt, eye4, asel = _rw_constants()
    prep_consts = (perm, permt, g64, eye4)
    fwd = _rw_prep_call(lay, (r_, kk_, v_), (lw_, kka_, km_), prep_consts, False)
    rev = _rw_prep_call(lay, (r_, kk_, v_), (lw_, kka_, km_), prep_consts, True)
    pick = lambda o: (o[0], o[1], o[2], o[4])
    ytf, ytr = _rw_scan_call(lay, pick(fwd), pick(rev))
    return _rw_fin_call(lay, ytf, ytr, fwd[3], rev[3], bon, gate, (asel, g64, lng, lnb))


def _merge_kernel(x_ref, mod_ref, g_ref, wg_ref, ya_ref, yb_ref, yc_ref, yd_ref, wb_ref, wo_ref, o_ref):
    x = x_ref[...]
    h = _modulate(x, g_ref[...], mod_ref[0, 0:1, :], mod_ref[0, 1:2, :]).astype(BF16)
    merged = None
    for i, y_ref in enumerate((ya_ref, yb_ref, yc_ref, yd_ref)):
        gate = _sigmoid(jnp.dot(h, wg_ref[:, i * D_MODEL:(i + 1) * D_MODEL], preferred_element_type=F32))
        term = gate * jnp.dot(y_ref[...], wb_ref[i], preferred_element_type=F32)
        merged = term if merged is None else merged + term
    out = jnp.dot(merged.astype(BF16), wo_ref[...], preferred_element_type=F32)
    o_ref[...] = x + mod_ref[0, 2:3, :] * out


def _merge_call(lay, with_ctx, x_all, mod, g, w_gate, ya, yb, yc, yd, w_branch, w_out):
    tm = lay.t
    src, mrow = lay.src_tile(with_ctx), lay.mod_row(with_ctx)
    full_row = lambda w: pl.BlockSpec((tm, w), lambda i: (src(i), 0))
    out_row = lambda w: pl.BlockSpec((tm, w), lambda i: (i, 0))
    return pl.pallas_call(
        _merge_kernel,
        grid=(lay.n_tiles(with_ctx),),
        in_specs=[full_row(D_MODEL), pl.BlockSpec((1, 6, D_MODEL), lambda i: (mrow(i), 0, 0)), _full(g), _full(w_gate),
                  out_row(MIX_W), full_row(MIX_W), out_row(MIX_W), full_row(MIX_W), _full(w_branch), _full(w_out)],
        out_specs=out_row(D_MODEL),
        out_shape=jax.ShapeDtypeStruct((lay.rows(with_ctx), D_MODEL), F32),
        compiler_params=_cparams("parallel"),
        name="merge_out",
    )(x_all, mod, g, w_gate, ya, yb, yc, yd, w_branch, w_out)


def _router_kernel(x_ref, mod_ref, g_ref, wh_ref, wl_ref, bias_ref, f_ref, comb_ref, gid_ref):
    f = _modulate(x_ref[...], g_ref[...], mod_ref[0, 3:4, :], mod_ref[0, 4:5, :])
    fh = f.astype(BF16)
    f_ref[...] = fh
    fl = (f - fh.astype(F32)).astype(BF16)
    nt = (((1,), (1,)), ((), ()))
    wh, wl = wh_ref[...], wl_ref[...]
    logits = (lax.dot_general(wh, fh, nt, preferred_element_type=F32)
              + lax.dot_general(wh, fl, nt, preferred_element_type=F32)
              + lax.dot_general(wl, fh, nt, preferred_element_type=F32))
    scores = _sigmoid(logits)
    biased = scores + bias_ref[...]
    sc = [scores[e:e + 1, :] for e in range(N_EXPERTS)]
    bi = [biased[e:e + 1, :] for e in range(N_EXPERTS)]
    group_score = []
    for g in range(N_GROUPS):
        a, b, c, d = bi[4 * g:4 * g + 4]
        m1, n1, m2, n2 = jnp.maximum(a, b), jnp.minimum(a, b), jnp.maximum(c, d), jnp.minimum(c, d)
        group_score.append(jnp.maximum(m1, m2) + jnp.maximum(jnp.minimum(m1, m2), jnp.maximum(n1, n2)))

    def first_argmax(vals):
        top = functools.reduce(jnp.maximum, vals)
        seen, hot = None, []
        for v in vals:
            h = v == top
            if seen is not None:
                h = jnp.logical_and(h, jnp.logical_not(seen))
            seen = h if seen is None else jnp.logical_or(seen, h)
            hot.append(h)
        return hot

    in_group = first_argmax(group_score)
    masked = [jnp.where(in_group[e // EXPERTS_PER_GROUP], bi[e], -jnp.inf) for e in range(N_EXPERTS)]
    hot1 = first_argmax(masked)
    hot2 = first_argmax([jnp.where(h, -jnp.inf, v) for h, v in zip(hot1, masked)])
    w1 = functools.reduce(jnp.add, [jnp.where(h, s, 0.0) for h, s in zip(hot1, sc)])
    w2 = functools.reduce(jnp.add, [jnp.where(h, s, 0.0) for h, s in zip(hot2, sc)])
    inv_tot = 1.0 / (w1 + w2)
    for e in range(N_EXPERTS):
        comb_ref[e:e + 1, :] = (jnp.where(hot1[e], w1, 0.0) + jnp.where(hot2[e], w2, 0.0)) * inv_tot
    gid_ref[...] = functools.reduce(jnp.add, [jnp.where(in_group[g], g, 0) for g in range(1, N_GROUPS)])


def _router_call(lay, with_ctx, x, mod, g, wh, wl, bias):
    t, tm = x.shape[0], lay.t
    mrow = lay.mod_row(with_ctx)
    return pl.pallas_call(
        _router_kernel,
        grid=(t // tm,),
        in_specs=[pl.BlockSpec((tm, D_MODEL), lambda i: (i, 0)),
                  pl.BlockSpec((1, 6, D_MODEL), lambda i: (mrow(i), 0, 0)), _full(g), _full(wh), _full(wl), _full(bias)],
        out_specs=[pl.BlockSpec((tm, D_MODEL), lambda i: (i, 0)), pl.BlockSpec((N_EXPERTS, tm), lambda i: (0, i)),
                   pl.BlockSpec((1, tm), lambda i: (0, i))],
        out_shape=[jax.ShapeDtypeStruct((t, D_MODEL), BF16), jax.ShapeDtypeStruct((N_EXPERTS, t), F32),
                   jax.ShapeDtypeStruct((1, t), jnp.int32)],
        compiler_params=_cparams("parallel"),
        name="moe_router",
    )(x, mod, g, wh, wl, bias)


def _moe_plan(gid, n_tiles, tm):
    g = gid.reshape(n_tiles, tm)
    onehot = (g[..., None] == jnp.arange(N_GROUPS, dtype=jnp.int32)).astype(jnp.int32)
    rank = jnp.cumsum(onehot, axis=1) - onehot
    counts = jnp.sum(onehot, axis=1)
    padded = (counts + 15) // 16 * 16
    offs = jnp.cumsum(padded, axis=1) - padded
    pos = jnp.sum(onehot * (offs[:, None, :] + rank), axis=-1)
    n_over = (jnp.maximum(padded - MOE_BLOCK, 0) + MOE_OVER - 1) // MOE_OVER
    return pos.astype(jnp.int32), offs.astype(jnp.int32), n_over.astype(jnp.int32)


def _moe_kernel(offs_ref, nover_ref, f_ref, posr_ref, posc_ref, comb_ref, wg_ref, wu_ref, wd_ref, x_ref, modb_ref,
                modc_ref, o_ref, xs_scr, cs_scr, ys_scr, *, ctx_rows, tiles_per_seq):
    i, e = pl.program_id(0), pl.program_id(1)
    n_slots, tm = xs_scr.shape[0], f_ref.shape[0]

    @pl.when(e == 0)
    def _():
        slot = lax.broadcasted_iota(jnp.int32, (n_slots, tm), 0)
        place = (slot == posr_ref[0]).astype(BF16)
        xs_scr[...] = jnp.dot(place, f_ref[...], preferred_element_type=F32).astype(BF16)
        cs_scr[...] = _split_dot_rhs(place, comb_ref[...])
        ys_scr[...] = jnp.zeros_like(ys_scr)

    grp = lax.shift_right_logical(e, 2)
    start = offs_ref[i, grp]
    lane = lax.broadcasted_iota(jnp.int32, (1, N_EXPERTS), 1)

    def run(rows):
        xb = xs_scr[rows, :]
        gate = jnp.dot(xb, wg_ref[0], preferred_element_type=F32)
        up = jnp.dot(xb, wu_ref[0], preferred_element_type=F32)
        act = (gate * _sigmoid(gate) * up).astype(BF16)
        down = jnp.dot(act, wd_ref[0], preferred_element_type=F32)
        c_e = jnp.sum(jnp.where(lane == e, cs_scr[rows, :], 0.0), axis=1, keepdims=True)
        ys_scr[rows, :] += c_e * down

    run(pl.ds(pl.multiple_of(start, 16), MOE_BLOCK))

    def overflow(k, carry):
        run(pl.ds(pl.multiple_of(start + MOE_BLOCK + k * MOE_OVER, 16), MOE_OVER))
        return carry

    lax.fori_loop(0, nover_ref[i, grp], overflow, 0)

    @pl.when(e == N_EXPERTS - 1)
    def _():
        slot = lax.broadcasted_iota(jnp.int32, (tm, n_slots), 1)
        fetch = (slot == posc_ref[...]).astype(BF16)
        y = jnp.dot(fetch, ys_scr[...].astype(BF16), preferred_element_type=F32)
        res_gate = modb_ref[0, 5:6, :]
        if ctx_rows:
            row = lax.broadcasted_iota(jnp.int32, y.shape, 0)
            first = i % tiles_per_seq == 0
            res_gate = jnp.where(jnp.logical_and(first, row < ctx_rows), modc_ref[0, 5:6, :], res_gate)
        o_ref[...] = x_ref[...] + res_gate * y


def _moe_call(lay, with_ctx, f, comb, gid, wg, wu, wd, x, mod):
    t = f.shape[0]
    seq = lay.n_tot if with_ctx else lay.n_lat
    tm = MOE_TILE if seq % MOE_TILE == 0 else math.gcd(seq, 1024)
    tps, n_tiles = seq // tm, t // tm
    ctx_rows = lay.n_ctx if with_ctx else 0
    assert ctx_rows <= tm
    n_slots = -(-(tm + 16 * N_GROUPS + MOE_BLOCK + MOE_OVER) // 256) * 256
    pos, offs, n_over = _moe_plan(gid, n_tiles, tm)
    wspec = lambda a: pl.BlockSpec((1,) + a.shape[1:], lambda i, e, *_: (e, 0, 0))
    tok = lambda w: pl.BlockSpec((tm, w), lambda i, e, *_: (i, 0))
    grid_spec = pltpu.PrefetchScalarGridSpec(
        num_scalar_prefetch=2,
        grid=(n_tiles, N_EXPERTS),
        in_specs=[tok(D_MODEL), pl.BlockSpec((1, 1, tm), lambda i, e, *_: (i, 0, 0)), tok(1), tok(N_EXPERTS),
                  wspec(wg), wspec(wu), wspec(wd), tok(D_MODEL),
                  pl.BlockSpec((1, 6, D_MODEL), lambda i, e, *_: (i // tps, 0, 0)),
                  pl.BlockSpec((1, 6, D_MODEL), lambda i, e, *_: (lay.b, 0, 0))],
        out_specs=tok(D_MODEL),
        scratch_shapes=[pltpu.VMEM((n_slots, D_MODEL), BF16), pltpu.VMEM((n_slots, N_EXPERTS), F32),
                        pltpu.VMEM((n_slots, D_MODEL), F32)])
    return pl.pallas_call(
        functools.partial(_moe_kernel, ctx_rows=ctx_rows, tiles_per_seq=tps),
        grid_spec=grid_spec,
        out_shape=jax.ShapeDtypeStruct((t, D_MODEL), F32),
        compiler_params=_cparams("parallel", "arbitrary"),
        name="moe_experts",
    )(offs, n_over, f, pos.reshape(n_tiles, 1, tm), pos.reshape(t, 1), comb, wg, wu, wd, x, mod, mod)


def _block_ones(n, group):
    i = np.arange(n) // group
    return jnp.asarray(i[:, None] == i[None, :], dtype=BF16)


def _rope_tables(n_ctx, n_lat):
    rows = n_lat // GRID_W
    row = jnp.repeat(jnp.arange(rows, dtype=F32), GRID_W)
    col = jnp.tile(jnp.arange(GRID_W, dtype=F32), rows)

    def angles(rot_dim):
        n_freq = rot_dim // 4
        inv_freq = ROPE_BASE ** (-jnp.arange(n_freq, dtype=F32) / n_freq)
        ang = jnp.concatenate([row[:, None] * inv_freq, col[:, None] * inv_freq], axis=-1)
        return jnp.cos(ang), jnp.sin(ang)

    c, s = angles(DA_DIM)
    cda = jnp.tile(jnp.concatenate([c, c], -1), (1, 2 * DA_HEADS))
    sda = jnp.tile(jnp.concatenate([-s, s], -1), (1, 2 * DA_HEADS))
    c, s = angles(MLA_ROPE)
    one = jnp.ones((n_lat, MLA_NOPE), F32)
    pad = MLA_HEAD_PAD - MLA_NOPE - MLA_ROPE
    cml = jnp.tile(jnp.concatenate([one, c, c, jnp.ones((n_lat, pad), F32)], -1), (1, MLA_HEADS))
    sml = jnp.tile(jnp.concatenate([0 * one, -s, s, jnp.zeros((n_lat, pad), F32)], -1), (1, MLA_HEADS))
    ident = lambda t, v: jnp.concatenate([jnp.full((n_ctx, MIX_W), v, F32), t], axis=0)
    return ident(cda, 1.0), ident(sda, 0.0), ident(cml, 1.0), ident(sml, 0.0)


def _pad_heads(w, n_heads, src_w, lo, hi, dst_w=MLA_HEAD_PAD):
    w = w.reshape(w.shape[0], n_heads, src_w)[:, :, lo:hi]
    w = jnp.pad(w, ((0, 0), (0, 0), (0, dst_w - (hi - lo))))
    return w.reshape(w.shape[0], n_heads * dst_w)


def _mix_weight(w_in_l):
    w = w_in_l
    kr = w[:, 1344:1360]
    z = lambda n: jnp.zeros((D_MODEL, n), w.dtype)
    kr_wide = jnp.concatenate([jnp.concatenate([z(MLA_NOPE), kr, z(MLA_HEAD_PAD - MLA_NOPE - MLA_ROPE)], 1)] * MLA_HEADS, 1)
    return jnp.concatenate([w[:, 0:1024], w[:, 1024:1216], z(64), w[:, 1216:1344], kr_wide, w[:, 1360:2384]], axis=1).astype(BF16)


def kernel(x, c, ctx, c_ctx, w_ada, b_ada, norm_mix_g, norm_ffn_g, w_in, da_qk_norm_g, da_lambda, da_subln_g, s5_lam_re, s5_lam_im, s5_log_dt, s5_b_re, s5_b_im, s5_c_re, s5_c_im, s5_d, s5_w_glu, s5_b_glu, mla_cq_norm_g, mla_ckv_norm_g, mla_w_uq, mla_w_ukv, mla_qk_norm_g, rw_mu, rw_w0, rw_w1, rw_w2, rw_a0, rw_a1, rw_a2, rw_g1, rw_g2, rw_k_k, rw_k_a, rw_r_k, rw_ln_g, rw_ln_b, w_branch, w_out, router_w, router_bias, exp_w_gate, exp_w_up, exp_w_down):
    b, n_lat, dm = x.shape
    n_ctx = ctx.shape[1]
    depth = w_ada.shape[0]
    assert dm == D_MODEL
    lay = _Layout(b, n_ctx, n_lat)
    t_all = b * lay.n_tot
    tm_big = 2 * lay.t

    g32 = _block_ones(MIX_W, DA_DIM)
    g64 = _block_ones(MIX_W, RW_DIM)
    tabs = _rope_tables(n_ctx, n_lat)
    row = lambda v: v.reshape(1, -1).astype(F32)
    bf = lambda a: a.astype(BF16)

    cc = jnp.zeros((16, dm), F32).at[:b].set(c).at[b].set(c_ctx)
    mod_all = _ada_call(cc, w_ada, b_ada)
    x_all = jnp.concatenate([ctx, x], axis=1).reshape(t_all, dm)

    wr_hi = router_w.T.astype(BF16)
    wr_lo = (router_w.T - wr_hi.astype(F32)).astype(BF16)
    r_bias = router_bias.reshape(N_EXPERTS, 1).astype(F32)

    for l in range(depth):
        need_ctx = l < depth - 1
        lambda_init = 0.8 - 0.6 * math.exp(-0.3 * l)
        mod = mod_all[l, :b + 1].reshape(b + 1, 6, dm)
        g_mix = row(norm_mix_g[l])
        da, s5a, s5b, mla, rw = _inproj_call(lay, x_all, mod, g_mix, _mix_weight(w_in[l]))

        log2e = math.log2(math.e)
        gda = jnp.stack([jnp.tile(da_qk_norm_g[l, 0], 2 * DA_HEADS) * (DA_DIM ** -0.5 * log2e), jnp.tile(da_qk_norm_g[l, 1], 2 * DA_HEADS)])
        mla_pad = MLA_HEAD_PAD - MLA_NOPE - MLA_ROPE
        gml = jnp.stack([jnp.tile(jnp.pad(mla_qk_norm_g[l, 0], (0, mla_pad)), MLA_HEADS) * ((MLA_NOPE + MLA_ROPE) ** -0.5 * log2e),
                         jnp.tile(jnp.pad(mla_qk_norm_g[l, 1], (0, mla_pad)), MLA_HEADS)])
        wuq = bf(jnp.pad(_pad_heads(mla_w_uq[l], MLA_HEADS, MLA_NOPE + MLA_ROPE, 0, MLA_NOPE + MLA_ROPE), ((0, 64), (0, 0))))
        wuk = bf(_pad_heads(mla_w_ukv[l], MLA_HEADS, MLA_NOPE + MLA_VDIM, 0, MLA_NOPE))
        wuv = bf(_pad_heads(mla_w_ukv[l], MLA_HEADS, MLA_NOPE + MLA_VDIM, MLA_NOPE, MLA_NOPE + MLA_VDIM))
        consts = (g32, g64, gda.astype(F32), gml.astype(F32), row(jnp.pad(mla_cq_norm_g[l], (0, 64))), row(mla_ckv_norm_g[l]),
                  wuq, wuk, wuv)
        qd, kdt, vd, qm, kmt, vm = _qkprep_call(lay, da, mla, tabs, consts)

        lam32 = da_lambda[l].astype(F32)
        lmbda = (jnp.exp(jnp.sum(lam32[0] * lam32[1])) - jnp.exp(jnp.sum(lam32[2] * lam32[3])) + lambda_init).reshape(1, 1)
        subln = row(jnp.tile(da_subln_g[l], DA_HEADS) * (1.0 - lambda_init))
        ya = _attention(lay, qd, kdt, vd, (lmbda, subln, g64), True, need_ctx, "diff_attn")
        yc = _attention(lay, qm, kmt, vm, (lmbda, subln, g64), False, need_ctx, "mla_attn")

        mats = _s5_mats(s5_lam_re[l], s5_lam_im[l], s5_log_dt[l], s5_b_re[l], s5_b_im[l], s5_c_re[l], s5_c_im[l])
        ys_a, ys_b = _s5_scan(lay, s5a, s5b, mats)
        yb = _s5_glu_call(s5a, s5b, ys_a, ys_b, row(s5_d[l]), bf(s5_w_glu[l]), row(s5_b_glu[l]), tm_big)

        pre_consts = (row(rw_mu[l]), g64, row(rw_k_k[l]), row(rw_k_a[l]), row(rw_r_k[l]),
                      rw_w0[l].reshape(2, 1, MIX_W), bf(rw_w1[l]), bf(rw_w2[l]),
                      rw_a0[l].reshape(2, 1, MIX_W), bf(rw_a1[l]), bf(rw_a2[l]), bf(rw_g1[l]), bf(rw_g2[l]))
        yd = _rwkv_branch(lay, rw, pre_consts, g64, row(rw_ln_g[l]), row(rw_ln_b[l]))

        x_mid = _merge_call(lay, need_ctx, x_all, mod, g_mix, bf(w_in[l][:, 2384:]), ya, yb, yc, yd,
                            bf(w_branch[l]), bf(w_out[l]))
        f, comb_t, gid = _router_call(lay, need_ctx, x_mid, mod, row(norm_ffn_g[l]), wr_hi, wr_lo, r_bias)
        x_all = _moe_call(lay, need_ctx, f, comb_t.T, gid, bf(exp_w_gate[l]), bf(exp_w_up[l]), bf(exp_w_down[l]), x_mid, mod)
    return x_all.reshape(b, n_lat, dm)
```

```python
import functools
import math

import numpy as np
import jax
import jax.numpy as jnp
from jax import lax
from jax.experimental import pallas as pl
from jax.experimental.pallas import tpu as pltpu

F32 = jnp.float32
BF16 = jnp.bfloat16

D_MODEL = 1024
GRID_W = 64
ROPE_BASE = 10000.0
EPS = 1e-6
DA_HEADS, DA_DIM, DA_VDIM = 4, 32, 64
S5_GROUPS, S5_CH, S5_STATE = 16, 16, 64
MLA_HEADS, MLA_NOPE, MLA_ROPE, MLA_VDIM = 4, 32, 16, 64
MLA_Q_RANK, MLA_KV_RANK = 192, 128
MLA_HEAD_PAD = 64
RW_HEADS, RW_DIM = 4, 64
RW_LN_EPS = 64e-5
N_BRANCH = 4
N_EXPERTS, N_GROUPS, EXPERTS_PER_GROUP = 16, 4, 4
D_FF = 512
MIX_W = 256

S5_CHUNK = 8
S5_FLAT = S5_CHUNK * MIX_W
S5_STATE_W = S5_GROUPS * S5_STATE
RW_CHUNK = 16
RW_TILE = 128
RW_PREP_TILES = 2
_NT = (((1,), (1,)), ((), ()))
TOKEN_TILE = 256
MOE_TILE = 1152
MOE_BLOCK = 384
MOE_OVER = 128

_DA_W, _S5_W, _MLA_W, _RW_W = 768, 256, 640, 1024
_MIX_COLS = _DA_W + _S5_W + _MLA_W + _RW_W

V7X_VMEM_BYTES = 64 * 2**20
_VMEM_LIMIT = V7X_VMEM_BYTES - 8 * 2**20


def _cparams(*sem):
    return pltpu.CompilerParams(dimension_semantics=sem, vmem_limit_bytes=_VMEM_LIMIT)


def _full(a):
    return pl.BlockSpec(a.shape, lambda *_, nd=a.ndim: (0,) * nd)


def _split_dot(x, w, terms=2):
    acc = None
    rem = x
    for i in range(terms):
        part = rem.astype(BF16)
        d = jnp.dot(part, w, preferred_element_type=F32)
        acc = d if acc is None else acc + d
        if i + 1 < terms:
            rem = rem - part.astype(F32)
    return acc


def _split_dot_rhs(w, x):
    hi = x.astype(BF16)
    lo = (x - hi.astype(F32)).astype(BF16)
    return jnp.dot(w, hi, preferred_element_type=F32) + jnp.dot(w, lo, preferred_element_type=F32)


def _modulate(x, g, shift, scale):
    xn = x * lax.rsqrt(jnp.mean(x * x, axis=-1, keepdims=True) + EPS)
    return xn * g * (1.0 + scale) + shift


def _sigmoid(x):
    return 1.0 / (1.0 + jnp.exp(-x))


def _group_rms(x, ones_bd, inv_n, gain):
    ms = _split_dot(x * x, ones_bd) * inv_n
    return x * lax.rsqrt(ms + EPS) * gain


def _lane_partner(x, half, period, first_end):
    n = x.shape[1]
    lane = lax.broadcasted_iota(jnp.int32, x.shape, 1)
    up = pltpu.roll(x, n - half, axis=1)
    down = pltpu.roll(x, half, axis=1)
    return jnp.where((lane & (period - 1)) < first_end, up, down)


def _rope(x, cos_t, sin_t, half, period, first_end):
    return x * cos_t + _lane_partner(x, half, period, first_end) * sin_t


def _ada_kernel(c_ref, w_ref, b_ref, o_ref):
    c = c_ref[...]
    s = c * _sigmoid(c)
    o_ref[0] = jnp.dot(s.astype(BF16), w_ref[0].astype(BF16), preferred_element_type=F32) + b_ref[0]


def _ada_call(cc, w_ada, b_ada):
    depth, dm, n = w_ada.shape
    tn = n // 4
    return pl.pallas_call(
        _ada_kernel,
        grid=(depth, n // tn),
        in_specs=[
            pl.BlockSpec(cc.shape, lambda l, j: (0, 0)),
            pl.BlockSpec((1, dm, tn), lambda l, j: (l, 0, j)),
            pl.BlockSpec((1, 1, tn), lambda l, j: (l, 0, j)),
        ],
        out_specs=pl.BlockSpec((1, cc.shape[0], tn), lambda l, j: (l, 0, j)),
        out_shape=jax.ShapeDtypeStruct((depth, cc.shape[0], n), F32),
        compiler_params=_cparams("parallel", "parallel"),
        name="ada_mod",
    )(cc, w_ada, b_ada.reshape(depth, 1, n))


class _Layout:
    def __init__(self, n_batch, n_ctx, n_lat):
        t = TOKEN_TILE
        assert n_ctx % t == 0 and n_lat % t == 0
        self.b, self.n_ctx, self.n_lat, self.n_tot = n_batch, n_ctx, n_lat, n_ctx + n_lat
        self.t = t
        self.ctx_tiles, self.lat_tiles, self.seq_tiles = n_ctx // t, n_lat // t, (n_ctx + n_lat) // t

    def rows(self, with_ctx):
        return self.b * (self.n_tot if with_ctx else self.n_lat)

    def n_tiles(self, with_ctx):
        return self.b * (self.seq_tiles if with_ctx else self.lat_tiles)

    def src_tile(self, with_ctx):
        if with_ctx:
            return lambda i: i
        return lambda i: (i // self.lat_tiles) * self.seq_tiles + i % self.lat_tiles + self.ctx_tiles

    def mod_row(self, with_ctx):
        if with_ctx:
            return lambda i: jnp.where(i % self.seq_tiles < self.ctx_tiles, self.b, i // self.seq_tiles)
        return lambda i: i // self.lat_tiles


def _inproj_kernel(x_ref, mod_ref, g_ref, w_ref, da_ref, s5a_ref, s5b_ref, mla_ref, rw_ref):
    h = _modulate(x_ref[...], g_ref[...], mod_ref[0, 0:1, :], mod_ref[0, 1:2, :])
    acc = jnp.dot(h.astype(BF16), w_ref[...], preferred_element_type=F32)
    da_ref[...] = acc[:, 0:_DA_W]
    s5a_ref[...] = acc[:, _DA_W:_DA_W + _S5_W // 2]
    s5b_ref[...] = acc[:, _DA_W + _S5_W // 2:_DA_W + _S5_W]
    mla_ref[...] = acc[:, _DA_W + _S5_W:_DA_W + _S5_W + _MLA_W]
    rw_ref[...] = acc[:, _DA_W + _S5_W + _MLA_W:_MIX_COLS]


def _inproj_call(lay, x_all, mod, g, w_mix):
    t, tm = x_all.shape[0], lay.t
    widths = (_DA_W, _S5_W // 2, _S5_W // 2, _MLA_W, _RW_W)
    mrow = lay.mod_row(True)
    return pl.pallas_call(
        _inproj_kernel,
        grid=(t // tm,),
        in_specs=[
            pl.BlockSpec((tm, D_MODEL), lambda i: (i, 0)),
            pl.BlockSpec((1, 6, D_MODEL), lambda i: (mrow(i), 0, 0)),
            _full(g), _full(w_mix),
        ],
        out_specs=[pl.BlockSpec((tm, w), lambda i: (i, 0)) for w in widths],
        out_shape=[jax.ShapeDtypeStruct((t, w), F32) for w in widths],
        compiler_params=_cparams("parallel"),
        name="in_proj",
    )(x_all, mod, g, w_mix)


def _qkprep_kernel(da_ref, mla_ref, cda_ref, sda_ref, cml_ref, sml_ref, g32_ref, g64_ref,
                   gda_ref, gml_ref, cqg_ref, ckvg_ref, wuq_ref, wuk_ref, wuv_ref,
                   qd_ref, kd_ref, vd_ref, qm_ref, km_ref, vm_ref):
    g32 = g32_ref[...]
    g64 = g64_ref[...]
    cda, sda = cda_ref[...], sda_ref[...]
    q = _group_rms(da_ref[:, 0:MIX_W], g32, 1.0 / DA_DIM, gda_ref[0:1, :])
    qd_ref[...] = _rope(q, cda, sda, DA_DIM // 2, DA_DIM, DA_DIM // 2).astype(BF16)
    k = _group_rms(da_ref[:, MIX_W:2 * MIX_W], g32, 1.0 / DA_DIM, gda_ref[1:2, :])
    kd_ref[0] = _rope(k, cda, sda, DA_DIM // 2, DA_DIM, DA_DIM // 2).T.astype(BF16)
    vd_ref[...] = da_ref[:, 2 * MIX_W:3 * MIX_W].astype(BF16)

    cml, sml = cml_ref[...], sml_ref[...]
    cq = mla_ref[:, 0:256]
    cqn = cq * lax.rsqrt(jnp.sum(cq * cq, axis=-1, keepdims=True) * (1.0 / MLA_Q_RANK) + EPS) * cqg_ref[...]
    q = jnp.dot(cqn.astype(BF16), wuq_ref[...], preferred_element_type=F32)
    ckv = mla_ref[:, 256:384]
    ckvn = ckv * lax.rsqrt(jnp.mean(ckv * ckv, axis=-1, keepdims=True) + EPS) * ckvg_ref[...]
    ckvb = ckvn.astype(BF16)
    k = jnp.dot(ckvb, wuk_ref[...], preferred_element_type=F32) + mla_ref[:, 384:640]
    vm_ref[...] = jnp.dot(ckvb, wuv_ref[...], preferred_element_type=F32).astype(BF16)
    inv_n = 1.0 / (MLA_NOPE + MLA_ROPE)
    half = MLA_ROPE // 2
    q = _group_rms(q, g64, inv_n, gml_ref[0:1, :])
    qm_ref[...] = _rope(q, cml, sml, half, MLA_HEAD_PAD, MLA_NOPE + half).astype(BF16)
    k = _group_rms(k, g64, inv_n, gml_ref[1:2, :])
    km_ref[0] = _rope(k, cml, sml, half, MLA_HEAD_PAD, MLA_NOPE + half).T.astype(BF16)


def _qkprep_call(lay, da, mla, tabs, consts):
    t, tm = da.shape[0], lay.t
    st = lay.seq_tiles
    row = pl.BlockSpec((tm, MIX_W), lambda i: (i, 0))
    key_t = pl.BlockSpec((1, MIX_W, tm), lambda i: (i // st, 0, i % st))
    in_specs = [pl.BlockSpec((tm, _DA_W), lambda i: (i, 0)), pl.BlockSpec((tm, _MLA_W), lambda i: (i, 0))]
    in_specs += [pl.BlockSpec((tm, MIX_W), lambda i: (i % st, 0)) for _ in tabs]
    in_specs += [_full(a) for a in consts]
    tok = jax.ShapeDtypeStruct((t, MIX_W), BF16)
    keys = jax.ShapeDtypeStruct((lay.b, MIX_W, lay.n_tot), BF16)
    return pl.pallas_call(
        _qkprep_kernel,
        grid=(t // tm,),
        in_specs=in_specs,
        out_specs=[row, key_t, row, row, key_t, row],
        out_shape=[tok, keys, tok, tok, keys, tok],
        compiler_params=_cparams("parallel"),
        name="qk_prep",
    )(da, mla, *tabs, *consts)


def _softmax_parts(q, kt):
    s = jnp.dot(q, kt, preferred_element_type=F32)
    p = jnp.exp2(s - jnp.max(s, axis=-1, keepdims=True))
    return p, 1.0 / jnp.sum(p, axis=-1, keepdims=True)


def _attn_heads(q, kt_ref, v_ref, nk, diff, lam):
    lane = lax.broadcasted_iota(jnp.int32, (q.shape[0], MIX_W), 1)
    v = v_ref[0, 0:nk, :]
    acc = jnp.zeros((q.shape[0], MIX_W), F32)
    for h in range(DA_HEADS):
        if diff:
            e0, e1 = 2 * h * DA_DIM, (2 * h + 1) * DA_DIM
            p0, r0 = _softmax_parts(q[:, e0:e0 + DA_DIM], kt_ref[0, e0:e0 + DA_DIM, 0:nk])
            p1, r1 = _softmax_parts(q[:, e1:e1 + DA_DIM], kt_ref[0, e1:e1 + DA_DIM, 0:nk])
            o = jnp.dot((p0 * r0 - p1 * (r1 * lam)).astype(BF16), v, preferred_element_type=F32)
        else:
            e0 = h * MLA_HEAD_PAD
            p, r = _softmax_parts(q[:, e0:e0 + MLA_HEAD_PAD], kt_ref[0, e0:e0 + MLA_HEAD_PAD, 0:nk])
            o = jnp.dot(p.astype(BF16), v, preferred_element_type=F32) * r
        in_head = jnp.logical_and(lane >= h * DA_VDIM, lane < (h + 1) * DA_VDIM)
        acc = jnp.where(in_head, o, acc)
    return acc


def _attn_kernel(q_ref, kt_ref, v_ref, lam_ref, gain_ref, g64_ref, o_ref, *, diff, n_ctx, n_tot, ctx_tiles):
    q = q_ref[...]
    lam = lam_ref[...]

    def run(nk):
        o = _attn_heads(q, kt_ref, v_ref, nk, diff, lam)
        if diff:
            o = _group_rms(o, g64_ref[...], 1.0 / DA_VDIM, gain_ref[...])
        o_ref[...] = o.astype(BF16)

    if ctx_tiles:
        is_ctx = pl.program_id(1) < ctx_tiles
        pl.when(is_ctx)(lambda: run(n_ctx))
        pl.when(jnp.logical_not(is_ctx))(lambda: run(n_tot))
    else:
        run(n_tot)


def _attention(lay, q, kt, v, extra, diff, with_ctx, name):
    tq = lay.t
    tiles = lay.seq_tiles if with_ctx else lay.lat_tiles
    off = 0 if with_ctx else lay.ctx_tiles
    v3 = v.reshape(lay.b, lay.n_tot, MIX_W)
    kern = functools.partial(_attn_kernel, diff=diff, n_ctx=lay.n_ctx, n_tot=lay.n_tot,
                             ctx_tiles=lay.ctx_tiles if with_ctx else 0)
    return pl.pallas_call(
        kern,
        grid=(lay.b, tiles),
        in_specs=[
            pl.BlockSpec((tq, MIX_W), lambda b, j: (b * lay.seq_tiles + j + off, 0)),
            pl.BlockSpec((1, MIX_W, lay.n_tot), lambda b, j: (b, 0, 0)),
            pl.BlockSpec((1, lay.n_tot, MIX_W), lambda b, j: (b, 0, 0)),
        ] + [_full(a) for a in extra],
        out_specs=pl.BlockSpec((tq, MIX_W), lambda b, j: (b * tiles + j, 0)),
        out_shape=jax.ShapeDtypeStruct((lay.rows(with_ctx), MIX_W), BF16),
        compiler_params=_cparams("parallel", "parallel"),
        name=name,
    )(q, kt, v3, *extra)


def _chunk_rows(ua_ref, ub_ref):
    n = ua_ref.shape[0] // S5_CHUNK
    parts = []
    for s in range(S5_CHUNK):
        rows = pl.ds(s, n, stride=S5_CHUNK)
        parts += [ua_ref[rows, :], ub_ref[rows, :]]
    return jnp.concatenate(parts, axis=1).astype(BF16)


def _s5_proj_kernel(ua_ref, ub_ref, bre_ref, bim_ref, sre_ref, sim_ref):
    u = _chunk_rows(ua_ref, ub_ref)
    sre_ref[0] = jnp.dot(u, bre_ref[0], preferred_element_type=F32)
    sim_ref[0] = jnp.dot(u, bim_ref[0], preferred_element_type=F32)


def _s5_rec_kernel(sre_ref, sim_ref, are_ref, aim_ref, hre_ref, him_ref, *, n_batch, n_chunks, ctx_chunks):
    rev = pl.program_id(0) == 1
    ar, ai = are_ref[0], aim_ref[0]
    sre, sim, hre, him = sre_ref.at[0], sim_ref.at[0], hre_ref.at[0], him_ref.at[0]

    def step(i, carry):
        hr, hi = carry
        k_rev = jnp.where(i < ctx_chunks, ctx_chunks - 1 - i, n_chunks - 1 + ctx_chunks - i)
        k = jnp.where(rev, k_rev, i)
        rows = pl.ds(k, n_batch, stride=n_chunks)
        hre[rows, :] = hr
        him[rows, :] = hi
        return ar * hr - ai * hi + sre[rows, :], ar * hi + ai * hr + sim[rows, :]

    zero = jnp.zeros((n_batch, 128), F32)
    lax.fori_loop(0, n_chunks, step, (zero, zero))


def _s5_out_kernel(ua_ref, ub_ref, hre_ref, him_ref, m_ref, cre_ref, cim_ref, ya_ref, yb_ref):
    y = jnp.dot(_chunk_rows(ua_ref, ub_ref), m_ref[0], preferred_element_type=F32)
    y = y + _split_dot(hre_ref[0], cre_ref[0]) + _split_dot(him_ref[0], cim_ref[0])
    n = y.shape[0]
    ya, yb = ya_ref.at[0], yb_ref.at[0]
    for s in range(S5_CHUNK):
        rows = pl.ds(s, n, stride=S5_CHUNK)
        ya[rows, :] = y[:, s * MIX_W:s * MIX_W + 128]
        yb[rows, :] = y[:, s * MIX_W + 128:(s + 1) * MIX_W]


def _s5_mats(lam_re, lam_im, log_dt, b_re, b_im, c_re, c_im):
    hp = lax.Precision.HIGHEST
    L, G, P, CH = S5_CHUNK, S5_GROUPS, S5_STATE, S5_CH
    lr, li = lam_re.astype(F32), lam_im.astype(F32)
    dt = jnp.exp(log_dt.astype(F32))[..., None]
    zr, zi = lr * dt, li * dt
    j = jnp.arange(L + 1, dtype=F32)[:, None, None, None]
    mag = jnp.exp(zr[None] * j)
    pw_re, pw_im = mag * jnp.cos(zi[None] * j), mag * jnp.sin(zi[None] * j)
    nr, ni = pw_re[1] - 1.0, pw_im[1]
    den = lr * lr + li * li
    cr, ci = (nr * lr + ni * li) / den, (ni * lr - nr * li) / den
    bre, bim = b_re.astype(F32), b_im.astype(F32)
    bb_re = cr[..., None] * bre - ci[..., None] * bim
    bb_im = cr[..., None] * bim + ci[..., None] * bre
    x_re = pw_re[..., None] * bb_re[None] - pw_im[..., None] * bb_im[None]
    x_im = pw_re[..., None] * bb_im[None] + pw_im[..., None] * bb_re[None]
    cre, cim = c_re.astype(F32), c_im.astype(F32)
    kern = (jnp.einsum('dgcp,jdgpe->dgjce', cre, x_re[:L], precision=hp)
            - jnp.einsum('dgcp,jdgpe->dgjce', cim, x_im[:L], precision=hp))
    def spread_mask(a, b):
        spread = jnp.asarray(np.tile(np.eye(b, dtype=np.float32), (1, G)))
        mask = jnp.asarray(np.kron(np.eye(G, dtype=np.float32), np.ones((a, b), np.float32)))
        return spread, mask

    kt = kern.transpose(0, 2, 1, 4, 3)
    xt_re, xt_im = x_re.transpose(1, 0, 2, 4, 3), x_im.transpose(1, 0, 2, 4, 3)
    pwt_re, pwt_im = pw_re.transpose(1, 0, 2, 3)[:, :, :, :, None], pw_im.transpose(1, 0, 2, 3)[:, :, :, :, None]
    cret, cimt = cre.transpose(0, 1, 3, 2)[:, None], cim.transpose(0, 1, 3, 2)[:, None]
    ca_re, ca_im = cret * pwt_re - cimt * pwt_im, -(cret * pwt_im + cimt * pwt_re)
    s_idx, t_idx = np.arange(L)[:, None], np.arange(L)[None, :]
    k_st, xb_re, xb_im, cq_re, cq_im = [], [], [], [], []
    for d in range(2):
        lag = (t_idx - s_idx) if d == 0 else (s_idx - t_idx)
        k_st.append(jnp.where(jnp.asarray(lag >= 0)[:, :, None, None, None], kt[d][np.clip(lag, 0, L - 1)], 0.0))
        pw = np.arange(L - 1, -1, -1) if d == 0 else np.arange(L)
        xb_re.append(xt_re[d][pw])
        xb_im.append(xt_im[d][pw])
        q = np.arange(1, L + 1) if d == 0 else np.arange(L, 0, -1)
        cq_re.append(ca_re[d][q])
        cq_im.append(ca_im[d][q])
    sp, mk = spread_mask(CH, CH)
    m = jnp.einsum('dstrb,bc->dsrtc', jnp.stack(k_st).reshape(2, L, L, G * CH, CH), sp, precision=hp) * mk[:, None, :]
    m = m.astype(BF16).reshape(2, L * G * CH, L * G * CH)
    sp, mk = spread_mask(CH, P)
    to_b = lambda x: (jnp.einsum('dsrb,bc->dsrc', jnp.stack(x).reshape(2, L, G * CH, P), sp, precision=hp) * mk
                      ).astype(BF16).reshape(2, L * G * CH, G * P)
    sp_c, mk_c = spread_mask(P, CH)
    to_c = lambda x: (jnp.einsum('dtrb,bc->drtc', jnp.stack(x).reshape(2, L, G * P, CH), sp_c, precision=hp)
                      * mk_c[:, None, :]).astype(BF16).reshape(2, G * P, L * G * CH)
    a_re, a_im = pw_re[L].reshape(2, 1, G * P), pw_im[L].reshape(2, 1, G * P)
    return m, to_b(xb_re), to_b(xb_im), to_c(cq_re), to_c(cq_im), a_re, a_im


def _s5_scan(lay, ua, ub, mats):
    m, b_r, b_i, c_r, c_i, a_re, a_im = mats
    n_chunks = lay.n_tot // S5_CHUNK
    rows = lay.b * n_chunks
    tr = min(lay.t, rows)
    tok = tr * S5_CHUNK
    half = MIX_W // 2
    wspec = lambda a: pl.BlockSpec((1,) + a.shape[1:], lambda d, i: (d, 0, 0))
    state = jax.ShapeDtypeStruct((2, rows, S5_STATE_W), F32)
    sblk = pl.BlockSpec((1, tr, S5_STATE_W), lambda d, i: (d, i, 0))
    ublk = pl.BlockSpec((tok, half), lambda d, i: (i, 0))
    s_re, s_im = pl.pallas_call(
        _s5_proj_kernel,
        grid=(2, rows // tr),
        in_specs=[ublk, ublk, wspec(b_r), wspec(b_i)],
        out_specs=[sblk, sblk],
        out_shape=[state, state],
        compiler_params=_cparams("parallel", "parallel"),
        name="s5_proj",
    )(ua, ub, b_r, b_i)
    col = pl.BlockSpec((1, rows, 128), lambda d, j: (d, 0, j))
    acol = pl.BlockSpec((1, 1, 128), lambda d, j: (d, 0, j))
    h_re, h_im = pl.pallas_call(
        functools.partial(_s5_rec_kernel, n_batch=lay.b, n_chunks=n_chunks, ctx_chunks=lay.n_ctx // S5_CHUNK),
        grid=(2, S5_STATE_W // 128),
        in_specs=[col, col, acol, acol],
        out_specs=[col, col],
        out_shape=[state, state],
        compiler_params=_cparams("parallel", "parallel"),
        name="s5_rec",
    )(s_re, s_im, a_re, a_im)
    yblk = pl.BlockSpec((1, tok, half), lambda d, i: (d, i, 0))
    yshape = jax.ShapeDtypeStruct((2, lay.b * lay.n_tot, half), F32)
    return pl.pallas_call(
        _s5_out_kernel,
        grid=(2, rows // tr),
        in_specs=[ublk, ublk, sblk, sblk, wspec(m), wspec(c_r), wspec(c_i)],
        out_specs=[yblk, yblk],
        out_shape=[yshape, yshape],
        compiler_params=_cparams("parallel", "parallel"),
        name="s5_out",
    )(ua, ub, h_re, h_im, m, c_r, c_i)


def _s5_glu_kernel(ua_ref, ub_ref, ya_ref, yb_ref, d_ref, w_ref, b_ref, o_ref):
    u = jnp.concatenate([ua_ref[...], ub_ref[...]], axis=1)
    y = d_ref[...] * u + jnp.concatenate([ya_ref[0] + ya_ref[1], yb_ref[0] + yb_ref[1]], axis=1)
    z = 0.5 * y * (1.0 + jnp.tanh(math.sqrt(2.0 / math.pi) * (y + 0.044715 * (y * y * y))))
    gate = _sigmoid(jnp.dot(z.astype(BF16), w_ref[...], preferred_element_type=F32) + b_ref[...])
    o_ref[...] = (z * gate).astype(BF16)


def _s5_glu_call(ua, ub, ya, yb, d, w, bias, tm):
    t, half = ua.shape
    urow = pl.BlockSpec((tm, half), lambda i: (i, 0))
    yrow = pl.BlockSpec((2, tm, half), lambda i: (0, i, 0))
    return pl.pallas_call(
        _s5_glu_kernel,
        grid=(t // tm,),
        in_specs=[urow, urow, yrow, yrow, _full(d), _full(w), _full(bias)],
        out_specs=pl.BlockSpec((tm, MIX_W), lambda i: (i, 0)),
        out_shape=jax.ShapeDtypeStruct((t, MIX_W), BF16),
        compiler_params=_cparams("parallel"),
        name="s5_glu",
    )(ua, ub, ya, yb, d, w, bias)


def _rw_pre_kernel(x_ref, prev_ref, next_ref, mu_ref, g64_ref, kk_g_ref, ka_ref, rk_ref,
                   w0_ref, w1_ref, w2_ref, a0_ref, a1_ref, a2_ref, g1_ref, g2_ref,
                   r_ref, v_ref, kk_ref, lw_ref, kka_ref, km_ref, bon_ref, gate_ref,
                   *, seq_tiles, ctx_tiles):
    x = x_ref[...]
    n = x.shape[0]
    j = pl.program_id(0) % seq_tiles
    starts = jnp.logical_or(j == 0, j == ctx_tiles)
    ends = jnp.logical_or(j == ctx_tiles - 1, j == seq_tiles - 1)
    prev_row = jnp.where(starts, 0.0, prev_ref[0, 7:8, :])
    next_row = jnp.where(ends, 0.0, next_ref[0, 0:1, :])
    row = lax.broadcasted_iota(jnp.int32, x.shape, 0)
    left = jnp.where(row == 0, prev_row, pltpu.roll(x, 1, axis=0))
    right = jnp.where(row == n - 1, next_row, pltpu.roll(x, n - 1, axis=0))
    x = x + (0.5 * (left + right) - x) * mu_ref[...]
    r, k, v, xd = (x[:, i * MIX_W:(i + 1) * MIX_W] for i in range(4))
    g64 = g64_ref[...]
    kscaled = k * kk_g_ref[...]
    kk = kscaled / jnp.maximum(jnp.sqrt(_split_dot(kscaled * kscaled, g64)), 1e-12)
    xdb = xd.astype(BF16)
    r_ref[...] = r
    v_ref[...] = v
    kk_ref[...] = kk
    km_sum = None
    for d in range(2):
        lo = jnp.tanh(jnp.dot(xdb, w1_ref[d], preferred_element_type=F32))
        w_raw = w0_ref[d] + jnp.dot(lo.astype(BF16), w2_ref[d], preferred_element_type=F32)
        lw_ref[d] = -_sigmoid(w_raw) * math.exp(-0.5)
        ar = jnp.dot(xdb, a1_ref[d], preferred_element_type=F32)
        a = _sigmoid(a0_ref[d] + jnp.dot(ar.astype(BF16), a2_ref[d], preferred_element_type=F32))
        km = k * (1.0 + (a - 1.0) * ka_ref[...])
        kka_ref[d] = kk * a
        km_ref[d] = km
        km_sum = km if km_sum is None else km_sum + km
    bon_ref[...] = _split_dot(r * km_sum * rk_ref[...], g64) * v
    gr = _sigmoid(jnp.dot(xdb, g1_ref[...], preferred_element_type=F32))
    gate_ref[...] = jnp.dot(gr.astype(BF16), g2_ref[...], preferred_element_type=F32)


def _rw_pre_call(lay, rw, consts):
    t, tr = rw.shape[0], lay.t
    nt = t // tr
    g8 = tr // 8
    rw8 = rw.reshape(t // 8, 8, _RW_W)
    row = pl.BlockSpec((tr, MIX_W), lambda i: (i, 0))
    row2 = pl.BlockSpec((2, tr, MIX_W), lambda i: (0, i, 0))
    sd = jax.ShapeDtypeStruct((t, MIX_W), F32)
    sd2 = jax.ShapeDtypeStruct((2, t, MIX_W), F32)
    return pl.pallas_call(
        functools.partial(_rw_pre_kernel, seq_tiles=lay.seq_tiles, ctx_tiles=lay.ctx_tiles),
        grid=(nt,),
        in_specs=[pl.BlockSpec((tr, _RW_W), lambda i: (i, 0)),
                  pl.BlockSpec((1, 8, _RW_W), lambda i: (jnp.maximum(i * g8 - 1, 0), 0, 0)),
                  pl.BlockSpec((1, 8, _RW_W), lambda i: (jnp.minimum((i + 1) * g8, t // 8 - 1), 0, 0))]
                 + [_full(a) for a in consts],
        out_specs=[row, row, row, row2, row2, row2, row, row],
        out_shape=[sd, sd, sd, sd2, sd2, sd2, sd, sd],
        compiler_params=_cparams("parallel"),
        name="rwkv_pre",
    )(rw, rw8, rw8, *consts)


def _head_masks(shape, lane_axis, seg):
    lane = lax.broadcasted_iota(jnp.int32, shape, lane_axis)
    return [jnp.logical_and(lane >= h * seg, lane < (h + 1) * seg) for h in range(RW_HEADS)]


def _rw_prep_kernel(*refs, rev):
    tiles = [_rw_prep_tile(sub, *refs, rev=rev) for sub in range(RW_PREP_TILES)]
    while tiles:
        tiles = [t for t in tiles if next(t, None) is not None]


def _rw_prep_tile(sub, r_ref, kk_ref, v_ref, lw_ref, ka_ref, km_ref, perm_ref, permt_ref, g_ref, eye_ref,
                  br_ref, ck_ref, uvt_ref, y0_ref, pc_ref, *, rev):
    C, NC = RW_CHUNK, RW_TILE // RW_CHUNK
    perm, permt, g64, eye4 = perm_ref[...], permt_ref[...], g_ref[...], eye_ref[...]
    tok = slice(sub * RW_TILE, (sub + 1) * RW_TILE)
    nat = jnp.concatenate([r_ref[0, tok, :], kk_ref[0, tok, :], v_ref[0, tok, :],
                           lw_ref[0, 0, tok, :], ka_ref[0, 0, tok, :], km_ref[0, 0, tok, :]], axis=1)
    hi = nat.astype(BF16)
    lo = (nat - hi.astype(F32)).astype(BF16)
    pm = jnp.dot(perm, hi, preferred_element_type=F32) + jnp.dot(perm, lo, preferred_element_type=F32)
    r, kk, v, lw, ka, km = (pm[:, i * MIX_W:(i + 1) * MIX_W] for i in range(6))
    slab = lambda x, j: x[j * NC:(j + 1) * NC, :]
    order = list(range(C))[::-1] if rev else list(range(C))
    pos = {j: i for i, j in enumerate(order)}
    cum, run = {}, None
    for j in order:
        run = slab(lw, j) if run is None else run + slab(lw, j)
        cum[j] = run
    tot = run
    yield True
    bh, ch, kh, rh, cp, kp, vv = {}, {}, {}, {}, {}, {}, {}
    for j in range(C):
        e_inv, e_end = jnp.exp(-cum[j]), jnp.exp(tot - cum[j])
        bh[j] = -slab(kk, j) * jnp.exp(cum[j] - slab(lw, j))
        ch[j], kh[j] = slab(ka, j) * e_inv, slab(km, j) * e_inv
        rh[j] = slab(r, j) * jnp.exp(cum[j])
        cp[j], kp[j] = slab(ka, j) * e_end, slab(km, j) * e_end
        vv[j] = slab(v, j)
    strict = [(t, s) for t in order for s in order if pos[s] < pos[t]]
    incl = [(t, s) for t in order for s in order if pos[s] <= pos[t]]
    def head_dots(lhs, rhs, pairs):
        prods = jnp.concatenate([lhs[t] * rhs[s] for t, s in pairs], axis=0).astype(BF16)
        gram = jnp.dot(prods, g64, preferred_element_type=F32)
        return {p: gram[i * NC:(i + 1) * NC, :] for i, p in enumerate(pairs)}

    yield True
    acb = head_dots(bh, ch, strict)
    yield True
    akb = head_dots(bh, kh, strict)
    yield True
    mcr = head_dots(rh, ch, incl)
    yield True
    mkr = head_dots(rh, kh, incl)
    yield True
    bt, u0 = {}, {}
    for t in order:
        b_acc, u_acc = bh[t], jnp.zeros_like(bh[t])
        for s in order:
            if pos[s] < pos[t]:
                b_acc = b_acc + acb[(t, s)] * bt[s]
                u_acc = u_acc + akb[(t, s)] * vv[s] + acb[(t, s)] * u0[s]
        bt[t], u0[t] = b_acc, u_acc
        yield True
    rt, y0 = {}, {}
    for t in order:
        r_acc, y_acc = rh[t], jnp.zeros_like(rh[t])
        for s in order:
            if pos[s] <= pos[t]:
                r_acc = r_acc + mcr[(t, s)] * bt[s]
                y_acc = y_acc + mcr[(t, s)] * u0[s] + mkr[(t, s)] * vv[s]
        rt[t], y0[t] = r_acc, y_acc
        yield True
    stackp = lambda dct: jnp.concatenate([dct[j] for j in range(C)], axis=0)
    b16 = lambda x: x.astype(BF16)
    y0p, u0p = stackp(y0), stackp(u0)
    y0h, u0h = b16(y0p), b16(u0p)
    cat = jnp.concatenate([b16(stackp(bt)), b16(stackp(rt)), b16(stackp(cp)), b16(stackp(kp)),
                           y0h, b16(y0p - y0h.astype(F32)), u0h, b16(u0p - u0h.astype(F32)), b16(stackp(vv))], axis=1)
    natural = jnp.dot(permt, cat, preferred_element_type=F32)
    seg = lambda i: natural[:, i * MIX_W:(i + 1) * MIX_W]
    yield True
    btn, rtn, cpn, kpn = b16(seg(0)), b16(seg(1)), b16(seg(2)), b16(seg(3))
    y0_ref[0, tok, :] = seg(4) + seg(5)
    u0h_n, u0l_n, vn = b16(seg(6)), b16(seg(7)), b16(seg(8))
    hm = _head_masks((C, MIX_W), 1, RW_DIM)
    zero = jnp.zeros((C, MIX_W), BF16)
    zh, zl = [], []
    for c in range(NC):
        rows = slice(c * C, (c + 1) * C)
        br_ref[0, sub * NC + c, 0:C, :] = btn[rows]
        br_ref[0, sub * NC + c, C:2 * C, :] = rtn[rows]
        ck_ref[0, sub * NC + c, 0:C, :] = cpn[rows]
        ck_ref[0, sub * NC + c, C:2 * C, :] = kpn[rows]
        for h in range(RW_HEADS):
            zh += [jnp.where(hm[h], u0h_n[rows], zero), jnp.where(hm[h], vn[rows], zero)]
            zl += [jnp.where(hm[h], u0l_n[rows], zero), zero]
    uvt = (lax.dot_general(eye4, jnp.concatenate(zh, axis=0), _NT, preferred_element_type=F32)
           + lax.dot_general(eye4, jnp.concatenate(zl, axis=0), _NT, preferred_element_type=F32))
    for c in range(NC):
        uvt_ref[0, sub * NC + c] = uvt[:, c * 2 * C * RW_HEADS:(c + 1) * 2 * C * RW_HEADS]
    pc_ref[0, sub * NC:(sub + 1) * NC, :] = jnp.exp(tot)


def _rw_prep_call(lay, shared, perdir, consts, rev):
    b, n_tot, tt = lay.b, lay.n_tot, RW_TILE * RW_PREP_TILES
    assert n_tot % tt == 0
    nck = n_tot // RW_CHUNK
    cpt = tt // RW_CHUNK
    d = 1 if rev else 0
    sh = [a.reshape(b, n_tot, MIX_W) for a in shared]
    pd = [a.reshape(2, b, n_tot, MIX_W) for a in perdir]
    tok = pl.BlockSpec((1, tt, MIX_W), lambda i, j: (i, j, 0))
    tok_d = pl.BlockSpec((1, 1, tt, MIX_W), lambda i, j: (d, i, j, 0))
    rows32 = pl.BlockSpec((1, cpt, 2 * RW_CHUNK, MIX_W), lambda i, j: (i, j, 0, 0))
    return pl.pallas_call(
        functools.partial(_rw_prep_kernel, rev=rev),
        grid=(b, n_tot // tt),
        in_specs=[tok] * 3 + [tok_d] * 3 + [_full(a) for a in consts],
        out_specs=[rows32, rows32,
                   pl.BlockSpec((1, cpt, RW_DIM, 2 * RW_CHUNK * RW_HEADS), lambda i, j: (i, j, 0, 0)),
                   tok,
                   pl.BlockSpec((1, cpt, MIX_W), lambda i, j: (i, j, 0))],
        out_shape=[jax.ShapeDtypeStruct((b, nck, 2 * RW_CHUNK, MIX_W), BF16),
                   jax.ShapeDtypeStruct((b, nck, 2 * RW_CHUNK, MIX_W), BF16),
                   jax.ShapeDtypeStruct((b, nck, RW_DIM, 2 * RW_CHUNK * RW_HEADS), F32),
                   jax.ShapeDtypeStruct((b, n_tot, MIX_W), F32),
                   jax.ShapeDtypeStruct((b, nck, MIX_W), F32)],
        compiler_params=_cparams("parallel", "parallel"),
        name="rwkv_prep_rev" if rev else "rwkv_prep_fwd",
    )(*sh, *pd, *consts)


def _rw_scan_kernel(brf, ckf, uvtf, pcf, brr, ckr, uvtr, pcr, ytf_ref, ytr_ref, s_scr, *, n_batch):
    @pl.when(pl.program_id(0) == 0)
    def _():
        s_scr[...] = jnp.zeros_like(s_scr)

    cpt = RW_TILE // RW_CHUNK
    hm = _head_masks((2 * RW_CHUNK, MIX_W), 1, RW_DIM)
    lane = lax.broadcasted_iota(jnp.int32, (RW_DIM, 2 * RW_CHUNK * RW_HEADS), 1)
    is_u = (lane & (2 * RW_CHUNK - 1)) < RW_CHUNK
    per_head = lambda x: jnp.concatenate([jnp.where(m, x, jnp.zeros_like(x)) for m in hm], axis=0)

    def step(c, carry):
        for p in range(2 * n_batch):
            d, b = divmod(p, n_batch)
            br_ref, ck_ref, uvt_ref, pc_ref, yt_ref = (brf, ckf, uvtf, pcf, ytf_ref) if d == 0 else (brr, ckr, uvtr, pcr, ytr_ref)
            cc = c if d == 0 else cpt - 1 - c
            s = s_scr[p]
            shi = s.astype(BF16)
            slo = (s - shi.astype(F32)).astype(BF16)
            w2 = lax.dot_general(jnp.concatenate([shi, slo], axis=0), per_head(br_ref[b, cc]), _NT, preferred_element_type=F32)
            w = w2[:RW_DIM] + w2[RW_DIM:]
            yt_ref[b, cc] = w
            uvt = uvt_ref[b, cc]
            lhs = jnp.where(is_u, w + uvt, uvt).astype(BF16)
            s_scr[p] = s * pc_ref[b, pl.ds(cc, 1), :] + jnp.dot(lhs, per_head(ck_ref[b, cc]), preferred_element_type=F32)
        return carry

    lax.fori_loop(0, cpt, step, 0)


def _rw_scan_call(lay, fwd, rev):
    b, n_tot, tt = lay.b, lay.n_tot, RW_TILE
    assert lay.n_ctx % tt == 0 and lay.n_lat % tt == 0
    nt, ct = n_tot // tt, lay.n_ctx // tt
    cpt = tt // RW_CHUNK
    rev_tile = lambda i: jnp.where(i < ct, ct - 1 - i, nt - 1 + ct - i)

    def specs(tile):
        return [pl.BlockSpec((b, cpt, 2 * RW_CHUNK, MIX_W), lambda i: (0, tile(i), 0, 0)),
                pl.BlockSpec((b, cpt, 2 * RW_CHUNK, MIX_W), lambda i: (0, tile(i), 0, 0)),
                pl.BlockSpec((b, cpt, RW_DIM, 2 * RW_CHUNK * RW_HEADS), lambda i: (0, tile(i), 0, 0)),
                pl.BlockSpec((b, cpt, MIX_W), lambda i: (0, tile(i), 0))]

    ident = lambda i: i
    yt = jax.ShapeDtypeStruct((b, n_tot // RW_CHUNK, RW_DIM, 2 * RW_CHUNK * RW_HEADS), F32)
    return pl.pallas_call(
        functools.partial(_rw_scan_kernel, n_batch=b),
        grid=(nt,),
        in_specs=specs(ident) + specs(rev_tile),
        out_specs=[specs(ident)[2], specs(rev_tile)[2]],
        out_shape=[yt, yt],
        scratch_shapes=[pltpu.VMEM((2 * b, RW_DIM, MIX_W), F32)],
        compiler_params=_cparams("arbitrary"),
        name="rwkv_scan",
    )(*fwd, *rev)


def _rw_fin_kernel(ytf_ref, ytr_ref, y0f_ref, y0r_ref, bon_ref, gate_ref, asel_ref, g64_ref, lng_ref, lnb_ref, o_ref):
    cpt = RW_TILE // RW_CHUNK
    asel = asel_ref[...]
    width = cpt * 2 * RW_CHUNK * RW_HEADS
    lane = lax.broadcasted_iota(jnp.int32, (RW_DIM, width), 1)
    lane_head = jnp.bitwise_and(jnp.right_shift(lane, 5), RW_HEADS - 1)

    def base(yt_ref):
        yt = jnp.concatenate([yt_ref[0, c] for c in range(cpt)], axis=1).astype(BF16)
        rows = jnp.concatenate([jnp.where(lane_head == h, yt, jnp.zeros_like(yt)) for h in range(RW_HEADS)], axis=0)
        return lax.dot_general(asel, rows, _NT, preferred_element_type=F32)

    y = base(ytf_ref) + y0f_ref[0] + base(ytr_ref) + y0r_ref[0]
    g64 = g64_ref[...]
    mean = _split_dot(y, g64) * (1.0 / RW_DIM)
    c = y - mean
    var = _split_dot(c * c, g64) * (1.0 / RW_DIM)
    out = c * lax.rsqrt(var + RW_LN_EPS) * lng_ref[...] + lnb_ref[...] + bon_ref[...]
    o_ref[...] = (out * gate_ref[...]).astype(BF16)


def _rw_fin_call(lay, ytf, ytr, y0f, y0r, bon, gate, consts):
    b, n_tot, tt = lay.b, lay.n_tot, RW_TILE
    nt = n_tot // tt
    cpt = tt // RW_CHUNK
    ytb = pl.BlockSpec((1, cpt, RW_DIM, 2 * RW_CHUNK * RW_HEADS), lambda i, j: (i, j, 0, 0))
    y0b = pl.BlockSpec((1, tt, MIX_W), lambda i, j: (i, j, 0))
    row = pl.BlockSpec((tt, MIX_W), lambda i, j: (i * nt + j, 0))
    return pl.pallas_call(
        _rw_fin_kernel,
        grid=(b, nt),
        in_specs=[ytb, ytb, y0b, y0b, row, row] + [_full(a) for a in consts],
        out_specs=row,
        out_shape=jax.ShapeDtypeStruct((b * n_tot, MIX_W), BF16),
        compiler_params=_cparams("parallel", "parallel"),
        name="rwkv_finish",
    )(ytf, ytr, y0f, y0r, bon, gate, *consts)


def _rw_constants():
    c, nc = RW_CHUNK, RW_TILE // RW_CHUNK
    perm = np.zeros((RW_TILE, RW_TILE), np.float32)
    for ci in range(nc):
        for j in range(c):
            perm[j * nc + ci, ci * c + j] = 1.0
    lane = np.arange(MIX_W) % RW_DIM
    eye4 = (lane[None, :] == np.arange(RW_DIM)[:, None]).astype(np.float32)
    lanes = np.arange(nc * 2 * c * RW_HEADS)
    lane_chunk, lane_tok = lanes // (2 * c * RW_HEADS), lanes % (2 * c)
    t = np.arange(RW_TILE)
    asel = ((lane_chunk[None, :] == (t // c)[:, None]) & (lane_tok[None, :] == (c + t % c)[:, None])).astype(np.float32)
    as16 = lambda a: jnp.asarray(a, BF16)
    return as16(perm), as16(perm.T), as16(eye4), as16(asel)


def _rwkv_branch(lay, rw, pre_consts, g64, lng, lnb):
    r_, v_, kk_, lw_, kka_, km_, bon, gate = _rw_pre_call(lay, rw, pre_consts)
    perm, permt, eye4, asel = _rw_constants()
    prep_consts = (perm, permt, g64, eye4)
    fwd = _rw_prep_call(lay, (r_, kk_, v_), (lw_, kka_, km_), prep_consts, False)
    rev = _rw_prep_call(lay, (r_, kk_, v_), (lw_, kka_, km_), prep_consts, True)
    pick = lambda o: (o[0], o[1], o[2], o[4])
    ytf, ytr = _rw_scan_call(lay, pick(fwd), pick(rev))
    return _rw_fin_call(lay, ytf, ytr, fwd[3], rev[3], bon, gate, (asel, g64, lng, lnb))


def _merge_kernel(x_ref, mod_ref, g_ref, wg_ref, ya_ref, yb_ref, yc_ref, yd_ref, wb_ref, wo_ref, o_ref):
    x = x_ref[...]
    h = _modulate(x, g_ref[...], mod_ref[0, 0:1, :], mod_ref[0, 1:2, :]).astype(BF16)
    merged = None
    for i, y_ref in enumerate((ya_ref, yb_ref, yc_ref, yd_ref)):
        gate = _sigmoid(jnp.dot(h, wg_ref[:, i * D_MODEL:(i + 1) * D_MODEL], preferred_element_type=F32))
        term = gate * jnp.dot(y_ref[...], wb_ref[i], preferred_element_type=F32)
        merged = term if merged is None else merged + term
    out = jnp.dot(merged.astype(BF16), wo_ref[...], preferred_element_type=F32)
    o_ref[...] = x + mod_ref[0, 2:3, :] * out


def _merge_call(lay, with_ctx, x_all, mod, g, w_gate, ya, yb, yc, yd, w_branch, w_out):
    tm = lay.t
    src, mrow = lay.src_tile(with_ctx), lay.mod_row(with_ctx)
    full_row = lambda w: pl.BlockSpec((tm, w), lambda i: (src(i), 0))
    out_row = lambda w: pl.BlockSpec((tm, w), lambda i: (i, 0))
    return pl.pallas_call(
        _merge_kernel,
        grid=(lay.n_tiles(with_ctx),),
        in_specs=[full_row(D_MODEL), pl.BlockSpec((1, 6, D_MODEL), lambda i: (mrow(i), 0, 0)), _full(g), _full(w_gate),
                  out_row(MIX_W), full_row(MIX_W), out_row(MIX_W), full_row(MIX_W), _full(w_branch), _full(w_out)],
        out_specs=out_row(D_MODEL),
        out_shape=jax.ShapeDtypeStruct((lay.rows(with_ctx), D_MODEL), F32),
        compiler_params=_cparams("parallel"),
        name="merge_out",
    )(x_all, mod, g, w_gate, ya, yb, yc, yd, w_branch, w_out)


def _router_kernel(x_ref, mod_ref, g_ref, wh_ref, wl_ref, bias_ref, f_ref, comb_ref, gid_ref):
    f = _modulate(x_ref[...], g_ref[...], mod_ref[0, 3:4, :], mod_ref[0, 4:5, :])
    fh = f.astype(BF16)
    f_ref[...] = fh
    fl = (f - fh.astype(F32)).astype(BF16)
    nt = (((1,), (1,)), ((), ()))
    wh, wl = wh_ref[...], wl_ref[...]
    logits = (lax.dot_general(wh, fh, nt, preferred_element_type=F32)
              + lax.dot_general(wh, fl, nt, preferred_element_type=F32)
              + lax.dot_general(wl, fh, nt, preferred_element_type=F32))
    scores = _sigmoid(logits)
    biased = scores + bias_ref[...]
    sc = [scores[e:e + 1, :] for e in range(N_EXPERTS)]
    bi = [biased[e:e + 1, :] for e in range(N_EXPERTS)]
    group_score = []
    for g in range(N_GROUPS):
        a, b, c, d = bi[4 * g:4 * g + 4]
        m1, n1, m2, n2 = jnp.maximum(a, b), jnp.minimum(a, b), jnp.maximum(c, d), jnp.minimum(c, d)
        group_score.append(jnp.maximum(m1, m2) + jnp.maximum(jnp.minimum(m1, m2), jnp.maximum(n1, n2)))

    def first_argmax(vals):
        top = functools.reduce(jnp.maximum, vals)
        seen, hot = None, []
        for v in vals:
            h = v == top
            if seen is not None:
                h = jnp.logical_and(h, jnp.logical_not(seen))
            seen = h if seen is None else jnp.logical_or(seen, h)
            hot.append(h)
        return hot

    in_group = first_argmax(group_score)
    masked = [jnp.where(in_group[e // EXPERTS_PER_GROUP], bi[e], -jnp.inf) for e in range(N_EXPERTS)]
    hot1 = first_argmax(masked)
    hot2 = first_argmax([jnp.where(h, -jnp.inf, v) for h, v in zip(hot1, masked)])
    w1 = functools.reduce(jnp.add, [jnp.where(h, s, 0.0) for h, s in zip(hot1, sc)])
    w2 = functools.reduce(jnp.add, [jnp.where(h, s, 0.0) for h, s in zip(hot2, sc)])
    inv_tot = 1.0 / (w1 + w2)
    for e in range(N_EXPERTS):
        comb_ref[e:e + 1, :] = (jnp.where(hot1[e], w1, 0.0) + jnp.where(hot2[e], w2, 0.0)) * inv_tot
    gid_ref[...] = functools.reduce(jnp.add, [jnp.where(in_group[g], g, 0) for g in range(1, N_GROUPS)])


def _router_call(lay, with_ctx, x, mod, g, wh, wl, bias):
    t, tm = x.shape[0], lay.t
    mrow = lay.mod_row(with_ctx)
    return pl.pallas_call(
        _router_kernel,
        grid=(t // tm,),
        in_specs=[pl.BlockSpec((tm, D_MODEL), lambda i: (i, 0)),
                  pl.BlockSpec((1, 6, D_MODEL), lambda i: (mrow(i), 0, 0)), _full(g), _full(wh), _full(wl), _full(bias)],
        out_specs=[pl.BlockSpec((tm, D_MODEL), lambda i: (i, 0)), pl.BlockSpec((N_EXPERTS, tm), lambda i: (0, i)),
                   pl.BlockSpec((1, tm), lambda i: (0, i))],
        out_shape=[jax.ShapeDtypeStruct((t, D_MODEL), BF16), jax.ShapeDtypeStruct((N_EXPERTS, t), F32),
                   jax.ShapeDtypeStruct((1, t), jnp.int32)],
        compiler_params=_cparams("parallel"),
        name="moe_router",
    )(x, mod, g, wh, wl, bias)


def _moe_plan(gid, n_tiles, tm):
    g = gid.reshape(n_tiles, tm)
    onehot = (g[..., None] == jnp.arange(N_GROUPS, dtype=jnp.int32)).astype(jnp.int32)
    rank = jnp.cumsum(onehot, axis=1) - onehot
    counts = jnp.sum(onehot, axis=1)
    padded = (counts + 15) // 16 * 16
    offs = jnp.cumsum(padded, axis=1) - padded
    pos = jnp.sum(onehot * (offs[:, None, :] + rank), axis=-1)
    n_over = (jnp.maximum(padded - MOE_BLOCK, 0) + MOE_OVER - 1) // MOE_OVER
    return pos.astype(jnp.int32), offs.astype(jnp.int32), n_over.astype(jnp.int32)


def _moe_kernel(offs_ref, nover_ref, f_ref, posr_ref, posc_ref, comb_ref, wg_ref, wu_ref, wd_ref, x_ref, modb_ref,
                modc_ref, o_ref, xs_scr, cs_scr, ys_scr, *, ctx_rows, tiles_per_seq):
    i, e = pl.program_id(0), pl.program_id(1)
    n_slots, tm = xs_scr.shape[0], f_ref.shape[0]
    n_live = min(n_slots, -(-(tm + 16 * N_GROUPS) // 256) * 256)

    @pl.when(e == 0)
    def _():
        slot = lax.broadcasted_iota(jnp.int32, (n_live, tm), 0)
        place = (slot == posr_ref[0]).astype(BF16)
        xs_scr[0:n_live, :] = jnp.dot(place, f_ref[...], preferred_element_type=F32).astype(BF16)
        xs_scr[n_live:n_slots, :] = jnp.zeros((n_slots - n_live, D_MODEL), BF16)
        cs_scr[0:n_live, :] = _split_dot_rhs(place, comb_ref[...])
        cs_scr[n_live:n_slots, :] = jnp.zeros((n_slots - n_live, N_EXPERTS), F32)
        ys_scr[...] = jnp.zeros_like(ys_scr)

    grp = lax.shift_right_logical(e, 2)
    start = offs_ref[i, grp]
    lane = lax.broadcasted_iota(jnp.int32, (1, N_EXPERTS), 1)

    def run(rows):
        xb = xs_scr[rows, :]
        gate = jnp.dot(xb, wg_ref[0], preferred_element_type=F32)
        up = jnp.dot(xb, wu_ref[0], preferred_element_type=F32)
        act = (gate * _sigmoid(gate) * up).astype(BF16)
        down = jnp.dot(act, wd_ref[0], preferred_element_type=F32)
        c_e = jnp.sum(jnp.where(lane == e, cs_scr[rows, :], 0.0), axis=1, keepdims=True)
        ys_scr[rows, :] += c_e * down

    run(pl.ds(pl.multiple_of(start, 16), MOE_BLOCK))

    def overflow(k, carry):
        run(pl.ds(pl.multiple_of(start + MOE_BLOCK + k * MOE_OVER, 16), MOE_OVER))
        return carry

    lax.fori_loop(0, nover_ref[i, grp], overflow, 0)

    @pl.when(e == N_EXPERTS - 1)
    def _():
        slot = lax.broadcasted_iota(jnp.int32, (tm, n_live), 1)
        fetch = (slot == posc_ref[...]).astype(BF16)
        y = jnp.dot(fetch, ys_scr[0:n_live, :].astype(BF16), preferred_element_type=F32)
        res_gate = modb_ref[0, 5:6, :]
        if ctx_rows:
            row = lax.broadcasted_iota(jnp.int32, y.shape, 0)
            first = i % tiles_per_seq == 0
            res_gate = jnp.where(jnp.logical_and(first, row < ctx_rows), modc_ref[0, 5:6, :], res_gate)
        o_ref[...] = x_ref[...] + res_gate * y


def _moe_call(lay, with_ctx, f, comb, gid, wg, wu, wd, x, mod):
    t = f.shape[0]
    seq = lay.n_tot if with_ctx else lay.n_lat
    tm = MOE_TILE if seq % MOE_TILE == 0 else math.gcd(seq, 1024)
    tps, n_tiles = seq // tm, t // tm
    ctx_rows = lay.n_ctx if with_ctx else 0
    assert ctx_rows <= tm
    n_slots = -(-(tm + 16 * N_GROUPS + MOE_BLOCK + MOE_OVER) // 256) * 256
    pos, offs, n_over = _moe_plan(gid, n_tiles, tm)
    wspec = lambda a: pl.BlockSpec((1,) + a.shape[1:], lambda i, e, *_: (e, 0, 0))
    tok = lambda w: pl.BlockSpec((tm, w), lambda i, e, *_: (i, 0))
    grid_spec = pltpu.PrefetchScalarGridSpec(
        num_scalar_prefetch=2,
        grid=(n_tiles, N_EXPERTS),
        in_specs=[tok(D_MODEL), pl.BlockSpec((1, 1, tm), lambda i, e, *_: (i, 0, 0)), tok(1), tok(N_EXPERTS),
                  wspec(wg), wspec(wu), wspec(wd), tok(D_MODEL),
                  pl.BlockSpec((1, 6, D_MODEL), lambda i, e, *_: (i // tps, 0, 0)),
                  pl.BlockSpec((1, 6, D_MODEL), lambda i, e, *_: (lay.b, 0, 0))],
        out_specs=tok(D_MODEL),
        scratch_shapes=[pltpu.VMEM((n_slots, D_MODEL), BF16), pltpu.VMEM((n_slots, N_EXPERTS), F32),
                        pltpu.VMEM((n_slots, D_MODEL), F32)])
    return pl.pallas_call(
        functools.partial(_moe_kernel, ctx_rows=ctx_rows, tiles_per_seq=tps),
        grid_spec=grid_spec,
        out_shape=jax.ShapeDtypeStruct((t, D_MODEL), F32),
        compiler_params=_cparams("parallel", "arbitrary"),
        name="moe_experts",
    )(offs, n_over, f, pos.reshape(n_tiles, 1, tm), pos.reshape(t, 1), comb, wg, wu, wd, x, mod, mod)


def _block_ones(n, group):
    i = np.arange(n) // group
    return jnp.asarray(i[:, None] == i[None, :], dtype=BF16)


def _rope_tables(n_ctx, n_lat):
    rows = n_lat // GRID_W
    row = jnp.repeat(jnp.arange(rows, dtype=F32), GRID_W)
    col = jnp.tile(jnp.arange(GRID_W, dtype=F32), rows)

    def angles(rot_dim):
        n_freq = rot_dim // 4
        inv_freq = ROPE_BASE ** (-jnp.arange(n_freq, dtype=F32) / n_freq)
        ang = jnp.concatenate([row[:, None] * inv_freq, col[:, None] * inv_freq], axis=-1)
        return jnp.cos(ang), jnp.sin(ang)

    c, s = angles(DA_DIM)
    cda = jnp.tile(jnp.concatenate([c, c], -1), (1, 2 * DA_HEADS))
    sda = jnp.tile(jnp.concatenate([-s, s], -1), (1, 2 * DA_HEADS))
    c, s = angles(MLA_ROPE)
    one = jnp.ones((n_lat, MLA_NOPE), F32)
    pad = MLA_HEAD_PAD - MLA_NOPE - MLA_ROPE
    cml = jnp.tile(jnp.concatenate([one, c, c, jnp.ones((n_lat, pad), F32)], -1), (1, MLA_HEADS))
    sml = jnp.tile(jnp.concatenate([0 * one, -s, s, jnp.zeros((n_lat, pad), F32)], -1), (1, MLA_HEADS))
    ident = lambda t, v: jnp.concatenate([jnp.full((n_ctx, MIX_W), v, F32), t], axis=0)
    return ident(cda, 1.0), ident(sda, 0.0), ident(cml, 1.0), ident(sml, 0.0)


def _pad_heads(w, n_heads, src_w, lo, hi, dst_w=MLA_HEAD_PAD):
    w = w.reshape(w.shape[0], n_heads, src_w)[:, :, lo:hi]
    w = jnp.pad(w, ((0, 0), (0, 0), (0, dst_w - (hi - lo))))
    return w.reshape(w.shape[0], n_heads * dst_w)


def _mix_weight(w_in_l):
    w = w_in_l
    kr = w[:, 1344:1360]
    z = lambda n: jnp.zeros((D_MODEL, n), w.dtype)
    kr_wide = jnp.concatenate([jnp.concatenate([z(MLA_NOPE), kr, z(MLA_HEAD_PAD - MLA_NOPE - MLA_ROPE)], 1)] * MLA_HEADS, 1)
    return jnp.concatenate([w[:, 0:1024], w[:, 1024:1216], z(64), w[:, 1216:1344], kr_wide, w[:, 1360:2384]], axis=1).astype(BF16)


def kernel(x, c, ctx, c_ctx, w_ada, b_ada, norm_mix_g, norm_ffn_g, w_in, da_qk_norm_g, da_lambda, da_subln_g, s5_lam_re, s5_lam_im, s5_log_dt, s5_b_re, s5_b_im, s5_c_re, s5_c_im, s5_d, s5_w_glu, s5_b_glu, mla_cq_norm_g, mla_ckv_norm_g, mla_w_uq, mla_w_ukv, mla_qk_norm_g, rw_mu, rw_w0, rw_w1, rw_w2, rw_a0, rw_a1, rw_a2, rw_g1, rw_g2, rw_k_k, rw_k_a, rw_r_k, rw_ln_g, rw_ln_b, w_branch, w_out, router_w, router_bias, exp_w_gate, exp_w_up, exp_w_down):
    b, n_lat, dm = x.shape
    n_ctx = ctx.shape[1]
    depth = w_ada.shape[0]
    assert dm == D_MODEL
    lay = _Layout(b, n_ctx, n_lat)
    t_all = b * lay.n_tot
    tm_big = 2 * lay.t

    g32 = _block_ones(MIX_W, DA_DIM)
    g64 = _block_ones(MIX_W, RW_DIM)
    tabs = _rope_tables(n_ctx, n_lat)
    row = lambda v: v.reshape(1, -1).astype(F32)
    bf = lambda a: a.astype(BF16)

    cc = jnp.zeros((16, dm), F32).at[:b].set(c).at[b].set(c_ctx)
    mod_all = _ada_call(cc, w_ada, b_ada)
    x_all = jnp.concatenate([ctx, x], axis=1).reshape(t_all, dm)

    wr_hi = router_w.T.astype(BF16)
    wr_lo = (router_w.T - wr_hi.astype(F32)).astype(BF16)
    r_bias = router_bias.reshape(N_EXPERTS, 1).astype(F32)

    for l in range(depth):
        need_ctx = l < depth - 1
        lambda_init = 0.8 - 0.6 * math.exp(-0.3 * l)
        mod = mod_all[l, :b + 1].reshape(b + 1, 6, dm)
        g_mix = row(norm_mix_g[l])
        da, s5a, s5b, mla, rw = _inproj_call(lay, x_all, mod, g_mix, _mix_weight(w_in[l]))

        log2e = math.log2(math.e)
        gda = jnp.stack([jnp.tile(da_qk_norm_g[l, 0], 2 * DA_HEADS) * (DA_DIM ** -0.5 * log2e), jnp.tile(da_qk_norm_g[l, 1], 2 * DA_HEADS)])
        mla_pad = MLA_HEAD_PAD - MLA_NOPE - MLA_ROPE
        gml = jnp.stack([jnp.tile(jnp.pad(mla_qk_norm_g[l, 0], (0, mla_pad)), MLA_HEADS) * ((MLA_NOPE + MLA_ROPE) ** -0.5 * log2e),
                         jnp.tile(jnp.pad(mla_qk_norm_g[l, 1], (0, mla_pad)), MLA_HEADS)])
        wuq = bf(jnp.pad(_pad_heads(mla_w_uq[l], MLA_HEADS, MLA_NOPE + MLA_ROPE, 0, MLA_NOPE + MLA_ROPE), ((0, 64), (0, 0))))
        wuk = bf(_pad_heads(mla_w_ukv[l], MLA_HEADS, MLA_NOPE + MLA_VDIM, 0, MLA_NOPE))
        wuv = bf(_pad_heads(mla_w_ukv[l], MLA_HEADS, MLA_NOPE + MLA_VDIM, MLA_NOPE, MLA_NOPE + MLA_VDIM))
        consts = (g32, g64, gda.astype(F32), gml.astype(F32), row(jnp.pad(mla_cq_norm_g[l], (0, 64))), row(mla_ckv_norm_g[l]),
                  wuq, wuk, wuv)
        qd, kdt, vd, qm, kmt, vm = _qkprep_call(lay, da, mla, tabs, consts)

        lam32 = da_lambda[l].astype(F32)
        lmbda = (jnp.exp(jnp.sum(lam32[0] * lam32[1])) - jnp.exp(jnp.sum(lam32[2] * lam32[3])) + lambda_init).reshape(1, 1)
        subln = row(jnp.tile(da_subln_g[l], DA_HEADS) * (1.0 - lambda_init))
        ya = _attention(lay, qd, kdt, vd, (lmbda, subln, g64), True, need_ctx, "diff_attn")
        yc = _attention(lay, qm, kmt, vm, (lmbda, subln, g64), False, need_ctx, "mla_attn")

        mats = _s5_mats(s5_lam_re[l], s5_lam_im[l], s5_log_dt[l], s5_b_re[l], s5_b_im[l], s5_c_re[l], s5_c_im[l])
        ys_a, ys_b = _s5_scan(lay, s5a, s5b, mats)
        yb = _s5_glu_call(s5a, s5b, ys_a, ys_b, row(s5_d[l]), bf(s5_w_glu[l]), row(s5_b_glu[l]), tm_big)

        pre_consts = (row(rw_mu[l]), g64, row(rw_k_k[l]), row(rw_k_a[l]), row(rw_r_k[l]),
                      rw_w0[l].reshape(2, 1, MIX_W), bf(rw_w1[l]), bf(rw_w2[l]),
                      rw_a0[l].reshape(2, 1, MIX_W), bf(rw_a1[l]), bf(rw_a2[l]), bf(rw_g1[l]), bf(rw_g2[l]))
        yd = _rwkv_branch(lay, rw, pre_consts, g64, row(rw_ln_g[l]), row(rw_ln_b[l]))

        x_mid = _merge_call(lay, need_ctx, x_all, mod, g_mix, bf(w_in[l][:, 2384:]), ya, yb, yc, yd,
                            bf(w_branch[l]), bf(w_out[l]))
        f, comb_t, gid = _router_call(lay, need_ctx, x_mid, mod, row(norm_ffn_g[l]), wr_hi, wr_lo, r_bias)
        x_all = _moe_call(lay, need_ctx, f, comb_t.T, gid, bf(exp_w_gate[l]), bf(exp_w_up[l]), bf(exp_w_down[l]), x_mid, mod)
    return x_all.reshape(b, n_lat, dm)
```

```python
import functools
import math

import numpy as np
import jax
import jax.numpy as jnp
from jax import lax
from jax.experimental import pallas as pl
from jax.experimental.pallas import tpu as pltpu

F32 = jnp.float32
BF16 = jnp.bfloat16

D_MODEL = 1024
GRID_W = 64
ROPE_BASE = 10000.0
EPS = 1e-6
DA_HEADS, DA_DIM, DA_VDIM = 4, 32, 64
S5_GROUPS, S5_CH, S5_STATE = 16, 16, 64
MLA_HEADS, MLA_NOPE, MLA_ROPE, MLA_VDIM = 4, 32, 16, 64
MLA_Q_RANK, MLA_KV_RANK = 192, 128
MLA_HEAD_PAD = 64
RW_HEADS, RW_DIM = 4, 64
RW_LN_EPS = 64e-5
N_BRANCH = 4
N_EXPERTS, N_GROUPS, EXPERTS_PER_GROUP = 16, 4, 4
D_FF = 512
MIX_W = 256

S5_CHUNK = 8
S5_FLAT = S5_CHUNK * MIX_W
S5_STATE_W = S5_GROUPS * S5_STATE
RW_CHUNK = 16
RW_TILE = 128
RW_PREP_TILES = 2
_NT = (((1,), (1,)), ((), ()))
TOKEN_TILE = 256
MOE_TILE = 1152
MOE_BLOCK = 384
MOE_OVER = 128

_DA_W, _S5_W, _MLA_W, _RW_W = 768, 256, 640, 1024
_MIX_COLS = _DA_W + _S5_W + _MLA_W + _RW_W

V7X_VMEM_BYTES = 64 * 2**20
_VMEM_LIMIT = V7X_VMEM_BYTES - 8 * 2**20


def _cparams(*sem):
    return pltpu.CompilerParams(dimension_semantics=sem, vmem_limit_bytes=_VMEM_LIMIT)


def _full(a):
    return pl.BlockSpec(a.shape, lambda *_, nd=a.ndim: (0,) * nd)


def _split_dot(x, w, terms=2):
    acc = None
    rem = x
    for i in range(terms):
        part = rem.astype(BF16)
        d = jnp.dot(part, w, preferred_element_type=F32)
        acc = d if acc is None else acc + d
        if i + 1 < terms:
            rem = rem - part.astype(F32)
    return acc


def _split_dot_rhs(w, x):
    hi = x.astype(BF16)
    lo = (x - hi.astype(F32)).astype(BF16)
    return jnp.dot(w, hi, preferred_element_type=F32) + jnp.dot(w, lo, preferred_element_type=F32)


def _modulate(x, g, shift, scale):
    xn = x * lax.rsqrt(jnp.mean(x * x, axis=-1, keepdims=True) + EPS)
    return xn * g * (1.0 + scale) + shift


def _sigmoid(x):
    return 1.0 / (1.0 + jnp.exp(-x))


def _group_rms(x, ones_bd, inv_n, gain):
    ms = _split_dot(x * x, ones_bd) * inv_n
    return x * lax.rsqrt(ms + EPS) * gain


def _lane_partner(x, half, period, first_end):
    n = x.shape[1]
    lane = lax.broadcasted_iota(jnp.int32, x.shape, 1)
    up = pltpu.roll(x, n - half, axis=1)
    down = pltpu.roll(x, half, axis=1)
    return jnp.where((lane & (period - 1)) < first_end, up, down)


def _rope(x, cos_t, sin_t, half, period, first_end):
    return x * cos_t + _lane_partner(x, half, period, first_end) * sin_t


def _ada_kernel(c_ref, w_ref, b_ref, o_ref):
    c = c_ref[...]
    s = c * _sigmoid(c)
    o_ref[0] = jnp.dot(s.astype(BF16), w_ref[0].astype(BF16), preferred_element_type=F32) + b_ref[0]


def _ada_call(cc, w_ada, b_ada):
    depth, dm, n = w_ada.shape
    tn = n // 4
    return pl.pallas_call(
        _ada_kernel,
        grid=(depth, n // tn),
        in_specs=[
            pl.BlockSpec(cc.shape, lambda l, j: (0, 0)),
            pl.BlockSpec((1, dm, tn), lambda l, j: (l, 0, j)),
            pl.BlockSpec((1, 1, tn), lambda l, j: (l, 0, j)),
        ],
        out_specs=pl.BlockSpec((1, cc.shape[0], tn), lambda l, j: (l, 0, j)),
        out_shape=jax.ShapeDtypeStruct((depth, cc.shape[0], n), F32),
        compiler_params=_cparams("parallel", "parallel"),
        name="ada_mod",
    )(cc, w_ada, b_ada.reshape(depth, 1, n))


class _Layout:
    def __init__(self, n_batch, n_ctx, n_lat):
        t = TOKEN_TILE
        assert n_ctx % t == 0 and n_lat % t == 0
        self.b, self.n_ctx, self.n_lat, self.n_tot = n_batch, n_ctx, n_lat, n_ctx + n_lat
        self.t = t
        self.ctx_tiles, self.lat_tiles, self.seq_tiles = n_ctx // t, n_lat // t, (n_ctx + n_lat) // t

    def rows(self, with_ctx):
        return self.b * (self.n_tot if with_ctx else self.n_lat)

    def n_tiles(self, with_ctx):
        return self.b * (self.seq_tiles if with_ctx else self.lat_tiles)

    def src_tile(self, with_ctx):
        if with_ctx:
            return lambda i: i
        return lambda i: (i // self.lat_tiles) * self.seq_tiles + i % self.lat_tiles + self.ctx_tiles

    def mod_row(self, with_ctx):
        if with_ctx:
            return lambda i: jnp.where(i % self.seq_tiles < self.ctx_tiles, self.b, i // self.seq_tiles)
        return lambda i: i // self.lat_tiles


def _inproj_kernel(x_ref, mod_ref, g_ref, w_ref, da_ref, s5a_ref, s5b_ref, mla_ref, rw_ref):
    h = _modulate(x_ref[...], g_ref[...], mod_ref[0, 0:1, :], mod_ref[0, 1:2, :])
    acc = jnp.dot(h.astype(BF16), w_ref[...], preferred_element_type=F32)
    da_ref[...] = acc[:, 0:_DA_W]
    s5a_ref[...] = acc[:, _DA_W:_DA_W + _S5_W // 2]
    s5b_ref[...] = acc[:, _DA_W + _S5_W // 2:_DA_W + _S5_W]
    mla_ref[...] = acc[:, _DA_W + _S5_W:_DA_W + _S5_W + _MLA_W]
    rw_ref[...] = acc[:, _DA_W + _S5_W + _MLA_W:_MIX_COLS]


def _inproj_call(lay, x_all, mod, g, w_mix):
    t, tm = x_all.shape[0], lay.t
    widths = (_DA_W, _S5_W // 2, _S5_W // 2, _MLA_W, _RW_W)
    mrow = lay.mod_row(True)
    return pl.pallas_call(
        _inproj_kernel,
        grid=(t // tm,),
        in_specs=[
            pl.BlockSpec((tm, D_MODEL), lambda i: (i, 0)),
            pl.BlockSpec((1, 6, D_MODEL), lambda i: (mrow(i), 0, 0)),
            _full(g), _full(w_mix),
        ],
        out_specs=[pl.BlockSpec((tm, w), lambda i: (i, 0)) for w in widths],
        out_shape=[jax.ShapeDtypeStruct((t, w), F32) for w in widths],
        compiler_params=_cparams("parallel"),
        name="in_proj",
    )(x_all, mod, g, w_mix)


def _qkprep_kernel(da_ref, mla_ref, cda_ref, sda_ref, cml_ref, sml_ref, g32_ref, g64_ref,
                   gda_ref, gml_ref, cqg_ref, ckvg_ref, wuq_ref, wuk_ref, wuv_ref,
                   qd_ref, kd_ref, vd_ref, qm_ref, km_ref, vm_ref):
    g32 = g32_ref[...]
    g64 = g64_ref[...]
    cda, sda = cda_ref[...], sda_ref[...]
    q = _group_rms(da_ref[:, 0:MIX_W], g32, 1.0 / DA_DIM, gda_ref[0:1, :])
    qd_ref[...] = _rope(q, cda, sda, DA_DIM // 2, DA_DIM, DA_DIM // 2).astype(BF16)
    k = _group_rms(da_ref[:, MIX_W:2 * MIX_W], g32, 1.0 / DA_DIM, gda_ref[1:2, :])
    kd_ref[0] = _rope(k, cda, sda, DA_DIM // 2, DA_DIM, DA_DIM // 2).T.astype(BF16)
    vd_ref[...] = da_ref[:, 2 * MIX_W:3 * MIX_W].astype(BF16)

    cml, sml = cml_ref[...], sml_ref[...]
    cq = mla_ref[:, 0:256]
    cqn = cq * lax.rsqrt(jnp.sum(cq * cq, axis=-1, keepdims=True) * (1.0 / MLA_Q_RANK) + EPS) * cqg_ref[...]
    q = jnp.dot(cqn.astype(BF16), wuq_ref[...], preferred_element_type=F32)
    ckv = mla_ref[:, 256:384]
    ckvn = ckv * lax.rsqrt(jnp.mean(ckv * ckv, axis=-1, keepdims=True) + EPS) * ckvg_ref[...]
    ckvb = ckvn.astype(BF16)
    k = jnp.dot(ckvb, wuk_ref[...], preferred_element_type=F32) + mla_ref[:, 384:640]
    vm_ref[...] = jnp.dot(ckvb, wuv_ref[...], preferred_element_type=F32).astype(BF16)
    inv_n = 1.0 / (MLA_NOPE + MLA_ROPE)
    half = MLA_ROPE // 2
    q = _group_rms(q, g64, inv_n, gml_ref[0:1, :])
    qm_ref[...] = _rope(q, cml, sml, half, MLA_HEAD_PAD, MLA_NOPE + half).astype(BF16)
    k = _group_rms(k, g64, inv_n, gml_ref[1:2, :])
    km_ref[0] = _rope(k, cml, sml, half, MLA_HEAD_PAD, MLA_NOPE + half).T.astype(BF16)


def _qkprep_call(lay, da, mla, tabs, consts):
    t, tm = da.shape[0], lay.t
    st = lay.seq_tiles
    row = pl.BlockSpec((tm, MIX_W), lambda i: (i, 0))
    key_t = pl.BlockSpec((1, MIX_W, tm), lambda i: (i // st, 0, i % st))
    in_specs = [pl.BlockSpec((tm, _DA_W), lambda i: (i, 0)), pl.BlockSpec((tm, _MLA_W), lambda i: (i, 0))]
    in_specs += [pl.BlockSpec((tm, MIX_W), lambda i: (i % st, 0)) for _ in tabs]
    in_specs += [_full(a) for a in consts]
    tok = jax.ShapeDtypeStruct((t, MIX_W), BF16)
    keys = jax.ShapeDtypeStruct((lay.b, MIX_W, lay.n_tot), BF16)
    return pl.pallas_call(
        _qkprep_kernel,
        grid=(t // tm,),
        in_specs=in_specs,
        out_specs=[row, key_t, row, row, key_t, row],
        out_shape=[tok, keys, tok, tok, keys, tok],
        compiler_params=_cparams("parallel"),
        name="qk_prep",
    )(da, mla, *tabs, *consts)


def _softmax_parts(s):
    p = jnp.exp2(s - jnp.max(s, axis=-1, keepdims=True))
    return p, 1.0 / jnp.sum(p, axis=-1, keepdims=True)


def _attn_heads(q, kt_ref, v_ref, nk, diff, lam):
    lane = lax.broadcasted_iota(jnp.int32, (q.shape[0], MIX_W), 1)
    v = v_ref[0, 0:nk, :]
    acc = jnp.zeros((q.shape[0], MIX_W), F32)
    dk = DA_DIM if diff else MLA_HEAD_PAD
    per_head = 2 if diff else 1

    def scores(h):
        return [jnp.dot(q[:, e * dk:(e + 1) * dk], kt_ref[0, e * dk:(e + 1) * dk, 0:nk], preferred_element_type=F32)
                for e in range(per_head * h, per_head * (h + 1))]

    ahead = scores(0)
    for h in range(DA_HEADS):
        s = ahead
        if h + 1 < DA_HEADS:
            ahead = scores(h + 1)
        if diff:
            p0, r0 = _softmax_parts(s[0])
            p1, r1 = _softmax_parts(s[1])
            o = jnp.dot((p0 * r0 - p1 * (r1 * lam)).astype(BF16), v, preferred_element_type=F32)
        else:
            p, r = _softmax_parts(s[0])
            o = jnp.dot(p.astype(BF16), v, preferred_element_type=F32) * r
        in_head = jnp.logical_and(lane >= h * DA_VDIM, lane < (h + 1) * DA_VDIM)
        acc = jnp.where(in_head, o, acc)
    return acc


def _attn_kernel(q_ref, kt_ref, v_ref, lam_ref, gain_ref, g64_ref, o_ref, *, diff, n_ctx, n_tot, ctx_tiles):
    q = q_ref[...]
    lam = lam_ref[...]

    def run(nk):
        o = _attn_heads(q, kt_ref, v_ref, nk, diff, lam)
        if diff:
            o = _group_rms(o, g64_ref[...], 1.0 / DA_VDIM, gain_ref[...])
        o_ref[...] = o.astype(BF16)

    if ctx_tiles:
        is_ctx = pl.program_id(1) < ctx_tiles
        pl.when(is_ctx)(lambda: run(n_ctx))
        pl.when(jnp.logical_not(is_ctx))(lambda: run(n_tot))
    else:
        run(n_tot)


def _attention(lay, q, kt, v, extra, diff, with_ctx, name):
    tq = lay.t
    tiles = lay.seq_tiles if with_ctx else lay.lat_tiles
    off = 0 if with_ctx else lay.ctx_tiles
    v3 = v.reshape(lay.b, lay.n_tot, MIX_W)
    kern = functools.partial(_attn_kernel, diff=diff, n_ctx=lay.n_ctx, n_tot=lay.n_tot,
                             ctx_tiles=lay.ctx_tiles if with_ctx else 0)
    return pl.pallas_call(
        kern,
        grid=(lay.b, tiles),
        in_specs=[
            pl.BlockSpec((tq, MIX_W), lambda b, j: (b * lay.seq_tiles + j + off, 0)),
            pl.BlockSpec((1, MIX_W, lay.n_tot), lambda b, j: (b, 0, 0)),
            pl.BlockSpec((1, lay.n_tot, MIX_W), lambda b, j: (b, 0, 0)),
        ] + [_full(a) for a in extra],
        out_specs=pl.BlockSpec((tq, MIX_W), lambda b, j: (b * tiles + j, 0)),
        out_shape=jax.ShapeDtypeStruct((lay.rows(with_ctx), MIX_W), BF16),
        compiler_params=_cparams("parallel", "parallel"),
        name=name,
    )(q, kt, v3, *extra)


def _chunk_rows(ua_ref, ub_ref):
    n = ua_ref.shape[0] // S5_CHUNK
    parts = []
    for s in range(S5_CHUNK):
        rows = pl.ds(s, n, stride=S5_CHUNK)
        parts += [ua_ref[rows, :], ub_ref[rows, :]]
    return jnp.concatenate(parts, axis=1).astype(BF16)


def _s5_proj_kernel(ua_ref, ub_ref, bre_ref, bim_ref, sre_ref, sim_ref):
    u = _chunk_rows(ua_ref, ub_ref)
    sre_ref[0] = jnp.dot(u, bre_ref[0], preferred_element_type=F32)
    sim_ref[0] = jnp.dot(u, bim_ref[0], preferred_element_type=F32)


def _s5_rec_kernel(sre_ref, sim_ref, are_ref, aim_ref, hre_ref, him_ref, *, n_batch, n_chunks, ctx_chunks):
    rev = pl.program_id(0) == 1
    ar, ai = are_ref[0], aim_ref[0]
    sre, sim, hre, him = sre_ref.at[0], sim_ref.at[0], hre_ref.at[0], him_ref.at[0]

    def step(i, carry):
        hr, hi = carry
        k_rev = jnp.where(i < ctx_chunks, ctx_chunks - 1 - i, n_chunks - 1 + ctx_chunks - i)
        k = jnp.where(rev, k_rev, i)
        rows = pl.ds(k, n_batch, stride=n_chunks)
        hre[rows, :] = hr
        him[rows, :] = hi
        return ar * hr - ai * hi + sre[rows, :], ar * hi + ai * hr + sim[rows, :]

    zero = jnp.zeros((n_batch, 128), F32)
    lax.fori_loop(0, n_chunks, step, (zero, zero))


def _s5_out_kernel(ua_ref, ub_ref, hre_ref, him_ref, m_ref, cre_ref, cim_ref, ya_ref, yb_ref):
    y = jnp.dot(_chunk_rows(ua_ref, ub_ref), m_ref[0], preferred_element_type=F32)
    y = y + _split_dot(hre_ref[0], cre_ref[0]) + _split_dot(him_ref[0], cim_ref[0])
    n = y.shape[0]
    ya, yb = ya_ref.at[0], yb_ref.at[0]
    for s in range(S5_CHUNK):
        rows = pl.ds(s, n, stride=S5_CHUNK)
        ya[rows, :] = y[:, s * MIX_W:s * MIX_W + 128]
        yb[rows, :] = y[:, s * MIX_W + 128:(s + 1) * MIX_W]


def _s5_mats(lam_re, lam_im, log_dt, b_re, b_im, c_re, c_im):
    hp = lax.Precision.HIGHEST
    L, G, P, CH = S5_CHUNK, S5_GROUPS, S5_STATE, S5_CH
    lr, li = lam_re.astype(F32), lam_im.astype(F32)
    dt = jnp.exp(log_dt.astype(F32))[..., None]
    zr, zi = lr * dt, li * dt
    j = jnp.arange(L + 1, dtype=F32)[:, None, None, None]
    mag = jnp.exp(zr[None] * j)
    pw_re, pw_im = mag * jnp.cos(zi[None] * j), mag * jnp.sin(zi[None] * j)
    nr, ni = pw_re[1] - 1.0, pw_im[1]
    den = lr * lr + li * li
    cr, ci = (nr * lr + ni * li) / den, (ni * lr - nr * li) / den
    bre, bim = b_re.astype(F32), b_im.astype(F32)
    bb_re = cr[..., None] * bre - ci[..., None] * bim
    bb_im = cr[..., None] * bim + ci[..., None] * bre
    x_re = pw_re[..., None] * bb_re[None] - pw_im[..., None] * bb_im[None]
    x_im = pw_re[..., None] * bb_im[None] + pw_im[..., None] * bb_re[None]
    cre, cim = c_re.astype(F32), c_im.astype(F32)
    kern = (jnp.einsum('dgcp,jdgpe->dgjce', cre, x_re[:L], precision=hp)
            - jnp.einsum('dgcp,jdgpe->dgjce', cim, x_im[:L], precision=hp))
    def spread_mask(a, b):
        spread = jnp.asarray(np.tile(np.eye(b, dtype=np.float32), (1, G)))
        mask = jnp.asarray(np.kron(np.eye(G, dtype=np.float32), np.ones((a, b), np.float32)))
        return spread, mask

    kt = kern.transpose(0, 2, 1, 4, 3)
    xt_re, xt_im = x_re.transpose(1, 0, 2, 4, 3), x_im.transpose(1, 0, 2, 4, 3)
    pwt_re, pwt_im = pw_re.transpose(1, 0, 2, 3)[:, :, :, :, None], pw_im.transpose(1, 0, 2, 3)[:, :, :, :, None]
    cret, cimt = cre.transpose(0, 1, 3, 2)[:, None], cim.transpose(0, 1, 3, 2)[:, None]
    ca_re, ca_im = cret * pwt_re - cimt * pwt_im, -(cret * pwt_im + cimt * pwt_re)
    s_idx, t_idx = np.arange(L)[:, None], np.arange(L)[None, :]
    k_st, xb_re, xb_im, cq_re, cq_im = [], [], [], [], []
    for d in range(2):
        lag = (t_idx - s_idx) if d == 0 else (s_idx - t_idx)
        k_st.append(jnp.where(jnp.asarray(lag >= 0)[:, :, None, None, None], kt[d][np.clip(lag, 0, L - 1)], 0.0))
        pw = np.arange(L - 1, -1, -1) if d == 0 else np.arange(L)
        xb_re.append(xt_re[d][pw])
        xb_im.append(xt_im[d][pw])
        q = np.arange(1, L + 1) if d == 0 else np.arange(L, 0, -1)
        cq_re.append(ca_re[d][q])
        cq_im.append(ca_im[d][q])
    sp, mk = spread_mask(CH, CH)
    m = jnp.einsum('dstrb,bc->dsrtc', jnp.stack(k_st).reshape(2, L, L, G * CH, CH), sp, precision=hp) * mk[:, None, :]
    m = m.astype(BF16).reshape(2, L * G * CH, L * G * CH)
    sp, mk = spread_mask(CH, P)
    to_b = lambda x: (jnp.einsum('dsrb,bc->dsrc', jnp.stack(x).reshape(2, L, G * CH, P), sp, precision=hp) * mk
                      ).astype(BF16).reshape(2, L * G * CH, G * P)
    sp_c, mk_c = spread_mask(P, CH)
    to_c = lambda x: (jnp.einsum('dtrb,bc->drtc', jnp.stack(x).reshape(2, L, G * P, CH), sp_c, precision=hp)
                      * mk_c[:, None, :]).astype(BF16).reshape(2, G * P, L * G * CH)
    a_re, a_im = pw_re[L].reshape(2, 1, G * P), pw_im[L].reshape(2, 1, G * P)
    return m, to_b(xb_re), to_b(xb_im), to_c(cq_re), to_c(cq_im), a_re, a_im


def _s5_scan(lay, ua, ub, mats):
    m, b_r, b_i, c_r, c_i, a_re, a_im = mats
    n_chunks = lay.n_tot // S5_CHUNK
    rows = lay.b * n_chunks
    tr = min(lay.t, rows)
    tok = tr * S5_CHUNK
    half = MIX_W // 2
    wspec = lambda a: pl.BlockSpec((1,) + a.shape[1:], lambda d, i: (d, 0, 0))
    state = jax.ShapeDtypeStruct((2, rows, S5_STATE_W), F32)
    sblk = pl.BlockSpec((1, tr, S5_STATE_W), lambda d, i: (d, i, 0))
    ublk = pl.BlockSpec((tok, half), lambda d, i: (i, 0))
    s_re, s_im = pl.pallas_call(
        _s5_proj_kernel,
        grid=(2, rows // tr),
        in_specs=[ublk, ublk, wspec(b_r), wspec(b_i)],
        out_specs=[sblk, sblk],
        out_shape=[state, state],
        compiler_params=_cparams("parallel", "parallel"),
        name="s5_proj",
    )(ua, ub, b_r, b_i)
    col = pl.BlockSpec((1, rows, 128), lambda d, j: (d, 0, j))
    acol = pl.BlockSpec((1, 1, 128), lambda d, j: (d, 0, j))
    h_re, h_im = pl.pallas_call(
        functools.partial(_s5_rec_kernel, n_batch=lay.b, n_chunks=n_chunks, ctx_chunks=lay.n_ctx // S5_CHUNK),
        grid=(2, S5_STATE_W // 128),
        in_specs=[col, col, acol, acol],
        out_specs=[col, col],
        out_shape=[state, state],
        compiler_params=_cparams("parallel", "parallel"),
        name="s5_rec",
    )(s_re, s_im, a_re, a_im)
    yblk = pl.BlockSpec((1, tok, half), lambda d, i: (d, i, 0))
    yshape = jax.ShapeDtypeStruct((2, lay.b * lay.n_tot, half), F32)
    return pl.pallas_call(
        _s5_out_kernel,
        grid=(2, rows // tr),
        in_specs=[ublk, ublk, sblk, sblk, wspec(m), wspec(c_r), wspec(c_i)],
        out_specs=[yblk, yblk],
        out_shape=[yshape, yshape],
        compiler_params=_cparams("parallel", "parallel"),
        name="s5_out",
    )(ua, ub, h_re, h_im, m, c_r, c_i)


def _s5_glu_kernel(ua_ref, ub_ref, ya_ref, yb_ref, d_ref, w_ref, b_ref, o_ref):
    u = jnp.concatenate([ua_ref[...], ub_ref[...]], axis=1)
    y = d_ref[...] * u + jnp.concatenate([ya_ref[0] + ya_ref[1], yb_ref[0] + yb_ref[1]], axis=1)
    z = 0.5 * y * (1.0 + jnp.tanh(math.sqrt(2.0 / math.pi) * (y + 0.044715 * (y * y * y))))
    gate = _sigmoid(jnp.dot(z.astype(BF16), w_ref[...], preferred_element_type=F32) + b_ref[...])
    o_ref[...] = (z * gate).astype(BF16)


def _s5_glu_call(ua, ub, ya, yb, d, w, bias, tm):
    t, half = ua.shape
    urow = pl.BlockSpec((tm, half), lambda i: (i, 0))
    yrow = pl.BlockSpec((2, tm, half), lambda i: (0, i, 0))
    return pl.pallas_call(
        _s5_glu_kernel,
        grid=(t // tm,),
        in_specs=[urow, urow, yrow, yrow, _full(d), _full(w), _full(bias)],
        out_specs=pl.BlockSpec((tm, MIX_W), lambda i: (i, 0)),
        out_shape=jax.ShapeDtypeStruct((t, MIX_W), BF16),
        compiler_params=_cparams("parallel"),
        name="s5_glu",
    )(ua, ub, ya, yb, d, w, bias)


def _rw_pre_kernel(x_ref, prev_ref, next_ref, mu_ref, g64_ref, kk_g_ref, ka_ref, rk_ref,
                   w0_ref, w1_ref, w2_ref, a0_ref, a1_ref, a2_ref, g1_ref, g2_ref,
                   r_ref, v_ref, kk_ref, lw_ref, kka_ref, km_ref, bon_ref, gate_ref,
                   *, seq_tiles, ctx_tiles):
    x = x_ref[...]
    n = x.shape[0]
    j = pl.program_id(0) % seq_tiles
    starts = jnp.logical_or(j == 0, j == ctx_tiles)
    ends = jnp.logical_or(j == ctx_tiles - 1, j == seq_tiles - 1)
    prev_row = jnp.where(starts, 0.0, prev_ref[0, 7:8, :])
    next_row = jnp.where(ends, 0.0, next_ref[0, 0:1, :])
    row = lax.broadcasted_iota(jnp.int32, x.shape, 0)
    left = jnp.where(row == 0, prev_row, pltpu.roll(x, 1, axis=0))
    right = jnp.where(row == n - 1, next_row, pltpu.roll(x, n - 1, axis=0))
    x = x + (0.5 * (left + right) - x) * mu_ref[...]
    r, k, v, xd = (x[:, i * MIX_W:(i + 1) * MIX_W] for i in range(4))
    g64 = g64_ref[...]
    kscaled = k * kk_g_ref[...]
    kk = kscaled / jnp.maximum(jnp.sqrt(_split_dot(kscaled * kscaled, g64)), 1e-12)
    xdb = xd.astype(BF16)
    r_ref[...] = r
    v_ref[...] = v
    kk_ref[...] = kk
    km_sum = None
    for d in range(2):
        lo = jnp.tanh(jnp.dot(xdb, w1_ref[d], preferred_element_type=F32))
        w_raw = w0_ref[d] + jnp.dot(lo.astype(BF16), w2_ref[d], preferred_element_type=F32)
        lw_ref[d] = -_sigmoid(w_raw) * math.exp(-0.5)
        ar = jnp.dot(xdb, a1_ref[d], preferred_element_type=F32)
        a = _sigmoid(a0_ref[d] + jnp.dot(ar.astype(BF16), a2_ref[d], preferred_element_type=F32))
        km = k * (1.0 + (a - 1.0) * ka_ref[...])
        kka_ref[d] = kk * a
        km_ref[d] = km
        km_sum = km if km_sum is None else km_sum + km
    bon_ref[...] = _split_dot(r * km_sum * rk_ref[...], g64) * v
    gr = _sigmoid(jnp.dot(xdb, g1_ref[...], preferred_element_type=F32))
    gate_ref[...] = jnp.dot(gr.astype(BF16), g2_ref[...], preferred_element_type=F32)


def _rw_pre_call(lay, rw, consts):
    t, tr = rw.shape[0], lay.t
    nt = t // tr
    g8 = tr // 8
    rw8 = rw.reshape(t // 8, 8, _RW_W)
    row = pl.BlockSpec((tr, MIX_W), lambda i: (i, 0))
    row2 = pl.BlockSpec((2, tr, MIX_W), lambda i: (0, i, 0))
    sd = jax.ShapeDtypeStruct((t, MIX_W), F32)
    sd2 = jax.ShapeDtypeStruct((2, t, MIX_W), F32)
    return pl.pallas_call(
        functools.partial(_rw_pre_kernel, seq_tiles=lay.seq_tiles, ctx_tiles=lay.ctx_tiles),
        grid=(nt,),
        in_specs=[pl.BlockSpec((tr, _RW_W), lambda i: (i, 0)),
                  pl.BlockSpec((1, 8, _RW_W), lambda i: (jnp.maximum(i * g8 - 1, 0), 0, 0)),
                  pl.BlockSpec((1, 8, _RW_W), lambda i: (jnp.minimum((i + 1) * g8, t // 8 - 1), 0, 0))]
                 + [_full(a) for a in consts],
        out_specs=[row, row, row, row2, row2, row2, row, row],
        out_shape=[sd, sd, sd, sd2, sd2, sd2, sd, sd],
        compiler_params=_cparams("parallel"),
        name="rwkv_pre",
    )(rw, rw8, rw8, *consts)


def _head_masks(shape, lane_axis, seg):
    lane = lax.broadcasted_iota(jnp.int32, shape, lane_axis)
    return [jnp.logical_and(lane >= h * seg, lane < (h + 1) * seg) for h in range(RW_HEADS)]


def _rw_prep_kernel(*refs, rev):
    tiles = [_rw_prep_tile(sub, *refs, rev=rev) for sub in range(RW_PREP_TILES)]
    while tiles:
        tiles = [t for t in tiles if next(t, None) is not None]


def _rw_prep_tile(sub, r_ref, kk_ref, v_ref, lw_ref, ka_ref, km_ref, perm_ref, permt_ref, g_ref, eye_ref,
                  br_ref, ck_ref, uvt_ref, y0_ref, pc_ref, *, rev):
    C, NC = RW_CHUNK, RW_TILE // RW_CHUNK
    perm, permt, g64, eye4 = perm_ref[...], permt_ref[...], g_ref[...], eye_ref[...]
    tok = slice(sub * RW_TILE, (sub + 1) * RW_TILE)
    nat = jnp.concatenate([r_ref[0, tok, :], kk_ref[0, tok, :], v_ref[0, tok, :],
                           lw_ref[0, 0, tok, :], ka_ref[0, 0, tok, :], km_ref[0, 0, tok, :]], axis=1)
    hi = nat.astype(BF16)
    lo = (nat - hi.astype(F32)).astype(BF16)
    pm = jnp.dot(perm, hi, preferred_element_type=F32) + jnp.dot(perm, lo, preferred_element_type=F32)
    r, kk, v, lw, ka, km = (pm[:, i * MIX_W:(i + 1) * MIX_W] for i in range(6))
    slab = lambda x, j: x[j * NC:(j + 1) * NC, :]
    order = list(range(C))[::-1] if rev else list(range(C))
    pos = {j: i for i, j in enumerate(order)}
    cum, run = {}, None
    for j in order:
        run = slab(lw, j) if run is None else run + slab(lw, j)
        cum[j] = run
    tot = run
    yield True
    bh, ch, kh, rh, cp, kp, vv = {}, {}, {}, {}, {}, {}, {}
    for j in range(C):
        e_inv, e_end = jnp.exp(-cum[j]), jnp.exp(tot - cum[j])
        bh[j] = -slab(kk, j) * jnp.exp(cum[j] - slab(lw, j))
        ch[j], kh[j] = slab(ka, j) * e_inv, slab(km, j) * e_inv
        rh[j] = slab(r, j) * jnp.exp(cum[j])
        cp[j], kp[j] = slab(ka, j) * e_end, slab(km, j) * e_end
        vv[j] = slab(v, j)
    strict = [(t, s) for t in order for s in order if pos[s] < pos[t]]
    incl = [(t, s) for t in order for s in order if pos[s] <= pos[t]]
    def head_dots(lhs, rhs, pairs):
        prods = jnp.concatenate([lhs[t] * rhs[s] for t, s in pairs], axis=0).astype(BF16)
        gram = jnp.dot(prods, g64, preferred_element_type=F32)
        return {p: gram[i * NC:(i + 1) * NC, :] for i, p in enumerate(pairs)}

    yield True
    acb = head_dots(bh, ch, strict)
    yield True
    akb = head_dots(bh, kh, strict)
    yield True
    mcr = head_dots(rh, ch, incl)
    yield True
    mkr = head_dots(rh, kh, incl)
    yield True
    bt, u0 = {}, {}
    for t in order:
        b_acc, u_acc = bh[t], jnp.zeros_like(bh[t])
        for s in order:
            if pos[s] < pos[t]:
                b_acc = b_acc + acb[(t, s)] * bt[s]
                u_acc = u_acc + akb[(t, s)] * vv[s] + acb[(t, s)] * u0[s]
        bt[t], u0[t] = b_acc, u_acc
        yield True
    rt, y0 = {}, {}
    for t in order:
        r_acc, y_acc = rh[t], jnp.zeros_like(rh[t])
        for s in order:
            if pos[s] <= pos[t]:
                r_acc = r_acc + mcr[(t, s)] * bt[s]
                y_acc = y_acc + mcr[(t, s)] * u0[s] + mkr[(t, s)] * vv[s]
        rt[t], y0[t] = r_acc, y_acc
        yield True
    stackp = lambda dct: jnp.concatenate([dct[j] for j in range(C)], axis=0)
    b16 = lambda x: x.astype(BF16)
    y0p, u0p = stackp(y0), stackp(u0)
    y0h, u0h = b16(y0p), b16(u0p)
    cat = jnp.concatenate([b16(stackp(bt)), b16(stackp(rt)), b16(stackp(cp)), b16(stackp(kp)),
                           y0h, b16(y0p - y0h.astype(F32)), u0h, b16(u0p - u0h.astype(F32)), b16(stackp(vv))], axis=1)
    natural = jnp.dot(permt, cat, preferred_element_type=F32)
    seg = lambda i: natural[:, i * MIX_W:(i + 1) * MIX_W]
    yield True
    btn, rtn, cpn, kpn = b16(seg(0)), b16(seg(1)), b16(seg(2)), b16(seg(3))
    y0_ref[0, tok, :] = seg(4) + seg(5)
    u0h_n, u0l_n, vn = b16(seg(6)), b16(seg(7)), b16(seg(8))
    hm = _head_masks((C, MIX_W), 1, RW_DIM)
    zero = jnp.zeros((C, MIX_W), BF16)
    zh, zl = [], []
    for c in range(NC):
        rows = slice(c * C, (c + 1) * C)
        br_ref[0, sub * NC + c, 0:C, :] = btn[rows]
        br_ref[0, sub * NC + c, C:2 * C, :] = rtn[rows]
        ck_ref[0, sub * NC + c, 0:C, :] = cpn[rows]
        ck_ref[0, sub * NC + c, C:2 * C, :] = kpn[rows]
        for h in range(RW_HEADS):
            zh += [jnp.where(hm[h], u0h_n[rows], zero), jnp.where(hm[h], vn[rows], zero)]
            zl += [jnp.where(hm[h], u0l_n[rows], zero), zero]
    uvt = (lax.dot_general(eye4, jnp.concatenate(zh, axis=0), _NT, preferred_element_type=F32)
           + lax.dot_general(eye4, jnp.concatenate(zl, axis=0), _NT, preferred_element_type=F32))
    for c in range(NC):
        uvt_ref[0, sub * NC + c] = uvt[:, c * 2 * C * RW_HEADS:(c + 1) * 2 * C * RW_HEADS]
    pc_ref[0, sub * NC:(sub + 1) * NC, :] = jnp.exp(tot)


def _rw_prep_call(lay, shared, perdir, consts, rev):
    b, n_tot, tt = lay.b, lay.n_tot, RW_TILE * RW_PREP_TILES
    assert n_tot % tt == 0
    nck = n_tot // RW_CHUNK
    cpt = tt // RW_CHUNK
    d = 1 if rev else 0
    sh = [a.reshape(b, n_tot, MIX_W) for a in shared]
    pd = [a.reshape(2, b, n_tot, MIX_W) for a in perdir]
    tok = pl.BlockSpec((1, tt, MIX_W), lambda i, j: (i, j, 0))
    tok_d = pl.BlockSpec((1, 1, tt, MIX_W), lambda i, j: (d, i, j, 0))
    rows32 = pl.BlockSpec((1, cpt, 2 * RW_CHUNK, MIX_W), lambda i, j: (i, j, 0, 0))
    return pl.pallas_call(
        functools.partial(_rw_prep_kernel, rev=rev),
        grid=(b, n_tot // tt),
        in_specs=[tok] * 3 + [tok_d] * 3 + [_full(a) for a in consts],
        out_specs=[rows32, rows32,
                   pl.BlockSpec((1, cpt, RW_DIM, 2 * RW_CHUNK * RW_HEADS), lambda i, j: (i, j, 0, 0)),
                   tok,
                   pl.BlockSpec((1, cpt, MIX_W), lambda i, j: (i, j, 0))],
        out_shape=[jax.ShapeDtypeStruct((b, nck, 2 * RW_CHUNK, MIX_W), BF16),
                   jax.ShapeDtypeStruct((b, nck, 2 * RW_CHUNK, MIX_W), BF16),
                   jax.ShapeDtypeStruct((b, nck, RW_DIM, 2 * RW_CHUNK * RW_HEADS), F32),
                   jax.ShapeDtypeStruct((b, n_tot, MIX_W), F32),
                   jax.ShapeDtypeStruct((b, nck, MIX_W), F32)],
        compiler_params=_cparams("parallel", "parallel"),
        name="rwkv_prep_rev" if rev else "rwkv_prep_fwd",
    )(*sh, *pd, *consts)


def _rw_scan_kernel(brf, ckf, uvtf, pcf, brr, ckr, uvtr, pcr, ytf_ref, ytr_ref, s_scr, *, n_batch):
    @pl.when(pl.program_id(0) == 0)
    def _():
        s_scr[...] = jnp.zeros_like(s_scr)

    cpt = RW_TILE // RW_CHUNK
    hm = _head_masks((2 * RW_CHUNK, MIX_W), 1, RW_DIM)
    lane = lax.broadcasted_iota(jnp.int32, (RW_DIM, 2 * RW_CHUNK * RW_HEADS), 1)
    is_u = (lane & (2 * RW_CHUNK - 1)) < RW_CHUNK
    per_head = lambda x: jnp.concatenate([jnp.where(m, x, jnp.zeros_like(x)) for m in hm], axis=0)

    def refs_of(p, c):
        d, b = divmod(p, n_batch)
        refs = (brf, ckf, uvtf, pcf, ytf_ref) if d == 0 else (brr, ckr, uvtr, pcr, ytr_ref)
        return refs, b, (c if d == 0 else cpt - 1 - c)

    def step(c, carry):
        lhs = []
        for p in range(2 * n_batch):
            (br_ref, _, uvt_ref, _, yt_ref), b, cc = refs_of(p, c)
            s = s_scr[p]
            shi = s.astype(BF16)
            slo = (s - shi.astype(F32)).astype(BF16)
            w2 = lax.dot_general(jnp.concatenate([shi, slo], axis=0), per_head(br_ref[b, cc]), _NT, preferred_element_type=F32)
            w = w2[:RW_DIM] + w2[RW_DIM:]
            yt_ref[b, cc] = w
            uvt = uvt_ref[b, cc]
            lhs.append(jnp.where(is_u, w + uvt, uvt).astype(BF16))
        for p in range(2 * n_batch):
            (_, ck_ref, _, pc_ref, _), b, cc = refs_of(p, c)
            s_scr[p] = (s_scr[p] * pc_ref[b, pl.ds(cc, 1), :]
                        + jnp.dot(lhs[p], per_head(ck_ref[b, cc]), preferred_element_type=F32))
        return carry

    lax.fori_loop(0, cpt, step, 0)


def _rw_scan_call(lay, fwd, rev):
    b, n_tot, tt = lay.b, lay.n_tot, RW_TILE
    assert lay.n_ctx % tt == 0 and lay.n_lat % tt == 0
    nt, ct = n_tot // tt, lay.n_ctx // tt
    cpt = tt // RW_CHUNK
    rev_tile = lambda i: jnp.where(i < ct, ct - 1 - i, nt - 1 + ct - i)

    def specs(tile):
        return [pl.BlockSpec((b, cpt, 2 * RW_CHUNK, MIX_W), lambda i: (0, tile(i), 0, 0)),
                pl.BlockSpec((b, cpt, 2 * RW_CHUNK, MIX_W), lambda i: (0, tile(i), 0, 0)),
                pl.BlockSpec((b, cpt, RW_DIM, 2 * RW_CHUNK * RW_HEADS), lambda i: (0, tile(i), 0, 0)),
                pl.BlockSpec((b, cpt, MIX_W), lambda i: (0, tile(i), 0))]

    ident = lambda i: i
    yt = jax.ShapeDtypeStruct((b, n_tot // RW_CHUNK, RW_DIM, 2 * RW_CHUNK * RW_HEADS), F32)
    return pl.pallas_call(
        functools.partial(_rw_scan_kernel, n_batch=b),
        grid=(nt,),
        in_specs=specs(ident) + specs(rev_tile),
        out_specs=[specs(ident)[2], specs(rev_tile)[2]],
        out_shape=[yt, yt],
        scratch_shapes=[pltpu.VMEM((2 * b, RW_DIM, MIX_W), F32)],
        compiler_params=_cparams("arbitrary"),
        name="rwkv_scan",
    )(*fwd, *rev)


def _rw_fin_kernel(ytf_ref, ytr_ref, y0f_ref, y0r_ref, bon_ref, gate_ref, asel_ref, g64_ref, lng_ref, lnb_ref, o_ref):
    cpt = RW_TILE // RW_CHUNK
    asel = asel_ref[...]
    width = cpt * 2 * RW_CHUNK * RW_HEADS
    lane = lax.broadcasted_iota(jnp.int32, (RW_DIM, width), 1)
    lane_head = jnp.bitwise_and(jnp.right_shift(lane, 5), RW_HEADS - 1)

    def base(yt_ref):
        yt = jnp.concatenate([yt_ref[0, c] for c in range(cpt)], axis=1).astype(BF16)
        rows = jnp.concatenate([jnp.where(lane_head == h, yt, jnp.zeros_like(yt)) for h in range(RW_HEADS)], axis=0)
        return lax.dot_general(asel, rows, _NT, preferred_element_type=F32)

    y = base(ytf_ref) + y0f_ref[0] + base(ytr_ref) + y0r_ref[0]
    g64 = g64_ref[...]
    mean = _split_dot(y, g64) * (1.0 / RW_DIM)
    c = y - mean
    var = _split_dot(c * c, g64) * (1.0 / RW_DIM)
    out = c * lax.rsqrt(var + RW_LN_EPS) * lng_ref[...] + lnb_ref[...] + bon_ref[...]
    o_ref[...] = (out * gate_ref[...]).astype(BF16)


def _rw_fin_call(lay, ytf, ytr, y0f, y0r, bon, gate, consts):
    b, n_tot, tt = lay.b, lay.n_tot, RW_TILE
    nt = n_tot // tt
    cpt = tt // RW_CHUNK
    ytb = pl.BlockSpec((1, cpt, RW_DIM, 2 * RW_CHUNK * RW_HEADS), lambda i, j: (i, j, 0, 0))
    y0b = pl.BlockSpec((1, tt, MIX_W), lambda i, j: (i, j, 0))
    row = pl.BlockSpec((tt, MIX_W), lambda i, j: (i * nt + j, 0))
    return pl.pallas_call(
        _rw_fin_kernel,
        grid=(b, nt),
        in_specs=[ytb, ytb, y0b, y0b, row, row] + [_full(a) for a in consts],
        out_specs=row,
        out_shape=jax.ShapeDtypeStruct((b * n_tot, MIX_W), BF16),
        compiler_params=_cparams("parallel", "parallel"),
        name="rwkv_finish",
    )(ytf, ytr, y0f, y0r, bon, gate, *consts)


def _rw_constants():
    c, nc = RW_CHUNK, RW_TILE // RW_CHUNK
    perm = np.zeros((RW_TILE, RW_TILE), np.float32)
    for ci in range(nc):
        for j in range(c):
            perm[j * nc + ci, ci * c + j] = 1.0
    lane = np.arange(MIX_W) % RW_DIM
    eye4 = (lane[None, :] == np.arange(RW_DIM)[:, None]).astype(np.float32)
    lanes = np.arange(nc * 2 * c * RW_HEADS)
    lane_chunk, lane_tok = lanes // (2 * c * RW_HEADS), lanes % (2 * c)
    t = np.arange(RW_TILE)
    asel = ((lane_chunk[None, :] == (t // c)[:, None]) & (lane_tok[None, :] == (c + t % c)[:, None])).astype(np.float32)
    as16 = lambda a: jnp.asarray(a, BF16)
    return as16(perm), as16(perm.T), as16(eye4), as16(asel)


def _rwkv_branch(lay, rw, pre_consts, g64, lng, lnb):
    r_, v_, kk_, lw_, kka_, km_, bon, gate = _rw_pre_call(lay, rw, pre_consts)
    perm, permt, eye4, asel = _rw_constants()
    prep_consts = (perm, permt, g64, eye4)
    fwd = _rw_prep_call(lay, (r_, kk_, v_), (lw_, kka_, km_), prep_consts, False)
    rev = _rw_prep_call(lay, (r_, kk_, v_), (lw_, kka_, km_), prep_consts, True)
    pick = lambda o: (o[0], o[1], o[2], o[4])
    ytf, ytr = _rw_scan_call(lay, pick(fwd), pick(rev))
    return _rw_fin_call(lay, ytf, ytr, fwd[3], rev[3], bon, gate, (asel, g64, lng, lnb))


def _merge_kernel(x_ref, mod_ref, g_ref, wg_ref, ya_ref, yb_ref, yc_ref, yd_ref, wb_ref, wo_ref, o_ref):
    x = x_ref[...]
    h = _modulate(x, g_ref[...], mod_ref[0, 0:1, :], mod_ref[0, 1:2, :]).astype(BF16)
    merged = None
    for i, y_ref in enumerate((ya_ref, yb_ref, yc_ref, yd_ref)):
        gate = _sigmoid(jnp.dot(h, wg_ref[:, i * D_MODEL:(i + 1) * D_MODEL], preferred_element_type=F32))
        term = gate * jnp.dot(y_ref[...], wb_ref[i], preferred_element_type=F32)
        merged = term if merged is None else merged + term
    out = jnp.dot(merged.astype(BF16), wo_ref[...], preferred_element_type=F32)
    o_ref[...] = x + mod_ref[0, 2:3, :] * out


def _merge_call(lay, with_ctx, x_all, mod, g, w_gate, ya, yb, yc, yd, w_branch, w_out):
    tm = lay.t
    src, mrow = lay.src_tile(with_ctx), lay.mod_row(with_ctx)
    full_row = lambda w: pl.BlockSpec((tm, w), lambda i: (src(i), 0))
    out_row = lambda w: pl.BlockSpec((tm, w), lambda i: (i, 0))
    return pl.pallas_call(
        _merge_kernel,
        grid=(lay.n_tiles(with_ctx),),
        in_specs=[full_row(D_MODEL), pl.BlockSpec((1, 6, D_MODEL), lambda i: (mrow(i), 0, 0)), _full(g), _full(w_gate),
                  out_row(MIX_W), full_row(MIX_W), out_row(MIX_W), full_row(MIX_W), _full(w_branch), _full(w_out)],
        out_specs=out_row(D_MODEL),
        out_shape=jax.ShapeDtypeStruct((lay.rows(with_ctx), D_MODEL), F32),
        compiler_params=_cparams("parallel"),
        name="merge_out",
    )(x_all, mod, g, w_gate, ya, yb, yc, yd, w_branch, w_out)


def _router_kernel(x_ref, mod_ref, g_ref, wh_ref, wl_ref, bias_ref, f_ref, comb_ref, gid_ref):
    f = _modulate(x_ref[...], g_ref[...], mod_ref[0, 3:4, :], mod_ref[0, 4:5, :])
    fh = f.astype(BF16)
    f_ref[...] = fh
    fl = (f - fh.astype(F32)).astype(BF16)
    nt = (((1,), (1,)), ((), ()))
    wh, wl = wh_ref[...], wl_ref[...]
    logits = (lax.dot_general(wh, fh, nt, preferred_element_type=F32)
              + lax.dot_general(wh, fl, nt, preferred_element_type=F32)
              + lax.dot_general(wl, fh, nt, preferred_element_type=F32))
    scores = _sigmoid(logits)
    biased = scores + bias_ref[...]
    sc = [scores[e:e + 1, :] for e in range(N_EXPERTS)]
    bi = [biased[e:e + 1, :] for e in range(N_EXPERTS)]
    group_score = []
    for g in range(N_GROUPS):
        a, b, c, d = bi[4 * g:4 * g + 4]
        m1, n1, m2, n2 = jnp.maximum(a, b), jnp.minimum(a, b), jnp.maximum(c, d), jnp.minimum(c, d)
        group_score.append(jnp.maximum(m1, m2) + jnp.maximum(jnp.minimum(m1, m2), jnp.maximum(n1, n2)))

    def first_argmax(vals):
        top = functools.reduce(jnp.maximum, vals)
        seen, hot = None, []
        for v in vals:
            h = v == top
            if seen is not None:
                h = jnp.logical_and(h, jnp.logical_not(seen))
            seen = h if seen is None else jnp.logical_or(seen, h)
            hot.append(h)
        return hot

    in_group = first_argmax(group_score)
    masked = [jnp.where(in_group[e // EXPERTS_PER_GROUP], bi[e], -jnp.inf) for e in range(N_EXPERTS)]
    hot1 = first_argmax(masked)
    hot2 = first_argmax([jnp.where(h, -jnp.inf, v) for h, v in zip(hot1, masked)])
    w1 = functools.reduce(jnp.add, [jnp.where(h, s, 0.0) for h, s in zip(hot1, sc)])
    w2 = functools.reduce(jnp.add, [jnp.where(h, s, 0.0) for h, s in zip(hot2, sc)])
    inv_tot = 1.0 / (w1 + w2)
    for e in range(N_EXPERTS):
        comb_ref[e:e + 1, :] = (jnp.where(hot1[e], w1, 0.0) + jnp.where(hot2[e], w2, 0.0)) * inv_tot
    gid_ref[...] = functools.reduce(jnp.add, [jnp.where(in_group[g], g, 0) for g in range(1, N_GROUPS)])


def _router_call(lay, with_ctx, x, mod, g, wh, wl, bias):
    t, tm = x.shape[0], lay.t
    mrow = lay.mod_row(with_ctx)
    return pl.pallas_call(
        _router_kernel,
        grid=(t // tm,),
        in_specs=[pl.BlockSpec((tm, D_MODEL), lambda i: (i, 0)),
                  pl.BlockSpec((1, 6, D_MODEL), lambda i: (mrow(i), 0, 0)), _full(g), _full(wh), _full(wl), _full(bias)],
        out_specs=[pl.BlockSpec((tm, D_MODEL), lambda i: (i, 0)), pl.BlockSpec((N_EXPERTS, tm), lambda i: (0, i)),
                   pl.BlockSpec((1, tm), lambda i: (0, i))],
        out_shape=[jax.ShapeDtypeStruct((t, D_MODEL), BF16), jax.ShapeDtypeStruct((N_EXPERTS, t), F32),
                   jax.ShapeDtypeStruct((1, t), jnp.int32)],
        compiler_params=_cparams("parallel"),
        name="moe_router",
    )(x, mod, g, wh, wl, bias)


def _moe_plan(gid, n_tiles, tm):
    g = gid.reshape(n_tiles, tm)
    onehot = (g[..., None] == jnp.arange(N_GROUPS, dtype=jnp.int32)).astype(jnp.int32)
    rank = jnp.cumsum(onehot, axis=1) - onehot
    counts = jnp.sum(onehot, axis=1)
    padded = (counts + 15) // 16 * 16
    offs = jnp.cumsum(padded, axis=1) - padded
    pos = jnp.sum(onehot * (offs[:, None, :] + rank), axis=-1)
    n_over = (jnp.maximum(padded - MOE_BLOCK, 0) + MOE_OVER - 1) // MOE_OVER
    return pos.astype(jnp.int32), offs.astype(jnp.int32), n_over.astype(jnp.int32)


def _moe_kernel(offs_ref, nover_ref, f_ref, posr_ref, posc_ref, comb_ref, wg_ref, wu_ref, wd_ref, x_ref, modb_ref,
                modc_ref, o_ref, xs_scr, cs_scr, ys_scr, *, ctx_rows, tiles_per_seq):
    i, e = pl.program_id(0), pl.program_id(1)
    n_slots, tm = xs_scr.shape[0], f_ref.shape[0]
    n_live = min(n_slots, -(-(tm + 16 * N_GROUPS) // 256) * 256)

    @pl.when(e == 0)
    def _():
        slot = lax.broadcasted_iota(jnp.int32, (n_live, tm), 0)
        place = (slot == posr_ref[0]).astype(BF16)
        xs_scr[0:n_live, :] = jnp.dot(place, f_ref[...], preferred_element_type=F32).astype(BF16)
        xs_scr[n_live:n_slots, :] = jnp.zeros((n_slots - n_live, D_MODEL), BF16)
        cs_scr[0:n_live, :] = _split_dot_rhs(place, comb_ref[...])
        cs_scr[n_live:n_slots, :] = jnp.zeros((n_slots - n_live, N_EXPERTS), F32)
        ys_scr[...] = jnp.zeros_like(ys_scr)

    grp = lax.shift_right_logical(e, 2)
    start = offs_ref[i, grp]
    lane = lax.broadcasted_iota(jnp.int32, (1, N_EXPERTS), 1)

    def run(rows):
        xb = xs_scr[rows, :]
        gate = jnp.dot(xb, wg_ref[0], preferred_element_type=F32)
        up = jnp.dot(xb, wu_ref[0], preferred_element_type=F32)
        act = (gate * _sigmoid(gate) * up).astype(BF16)
        down = jnp.dot(act, wd_ref[0], preferred_element_type=F32)
        c_e = jnp.sum(jnp.where(lane == e, cs_scr[rows, :], 0.0), axis=1, keepdims=True)
        ys_scr[rows, :] += c_e * down

    run(pl.ds(pl.multiple_of(start, 16), MOE_BLOCK))

    def overflow(k, carry):
        run(pl.ds(pl.multiple_of(start + MOE_BLOCK + k * MOE_OVER, 16), MOE_OVER))
        return carry

    lax.fori_loop(0, nover_ref[i, grp], overflow, 0)

    @pl.when(e == N_EXPERTS - 1)
    def _():
        slot = lax.broadcasted_iota(jnp.int32, (tm, n_live), 1)
        fetch = (slot == posc_ref[...]).astype(BF16)
        y = jnp.dot(fetch, ys_scr[0:n_live, :].astype(BF16), preferred_element_type=F32)
        res_gate = modb_ref[0, 5:6, :]
        if ctx_rows:
            row = lax.broadcasted_iota(jnp.int32, y.shape, 0)
            first = i % tiles_per_seq == 0
            res_gate = jnp.where(jnp.logical_and(first, row < ctx_rows), modc_ref[0, 5:6, :], res_gate)
        o_ref[...] = x_ref[...] + res_gate * y


def _moe_call(lay, with_ctx, f, comb, gid, wg, wu, wd, x, mod):
    t = f.shape[0]
    seq = lay.n_tot if with_ctx else lay.n_lat
    tm = MOE_TILE if seq % MOE_TILE == 0 else math.gcd(seq, 1024)
    tps, n_tiles = seq // tm, t // tm
    ctx_rows = lay.n_ctx if with_ctx else 0
    assert ctx_rows <= tm
    n_slots = -(-(tm + 16 * N_GROUPS + MOE_BLOCK + MOE_OVER) // 256) * 256
    pos, offs, n_over = _moe_plan(gid, n_tiles, tm)
    wspec = lambda a: pl.BlockSpec((1,) + a.shape[1:], lambda i, e, *_: (e, 0, 0))
    tok = lambda w: pl.BlockSpec((tm, w), lambda i, e, *_: (i, 0))
    grid_spec = pltpu.PrefetchScalarGridSpec(
        num_scalar_prefetch=2,
        grid=(n_tiles, N_EXPERTS),
        in_specs=[tok(D_MODEL), pl.BlockSpec((1, 1, tm), lambda i, e, *_: (i, 0, 0)), tok(1), tok(N_EXPERTS),
                  wspec(wg), wspec(wu), wspec(wd), tok(D_MODEL),
                  pl.BlockSpec((1, 6, D_MODEL), lambda i, e, *_: (i // tps, 0, 0)),
                  pl.BlockSpec((1, 6, D_MODEL), lambda i, e, *_: (lay.b, 0, 0))],
        out_specs=tok(D_MODEL),
        scratch_shapes=[pltpu.VMEM((n_slots, D_MODEL), BF16), pltpu.VMEM((n_slots, N_EXPERTS), F32),
                        pltpu.VMEM((n_slots, D_MODEL), F32)])
    return pl.pallas_call(
        functools.partial(_moe_kernel, ctx_rows=ctx_rows, tiles_per_seq=tps),
        grid_spec=grid_spec,
        out_shape=jax.ShapeDtypeStruct((t, D_MODEL), F32),
        compiler_params=_cparams("parallel", "arbitrary"),
        name="moe_experts",
    )(offs, n_over, f, pos.reshape(n_tiles, 1, tm), pos.reshape(t, 1), comb, wg, wu, wd, x, mod, mod)


def _block_ones(n, group):
    i = np.arange(n) // group
    return jnp.asarray(i[:, None] == i[None, :], dtype=BF16)


def _rope_tables(n_ctx, n_lat):
    rows = n_lat // GRID_W
    row = jnp.repeat(jnp.arange(rows, dtype=F32), GRID_W)
    col = jnp.tile(jnp.arange(GRID_W, dtype=F32), rows)

    def angles(rot_dim):
        n_freq = rot_dim // 4
        inv_freq = ROPE_BASE ** (-jnp.arange(n_freq, dtype=F32) / n_freq)
        ang = jnp.concatenate([row[:, None] * inv_freq, col[:, None] * inv_freq], axis=-1)
        return jnp.cos(ang), jnp.sin(ang)

    c, s = angles(DA_DIM)
    cda = jnp.tile(jnp.concatenate([c, c], -1), (1, 2 * DA_HEADS))
    sda = jnp.tile(jnp.concatenate([-s, s], -1), (1, 2 * DA_HEADS))
    c, s = angles(MLA_ROPE)
    one = jnp.ones((n_lat, MLA_NOPE), F32)
    pad = MLA_HEAD_PAD - MLA_NOPE - MLA_ROPE
    cml = jnp.tile(jnp.concatenate([one, c, c, jnp.ones((n_lat, pad), F32)], -1), (1, MLA_HEADS))
    sml = jnp.tile(jnp.concatenate([0 * one, -s, s, jnp.zeros((n_lat, pad), F32)], -1), (1, MLA_HEADS))
    ident = lambda t, v: jnp.concatenate([jnp.full((n_ctx, MIX_W), v, F32), t], axis=0)
    return ident(cda, 1.0), ident(sda, 0.0), ident(cml, 1.0), ident(sml, 0.0)


def _pad_heads(w, n_heads, src_w, lo, hi, dst_w=MLA_HEAD_PAD):
    w = w.reshape(w.shape[0], n_heads, src_w)[:, :, lo:hi]
    w = jnp.pad(w, ((0, 0), (0, 0), (0, dst_w - (hi - lo))))
    return w.reshape(w.shape[0], n_heads * dst_w)


def _mix_weight(w_in_l):
    w = w_in_l
    kr = w[:, 1344:1360]
    z = lambda n: jnp.zeros((D_MODEL, n), w.dtype)
    kr_wide = jnp.concatenate([jnp.concatenate([z(MLA_NOPE), kr, z(MLA_HEAD_PAD - MLA_NOPE - MLA_ROPE)], 1)] * MLA_HEADS, 1)
    return jnp.concatenate([w[:, 0:1024], w[:, 1024:1216], z(64), w[:, 1216:1344], kr_wide, w[:, 1360:2384]], axis=1).astype(BF16)


def kernel(x, c, ctx, c_ctx, w_ada, b_ada, norm_mix_g, norm_ffn_g, w_in, da_qk_norm_g, da_lambda, da_subln_g, s5_lam_re, s5_lam_im, s5_log_dt, s5_b_re, s5_b_im, s5_c_re, s5_c_im, s5_d, s5_w_glu, s5_b_glu, mla_cq_norm_g, mla_ckv_norm_g, mla_w_uq, mla_w_ukv, mla_qk_norm_g, rw_mu, rw_w0, rw_w1, rw_w2, rw_a0, rw_a1, rw_a2, rw_g1, rw_g2, rw_k_k, rw_k_a, rw_r_k, rw_ln_g, rw_ln_b, w_branch, w_out, router_w, router_bias, exp_w_gate, exp_w_up, exp_w_down):
    b, n_lat, dm = x.shape
    n_ctx = ctx.shape[1]
    depth = w_ada.shape[0]
    assert dm == D_MODEL
    lay = _Layout(b, n_ctx, n_lat)
    t_all = b * lay.n_tot
    tm_big = 2 * lay.t

    g32 = _block_ones(MIX_W, DA_DIM)
    g64 = _block_ones(MIX_W, RW_DIM)
    tabs = _rope_tables(n_ctx, n_lat)
    row = lambda v: v.reshape(1, -1).astype(F32)
    bf = lambda a: a.astype(BF16)

    cc = jnp.zeros((16, dm), F32).at[:b].set(c).at[b].set(c_ctx)
    mod_all = _ada_call(cc, w_ada, b_ada)
    x_all = jnp.concatenate([ctx, x], axis=1).reshape(t_all, dm)

    wr_hi = router_w.T.astype(BF16)
    wr_lo = (router_w.T - wr_hi.astype(F32)).astype(BF16)
    r_bias = router_bias.reshape(N_EXPERTS, 1).astype(F32)

    for l in range(depth):
        need_ctx = l < depth - 1
        lambda_init = 0.8 - 0.6 * math.exp(-0.3 * l)
        mod = mod_all[l, :b + 1].reshape(b + 1, 6, dm)
        g_mix = row(norm_mix_g[l])
        da, s5a, s5b, mla, rw = _inproj_call(lay, x_all, mod, g_mix, _mix_weight(w_in[l]))

        log2e = math.log2(math.e)
        gda = jnp.stack([jnp.tile(da_qk_norm_g[l, 0], 2 * DA_HEADS) * (DA_DIM ** -0.5 * log2e), jnp.tile(da_qk_norm_g[l, 1], 2 * DA_HEADS)])
        mla_pad = MLA_HEAD_PAD - MLA_NOPE - MLA_ROPE
        gml = jnp.stack([jnp.tile(jnp.pad(mla_qk_norm_g[l, 0], (0, mla_pad)), MLA_HEADS) * ((MLA_NOPE + MLA_ROPE) ** -0.5 * log2e),
                         jnp.tile(jnp.pad(mla_qk_norm_g[l, 1], (0, mla_pad)), MLA_HEADS)])
        wuq = bf(jnp.pad(_pad_heads(mla_w_uq[l], MLA_HEADS, MLA_NOPE + MLA_ROPE, 0, MLA_NOPE + MLA_ROPE), ((0, 64), (0, 0))))
        wuk = bf(_pad_heads(mla_w_ukv[l], MLA_HEADS, MLA_NOPE + MLA_VDIM, 0, MLA_NOPE))
        wuv = bf(_pad_heads(mla_w_ukv[l], MLA_HEADS, MLA_NOPE + MLA_VDIM, MLA_NOPE, MLA_NOPE + MLA_VDIM))
        consts = (g32, g64, gda.astype(F32), gml.astype(F32), row(jnp.pad(mla_cq_norm_g[l], (0, 64))), row(mla_ckv_norm_g[l]),
                  wuq, wuk, wuv)
        qd, kdt, vd, qm, kmt, vm = _qkprep_call(lay, da, mla, tabs, consts)

        lam32 = da_lambda[l].astype(F32)
        lmbda = (jnp.exp(jnp.sum(lam32[0] * lam32[1])) - jnp.exp(jnp.sum(lam32[2] * lam32[3])) + lambda_init).reshape(1, 1)
        subln = row(jnp.tile(da_subln_g[l], DA_HEADS) * (1.0 - lambda_init))
        ya = _attention(lay, qd, kdt, vd, (lmbda, subln, g64), True, need_ctx, "diff_attn")
        yc = _attention(lay, qm, kmt, vm, (lmbda, subln, g64), False, need_ctx, "mla_attn")

        mats = _s5_mats(s5_lam_re[l], s5_lam_im[l], s5_log_dt[l], s5_b_re[l], s5_b_im[l], s5_c_re[l], s5_c_im[l])
        ys_a, ys_b = _s5_scan(lay, s5a, s5b, mats)
        yb = _s5_glu_call(s5a, s5b, ys_a, ys_b, row(s5_d[l]), bf(s5_w_glu[l]), row(s5_b_glu[l]), tm_big)

        pre_consts = (row(rw_mu[l]), g64, row(rw_k_k[l]), row(rw_k_a[l]), row(rw_r_k[l]),
                      rw_w0[l].reshape(2, 1, MIX_W), bf(rw_w1[l]), bf(rw_w2[l]),
                      rw_a0[l].reshape(2, 1, MIX_W), bf(rw_a1[l]), bf(rw_a2[l]), bf(rw_g1[l]), bf(rw_g2[l]))
        yd = _rwkv_branch(lay, rw, pre_consts, g64, row(rw_ln_g[l]), row(rw_ln_b[l]))

        x_mid = _merge_call(lay, need_ctx, x_all, mod, g_mix, bf(w_in[l][:, 2384:]), ya, yb, yc, yd,
                            bf(w_branch[l]), bf(w_out[l]))
        f, comb_t, gid = _router_call(lay, need_ctx, x_mid, mod, row(norm_ffn_g[l]), wr_hi, wr_lo, r_bias)
        x_all = _moe_call(lay, need_ctx, f, comb_t.T, gid, bf(exp_w_gate[l]), bf(exp_w_up[l]), bf(exp_w_down[l]), x_mid, mod)
    return x_all.reshape(b, n_lat, dm)
```

```python
import functools
import math

import numpy as np
import jax
import jax.numpy as jnp
from jax import lax
from jax.experimental import pallas as pl
from jax.experimental.pallas import tpu as pltpu

F32 = jnp.float32
BF16 = jnp.bfloat16

D_MODEL = 1024
GRID_W = 64
ROPE_BASE = 10000.0
EPS = 1e-6
DA_HEADS, DA_DIM, DA_VDIM = 4, 32, 64
S5_GROUPS, S5_CH, S5_STATE = 16, 16, 64
MLA_HEADS, MLA_NOPE, MLA_ROPE, MLA_VDIM = 4, 32, 16, 64
MLA_Q_RANK, MLA_KV_RANK = 192, 128
MLA_HEAD_PAD = 64
RW_HEADS, RW_DIM = 4, 64
RW_LN_EPS = 64e-5
N_BRANCH = 4
N_EXPERTS, N_GROUPS, EXPERTS_PER_GROUP = 16, 4, 4
D_FF = 512
MIX_W = 256

S5_CHUNK = 8
S5_FLAT = S5_CHUNK * MIX_W
S5_STATE_W = S5_GROUPS * S5_STATE
RW_CHUNK = 16
RW_TILE = 128
RW_PREP_TILES = 2
_NT = (((1,), (1,)), ((), ()))
TOKEN_TILE = 256
PROJ_TILE = 768
MOE_TILE = 1152
MOE_BLOCK = 384
MOE_OVER = 128

_DA_W, _S5_W, _MLA_W, _RW_W = 768, 256, 640, 1024
_MIX_COLS = _DA_W + _S5_W + _MLA_W + _RW_W

V7X_VMEM_BYTES = 64 * 2**20
_VMEM_LIMIT = V7X_VMEM_BYTES - 8 * 2**20


def _cparams(*sem):
    return pltpu.CompilerParams(dimension_semantics=sem, vmem_limit_bytes=_VMEM_LIMIT)


def _full(a):
    return pl.BlockSpec(a.shape, lambda *_, nd=a.ndim: (0,) * nd)


def _split_dot(x, w, terms=2):
    acc = None
    rem = x
    for i in range(terms):
        part = rem.astype(BF16)
        d = jnp.dot(part, w, preferred_element_type=F32)
        acc = d if acc is None else acc + d
        if i + 1 < terms:
            rem = rem - part.astype(F32)
    return acc


def _split_dot_rhs(w, x):
    hi = x.astype(BF16)
    lo = (x - hi.astype(F32)).astype(BF16)
    return jnp.dot(w, hi, preferred_element_type=F32) + jnp.dot(w, lo, preferred_element_type=F32)


def _modulate(x, g, shift, scale):
    xn = x * lax.rsqrt(jnp.mean(x * x, axis=-1, keepdims=True) + EPS)
    return xn * g * (1.0 + scale) + shift


def _sigmoid(x):
    return 1.0 / (1.0 + jnp.exp(-x))


def _group_rms(x, ones_bd, inv_n, gain):
    ms = _split_dot(x * x, ones_bd) * inv_n
    return x * lax.rsqrt(ms + EPS) * gain


def _lane_partner(x, half, period, first_end):
    n = x.shape[1]
    lane = lax.broadcasted_iota(jnp.int32, x.shape, 1)
    up = pltpu.roll(x, n - half, axis=1)
    down = pltpu.roll(x, half, axis=1)
    return jnp.where((lane & (period - 1)) < first_end, up, down)


def _rope(x, cos_t, sin_t, half, period, first_end):
    return x * cos_t + _lane_partner(x, half, period, first_end) * sin_t


def _ada_kernel(c_ref, w_ref, b_ref, o_ref):
    c = c_ref[...]
    s = c * _sigmoid(c)
    o_ref[0] = jnp.dot(s.astype(BF16), w_ref[0].astype(BF16), preferred_element_type=F32) + b_ref[0]


def _ada_call(cc, w_ada, b_ada):
    depth, dm, n = w_ada.shape
    tn = n // 4
    return pl.pallas_call(
        _ada_kernel,
        grid=(depth, n // tn),
        in_specs=[
            pl.BlockSpec(cc.shape, lambda l, j: (0, 0)),
            pl.BlockSpec((1, dm, tn), lambda l, j: (l, 0, j)),
            pl.BlockSpec((1, 1, tn), lambda l, j: (l, 0, j)),
        ],
        out_specs=pl.BlockSpec((1, cc.shape[0], tn), lambda l, j: (l, 0, j)),
        out_shape=jax.ShapeDtypeStruct((depth, cc.shape[0], n), F32),
        compiler_params=_cparams("parallel", "parallel"),
        name="ada_mod",
    )(cc, w_ada, b_ada.reshape(depth, 1, n))


class _Layout:
    def __init__(self, n_batch, n_ctx, n_lat):
        t = TOKEN_TILE
        assert n_ctx % t == 0 and n_lat % t == 0
        self.b, self.n_ctx, self.n_lat, self.n_tot = n_batch, n_ctx, n_lat, n_ctx + n_lat
        self.t = t
        self.ctx_tiles, self.lat_tiles, self.seq_tiles = n_ctx // t, n_lat // t, (n_ctx + n_lat) // t

    def rows(self, with_ctx):
        return self.b * (self.n_tot if with_ctx else self.n_lat)

    def n_tiles(self, with_ctx):
        return self.b * (self.seq_tiles if with_ctx else self.lat_tiles)

    def src_tile(self, with_ctx):
        if with_ctx:
            return lambda i: i
        return lambda i: (i // self.lat_tiles) * self.seq_tiles + i % self.lat_tiles + self.ctx_tiles

    def mod_row(self, with_ctx):
        if with_ctx:
            return lambda i: jnp.where(i % self.seq_tiles < self.ctx_tiles, self.b, i // self.seq_tiles)
        return lambda i: i // self.lat_tiles


def _mod_row(modb_ref, modc_ref, r, n_rows, ctx_rows, tiles_per_seq):
    per_batch = modb_ref[0, r:r + 1, :]
    if not ctx_rows:
        return per_batch
    row = lax.broadcasted_iota(jnp.int32, (n_rows, 1), 0)
    first = pl.program_id(0) % tiles_per_seq == 0
    return jnp.where(jnp.logical_and(first, row < ctx_rows), modc_ref[0, r:r + 1, :], per_batch)


def _inproj_kernel(x_ref, modb_ref, modc_ref, g_ref, w_ref, da_ref, s5a_ref, s5b_ref, mla_ref, rw_ref,
                   *, ctx_rows, tiles_per_seq):
    x = x_ref[...]
    mrow = lambda r: _mod_row(modb_ref, modc_ref, r, x.shape[0], ctx_rows, tiles_per_seq)
    h = _modulate(x, g_ref[...], mrow(0), mrow(1))
    acc = jnp.dot(h.astype(BF16), w_ref[...], preferred_element_type=F32)
    da_ref[...] = acc[:, 0:_DA_W]
    s5a_ref[...] = acc[:, _DA_W:_DA_W + _S5_W // 2]
    s5b_ref[...] = acc[:, _DA_W + _S5_W // 2:_DA_W + _S5_W]
    mla_ref[...] = acc[:, _DA_W + _S5_W:_DA_W + _S5_W + _MLA_W]
    rw_ref[...] = acc[:, _DA_W + _S5_W + _MLA_W:_MIX_COLS]


def _seq_tile(lay):
    tm = PROJ_TILE if lay.n_tot % PROJ_TILE == 0 and lay.n_ctx <= PROJ_TILE else lay.t
    return tm, lay.n_tot // tm


def _inproj_call(lay, x_all, mod, g, w_mix):
    t = x_all.shape[0]
    tm, tps = _seq_tile(lay)
    widths = (_DA_W, _S5_W // 2, _S5_W // 2, _MLA_W, _RW_W)
    return pl.pallas_call(
        functools.partial(_inproj_kernel, ctx_rows=lay.n_ctx, tiles_per_seq=tps),
        grid=(t // tm,),
        in_specs=[
            pl.BlockSpec((tm, D_MODEL), lambda i: (i, 0)),
            pl.BlockSpec((1, 6, D_MODEL), lambda i: (i // tps, 0, 0)),
            pl.BlockSpec((1, 6, D_MODEL), lambda i: (lay.b, 0, 0)),
            _full(g), _full(w_mix),
        ],
        out_specs=[pl.BlockSpec((tm, w), lambda i: (i, 0)) for w in widths],
        out_shape=[jax.ShapeDtypeStruct((t, w), F32) for w in widths],
        compiler_params=_cparams("parallel"),
        name="in_proj",
    )(x_all, mod, mod, g, w_mix)


def _qkprep_kernel(da_ref, mla_ref, cda_ref, sda_ref, cml_ref, sml_ref, g32_ref, g64_ref,
                   gda_ref, gml_ref, cqg_ref, ckvg_ref, wuq_ref, wuk_ref, wuv_ref,
                   qd_ref, kd_ref, vd_ref, qm_ref, km_ref, vm_ref):
    g32 = g32_ref[...]
    g64 = g64_ref[...]
    cda, sda = cda_ref[...], sda_ref[...]
    q = _group_rms(da_ref[:, 0:MIX_W], g32, 1.0 / DA_DIM, gda_ref[0:1, :])
    qd_ref[...] = _rope(q, cda, sda, DA_DIM // 2, DA_DIM, DA_DIM // 2).astype(BF16)
    k = _group_rms(da_ref[:, MIX_W:2 * MIX_W], g32, 1.0 / DA_DIM, gda_ref[1:2, :])
    kd_ref[0] = _rope(k, cda, sda, DA_DIM // 2, DA_DIM, DA_DIM // 2).T.astype(BF16)
    vd_ref[...] = da_ref[:, 2 * MIX_W:3 * MIX_W].astype(BF16)

    cml, sml = cml_ref[...], sml_ref[...]
    cq = mla_ref[:, 0:256]
    cqn = cq * lax.rsqrt(jnp.sum(cq * cq, axis=-1, keepdims=True) * (1.0 / MLA_Q_RANK) + EPS) * cqg_ref[...]
    q = jnp.dot(cqn.astype(BF16), wuq_ref[...], preferred_element_type=F32)
    ckv = mla_ref[:, 256:384]
    ckvn = ckv * lax.rsqrt(jnp.mean(ckv * ckv, axis=-1, keepdims=True) + EPS) * ckvg_ref[...]
    ckvb = ckvn.astype(BF16)
    k = jnp.dot(ckvb, wuk_ref[...], preferred_element_type=F32) + mla_ref[:, 384:640]
    vm_ref[...] = jnp.dot(ckvb, wuv_ref[...], preferred_element_type=F32).astype(BF16)
    inv_n = 1.0 / (MLA_NOPE + MLA_ROPE)
    half = MLA_ROPE // 2
    q = _group_rms(q, g64, inv_n, gml_ref[0:1, :])
    qm_ref[...] = _rope(q, cml, sml, half, MLA_HEAD_PAD, MLA_NOPE + half).astype(BF16)
    k = _group_rms(k, g64, inv_n, gml_ref[1:2, :])
    km_ref[0] = _rope(k, cml, sml, half, MLA_HEAD_PAD, MLA_NOPE + half).T.astype(BF16)


def _qkprep_call(lay, da, mla, tabs, consts):
    t, tm = da.shape[0], lay.t
    st = lay.seq_tiles
    row = pl.BlockSpec((tm, MIX_W), lambda i: (i, 0))
    key_t = pl.BlockSpec((1, MIX_W, tm), lambda i: (i // st, 0, i % st))
    in_specs = [pl.BlockSpec((tm, _DA_W), lambda i: (i, 0)), pl.BlockSpec((tm, _MLA_W), lambda i: (i, 0))]
    in_specs += [pl.BlockSpec((tm, MIX_W), lambda i: (i % st, 0)) for _ in tabs]
    in_specs += [_full(a) for a in consts]
    tok = jax.ShapeDtypeStruct((t, MIX_W), BF16)
    keys = jax.ShapeDtypeStruct((lay.b, MIX_W, lay.n_tot), BF16)
    return pl.pallas_call(
        _qkprep_kernel,
        grid=(t // tm,),
        in_specs=in_specs,
        out_specs=[row, key_t, row, row, key_t, row],
        out_shape=[tok, keys, tok, tok, keys, tok],
        compiler_params=_cparams("parallel"),
        name="qk_prep",
    )(da, mla, *tabs, *consts)


def _softmax_parts(s):
    p = jnp.exp2(s - jnp.max(s, axis=-1, keepdims=True))
    return p, 1.0 / jnp.sum(p, axis=-1, keepdims=True)


def _attn_heads(q, kt_ref, v_ref, nk, diff, lam):
    lane = lax.broadcasted_iota(jnp.int32, (q.shape[0], MIX_W), 1)
    v = v_ref[0, 0:nk, :]
    acc = jnp.zeros((q.shape[0], MIX_W), F32)
    dk = DA_DIM if diff else MLA_HEAD_PAD
    per_head = 2 if diff else 1

    def scores(h):
        return [jnp.dot(q[:, e * dk:(e + 1) * dk], kt_ref[0, e * dk:(e + 1) * dk, 0:nk], preferred_element_type=F32)
                for e in range(per_head * h, per_head * (h + 1))]

    ahead = scores(0)
    for h in range(DA_HEADS):
        s = ahead
        if h + 1 < DA_HEADS:
            ahead = scores(h + 1)
        if diff:
            p0, r0 = _softmax_parts(s[0])
            p1, r1 = _softmax_parts(s[1])
            o = jnp.dot((p0 * r0 - p1 * (r1 * lam)).astype(BF16), v, preferred_element_type=F32)
        else:
            p, r = _softmax_parts(s[0])
            o = jnp.dot(p.astype(BF16), v, preferred_element_type=F32) * r
        in_head = jnp.logical_and(lane >= h * DA_VDIM, lane < (h + 1) * DA_VDIM)
        acc = jnp.where(in_head, o, acc)
    return acc


def _attn_kernel(q_ref, kt_ref, v_ref, lam_ref, gain_ref, g64_ref, o_ref, *, diff, n_ctx, n_tot, ctx_tiles):
    q = q_ref[...]
    lam = lam_ref[...]

    def run(nk):
        o = _attn_heads(q, kt_ref, v_ref, nk, diff, lam)
        if diff:
            o = _group_rms(o, g64_ref[...], 1.0 / DA_VDIM, gain_ref[...])
        o_ref[...] = o.astype(BF16)

    if ctx_tiles:
        is_ctx = pl.program_id(1) < ctx_tiles
        pl.when(is_ctx)(lambda: run(n_ctx))
        pl.when(jnp.logical_not(is_ctx))(lambda: run(n_tot))
    else:
        run(n_tot)


def _attention(lay, q, kt, v, extra, diff, with_ctx, name):
    tq = lay.t
    tiles = lay.seq_tiles if with_ctx else lay.lat_tiles
    off = 0 if with_ctx else lay.ctx_tiles
    v3 = v.reshape(lay.b, lay.n_tot, MIX_W)
    kern = functools.partial(_attn_kernel, diff=diff, n_ctx=lay.n_ctx, n_tot=lay.n_tot,
                             ctx_tiles=lay.ctx_tiles if with_ctx else 0)
    return pl.pallas_call(
        kern,
        grid=(lay.b, tiles),
        in_specs=[
            pl.BlockSpec((tq, MIX_W), lambda b, j: (b * lay.seq_tiles + j + off, 0)),
            pl.BlockSpec((1, MIX_W, lay.n_tot), lambda b, j: (b, 0, 0)),
            pl.BlockSpec((1, lay.n_tot, MIX_W), lambda b, j: (b, 0, 0)),
        ] + [_full(a) for a in extra],
        out_specs=pl.BlockSpec((tq, MIX_W), lambda b, j: (b * tiles + j, 0)),
        out_shape=jax.ShapeDtypeStruct((lay.rows(with_ctx), MIX_W), BF16),
        compiler_params=_cparams("parallel", "parallel"),
        name=name,
    )(q, kt, v3, *extra)


def _chunk_rows(ua_ref, ub_ref):
    n = ua_ref.shape[0] // S5_CHUNK
    parts = []
    for s in range(S5_CHUNK):
        rows = pl.ds(s, n, stride=S5_CHUNK)
        parts += [ua_ref[rows, :], ub_ref[rows, :]]
    return jnp.concatenate(parts, axis=1).astype(BF16)


def _s5_proj_kernel(ua_ref, ub_ref, bre_ref, bim_ref, sre_ref, sim_ref):
    u = _chunk_rows(ua_ref, ub_ref)
    sre_ref[0] = jnp.dot(u, bre_ref[0], preferred_element_type=F32)
    sim_ref[0] = jnp.dot(u, bim_ref[0], preferred_element_type=F32)


def _s5_rec_kernel(sre_ref, sim_ref, are_ref, aim_ref, hre_ref, him_ref, *, n_batch, n_chunks, ctx_chunks):
    rev = pl.program_id(0) == 1
    ar, ai = are_ref[0], aim_ref[0]
    sre, sim, hre, him = sre_ref.at[0], sim_ref.at[0], hre_ref.at[0], him_ref.at[0]

    def step(i, carry):
        hr, hi = carry
        k_rev = jnp.where(i < ctx_chunks, ctx_chunks - 1 - i, n_chunks - 1 + ctx_chunks - i)
        k = jnp.where(rev, k_rev, i)
        rows = pl.ds(k, n_batch, stride=n_chunks)
        hre[rows, :] = hr
        him[rows, :] = hi
        return ar * hr - ai * hi + sre[rows, :], ar * hi + ai * hr + sim[rows, :]

    zero = jnp.zeros((n_batch, 128), F32)
    lax.fori_loop(0, n_chunks, step, (zero, zero), unroll=2)


def _s5_out_kernel(ua_ref, ub_ref, hre_ref, him_ref, m_ref, cre_ref, cim_ref, ya_ref, yb_ref):
    y = jnp.dot(_chunk_rows(ua_ref, ub_ref), m_ref[0], preferred_element_type=F32)
    y = y + _split_dot(hre_ref[0], cre_ref[0]) + _split_dot(him_ref[0], cim_ref[0])
    n = y.shape[0]
    ya, yb = ya_ref.at[0], yb_ref.at[0]
    for s in range(S5_CHUNK):
        rows = pl.ds(s, n, stride=S5_CHUNK)
        ya[rows, :] = y[:, s * MIX_W:s * MIX_W + 128]
        yb[rows, :] = y[:, s * MIX_W + 128:(s + 1) * MIX_W]


def _s5_mats(lam_re, lam_im, log_dt, b_re, b_im, c_re, c_im):
    hp = lax.Precision.HIGHEST
    L, G, P, CH = S5_CHUNK, S5_GROUPS, S5_STATE, S5_CH
    lr, li = lam_re.astype(F32), lam_im.astype(F32)
    dt = jnp.exp(log_dt.astype(F32))[..., None]
    zr, zi = lr * dt, li * dt
    j = jnp.arange(L + 1, dtype=F32)[:, None, None, None]
    mag = jnp.exp(zr[None] * j)
    pw_re, pw_im = mag * jnp.cos(zi[None] * j), mag * jnp.sin(zi[None] * j)
    nr, ni = pw_re[1] - 1.0, pw_im[1]
    den = lr * lr + li * li
    cr, ci = (nr * lr + ni * li) / den, (ni * lr - nr * li) / den
    bre, bim = b_re.astype(F32), b_im.astype(F32)
    bb_re = cr[..., None] * bre - ci[..., None] * bim
    bb_im = cr[..., None] * bim + ci[..., None] * bre
    x_re = pw_re[..., None] * bb_re[None] - pw_im[..., None] * bb_im[None]
    x_im = pw_re[..., None] * bb_im[None] + pw_im[..., None] * bb_re[None]
    cre, cim = c_re.astype(F32), c_im.astype(F32)
    kern = (jnp.einsum('dgcp,jdgpe->dgjce', cre, x_re[:L], precision=hp)
            - jnp.einsum('dgcp,jdgpe->dgjce', cim, x_im[:L], precision=hp))
    def spread_mask(a, b):
        spread = jnp.asarray(np.tile(np.eye(b, dtype=np.float32), (1, G)))
        mask = jnp.asarray(np.kron(np.eye(G, dtype=np.float32), np.ones((a, b), np.float32)))
        return spread, mask

    kt = kern.transpose(0, 2, 1, 4, 3)
    xt_re, xt_im = x_re.transpose(1, 0, 2, 4, 3), x_im.transpose(1, 0, 2, 4, 3)
    pwt_re, pwt_im = pw_re.transpose(1, 0, 2, 3)[:, :, :, :, None], pw_im.transpose(1, 0, 2, 3)[:, :, :, :, None]
    cret, cimt = cre.transpose(0, 1, 3, 2)[:, None], cim.transpose(0, 1, 3, 2)[:, None]
    ca_re, ca_im = cret * pwt_re - cimt * pwt_im, -(cret * pwt_im + cimt * pwt_re)
    s_idx, t_idx = np.arange(L)[:, None], np.arange(L)[None, :]
    k_st, xb_re, xb_im, cq_re, cq_im = [], [], [], [], []
    for d in range(2):
        lag = (t_idx - s_idx) if d == 0 else (s_idx - t_idx)
        k_st.append(jnp.where(jnp.asarray(lag >= 0)[:, :, None, None, None], kt[d][np.clip(lag, 0, L - 1)], 0.0))
        pw = np.arange(L - 1, -1, -1) if d == 0 else np.arange(L)
        xb_re.append(xt_re[d][pw])
        xb_im.append(xt_im[d][pw])
        q = np.arange(1, L + 1) if d == 0 else np.arange(L, 0, -1)
        cq_re.append(ca_re[d][q])
        cq_im.append(ca_im[d][q])
    sp, mk = spread_mask(CH, CH)
    m = jnp.einsum('dstrb,bc->dsrtc', jnp.stack(k_st).reshape(2, L, L, G * CH, CH), sp, precision=hp) * mk[:, None, :]
    m = m.astype(BF16).reshape(2, L * G * CH, L * G * CH)
    sp, mk = spread_mask(CH, P)
    to_b = lambda x: (jnp.einsum('dsrb,bc->dsrc', jnp.stack(x).reshape(2, L, G * CH, P), sp, precision=hp) * mk
                      ).astype(BF16).reshape(2, L * G * CH, G * P)
    sp_c, mk_c = spread_mask(P, CH)
    to_c = lambda x: (jnp.einsum('dtrb,bc->drtc', jnp.stack(x).reshape(2, L, G * P, CH), sp_c, precision=hp)
                      * mk_c[:, None, :]).astype(BF16).reshape(2, G * P, L * G * CH)
    a_re, a_im = pw_re[L].reshape(2, 1, G * P), pw_im[L].reshape(2, 1, G * P)
    return m, to_b(xb_re), to_b(xb_im), to_c(cq_re), to_c(cq_im), a_re, a_im


def _s5_scan(lay, ua, ub, mats):
    m, b_r, b_i, c_r, c_i, a_re, a_im = mats
    n_chunks = lay.n_tot // S5_CHUNK
    rows = lay.b * n_chunks
    tr = min(lay.t, rows)
    tok = tr * S5_CHUNK
    half = MIX_W // 2
    wspec = lambda a: pl.BlockSpec((1,) + a.shape[1:], lambda d, i: (d, 0, 0))
    state = jax.ShapeDtypeStruct((2, rows, S5_STATE_W), F32)
    sblk = pl.BlockSpec((1, tr, S5_STATE_W), lambda d, i: (d, i, 0))
    ublk = pl.BlockSpec((tok, half), lambda d, i: (i, 0))
    s_re, s_im = pl.pallas_call(
        _s5_proj_kernel,
        grid=(2, rows // tr),
        in_specs=[ublk, ublk, wspec(b_r), wspec(b_i)],
        out_specs=[sblk, sblk],
        out_shape=[state, state],
        compiler_params=_cparams("parallel", "parallel"),
        name="s5_proj",
    )(ua, ub, b_r, b_i)
    col = pl.BlockSpec((1, rows, 128), lambda d, j: (d, 0, j))
    acol = pl.BlockSpec((1, 1, 128), lambda d, j: (d, 0, j))
    h_re, h_im = pl.pallas_call(
        functools.partial(_s5_rec_kernel, n_batch=lay.b, n_chunks=n_chunks, ctx_chunks=lay.n_ctx // S5_CHUNK),
        grid=(2, S5_STATE_W // 128),
        in_specs=[col, col, acol, acol],
        out_specs=[col, col],
        out_shape=[state, state],
        compiler_params=_cparams("parallel", "parallel"),
        name="s5_rec",
    )(s_re, s_im, a_re, a_im)
    yblk = pl.BlockSpec((1, tok, half), lambda d, i: (d, i, 0))
    yshape = jax.ShapeDtypeStruct((2, lay.b * lay.n_tot, half), F32)
    return pl.pallas_call(
        _s5_out_kernel,
        grid=(2, rows // tr),
        in_specs=[ublk, ublk, sblk, sblk, wspec(m), wspec(c_r), wspec(c_i)],
        out_specs=[yblk, yblk],
        out_shape=[yshape, yshape],
        compiler_params=_cparams("parallel", "parallel"),
        name="s5_out",
    )(ua, ub, h_re, h_im, m, c_r, c_i)


def _s5_glu_kernel(ua_ref, ub_ref, ya_ref, yb_ref, d_ref, w_ref, b_ref, o_ref):
    u = jnp.concatenate([ua_ref[...], ub_ref[...]], axis=1)
    y = d_ref[...] * u + jnp.concatenate([ya_ref[0] + ya_ref[1], yb_ref[0] + yb_ref[1]], axis=1)
    z = 0.5 * y * (1.0 + jnp.tanh(math.sqrt(2.0 / math.pi) * (y + 0.044715 * (y * y * y))))
    gate = _sigmoid(jnp.dot(z.astype(BF16), w_ref[...], preferred_element_type=F32) + b_ref[...])
    o_ref[...] = (z * gate).astype(BF16)


def _s5_glu_call(ua, ub, ya, yb, d, w, bias, tm):
    t, half = ua.shape
    urow = pl.BlockSpec((tm, half), lambda i: (i, 0))
    yrow = pl.BlockSpec((2, tm, half), lambda i: (0, i, 0))
    return pl.pallas_call(
        _s5_glu_kernel,
        grid=(t // tm,),
        in_specs=[urow, urow, yrow, yrow, _full(d), _full(w), _full(bias)],
        out_specs=pl.BlockSpec((tm, MIX_W), lambda i: (i, 0)),
        out_shape=jax.ShapeDtypeStruct((t, MIX_W), BF16),
        compiler_params=_cparams("parallel"),
        name="s5_glu",
    )(ua, ub, ya, yb, d, w, bias)


def _rw_pre_kernel(x_ref, prev_ref, next_ref, mu_ref, g64_ref, kk_g_ref, ka_ref, rk_ref,
                   w0_ref, w1_ref, w2_ref, a0_ref, a1_ref, a2_ref, g1_ref, g2_ref,
                   r_ref, v_ref, kk_ref, lw_ref, kka_ref, km_ref, bon_ref, gate_ref,
                   *, seq_tiles, ctx_tiles):
    x = x_ref[...]
    n = x.shape[0]
    j = pl.program_id(0) % seq_tiles
    starts = jnp.logical_or(j == 0, j == ctx_tiles)
    ends = jnp.logical_or(j == ctx_tiles - 1, j == seq_tiles - 1)
    prev_row = jnp.where(starts, 0.0, prev_ref[0, 7:8, :])
    next_row = jnp.where(ends, 0.0, next_ref[0, 0:1, :])
    row = lax.broadcasted_iota(jnp.int32, x.shape, 0)
    left = jnp.where(row == 0, prev_row, pltpu.roll(x, 1, axis=0))
    right = jnp.where(row == n - 1, next_row, pltpu.roll(x, n - 1, axis=0))
    x = x + (0.5 * (left + right) - x) * mu_ref[...]
    r, k, v, xd = (x[:, i * MIX_W:(i + 1) * MIX_W] for i in range(4))
    g64 = g64_ref[...]
    kscaled = k * kk_g_ref[...]
    kk = kscaled / jnp.maximum(jnp.sqrt(_split_dot(kscaled * kscaled, g64)), 1e-12)
    xdb = xd.astype(BF16)
    r_ref[...] = r
    v_ref[...] = v
    kk_ref[...] = kk
    km_sum = None
    for d in range(2):
        lo = jnp.tanh(jnp.dot(xdb, w1_ref[d], preferred_element_type=F32))
        w_raw = w0_ref[d] + jnp.dot(lo.astype(BF16), w2_ref[d], preferred_element_type=F32)
        lw_ref[d] = -_sigmoid(w_raw) * math.exp(-0.5)
        ar = jnp.dot(xdb, a1_ref[d], preferred_element_type=F32)
        a = _sigmoid(a0_ref[d] + jnp.dot(ar.astype(BF16), a2_ref[d], preferred_element_type=F32))
        km = k * (1.0 + (a - 1.0) * ka_ref[...])
        kka_ref[d] = kk * a
        km_ref[d] = km
        km_sum = km if km_sum is None else km_sum + km
    bon_ref[...] = _split_dot(r * km_sum * rk_ref[...], g64) * v
    gr = _sigmoid(jnp.dot(xdb, g1_ref[...], preferred_element_type=F32))
    gate_ref[...] = jnp.dot(gr.astype(BF16), g2_ref[...], preferred_element_type=F32)


def _rw_pre_call(lay, rw, consts):
    t, tr = rw.shape[0], lay.t
    nt = t // tr
    g8 = tr // 8
    rw8 = rw.reshape(t // 8, 8, _RW_W)
    row = pl.BlockSpec((tr, MIX_W), lambda i: (i, 0))
    row2 = pl.BlockSpec((2, tr, MIX_W), lambda i: (0, i, 0))
    sd = jax.ShapeDtypeStruct((t, MIX_W), F32)
    sd2 = jax.ShapeDtypeStruct((2, t, MIX_W), F32)
    return pl.pallas_call(
        functools.partial(_rw_pre_kernel, seq_tiles=lay.seq_tiles, ctx_tiles=lay.ctx_tiles),
        grid=(nt,),
        in_specs=[pl.BlockSpec((tr, _RW_W), lambda i: (i, 0)),
                  pl.BlockSpec((1, 8, _RW_W), lambda i: (jnp.maximum(i * g8 - 1, 0), 0, 0)),
                  pl.BlockSpec((1, 8, _RW_W), lambda i: (jnp.minimum((i + 1) * g8, t // 8 - 1), 0, 0))]
                 + [_full(a) for a in consts],
        out_specs=[row, row, row, row2, row2, row2, row, row],
        out_shape=[sd, sd, sd, sd2, sd2, sd2, sd, sd],
        compiler_params=_cparams("parallel"),
        name="rwkv_pre",
    )(rw, rw8, rw8, *consts)


def _head_masks(shape, lane_axis, seg):
    lane = lax.broadcasted_iota(jnp.int32, shape, lane_axis)
    return [jnp.logical_and(lane >= h * seg, lane < (h + 1) * seg) for h in range(RW_HEADS)]


def _rw_prep_kernel(*refs, rev):
    tiles = [_rw_prep_tile(sub, *refs, rev=rev) for sub in range(RW_PREP_TILES)]
    while tiles:
        tiles = [t for t in tiles if next(t, None) is not None]


def _rw_prep_tile(sub, r_ref, kk_ref, v_ref, lw_ref, ka_ref, km_ref, perm_ref, permt_ref, g_ref, eye_ref,
                  br_ref, ck_ref, uvt_ref, y0_ref, pc_ref, *, rev):
    C, NC = RW_CHUNK, RW_TILE // RW_CHUNK
    perm, permt, g64, eye4 = perm_ref[...], permt_ref[...], g_ref[...], eye_ref[...]
    tok = slice(sub * RW_TILE, (sub + 1) * RW_TILE)
    nat = jnp.concatenate([r_ref[0, tok, :], kk_ref[0, tok, :], v_ref[0, tok, :],
                           lw_ref[0, 0, tok, :], ka_ref[0, 0, tok, :], km_ref[0, 0, tok, :]], axis=1)
    hi = nat.astype(BF16)
    lo = (nat - hi.astype(F32)).astype(BF16)
    pm = jnp.dot(perm, hi, preferred_element_type=F32) + jnp.dot(perm, lo, preferred_element_type=F32)
    r, kk, v, lw, ka, km = (pm[:, i * MIX_W:(i + 1) * MIX_W] for i in range(6))
    slab = lambda x, j: x[j * NC:(j + 1) * NC, :]
    order = list(range(C))[::-1] if rev else list(range(C))
    pos = {j: i for i, j in enumerate(order)}
    cum, run = {}, None
    for j in order:
        run = slab(lw, j) if run is None else run + slab(lw, j)
        cum[j] = run
    tot = run
    yield True
    bh, ch, kh, rh, cp, kp, vv = {}, {}, {}, {}, {}, {}, {}
    for j in range(C):
        e_inv, e_end = jnp.exp(-cum[j]), jnp.exp(tot - cum[j])
        bh[j] = -slab(kk, j) * jnp.exp(cum[j] - slab(lw, j))
        ch[j], kh[j] = slab(ka, j) * e_inv, slab(km, j) * e_inv
        rh[j] = slab(r, j) * jnp.exp(cum[j])
        cp[j], kp[j] = slab(ka, j) * e_end, slab(km, j) * e_end
        vv[j] = slab(v, j)
    strict = [(t, s) for t in order for s in order if pos[s] < pos[t]]
    incl = [(t, s) for t in order for s in order if pos[s] <= pos[t]]
    def head_dots(lhs, rhs, pairs):
        prods = jnp.concatenate([lhs[t] * rhs[s] for t, s in pairs], axis=0).astype(BF16)
        gram = jnp.dot(prods, g64, preferred_element_type=F32)
        return {p: gram[i * NC:(i + 1) * NC, :] for i, p in enumerate(pairs)}

    yield True
    acb = head_dots(bh, ch, strict)
    yield True
    akb = head_dots(bh, kh, strict)
    yield True
    mcr = head_dots(rh, ch, incl)
    yield True
    mkr = head_dots(rh, kh, incl)
    yield True
    bt, u0 = {}, {}
    for t in order:
        b_acc, u_acc = bh[t], jnp.zeros_like(bh[t])
        for s in order:
            if pos[s] < pos[t]:
                b_acc = b_acc + acb[(t, s)] * bt[s]
                u_acc = u_acc + akb[(t, s)] * vv[s] + acb[(t, s)] * u0[s]
        bt[t], u0[t] = b_acc, u_acc
        yield True
    rt, y0 = {}, {}
    for t in order:
        r_acc, y_acc = rh[t], jnp.zeros_like(rh[t])
        for s in order:
            if pos[s] <= pos[t]:
                r_acc = r_acc + mcr[(t, s)] * bt[s]
                y_acc = y_acc + mcr[(t, s)] * u0[s] + mkr[(t, s)] * vv[s]
        rt[t], y0[t] = r_acc, y_acc
        yield True
    stackp = lambda dct: jnp.concatenate([dct[j] for j in range(C)], axis=0)
    b16 = lambda x: x.astype(BF16)
    y0p, u0p = stackp(y0), stackp(u0)
    y0h, u0h = b16(y0p), b16(u0p)
    cat = jnp.concatenate([b16(stackp(bt)), b16(stackp(rt)), b16(stackp(cp)), b16(stackp(kp)),
                           y0h, b16(y0p - y0h.astype(F32)), u0h, b16(u0p - u0h.astype(F32)), b16(stackp(vv))], axis=1)
    natural = jnp.dot(permt, cat, preferred_element_type=F32)
    seg = lambda i: natural[:, i * MIX_W:(i + 1) * MIX_W]
    yield True
    btn, rtn, cpn, kpn = b16(seg(0)), b16(seg(1)), b16(seg(2)), b16(seg(3))
    y0_ref[0, tok, :] = seg(4) + seg(5)
    u0h_n, u0l_n, vn = b16(seg(6)), b16(seg(7)), b16(seg(8))
    hm = _head_masks((C, MIX_W), 1, RW_DIM)
    zero = jnp.zeros((C, MIX_W), BF16)
    zh, zl = [], []
    for c in range(NC):
        rows = slice(c * C, (c + 1) * C)
        br_ref[0, sub * NC + c, 0:C, :] = btn[rows]
        br_ref[0, sub * NC + c, C:2 * C, :] = rtn[rows]
        ck_ref[0, sub * NC + c, 0:C, :] = cpn[rows]
        ck_ref[0, sub * NC + c, C:2 * C, :] = kpn[rows]
        for h in range(RW_HEADS):
            zh += [jnp.where(hm[h], u0h_n[rows], zero), jnp.where(hm[h], vn[rows], zero)]
            zl += [jnp.where(hm[h], u0l_n[rows], zero), zero]
    uvt = (lax.dot_general(eye4, jnp.concatenate(zh, axis=0), _NT, preferred_element_type=F32)
           + lax.dot_general(eye4, jnp.concatenate(zl, axis=0), _NT, preferred_element_type=F32))
    for c in range(NC):
        uvt_ref[0, sub * NC + c] = uvt[:, c * 2 * C * RW_HEADS:(c + 1) * 2 * C * RW_HEADS]
    pc_ref[0, sub * NC:(sub + 1) * NC, :] = jnp.exp(tot)


def _rw_prep_call(lay, shared, perdir, consts, rev):
    b, n_tot, tt = lay.b, lay.n_tot, RW_TILE * RW_PREP_TILES
    assert n_tot % tt == 0
    nck = n_tot // RW_CHUNK
    cpt = tt // RW_CHUNK
    d = 1 if rev else 0
    sh = [a.reshape(b, n_tot, MIX_W) for a in shared]
    pd = [a.reshape(2, b, n_tot, MIX_W) for a in perdir]
    tok = pl.BlockSpec((1, tt, MIX_W), lambda i, j: (i, j, 0))
    tok_d = pl.BlockSpec((1, 1, tt, MIX_W), lambda i, j: (d, i, j, 0))
    rows32 = pl.BlockSpec((1, cpt, 2 * RW_CHUNK, MIX_W), lambda i, j: (i, j, 0, 0))
    return pl.pallas_call(
        functools.partial(_rw_prep_kernel, rev=rev),
        grid=(b, n_tot // tt),
        in_specs=[tok] * 3 + [tok_d] * 3 + [_full(a) for a in consts],
        out_specs=[rows32, rows32,
                   pl.BlockSpec((1, cpt, RW_DIM, 2 * RW_CHUNK * RW_HEADS), lambda i, j: (i, j, 0, 0)),
                   tok,
                   pl.BlockSpec((1, cpt, MIX_W), lambda i, j: (i, j, 0))],
        out_shape=[jax.ShapeDtypeStruct((b, nck, 2 * RW_CHUNK, MIX_W), BF16),
                   jax.ShapeDtypeStruct((b, nck, 2 * RW_CHUNK, MIX_W), BF16),
                   jax.ShapeDtypeStruct((b, nck, RW_DIM, 2 * RW_CHUNK * RW_HEADS), F32),
                   jax.ShapeDtypeStruct((b, n_tot, MIX_W), F32),
                   jax.ShapeDtypeStruct((b, nck, MIX_W), F32)],
        compiler_params=_cparams("parallel", "parallel"),
        name="rwkv_prep_rev" if rev else "rwkv_prep_fwd",
    )(*sh, *pd, *consts)


def _rw_scan_kernel(brf, ckf, uvtf, pcf, brr, ckr, uvtr, pcr, ytf_ref, ytr_ref, s_scr, *, n_batch):
    @pl.when(pl.program_id(0) == 0)
    def _():
        s_scr[...] = jnp.zeros_like(s_scr)

    cpt = RW_TILE // RW_CHUNK
    hm = _head_masks((2 * RW_CHUNK, MIX_W), 1, RW_DIM)
    lane = lax.broadcasted_iota(jnp.int32, (RW_DIM, 2 * RW_CHUNK * RW_HEADS), 1)
    is_u = (lane & (2 * RW_CHUNK - 1)) < RW_CHUNK
    per_head = lambda x: jnp.concatenate([jnp.where(m, x, jnp.zeros_like(x)) for m in hm], axis=0)

    def refs_of(p, c):
        d, b = divmod(p, n_batch)
        refs = (brf, ckf, uvtf, pcf, ytf_ref) if d == 0 else (brr, ckr, uvtr, pcr, ytr_ref)
        return refs, b, (c if d == 0 else cpt - 1 - c)

    def step(c, carry):
        lhs = []
        for p in range(2 * n_batch):
            (br_ref, _, uvt_ref, _, yt_ref), b, cc = refs_of(p, c)
            s = s_scr[p]
            shi = s.astype(BF16)
            slo = (s - shi.astype(F32)).astype(BF16)
            w2 = lax.dot_general(jnp.concatenate([shi, slo], axis=0), per_head(br_ref[b, cc]), _NT, preferred_element_type=F32)
            w = w2[:RW_DIM] + w2[RW_DIM:]
            yt_ref[b, cc] = w
            uvt = uvt_ref[b, cc]
            lhs.append(jnp.where(is_u, w + uvt, uvt).astype(BF16))
        for p in range(2 * n_batch):
            (_, ck_ref, _, pc_ref, _), b, cc = refs_of(p, c)
            s_scr[p] = (s_scr[p] * pc_ref[b, pl.ds(cc, 1), :]
                        + jnp.dot(lhs[p], per_head(ck_ref[b, cc]), preferred_element_type=F32))
        return carry

    lax.fori_loop(0, cpt, step, 0)


def _rw_scan_call(lay, fwd, rev):
    b, n_tot, tt = lay.b, lay.n_tot, RW_TILE
    assert lay.n_ctx % tt == 0 and lay.n_lat % tt == 0
    nt, ct = n_tot // tt, lay.n_ctx // tt
    cpt = tt // RW_CHUNK
    rev_tile = lambda i: jnp.where(i < ct, ct - 1 - i, nt - 1 + ct - i)

    def specs(tile):
        return [pl.BlockSpec((b, cpt, 2 * RW_CHUNK, MIX_W), lambda i: (0, tile(i), 0, 0)),
                pl.BlockSpec((b, cpt, 2 * RW_CHUNK, MIX_W), lambda i: (0, tile(i), 0, 0)),
                pl.BlockSpec((b, cpt, RW_DIM, 2 * RW_CHUNK * RW_HEADS), lambda i: (0, tile(i), 0, 0)),
                pl.BlockSpec((b, cpt, MIX_W), lambda i: (0, tile(i), 0))]

    ident = lambda i: i
    yt = jax.ShapeDtypeStruct((b, n_tot // RW_CHUNK, RW_DIM, 2 * RW_CHUNK * RW_HEADS), F32)
    return pl.pallas_call(
        functools.partial(_rw_scan_kernel, n_batch=b),
        grid=(nt,),
        in_specs=specs(ident) + specs(rev_tile),
        out_specs=[specs(ident)[2], specs(rev_tile)[2]],
        out_shape=[yt, yt],
        scratch_shapes=[pltpu.VMEM((2 * b, RW_DIM, MIX_W), F32)],
        compiler_params=_cparams("arbitrary"),
        name="rwkv_scan",
    )(*fwd, *rev)


def _rw_fin_kernel(ytf_ref, ytr_ref, y0f_ref, y0r_ref, bon_ref, gate_ref, asel_ref, g64_ref, lng_ref, lnb_ref, o_ref):
    cpt = RW_TILE // RW_CHUNK
    asel = asel_ref[...]
    width = cpt * 2 * RW_CHUNK * RW_HEADS
    lane = lax.broadcasted_iota(jnp.int32, (RW_DIM, width), 1)
    lane_head = jnp.bitwise_and(jnp.right_shift(lane, 5), RW_HEADS - 1)

    def base(yt_ref):
        yt = jnp.concatenate([yt_ref[0, c] for c in range(cpt)], axis=1).astype(BF16)
        rows = jnp.concatenate([jnp.where(lane_head == h, yt, jnp.zeros_like(yt)) for h in range(RW_HEADS)], axis=0)
        return lax.dot_general(asel, rows, _NT, preferred_element_type=F32)

    y = base(ytf_ref) + y0f_ref[0] + base(ytr_ref) + y0r_ref[0]
    g64 = g64_ref[...]
    mean = _split_dot(y, g64) * (1.0 / RW_DIM)
    c = y - mean
    var = _split_dot(c * c, g64) * (1.0 / RW_DIM)
    out = c * lax.rsqrt(var + RW_LN_EPS) * lng_ref[...] + lnb_ref[...] + bon_ref[...]
    o_ref[...] = (out * gate_ref[...]).astype(BF16)


def _rw_fin_call(lay, ytf, ytr, y0f, y0r, bon, gate, consts):
    b, n_tot, tt = lay.b, lay.n_tot, RW_TILE
    nt = n_tot // tt
    cpt = tt // RW_CHUNK
    ytb = pl.BlockSpec((1, cpt, RW_DIM, 2 * RW_CHUNK * RW_HEADS), lambda i, j: (i, j, 0, 0))
    y0b = pl.BlockSpec((1, tt, MIX_W), lambda i, j: (i, j, 0))
    row = pl.BlockSpec((tt, MIX_W), lambda i, j: (i * nt + j, 0))
    return pl.pallas_call(
        _rw_fin_kernel,
        grid=(b, nt),
        in_specs=[ytb, ytb, y0b, y0b, row, row] + [_full(a) for a in consts],
        out_specs=row,
        out_shape=jax.ShapeDtypeStruct((b * n_tot, MIX_W), BF16),
        compiler_params=_cparams("parallel", "parallel"),
        name="rwkv_finish",
    )(ytf, ytr, y0f, y0r, bon, gate, *consts)


def _rw_constants():
    c, nc = RW_CHUNK, RW_TILE // RW_CHUNK
    perm = np.zeros((RW_TILE, RW_TILE), np.float32)
    for ci in range(nc):
        for j in range(c):
            perm[j * nc + ci, ci * c + j] = 1.0
    lane = np.arange(MIX_W) % RW_DIM
    eye4 = (lane[None, :] == np.arange(RW_DIM)[:, None]).astype(np.float32)
    lanes = np.arange(nc * 2 * c * RW_HEADS)
    lane_chunk, lane_tok = lanes // (2 * c * RW_HEADS), lanes % (2 * c)
    t = np.arange(RW_TILE)
    asel = ((lane_chunk[None, :] == (t // c)[:, None]) & (lane_tok[None, :] == (c + t % c)[:, None])).astype(np.float32)
    as16 = lambda a: jnp.asarray(a, BF16)
    return as16(perm), as16(perm.T), as16(eye4), as16(asel)


def _rwkv_branch(lay, rw, pre_consts, g64, lng, lnb):
    r_, v_, kk_, lw_, kka_, km_, bon, gate = _rw_pre_call(lay, rw, pre_consts)
    perm, permt, eye4, asel = _rw_constants()
    prep_consts = (perm, permt, g64, eye4)
    fwd = _rw_prep_call(lay, (r_, kk_, v_), (lw_, kka_, km_), prep_consts, False)
    rev = _rw_prep_call(lay, (r_, kk_, v_), (lw_, kka_, km_), prep_consts, True)
    pick = lambda o: (o[0], o[1], o[2], o[4])
    ytf, ytr = _rw_scan_call(lay, pick(fwd), pick(rev))
    return _rw_fin_call(lay, ytf, ytr, fwd[3], rev[3], bon, gate, (asel, g64, lng, lnb))


def _merge_kernel(x_ref, modb_ref, modc_ref, g_ref, wg_ref, ya_ref, yb_ref, yc_ref, yd_ref, wb_ref, wo_ref, o_ref,
                  *, ctx_rows, tiles_per_seq):
    x = x_ref[...]
    mrow = lambda r: _mod_row(modb_ref, modc_ref, r, x.shape[0], ctx_rows, tiles_per_seq)
    h = _modulate(x, g_ref[...], mrow(0), mrow(1)).astype(BF16)
    merged = None
    for i, y_ref in enumerate((ya_ref, yb_ref, yc_ref, yd_ref)):
        gate = _sigmoid(jnp.dot(h, wg_ref[:, i * D_MODEL:(i + 1) * D_MODEL], preferred_element_type=F32))
        term = gate * jnp.dot(y_ref[...], wb_ref[i], preferred_element_type=F32)
        merged = term if merged is None else merged + term
    out = jnp.dot(merged.astype(BF16), wo_ref[...], preferred_element_type=F32)
    o_ref[...] = x + mrow(2) * out


def _merge_call(lay, with_ctx, x_all, mod, g, w_gate, ya, yb, yc, yd, w_branch, w_out):
    if with_ctx:
        tm, tps = _seq_tile(lay)
        src, n_tiles, ctx_rows = (lambda i: i), lay.b * tps, lay.n_ctx
    else:
        tm, tps = lay.t, lay.lat_tiles
        src, n_tiles, ctx_rows = lay.src_tile(False), lay.n_tiles(False), 0
    full_row = lambda w: pl.BlockSpec((tm, w), lambda i: (src(i), 0))
    out_row = lambda w: pl.BlockSpec((tm, w), lambda i: (i, 0))
    return pl.pallas_call(
        functools.partial(_merge_kernel, ctx_rows=ctx_rows, tiles_per_seq=tps),
        grid=(n_tiles,),
        in_specs=[full_row(D_MODEL), pl.BlockSpec((1, 6, D_MODEL), lambda i: (i // tps, 0, 0)),
                  pl.BlockSpec((1, 6, D_MODEL), lambda i: (lay.b, 0, 0)), _full(g), _full(w_gate),
                  out_row(MIX_W), full_row(MIX_W), out_row(MIX_W), full_row(MIX_W), _full(w_branch), _full(w_out)],
        out_specs=out_row(D_MODEL),
        out_shape=jax.ShapeDtypeStruct((lay.rows(with_ctx), D_MODEL), F32),
        compiler_params=_cparams("parallel"),
        name="merge_out",
    )(x_all, mod, mod, g, w_gate, ya, yb, yc, yd, w_branch, w_out)


def _router_kernel(x_ref, mod_ref, g_ref, wh_ref, wl_ref, bias_ref, f_ref, comb_ref, gid_ref):
    f = _modulate(x_ref[...], g_ref[...], mod_ref[0, 3:4, :], mod_ref[0, 4:5, :])
    fh = f.astype(BF16)
    f_ref[...] = fh
    fl = (f - fh.astype(F32)).astype(BF16)
    nt = (((1,), (1,)), ((), ()))
    wh, wl = wh_ref[...], wl_ref[...]
    logits = (lax.dot_general(wh, fh, nt, preferred_element_type=F32)
              + lax.dot_general(wh, fl, nt, preferred_element_type=F32)
              + lax.dot_general(wl, fh, nt, preferred_element_type=F32))
    scores = _sigmoid(logits)
    biased = scores + bias_ref[...]
    sc = [scores[e:e + 1, :] for e in range(N_EXPERTS)]
    bi = [biased[e:e + 1, :] for e in range(N_EXPERTS)]
    group_score = []
    for g in range(N_GROUPS):
        a, b, c, d = bi[4 * g:4 * g + 4]
        m1, n1, m2, n2 = jnp.maximum(a, b), jnp.minimum(a, b), jnp.maximum(c, d), jnp.minimum(c, d)
        group_score.append(jnp.maximum(m1, m2) + jnp.maximum(jnp.minimum(m1, m2), jnp.maximum(n1, n2)))

    def first_argmax(vals):
        top = functools.reduce(jnp.maximum, vals)
        seen, hot = None, []
        for v in vals:
            h = v == top
            if seen is not None:
                h = jnp.logical_and(h, jnp.logical_not(seen))
            seen = h if seen is None else jnp.logical_or(seen, h)
            hot.append(h)
        return hot

    in_group = first_argmax(group_score)
    masked = [jnp.where(in_group[e // EXPERTS_PER_GROUP], bi[e], -jnp.inf) for e in range(N_EXPERTS)]
    hot1 = first_argmax(masked)
    hot2 = first_argmax([jnp.where(h, -jnp.inf, v) for h, v in zip(hot1, masked)])
    w1 = functools.reduce(jnp.add, [jnp.where(h, s, 0.0) for h, s in zip(hot1, sc)])
    w2 = functools.reduce(jnp.add, [jnp.where(h, s, 0.0) for h, s in zip(hot2, sc)])
    inv_tot = 1.0 / (w1 + w2)
    for e in range(N_EXPERTS):
        comb_ref[e:e + 1, :] = (jnp.where(hot1[e], w1, 0.0) + jnp.where(hot2[e], w2, 0.0)) * inv_tot
    gid_ref[...] = functools.reduce(jnp.add, [jnp.where(in_group[g], g, 0) for g in range(1, N_GROUPS)])


def _router_call(lay, with_ctx, x, mod, g, wh, wl, bias):
    t, tm = x.shape[0], lay.t
    mrow = lay.mod_row(with_ctx)
    return pl.pallas_call(
        _router_kernel,
        grid=(t // tm,),
        in_specs=[pl.BlockSpec((tm, D_MODEL), lambda i: (i, 0)),
                  pl.BlockSpec((1, 6, D_MODEL), lambda i: (mrow(i), 0, 0)), _full(g), _full(wh), _full(wl), _full(bias)],
        out_specs=[pl.BlockSpec((tm, D_MODEL), lambda i: (i, 0)), pl.BlockSpec((N_EXPERTS, tm), lambda i: (0, i)),
                   pl.BlockSpec((1, tm), lambda i: (0, i))],
        out_shape=[jax.ShapeDtypeStruct((t, D_MODEL), BF16), jax.ShapeDtypeStruct((N_EXPERTS, t), F32),
                   jax.ShapeDtypeStruct((1, t), jnp.int32)],
        compiler_params=_cparams("parallel"),
        name="moe_router",
    )(x, mod, g, wh, wl, bias)


def _moe_plan(gid, n_tiles, tm):
    g = gid.reshape(n_tiles, tm)
    onehot = (g[..., None] == jnp.arange(N_GROUPS, dtype=jnp.int32)).astype(jnp.int32)
    rank = jnp.cumsum(onehot, axis=1) - onehot
    counts = jnp.sum(onehot, axis=1)
    padded = (counts + 15) // 16 * 16
    offs = jnp.cumsum(padded, axis=1) - padded
    pos = jnp.sum(onehot * (offs[:, None, :] + rank), axis=-1)
    n_over = (jnp.maximum(padded - MOE_BLOCK, 0) + MOE_OVER - 1) // MOE_OVER
    return pos.astype(jnp.int32), offs.astype(jnp.int32), n_over.astype(jnp.int32)


def _moe_kernel(offs_ref, nover_ref, f_ref, posr_ref, posc_ref, comb_ref, wg_ref, wu_ref, wd_ref, x_ref, modb_ref,
                modc_ref, o_ref, xs_scr, cs_scr, ys_scr, *, ctx_rows, tiles_per_seq):
    i, e = pl.program_id(0), pl.program_id(1)
    n_slots, tm = xs_scr.shape[0], f_ref.shape[0]
    n_live = min(n_slots, -(-(tm + 16 * N_GROUPS) // 256) * 256)

    @pl.when(e == 0)
    def _():
        slot = lax.broadcasted_iota(jnp.int32, (n_live, tm), 0)
        place = (slot == posr_ref[0]).astype(BF16)
        xs_scr[0:n_live, :] = jnp.dot(place, f_ref[...], preferred_element_type=F32).astype(BF16)
        xs_scr[n_live:n_slots, :] = jnp.zeros((n_slots - n_live, D_MODEL), BF16)
        cs_scr[0:n_live, :] = _split_dot_rhs(place, comb_ref[...])
        cs_scr[n_live:n_slots, :] = jnp.zeros((n_slots - n_live, N_EXPERTS), F32)
        ys_scr[...] = jnp.zeros_like(ys_scr)

    grp = lax.shift_right_logical(e, 2)
    start = offs_ref[i, grp]
    lane = lax.broadcasted_iota(jnp.int32, (1, N_EXPERTS), 1)

    def run(rows):
        xb = xs_scr[rows, :]
        gate = jnp.dot(xb, wg_ref[0], preferred_element_type=F32)
        up = jnp.dot(xb, wu_ref[0], preferred_element_type=F32)
        act = (gate * _sigmoid(gate) * up).astype(BF16)
        down = jnp.dot(act, wd_ref[0], preferred_element_type=F32)
        c_e = jnp.sum(jnp.where(lane == e, cs_scr[rows, :], 0.0), axis=1, keepdims=True)
        ys_scr[rows, :] += c_e * down

    run(pl.ds(pl.multiple_of(start, 16), MOE_BLOCK))

    def overflow(k, carry):
        run(pl.ds(pl.multiple_of(start + MOE_BLOCK + k * MOE_OVER, 16), MOE_OVER))
        return carry

    lax.fori_loop(0, nover_ref[i, grp], overflow, 0)

    @pl.when(e == N_EXPERTS - 1)
    def _():
        slot = lax.broadcasted_iota(jnp.int32, (tm, n_live), 1)
        fetch = (slot == posc_ref[...]).astype(BF16)
        y = jnp.dot(fetch, ys_scr[0:n_live, :].astype(BF16), preferred_element_type=F32)
        res_gate = modb_ref[0, 5:6, :]
        if ctx_rows:
            row = lax.broadcasted_iota(jnp.int32, y.shape, 0)
            first = i % tiles_per_seq == 0
            res_gate = jnp.where(jnp.logical_and(first, row < ctx_rows), modc_ref[0, 5:6, :], res_gate)
        o_ref[...] = x_ref[...] + res_gate * y


def _moe_call(lay, with_ctx, f, comb, gid, wg, wu, wd, x, mod):
    t = f.shape[0]
    seq = lay.n_tot if with_ctx else lay.n_lat
    tm = MOE_TILE if seq % MOE_TILE == 0 else math.gcd(seq, 1024)
    tps, n_tiles = seq // tm, t // tm
    ctx_rows = lay.n_ctx if with_ctx else 0
    assert ctx_rows <= tm
    n_slots = -(-(tm + 16 * N_GROUPS + MOE_BLOCK + MOE_OVER) // 256) * 256
    pos, offs, n_over = _moe_plan(gid, n_tiles, tm)
    wspec = lambda a: pl.BlockSpec((1,) + a.shape[1:], lambda i, e, *_: (e, 0, 0))
    tok = lambda w: pl.BlockSpec((tm, w), lambda i, e, *_: (i, 0))
    grid_spec = pltpu.PrefetchScalarGridSpec(
        num_scalar_prefetch=2,
        grid=(n_tiles, N_EXPERTS),
        in_specs=[tok(D_MODEL), pl.BlockSpec((1, 1, tm), lambda i, e, *_: (i, 0, 0)), tok(1), tok(N_EXPERTS),
                  wspec(wg), wspec(wu), wspec(wd), tok(D_MODEL),
                  pl.BlockSpec((1, 6, D_MODEL), lambda i, e, *_: (i // tps, 0, 0)),
                  pl.BlockSpec((1, 6, D_MODEL), lambda i, e, *_: (lay.b, 0, 0))],
        out_specs=tok(D_MODEL),
        scratch_shapes=[pltpu.VMEM((n_slots, D_MODEL), BF16), pltpu.VMEM((n_slots, N_EXPERTS), F32),
                        pltpu.VMEM((n_slots, D_MODEL), F32)])
    return pl.pallas_call(
        functools.partial(_moe_kernel, ctx_rows=ctx_rows, tiles_per_seq=tps),
        grid_spec=grid_spec,
        out_shape=jax.ShapeDtypeStruct((t, D_MODEL), F32),
        compiler_params=_cparams("parallel", "arbitrary"),
        name="moe_experts",
    )(offs, n_over, f, pos.reshape(n_tiles, 1, tm), pos.reshape(t, 1), comb, wg, wu, wd, x, mod, mod)


def _block_ones(n, group):
    i = np.arange(n) // group
    return jnp.asarray(i[:, None] == i[None, :], dtype=BF16)


def _rope_tables(n_ctx, n_lat):
    rows = n_lat // GRID_W
    row = jnp.repeat(jnp.arange(rows, dtype=F32), GRID_W)
    col = jnp.tile(jnp.arange(GRID_W, dtype=F32), rows)

    def angles(rot_dim):
        n_freq = rot_dim // 4
        inv_freq = ROPE_BASE ** (-jnp.arange(n_freq, dtype=F32) / n_freq)
        ang = jnp.concatenate([row[:, None] * inv_freq, col[:, None] * inv_freq], axis=-1)
        return jnp.cos(ang), jnp.sin(ang)

    c, s = angles(DA_DIM)
    cda = jnp.tile(jnp.concatenate([c, c], -1), (1, 2 * DA_HEADS))
    sda = jnp.tile(jnp.concatenate([-s, s], -1), (1, 2 * DA_HEADS))
    c, s = angles(MLA_ROPE)
    one = jnp.ones((n_lat, MLA_NOPE), F32)
    pad = MLA_HEAD_PAD - MLA_NOPE - MLA_ROPE
    cml = jnp.tile(jnp.concatenate([one, c, c, jnp.ones((n_lat, pad), F32)], -1), (1, MLA_HEADS))
    sml = jnp.tile(jnp.concatenate([0 * one, -s, s, jnp.zeros((n_lat, pad), F32)], -1), (1, MLA_HEADS))
    ident = lambda t, v: jnp.concatenate([jnp.full((n_ctx, MIX_W), v, F32), t], axis=0)
    return ident(cda, 1.0), ident(sda, 0.0), ident(cml, 1.0), ident(sml, 0.0)


def _pad_heads(w, n_heads, src_w, lo, hi, dst_w=MLA_HEAD_PAD):
    w = w.reshape(w.shape[0], n_heads, src_w)[:, :, lo:hi]
    w = jnp.pad(w, ((0, 0), (0, 0), (0, dst_w - (hi - lo))))
    return w.reshape(w.shape[0], n_heads * dst_w)


def _mix_weight(w_in_l):
    w = w_in_l
    kr = w[:, 1344:1360]
    z = lambda n: jnp.zeros((D_MODEL, n), w.dtype)
    kr_wide = jnp.concatenate([jnp.concatenate([z(MLA_NOPE), kr, z(MLA_HEAD_PAD - MLA_NOPE - MLA_ROPE)], 1)] * MLA_HEADS, 1)
    return jnp.concatenate([w[:, 0:1024], w[:, 1024:1216], z(64), w[:, 1216:1344], kr_wide, w[:, 1360:2384]], axis=1).astype(BF16)


def kernel(x, c, ctx, c_ctx, w_ada, b_ada, norm_mix_g, norm_ffn_g, w_in, da_qk_norm_g, da_lambda, da_subln_g, s5_lam_re, s5_lam_im, s5_log_dt, s5_b_re, s5_b_im, s5_c_re, s5_c_im, s5_d, s5_w_glu, s5_b_glu, mla_cq_norm_g, mla_ckv_norm_g, mla_w_uq, mla_w_ukv, mla_qk_norm_g, rw_mu, rw_w0, rw_w1, rw_w2, rw_a0, rw_a1, rw_a2, rw_g1, rw_g2, rw_k_k, rw_k_a, rw_r_k, rw_ln_g, rw_ln_b, w_branch, w_out, router_w, router_bias, exp_w_gate, exp_w_up, exp_w_down):
    b, n_lat, dm = x.shape
    n_ctx = ctx.shape[1]
    depth = w_ada.shape[0]
    assert dm == D_MODEL
    lay = _Layout(b, n_ctx, n_lat)
    t_all = b * lay.n_tot
    tm_big = 2 * lay.t

    g32 = _block_ones(MIX_W, DA_DIM)
    g64 = _block_ones(MIX_W, RW_DIM)
    tabs = _rope_tables(n_ctx, n_lat)
    row = lambda v: v.reshape(1, -1).astype(F32)
    bf = lambda a: a.astype(BF16)

    cc = jnp.zeros((16, dm), F32).at[:b].set(c).at[b].set(c_ctx)
    mod_all = _ada_call(cc, w_ada, b_ada)
    x_all = jnp.concatenate([ctx, x], axis=1).reshape(t_all, dm)

    wr_hi = router_w.T.astype(BF16)
    wr_lo = (router_w.T - wr_hi.astype(F32)).astype(BF16)
    r_bias = router_bias.reshape(N_EXPERTS, 1).astype(F32)

    for l in range(depth):
        need_ctx = l < depth - 1
        lambda_init = 0.8 - 0.6 * math.exp(-0.3 * l)
        mod = mod_all[l, :b + 1].reshape(b + 1, 6, dm)
        g_mix = row(norm_mix_g[l])
        da, s5a, s5b, mla, rw = _inproj_call(lay, x_all, mod, g_mix, _mix_weight(w_in[l]))

        log2e = math.log2(math.e)
        gda = jnp.stack([jnp.tile(da_qk_norm_g[l, 0], 2 * DA_HEADS) * (DA_DIM ** -0.5 * log2e), jnp.tile(da_qk_norm_g[l, 1], 2 * DA_HEADS)])
        mla_pad = MLA_HEAD_PAD - MLA_NOPE - MLA_ROPE
        gml = jnp.stack([jnp.tile(jnp.pad(mla_qk_norm_g[l, 0], (0, mla_pad)), MLA_HEADS) * ((MLA_NOPE + MLA_ROPE) ** -0.5 * log2e),
                         jnp.tile(jnp.pad(mla_qk_norm_g[l, 1], (0, mla_pad)), MLA_HEADS)])
        wuq = bf(jnp.pad(_pad_heads(mla_w_uq[l], MLA_HEADS, MLA_NOPE + MLA_ROPE, 0, MLA_NOPE + MLA_ROPE), ((0, 64), (0, 0))))
        wuk = bf(_pad_heads(mla_w_ukv[l], MLA_HEADS, MLA_NOPE + MLA_VDIM, 0, MLA_NOPE))
        wuv = bf(_pad_heads(mla_w_ukv[l], MLA_HEADS, MLA_NOPE + MLA_VDIM, MLA_NOPE, MLA_NOPE + MLA_VDIM))
        consts = (g32, g64, gda.astype(F32), gml.astype(F32), row(jnp.pad(mla_cq_norm_g[l], (0, 64))), row(mla_ckv_norm_g[l]),
                  wuq, wuk, wuv)
        qd, kdt, vd, qm, kmt, vm = _qkprep_call(lay, da, mla, tabs, consts)

        lam32 = da_lambda[l].astype(F32)
        lmbda = (jnp.exp(jnp.sum(lam32[0] * lam32[1])) - jnp.exp(jnp.sum(lam32[2] * lam32[3])) + lambda_init).reshape(1, 1)
        subln = row(jnp.tile(da_subln_g[l], DA_HEADS) * (1.0 - lambda_init))
        ya = _attention(lay, qd, kdt, vd, (lmbda, subln, g64), True, need_ctx, "diff_attn")
        yc = _attention(lay, qm, kmt, vm, (lmbda, subln, g64), False, need_ctx, "mla_attn")

        mats = _s5_mats(s5_lam_re[l], s5_lam_im[l], s5_log_dt[l], s5_b_re[l], s5_b_im[l], s5_c_re[l], s5_c_im[l])
        ys_a, ys_b = _s5_scan(lay, s5a, s5b, mats)
        yb = _s5_glu_call(s5a, s5b, ys_a, ys_b, row(s5_d[l]), bf(s5_w_glu[l]), row(s5_b_glu[l]), tm_big)

        pre_consts = (row(rw_mu[l]), g64, row(rw_k_k[l]), row(rw_k_a[l]), row(rw_r_k[l]),
                      rw_w0[l].reshape(2, 1, MIX_W), bf(rw_w1[l]), bf(rw_w2[l]),
                      rw_a0[l].reshape(2, 1, MIX_W), bf(rw_a1[l]), bf(rw_a2[l]), bf(rw_g1[l]), bf(rw_g2[l]))
        yd = _rwkv_branch(lay, rw, pre_consts, g64, row(rw_ln_g[l]), row(rw_ln_b[l]))

        x_mid = _merge_call(lay, need_ctx, x_all, mod, g_mix, bf(w_in[l][:, 2384:]), ya, yb, yc, yd,
                            bf(w_branch[l]), bf(w_out[l]))
        f, comb_t, gid = _router_call(lay, need_ctx, x_mid, mod, row(norm_ffn_g[l]), wr_hi, wr_lo, r_bias)
        x_all = _moe_call(lay, need_ctx, f, comb_t.T, gid, bf(exp_w_gate[l]), bf(exp_w_up[l]), bf(exp_w_down[l]), x_mid, mod)
    return x_all.reshape(b, n_lat, dm)
```

```python
import functools
import math

import numpy as np
import jax
import jax.numpy as jnp
from jax import lax
from jax.experimental import pallas as pl
from jax.experimental.pallas import tpu as pltpu

F32 = jnp.float32
BF16 = jnp.bfloat16

D_MODEL = 1024
GRID_W = 64
ROPE_BASE = 10000.0
EPS = 1e-6
DA_HEADS, DA_DIM, DA_VDIM = 4, 32, 64
S5_GROUPS, S5_CH, S5_STATE = 16, 16, 64
MLA_HEADS, MLA_NOPE, MLA_ROPE, MLA_VDIM = 4, 32, 16, 64
MLA_Q_RANK, MLA_KV_RANK = 192, 128
MLA_HEAD_PAD = 64
RW_HEADS, RW_DIM = 4, 64
RW_LN_EPS = 64e-5
N_BRANCH = 4
N_EXPERTS, N_GROUPS, EXPERTS_PER_GROUP = 16, 4, 4
D_FF = 512
MIX_W = 256

S5_CHUNK = 8
S5_FLAT = S5_CHUNK * MIX_W
S5_STATE_W = S5_GROUPS * S5_STATE
RW_CHUNK = 16
RW_TILE = 128
RW_PREP_TILES = 2
_NT = (((1,), (1,)), ((), ()))
TOKEN_TILE = 256
PROJ_TILE = 768
MOE_TILE = 1152
MOE_BLOCK = 384
MOE_OVER = 128

_DA_W, _S5_W, _MLA_W, _RW_W = 768, 256, 640, 1024
_MIX_COLS = _DA_W + _S5_W + _MLA_W + _RW_W

V7X_VMEM_BYTES = 64 * 2**20
_VMEM_LIMIT = V7X_VMEM_BYTES - 8 * 2**20


def _cparams(*sem):
    return pltpu.CompilerParams(dimension_semantics=sem, vmem_limit_bytes=_VMEM_LIMIT)


def _full(a):
    return pl.BlockSpec(a.shape, lambda *_, nd=a.ndim: (0,) * nd)


def _split_dot(x, w, terms=2):
    acc = None
    rem = x
    for i in range(terms):
        part = rem.astype(BF16)
        d = jnp.dot(part, w, preferred_element_type=F32)
        acc = d if acc is None else acc + d
        if i + 1 < terms:
            rem = rem - part.astype(F32)
    return acc


def _split_dot_rhs(w, x):
    hi = x.astype(BF16)
    lo = (x - hi.astype(F32)).astype(BF16)
    return jnp.dot(w, hi, preferred_element_type=F32) + jnp.dot(w, lo, preferred_element_type=F32)


def _modulate(x, g, shift, scale):
    xn = x * lax.rsqrt(jnp.mean(x * x, axis=-1, keepdims=True) + EPS)
    return xn * g * (1.0 + scale) + shift


def _sigmoid(x):
    return 1.0 / (1.0 + jnp.exp(-x))


def _group_rms(x, ones_bd, inv_n, gain):
    ms = _split_dot(x * x, ones_bd) * inv_n
    return x * lax.rsqrt(ms + EPS) * gain


def _lane_partner(x, half, period, first_end):
    n = x.shape[1]
    lane = lax.broadcasted_iota(jnp.int32, x.shape, 1)
    up = pltpu.roll(x, n - half, axis=1)
    down = pltpu.roll(x, half, axis=1)
    return jnp.where((lane & (period - 1)) < first_end, up, down)


def _rope(x, cos_t, sin_t, half, period, first_end):
    return x * cos_t + _lane_partner(x, half, period, first_end) * sin_t


def _ada_kernel(c_ref, w_ref, b_ref, o_ref):
    c = c_ref[...]
    s = c * _sigmoid(c)
    o_ref[0] = jnp.dot(s.astype(BF16), w_ref[0].astype(BF16), preferred_element_type=F32) + b_ref[0]


def _ada_call(cc, w_ada, b_ada):
    depth, dm, n = w_ada.shape
    tn = n // 4
    return pl.pallas_call(
        _ada_kernel,
        grid=(depth, n // tn),
        in_specs=[
            pl.BlockSpec(cc.shape, lambda l, j: (0, 0)),
            pl.BlockSpec((1, dm, tn), lambda l, j: (l, 0, j)),
            pl.BlockSpec((1, 1, tn), lambda l, j: (l, 0, j)),
        ],
        out_specs=pl.BlockSpec((1, cc.shape[0], tn), lambda l, j: (l, 0, j)),
        out_shape=jax.ShapeDtypeStruct((depth, cc.shape[0], n), F32),
        compiler_params=_cparams("parallel", "parallel"),
        name="ada_mod",
    )(cc, w_ada, b_ada.reshape(depth, 1, n))


class _Layout:
    def __init__(self, n_batch, n_ctx, n_lat):
        t = TOKEN_TILE
        assert n_ctx % t == 0 and n_lat % t == 0
        self.b, self.n_ctx, self.n_lat, self.n_tot = n_batch, n_ctx, n_lat, n_ctx + n_lat
        self.t = t
        self.ctx_tiles, self.lat_tiles, self.seq_tiles = n_ctx // t, n_lat // t, (n_ctx + n_lat) // t

    def rows(self, with_ctx):
        return self.b * (self.n_tot if with_ctx else self.n_lat)

    def n_tiles(self, with_ctx):
        return self.b * (self.seq_tiles if with_ctx else self.lat_tiles)

    def src_tile(self, with_ctx):
        if with_ctx:
            return lambda i: i
        return lambda i: (i // self.lat_tiles) * self.seq_tiles + i % self.lat_tiles + self.ctx_tiles


def _mod_row(modb_ref, modc_ref, r, n_rows, ctx_rows, tiles_per_seq):
    per_batch = modb_ref[0, r:r + 1, :]
    if not ctx_rows:
        return per_batch
    row = lax.broadcasted_iota(jnp.int32, (n_rows, 1), 0)
    first = pl.program_id(0) % tiles_per_seq == 0
    return jnp.where(jnp.logical_and(first, row < ctx_rows), modc_ref[0, r:r + 1, :], per_batch)


def _inproj_kernel(x_ref, modb_ref, modc_ref, g_ref, w_ref, da_ref, s5a_ref, s5b_ref, mla_ref, rw_ref,
                   *, ctx_rows, tiles_per_seq):
    x = x_ref[...]
    mrow = lambda r: _mod_row(modb_ref, modc_ref, r, x.shape[0], ctx_rows, tiles_per_seq)
    h = _modulate(x, g_ref[...], mrow(0), mrow(1))
    acc = jnp.dot(h.astype(BF16), w_ref[...], preferred_element_type=F32)
    da_ref[...] = acc[:, 0:_DA_W]
    s5a_ref[...] = acc[:, _DA_W:_DA_W + _S5_W // 2]
    s5b_ref[...] = acc[:, _DA_W + _S5_W // 2:_DA_W + _S5_W]
    mla_ref[...] = acc[:, _DA_W + _S5_W:_DA_W + _S5_W + _MLA_W]
    rw_ref[...] = acc[:, _DA_W + _S5_W + _MLA_W:_MIX_COLS]


def _seq_tile(lay):
    tm = PROJ_TILE if lay.n_tot % PROJ_TILE == 0 and lay.n_ctx <= PROJ_TILE else lay.t
    return tm, lay.n_tot // tm


def _inproj_call(lay, x_all, mod, g, w_mix):
    t = x_all.shape[0]
    tm, tps = _seq_tile(lay)
    widths = (_DA_W, _S5_W // 2, _S5_W // 2, _MLA_W, _RW_W)
    return pl.pallas_call(
        functools.partial(_inproj_kernel, ctx_rows=lay.n_ctx, tiles_per_seq=tps),
        grid=(t // tm,),
        in_specs=[
            pl.BlockSpec((tm, D_MODEL), lambda i: (i, 0)),
            pl.BlockSpec((1, 6, D_MODEL), lambda i: (i // tps, 0, 0)),
            pl.BlockSpec((1, 6, D_MODEL), lambda i: (lay.b, 0, 0)),
            _full(g), _full(w_mix),
        ],
        out_specs=[pl.BlockSpec((tm, w), lambda i: (i, 0)) for w in widths],
        out_shape=[jax.ShapeDtypeStruct((t, w), F32) for w in widths],
        compiler_params=_cparams("parallel"),
        name="in_proj",
    )(x_all, mod, mod, g, w_mix)


def _qkprep_kernel(da_ref, mla_ref, cda_ref, sda_ref, cml_ref, sml_ref, g32_ref, g64_ref,
                   gda_ref, gml_ref, cqg_ref, ckvg_ref, wuq_ref, wuk_ref, wuv_ref,
                   qd_ref, kd_ref, vd_ref, qm_ref, km_ref, vm_ref):
    g32 = g32_ref[...]
    g64 = g64_ref[...]
    cda, sda = cda_ref[...], sda_ref[...]
    q = _group_rms(da_ref[:, 0:MIX_W], g32, 1.0 / DA_DIM, gda_ref[0:1, :])
    qd_ref[...] = _rope(q, cda, sda, DA_DIM // 2, DA_DIM, DA_DIM // 2).astype(BF16)
    k = _group_rms(da_ref[:, MIX_W:2 * MIX_W], g32, 1.0 / DA_DIM, gda_ref[1:2, :])
    kd_ref[0] = _rope(k, cda, sda, DA_DIM // 2, DA_DIM, DA_DIM // 2).T.astype(BF16)
    vd_ref[...] = da_ref[:, 2 * MIX_W:3 * MIX_W].astype(BF16)

    cml, sml = cml_ref[...], sml_ref[...]
    cq = mla_ref[:, 0:256]
    cqn = cq * lax.rsqrt(jnp.sum(cq * cq, axis=-1, keepdims=True) * (1.0 / MLA_Q_RANK) + EPS) * cqg_ref[...]
    q = jnp.dot(cqn.astype(BF16), wuq_ref[...], preferred_element_type=F32)
    ckv = mla_ref[:, 256:384]
    ckvn = ckv * lax.rsqrt(jnp.mean(ckv * ckv, axis=-1, keepdims=True) + EPS) * ckvg_ref[...]
    ckvb = ckvn.astype(BF16)
    k = jnp.dot(ckvb, wuk_ref[...], preferred_element_type=F32) + mla_ref[:, 384:640]
    vm_ref[...] = jnp.dot(ckvb, wuv_ref[...], preferred_element_type=F32).astype(BF16)
    inv_n = 1.0 / (MLA_NOPE + MLA_ROPE)
    half = MLA_ROPE // 2
    q = _group_rms(q, g64, inv_n, gml_ref[0:1, :])
    qm_ref[...] = _rope(q, cml, sml, half, MLA_HEAD_PAD, MLA_NOPE + half).astype(BF16)
    k = _group_rms(k, g64, inv_n, gml_ref[1:2, :])
    km_ref[0] = _rope(k, cml, sml, half, MLA_HEAD_PAD, MLA_NOPE + half).T.astype(BF16)


def _qkprep_call(lay, da, mla, tabs, consts):
    t = da.shape[0]
    tm, st = _seq_tile(lay)
    row = pl.BlockSpec((tm, MIX_W), lambda i: (i, 0))
    key_t = pl.BlockSpec((1, MIX_W, tm), lambda i: (i // st, 0, i % st))
    in_specs = [pl.BlockSpec((tm, _DA_W), lambda i: (i, 0)), pl.BlockSpec((tm, _MLA_W), lambda i: (i, 0))]
    in_specs += [pl.BlockSpec((tm, MIX_W), lambda i: (i % st, 0)) for _ in tabs]
    in_specs += [_full(a) for a in consts]
    tok = jax.ShapeDtypeStruct((t, MIX_W), BF16)
    keys = jax.ShapeDtypeStruct((lay.b, MIX_W, lay.n_tot), BF16)
    return pl.pallas_call(
        _qkprep_kernel,
        grid=(t // tm,),
        in_specs=in_specs,
        out_specs=[row, key_t, row, row, key_t, row],
        out_shape=[tok, keys, tok, tok, keys, tok],
        compiler_params=_cparams("parallel"),
        name="qk_prep",
    )(da, mla, *tabs, *consts)


def _softmax_parts(s):
    p = jnp.exp2(s - jnp.max(s, axis=-1, keepdims=True))
    return p, 1.0 / jnp.sum(p, axis=-1, keepdims=True)


def _attn_heads(q, kt_ref, v_ref, nk, diff, lam):
    lane = lax.broadcasted_iota(jnp.int32, (q.shape[0], MIX_W), 1)
    v = v_ref[0, 0:nk, :]
    acc = jnp.zeros((q.shape[0], MIX_W), F32)
    dk = DA_DIM if diff else MLA_HEAD_PAD
    per_head = 2 if diff else 1

    def scores(h):
        return [jnp.dot(q[:, e * dk:(e + 1) * dk], kt_ref[0, e * dk:(e + 1) * dk, 0:nk], preferred_element_type=F32)
                for e in range(per_head * h, per_head * (h + 1))]

    ahead = scores(0)
    for h in range(DA_HEADS):
        s = ahead
        if h + 1 < DA_HEADS:
            ahead = scores(h + 1)
        if diff:
            p0, r0 = _softmax_parts(s[0])
            p1, r1 = _softmax_parts(s[1])
            o = jnp.dot((p0 * r0 - p1 * (r1 * lam)).astype(BF16), v, preferred_element_type=F32)
        else:
            p, r = _softmax_parts(s[0])
            o = jnp.dot(p.astype(BF16), v, preferred_element_type=F32) * r
        in_head = jnp.logical_and(lane >= h * DA_VDIM, lane < (h + 1) * DA_VDIM)
        acc = jnp.where(in_head, o, acc)
    return acc


def _attn_kernel(q_ref, kt_ref, v_ref, lam_ref, gain_ref, g64_ref, o_ref, *, diff, n_ctx, n_tot, ctx_tiles):
    q = q_ref[...]
    lam = lam_ref[...]

    def run(nk):
        o = _attn_heads(q, kt_ref, v_ref, nk, diff, lam)
        if diff:
            o = _group_rms(o, g64_ref[...], 1.0 / DA_VDIM, gain_ref[...])
        o_ref[...] = o.astype(BF16)

    if ctx_tiles:
        is_ctx = pl.program_id(1) < ctx_tiles
        pl.when(is_ctx)(lambda: run(n_ctx))
        pl.when(jnp.logical_not(is_ctx))(lambda: run(n_tot))
    else:
        run(n_tot)


def _attention(lay, q, kt, v, extra, diff, with_ctx, name):
    tq = lay.t
    tiles = lay.seq_tiles if with_ctx else lay.lat_tiles
    off = 0 if with_ctx else lay.ctx_tiles
    v3 = v.reshape(lay.b, lay.n_tot, MIX_W)
    kern = functools.partial(_attn_kernel, diff=diff, n_ctx=lay.n_ctx, n_tot=lay.n_tot,
                             ctx_tiles=lay.ctx_tiles if with_ctx else 0)
    return pl.pallas_call(
        kern,
        grid=(lay.b, tiles),
        in_specs=[
            pl.BlockSpec((tq, MIX_W), lambda b, j: (b * lay.seq_tiles + j + off, 0)),
            pl.BlockSpec((1, MIX_W, lay.n_tot), lambda b, j: (b, 0, 0)),
            pl.BlockSpec((1, lay.n_tot, MIX_W), lambda b, j: (b, 0, 0)),
        ] + [_full(a) for a in extra],
        out_specs=pl.BlockSpec((tq, MIX_W), lambda b, j: (b * tiles + j, 0)),
        out_shape=jax.ShapeDtypeStruct((lay.rows(with_ctx), MIX_W), BF16),
        compiler_params=_cparams("parallel", "parallel"),
        name=name,
    )(q, kt, v3, *extra)


def _chunk_rows(ua_ref, ub_ref):
    n = ua_ref.shape[0] // S5_CHUNK
    parts = []
    for s in range(S5_CHUNK):
        rows = pl.ds(s, n, stride=S5_CHUNK)
        parts += [ua_ref[rows, :], ub_ref[rows, :]]
    return jnp.concatenate(parts, axis=1).astype(BF16)


def _s5_proj_kernel(ua_ref, ub_ref, bre_ref, bim_ref, sre_ref, sim_ref):
    u = _chunk_rows(ua_ref, ub_ref)
    sre_ref[0] = jnp.dot(u, bre_ref[0], preferred_element_type=F32)
    sim_ref[0] = jnp.dot(u, bim_ref[0], preferred_element_type=F32)


def _s5_rec_kernel(sre_ref, sim_ref, are_ref, aim_ref, hre_ref, him_ref, *, n_batch, n_chunks, ctx_chunks):
    rev = pl.program_id(0) == 1
    ar, ai = are_ref[0], aim_ref[0]
    sre, sim, hre, him = sre_ref.at[0], sim_ref.at[0], hre_ref.at[0], him_ref.at[0]

    def step(i, carry):
        hr, hi = carry
        k_rev = jnp.where(i < ctx_chunks, ctx_chunks - 1 - i, n_chunks - 1 + ctx_chunks - i)
        k = jnp.where(rev, k_rev, i)
        rows = pl.ds(k, n_batch, stride=n_chunks)
        hre[rows, :] = hr
        him[rows, :] = hi
        return ar * hr - ai * hi + sre[rows, :], ar * hi + ai * hr + sim[rows, :]

    zero = jnp.zeros((n_batch, 128), F32)
    lax.fori_loop(0, n_chunks, step, (zero, zero), unroll=2)


def _s5_out_kernel(ua_ref, ub_ref, hre_ref, him_ref, m_ref, cre_ref, cim_ref, ya_ref, yb_ref):
    y = jnp.dot(_chunk_rows(ua_ref, ub_ref), m_ref[0], preferred_element_type=F32)
    y = y + _split_dot(hre_ref[0], cre_ref[0]) + _split_dot(him_ref[0], cim_ref[0])
    n = y.shape[0]
    ya, yb = ya_ref.at[0], yb_ref.at[0]
    for s in range(S5_CHUNK):
        rows = pl.ds(s, n, stride=S5_CHUNK)
        ya[rows, :] = y[:, s * MIX_W:s * MIX_W + 128]
        yb[rows, :] = y[:, s * MIX_W + 128:(s + 1) * MIX_W]


def _s5_mats(lam_re, lam_im, log_dt, b_re, b_im, c_re, c_im):
    hp = lax.Precision.HIGHEST
    L, G, P, CH = S5_CHUNK, S5_GROUPS, S5_STATE, S5_CH
    lr, li = lam_re.astype(F32), lam_im.astype(F32)
    dt = jnp.exp(log_dt.astype(F32))[..., None]
    zr, zi = lr * dt, li * dt
    j = jnp.arange(L + 1, dtype=F32)[:, None, None, None]
    mag = jnp.exp(zr[None] * j)
    pw_re, pw_im = mag * jnp.cos(zi[None] * j), mag * jnp.sin(zi[None] * j)
    nr, ni = pw_re[1] - 1.0, pw_im[1]
    den = lr * lr + li * li
    cr, ci = (nr * lr + ni * li) / den, (ni * lr - nr * li) / den
    bre, bim = b_re.astype(F32), b_im.astype(F32)
    bb_re = cr[..., None] * bre - ci[..., None] * bim
    bb_im = cr[..., None] * bim + ci[..., None] * bre
    x_re = pw_re[..., None] * bb_re[None] - pw_im[..., None] * bb_im[None]
    x_im = pw_re[..., None] * bb_im[None] + pw_im[..., None] * bb_re[None]
    cre, cim = c_re.astype(F32), c_im.astype(F32)
    kern = (jnp.einsum('dgcp,jdgpe->dgjce', cre, x_re[:L], precision=hp)
            - jnp.einsum('dgcp,jdgpe->dgjce', cim, x_im[:L], precision=hp))
    def spread_mask(a, b):
        spread = jnp.asarray(np.tile(np.eye(b, dtype=np.float32), (1, G)))
        mask = jnp.asarray(np.kron(np.eye(G, dtype=np.float32), np.ones((a, b), np.float32)))
        return spread, mask

    kt = kern.transpose(0, 2, 1, 4, 3)
    xt_re, xt_im = x_re.transpose(1, 0, 2, 4, 3), x_im.transpose(1, 0, 2, 4, 3)
    pwt_re, pwt_im = pw_re.transpose(1, 0, 2, 3)[:, :, :, :, None], pw_im.transpose(1, 0, 2, 3)[:, :, :, :, None]
    cret, cimt = cre.transpose(0, 1, 3, 2)[:, None], cim.transpose(0, 1, 3, 2)[:, None]
    ca_re, ca_im = cret * pwt_re - cimt * pwt_im, -(cret * pwt_im + cimt * pwt_re)
    s_idx, t_idx = np.arange(L)[:, None], np.arange(L)[None, :]
    k_st, xb_re, xb_im, cq_re, cq_im = [], [], [], [], []
    for d in range(2):
        lag = (t_idx - s_idx) if d == 0 else (s_idx - t_idx)
        k_st.append(jnp.where(jnp.asarray(lag >= 0)[:, :, None, None, None], kt[d][np.clip(lag, 0, L - 1)], 0.0))
        pw = np.arange(L - 1, -1, -1) if d == 0 else np.arange(L)
        xb_re.append(xt_re[d][pw])
        xb_im.append(xt_im[d][pw])
        q = np.arange(1, L + 1) if d == 0 else np.arange(L, 0, -1)
        cq_re.append(ca_re[d][q])
        cq_im.append(ca_im[d][q])
    sp, mk = spread_mask(CH, CH)
    m = jnp.einsum('dstrb,bc->dsrtc', jnp.stack(k_st).reshape(2, L, L, G * CH, CH), sp, precision=hp) * mk[:, None, :]
    m = m.astype(BF16).reshape(2, L * G * CH, L * G * CH)
    sp, mk = spread_mask(CH, P)
    to_b = lambda x: (jnp.einsum('dsrb,bc->dsrc', jnp.stack(x).reshape(2, L, G * CH, P), sp, precision=hp) * mk
                      ).astype(BF16).reshape(2, L * G * CH, G * P)
    sp_c, mk_c = spread_mask(P, CH)
    to_c = lambda x: (jnp.einsum('dtrb,bc->drtc', jnp.stack(x).reshape(2, L, G * P, CH), sp_c, precision=hp)
                      * mk_c[:, None, :]).astype(BF16).reshape(2, G * P, L * G * CH)
    a_re, a_im = pw_re[L].reshape(2, 1, G * P), pw_im[L].reshape(2, 1, G * P)
    return m, to_b(xb_re), to_b(xb_im), to_c(cq_re), to_c(cq_im), a_re, a_im


def _s5_scan(lay, ua, ub, mats):
    m, b_r, b_i, c_r, c_i, a_re, a_im = mats
    n_chunks = lay.n_tot // S5_CHUNK
    rows = lay.b * n_chunks
    tr = min(lay.t, rows)
    tok = tr * S5_CHUNK
    half = MIX_W // 2
    wspec = lambda a: pl.BlockSpec((1,) + a.shape[1:], lambda d, i: (d, 0, 0))
    state = jax.ShapeDtypeStruct((2, rows, S5_STATE_W), F32)
    sblk = pl.BlockSpec((1, tr, S5_STATE_W), lambda d, i: (d, i, 0))
    ublk = pl.BlockSpec((tok, half), lambda d, i: (i, 0))
    s_re, s_im = pl.pallas_call(
        _s5_proj_kernel,
        grid=(2, rows // tr),
        in_specs=[ublk, ublk, wspec(b_r), wspec(b_i)],
        out_specs=[sblk, sblk],
        out_shape=[state, state],
        compiler_params=_cparams("parallel", "parallel"),
        name="s5_proj",
    )(ua, ub, b_r, b_i)
    col = pl.BlockSpec((1, rows, 128), lambda d, j: (d, 0, j))
    acol = pl.BlockSpec((1, 1, 128), lambda d, j: (d, 0, j))
    h_re, h_im = pl.pallas_call(
        functools.partial(_s5_rec_kernel, n_batch=lay.b, n_chunks=n_chunks, ctx_chunks=lay.n_ctx // S5_CHUNK),
        grid=(2, S5_STATE_W // 128),
        in_specs=[col, col, acol, acol],
        out_specs=[col, col],
        out_shape=[state, state],
        compiler_params=_cparams("parallel", "parallel"),
        name="s5_rec",
    )(s_re, s_im, a_re, a_im)
    yblk = pl.BlockSpec((1, tok, half), lambda d, i: (d, i, 0))
    yshape = jax.ShapeDtypeStruct((2, lay.b * lay.n_tot, half), F32)
    return pl.pallas_call(
        _s5_out_kernel,
        grid=(2, rows // tr),
        in_specs=[ublk, ublk, sblk, sblk, wspec(m), wspec(c_r), wspec(c_i)],
        out_specs=[yblk, yblk],
        out_shape=[yshape, yshape],
        compiler_params=_cparams("parallel", "parallel"),
        name="s5_out",
    )(ua, ub, h_re, h_im, m, c_r, c_i)


def _s5_glu_kernel(ua_ref, ub_ref, ya_ref, yb_ref, d_ref, w_ref, b_ref, o_ref):
    u = jnp.concatenate([ua_ref[...], ub_ref[...]], axis=1)
    y = d_ref[...] * u + jnp.concatenate([ya_ref[0] + ya_ref[1], yb_ref[0] + yb_ref[1]], axis=1)
    z = 0.5 * y * (1.0 + jnp.tanh(math.sqrt(2.0 / math.pi) * (y + 0.044715 * (y * y * y))))
    gate = _sigmoid(jnp.dot(z.astype(BF16), w_ref[...], preferred_element_type=F32) + b_ref[...])
    o_ref[...] = (z * gate).astype(BF16)


def _s5_glu_call(ua, ub, ya, yb, d, w, bias, tm):
    t, half = ua.shape
    urow = pl.BlockSpec((tm, half), lambda i: (i, 0))
    yrow = pl.BlockSpec((2, tm, half), lambda i: (0, i, 0))
    return pl.pallas_call(
        _s5_glu_kernel,
        grid=(t // tm,),
        in_specs=[urow, urow, yrow, yrow, _full(d), _full(w), _full(bias)],
        out_specs=pl.BlockSpec((tm, MIX_W), lambda i: (i, 0)),
        out_shape=jax.ShapeDtypeStruct((t, MIX_W), BF16),
        compiler_params=_cparams("parallel"),
        name="s5_glu",
    )(ua, ub, ya, yb, d, w, bias)


def _rw_pre_kernel(x_ref, prev_ref, next_ref, mu_ref, g64_ref, kk_g_ref, ka_ref, rk_ref,
                   w0_ref, w1_ref, w2_ref, a0_ref, a1_ref, a2_ref, g1_ref, g2_ref,
                   r_ref, v_ref, kk_ref, lw_ref, kka_ref, km_ref, bon_ref, gate_ref,
                   *, seq_tiles, ctx_tiles):
    x = x_ref[...]
    n = x.shape[0]
    j = pl.program_id(0) % seq_tiles
    starts = jnp.logical_or(j == 0, j == ctx_tiles)
    ends = jnp.logical_or(j == ctx_tiles - 1, j == seq_tiles - 1)
    prev_row = jnp.where(starts, 0.0, prev_ref[0, 7:8, :])
    next_row = jnp.where(ends, 0.0, next_ref[0, 0:1, :])
    row = lax.broadcasted_iota(jnp.int32, x.shape, 0)
    left = jnp.where(row == 0, prev_row, pltpu.roll(x, 1, axis=0))
    right = jnp.where(row == n - 1, next_row, pltpu.roll(x, n - 1, axis=0))
    x = x + (0.5 * (left + right) - x) * mu_ref[...]
    r, k, v, xd = (x[:, i * MIX_W:(i + 1) * MIX_W] for i in range(4))
    g64 = g64_ref[...]
    kscaled = k * kk_g_ref[...]
    kk = kscaled / jnp.maximum(jnp.sqrt(_split_dot(kscaled * kscaled, g64)), 1e-12)
    xdb = xd.astype(BF16)
    r_ref[...] = r
    v_ref[...] = v
    kk_ref[...] = kk
    km_sum = None
    for d in range(2):
        lo = jnp.tanh(jnp.dot(xdb, w1_ref[d], preferred_element_type=F32))
        w_raw = w0_ref[d] + jnp.dot(lo.astype(BF16), w2_ref[d], preferred_element_type=F32)
        lw_ref[d] = -_sigmoid(w_raw) * math.exp(-0.5)
        ar = jnp.dot(xdb, a1_ref[d], preferred_element_type=F32)
        a = _sigmoid(a0_ref[d] + jnp.dot(ar.astype(BF16), a2_ref[d], preferred_element_type=F32))
        km = k * (1.0 + (a - 1.0) * ka_ref[...])
        kka_ref[d] = kk * a
        km_ref[d] = km
        km_sum = km if km_sum is None else km_sum + km
    bon_ref[...] = _split_dot(r * km_sum * rk_ref[...], g64) * v
    gr = _sigmoid(jnp.dot(xdb, g1_ref[...], preferred_element_type=F32))
    gate_ref[...] = jnp.dot(gr.astype(BF16), g2_ref[...], preferred_element_type=F32)


def _rw_pre_call(lay, rw, consts):
    t, tr = rw.shape[0], lay.t
    nt = t // tr
    g8 = tr // 8
    rw8 = rw.reshape(t // 8, 8, _RW_W)
    row = pl.BlockSpec((tr, MIX_W), lambda i: (i, 0))
    row2 = pl.BlockSpec((2, tr, MIX_W), lambda i: (0, i, 0))
    sd = jax.ShapeDtypeStruct((t, MIX_W), F32)
    sd2 = jax.ShapeDtypeStruct((2, t, MIX_W), F32)
    return pl.pallas_call(
        functools.partial(_rw_pre_kernel, seq_tiles=lay.seq_tiles, ctx_tiles=lay.ctx_tiles),
        grid=(nt,),
        in_specs=[pl.BlockSpec((tr, _RW_W), lambda i: (i, 0)),
                  pl.BlockSpec((1, 8, _RW_W), lambda i: (jnp.maximum(i * g8 - 1, 0), 0, 0)),
                  pl.BlockSpec((1, 8, _RW_W), lambda i: (jnp.minimum((i + 1) * g8, t // 8 - 1), 0, 0))]
                 + [_full(a) for a in consts],
        out_specs=[row, row, row, row2, row2, row2, row, row],
        out_shape=[sd, sd, sd, sd2, sd2, sd2, sd, sd],
        compiler_params=_cparams("parallel"),
        name="rwkv_pre",
    )(rw, rw8, rw8, *consts)


def _head_masks(shape, lane_axis, seg):
    lane = lax.broadcasted_iota(jnp.int32, shape, lane_axis)
    return [jnp.logical_and(lane >= h * seg, lane < (h + 1) * seg) for h in range(RW_HEADS)]


def _rw_prep_kernel(*refs, rev):
    tiles = [_rw_prep_tile(sub, *refs, rev=rev) for sub in range(RW_PREP_TILES)]
    while tiles:
        tiles = [t for t in tiles if next(t, None) is not None]


def _rw_prep_tile(sub, r_ref, kk_ref, v_ref, lw_ref, ka_ref, km_ref, perm_ref, permt_ref, g_ref, eye_ref,
                  br_ref, ck_ref, uvt_ref, y0_ref, pc_ref, *, rev):
    C, NC = RW_CHUNK, RW_TILE // RW_CHUNK
    perm, permt, g64, eye4 = perm_ref[...], permt_ref[...], g_ref[...], eye_ref[...]
    tok = slice(sub * RW_TILE, (sub + 1) * RW_TILE)
    nat = jnp.concatenate([r_ref[0, tok, :], kk_ref[0, tok, :], v_ref[0, tok, :],
                           lw_ref[0, 0, tok, :], ka_ref[0, 0, tok, :], km_ref[0, 0, tok, :]], axis=1)
    hi = nat.astype(BF16)
    lo = (nat - hi.astype(F32)).astype(BF16)
    pm = jnp.dot(perm, hi, preferred_element_type=F32) + jnp.dot(perm, lo, preferred_element_type=F32)
    r, kk, v, lw, ka, km = (pm[:, i * MIX_W:(i + 1) * MIX_W] for i in range(6))
    slab = lambda x, j: x[j * NC:(j + 1) * NC, :]
    order = list(range(C))[::-1] if rev else list(range(C))
    pos = {j: i for i, j in enumerate(order)}
    cum, run = {}, None
    for j in order:
        run = slab(lw, j) if run is None else run + slab(lw, j)
        cum[j] = run
    tot = run
    yield True
    bh, ch, kh, rh, cp, kp, vv = {}, {}, {}, {}, {}, {}, {}
    for j in range(C):
        e_inv, e_end = jnp.exp(-cum[j]), jnp.exp(tot - cum[j])
        bh[j] = -slab(kk, j) * jnp.exp(cum[j] - slab(lw, j))
        ch[j], kh[j] = slab(ka, j) * e_inv, slab(km, j) * e_inv
        rh[j] = slab(r, j) * jnp.exp(cum[j])
        cp[j], kp[j] = slab(ka, j) * e_end, slab(km, j) * e_end
        vv[j] = slab(v, j)
    strict = [(t, s) for t in order for s in order if pos[s] < pos[t]]
    incl = [(t, s) for t in order for s in order if pos[s] <= pos[t]]
    def head_dots(lhs, rhs, pairs):
        prods = jnp.concatenate([lhs[t] * rhs[s] for t, s in pairs], axis=0).astype(BF16)
        gram = jnp.dot(prods, g64, preferred_element_type=F32)
        return {p: gram[i * NC:(i + 1) * NC, :] for i, p in enumerate(pairs)}

    yield True
    acb = head_dots(bh, ch, strict)
    yield True
    akb = head_dots(bh, kh, strict)
    yield True
    mcr = head_dots(rh, ch, incl)
    yield True
    mkr = head_dots(rh, kh, incl)
    yield True
    bt, u0 = {}, {}
    for t in order:
        b_acc, u_acc = bh[t], jnp.zeros_like(bh[t])
        for s in order:
            if pos[s] < pos[t]:
                b_acc = b_acc + acb[(t, s)] * bt[s]
                u_acc = u_acc + akb[(t, s)] * vv[s] + acb[(t, s)] * u0[s]
        bt[t], u0[t] = b_acc, u_acc
        yield True
    rt, y0 = {}, {}
    for t in order:
        r_acc, y_acc = rh[t], jnp.zeros_like(rh[t])
        for s in order:
            if pos[s] <= pos[t]:
                r_acc = r_acc + mcr[(t, s)] * bt[s]
                y_acc = y_acc + mcr[(t, s)] * u0[s] + mkr[(t, s)] * vv[s]
        rt[t], y0[t] = r_acc, y_acc
        yield True
    stackp = lambda dct: jnp.concatenate([dct[j] for j in range(C)], axis=0)
    b16 = lambda x: x.astype(BF16)
    y0p, u0p = stackp(y0), stackp(u0)
    y0h, u0h = b16(y0p), b16(u0p)
    cat = jnp.concatenate([b16(stackp(bt)), b16(stackp(rt)), b16(stackp(cp)), b16(stackp(kp)),
                           y0h, b16(y0p - y0h.astype(F32)), u0h, b16(u0p - u0h.astype(F32)), b16(stackp(vv))], axis=1)
    natural = jnp.dot(permt, cat, preferred_element_type=F32)
    seg = lambda i: natural[:, i * MIX_W:(i + 1) * MIX_W]
    yield True
    btn, rtn, cpn, kpn = b16(seg(0)), b16(seg(1)), b16(seg(2)), b16(seg(3))
    y0_ref[0, tok, :] = seg(4) + seg(5)
    u0h_n, u0l_n, vn = b16(seg(6)), b16(seg(7)), b16(seg(8))
    hm = _head_masks((C, MIX_W), 1, RW_DIM)
    zero = jnp.zeros((C, MIX_W), BF16)
    zh, zl = [], []
    for c in range(NC):
        rows = slice(c * C, (c + 1) * C)
        br_ref[0, sub * NC + c, 0:C, :] = btn[rows]
        br_ref[0, sub * NC + c, C:2 * C, :] = rtn[rows]
        ck_ref[0, sub * NC + c, 0:C, :] = cpn[rows]
        ck_ref[0, sub * NC + c, C:2 * C, :] = kpn[rows]
        for h in range(RW_HEADS):
            zh += [jnp.where(hm[h], u0h_n[rows], zero), jnp.where(hm[h], vn[rows], zero)]
            zl += [jnp.where(hm[h], u0l_n[rows], zero), zero]
    uvt = (lax.dot_general(eye4, jnp.concatenate(zh, axis=0), _NT, preferred_element_type=F32)
           + lax.dot_general(eye4, jnp.concatenate(zl, axis=0), _NT, preferred_element_type=F32))
    for c in range(NC):
        uvt_ref[0, sub * NC + c] = uvt[:, c * 2 * C * RW_HEADS:(c + 1) * 2 * C * RW_HEADS]
    pc_ref[0, sub * NC:(sub + 1) * NC, :] = jnp.exp(tot)


def _rw_prep_call(lay, shared, perdir, consts, rev):
    b, n_tot, tt = lay.b, lay.n_tot, RW_TILE * RW_PREP_TILES
    assert n_tot % tt == 0
    nck = n_tot // RW_CHUNK
    cpt = tt // RW_CHUNK
    d = 1 if rev else 0
    sh = [a.reshape(b, n_tot, MIX_W) for a in shared]
    pd = [a.reshape(2, b, n_tot, MIX_W) for a in perdir]
    tok = pl.BlockSpec((1, tt, MIX_W), lambda i, j: (i, j, 0))
    tok_d = pl.BlockSpec((1, 1, tt, MIX_W), lambda i, j: (d, i, j, 0))
    rows32 = pl.BlockSpec((1, cpt, 2 * RW_CHUNK, MIX_W), lambda i, j: (i, j, 0, 0))
    return pl.pallas_call(
        functools.partial(_rw_prep_kernel, rev=rev),
        grid=(b, n_tot // tt),
        in_specs=[tok] * 3 + [tok_d] * 3 + [_full(a) for a in consts],
        out_specs=[rows32, rows32,
                   pl.BlockSpec((1, cpt, RW_DIM, 2 * RW_CHUNK * RW_HEADS), lambda i, j: (i, j, 0, 0)),
                   tok,
                   pl.BlockSpec((1, cpt, MIX_W), lambda i, j: (i, j, 0))],
        out_shape=[jax.ShapeDtypeStruct((b, nck, 2 * RW_CHUNK, MIX_W), BF16),
                   jax.ShapeDtypeStruct((b, nck, 2 * RW_CHUNK, MIX_W), BF16),
                   jax.ShapeDtypeStruct((b, nck, RW_DIM, 2 * RW_CHUNK * RW_HEADS), F32),
                   jax.ShapeDtypeStruct((b, n_tot, MIX_W), F32),
                   jax.ShapeDtypeStruct((b, nck, MIX_W), F32)],
        compiler_params=_cparams("parallel", "parallel"),
        name="rwkv_prep_rev" if rev else "rwkv_prep_fwd",
    )(*sh, *pd, *consts)


def _rw_scan_kernel(brf, ckf, uvtf, pcf, brr, ckr, uvtr, pcr, ytf_ref, ytr_ref, s_scr, *, n_batch):
    @pl.when(pl.program_id(0) == 0)
    def _():
        s_scr[...] = jnp.zeros_like(s_scr)

    cpt = RW_TILE // RW_CHUNK
    hm = _head_masks((2 * RW_CHUNK, MIX_W), 1, RW_DIM)
    lane = lax.broadcasted_iota(jnp.int32, (RW_DIM, 2 * RW_CHUNK * RW_HEADS), 1)
    is_u = (lane & (2 * RW_CHUNK - 1)) < RW_CHUNK
    per_head = lambda x: jnp.concatenate([jnp.where(m, x, jnp.zeros_like(x)) for m in hm], axis=0)

    def refs_of(p, c):
        d, b = divmod(p, n_batch)
        refs = (brf, ckf, uvtf, pcf, ytf_ref) if d == 0 else (brr, ckr, uvtr, pcr, ytr_ref)
        return refs, b, (c if d == 0 else cpt - 1 - c)

    def step(c, carry):
        lhs = []
        for p in range(2 * n_batch):
            (br_ref, _, uvt_ref, _, yt_ref), b, cc = refs_of(p, c)
            s = s_scr[p]
            shi = s.astype(BF16)
            slo = (s - shi.astype(F32)).astype(BF16)
            w2 = lax.dot_general(jnp.concatenate([shi, slo], axis=0), per_head(br_ref[b, cc]), _NT, preferred_element_type=F32)
            w = w2[:RW_DIM] + w2[RW_DIM:]
            yt_ref[b, cc] = w
            uvt = uvt_ref[b, cc]
            lhs.append(jnp.where(is_u, w + uvt, uvt).astype(BF16))
        for p in range(2 * n_batch):
            (_, ck_ref, _, pc_ref, _), b, cc = refs_of(p, c)
            s_scr[p] = (s_scr[p] * pc_ref[b, pl.ds(cc, 1), :]
                        + jnp.dot(lhs[p], per_head(ck_ref[b, cc]), preferred_element_type=F32))
        return carry

    lax.fori_loop(0, cpt, step, 0)


def _rw_scan_call(lay, fwd, rev):
    b, n_tot, tt = lay.b, lay.n_tot, RW_TILE
    assert lay.n_ctx % tt == 0 and lay.n_lat % tt == 0
    nt, ct = n_tot // tt, lay.n_ctx // tt
    cpt = tt // RW_CHUNK
    rev_tile = lambda i: jnp.where(i < ct, ct - 1 - i, nt - 1 + ct - i)

    def specs(tile):
        return [pl.BlockSpec((b, cpt, 2 * RW_CHUNK, MIX_W), lambda i: (0, tile(i), 0, 0)),
                pl.BlockSpec((b, cpt, 2 * RW_CHUNK, MIX_W), lambda i: (0, tile(i), 0, 0)),
                pl.BlockSpec((b, cpt, RW_DIM, 2 * RW_CHUNK * RW_HEADS), lambda i: (0, tile(i), 0, 0)),
                pl.BlockSpec((b, cpt, MIX_W), lambda i: (0, tile(i), 0))]

    ident = lambda i: i
    yt = jax.ShapeDtypeStruct((b, n_tot // RW_CHUNK, RW_DIM, 2 * RW_CHUNK * RW_HEADS), F32)
    return pl.pallas_call(
        functools.partial(_rw_scan_kernel, n_batch=b),
        grid=(nt,),
        in_specs=specs(ident) + specs(rev_tile),
        out_specs=[specs(ident)[2], specs(rev_tile)[2]],
        out_shape=[yt, yt],
        scratch_shapes=[pltpu.VMEM((2 * b, RW_DIM, MIX_W), F32)],
        compiler_params=_cparams("arbitrary"),
        name="rwkv_scan",
    )(*fwd, *rev)


def _rw_fin_kernel(ytf_ref, ytr_ref, y0f_ref, y0r_ref, bon_ref, gate_ref, asel_ref, g64_ref, lng_ref, lnb_ref, o_ref):
    cpt = RW_TILE // RW_CHUNK
    asel = asel_ref[...]
    width = cpt * 2 * RW_CHUNK * RW_HEADS
    lane = lax.broadcasted_iota(jnp.int32, (RW_DIM, width), 1)
    lane_head = jnp.bitwise_and(jnp.right_shift(lane, 5), RW_HEADS - 1)

    def base(yt_ref, sub):
        yt = jnp.concatenate([yt_ref[0, sub * cpt + c] for c in range(cpt)], axis=1).astype(BF16)
        rows = jnp.concatenate([jnp.where(lane_head == h, yt, jnp.zeros_like(yt)) for h in range(RW_HEADS)], axis=0)
        return lax.dot_general(asel, rows, _NT, preferred_element_type=F32)

    g64 = g64_ref[...]
    bases = [(base(ytf_ref, sub), base(ytr_ref, sub)) for sub in range(RW_PREP_TILES)]
    for sub, (yb_f, yb_r) in enumerate(bases):
        tok = slice(sub * RW_TILE, (sub + 1) * RW_TILE)
        y = yb_f + y0f_ref[0, tok, :] + yb_r + y0r_ref[0, tok, :]
        mean = _split_dot(y, g64) * (1.0 / RW_DIM)
        c = y - mean
        var = _split_dot(c * c, g64) * (1.0 / RW_DIM)
        out = c * lax.rsqrt(var + RW_LN_EPS) * lng_ref[...] + lnb_ref[...] + bon_ref[tok, :]
        o_ref[tok, :] = (out * gate_ref[tok, :]).astype(BF16)


def _rw_fin_call(lay, ytf, ytr, y0f, y0r, bon, gate, consts):
    b, n_tot, tt = lay.b, lay.n_tot, RW_TILE * RW_PREP_TILES
    nt = n_tot // tt
    cpt = tt // RW_CHUNK
    ytb = pl.BlockSpec((1, cpt, RW_DIM, 2 * RW_CHUNK * RW_HEADS), lambda i, j: (i, j, 0, 0))
    y0b = pl.BlockSpec((1, tt, MIX_W), lambda i, j: (i, j, 0))
    row = pl.BlockSpec((tt, MIX_W), lambda i, j: (i * nt + j, 0))
    return pl.pallas_call(
        _rw_fin_kernel,
        grid=(b, nt),
        in_specs=[ytb, ytb, y0b, y0b, row, row] + [_full(a) for a in consts],
        out_specs=row,
        out_shape=jax.ShapeDtypeStruct((b * n_tot, MIX_W), BF16),
        compiler_params=_cparams("parallel", "parallel"),
        name="rwkv_finish",
    )(ytf, ytr, y0f, y0r, bon, gate, *consts)


def _rw_constants():
    c, nc = RW_CHUNK, RW_TILE // RW_CHUNK
    perm = np.zeros((RW_TILE, RW_TILE), np.float32)
    for ci in range(nc):
        for j in range(c):
            perm[j * nc + ci, ci * c + j] = 1.0
    lane = np.arange(MIX_W) % RW_DIM
    eye4 = (lane[None, :] == np.arange(RW_DIM)[:, None]).astype(np.float32)
    lanes = np.arange(nc * 2 * c * RW_HEADS)
    lane_chunk, lane_tok = lanes // (2 * c * RW_HEADS), lanes % (2 * c)
    t = np.arange(RW_TILE)
    asel = ((lane_chunk[None, :] == (t // c)[:, None]) & (lane_tok[None, :] == (c + t % c)[:, None])).astype(np.float32)
    as16 = lambda a: jnp.asarray(a, BF16)
    return as16(perm), as16(perm.T), as16(eye4), as16(asel)


def _rwkv_branch(lay, rw, pre_consts, g64, lng, lnb):
    r_, v_, kk_, lw_, kka_, km_, bon, gate = _rw_pre_call(lay, rw, pre_consts)
    perm, permt, eye4, asel = _rw_constants()
    prep_consts = (perm, permt, g64, eye4)
    fwd = _rw_prep_call(lay, (r_, kk_, v_), (lw_, kka_, km_), prep_consts, False)
    rev = _rw_prep_call(lay, (r_, kk_, v_), (lw_, kka_, km_), prep_consts, True)
    pick = lambda o: (o[0], o[1], o[2], o[4])
    ytf, ytr = _rw_scan_call(lay, pick(fwd), pick(rev))
    return _rw_fin_call(lay, ytf, ytr, fwd[3], rev[3], bon, gate, (asel, g64, lng, lnb))


def _merge_kernel(x_ref, modb_ref, modc_ref, g_ref, wg_ref, ya_ref, yb_ref, yc_ref, yd_ref, wb_ref, wo_ref, o_ref,
                  *, ctx_rows, tiles_per_seq):
    x = x_ref[...]
    mrow = lambda r: _mod_row(modb_ref, modc_ref, r, x.shape[0], ctx_rows, tiles_per_seq)
    h = _modulate(x, g_ref[...], mrow(0), mrow(1)).astype(BF16)
    merged = None
    for i, y_ref in enumerate((ya_ref, yb_ref, yc_ref, yd_ref)):
        gate = _sigmoid(jnp.dot(h, wg_ref[:, i * D_MODEL:(i + 1) * D_MODEL], preferred_element_type=F32))
        term = gate * jnp.dot(y_ref[...], wb_ref[i], preferred_element_type=F32)
        merged = term if merged is None else merged + term
    out = jnp.dot(merged.astype(BF16), wo_ref[...], preferred_element_type=F32)
    o_ref[...] = x + mrow(2) * out


def _merge_call(lay, with_ctx, x_all, mod, g, w_gate, ya, yb, yc, yd, w_branch, w_out):
    if with_ctx:
        tm, tps = _seq_tile(lay)
        src, n_tiles, ctx_rows = (lambda i: i), lay.b * tps, lay.n_ctx
    else:
        tm, tps = lay.t, lay.lat_tiles
        src, n_tiles, ctx_rows = lay.src_tile(False), lay.n_tiles(False), 0
    full_row = lambda w: pl.BlockSpec((tm, w), lambda i: (src(i), 0))
    out_row = lambda w: pl.BlockSpec((tm, w), lambda i: (i, 0))
    return pl.pallas_call(
        functools.partial(_merge_kernel, ctx_rows=ctx_rows, tiles_per_seq=tps),
        grid=(n_tiles,),
        in_specs=[full_row(D_MODEL), pl.BlockSpec((1, 6, D_MODEL), lambda i: (i // tps, 0, 0)),
                  pl.BlockSpec((1, 6, D_MODEL), lambda i: (lay.b, 0, 0)), _full(g), _full(w_gate),
                  out_row(MIX_W), full_row(MIX_W), out_row(MIX_W), full_row(MIX_W), _full(w_branch), _full(w_out)],
        out_specs=out_row(D_MODEL),
        out_shape=jax.ShapeDtypeStruct((lay.rows(with_ctx), D_MODEL), F32),
        compiler_params=_cparams("parallel"),
        name="merge_out",
    )(x_all, mod, mod, g, w_gate, ya, yb, yc, yd, w_branch, w_out)


def _router_kernel(x_ref, modb_ref, modc_ref, g_ref, wh_ref, wl_ref, bias_ref, f_ref, comb_ref, gid_ref,
                   *, ctx_rows, tiles_per_seq):
    x = x_ref[...]
    mrow = lambda r: _mod_row(modb_ref, modc_ref, r, x.shape[0], ctx_rows, tiles_per_seq)
    f = _modulate(x, g_ref[...], mrow(3), mrow(4))
    fh = f.astype(BF16)
    f_ref[...] = fh
    fl = (f - fh.astype(F32)).astype(BF16)
    nt = (((1,), (1,)), ((), ()))
    wh, wl = wh_ref[...], wl_ref[...]
    logits = (lax.dot_general(wh, fh, nt, preferred_element_type=F32)
              + lax.dot_general(wh, fl, nt, preferred_element_type=F32)
              + lax.dot_general(wl, fh, nt, preferred_element_type=F32))
    scores = _sigmoid(logits)
    biased = scores + bias_ref[...]
    sc = [scores[e:e + 1, :] for e in range(N_EXPERTS)]
    bi = [biased[e:e + 1, :] for e in range(N_EXPERTS)]
    group_score = []
    for g in range(N_GROUPS):
        a, b, c, d = bi[4 * g:4 * g + 4]
        m1, n1, m2, n2 = jnp.maximum(a, b), jnp.minimum(a, b), jnp.maximum(c, d), jnp.minimum(c, d)
        group_score.append(jnp.maximum(m1, m2) + jnp.maximum(jnp.minimum(m1, m2), jnp.maximum(n1, n2)))

    def first_argmax(vals):
        top = functools.reduce(jnp.maximum, vals)
        seen, hot = None, []
        for v in vals:
            h = v == top
            if seen is not None:
                h = jnp.logical_and(h, jnp.logical_not(seen))
            seen = h if seen is None else jnp.logical_or(seen, h)
            hot.append(h)
        return hot

    in_group = first_argmax(group_score)
    masked = [jnp.where(in_group[e // EXPERTS_PER_GROUP], bi[e], -jnp.inf) for e in range(N_EXPERTS)]
    hot1 = first_argmax(masked)
    hot2 = first_argmax([jnp.where(h, -jnp.inf, v) for h, v in zip(hot1, masked)])
    w1 = functools.reduce(jnp.add, [jnp.where(h, s, 0.0) for h, s in zip(hot1, sc)])
    w2 = functools.reduce(jnp.add, [jnp.where(h, s, 0.0) for h, s in zip(hot2, sc)])
    inv_tot = 1.0 / (w1 + w2)
    for e in range(N_EXPERTS):
        comb_ref[e:e + 1, :] = (jnp.where(hot1[e], w1, 0.0) + jnp.where(hot2[e], w2, 0.0)) * inv_tot
    gid_ref[...] = functools.reduce(jnp.add, [jnp.where(in_group[g], g, 0) for g in range(1, N_GROUPS)])


def _router_call(lay, with_ctx, x, mod, g, wh, wl, bias):
    t = x.shape[0]
    if with_ctx:
        (tm, tps), ctx_rows = _seq_tile(lay), lay.n_ctx
    else:
        tm, tps, ctx_rows = 2 * lay.t, lay.lat_tiles // 2, 0
        assert lay.lat_tiles % 2 == 0
    return pl.pallas_call(
        functools.partial(_router_kernel, ctx_rows=ctx_rows, tiles_per_seq=tps),
        grid=(t // tm,),
        in_specs=[pl.BlockSpec((tm, D_MODEL), lambda i: (i, 0)),
                  pl.BlockSpec((1, 6, D_MODEL), lambda i: (i // tps, 0, 0)),
                  pl.BlockSpec((1, 6, D_MODEL), lambda i: (lay.b, 0, 0)), _full(g), _full(wh), _full(wl), _full(bias)],
        out_specs=[pl.BlockSpec((tm, D_MODEL), lambda i: (i, 0)), pl.BlockSpec((N_EXPERTS, tm), lambda i: (0, i)),
                   pl.BlockSpec((1, tm), lambda i: (0, i))],
        out_shape=[jax.ShapeDtypeStruct((t, D_MODEL), BF16), jax.ShapeDtypeStruct((N_EXPERTS, t), F32),
                   jax.ShapeDtypeStruct((1, t), jnp.int32)],
        compiler_params=_cparams("parallel"),
        name="moe_router",
    )(x, mod, mod, g, wh, wl, bias)


def _moe_plan(gid, n_tiles, tm):
    g = gid.reshape(n_tiles, tm)
    onehot = (g[..., None] == jnp.arange(N_GROUPS, dtype=jnp.int32)).astype(jnp.int32)
    rank = jnp.cumsum(onehot, axis=1) - onehot
    counts = jnp.sum(onehot, axis=1)
    padded = (counts + 15) // 16 * 16
    offs = jnp.cumsum(padded, axis=1) - padded
    pos = jnp.sum(onehot * (offs[:, None, :] + rank), axis=-1)
    n_over = (jnp.maximum(padded - MOE_BLOCK, 0) + MOE_OVER - 1) // MOE_OVER
    return pos.astype(jnp.int32), offs.astype(jnp.int32), n_over.astype(jnp.int32)


def _moe_kernel(offs_ref, nover_ref, f_ref, posr_ref, posc_ref, comb_ref, wg_ref, wu_ref, wd_ref, x_ref, modb_ref,
                modc_ref, o_ref, xs_scr, cs_scr, ys_scr, *, ctx_rows, tiles_per_seq):
    i, e = pl.program_id(0), pl.program_id(1)
    n_slots, tm = xs_scr.shape[0], f_ref.shape[0]
    n_live = min(n_slots, -(-(tm + 16 * N_GROUPS) // 256) * 256)

    @pl.when(e == 0)
    def _():
        slot = lax.broadcasted_iota(jnp.int32, (n_live, tm), 0)
        place = (slot == posr_ref[0]).astype(BF16)
        xs_scr[0:n_live, :] = jnp.dot(place, f_ref[...], preferred_element_type=F32).astype(BF16)
        xs_scr[n_live:n_slots, :] = jnp.zeros((n_slots - n_live, D_MODEL), BF16)
        cs_scr[0:n_live, :] = _split_dot_rhs(place, comb_ref[...])
        cs_scr[n_live:n_slots, :] = jnp.zeros((n_slots - n_live, N_EXPERTS), F32)
        ys_scr[...] = jnp.zeros_like(ys_scr)

    grp = lax.shift_right_logical(e, 2)
    start = offs_ref[i, grp]
    lane = lax.broadcasted_iota(jnp.int32, (1, N_EXPERTS), 1)

    def run(rows):
        xb = xs_scr[rows, :]
        gate = jnp.dot(xb, wg_ref[0], preferred_element_type=F32)
        up = jnp.dot(xb, wu_ref[0], preferred_element_type=F32)
        act = (gate * _sigmoid(gate) * up).astype(BF16)
        down = jnp.dot(act, wd_ref[0], preferred_element_type=F32)
        c_e = jnp.sum(jnp.where(lane == e, cs_scr[rows, :], 0.0), axis=1, keepdims=True)
        ys_scr[rows, :] += c_e * down

    run(pl.ds(pl.multiple_of(start, 16), MOE_BLOCK))

    def overflow(k, carry):
        run(pl.ds(pl.multiple_of(start + MOE_BLOCK + k * MOE_OVER, 16), MOE_OVER))
        return carry

    lax.fori_loop(0, nover_ref[i, grp], overflow, 0)

    @pl.when(e == N_EXPERTS - 1)
    def _():
        slot = lax.broadcasted_iota(jnp.int32, (tm, n_live), 1)
        fetch = (slot == posc_ref[...]).astype(BF16)
        y = jnp.dot(fetch, ys_scr[0:n_live, :].astype(BF16), preferred_element_type=F32)
        res_gate = modb_ref[0, 5:6, :]
        if ctx_rows:
            row = lax.broadcasted_iota(jnp.int32, y.shape, 0)
            first = i % tiles_per_seq == 0
            res_gate = jnp.where(jnp.logical_and(first, row < ctx_rows), modc_ref[0, 5:6, :], res_gate)
        o_ref[...] = x_ref[...] + res_gate * y


def _moe_call(lay, with_ctx, f, comb, gid, wg, wu, wd, x, mod):
    t = f.shape[0]
    seq = lay.n_tot if with_ctx else lay.n_lat
    tm = MOE_TILE if seq % MOE_TILE == 0 else math.gcd(seq, 1024)
    tps, n_tiles = seq // tm, t // tm
    ctx_rows = lay.n_ctx if with_ctx else 0
    assert ctx_rows <= tm
    n_slots = -(-(tm + 16 * N_GROUPS + MOE_BLOCK + MOE_OVER) // 256) * 256
    pos, offs, n_over = _moe_plan(gid, n_tiles, tm)
    wspec = lambda a: pl.BlockSpec((1,) + a.shape[1:], lambda i, e, *_: (e, 0, 0))
    tok = lambda w: pl.BlockSpec((tm, w), lambda i, e, *_: (i, 0))
    grid_spec = pltpu.PrefetchScalarGridSpec(
        num_scalar_prefetch=2,
        grid=(n_tiles, N_EXPERTS),
        in_specs=[tok(D_MODEL), pl.BlockSpec((1, 1, tm), lambda i, e, *_: (i, 0, 0)), tok(1), tok(N_EXPERTS),
                  wspec(wg), wspec(wu), wspec(wd), tok(D_MODEL),
                  pl.BlockSpec((1, 6, D_MODEL), lambda i, e, *_: (i // tps, 0, 0)),
                  pl.BlockSpec((1, 6, D_MODEL), lambda i, e, *_: (lay.b, 0, 0))],
        out_specs=tok(D_MODEL),
        scratch_shapes=[pltpu.VMEM((n_slots, D_MODEL), BF16), pltpu.VMEM((n_slots, N_EXPERTS), F32),
                        pltpu.VMEM((n_slots, D_MODEL), F32)])
    return pl.pallas_call(
        functools.partial(_moe_kernel, ctx_rows=ctx_rows, tiles_per_seq=tps),
        grid_spec=grid_spec,
        out_shape=jax.ShapeDtypeStruct((t, D_MODEL), F32),
        compiler_params=_cparams("parallel", "arbitrary"),
        name="moe_experts",
    )(offs, n_over, f, pos.reshape(n_tiles, 1, tm), pos.reshape(t, 1), comb, wg, wu, wd, x, mod, mod)


def _block_ones(n, group):
    i = np.arange(n) // group
    return jnp.asarray(i[:, None] == i[None, :], dtype=BF16)


def _rope_tables(n_ctx, n_lat):
    rows = n_lat // GRID_W
    row = jnp.repeat(jnp.arange(rows, dtype=F32), GRID_W)
    col = jnp.tile(jnp.arange(GRID_W, dtype=F32), rows)

    def angles(rot_dim):
        n_freq = rot_dim // 4
        inv_freq = ROPE_BASE ** (-jnp.arange(n_freq, dtype=F32) / n_freq)
        ang = jnp.concatenate([row[:, None] * inv_freq, col[:, None] * inv_freq], axis=-1)
        return jnp.cos(ang), jnp.sin(ang)

    c, s = angles(DA_DIM)
    cda = jnp.tile(jnp.concatenate([c, c], -1), (1, 2 * DA_HEADS))
    sda = jnp.tile(jnp.concatenate([-s, s], -1), (1, 2 * DA_HEADS))
    c, s = angles(MLA_ROPE)
    one = jnp.ones((n_lat, MLA_NOPE), F32)
    pad = MLA_HEAD_PAD - MLA_NOPE - MLA_ROPE
    cml = jnp.tile(jnp.concatenate([one, c, c, jnp.ones((n_lat, pad), F32)], -1), (1, MLA_HEADS))
    sml = jnp.tile(jnp.concatenate([0 * one, -s, s, jnp.zeros((n_lat, pad), F32)], -1), (1, MLA_HEADS))
    ident = lambda t, v: jnp.concatenate([jnp.full((n_ctx, MIX_W), v, F32), t], axis=0)
    return ident(cda, 1.0), ident(sda, 0.0), ident(cml, 1.0), ident(sml, 0.0)


def _pad_heads(w, n_heads, src_w, lo, hi, dst_w=MLA_HEAD_PAD):
    w = w.reshape(w.shape[0], n_heads, src_w)[:, :, lo:hi]
    w = jnp.pad(w, ((0, 0), (0, 0), (0, dst_w - (hi - lo))))
    return w.reshape(w.shape[0], n_heads * dst_w)


def _mix_weight(w_in_l):
    w = w_in_l
    kr = w[:, 1344:1360]
    z = lambda n: jnp.zeros((D_MODEL, n), w.dtype)
    kr_wide = jnp.concatenate([jnp.concatenate([z(MLA_NOPE), kr, z(MLA_HEAD_PAD - MLA_NOPE - MLA_ROPE)], 1)] * MLA_HEADS, 1)
    return jnp.concatenate([w[:, 0:1024], w[:, 1024:1216], z(64), w[:, 1216:1344], kr_wide, w[:, 1360:2384]], axis=1).astype(BF16)


def kernel(x, c, ctx, c_ctx, w_ada, b_ada, norm_mix_g, norm_ffn_g, w_in, da_qk_norm_g, da_lambda, da_subln_g, s5_lam_re, s5_lam_im, s5_log_dt, s5_b_re, s5_b_im, s5_c_re, s5_c_im, s5_d, s5_w_glu, s5_b_glu, mla_cq_norm_g, mla_ckv_norm_g, mla_w_uq, mla_w_ukv, mla_qk_norm_g, rw_mu, rw_w0, rw_w1, rw_w2, rw_a0, rw_a1, rw_a2, rw_g1, rw_g2, rw_k_k, rw_k_a, rw_r_k, rw_ln_g, rw_ln_b, w_branch, w_out, router_w, router_bias, exp_w_gate, exp_w_up, exp_w_down):
    b, n_lat, dm = x.shape
    n_ctx = ctx.shape[1]
    depth = w_ada.shape[0]
    assert dm == D_MODEL
    lay = _Layout(b, n_ctx, n_lat)
    t_all = b * lay.n_tot
    tm_big = 2 * lay.t

    g32 = _block_ones(MIX_W, DA_DIM)
    g64 = _block_ones(MIX_W, RW_DIM)
    tabs = _rope_tables(n_ctx, n_lat)
    row = lambda v: v.reshape(1, -1).astype(F32)
    bf = lambda a: a.astype(BF16)

    cc = jnp.zeros((16, dm), F32).at[:b].set(c).at[b].set(c_ctx)
    mod_all = _ada_call(cc, w_ada, b_ada)
    x_all = jnp.concatenate([ctx, x], axis=1).reshape(t_all, dm)

    wr_hi = router_w.T.astype(BF16)
    wr_lo = (router_w.T - wr_hi.astype(F32)).astype(BF16)
    r_bias = router_bias.reshape(N_EXPERTS, 1).astype(F32)

    for l in range(depth):
        need_ctx = l < depth - 1
        lambda_init = 0.8 - 0.6 * math.exp(-0.3 * l)
        mod = mod_all[l, :b + 1].reshape(b + 1, 6, dm)
        g_mix = row(norm_mix_g[l])
        da, s5a, s5b, mla, rw = _inproj_call(lay, x_all, mod, g_mix, _mix_weight(w_in[l]))

        log2e = math.log2(math.e)
        gda = jnp.stack([jnp.tile(da_qk_norm_g[l, 0], 2 * DA_HEADS) * (DA_DIM ** -0.5 * log2e), jnp.tile(da_qk_norm_g[l, 1], 2 * DA_HEADS)])
        mla_pad = MLA_HEAD_PAD - MLA_NOPE - MLA_ROPE
        gml = jnp.stack([jnp.tile(jnp.pad(mla_qk_norm_g[l, 0], (0, mla_pad)), MLA_HEADS) * ((MLA_NOPE + MLA_ROPE) ** -0.5 * log2e),
                         jnp.tile(jnp.pad(mla_qk_norm_g[l, 1], (0, mla_pad)), MLA_HEADS)])
        wuq = bf(jnp.pad(_pad_heads(mla_w_uq[l], MLA_HEADS, MLA_NOPE + MLA_ROPE, 0, MLA_NOPE + MLA_ROPE), ((0, 64), (0, 0))))
        wuk = bf(_pad_heads(mla_w_ukv[l], MLA_HEADS, MLA_NOPE + MLA_VDIM, 0, MLA_NOPE))
        wuv = bf(_pad_heads(mla_w_ukv[l], MLA_HEADS, MLA_NOPE + MLA_VDIM, MLA_NOPE, MLA_NOPE + MLA_VDIM))
        consts = (g32, g64, gda.astype(F32), gml.astype(F32), row(jnp.pad(mla_cq_norm_g[l], (0, 64))), row(mla_ckv_norm_g[l]),
                  wuq, wuk, wuv)
        qd, kdt, vd, qm, kmt, vm = _qkprep_call(lay, da, mla, tabs, consts)

        lam32 = da_lambda[l].astype(F32)
        lmbda = (jnp.exp(jnp.sum(lam32[0] * lam32[1])) - jnp.exp(jnp.sum(lam32[2] * lam32[3])) + lambda_init).reshape(1, 1)
        subln = row(jnp.tile(da_subln_g[l], DA_HEADS) * (1.0 - lambda_init))
        ya = _attention(lay, qd, kdt, vd, (lmbda, subln, g64), True, need_ctx, "diff_attn")
        yc = _attention(lay, qm, kmt, vm, (lmbda, subln, g64), False, need_ctx, "mla_attn")

        mats = _s5_mats(s5_lam_re[l], s5_lam_im[l], s5_log_dt[l], s5_b_re[l], s5_b_im[l], s5_c_re[l], s5_c_im[l])
        ys_a, ys_b = _s5_scan(lay, s5a, s5b, mats)
        yb = _s5_glu_call(s5a, s5b, ys_a, ys_b, row(s5_d[l]), bf(s5_w_glu[l]), row(s5_b_glu[l]), tm_big)

        pre_consts = (row(rw_mu[l]), g64, row(rw_k_k[l]), row(rw_k_a[l]), row(rw_r_k[l]),
                      rw_w0[l].reshape(2, 1, MIX_W), bf(rw_w1[l]), bf(rw_w2[l]),
                      rw_a0[l].reshape(2, 1, MIX_W), bf(rw_a1[l]), bf(rw_a2[l]), bf(rw_g1[l]), bf(rw_g2[l]))
        yd = _rwkv_branch(lay, rw, pre_consts, g64, row(rw_ln_g[l]), row(rw_ln_b[l]))

        x_mid = _merge_call(lay, need_ctx, x_all, mod, g_mix, bf(w_in[l][:, 2384:]), ya, yb, yc, yd,
                            bf(w_branch[l]), bf(w_out[l]))
        f, comb_t, gid = _router_call(lay, need_ctx, x_mid, mod, row(norm_ffn_g[l]), wr_hi, wr_lo, r_bias)
        x_all = _moe_call(lay, need_ctx, f, comb_t.T, gid, bf(exp_w_gate[l]), bf(exp_w_up[l]), bf(exp_w_down[l]), x_mid, mod)
    return x_all.reshape(b, n_lat, dm)
```

```python
import functools
import math

import numpy as np
import jax
import jax.numpy as jnp
from jax import lax
from jax.experimental import pallas as pl
from jax.experimental.pallas import tpu as pltpu

F32 = jnp.float32
BF16 = jnp.bfloat16

D_MODEL = 1024
GRID_W = 64
ROPE_BASE = 10000.0
EPS = 1e-6
DA_HEADS, DA_DIM, DA_VDIM = 4, 32, 64
S5_GROUPS, S5_CH, S5_STATE = 16, 16, 64
MLA_HEADS, MLA_NOPE, MLA_ROPE, MLA_VDIM = 4, 32, 16, 64
MLA_Q_RANK, MLA_KV_RANK = 192, 128
MLA_HEAD_PAD = 64
RW_HEADS, RW_DIM = 4, 64
RW_LN_EPS = 64e-5
N_BRANCH = 4
N_EXPERTS, N_GROUPS, EXPERTS_PER_GROUP = 16, 4, 4
D_FF = 512
MIX_W = 256

S5_CHUNK = 8
S5_FLAT = S5_CHUNK * MIX_W
S5_STATE_W = S5_GROUPS * S5_STATE
RW_CHUNK = 16
RW_TILE = 128
RW_PREP_TILES = 2
_NT = (((1,), (1,)), ((), ()))
TOKEN_TILE = 256
PROJ_TILE = 768
MOE_TILE = 1152
MOE_BLOCK = 384
MOE_OVER = 128

_DA_W, _S5_W, _MLA_W, _RW_W = 768, 256, 640, 1024
_MIX_COLS = _DA_W + _S5_W + _MLA_W + _RW_W

V7X_VMEM_BYTES = 64 * 2**20
_VMEM_LIMIT = V7X_VMEM_BYTES - 8 * 2**20


def _cparams(*sem):
    return pltpu.CompilerParams(dimension_semantics=sem, vmem_limit_bytes=_VMEM_LIMIT)


def _full(a):
    return pl.BlockSpec(a.shape, lambda *_, nd=a.ndim: (0,) * nd)


def _split_dot(x, w, terms=2):
    acc = None
    rem = x
    for i in range(terms):
        part = rem.astype(BF16)
        d = jnp.dot(part, w, preferred_element_type=F32)
        acc = d if acc is None else acc + d
        if i + 1 < terms:
            rem = rem - part.astype(F32)
    return acc


def _split_dot_rhs(w, x):
    hi = x.astype(BF16)
    lo = (x - hi.astype(F32)).astype(BF16)
    return jnp.dot(w, hi, preferred_element_type=F32) + jnp.dot(w, lo, preferred_element_type=F32)


def _modulate(x, g, shift, scale):
    xn = x * lax.rsqrt(jnp.mean(x * x, axis=-1, keepdims=True) + EPS)
    return xn * g * (1.0 + scale) + shift


def _sigmoid(x):
    return 1.0 / (1.0 + jnp.exp(-x))


def _group_rms(x, ones_bd, inv_n, gain):
    ms = _split_dot(x * x, ones_bd) * inv_n
    return x * lax.rsqrt(ms + EPS) * gain


def _lane_partner(x, half, period, first_end):
    n = x.shape[1]
    lane = lax.broadcasted_iota(jnp.int32, x.shape, 1)
    up = pltpu.roll(x, n - half, axis=1)
    down = pltpu.roll(x, half, axis=1)
    return jnp.where((lane & (period - 1)) < first_end, up, down)


def _rope(x, cos_t, sin_t, half, period, first_end):
    return x * cos_t + _lane_partner(x, half, period, first_end) * sin_t


def _ada_kernel(c_ref, w_ref, b_ref, o_ref):
    c = c_ref[...]
    s = c * _sigmoid(c)
    o_ref[0] = jnp.dot(s.astype(BF16), w_ref[0].astype(BF16), preferred_element_type=F32) + b_ref[0]


def _ada_call(cc, w_ada, b_ada):
    depth, dm, n = w_ada.shape
    tn = n // 4
    return pl.pallas_call(
        _ada_kernel,
        grid=(depth, n // tn),
        in_specs=[
            pl.BlockSpec(cc.shape, lambda l, j: (0, 0)),
            pl.BlockSpec((1, dm, tn), lambda l, j: (l, 0, j)),
            pl.BlockSpec((1, 1, tn), lambda l, j: (l, 0, j)),
        ],
        out_specs=pl.BlockSpec((1, cc.shape[0], tn), lambda l, j: (l, 0, j)),
        out_shape=jax.ShapeDtypeStruct((depth, cc.shape[0], n), F32),
        compiler_params=_cparams("parallel", "parallel"),
        name="ada_mod",
    )(cc, w_ada, b_ada.reshape(depth, 1, n))


class _Layout:
    def __init__(self, n_batch, n_ctx, n_lat):
        t = TOKEN_TILE
        assert n_ctx % t == 0 and n_lat % t == 0
        self.b, self.n_ctx, self.n_lat, self.n_tot = n_batch, n_ctx, n_lat, n_ctx + n_lat
        self.t = t
        self.ctx_tiles, self.lat_tiles, self.seq_tiles = n_ctx // t, n_lat // t, (n_ctx + n_lat) // t

    def rows(self, with_ctx):
        return self.b * (self.n_tot if with_ctx else self.n_lat)

    def n_tiles(self, with_ctx):
        return self.b * (self.seq_tiles if with_ctx else self.lat_tiles)

    def src_tile(self, with_ctx):
        if with_ctx:
            return lambda i: i
        return lambda i: (i // self.lat_tiles) * self.seq_tiles + i % self.lat_tiles + self.ctx_tiles


def _mod_row(modb_ref, modc_ref, r, n_rows, ctx_rows, tiles_per_seq):
    per_batch = modb_ref[0, r:r + 1, :]
    if not ctx_rows:
        return per_batch
    row = lax.broadcasted_iota(jnp.int32, (n_rows, 1), 0)
    first = pl.program_id(0) % tiles_per_seq == 0
    return jnp.where(jnp.logical_and(first, row < ctx_rows), modc_ref[0, r:r + 1, :], per_batch)


def _inproj_kernel(x_ref, modb_ref, modc_ref, g_ref, w_ref, da_ref, s5a_ref, s5b_ref, mla_ref, rw_ref,
                   *, ctx_rows, tiles_per_seq):
    x = x_ref[...]
    mrow = lambda r: _mod_row(modb_ref, modc_ref, r, x.shape[0], ctx_rows, tiles_per_seq)
    h = _modulate(x, g_ref[...], mrow(0), mrow(1))
    acc = jnp.dot(h.astype(BF16), w_ref[...], preferred_element_type=F32)
    da_ref[...] = acc[:, 0:_DA_W]
    s5a_ref[...] = acc[:, _DA_W:_DA_W + _S5_W // 2]
    s5b_ref[...] = acc[:, _DA_W + _S5_W // 2:_DA_W + _S5_W]
    mla_ref[...] = acc[:, _DA_W + _S5_W:_DA_W + _S5_W + _MLA_W]
    rw_ref[...] = acc[:, _DA_W + _S5_W + _MLA_W:_MIX_COLS]


def _seq_tile(lay):
    tm = PROJ_TILE if lay.n_tot % PROJ_TILE == 0 and lay.n_ctx <= PROJ_TILE else lay.t
    return tm, lay.n_tot // tm


def _inproj_call(lay, x_all, mod, g, w_mix):
    t = x_all.shape[0]
    tm, tps = _seq_tile(lay)
    widths = (_DA_W, _S5_W // 2, _S5_W // 2, _MLA_W, _RW_W)
    return pl.pallas_call(
        functools.partial(_inproj_kernel, ctx_rows=lay.n_ctx, tiles_per_seq=tps),
        grid=(t // tm,),
        in_specs=[
            pl.BlockSpec((tm, D_MODEL), lambda i: (i, 0)),
            pl.BlockSpec((1, 6, D_MODEL), lambda i: (i // tps, 0, 0)),
            pl.BlockSpec((1, 6, D_MODEL), lambda i: (lay.b, 0, 0)),
            _full(g), _full(w_mix),
        ],
        out_specs=[pl.BlockSpec((tm, w), lambda i: (i, 0)) for w in widths],
        out_shape=[jax.ShapeDtypeStruct((t, w), F32) for w in widths],
        compiler_params=_cparams("parallel"),
        name="in_proj",
    )(x_all, mod, mod, g, w_mix)


def _qkprep_kernel(da_ref, mla_ref, cda_ref, sda_ref, cml_ref, sml_ref, g32_ref, g64_ref,
                   gda_ref, gml_ref, cqg_ref, ckvg_ref, wuq_ref, wuk_ref, wuv_ref,
                   qd_ref, kd_ref, vd_ref, qm_ref, km_ref, vm_ref):
    g32 = g32_ref[...]
    g64 = g64_ref[...]
    cda, sda = cda_ref[...], sda_ref[...]
    q = _group_rms(da_ref[:, 0:MIX_W], g32, 1.0 / DA_DIM, gda_ref[0:1, :])
    qd_ref[...] = _rope(q, cda, sda, DA_DIM // 2, DA_DIM, DA_DIM // 2).astype(BF16)
    k = _group_rms(da_ref[:, MIX_W:2 * MIX_W], g32, 1.0 / DA_DIM, gda_ref[1:2, :])
    kd_ref[0] = _rope(k, cda, sda, DA_DIM // 2, DA_DIM, DA_DIM // 2).T.astype(BF16)
    vd_ref[...] = da_ref[:, 2 * MIX_W:3 * MIX_W].astype(BF16)

    cml, sml = cml_ref[...], sml_ref[...]
    cq = mla_ref[:, 0:256]
    cqn = cq * lax.rsqrt(jnp.sum(cq * cq, axis=-1, keepdims=True) * (1.0 / MLA_Q_RANK) + EPS) * cqg_ref[...]
    q = jnp.dot(cqn.astype(BF16), wuq_ref[...], preferred_element_type=F32)
    ckv = mla_ref[:, 256:384]
    ckvn = ckv * lax.rsqrt(jnp.mean(ckv * ckv, axis=-1, keepdims=True) + EPS) * ckvg_ref[...]
    ckvb = ckvn.astype(BF16)
    k = jnp.dot(ckvb, wuk_ref[...], preferred_element_type=F32) + mla_ref[:, 384:640]
    vm_ref[...] = jnp.dot(ckvb, wuv_ref[...], preferred_element_type=F32).astype(BF16)
    inv_n = 1.0 / (MLA_NOPE + MLA_ROPE)
    half = MLA_ROPE // 2
    q = _group_rms(q, g64, inv_n, gml_ref[0:1, :])
    qm_ref[...] = _rope(q, cml, sml, half, MLA_HEAD_PAD, MLA_NOPE + half).astype(BF16)
    k = _group_rms(k, g64, inv_n, gml_ref[1:2, :])
    km_ref[0] = _rope(k, cml, sml, half, MLA_HEAD_PAD, MLA_NOPE + half).T.astype(BF16)


def _qkprep_call(lay, da, mla, tabs, consts):
    t = da.shape[0]
    tm, st = _seq_tile(lay)
    row = pl.BlockSpec((tm, MIX_W), lambda i: (i, 0))
    key_t = pl.BlockSpec((1, MIX_W, tm), lambda i: (i // st, 0, i % st))
    in_specs = [pl.BlockSpec((tm, _DA_W), lambda i: (i, 0)), pl.BlockSpec((tm, _MLA_W), lambda i: (i, 0))]
    in_specs += [pl.BlockSpec((tm, MIX_W), lambda i: (i % st, 0)) for _ in tabs]
    in_specs += [_full(a) for a in consts]
    tok = jax.ShapeDtypeStruct((t, MIX_W), BF16)
    keys = jax.ShapeDtypeStruct((lay.b, MIX_W, lay.n_tot), BF16)
    return pl.pallas_call(
        _qkprep_kernel,
        grid=(t // tm,),
        in_specs=in_specs,
        out_specs=[row, key_t, row, row, key_t, row],
        out_shape=[tok, keys, tok, tok, keys, tok],
        compiler_params=_cparams("parallel"),
        name="qk_prep",
    )(da, mla, *tabs, *consts)


def _softmax_parts(s):
    p = jnp.exp2(s - jnp.max(s, axis=-1, keepdims=True))
    return p, 1.0 / jnp.sum(p, axis=-1, keepdims=True)


def _attn_heads(q, kt_ref, v_ref, nk, diff, lam):
    lane = lax.broadcasted_iota(jnp.int32, (q.shape[0], MIX_W), 1)
    v = v_ref[0, 0:nk, :]
    acc = jnp.zeros((q.shape[0], MIX_W), F32)
    dk = DA_DIM if diff else MLA_HEAD_PAD
    per_head = 2 if diff else 1

    def scores(h):
        return [jnp.dot(q[:, e * dk:(e + 1) * dk], kt_ref[0, e * dk:(e + 1) * dk, 0:nk], preferred_element_type=F32)
                for e in range(per_head * h, per_head * (h + 1))]

    ahead = scores(0)
    for h in range(DA_HEADS):
        s = ahead
        if h + 1 < DA_HEADS:
            ahead = scores(h + 1)
        if diff:
            p0, r0 = _softmax_parts(s[0])
            p1, r1 = _softmax_parts(s[1])
            o = jnp.dot((p0 * r0 - p1 * (r1 * lam)).astype(BF16), v, preferred_element_type=F32)
        else:
            p, r = _softmax_parts(s[0])
            o = jnp.dot(p.astype(BF16), v, preferred_element_type=F32) * r
        in_head = jnp.logical_and(lane >= h * DA_VDIM, lane < (h + 1) * DA_VDIM)
        acc = jnp.where(in_head, o, acc)
    return acc


def _attn_kernel(q_ref, kt_ref, v_ref, lam_ref, gain_ref, g64_ref, o_ref, *, diff, n_ctx, n_tot, ctx_tiles):
    q = q_ref[...]
    lam = lam_ref[...]

    def run(nk):
        o = _attn_heads(q, kt_ref, v_ref, nk, diff, lam)
        if diff:
            o = _group_rms(o, g64_ref[...], 1.0 / DA_VDIM, gain_ref[...])
        o_ref[...] = o.astype(BF16)

    if ctx_tiles:
        is_ctx = pl.program_id(1) < ctx_tiles
        pl.when(is_ctx)(lambda: run(n_ctx))
        pl.when(jnp.logical_not(is_ctx))(lambda: run(n_tot))
    else:
        run(n_tot)


def _attention(lay, q, kt, v, extra, diff, with_ctx, name):
    tq = lay.t
    tiles = lay.seq_tiles if with_ctx else lay.lat_tiles
    off = 0 if with_ctx else lay.ctx_tiles
    v3 = v.reshape(lay.b, lay.n_tot, MIX_W)
    kern = functools.partial(_attn_kernel, diff=diff, n_ctx=lay.n_ctx, n_tot=lay.n_tot,
                             ctx_tiles=lay.ctx_tiles if with_ctx else 0)
    return pl.pallas_call(
        kern,
        grid=(lay.b, tiles),
        in_specs=[
            pl.BlockSpec((tq, MIX_W), lambda b, j: (b * lay.seq_tiles + j + off, 0)),
            pl.BlockSpec((1, MIX_W, lay.n_tot), lambda b, j: (b, 0, 0)),
            pl.BlockSpec((1, lay.n_tot, MIX_W), lambda b, j: (b, 0, 0)),
        ] + [_full(a) for a in extra],
        out_specs=pl.BlockSpec((tq, MIX_W), lambda b, j: (b * tiles + j, 0)),
        out_shape=jax.ShapeDtypeStruct((lay.rows(with_ctx), MIX_W), BF16),
        compiler_params=_cparams("parallel", "parallel"),
        name=name,
    )(q, kt, v3, *extra)


def _chunk_rows(ua_ref, ub_ref):
    n = ua_ref.shape[0] // S5_CHUNK
    parts = []
    for s in range(S5_CHUNK):
        rows = pl.ds(s, n, stride=S5_CHUNK)
        parts += [ua_ref[rows, :], ub_ref[rows, :]]
    return jnp.concatenate(parts, axis=1).astype(BF16)


def _s5_proj_kernel(ua_ref, ub_ref, bre_ref, bim_ref, sre_ref, sim_ref):
    u = _chunk_rows(ua_ref, ub_ref)
    sre_ref[0] = jnp.dot(u, bre_ref[0, 0], preferred_element_type=F32)
    sim_ref[0] = jnp.dot(u, bim_ref[0, 0], preferred_element_type=F32)


def _s5_rec_kernel(sre_ref, sim_ref, are_ref, aim_ref, hre_ref, him_ref, *, n_batch, n_chunks, ctx_chunks):
    rev = pl.program_id(0) == 1
    ar, ai = are_ref[0, 0], aim_ref[0, 0]
    sre, sim, hre, him = sre_ref.at[0], sim_ref.at[0], hre_ref.at[0], him_ref.at[0]

    def step(i, carry):
        hr, hi = carry
        k_rev = jnp.where(i < ctx_chunks, ctx_chunks - 1 - i, n_chunks - 1 + ctx_chunks - i)
        k = jnp.where(rev, k_rev, i)
        rows = pl.ds(k, n_batch, stride=n_chunks)
        hre[rows, :] = hr
        him[rows, :] = hi
        return ar * hr - ai * hi + sre[rows, :], ar * hi + ai * hr + sim[rows, :]

    zero = jnp.zeros((n_batch, 128), F32)
    lax.fori_loop(0, n_chunks, step, (zero, zero), unroll=2)


def _s5_out_kernel(ua_ref, ub_ref, hre_ref, him_ref, m_ref, cre_ref, cim_ref, ya_ref, yb_ref):
    y = jnp.dot(_chunk_rows(ua_ref, ub_ref), m_ref[0, 0], preferred_element_type=F32)
    y = y + _split_dot(hre_ref[0], cre_ref[0, 0]) + _split_dot(him_ref[0], cim_ref[0, 0])
    n = y.shape[0]
    ya, yb = ya_ref.at[0], yb_ref.at[0]
    for s in range(S5_CHUNK):
        rows = pl.ds(s, n, stride=S5_CHUNK)
        ya[rows, :] = y[:, s * MIX_W:s * MIX_W + 128]
        yb[rows, :] = y[:, s * MIX_W + 128:(s + 1) * MIX_W]


def _s5_mats(lam_re, lam_im, log_dt, b_re, b_im, c_re, c_im):
    hp = lax.Precision.HIGHEST
    L, G, P, CH = S5_CHUNK, S5_GROUPS, S5_STATE, S5_CH
    lr, li = lam_re.astype(F32), lam_im.astype(F32)
    dt = jnp.exp(log_dt.astype(F32))[..., None]
    zr, zi = lr * dt, li * dt
    j = jnp.arange(L + 1, dtype=F32)[:, None, None, None]
    mag = jnp.exp(zr[None] * j)
    pw_re, pw_im = mag * jnp.cos(zi[None] * j), mag * jnp.sin(zi[None] * j)
    nr, ni = pw_re[1] - 1.0, pw_im[1]
    den = lr * lr + li * li
    cr, ci = (nr * lr + ni * li) / den, (ni * lr - nr * li) / den
    bre, bim = b_re.astype(F32), b_im.astype(F32)
    bb_re = cr[..., None] * bre - ci[..., None] * bim
    bb_im = cr[..., None] * bim + ci[..., None] * bre
    x_re = pw_re[..., None] * bb_re[None] - pw_im[..., None] * bb_im[None]
    x_im = pw_re[..., None] * bb_im[None] + pw_im[..., None] * bb_re[None]
    cre, cim = c_re.astype(F32), c_im.astype(F32)
    kern = (jnp.einsum('dgcp,jdgpe->dgjce', cre, x_re[:L], precision=hp)
            - jnp.einsum('dgcp,jdgpe->dgjce', cim, x_im[:L], precision=hp))
    def spread_mask(a, b):
        spread = jnp.asarray(np.tile(np.eye(b, dtype=np.float32), (1, G)))
        mask = jnp.asarray(np.kron(np.eye(G, dtype=np.float32), np.ones((a, b), np.float32)))
        return spread, mask

    kt = kern.transpose(0, 2, 1, 4, 3)
    xt_re, xt_im = x_re.transpose(1, 0, 2, 4, 3), x_im.transpose(1, 0, 2, 4, 3)
    pwt_re, pwt_im = pw_re.transpose(1, 0, 2, 3)[:, :, :, :, None], pw_im.transpose(1, 0, 2, 3)[:, :, :, :, None]
    cret, cimt = cre.transpose(0, 1, 3, 2)[:, None], cim.transpose(0, 1, 3, 2)[:, None]
    ca_re, ca_im = cret * pwt_re - cimt * pwt_im, -(cret * pwt_im + cimt * pwt_re)
    s_idx, t_idx = np.arange(L)[:, None], np.arange(L)[None, :]
    k_st, xb_re, xb_im, cq_re, cq_im = [], [], [], [], []
    for d in range(2):
        lag = (t_idx - s_idx) if d == 0 else (s_idx - t_idx)
        k_st.append(jnp.where(jnp.asarray(lag >= 0)[:, :, None, None, None], kt[d][np.clip(lag, 0, L - 1)], 0.0))
        pw = np.arange(L - 1, -1, -1) if d == 0 else np.arange(L)
        xb_re.append(xt_re[d][pw])
        xb_im.append(xt_im[d][pw])
        q = np.arange(1, L + 1) if d == 0 else np.arange(L, 0, -1)
        cq_re.append(ca_re[d][q])
        cq_im.append(ca_im[d][q])
    sp, mk = spread_mask(CH, CH)
    m = jnp.einsum('dstrb,bc->dsrtc', jnp.stack(k_st).reshape(2, L, L, G * CH, CH), sp, precision=hp) * mk[:, None, :]
    m = m.astype(BF16).reshape(2, L * G * CH, L * G * CH)
    sp, mk = spread_mask(CH, P)
    to_b = lambda x: (jnp.einsum('dsrb,bc->dsrc', jnp.stack(x).reshape(2, L, G * CH, P), sp, precision=hp) * mk
                      ).astype(BF16).reshape(2, L * G * CH, G * P)
    sp_c, mk_c = spread_mask(P, CH)
    to_c = lambda x: (jnp.einsum('dtrb,bc->drtc', jnp.stack(x).reshape(2, L, G * P, CH), sp_c, precision=hp)
                      * mk_c[:, None, :]).astype(BF16).reshape(2, G * P, L * G * CH)
    a_re, a_im = pw_re[L].reshape(2, 1, G * P), pw_im[L].reshape(2, 1, G * P)
    return m, to_b(xb_re), to_b(xb_im), to_c(cq_re), to_c(cq_im), a_re, a_im


def _s5_scan(lay, ua, ub, mats, layer):
    m, b_r, b_i, c_r, c_i, a_re, a_im = mats
    n_chunks = lay.n_tot // S5_CHUNK
    rows = lay.b * n_chunks
    tr = min(lay.t, rows)
    tok = tr * S5_CHUNK
    half = MIX_W // 2
    wspec = lambda a: pl.BlockSpec((1, 1) + a.shape[2:], lambda d, i: (layer, d, 0, 0))
    state = jax.ShapeDtypeStruct((2, rows, S5_STATE_W), F32)
    sblk = pl.BlockSpec((1, tr, S5_STATE_W), lambda d, i: (d, i, 0))
    ublk = pl.BlockSpec((tok, half), lambda d, i: (i, 0))
    s_re, s_im = pl.pallas_call(
        _s5_proj_kernel,
        grid=(2, rows // tr),
        in_specs=[ublk, ublk, wspec(b_r), wspec(b_i)],
        out_specs=[sblk, sblk],
        out_shape=[state, state],
        compiler_params=_cparams("parallel", "parallel"),
        name="s5_proj",
    )(ua, ub, b_r, b_i)
    col = pl.BlockSpec((1, rows, 128), lambda d, j: (d, 0, j))
    acol = pl.BlockSpec((1, 1, 1, 128), lambda d, j: (layer, d, 0, j))
    h_re, h_im = pl.pallas_call(
        functools.partial(_s5_rec_kernel, n_batch=lay.b, n_chunks=n_chunks, ctx_chunks=lay.n_ctx // S5_CHUNK),
        grid=(2, S5_STATE_W // 128),
        in_specs=[col, col, acol, acol],
        out_specs=[col, col],
        out_shape=[state, state],
        compiler_params=_cparams("parallel", "parallel"),
        name="s5_rec",
    )(s_re, s_im, a_re, a_im)
    yblk = pl.BlockSpec((1, tok, half), lambda d, i: (d, i, 0))
    yshape = jax.ShapeDtypeStruct((2, lay.b * lay.n_tot, half), F32)
    return pl.pallas_call(
        _s5_out_kernel,
        grid=(2, rows // tr),
        in_specs=[ublk, ublk, sblk, sblk, wspec(m), wspec(c_r), wspec(c_i)],
        out_specs=[yblk, yblk],
        out_shape=[yshape, yshape],
        compiler_params=_cparams("parallel", "parallel"),
        name="s5_out",
    )(ua, ub, h_re, h_im, m, c_r, c_i)


def _s5_glu_kernel(ua_ref, ub_ref, ya_ref, yb_ref, d_ref, w_ref, b_ref, o_ref):
    u = jnp.concatenate([ua_ref[...], ub_ref[...]], axis=1)
    y = d_ref[...] * u + jnp.concatenate([ya_ref[0] + ya_ref[1], yb_ref[0] + yb_ref[1]], axis=1)
    z = 0.5 * y * (1.0 + jnp.tanh(math.sqrt(2.0 / math.pi) * (y + 0.044715 * (y * y * y))))
    gate = _sigmoid(jnp.dot(z.astype(BF16), w_ref[...], preferred_element_type=F32) + b_ref[...])
    o_ref[...] = (z * gate).astype(BF16)


def _s5_glu_call(ua, ub, ya, yb, d, w, bias, tm):
    t, half = ua.shape
    urow = pl.BlockSpec((tm, half), lambda i: (i, 0))
    yrow = pl.BlockSpec((2, tm, half), lambda i: (0, i, 0))
    return pl.pallas_call(
        _s5_glu_kernel,
        grid=(t // tm,),
        in_specs=[urow, urow, yrow, yrow, _full(d), _full(w), _full(bias)],
        out_specs=pl.BlockSpec((tm, MIX_W), lambda i: (i, 0)),
        out_shape=jax.ShapeDtypeStruct((t, MIX_W), BF16),
        compiler_params=_cparams("parallel"),
        name="s5_glu",
    )(ua, ub, ya, yb, d, w, bias)


def _rw_pre_kernel(x_ref, prev_ref, next_ref, mu_ref, g64_ref, kk_g_ref, ka_ref, rk_ref,
                   w0_ref, w1_ref, w2_ref, a0_ref, a1_ref, a2_ref, g1_ref, g2_ref,
                   r_ref, v_ref, kk_ref, lw_ref, kka_ref, km_ref, bon_ref, gate_ref,
                   *, seq_tiles, ctx_tiles):
    x = x_ref[...]
    n = x.shape[0]
    j = pl.program_id(0) % seq_tiles
    starts = jnp.logical_or(j == 0, j == ctx_tiles)
    ends = jnp.logical_or(j == ctx_tiles - 1, j == seq_tiles - 1)
    prev_row = jnp.where(starts, 0.0, prev_ref[0, 7:8, :])
    next_row = jnp.where(ends, 0.0, next_ref[0, 0:1, :])
    row = lax.broadcasted_iota(jnp.int32, x.shape, 0)
    left = jnp.where(row == 0, prev_row, pltpu.roll(x, 1, axis=0))
    right = jnp.where(row == n - 1, next_row, pltpu.roll(x, n - 1, axis=0))
    x = x + (0.5 * (left + right) - x) * mu_ref[...]
    r, k, v, xd = (x[:, i * MIX_W:(i + 1) * MIX_W] for i in range(4))
    g64 = g64_ref[...]
    kscaled = k * kk_g_ref[...]
    kk = kscaled / jnp.maximum(jnp.sqrt(_split_dot(kscaled * kscaled, g64)), 1e-12)
    xdb = xd.astype(BF16)
    r_ref[...] = r
    v_ref[...] = v
    kk_ref[...] = kk
    km_sum = None
    for d in range(2):
        lo = jnp.tanh(jnp.dot(xdb, w1_ref[d], preferred_element_type=F32))
        w_raw = w0_ref[d] + jnp.dot(lo.astype(BF16), w2_ref[d], preferred_element_type=F32)
        lw_ref[d] = -_sigmoid(w_raw) * math.exp(-0.5)
        ar = jnp.dot(xdb, a1_ref[d], preferred_element_type=F32)
        a = _sigmoid(a0_ref[d] + jnp.dot(ar.astype(BF16), a2_ref[d], preferred_element_type=F32))
        km = k * (1.0 + (a - 1.0) * ka_ref[...])
        kka_ref[d] = kk * a
        km_ref[d] = km
        km_sum = km if km_sum is None else km_sum + km
    bon_ref[...] = _split_dot(r * km_sum * rk_ref[...], g64) * v
    gr = _sigmoid(jnp.dot(xdb, g1_ref[...], preferred_element_type=F32))
    gate_ref[...] = jnp.dot(gr.astype(BF16), g2_ref[...], preferred_element_type=F32)


def _rw_pre_call(lay, rw, consts):
    t, tr = rw.shape[0], lay.t
    nt = t // tr
    g8 = tr // 8
    rw8 = rw.reshape(t // 8, 8, _RW_W)
    row = pl.BlockSpec((tr, MIX_W), lambda i: (i, 0))
    row2 = pl.BlockSpec((2, tr, MIX_W), lambda i: (0, i, 0))
    sd = jax.ShapeDtypeStruct((t, MIX_W), F32)
    sd2 = jax.ShapeDtypeStruct((2, t, MIX_W), F32)
    return pl.pallas_call(
        functools.partial(_rw_pre_kernel, seq_tiles=lay.seq_tiles, ctx_tiles=lay.ctx_tiles),
        grid=(nt,),
        in_specs=[pl.BlockSpec((tr, _RW_W), lambda i: (i, 0)),
                  pl.BlockSpec((1, 8, _RW_W), lambda i: (jnp.maximum(i * g8 - 1, 0), 0, 0)),
                  pl.BlockSpec((1, 8, _RW_W), lambda i: (jnp.minimum((i + 1) * g8, t // 8 - 1), 0, 0))]
                 + [_full(a) for a in consts],
        out_specs=[row, row, row, row2, row2, row2, row, row],
        out_shape=[sd, sd, sd, sd2, sd2, sd2, sd, sd],
        compiler_params=_cparams("parallel"),
        name="rwkv_pre",
    )(rw, rw8, rw8, *consts)


def _head_masks(shape, lane_axis, seg):
    lane = lax.broadcasted_iota(jnp.int32, shape, lane_axis)
    return [jnp.logical_and(lane >= h * seg, lane < (h + 1) * seg) for h in range(RW_HEADS)]


def _rw_prep_kernel(*refs, rev):
    tiles = [_rw_prep_tile(sub, *refs, rev=rev) for sub in range(RW_PREP_TILES)]
    while tiles:
        tiles = [t for t in tiles if next(t, None) is not None]


def _rw_prep_tile(sub, r_ref, kk_ref, v_ref, lw_ref, ka_ref, km_ref, perm_ref, permt_ref, g_ref, eye_ref,
                  br_ref, ck_ref, uvt_ref, y0_ref, pc_ref, *, rev):
    C, NC = RW_CHUNK, RW_TILE // RW_CHUNK
    perm, permt, g64, eye4 = perm_ref[...], permt_ref[...], g_ref[...], eye_ref[...]
    tok = slice(sub * RW_TILE, (sub + 1) * RW_TILE)
    nat = jnp.concatenate([r_ref[0, tok, :], kk_ref[0, tok, :], v_ref[0, tok, :],
                           lw_ref[0, 0, tok, :], ka_ref[0, 0, tok, :], km_ref[0, 0, tok, :]], axis=1)
    hi = nat.astype(BF16)
    lo = (nat - hi.astype(F32)).astype(BF16)
    pm = jnp.dot(perm, jnp.concatenate([hi, lo], axis=0), preferred_element_type=F32)
    r, kk, v, lw, ka, km = (pm[:, i * MIX_W:(i + 1) * MIX_W] for i in range(6))
    slab = lambda x, j: x[j * NC:(j + 1) * NC, :]
    order = list(range(C))[::-1] if rev else list(range(C))
    pos = {j: i for i, j in enumerate(order)}
    cum, run = {}, None
    for j in order:
        run = slab(lw, j) if run is None else run + slab(lw, j)
        cum[j] = run
    tot = run
    yield True
    bh, ch, kh, rh, cp, kp, vv = {}, {}, {}, {}, {}, {}, {}
    for j in range(C):
        e_inv, e_end = jnp.exp(-cum[j]), jnp.exp(tot - cum[j])
        bh[j] = -slab(kk, j) * jnp.exp(cum[j] - slab(lw, j))
        ch[j], kh[j] = slab(ka, j) * e_inv, slab(km, j) * e_inv
        rh[j] = slab(r, j) * jnp.exp(cum[j])
        cp[j], kp[j] = slab(ka, j) * e_end, slab(km, j) * e_end
        vv[j] = slab(v, j)
    strict = [(t, s) for t in order for s in order if pos[s] < pos[t]]
    incl = [(t, s) for t in order for s in order if pos[s] <= pos[t]]
    def head_dots(lhs, rhs, pairs):
        prods = jnp.concatenate([lhs[t] * rhs[s] for t, s in pairs], axis=0).astype(BF16)
        gram = jnp.dot(prods, g64, preferred_element_type=F32)
        return {p: gram[i * NC:(i + 1) * NC, :] for i, p in enumerate(pairs)}

    yield True
    acb = head_dots(bh, ch, strict)
    yield True
    akb = head_dots(bh, kh, strict)
    yield True
    mcr = head_dots(rh, ch, incl)
    yield True
    mkr = head_dots(rh, kh, incl)
    yield True
    bt, u0 = {}, {}
    for t in order:
        b_acc, u_acc = bh[t], jnp.zeros_like(bh[t])
        for s in order:
            if pos[s] < pos[t]:
                b_acc = b_acc + acb[(t, s)] * bt[s]
                u_acc = u_acc + akb[(t, s)] * vv[s] + acb[(t, s)] * u0[s]
        bt[t], u0[t] = b_acc, u_acc
        yield True
    rt, y0 = {}, {}
    for t in order:
        r_acc, y_acc = rh[t], jnp.zeros_like(rh[t])
        for s in order:
            if pos[s] <= pos[t]:
                r_acc = r_acc + mcr[(t, s)] * bt[s]
                y_acc = y_acc + mcr[(t, s)] * u0[s] + mkr[(t, s)] * vv[s]
        rt[t], y0[t] = r_acc, y_acc
        yield True
    stackp = lambda dct: jnp.concatenate([dct[j] for j in range(C)], axis=0)
    b16 = lambda x: x.astype(BF16)
    y0p, u0p = stackp(y0), stackp(u0)
    y0h, u0h = b16(y0p), b16(u0p)
    cat = jnp.concatenate([b16(stackp(bt)), b16(stackp(rt)), b16(stackp(cp)), b16(stackp(kp)),
                           y0h, b16(y0p - y0h.astype(F32)), u0h, b16(u0p - u0h.astype(F32)), b16(stackp(vv))], axis=1)
    natural = jnp.dot(permt, cat, preferred_element_type=F32)
    seg = lambda i: natural[:, i * MIX_W:(i + 1) * MIX_W]
    yield True
    btn, rtn, cpn, kpn = b16(seg(0)), b16(seg(1)), b16(seg(2)), b16(seg(3))
    y0_ref[0, tok, :] = seg(4) + seg(5)
    u0h_n, u0l_n, vn = b16(seg(6)), b16(seg(7)), b16(seg(8))
    hm = _head_masks((C, MIX_W), 1, RW_DIM)
    zero = jnp.zeros((C, MIX_W), BF16)
    zh, zl = [], []
    for c in range(NC):
        rows = slice(c * C, (c + 1) * C)
        br_ref[0, sub * NC + c, 0:C, :] = btn[rows]
        br_ref[0, sub * NC + c, C:2 * C, :] = rtn[rows]
        ck_ref[0, sub * NC + c, 0:C, :] = cpn[rows]
        ck_ref[0, sub * NC + c, C:2 * C, :] = kpn[rows]
        for h in range(RW_HEADS):
            zh += [jnp.where(hm[h], u0h_n[rows], zero), jnp.where(hm[h], vn[rows], zero)]
            zl += [jnp.where(hm[h], u0l_n[rows], zero), zero]
    z = jnp.concatenate([jnp.concatenate(zh, axis=0), jnp.concatenate(zl, axis=0)], axis=1)
    uvt = lax.dot_general(eye4, z, _NT, preferred_element_type=F32)
    for c in range(NC):
        uvt_ref[0, sub * NC + c] = uvt[:, c * 2 * C * RW_HEADS:(c + 1) * 2 * C * RW_HEADS]
    pc_ref[0, sub * NC:(sub + 1) * NC, :] = jnp.exp(tot)


def _rw_prep_call(lay, shared, perdir, consts, rev):
    b, n_tot, tt = lay.b, lay.n_tot, RW_TILE * RW_PREP_TILES
    assert n_tot % tt == 0
    nck = n_tot // RW_CHUNK
    cpt = tt // RW_CHUNK
    d = 1 if rev else 0
    sh = [a.reshape(b, n_tot, MIX_W) for a in shared]
    pd = [a.reshape(2, b, n_tot, MIX_W) for a in perdir]
    tok = pl.BlockSpec((1, tt, MIX_W), lambda i, j: (i, j, 0))
    tok_d = pl.BlockSpec((1, 1, tt, MIX_W), lambda i, j: (d, i, j, 0))
    rows32 = pl.BlockSpec((1, cpt, 2 * RW_CHUNK, MIX_W), lambda i, j: (i, j, 0, 0))
    return pl.pallas_call(
        functools.partial(_rw_prep_kernel, rev=rev),
        grid=(b, n_tot // tt),
        in_specs=[tok] * 3 + [tok_d] * 3 + [_full(a) for a in consts],
        out_specs=[rows32, rows32,
                   pl.BlockSpec((1, cpt, RW_DIM, 2 * RW_CHUNK * RW_HEADS), lambda i, j: (i, j, 0, 0)),
                   tok,
                   pl.BlockSpec((1, cpt, MIX_W), lambda i, j: (i, j, 0))],
        out_shape=[jax.ShapeDtypeStruct((b, nck, 2 * RW_CHUNK, MIX_W), BF16),
                   jax.ShapeDtypeStruct((b, nck, 2 * RW_CHUNK, MIX_W), BF16),
                   jax.ShapeDtypeStruct((b, nck, RW_DIM, 2 * RW_CHUNK * RW_HEADS), F32),
                   jax.ShapeDtypeStruct((b, n_tot, MIX_W), F32),
                   jax.ShapeDtypeStruct((b, nck, MIX_W), F32)],
        compiler_params=_cparams("parallel", "parallel"),
        name="rwkv_prep_rev" if rev else "rwkv_prep_fwd",
    )(*sh, *pd, *consts)


def _rw_scan_kernel(brf, ckf, uvtf, pcf, brr, ckr, uvtr, pcr, ytf_ref, ytr_ref, s_scr, *, n_batch):
    @pl.when(pl.program_id(0) == 0)
    def _():
        s_scr[...] = jnp.zeros_like(s_scr)

    cpt = RW_TILE // RW_CHUNK
    hm = _head_masks((2 * RW_CHUNK, MIX_W), 1, RW_DIM)
    lane = lax.broadcasted_iota(jnp.int32, (RW_DIM, 2 * RW_CHUNK * RW_HEADS), 1)
    is_u = (lane & (2 * RW_CHUNK - 1)) < RW_CHUNK
    per_head = lambda x: jnp.concatenate([jnp.where(m, x, jnp.zeros_like(x)) for m in hm], axis=0)

    def refs_of(p, c):
        d, b = divmod(p, n_batch)
        refs = (brf, ckf, uvtf, pcf, ytf_ref) if d == 0 else (brr, ckr, uvtr, pcr, ytr_ref)
        return refs, b, (c if d == 0 else cpt - 1 - c)

    def step(c, carry):
        lhs = []
        for p in range(2 * n_batch):
            (br_ref, _, uvt_ref, _, yt_ref), b, cc = refs_of(p, c)
            s = s_scr[p]
            shi = s.astype(BF16)
            slo = (s - shi.astype(F32)).astype(BF16)
            w2 = lax.dot_general(jnp.concatenate([shi, slo], axis=0), per_head(br_ref[b, cc]), _NT, preferred_element_type=F32)
            w = w2[:RW_DIM] + w2[RW_DIM:]
            yt_ref[b, cc] = w
            uvt = uvt_ref[b, cc]
            lhs.append(jnp.where(is_u, w + uvt, uvt).astype(BF16))
        for p in range(2 * n_batch):
            (_, ck_ref, _, pc_ref, _), b, cc = refs_of(p, c)
            s_scr[p] = (s_scr[p] * pc_ref[b, pl.ds(cc, 1), :]
                        + jnp.dot(lhs[p], per_head(ck_ref[b, cc]), preferred_element_type=F32))
        return carry

    lax.fori_loop(0, cpt, step, 0)


def _rw_scan_call(lay, fwd, rev):
    b, n_tot, tt = lay.b, lay.n_tot, RW_TILE
    assert lay.n_ctx % tt == 0 and lay.n_lat % tt == 0
    nt, ct = n_tot // tt, lay.n_ctx // tt
    cpt = tt // RW_CHUNK
    rev_tile = lambda i: jnp.where(i < ct, ct - 1 - i, nt - 1 + ct - i)

    def specs(tile):
        return [pl.BlockSpec((b, cpt, 2 * RW_CHUNK, MIX_W), lambda i: (0, tile(i), 0, 0)),
                pl.BlockSpec((b, cpt, 2 * RW_CHUNK, MIX_W), lambda i: (0, tile(i), 0, 0)),
                pl.BlockSpec((b, cpt, RW_DIM, 2 * RW_CHUNK * RW_HEADS), lambda i: (0, tile(i), 0, 0)),
                pl.BlockSpec((b, cpt, MIX_W), lambda i: (0, tile(i), 0))]

    ident = lambda i: i
    yt = jax.ShapeDtypeStruct((b, n_tot // RW_CHUNK, RW_DIM, 2 * RW_CHUNK * RW_HEADS), F32)
    return pl.pallas_call(
        functools.partial(_rw_scan_kernel, n_batch=b),
        grid=(nt,),
        in_specs=specs(ident) + specs(rev_tile),
        out_specs=[specs(ident)[2], specs(rev_tile)[2]],
        out_shape=[yt, yt],
        scratch_shapes=[pltpu.VMEM((2 * b, RW_DIM, MIX_W), F32)],
        compiler_params=_cparams("arbitrary"),
        name="rwkv_scan",
    )(*fwd, *rev)


def _rw_fin_kernel(ytf_ref, ytr_ref, y0f_ref, y0r_ref, bon_ref, gate_ref, asel_ref, g64_ref, lng_ref, lnb_ref, o_ref):
    cpt = RW_TILE // RW_CHUNK
    asel = asel_ref[...]
    width = cpt * 2 * RW_CHUNK * RW_HEADS
    lane = lax.broadcasted_iota(jnp.int32, (RW_DIM, width), 1)
    lane_head = jnp.bitwise_and(jnp.right_shift(lane, 5), RW_HEADS - 1)

    def base(yt_ref, sub):
        yt = jnp.concatenate([yt_ref[0, sub * cpt + c] for c in range(cpt)], axis=1).astype(BF16)
        rows = jnp.concatenate([jnp.where(lane_head == h, yt, jnp.zeros_like(yt)) for h in range(RW_HEADS)], axis=0)
        return lax.dot_general(asel, rows, _NT, preferred_element_type=F32)

    g64 = g64_ref[...]
    bases = [(base(ytf_ref, sub), base(ytr_ref, sub)) for sub in range(RW_PREP_TILES)]
    for sub, (yb_f, yb_r) in enumerate(bases):
        tok = slice(sub * RW_TILE, (sub + 1) * RW_TILE)
        y = yb_f + y0f_ref[0, tok, :] + yb_r + y0r_ref[0, tok, :]
        mean = _split_dot(y, g64) * (1.0 / RW_DIM)
        c = y - mean
        var = _split_dot(c * c, g64) * (1.0 / RW_DIM)
        out = c * lax.rsqrt(var + RW_LN_EPS) * lng_ref[...] + lnb_ref[...] + bon_ref[tok, :]
        o_ref[tok, :] = (out * gate_ref[tok, :]).astype(BF16)


def _rw_fin_call(lay, ytf, ytr, y0f, y0r, bon, gate, consts):
    b, n_tot, tt = lay.b, lay.n_tot, RW_TILE * RW_PREP_TILES
    nt = n_tot // tt
    cpt = tt // RW_CHUNK
    ytb = pl.BlockSpec((1, cpt, RW_DIM, 2 * RW_CHUNK * RW_HEADS), lambda i, j: (i, j, 0, 0))
    y0b = pl.BlockSpec((1, tt, MIX_W), lambda i, j: (i, j, 0))
    row = pl.BlockSpec((tt, MIX_W), lambda i, j: (i * nt + j, 0))
    return pl.pallas_call(
        _rw_fin_kernel,
        grid=(b, nt),
        in_specs=[ytb, ytb, y0b, y0b, row, row] + [_full(a) for a in consts],
        out_specs=row,
        out_shape=jax.ShapeDtypeStruct((b * n_tot, MIX_W), BF16),
        compiler_params=_cparams("parallel", "parallel"),
        name="rwkv_finish",
    )(ytf, ytr, y0f, y0r, bon, gate, *consts)


def _rw_constants():
    c, nc = RW_CHUNK, RW_TILE // RW_CHUNK
    perm = np.zeros((RW_TILE, RW_TILE), np.float32)
    for ci in range(nc):
        for j in range(c):
            perm[j * nc + ci, ci * c + j] = 1.0
    lane = np.arange(MIX_W) % RW_DIM
    eye4 = (lane[None, :] == np.arange(RW_DIM)[:, None]).astype(np.float32)
    lanes = np.arange(nc * 2 * c * RW_HEADS)
    lane_chunk, lane_tok = lanes // (2 * c * RW_HEADS), lanes % (2 * c)
    t = np.arange(RW_TILE)
    asel = ((lane_chunk[None, :] == (t // c)[:, None]) & (lane_tok[None, :] == (c + t % c)[:, None])).astype(np.float32)
    as16 = lambda a: jnp.asarray(a, BF16)
    twice = lambda a: np.concatenate([a, a], axis=1)
    return as16(twice(perm)), as16(perm.T), as16(twice(eye4)), as16(asel)


def _rwkv_branch(lay, rw, pre_consts, g64, lng, lnb):
    r_, v_, kk_, lw_, kka_, km_, bon, gate = _rw_pre_call(lay, rw, pre_consts)
    perm, permt, eye4, asel = _rw_constants()
    prep_consts = (perm, permt, g64, eye4)
    fwd = _rw_prep_call(lay, (r_, kk_, v_), (lw_, kka_, km_), prep_consts, False)
    rev = _rw_prep_call(lay, (r_, kk_, v_), (lw_, kka_, km_), prep_consts, True)
    pick = lambda o: (o[0], o[1], o[2], o[4])
    ytf, ytr = _rw_scan_call(lay, pick(fwd), pick(rev))
    return _rw_fin_call(lay, ytf, ytr, fwd[3], rev[3], bon, gate, (asel, g64, lng, lnb))


def _merge_kernel(x_ref, modb_ref, modc_ref, g_ref, wg_ref, ya_ref, yb_ref, yc_ref, yd_ref, wb_ref, wo_ref, o_ref,
                  *, ctx_rows, tiles_per_seq):
    x = x_ref[...]
    mrow = lambda r: _mod_row(modb_ref, modc_ref, r, x.shape[0], ctx_rows, tiles_per_seq)
    h = _modulate(x, g_ref[...], mrow(0), mrow(1)).astype(BF16)
    merged = None
    for i, y_ref in enumerate((ya_ref, yb_ref, yc_ref, yd_ref)):
        gate = _sigmoid(jnp.dot(h, wg_ref[:, i * D_MODEL:(i + 1) * D_MODEL], preferred_element_type=F32))
        term = gate * jnp.dot(y_ref[...], wb_ref[i], preferred_element_type=F32)
        merged = term if merged is None else merged + term
    out = jnp.dot(merged.astype(BF16), wo_ref[...], preferred_element_type=F32)
    o_ref[...] = x + mrow(2) * out


def _merge_call(lay, with_ctx, x_all, mod, g, w_gate, ya, yb, yc, yd, w_branch, w_out):
    if with_ctx:
        tm, tps = _seq_tile(lay)
        src, n_tiles, ctx_rows = (lambda i: i), lay.b * tps, lay.n_ctx
    else:
        tm, tps = lay.t, lay.lat_tiles
        src, n_tiles, ctx_rows = lay.src_tile(False), lay.n_tiles(False), 0
    full_row = lambda w: pl.BlockSpec((tm, w), lambda i: (src(i), 0))
    out_row = lambda w: pl.BlockSpec((tm, w), lambda i: (i, 0))
    return pl.pallas_call(
        functools.partial(_merge_kernel, ctx_rows=ctx_rows, tiles_per_seq=tps),
        grid=(n_tiles,),
        in_specs=[full_row(D_MODEL), pl.BlockSpec((1, 6, D_MODEL), lambda i: (i // tps, 0, 0)),
                  pl.BlockSpec((1, 6, D_MODEL), lambda i: (lay.b, 0, 0)), _full(g), _full(w_gate),
                  out_row(MIX_W), full_row(MIX_W), out_row(MIX_W), full_row(MIX_W), _full(w_branch), _full(w_out)],
        out_specs=out_row(D_MODEL),
        out_shape=jax.ShapeDtypeStruct((lay.rows(with_ctx), D_MODEL), F32),
        compiler_params=_cparams("parallel"),
        name="merge_out",
    )(x_all, mod, mod, g, w_gate, ya, yb, yc, yd, w_branch, w_out)


def _router_kernel(x_ref, modb_ref, modc_ref, g_ref, wh_ref, wl_ref, bias_ref, f_ref, comb_ref, gid_ref,
                   *, ctx_rows, tiles_per_seq):
    x = x_ref[...]
    mrow = lambda r: _mod_row(modb_ref, modc_ref, r, x.shape[0], ctx_rows, tiles_per_seq)
    f = _modulate(x, g_ref[...], mrow(3), mrow(4))
    fh = f.astype(BF16)
    f_ref[...] = fh
    fl = (f - fh.astype(F32)).astype(BF16)
    nt = (((1,), (1,)), ((), ()))
    wh, wl = wh_ref[...], wl_ref[...]
    logits = (lax.dot_general(wh, fh, nt, preferred_element_type=F32)
              + lax.dot_general(wh, fl, nt, preferred_element_type=F32)
              + lax.dot_general(wl, fh, nt, preferred_element_type=F32))
    scores = _sigmoid(logits)
    biased = scores + bias_ref[...]
    sc = [scores[e:e + 1, :] for e in range(N_EXPERTS)]
    bi = [biased[e:e + 1, :] for e in range(N_EXPERTS)]
    group_score = []
    for g in range(N_GROUPS):
        a, b, c, d = bi[4 * g:4 * g + 4]
        m1, n1, m2, n2 = jnp.maximum(a, b), jnp.minimum(a, b), jnp.maximum(c, d), jnp.minimum(c, d)
        group_score.append(jnp.maximum(m1, m2) + jnp.maximum(jnp.minimum(m1, m2), jnp.maximum(n1, n2)))

    def first_argmax(vals):
        top = functools.reduce(jnp.maximum, vals)
        seen, hot = None, []
        for v in vals:
            h = v == top
            if seen is not None:
                h = jnp.logical_and(h, jnp.logical_not(seen))
            seen = h if seen is None else jnp.logical_or(seen, h)
            hot.append(h)
        return hot

    in_group = first_argmax(group_score)
    masked = [jnp.where(in_group[e // EXPERTS_PER_GROUP], bi[e], -jnp.inf) for e in range(N_EXPERTS)]
    hot1 = first_argmax(masked)
    hot2 = first_argmax([jnp.where(h, -jnp.inf, v) for h, v in zip(hot1, masked)])
    w1 = functools.reduce(jnp.add, [jnp.where(h, s, 0.0) for h, s in zip(hot1, sc)])
    w2 = functools.reduce(jnp.add, [jnp.where(h, s, 0.0) for h, s in zip(hot2, sc)])
    inv_tot = 1.0 / (w1 + w2)
    for e in range(N_EXPERTS):
        comb_ref[e:e + 1, :] = (jnp.where(hot1[e], w1, 0.0) + jnp.where(hot2[e], w2, 0.0)) * inv_tot
    gid_ref[...] = functools.reduce(jnp.add, [jnp.where(in_group[g], g, 0) for g in range(1, N_GROUPS)])


def _router_call(lay, with_ctx, x, mod, g, wh, wl, bias):
    t = x.shape[0]
    if with_ctx:
        (tm, tps), ctx_rows = _seq_tile(lay), lay.n_ctx
    else:
        pair = 2 if lay.lat_tiles % 2 == 0 else 1
        tm, tps, ctx_rows = pair * lay.t, lay.lat_tiles // pair, 0
    return pl.pallas_call(
        functools.partial(_router_kernel, ctx_rows=ctx_rows, tiles_per_seq=tps),
        grid=(t // tm,),
        in_specs=[pl.BlockSpec((tm, D_MODEL), lambda i: (i, 0)),
                  pl.BlockSpec((1, 6, D_MODEL), lambda i: (i // tps, 0, 0)),
                  pl.BlockSpec((1, 6, D_MODEL), lambda i: (lay.b, 0, 0)), _full(g), _full(wh), _full(wl), _full(bias)],
        out_specs=[pl.BlockSpec((tm, D_MODEL), lambda i: (i, 0)), pl.BlockSpec((N_EXPERTS, tm), lambda i: (0, i)),
                   pl.BlockSpec((1, tm), lambda i: (0, i))],
        out_shape=[jax.ShapeDtypeStruct((t, D_MODEL), BF16), jax.ShapeDtypeStruct((N_EXPERTS, t), F32),
                   jax.ShapeDtypeStruct((1, t), jnp.int32)],
        compiler_params=_cparams("parallel"),
        name="moe_router",
    )(x, mod, mod, g, wh, wl, bias)


def _moe_plan(gid, n_tiles, tm):
    g = gid.reshape(n_tiles, tm)
    onehot = (g[..., None] == jnp.arange(N_GROUPS, dtype=jnp.int32)).astype(jnp.int32)
    rank = jnp.cumsum(onehot, axis=1) - onehot
    counts = jnp.sum(onehot, axis=1)
    padded = (counts + 15) // 16 * 16
    offs = jnp.cumsum(padded, axis=1) - padded
    pos = jnp.sum(onehot * (offs[:, None, :] + rank), axis=-1)
    n_over = (jnp.maximum(padded - MOE_BLOCK, 0) + MOE_OVER - 1) // MOE_OVER
    return pos.astype(jnp.int32), offs.astype(jnp.int32), n_over.astype(jnp.int32)


def _moe_kernel(offs_ref, nover_ref, f_ref, posr_ref, posc_ref, comb_ref, wgu_ref, wd_ref, x_ref, modb_ref,
                modc_ref, o_ref, xs_scr, cs_scr, ys_scr, *, ctx_rows, tiles_per_seq):
    i, e = pl.program_id(0), pl.program_id(1)
    n_slots, tm = xs_scr.shape[0], f_ref.shape[0]
    n_live = min(n_slots, -(-(tm + 16 * N_GROUPS) // 256) * 256)

    @pl.when(e == 0)
    def _():
        slot = lax.broadcasted_iota(jnp.int32, (n_live, tm), 0)
        place = (slot == posr_ref[0]).astype(BF16)
        xs_scr[0:n_live, :] = jnp.dot(place, f_ref[...], preferred_element_type=F32).astype(BF16)
        xs_scr[n_live:n_slots, :] = jnp.zeros((n_slots - n_live, D_MODEL), BF16)
        cs_scr[0:n_live, :] = _split_dot_rhs(place, comb_ref[...])
        cs_scr[n_live:n_slots, :] = jnp.zeros((n_slots - n_live, N_EXPERTS), F32)
        ys_scr[...] = jnp.zeros_like(ys_scr)

    grp = lax.shift_right_logical(e, 2)
    start = offs_ref[i, grp]
    lane = lax.broadcasted_iota(jnp.int32, (1, N_EXPERTS), 1)

    def run(rows):
        xb = xs_scr[rows, :]
        gate_up = jnp.dot(xb, wgu_ref[0], preferred_element_type=F32)
        gate, up = gate_up[:, 0:D_FF], gate_up[:, D_FF:2 * D_FF]
        act = (gate * _sigmoid(gate) * up).astype(BF16)
        down = jnp.dot(act, wd_ref[0], preferred_element_type=F32)
        c_e = jnp.sum(jnp.where(lane == e, cs_scr[rows, :], 0.0), axis=1, keepdims=True)
        ys_scr[rows, :] += c_e * down

    run(pl.ds(pl.multiple_of(start, 16), MOE_BLOCK))

    def overflow(k, carry):
        run(pl.ds(pl.multiple_of(start + MOE_BLOCK + k * MOE_OVER, 16), MOE_OVER))
        return carry

    lax.fori_loop(0, nover_ref[i, grp], overflow, 0)

    @pl.when(e == N_EXPERTS - 1)
    def _():
        slot = lax.broadcasted_iota(jnp.int32, (tm, n_live), 1)
        fetch = (slot == posc_ref[...]).astype(BF16)
        y = jnp.dot(fetch, ys_scr[0:n_live, :].astype(BF16), preferred_element_type=F32)
        res_gate = modb_ref[0, 5:6, :]
        if ctx_rows:
            row = lax.broadcasted_iota(jnp.int32, y.shape, 0)
            first = i % tiles_per_seq == 0
            res_gate = jnp.where(jnp.logical_and(first, row < ctx_rows), modc_ref[0, 5:6, :], res_gate)
        o_ref[...] = x_ref[...] + res_gate * y


def _moe_call(lay, with_ctx, f, comb, gid, wgu, wd, x, mod):
    t = f.shape[0]
    seq = lay.n_tot if with_ctx else lay.n_lat
    tm = MOE_TILE if seq % MOE_TILE == 0 else math.gcd(seq, 1024)
    tps, n_tiles = seq // tm, t // tm
    ctx_rows = lay.n_ctx if with_ctx else 0
    assert ctx_rows <= tm
    n_slots = -(-(tm + 16 * N_GROUPS + MOE_BLOCK + MOE_OVER) // 256) * 256
    pos, offs, n_over = _moe_plan(gid, n_tiles, tm)
    wspec = lambda a: pl.BlockSpec((1,) + a.shape[1:], lambda i, e, *_: (e, 0, 0))
    tok = lambda w: pl.BlockSpec((tm, w), lambda i, e, *_: (i, 0))
    grid_spec = pltpu.PrefetchScalarGridSpec(
        num_scalar_prefetch=2,
        grid=(n_tiles, N_EXPERTS),
        in_specs=[tok(D_MODEL), pl.BlockSpec((1, 1, tm), lambda i, e, *_: (i, 0, 0)), tok(1), tok(N_EXPERTS),
                  wspec(wgu), wspec(wd), tok(D_MODEL),
                  pl.BlockSpec((1, 6, D_MODEL), lambda i, e, *_: (i // tps, 0, 0)),
                  pl.BlockSpec((1, 6, D_MODEL), lambda i, e, *_: (lay.b, 0, 0))],
        out_specs=tok(D_MODEL),
        scratch_shapes=[pltpu.VMEM((n_slots, D_MODEL), BF16), pltpu.VMEM((n_slots, N_EXPERTS), F32),
                        pltpu.VMEM((n_slots, D_MODEL), F32)])
    return pl.pallas_call(
        functools.partial(_moe_kernel, ctx_rows=ctx_rows, tiles_per_seq=tps),
        grid_spec=grid_spec,
        out_shape=jax.ShapeDtypeStruct((t, D_MODEL), F32),
        compiler_params=_cparams("parallel", "arbitrary"),
        name="moe_experts",
    )(offs, n_over, f, pos.reshape(n_tiles, 1, tm), pos.reshape(t, 1), comb, wgu, wd, x, mod, mod)


def _block_ones(n, group):
    i = np.arange(n) // group
    return jnp.asarray(i[:, None] == i[None, :], dtype=BF16)


def _rope_tables(n_ctx, n_lat):
    rows = n_lat // GRID_W
    row = jnp.repeat(jnp.arange(rows, dtype=F32), GRID_W)
    col = jnp.tile(jnp.arange(GRID_W, dtype=F32), rows)

    def angles(rot_dim):
        n_freq = rot_dim // 4
        inv_freq = ROPE_BASE ** (-jnp.arange(n_freq, dtype=F32) / n_freq)
        ang = jnp.concatenate([row[:, None] * inv_freq, col[:, None] * inv_freq], axis=-1)
        return jnp.cos(ang), jnp.sin(ang)

    c, s = angles(DA_DIM)
    cda = jnp.tile(jnp.concatenate([c, c], -1), (1, 2 * DA_HEADS))
    sda = jnp.tile(jnp.concatenate([-s, s], -1), (1, 2 * DA_HEADS))
    c, s = angles(MLA_ROPE)
    one = jnp.ones((n_lat, MLA_NOPE), F32)
    pad = MLA_HEAD_PAD - MLA_NOPE - MLA_ROPE
    cml = jnp.tile(jnp.concatenate([one, c, c, jnp.ones((n_lat, pad), F32)], -1), (1, MLA_HEADS))
    sml = jnp.tile(jnp.concatenate([0 * one, -s, s, jnp.zeros((n_lat, pad), F32)], -1), (1, MLA_HEADS))
    ident = lambda t, v: jnp.concatenate([jnp.full((n_ctx, MIX_W), v, F32), t], axis=0)
    return ident(cda, 1.0), ident(sda, 0.0), ident(cml, 1.0), ident(sml, 0.0)


def _pad_heads(w, n_heads, src_w, lo, hi, dst_w=MLA_HEAD_PAD):
    w = w.reshape(w.shape[0], n_heads, src_w)[:, :, lo:hi]
    w = jnp.pad(w, ((0, 0), (0, 0), (0, dst_w - (hi - lo))))
    return w.reshape(w.shape[0], n_heads * dst_w)


def _mix_weight(w_in_l):
    w = w_in_l
    kr = w[:, 1344:1360]
    z = lambda n: jnp.zeros((D_MODEL, n), w.dtype)
    kr_wide = jnp.concatenate([jnp.concatenate([z(MLA_NOPE), kr, z(MLA_HEAD_PAD - MLA_NOPE - MLA_ROPE)], 1)] * MLA_HEADS, 1)
    return jnp.concatenate([w[:, 0:1024], w[:, 1024:1216], z(64), w[:, 1216:1344], kr_wide, w[:, 1360:2384]], axis=1).astype(BF16)


def kernel(x, c, ctx, c_ctx, w_ada, b_ada, norm_mix_g, norm_ffn_g, w_in, da_qk_norm_g, da_lambda, da_subln_g, s5_lam_re, s5_lam_im, s5_log_dt, s5_b_re, s5_b_im, s5_c_re, s5_c_im, s5_d, s5_w_glu, s5_b_glu, mla_cq_norm_g, mla_ckv_norm_g, mla_w_uq, mla_w_ukv, mla_qk_norm_g, rw_mu, rw_w0, rw_w1, rw_w2, rw_a0, rw_a1, rw_a2, rw_g1, rw_g2, rw_k_k, rw_k_a, rw_r_k, rw_ln_g, rw_ln_b, w_branch, w_out, router_w, router_bias, exp_w_gate, exp_w_up, exp_w_down):
    b, n_lat, dm = x.shape
    n_ctx = ctx.shape[1]
    depth = w_ada.shape[0]
    assert dm == D_MODEL
    lay = _Layout(b, n_ctx, n_lat)
    t_all = b * lay.n_tot
    tm_big = 2 * lay.t

    g32 = _block_ones(MIX_W, DA_DIM)
    g64 = _block_ones(MIX_W, RW_DIM)
    tabs = _rope_tables(n_ctx, n_lat)
    row = lambda v: v.reshape(1, -1).astype(F32)
    bf = lambda a: a.astype(BF16)

    cc = jnp.zeros((16, dm), F32).at[:b].set(c).at[b].set(c_ctx)
    mod_all = _ada_call(cc, w_ada, b_ada)
    x_all = jnp.concatenate([ctx, x], axis=1).reshape(t_all, dm)

    s5_mats = jax.vmap(_s5_mats)(s5_lam_re, s5_lam_im, s5_log_dt, s5_b_re, s5_b_im, s5_c_re, s5_c_im)

    wr_hi = router_w.T.astype(BF16)
    wr_lo = (router_w.T - wr_hi.astype(F32)).astype(BF16)
    r_bias = router_bias.reshape(N_EXPERTS, 1).astype(F32)

    for l in range(depth):
        need_ctx = l < depth - 1
        lambda_init = 0.8 - 0.6 * math.exp(-0.3 * l)
        mod = mod_all[l, :b + 1].reshape(b + 1, 6, dm)
        g_mix = row(norm_mix_g[l])
        da, s5a, s5b, mla, rw = _inproj_call(lay, x_all, mod, g_mix, _mix_weight(w_in[l]))

        log2e = math.log2(math.e)
        gda = jnp.stack([jnp.tile(da_qk_norm_g[l, 0], 2 * DA_HEADS) * (DA_DIM ** -0.5 * log2e), jnp.tile(da_qk_norm_g[l, 1], 2 * DA_HEADS)])
        mla_pad = MLA_HEAD_PAD - MLA_NOPE - MLA_ROPE
        gml = jnp.stack([jnp.tile(jnp.pad(mla_qk_norm_g[l, 0], (0, mla_pad)), MLA_HEADS) * ((MLA_NOPE + MLA_ROPE) ** -0.5 * log2e),
                         jnp.tile(jnp.pad(mla_qk_norm_g[l, 1], (0, mla_pad)), MLA_HEADS)])
        wuq = bf(jnp.pad(_pad_heads(mla_w_uq[l], MLA_HEADS, MLA_NOPE + MLA_ROPE, 0, MLA_NOPE + MLA_ROPE), ((0, 64), (0, 0))))
        wuk = bf(_pad_heads(mla_w_ukv[l], MLA_HEADS, MLA_NOPE + MLA_VDIM, 0, MLA_NOPE))
        wuv = bf(_pad_heads(mla_w_ukv[l], MLA_HEADS, MLA_NOPE + MLA_VDIM, MLA_NOPE, MLA_NOPE + MLA_VDIM))
        consts = (g32, g64, gda.astype(F32), gml.astype(F32), row(jnp.pad(mla_cq_norm_g[l], (0, 64))), row(mla_ckv_norm_g[l]),
                  wuq, wuk, wuv)
        qd, kdt, vd, qm, kmt, vm = _qkprep_call(lay, da, mla, tabs, consts)

        lam32 = da_lambda[l].astype(F32)
        lmbda = (jnp.exp(jnp.sum(lam32[0] * lam32[1])) - jnp.exp(jnp.sum(lam32[2] * lam32[3])) + lambda_init).reshape(1, 1)
        subln = row(jnp.tile(da_subln_g[l], DA_HEADS) * (1.0 - lambda_init))
        ya = _attention(lay, qd, kdt, vd, (lmbda, subln, g64), True, need_ctx, "diff_attn")
        yc = _attention(lay, qm, kmt, vm, (lmbda, subln, g64), False, need_ctx, "mla_attn")

        ys_a, ys_b = _s5_scan(lay, s5a, s5b, s5_mats, l)
        yb = _s5_glu_call(s5a, s5b, ys_a, ys_b, row(s5_d[l]), bf(s5_w_glu[l]), row(s5_b_glu[l]), tm_big)

        pre_consts = (row(rw_mu[l]), g64, row(rw_k_k[l]), row(rw_k_a[l]), row(rw_r_k[l]),
                      rw_w0[l].reshape(2, 1, MIX_W), bf(rw_w1[l]), bf(rw_w2[l]),
                      rw_a0[l].reshape(2, 1, MIX_W), bf(rw_a1[l]), bf(rw_a2[l]), bf(rw_g1[l]), bf(rw_g2[l]))
        yd = _rwkv_branch(lay, rw, pre_consts, g64, row(rw_ln_g[l]), row(rw_ln_b[l]))

        x_mid = _merge_call(lay, need_ctx, x_all, mod, g_mix, bf(w_in[l][:, 2384:]), ya, yb, yc, yd,
                            bf(w_branch[l]), bf(w_out[l]))
        f, comb_t, gid = _router_call(lay, need_ctx, x_mid, mod, row(norm_ffn_g[l]), wr_hi, wr_lo, r_bias)
        w_gate_up = bf(jnp.concatenate([exp_w_gate[l], exp_w_up[l]], axis=-1))
        x_all = _moe_call(lay, need_ctx, f, comb_t.T, gid, w_gate_up, bf(exp_w_down[l]), x_mid, mod)
    return x_all.reshape(b, n_lat, dm)
```

```python
import functools
import math

import numpy as np
import jax
import jax.numpy as jnp
from jax import lax
from jax.experimental import pallas as pl
from jax.experimental.pallas import tpu as pltpu

F32 = jnp.float32
BF16 = jnp.bfloat16

D_MODEL = 1024
GRID_W = 64
ROPE_BASE = 10000.0
EPS = 1e-6
DA_HEADS, DA_DIM, DA_VDIM = 4, 32, 64
S5_GROUPS, S5_CH, S5_STATE = 16, 16, 64
MLA_HEADS, MLA_NOPE, MLA_ROPE, MLA_VDIM = 4, 32, 16, 64
MLA_Q_RANK, MLA_KV_RANK = 192, 128
MLA_HEAD_PAD = 64
RW_HEADS, RW_DIM = 4, 64
RW_LN_EPS = 64e-5
N_BRANCH = 4
N_EXPERTS, N_GROUPS, EXPERTS_PER_GROUP = 16, 4, 4
D_FF = 512
MIX_W = 256

S5_CHUNK = 8
S5_FLAT = S5_CHUNK * MIX_W
S5_STATE_W = S5_GROUPS * S5_STATE
RW_CHUNK = 16
RW_TILE = 128
RW_PREP_TILES = 2
_NT = (((1,), (1,)), ((), ()))
TOKEN_TILE = 256
PROJ_TILE = 768
MOE_TILE = 1152
MOE_BLOCK = 384
MOE_OVER = 128

_DA_W, _S5_W, _MLA_W, _RW_W = 768, 256, 640, 1024
_MIX_COLS = _DA_W + _S5_W + _MLA_W + _RW_W

V7X_VMEM_BYTES = 64 * 2**20
_VMEM_LIMIT = V7X_VMEM_BYTES - 8 * 2**20


def _cparams(*sem):
    return pltpu.CompilerParams(dimension_semantics=sem, vmem_limit_bytes=_VMEM_LIMIT)


def _full(a):
    return pl.BlockSpec(a.shape, lambda *_, nd=a.ndim: (0,) * nd)


def _split_dot(x, w, terms=2):
    acc = None
    rem = x
    for i in range(terms):
        part = rem.astype(BF16)
        d = jnp.dot(part, w, preferred_element_type=F32)
        acc = d if acc is None else acc + d
        if i + 1 < terms:
            rem = rem - part.astype(F32)
    return acc


def _split_dot_rhs(w, x):
    hi = x.astype(BF16)
    lo = (x - hi.astype(F32)).astype(BF16)
    return jnp.dot(w, hi, preferred_element_type=F32) + jnp.dot(w, lo, preferred_element_type=F32)


def _modulate(x, g, shift, scale):
    xn = x * lax.rsqrt(jnp.mean(x * x, axis=-1, keepdims=True) + EPS)
    return xn * g * (1.0 + scale) + shift


def _sigmoid(x):
    return 1.0 / (1.0 + jnp.exp(-x))


def _group_rms(x, ones_bd, inv_n, gain):
    ms = _split_dot(x * x, ones_bd) * inv_n
    return x * lax.rsqrt(ms + EPS) * gain


def _lane_partner(x, half, period, first_end):
    n = x.shape[1]
    lane = lax.broadcasted_iota(jnp.int32, x.shape, 1)
    up = pltpu.roll(x, n - half, axis=1)
    down = pltpu.roll(x, half, axis=1)
    return jnp.where((lane & (period - 1)) < first_end, up, down)


def _rope(x, cos_t, sin_t, half, period, first_end):
    return x * cos_t + _lane_partner(x, half, period, first_end) * sin_t


def _ada_kernel(c_ref, w_ref, b_ref, o_ref):
    c = c_ref[...]
    s = c * _sigmoid(c)
    o_ref[0] = jnp.dot(s.astype(BF16), w_ref[0].astype(BF16), preferred_element_type=F32) + b_ref[0]


def _ada_call(cc, w_ada, b_ada):
    depth, dm, n = w_ada.shape
    tn = n // 4
    return pl.pallas_call(
        _ada_kernel,
        grid=(depth, n // tn),
        in_specs=[
            pl.BlockSpec(cc.shape, lambda l, j: (0, 0)),
            pl.BlockSpec((1, dm, tn), lambda l, j: (l, 0, j)),
            pl.BlockSpec((1, 1, tn), lambda l, j: (l, 0, j)),
        ],
        out_specs=pl.BlockSpec((1, cc.shape[0], tn), lambda l, j: (l, 0, j)),
        out_shape=jax.ShapeDtypeStruct((depth, cc.shape[0], n), F32),
        compiler_params=_cparams("parallel", "parallel"),
        name="ada_mod",
    )(cc, w_ada, b_ada.reshape(depth, 1, n))


class _Layout:
    def __init__(self, n_batch, n_ctx, n_lat):
        t = TOKEN_TILE
        assert n_ctx % t == 0 and n_lat % t == 0
        self.b, self.n_ctx, self.n_lat, self.n_tot = n_batch, n_ctx, n_lat, n_ctx + n_lat
        self.t = t
        self.ctx_tiles, self.lat_tiles, self.seq_tiles = n_ctx // t, n_lat // t, (n_ctx + n_lat) // t

    def rows(self, with_ctx):
        return self.b * (self.n_tot if with_ctx else self.n_lat)

    def n_tiles(self, with_ctx):
        return self.b * (self.seq_tiles if with_ctx else self.lat_tiles)

    def src_tile(self, with_ctx):
        if with_ctx:
            return lambda i: i
        return lambda i: (i // self.lat_tiles) * self.seq_tiles + i % self.lat_tiles + self.ctx_tiles


def _mod_row(modb_ref, modc_ref, r, n_rows, ctx_rows, tiles_per_seq):
    per_batch = modb_ref[0, r:r + 1, :]
    if not ctx_rows:
        return per_batch
    row = lax.broadcasted_iota(jnp.int32, (n_rows, 1), 0)
    first = pl.program_id(0) % tiles_per_seq == 0
    return jnp.where(jnp.logical_and(first, row < ctx_rows), modc_ref[0, r:r + 1, :], per_batch)


def _inproj_kernel(x_ref, modb_ref, modc_ref, g_ref, w_ref, da_ref, s5a_ref, s5b_ref, mla_ref, rw_ref,
                   *, ctx_rows, tiles_per_seq):
    x = x_ref[...]
    mrow = lambda r: _mod_row(modb_ref, modc_ref, r, x.shape[0], ctx_rows, tiles_per_seq)
    h = _modulate(x, g_ref[...], mrow(0), mrow(1))
    acc = jnp.dot(h.astype(BF16), w_ref[...], preferred_element_type=F32)
    da_ref[...] = acc[:, 0:_DA_W]
    s5a_ref[...] = acc[:, _DA_W:_DA_W + _S5_W // 2]
    s5b_ref[...] = acc[:, _DA_W + _S5_W // 2:_DA_W + _S5_W]
    mla_ref[...] = acc[:, _DA_W + _S5_W:_DA_W + _S5_W + _MLA_W]
    rw_ref[...] = acc[:, _DA_W + _S5_W + _MLA_W:_MIX_COLS]


def _seq_tile(lay):
    tm = PROJ_TILE if lay.n_tot % PROJ_TILE == 0 and lay.n_ctx <= PROJ_TILE else lay.t
    return tm, lay.n_tot // tm


def _inproj_call(lay, x_all, mod, g, w_mix):
    t = x_all.shape[0]
    tm, tps = _seq_tile(lay)
    widths = (_DA_W, _S5_W // 2, _S5_W // 2, _MLA_W, _RW_W)
    return pl.pallas_call(
        functools.partial(_inproj_kernel, ctx_rows=lay.n_ctx, tiles_per_seq=tps),
        grid=(t // tm,),
        in_specs=[
            pl.BlockSpec((tm, D_MODEL), lambda i: (i, 0)),
            pl.BlockSpec((1, 6, D_MODEL), lambda i: (i // tps, 0, 0)),
            pl.BlockSpec((1, 6, D_MODEL), lambda i: (lay.b, 0, 0)),
            _full(g), _full(w_mix),
        ],
        out_specs=[pl.BlockSpec((tm, w), lambda i: (i, 0)) for w in widths],
        out_shape=[jax.ShapeDtypeStruct((t, w), F32) for w in widths],
        compiler_params=_cparams("parallel"),
        name="in_proj",
    )(x_all, mod, mod, g, w_mix)


def _qkprep_kernel(da_ref, mla_ref, cda_ref, sda_ref, cml_ref, sml_ref, g32_ref, g64_ref,
                   gda_ref, gml_ref, cqg_ref, ckvg_ref, wuq_ref, wuk_ref, wuv_ref,
                   qd_ref, kd_ref, vd_ref, qm_ref, km_ref, vm_ref):
    g32 = g32_ref[...]
    g64 = g64_ref[...]
    cda, sda = cda_ref[...], sda_ref[...]
    q = _group_rms(da_ref[:, 0:MIX_W], g32, 1.0 / DA_DIM, gda_ref[0:1, :])
    qd_ref[...] = _rope(q, cda, sda, DA_DIM // 2, DA_DIM, DA_DIM // 2).astype(BF16)
    k = _group_rms(da_ref[:, MIX_W:2 * MIX_W], g32, 1.0 / DA_DIM, gda_ref[1:2, :])
    kd_ref[0] = _rope(k, cda, sda, DA_DIM // 2, DA_DIM, DA_DIM // 2).T.astype(BF16)
    vd_ref[...] = da_ref[:, 2 * MIX_W:3 * MIX_W].astype(BF16)

    cml, sml = cml_ref[...], sml_ref[...]
    cq = mla_ref[:, 0:256]
    cqn = cq * lax.rsqrt(jnp.sum(cq * cq, axis=-1, keepdims=True) * (1.0 / MLA_Q_RANK) + EPS) * cqg_ref[...]
    q = jnp.dot(cqn.astype(BF16), wuq_ref[...], preferred_element_type=F32)
    ckv = mla_ref[:, 256:384]
    ckvn = ckv * lax.rsqrt(jnp.mean(ckv * ckv, axis=-1, keepdims=True) + EPS) * ckvg_ref[...]
    ckvb = ckvn.astype(BF16)
    k = jnp.dot(ckvb, wuk_ref[...], preferred_element_type=F32) + mla_ref[:, 384:640]
    vm_ref[...] = jnp.dot(ckvb, wuv_ref[...], preferred_element_type=F32).astype(BF16)
    inv_n = 1.0 / (MLA_NOPE + MLA_ROPE)
    half = MLA_ROPE // 2
    q = _group_rms(q, g64, inv_n, gml_ref[0:1, :])
    qm_ref[...] = _rope(q, cml, sml, half, MLA_HEAD_PAD, MLA_NOPE + half).astype(BF16)
    k = _group_rms(k, g64, inv_n, gml_ref[1:2, :])
    km_ref[0] = _rope(k, cml, sml, half, MLA_HEAD_PAD, MLA_NOPE + half).T.astype(BF16)


def _qkprep_call(lay, da, mla, tabs, consts):
    t = da.shape[0]
    tm, st = _seq_tile(lay)
    row = pl.BlockSpec((tm, MIX_W), lambda i: (i, 0))
    key_t = pl.BlockSpec((1, MIX_W, tm), lambda i: (i // st, 0, i % st))
    in_specs = [pl.BlockSpec((tm, _DA_W), lambda i: (i, 0)), pl.BlockSpec((tm, _MLA_W), lambda i: (i, 0))]
    in_specs += [pl.BlockSpec((tm, MIX_W), lambda i: (i % st, 0)) for _ in tabs]
    in_specs += [_full(a) for a in consts]
    tok = jax.ShapeDtypeStruct((t, MIX_W), BF16)
    keys = jax.ShapeDtypeStruct((lay.b, MIX_W, lay.n_tot), BF16)
    return pl.pallas_call(
        _qkprep_kernel,
        grid=(t // tm,),
        in_specs=in_specs,
        out_specs=[row, key_t, row, row, key_t, row],
        out_shape=[tok, keys, tok, tok, keys, tok],
        compiler_params=_cparams("parallel"),
        name="qk_prep",
    )(da, mla, *tabs, *consts)


def _softmax_parts(s):
    p = jnp.exp2(s - jnp.max(s, axis=-1, keepdims=True))
    return p, 1.0 / jnp.sum(p, axis=-1, keepdims=True)


def _attn_heads(q, kt_ref, v_ref, nk, diff, lam):
    lane = lax.broadcasted_iota(jnp.int32, (q.shape[0], MIX_W), 1)
    v = v_ref[0, 0:nk, :]
    acc = jnp.zeros((q.shape[0], MIX_W), F32)
    dk = DA_DIM if diff else MLA_HEAD_PAD
    per_head = 2 if diff else 1

    def scores(h):
        return [jnp.dot(q[:, e * dk:(e + 1) * dk], kt_ref[0, e * dk:(e + 1) * dk, 0:nk], preferred_element_type=F32)
                for e in range(per_head * h, per_head * (h + 1))]

    ahead = scores(0)
    for h in range(DA_HEADS):
        s = ahead
        if h + 1 < DA_HEADS:
            ahead = scores(h + 1)
        if diff:
            p0, r0 = _softmax_parts(s[0])
            p1, r1 = _softmax_parts(s[1])
            o = jnp.dot((p0 * r0 - p1 * (r1 * lam)).astype(BF16), v, preferred_element_type=F32)
        else:
            p, r = _softmax_parts(s[0])
            o = jnp.dot(p.astype(BF16), v, preferred_element_type=F32) * r
        in_head = jnp.logical_and(lane >= h * DA_VDIM, lane < (h + 1) * DA_VDIM)
        acc = jnp.where(in_head, o, acc)
    return acc


def _attn_kernel(q_ref, kt_ref, v_ref, lam_ref, gain_ref, g64_ref, o_ref, *, diff, n_ctx, n_tot, ctx_tiles):
    q = q_ref[...]
    lam = lam_ref[...]

    def run(nk):
        o = _attn_heads(q, kt_ref, v_ref, nk, diff, lam)
        if diff:
            o = _group_rms(o, g64_ref[...], 1.0 / DA_VDIM, gain_ref[...])
        o_ref[...] = o.astype(BF16)

    if ctx_tiles:
        is_ctx = pl.program_id(1) < ctx_tiles
        pl.when(is_ctx)(lambda: run(n_ctx))
        pl.when(jnp.logical_not(is_ctx))(lambda: run(n_tot))
    else:
        run(n_tot)


def _attention(lay, q, kt, v, extra, diff, with_ctx, name):
    tq = lay.t
    tiles = lay.seq_tiles if with_ctx else lay.lat_tiles
    off = 0 if with_ctx else lay.ctx_tiles
    v3 = v.reshape(lay.b, lay.n_tot, MIX_W)
    kern = functools.partial(_attn_kernel, diff=diff, n_ctx=lay.n_ctx, n_tot=lay.n_tot,
                             ctx_tiles=lay.ctx_tiles if with_ctx else 0)
    return pl.pallas_call(
        kern,
        grid=(lay.b, tiles),
        in_specs=[
            pl.BlockSpec((tq, MIX_W), lambda b, j: (b * lay.seq_tiles + j + off, 0)),
            pl.BlockSpec((1, MIX_W, lay.n_tot), lambda b, j: (b, 0, 0)),
            pl.BlockSpec((1, lay.n_tot, MIX_W), lambda b, j: (b, 0, 0)),
        ] + [_full(a) for a in extra],
        out_specs=pl.BlockSpec((tq, MIX_W), lambda b, j: (b * tiles + j, 0)),
        out_shape=jax.ShapeDtypeStruct((lay.rows(with_ctx), MIX_W), BF16),
        compiler_params=_cparams("parallel", "parallel"),
        name=name,
    )(q, kt, v3, *extra)


def _chunk_rows(ua_ref, ub_ref):
    n = ua_ref.shape[0] // S5_CHUNK
    parts = []
    for s in range(S5_CHUNK):
        rows = pl.ds(s, n, stride=S5_CHUNK)
        parts += [ua_ref[rows, :], ub_ref[rows, :]]
    return jnp.concatenate(parts, axis=1).astype(BF16)


def _s5_proj_kernel(ua_ref, ub_ref, bre_ref, bim_ref, sre_ref, sim_ref):
    u = _chunk_rows(ua_ref, ub_ref)
    sre_ref[0] = jnp.dot(u, bre_ref[0, 0], preferred_element_type=F32)
    sim_ref[0] = jnp.dot(u, bim_ref[0, 0], preferred_element_type=F32)


def _s5_rec_kernel(sre_ref, sim_ref, are_ref, aim_ref, hre_ref, him_ref, *, n_batch, n_chunks, ctx_chunks):
    rev = pl.program_id(0) == 1
    ar, ai = are_ref[0, 0], aim_ref[0, 0]
    sre, sim, hre, him = sre_ref.at[0], sim_ref.at[0], hre_ref.at[0], him_ref.at[0]

    def step(i, carry):
        hr, hi = carry
        k_rev = jnp.where(i < ctx_chunks, ctx_chunks - 1 - i, n_chunks - 1 + ctx_chunks - i)
        k = jnp.where(rev, k_rev, i)
        rows = pl.ds(k, n_batch, stride=n_chunks)
        hre[rows, :] = hr
        him[rows, :] = hi
        return ar * hr - ai * hi + sre[rows, :], ar * hi + ai * hr + sim[rows, :]

    zero = jnp.zeros((n_batch, 128), F32)
    lax.fori_loop(0, n_chunks, step, (zero, zero), unroll=2)


def _s5_out_kernel(ua_ref, ub_ref, hre_ref, him_ref, m_ref, cre_ref, cim_ref, ya_ref, yb_ref):
    y = jnp.dot(_chunk_rows(ua_ref, ub_ref), m_ref[0, 0], preferred_element_type=F32)
    y = y + _split_dot(hre_ref[0], cre_ref[0, 0]) + _split_dot(him_ref[0], cim_ref[0, 0])
    n = y.shape[0]
    ya, yb = ya_ref.at[0], yb_ref.at[0]
    for s in range(S5_CHUNK):
        rows = pl.ds(s, n, stride=S5_CHUNK)
        ya[rows, :] = y[:, s * MIX_W:s * MIX_W + 128]
        yb[rows, :] = y[:, s * MIX_W + 128:(s + 1) * MIX_W]


def _s5_mats(lam_re, lam_im, log_dt, b_re, b_im, c_re, c_im):
    hp = lax.Precision.HIGHEST
    L, G, P, CH = S5_CHUNK, S5_GROUPS, S5_STATE, S5_CH
    lr, li = lam_re.astype(F32), lam_im.astype(F32)
    dt = jnp.exp(log_dt.astype(F32))[..., None]
    zr, zi = lr * dt, li * dt
    j = jnp.arange(L + 1, dtype=F32)[:, None, None, None]
    mag = jnp.exp(zr[None] * j)
    pw_re, pw_im = mag * jnp.cos(zi[None] * j), mag * jnp.sin(zi[None] * j)
    nr, ni = pw_re[1] - 1.0, pw_im[1]
    den = lr * lr + li * li
    cr, ci = (nr * lr + ni * li) / den, (ni * lr - nr * li) / den
    bre, bim = b_re.astype(F32), b_im.astype(F32)
    bb_re = cr[..., None] * bre - ci[..., None] * bim
    bb_im = cr[..., None] * bim + ci[..., None] * bre
    x_re = pw_re[..., None] * bb_re[None] - pw_im[..., None] * bb_im[None]
    x_im = pw_re[..., None] * bb_im[None] + pw_im[..., None] * bb_re[None]
    cre, cim = c_re.astype(F32), c_im.astype(F32)
    kern = (jnp.einsum('dgcp,jdgpe->dgjce', cre, x_re[:L], precision=hp)
            - jnp.einsum('dgcp,jdgpe->dgjce', cim, x_im[:L], precision=hp))
    def spread_mask(a, b):
        spread = jnp.asarray(np.tile(np.eye(b, dtype=np.float32), (1, G)))
        mask = jnp.asarray(np.kron(np.eye(G, dtype=np.float32), np.ones((a, b), np.float32)))
        return spread, mask

    kt = kern.transpose(0, 2, 1, 4, 3)
    xt_re, xt_im = x_re.transpose(1, 0, 2, 4, 3), x_im.transpose(1, 0, 2, 4, 3)
    pwt_re, pwt_im = pw_re.transpose(1, 0, 2, 3)[:, :, :, :, None], pw_im.transpose(1, 0, 2, 3)[:, :, :, :, None]
    cret, cimt = cre.transpose(0, 1, 3, 2)[:, None], cim.transpose(0, 1, 3, 2)[:, None]
    ca_re, ca_im = cret * pwt_re - cimt * pwt_im, -(cret * pwt_im + cimt * pwt_re)
    s_idx, t_idx = np.arange(L)[:, None], np.arange(L)[None, :]
    k_st, xb_re, xb_im, cq_re, cq_im = [], [], [], [], []
    for d in range(2):
        lag = (t_idx - s_idx) if d == 0 else (s_idx - t_idx)
        k_st.append(jnp.where(jnp.asarray(lag >= 0)[:, :, None, None, None], kt[d][np.clip(lag, 0, L - 1)], 0.0))
        pw = np.arange(L - 1, -1, -1) if d == 0 else np.arange(L)
        xb_re.append(xt_re[d][pw])
        xb_im.append(xt_im[d][pw])
        q = np.arange(1, L + 1) if d == 0 else np.arange(L, 0, -1)
        cq_re.append(ca_re[d][q])
        cq_im.append(ca_im[d][q])
    sp, mk = spread_mask(CH, CH)
    m = jnp.einsum('dstrb,bc->dsrtc', jnp.stack(k_st).reshape(2, L, L, G * CH, CH), sp, precision=hp) * mk[:, None, :]
    m = m.astype(BF16).reshape(2, L * G * CH, L * G * CH)
    sp, mk = spread_mask(CH, P)
    to_b = lambda x: (jnp.einsum('dsrb,bc->dsrc', jnp.stack(x).reshape(2, L, G * CH, P), sp, precision=hp) * mk
                      ).astype(BF16).reshape(2, L * G * CH, G * P)
    sp_c, mk_c = spread_mask(P, CH)
    to_c = lambda x: (jnp.einsum('dtrb,bc->drtc', jnp.stack(x).reshape(2, L, G * P, CH), sp_c, precision=hp)
                      * mk_c[:, None, :]).astype(BF16).reshape(2, G * P, L * G * CH)
    a_re, a_im = pw_re[L].reshape(2, 1, G * P), pw_im[L].reshape(2, 1, G * P)
    return m, to_b(xb_re), to_b(xb_im), to_c(cq_re), to_c(cq_im), a_re, a_im


def _s5_scan(lay, ua, ub, mats, layer):
    m, b_r, b_i, c_r, c_i, a_re, a_im = mats
    n_chunks = lay.n_tot // S5_CHUNK
    rows = lay.b * n_chunks
    tr = min(lay.t, rows)
    tok = tr * S5_CHUNK
    half = MIX_W // 2
    wspec = lambda a: pl.BlockSpec((1, 1) + a.shape[2:], lambda d, i: (layer, d, 0, 0))
    state = jax.ShapeDtypeStruct((2, rows, S5_STATE_W), F32)
    sblk = pl.BlockSpec((1, tr, S5_STATE_W), lambda d, i: (d, i, 0))
    ublk = pl.BlockSpec((tok, half), lambda d, i: (i, 0))
    s_re, s_im = pl.pallas_call(
        _s5_proj_kernel,
        grid=(2, rows // tr),
        in_specs=[ublk, ublk, wspec(b_r), wspec(b_i)],
        out_specs=[sblk, sblk],
        out_shape=[state, state],
        compiler_params=_cparams("parallel", "parallel"),
        name="s5_proj",
    )(ua, ub, b_r, b_i)
    col = pl.BlockSpec((1, rows, 128), lambda d, j: (d, 0, j))
    acol = pl.BlockSpec((1, 1, 1, 128), lambda d, j: (layer, d, 0, j))
    h_re, h_im = pl.pallas_call(
        functools.partial(_s5_rec_kernel, n_batch=lay.b, n_chunks=n_chunks, ctx_chunks=lay.n_ctx // S5_CHUNK),
        grid=(2, S5_STATE_W // 128),
        in_specs=[col, col, acol, acol],
        out_specs=[col, col],
        out_shape=[state, state],
        compiler_params=_cparams("parallel", "parallel"),
        name="s5_rec",
    )(s_re, s_im, a_re, a_im)
    yblk = pl.BlockSpec((1, tok, half), lambda d, i: (d, i, 0))
    yshape = jax.ShapeDtypeStruct((2, lay.b * lay.n_tot, half), F32)
    return pl.pallas_call(
        _s5_out_kernel,
        grid=(2, rows // tr),
        in_specs=[ublk, ublk, sblk, sblk, wspec(m), wspec(c_r), wspec(c_i)],
        out_specs=[yblk, yblk],
        out_shape=[yshape, yshape],
        compiler_params=_cparams("parallel", "parallel"),
        name="s5_out",
    )(ua, ub, h_re, h_im, m, c_r, c_i)


def _s5_glu_kernel(ua_ref, ub_ref, ya_ref, yb_ref, d_ref, w_ref, b_ref, o_ref):
    u = jnp.concatenate([ua_ref[...], ub_ref[...]], axis=1)
    y = d_ref[...] * u + jnp.concatenate([ya_ref[0] + ya_ref[1], yb_ref[0] + yb_ref[1]], axis=1)
    z = 0.5 * y * (1.0 + jnp.tanh(math.sqrt(2.0 / math.pi) * (y + 0.044715 * (y * y * y))))
    gate = _sigmoid(jnp.dot(z.astype(BF16), w_ref[...], preferred_element_type=F32) + b_ref[...])
    o_ref[...] = (z * gate).astype(BF16)


def _s5_glu_call(ua, ub, ya, yb, d, w, bias, tm):
    t, half = ua.shape
    urow = pl.BlockSpec((tm, half), lambda i: (i, 0))
    yrow = pl.BlockSpec((2, tm, half), lambda i: (0, i, 0))
    return pl.pallas_call(
        _s5_glu_kernel,
        grid=(t // tm,),
        in_specs=[urow, urow, yrow, yrow, _full(d), _full(w), _full(bias)],
        out_specs=pl.BlockSpec((tm, MIX_W), lambda i: (i, 0)),
        out_shape=jax.ShapeDtypeStruct((t, MIX_W), BF16),
        compiler_params=_cparams("parallel"),
        name="s5_glu",
    )(ua, ub, ya, yb, d, w, bias)


def _rw_pre_kernel(x_ref, prev_ref, next_ref, mu_ref, g64_ref, kk_g_ref, ka_ref, rk_ref,
                   w0_ref, w1_ref, w2_ref, a0_ref, a1_ref, a2_ref, g1_ref, g2_ref,
                   r_ref, v_ref, kk_ref, lw_ref, kka_ref, km_ref, bon_ref, gate_ref,
                   *, seq_tiles, n_ctx, n_tot):
    x = x_ref[...]
    n = x.shape[0]
    row = lax.broadcasted_iota(jnp.int32, (n, 1), 0)
    pos = (pl.program_id(0) % seq_tiles) * n + row
    left = jnp.where(row == 0, prev_ref[0, 7:8, :], pltpu.roll(x, 1, axis=0))
    left = jnp.where(jnp.logical_or(pos == 0, pos == n_ctx), 0.0, left)
    right = jnp.where(row == n - 1, next_ref[0, 0:1, :], pltpu.roll(x, n - 1, axis=0))
    right = jnp.where(jnp.logical_or(pos == n_ctx - 1, pos == n_tot - 1), 0.0, right)
    x = x + (0.5 * (left + right) - x) * mu_ref[...]
    r, k, v, xd = (x[:, i * MIX_W:(i + 1) * MIX_W] for i in range(4))
    g64 = g64_ref[...]
    kscaled = k * kk_g_ref[...]
    kk = kscaled / jnp.maximum(jnp.sqrt(_split_dot(kscaled * kscaled, g64)), 1e-12)
    xdb = xd.astype(BF16)
    r_ref[...] = r
    v_ref[...] = v
    kk_ref[...] = kk
    km_sum = None
    for d in range(2):
        lo = jnp.tanh(jnp.dot(xdb, w1_ref[d], preferred_element_type=F32))
        w_raw = w0_ref[d] + jnp.dot(lo.astype(BF16), w2_ref[d], preferred_element_type=F32)
        lw_ref[d] = -_sigmoid(w_raw) * math.exp(-0.5)
        ar = jnp.dot(xdb, a1_ref[d], preferred_element_type=F32)
        a = _sigmoid(a0_ref[d] + jnp.dot(ar.astype(BF16), a2_ref[d], preferred_element_type=F32))
        km = k * (1.0 + (a - 1.0) * ka_ref[...])
        kka_ref[d] = kk * a
        km_ref[d] = km
        km_sum = km if km_sum is None else km_sum + km
    bon_ref[...] = _split_dot(r * km_sum * rk_ref[...], g64) * v
    gr = _sigmoid(jnp.dot(xdb, g1_ref[...], preferred_element_type=F32))
    gate_ref[...] = jnp.dot(gr.astype(BF16), g2_ref[...], preferred_element_type=F32)


def _rw_pre_call(lay, rw, consts):
    t = rw.shape[0]
    tr, seq_tiles = _seq_tile(lay)
    nt = t // tr
    g8 = tr // 8
    rw8 = rw.reshape(t // 8, 8, _RW_W)
    row = pl.BlockSpec((tr, MIX_W), lambda i: (i, 0))
    row2 = pl.BlockSpec((2, tr, MIX_W), lambda i: (0, i, 0))
    sd = jax.ShapeDtypeStruct((t, MIX_W), F32)
    sd2 = jax.ShapeDtypeStruct((2, t, MIX_W), F32)
    return pl.pallas_call(
        functools.partial(_rw_pre_kernel, seq_tiles=seq_tiles, n_ctx=lay.n_ctx, n_tot=lay.n_tot),
        grid=(nt,),
        in_specs=[pl.BlockSpec((tr, _RW_W), lambda i: (i, 0)),
                  pl.BlockSpec((1, 8, _RW_W), lambda i: (jnp.maximum(i * g8 - 1, 0), 0, 0)),
                  pl.BlockSpec((1, 8, _RW_W), lambda i: (jnp.minimum((i + 1) * g8, t // 8 - 1), 0, 0))]
                 + [_full(a) for a in consts],
        out_specs=[row, row, row, row2, row2, row2, row, row],
        out_shape=[sd, sd, sd, sd2, sd2, sd2, sd, sd],
        compiler_params=_cparams("parallel"),
        name="rwkv_pre",
    )(rw, rw8, rw8, *consts)


def _head_masks(shape, lane_axis, seg):
    lane = lax.broadcasted_iota(jnp.int32, shape, lane_axis)
    return [jnp.logical_and(lane >= h * seg, lane < (h + 1) * seg) for h in range(RW_HEADS)]


def _rw_prep_kernel(*refs, rev):
    tiles = [_rw_prep_tile(sub, *refs, rev=rev) for sub in range(RW_PREP_TILES)]
    while tiles:
        tiles = [t for t in tiles if next(t, None) is not None]


def _rw_prep_tile(sub, r_ref, kk_ref, v_ref, lw_ref, ka_ref, km_ref, perm_ref, permt_ref, g_ref, eye_ref,
                  br_ref, ck_ref, uvt_ref, y0_ref, pc_ref, *, rev):
    C, NC = RW_CHUNK, RW_TILE // RW_CHUNK
    perm, permt, g64, eye4 = perm_ref[...], permt_ref[...], g_ref[...], eye_ref[...]
    tok = slice(sub * RW_TILE, (sub + 1) * RW_TILE)
    nat = jnp.concatenate([r_ref[0, tok, :], kk_ref[0, tok, :], v_ref[0, tok, :],
                           lw_ref[0, 0, tok, :], ka_ref[0, 0, tok, :], km_ref[0, 0, tok, :]], axis=1)
    hi = nat.astype(BF16)
    lo = (nat - hi.astype(F32)).astype(BF16)
    pm = jnp.dot(perm, jnp.concatenate([hi, lo], axis=0), preferred_element_type=F32)
    r, kk, v, lw, ka, km = (pm[:, i * MIX_W:(i + 1) * MIX_W] for i in range(6))
    slab = lambda x, j: x[j * NC:(j + 1) * NC, :]
    order = list(range(C))[::-1] if rev else list(range(C))
    pos = {j: i for i, j in enumerate(order)}
    cum, run = {}, None
    for j in order:
        run = slab(lw, j) if run is None else run + slab(lw, j)
        cum[j] = run
    tot = run
    yield True
    bh, ch, kh, rh, cp, kp, vv = {}, {}, {}, {}, {}, {}, {}
    for j in range(C):
        e_inv, e_end = jnp.exp(-cum[j]), jnp.exp(tot - cum[j])
        bh[j] = -slab(kk, j) * jnp.exp(cum[j] - slab(lw, j))
        ch[j], kh[j] = slab(ka, j) * e_inv, slab(km, j) * e_inv
        rh[j] = slab(r, j) * jnp.exp(cum[j])
        cp[j], kp[j] = slab(ka, j) * e_end, slab(km, j) * e_end
        vv[j] = slab(v, j)
    strict = [(t, s) for t in order for s in order if pos[s] < pos[t]]
    incl = [(t, s) for t in order for s in order if pos[s] <= pos[t]]
    def head_dots(lhs, rhs, pairs):
        prods = jnp.concatenate([lhs[t] * rhs[s] for t, s in pairs], axis=0).astype(BF16)
        gram = jnp.dot(prods, g64, preferred_element_type=F32)
        return {p: gram[i * NC:(i + 1) * NC, :] for i, p in enumerate(pairs)}

    yield True
    acb = head_dots(bh, ch, strict)
    yield True
    akb = head_dots(bh, kh, strict)
    yield True
    mcr = head_dots(rh, ch, incl)
    yield True
    mkr = head_dots(rh, kh, incl)
    yield True
    bt, u0 = {}, {}
    for t in order:
        b_acc, u_acc = bh[t], jnp.zeros_like(bh[t])
        for s in order:
            if pos[s] < pos[t]:
                b_acc = b_acc + acb[(t, s)] * bt[s]
                u_acc = u_acc + akb[(t, s)] * vv[s] + acb[(t, s)] * u0[s]
        bt[t], u0[t] = b_acc, u_acc
        yield True
    rt, y0 = {}, {}
    for t in order:
        r_acc, y_acc = rh[t], jnp.zeros_like(rh[t])
        for s in order:
            if pos[s] <= pos[t]:
                r_acc = r_acc + mcr[(t, s)] * bt[s]
                y_acc = y_acc + mcr[(t, s)] * u0[s] + mkr[(t, s)] * vv[s]
        rt[t], y0[t] = r_acc, y_acc
        yield True
    stackp = lambda dct: jnp.concatenate([dct[j] for j in range(C)], axis=0)
    b16 = lambda x: x.astype(BF16)
    y0p, u0p = stackp(y0), stackp(u0)
    y0h, u0h = b16(y0p), b16(u0p)
    cat = jnp.concatenate([b16(stackp(bt)), b16(stackp(rt)), b16(stackp(cp)), b16(stackp(kp)),
                           y0h, b16(y0p - y0h.astype(F32)), u0h, b16(u0p - u0h.astype(F32)), b16(stackp(vv))], axis=1)
    natural = jnp.dot(permt, cat, preferred_element_type=F32)
    seg = lambda i: natural[:, i * MIX_W:(i + 1) * MIX_W]
    yield True
    btn, rtn, cpn, kpn = b16(seg(0)), b16(seg(1)), b16(seg(2)), b16(seg(3))
    y0_ref[0, tok, :] = seg(4) + seg(5)
    u0h_n, u0l_n, vn = b16(seg(6)), b16(seg(7)), b16(seg(8))
    hm = _head_masks((C, MIX_W), 1, RW_DIM)
    zero = jnp.zeros((C, MIX_W), BF16)
    zh, zl = [], []
    for c in range(NC):
        rows = slice(c * C, (c + 1) * C)
        br_ref[0, sub * NC + c, 0:C, :] = btn[rows]
        br_ref[0, sub * NC + c, C:2 * C, :] = rtn[rows]
        ck_ref[0, sub * NC + c, 0:C, :] = cpn[rows]
        ck_ref[0, sub * NC + c, C:2 * C, :] = kpn[rows]
        for h in range(RW_HEADS):
            zh += [jnp.where(hm[h], u0h_n[rows], zero), jnp.where(hm[h], vn[rows], zero)]
            zl += [jnp.where(hm[h], u0l_n[rows], zero), zero]
    z = jnp.concatenate([jnp.concatenate(zh, axis=0), jnp.concatenate(zl, axis=0)], axis=1)
    uvt = lax.dot_general(eye4, z, _NT, preferred_element_type=F32)
    for c in range(NC):
        uvt_ref[0, sub * NC + c] = uvt[:, c * 2 * C * RW_HEADS:(c + 1) * 2 * C * RW_HEADS]
    pc_ref[0, sub * NC:(sub + 1) * NC, :] = jnp.exp(tot)


def _rw_prep_call(lay, shared, perdir, consts, rev):
    b, n_tot, tt = lay.b, lay.n_tot, RW_TILE * RW_PREP_TILES
    assert n_tot % tt == 0
    nck = n_tot // RW_CHUNK
    cpt = tt // RW_CHUNK
    d = 1 if rev else 0
    sh = [a.reshape(b, n_tot, MIX_W) for a in shared]
    pd = [a.reshape(2, b, n_tot, MIX_W) for a in perdir]
    tok = pl.BlockSpec((1, tt, MIX_W), lambda i, j: (i, j, 0))
    tok_d = pl.BlockSpec((1, 1, tt, MIX_W), lambda i, j: (d, i, j, 0))
    rows32 = pl.BlockSpec((1, cpt, 2 * RW_CHUNK, MIX_W), lambda i, j: (i, j, 0, 0))
    return pl.pallas_call(
        functools.partial(_rw_prep_kernel, rev=rev),
        grid=(b, n_tot // tt),
        in_specs=[tok] * 3 + [tok_d] * 3 + [_full(a) for a in consts],
        out_specs=[rows32, rows32,
                   pl.BlockSpec((1, cpt, RW_DIM, 2 * RW_CHUNK * RW_HEADS), lambda i, j: (i, j, 0, 0)),
                   tok,
                   pl.BlockSpec((1, cpt, MIX_W), lambda i, j: (i, j, 0))],
        out_shape=[jax.ShapeDtypeStruct((b, nck, 2 * RW_CHUNK, MIX_W), BF16),
                   jax.ShapeDtypeStruct((b, nck, 2 * RW_CHUNK, MIX_W), BF16),
                   jax.ShapeDtypeStruct((b, nck, RW_DIM, 2 * RW_CHUNK * RW_HEADS), F32),
                   jax.ShapeDtypeStruct((b, n_tot, MIX_W), F32),
                   jax.ShapeDtypeStruct((b, nck, MIX_W), F32)],
        compiler_params=_cparams("parallel", "parallel"),
        name="rwkv_prep_rev" if rev else "rwkv_prep_fwd",
    )(*sh, *pd, *consts)


def _rw_scan_kernel(brf, ckf, uvtf, pcf, brr, ckr, uvtr, pcr, ytf_ref, ytr_ref, s_scr, *, n_batch):
    @pl.when(pl.program_id(0) == 0)
    def _():
        s_scr[...] = jnp.zeros_like(s_scr)

    cpt = RW_TILE // RW_CHUNK
    hm = _head_masks((2 * RW_CHUNK, MIX_W), 1, RW_DIM)
    lane = lax.broadcasted_iota(jnp.int32, (RW_DIM, 2 * RW_CHUNK * RW_HEADS), 1)
    is_u = (lane & (2 * RW_CHUNK - 1)) < RW_CHUNK
    per_head = lambda x: jnp.concatenate([jnp.where(m, x, jnp.zeros_like(x)) for m in hm], axis=0)

    def refs_of(p, c):
        d, b = divmod(p, n_batch)
        refs = (brf, ckf, uvtf, pcf, ytf_ref) if d == 0 else (brr, ckr, uvtr, pcr, ytr_ref)
        return refs, b, (c if d == 0 else cpt - 1 - c)

    def step(c, carry):
        lhs = []
        for p in range(2 * n_batch):
            (br_ref, _, uvt_ref, _, yt_ref), b, cc = refs_of(p, c)
            s = s_scr[p]
            shi = s.astype(BF16)
            slo = (s - shi.astype(F32)).astype(BF16)
            w2 = lax.dot_general(jnp.concatenate([shi, slo], axis=0), per_head(br_ref[b, cc]), _NT, preferred_element_type=F32)
            w = w2[:RW_DIM] + w2[RW_DIM:]
            yt_ref[b, cc] = w
            uvt = uvt_ref[b, cc]
            lhs.append(jnp.where(is_u, w + uvt, uvt).astype(BF16))
        for p in range(2 * n_batch):
            (_, ck_ref, _, pc_ref, _), b, cc = refs_of(p, c)
            s_scr[p] = (s_scr[p] * pc_ref[b, pl.ds(cc, 1), :]
                        + jnp.dot(lhs[p], per_head(ck_ref[b, cc]), preferred_element_type=F32))
        return carry

    lax.fori_loop(0, cpt, step, 0)


def _rw_scan_call(lay, fwd, rev):
    b, n_tot, tt = lay.b, lay.n_tot, RW_TILE
    assert lay.n_ctx % tt == 0 and lay.n_lat % tt == 0
    nt, ct = n_tot // tt, lay.n_ctx // tt
    cpt = tt // RW_CHUNK
    rev_tile = lambda i: jnp.where(i < ct, ct - 1 - i, nt - 1 + ct - i)

    def specs(tile):
        return [pl.BlockSpec((b, cpt, 2 * RW_CHUNK, MIX_W), lambda i: (0, tile(i), 0, 0)),
                pl.BlockSpec((b, cpt, 2 * RW_CHUNK, MIX_W), lambda i: (0, tile(i), 0, 0)),
                pl.BlockSpec((b, cpt, RW_DIM, 2 * RW_CHUNK * RW_HEADS), lambda i: (0, tile(i), 0, 0)),
                pl.BlockSpec((b, cpt, MIX_W), lambda i: (0, tile(i), 0))]

    ident = lambda i: i
    yt = jax.ShapeDtypeStruct((b, n_tot // RW_CHUNK, RW_DIM, 2 * RW_CHUNK * RW_HEADS), F32)
    return pl.pallas_call(
        functools.partial(_rw_scan_kernel, n_batch=b),
        grid=(nt,),
        in_specs=specs(ident) + specs(rev_tile),
        out_specs=[specs(ident)[2], specs(rev_tile)[2]],
        out_shape=[yt, yt],
        scratch_shapes=[pltpu.VMEM((2 * b, RW_DIM, MIX_W), F32)],
        compiler_params=_cparams("arbitrary"),
        name="rwkv_scan",
    )(*fwd, *rev)


def _rw_fin_kernel(ytf_ref, ytr_ref, y0f_ref, y0r_ref, bon_ref, gate_ref, asel_ref, g64_ref, lng_ref, lnb_ref, o_ref):
    cpt = RW_TILE // RW_CHUNK
    asel = asel_ref[...]
    width = cpt * 2 * RW_CHUNK * RW_HEADS
    lane = lax.broadcasted_iota(jnp.int32, (RW_DIM, width), 1)
    lane_head = jnp.bitwise_and(jnp.right_shift(lane, 5), RW_HEADS - 1)

    def base(yt_ref, sub):
        yt = jnp.concatenate([yt_ref[0, sub * cpt + c] for c in range(cpt)], axis=1).astype(BF16)
        rows = jnp.concatenate([jnp.where(lane_head == h, yt, jnp.zeros_like(yt)) for h in range(RW_HEADS)], axis=0)
        return lax.dot_general(asel, rows, _NT, preferred_element_type=F32)

    g64 = g64_ref[...]
    bases = [(base(ytf_ref, sub), base(ytr_ref, sub)) for sub in range(RW_PREP_TILES)]
    for sub, (yb_f, yb_r) in enumerate(bases):
        tok = slice(sub * RW_TILE, (sub + 1) * RW_TILE)
        y = yb_f + y0f_ref[0, tok, :] + yb_r + y0r_ref[0, tok, :]
        mean = _split_dot(y, g64) * (1.0 / RW_DIM)
        c = y - mean
        var = _split_dot(c * c, g64) * (1.0 / RW_DIM)
        out = c * lax.rsqrt(var + RW_LN_EPS) * lng_ref[...] + lnb_ref[...] + bon_ref[tok, :]
        o_ref[tok, :] = (out * gate_ref[tok, :]).astype(BF16)


def _rw_fin_call(lay, ytf, ytr, y0f, y0r, bon, gate, consts):
    b, n_tot, tt = lay.b, lay.n_tot, RW_TILE * RW_PREP_TILES
    nt = n_tot // tt
    cpt = tt // RW_CHUNK
    ytb = pl.BlockSpec((1, cpt, RW_DIM, 2 * RW_CHUNK * RW_HEADS), lambda i, j: (i, j, 0, 0))
    y0b = pl.BlockSpec((1, tt, MIX_W), lambda i, j: (i, j, 0))
    row = pl.BlockSpec((tt, MIX_W), lambda i, j: (i * nt + j, 0))
    return pl.pallas_call(
        _rw_fin_kernel,
        grid=(b, nt),
        in_specs=[ytb, ytb, y0b, y0b, row, row] + [_full(a) for a in consts],
        out_specs=row,
        out_shape=jax.ShapeDtypeStruct((b * n_tot, MIX_W), BF16),
        compiler_params=_cparams("parallel", "parallel"),
        name="rwkv_finish",
    )(ytf, ytr, y0f, y0r, bon, gate, *consts)


def _rw_constants():
    c, nc = RW_CHUNK, RW_TILE // RW_CHUNK
    perm = np.zeros((RW_TILE, RW_TILE), np.float32)
    for ci in range(nc):
        for j in range(c):
            perm[j * nc + ci, ci * c + j] = 1.0
    lane = np.arange(MIX_W) % RW_DIM
    eye4 = (lane[None, :] == np.arange(RW_DIM)[:, None]).astype(np.float32)
    lanes = np.arange(nc * 2 * c * RW_HEADS)
    lane_chunk, lane_tok = lanes // (2 * c * RW_HEADS), lanes % (2 * c)
    t = np.arange(RW_TILE)
    asel = ((lane_chunk[None, :] == (t // c)[:, None]) & (lane_tok[None, :] == (c + t % c)[:, None])).astype(np.float32)
    as16 = lambda a: jnp.asarray(a, BF16)
    twice = lambda a: np.concatenate([a, a], axis=1)
    return as16(twice(perm)), as16(perm.T), as16(twice(eye4)), as16(asel)


def _rwkv_branch(lay, rw, pre_consts, g64, lng, lnb):
    r_, v_, kk_, lw_, kka_, km_, bon, gate = _rw_pre_call(lay, rw, pre_consts)
    perm, permt, eye4, asel = _rw_constants()
    prep_consts = (perm, permt, g64, eye4)
    fwd = _rw_prep_call(lay, (r_, kk_, v_), (lw_, kka_, km_), prep_consts, False)
    rev = _rw_prep_call(lay, (r_, kk_, v_), (lw_, kka_, km_), prep_consts, True)
    pick = lambda o: (o[0], o[1], o[2], o[4])
    ytf, ytr = _rw_scan_call(lay, pick(fwd), pick(rev))
    return _rw_fin_call(lay, ytf, ytr, fwd[3], rev[3], bon, gate, (asel, g64, lng, lnb))


def _merge_kernel(x_ref, modb_ref, modc_ref, g_ref, wg_ref, ya_ref, yb_ref, yc_ref, yd_ref, wb_ref, wo_ref, o_ref,
                  *, ctx_rows, tiles_per_seq):
    x = x_ref[...]
    mrow = lambda r: _mod_row(modb_ref, modc_ref, r, x.shape[0], ctx_rows, tiles_per_seq)
    h = _modulate(x, g_ref[...], mrow(0), mrow(1)).astype(BF16)
    merged = None
    for i, y_ref in enumerate((ya_ref, yb_ref, yc_ref, yd_ref)):
        gate = _sigmoid(jnp.dot(h, wg_ref[:, i * D_MODEL:(i + 1) * D_MODEL], preferred_element_type=F32))
        term = gate * jnp.dot(y_ref[...], wb_ref[i], preferred_element_type=F32)
        merged = term if merged is None else merged + term
    out = jnp.dot(merged.astype(BF16), wo_ref[...], preferred_element_type=F32)
    o_ref[...] = x + mrow(2) * out


def _merge_call(lay, with_ctx, x_all, mod, g, w_gate, ya, yb, yc, yd, w_branch, w_out):
    if with_ctx:
        tm, tps = _seq_tile(lay)
        src, n_tiles, ctx_rows = (lambda i: i), lay.b * tps, lay.n_ctx
    else:
        tm, tps = lay.t, lay.lat_tiles
        src, n_tiles, ctx_rows = lay.src_tile(False), lay.n_tiles(False), 0
    full_row = lambda w: pl.BlockSpec((tm, w), lambda i: (src(i), 0))
    out_row = lambda w: pl.BlockSpec((tm, w), lambda i: (i, 0))
    return pl.pallas_call(
        functools.partial(_merge_kernel, ctx_rows=ctx_rows, tiles_per_seq=tps),
        grid=(n_tiles,),
        in_specs=[full_row(D_MODEL), pl.BlockSpec((1, 6, D_MODEL), lambda i: (i // tps, 0, 0)),
                  pl.BlockSpec((1, 6, D_MODEL), lambda i: (lay.b, 0, 0)), _full(g), _full(w_gate),
                  out_row(MIX_W), full_row(MIX_W), out_row(MIX_W), full_row(MIX_W), _full(w_branch), _full(w_out)],
        out_specs=out_row(D_MODEL),
        out_shape=jax.ShapeDtypeStruct((lay.rows(with_ctx), D_MODEL), F32),
        compiler_params=_cparams("parallel"),
        name="merge_out",
    )(x_all, mod, mod, g, w_gate, ya, yb, yc, yd, w_branch, w_out)


def _router_kernel(x_ref, modb_ref, modc_ref, g_ref, wh_ref, wl_ref, bias_ref, f_ref, comb_ref, gid_ref,
                   *, ctx_rows, tiles_per_seq):
    x = x_ref[...]
    mrow = lambda r: _mod_row(modb_ref, modc_ref, r, x.shape[0], ctx_rows, tiles_per_seq)
    f = _modulate(x, g_ref[...], mrow(3), mrow(4))
    fh = f.astype(BF16)
    f_ref[...] = fh
    fl = (f - fh.astype(F32)).astype(BF16)
    nt = (((1,), (1,)), ((), ()))
    wh, wl = wh_ref[...], wl_ref[...]
    logits = (lax.dot_general(wh, fh, nt, preferred_element_type=F32)
              + lax.dot_general(wh, fl, nt, preferred_element_type=F32)
              + lax.dot_general(wl, fh, nt, preferred_element_type=F32))
    scores = _sigmoid(logits)
    biased = scores + bias_ref[...]
    sc = [scores[e:e + 1, :] for e in range(N_EXPERTS)]
    bi = [biased[e:e + 1, :] for e in range(N_EXPERTS)]
    group_score = []
    for g in range(N_GROUPS):
        a, b, c, d = bi[4 * g:4 * g + 4]
        m1, n1, m2, n2 = jnp.maximum(a, b), jnp.minimum(a, b), jnp.maximum(c, d), jnp.minimum(c, d)
        group_score.append(jnp.maximum(m1, m2) + jnp.maximum(jnp.minimum(m1, m2), jnp.maximum(n1, n2)))

    def first_argmax(vals):
        top = functools.reduce(jnp.maximum, vals)
        seen, hot = None, []
        for v in vals:
            h = v == top
            if seen is not None:
                h = jnp.logical_and(h, jnp.logical_not(seen))
            seen = h if seen is None else jnp.logical_or(seen, h)
            hot.append(h)
        return hot

    in_group = first_argmax(group_score)
    masked = [jnp.where(in_group[e // EXPERTS_PER_GROUP], bi[e], -jnp.inf) for e in range(N_EXPERTS)]
    hot1 = first_argmax(masked)
    hot2 = first_argmax([jnp.where(h, -jnp.inf, v) for h, v in zip(hot1, masked)])
    w1 = functools.reduce(jnp.add, [jnp.where(h, s, 0.0) for h, s in zip(hot1, sc)])
    w2 = functools.reduce(jnp.add, [jnp.where(h, s, 0.0) for h, s in zip(hot2, sc)])
    inv_tot = 1.0 / (w1 + w2)
    for e in range(N_EXPERTS):
        comb_ref[e:e + 1, :] = (jnp.where(hot1[e], w1, 0.0) + jnp.where(hot2[e], w2, 0.0)) * inv_tot
    gid_ref[...] = functools.reduce(jnp.add, [jnp.where(in_group[g], g, 0) for g in range(1, N_GROUPS)])


def _router_call(lay, with_ctx, x, mod, g, wh, wl, bias):
    t = x.shape[0]
    if with_ctx:
        (tm, tps), ctx_rows = _seq_tile(lay), lay.n_ctx
    else:
        pair = 2 if lay.lat_tiles % 2 == 0 else 1
        tm, tps, ctx_rows = pair * lay.t, lay.lat_tiles // pair, 0
    return pl.pallas_call(
        functools.partial(_router_kernel, ctx_rows=ctx_rows, tiles_per_seq=tps),
        grid=(t // tm,),
        in_specs=[pl.BlockSpec((tm, D_MODEL), lambda i: (i, 0)),
                  pl.BlockSpec((1, 6, D_MODEL), lambda i: (i // tps, 0, 0)),
                  pl.BlockSpec((1, 6, D_MODEL), lambda i: (lay.b, 0, 0)), _full(g), _full(wh), _full(wl), _full(bias)],
        out_specs=[pl.BlockSpec((tm, D_MODEL), lambda i: (i, 0)), pl.BlockSpec((N_EXPERTS, tm), lambda i: (0, i)),
                   pl.BlockSpec((1, tm), lambda i: (0, i))],
        out_shape=[jax.ShapeDtypeStruct((t, D_MODEL), BF16), jax.ShapeDtypeStruct((N_EXPERTS, t), F32),
                   jax.ShapeDtypeStruct((1, t), jnp.int32)],
        compiler_params=_cparams("parallel"),
        name="moe_router",
    )(x, mod, mod, g, wh, wl, bias)


def _moe_plan(gid, n_tiles, tm):
    g = gid.reshape(n_tiles, tm)
    onehot = (g[..., None] == jnp.arange(N_GROUPS, dtype=jnp.int32)).astype(jnp.int32)
    rank = jnp.cumsum(onehot, axis=1) - onehot
    counts = jnp.sum(onehot, axis=1)
    padded = (counts + 15) // 16 * 16
    offs = jnp.cumsum(padded, axis=1) - padded
    pos = jnp.sum(onehot * (offs[:, None, :] + rank), axis=-1)
    n_over = (jnp.maximum(padded - MOE_BLOCK, 0) + MOE_OVER - 1) // MOE_OVER
    return pos.astype(jnp.int32), offs.astype(jnp.int32), n_over.astype(jnp.int32)


def _moe_kernel(offs_ref, nover_ref, f_ref, posr_ref, posc_ref, comb_ref, wg_ref, wu_ref, wd_ref, x_ref, modb_ref,
                modc_ref, o_ref, xs_scr, cs_scr, ys_scr, *, ctx_rows, tiles_per_seq):
    i, e = pl.program_id(0), pl.program_id(1)
    n_slots, tm = xs_scr.shape[0], f_ref.shape[0]
    n_live = min(n_slots, -(-(tm + 16 * N_GROUPS) // 256) * 256)

    @pl.when(e == 0)
    def _():
        slot = lax.broadcasted_iota(jnp.int32, (n_live, tm), 0)
        place = (slot == posr_ref[0]).astype(BF16)
        xs_scr[0:n_live, :] = jnp.dot(place, f_ref[...], preferred_element_type=F32).astype(BF16)
        xs_scr[n_live:n_slots, :] = jnp.zeros((n_slots - n_live, D_MODEL), BF16)
        cs_scr[0:n_live, :] = _split_dot_rhs(place, comb_ref[...])
        cs_scr[n_live:n_slots, :] = jnp.zeros((n_slots - n_live, N_EXPERTS), F32)
        ys_scr[...] = jnp.zeros_like(ys_scr)

    grp = lax.shift_right_logical(e, 2)
    start = offs_ref[i, grp]
    lane = lax.broadcasted_iota(jnp.int32, (1, N_EXPERTS), 1)

    def run(rows):
        xb = xs_scr[rows, :]
        gate = jnp.dot(xb, wg_ref[0], preferred_element_type=F32)
        up = jnp.dot(xb, wu_ref[0], preferred_element_type=F32)
        act = (gate * _sigmoid(gate) * up).astype(BF16)
        down = jnp.dot(act, wd_ref[0], preferred_element_type=F32)
        c_e = jnp.sum(jnp.where(lane == e, cs_scr[rows, :], 0.0), axis=1, keepdims=True)
        ys_scr[rows, :] += c_e * down

    run(pl.ds(pl.multiple_of(start, 16), MOE_BLOCK))

    def overflow(k, carry):
        run(pl.ds(pl.multiple_of(start + MOE_BLOCK + k * MOE_OVER, 16), MOE_OVER))
        return carry

    lax.fori_loop(0, nover_ref[i, grp], overflow, 0)

    @pl.when(e == N_EXPERTS - 1)
    def _():
        slot = lax.broadcasted_iota(jnp.int32, (tm, n_live), 1)
        fetch = (slot == posc_ref[...]).astype(BF16)
        y = jnp.dot(fetch, ys_scr[0:n_live, :].astype(BF16), preferred_element_type=F32)
        res_gate = modb_ref[0, 5:6, :]
        if ctx_rows:
            row = lax.broadcasted_iota(jnp.int32, y.shape, 0)
            first = i % tiles_per_seq == 0
            res_gate = jnp.where(jnp.logical_and(first, row < ctx_rows), modc_ref[0, 5:6, :], res_gate)
        o_ref[...] = x_ref[...] + res_gate * y


def _moe_call(lay, with_ctx, f, comb, gid, wg, wu, wd, x, mod):
    t = f.shape[0]
    seq = lay.n_tot if with_ctx else lay.n_lat
    tm = MOE_TILE if seq % MOE_TILE == 0 else math.gcd(seq, 1024)
    tps, n_tiles = seq // tm, t // tm
    ctx_rows = lay.n_ctx if with_ctx else 0
    assert ctx_rows <= tm
    n_slots = -(-(tm + 16 * N_GROUPS + MOE_BLOCK + MOE_OVER) // 256) * 256
    pos, offs, n_over = _moe_plan(gid, n_tiles, tm)
    wspec = lambda a: pl.BlockSpec((1,) + a.shape[1:], lambda i, e, *_: (e, 0, 0))
    tok = lambda w: pl.BlockSpec((tm, w), lambda i, e, *_: (i, 0))
    grid_spec = pltpu.PrefetchScalarGridSpec(
        num_scalar_prefetch=2,
        grid=(n_tiles, N_EXPERTS),
        in_specs=[tok(D_MODEL), pl.BlockSpec((1, 1, tm), lambda i, e, *_: (i, 0, 0)), tok(1), tok(N_EXPERTS),
                  wspec(wg), wspec(wu), wspec(wd), tok(D_MODEL),
                  pl.BlockSpec((1, 6, D_MODEL), lambda i, e, *_: (i // tps, 0, 0)),
                  pl.BlockSpec((1, 6, D_MODEL), lambda i, e, *_: (lay.b, 0, 0))],
        out_specs=tok(D_MODEL),
        scratch_shapes=[pltpu.VMEM((n_slots, D_MODEL), BF16), pltpu.VMEM((n_slots, N_EXPERTS), F32),
                        pltpu.VMEM((n_slots, D_MODEL), F32)])
    return pl.pallas_call(
        functools.partial(_moe_kernel, ctx_rows=ctx_rows, tiles_per_seq=tps),
        grid_spec=grid_spec,
        out_shape=jax.ShapeDtypeStruct((t, D_MODEL), F32),
        compiler_params=_cparams("parallel", "arbitrary"),
        name="moe_experts",
    )(offs, n_over, f, pos.reshape(n_tiles, 1, tm), pos.reshape(t, 1), comb, wg, wu, wd, x, mod, mod)


def _block_ones(n, group):
    i = np.arange(n) // group
    return jnp.asarray(i[:, None] == i[None, :], dtype=BF16)


def _rope_tables(n_ctx, n_lat):
    rows = n_lat // GRID_W
    row = jnp.repeat(jnp.arange(rows, dtype=F32), GRID_W)
    col = jnp.tile(jnp.arange(GRID_W, dtype=F32), rows)

    def angles(rot_dim):
        n_freq = rot_dim // 4
        inv_freq = ROPE_BASE ** (-jnp.arange(n_freq, dtype=F32) / n_freq)
        ang = jnp.concatenate([row[:, None] * inv_freq, col[:, None] * inv_freq], axis=-1)
        return jnp.cos(ang), jnp.sin(ang)

    c, s = angles(DA_DIM)
    cda = jnp.tile(jnp.concatenate([c, c], -1), (1, 2 * DA_HEADS))
    sda = jnp.tile(jnp.concatenate([-s, s], -1), (1, 2 * DA_HEADS))
    c, s = angles(MLA_ROPE)
    one = jnp.ones((n_lat, MLA_NOPE), F32)
    pad = MLA_HEAD_PAD - MLA_NOPE - MLA_ROPE
    cml = jnp.tile(jnp.concatenate([one, c, c, jnp.ones((n_lat, pad), F32)], -1), (1, MLA_HEADS))
    sml = jnp.tile(jnp.concatenate([0 * one, -s, s, jnp.zeros((n_lat, pad), F32)], -1), (1, MLA_HEADS))
    ident = lambda t, v: jnp.concatenate([jnp.full((n_ctx, MIX_W), v, F32), t], axis=0)
    return ident(cda, 1.0), ident(sda, 0.0), ident(cml, 1.0), ident(sml, 0.0)


def _pad_heads(w, n_heads, src_w, lo, hi, dst_w=MLA_HEAD_PAD):
    w = w.reshape(w.shape[0], n_heads, src_w)[:, :, lo:hi]
    w = jnp.pad(w, ((0, 0), (0, 0), (0, dst_w - (hi - lo))))
    return w.reshape(w.shape[0], n_heads * dst_w)


def _mix_weight(w_in_l):
    w = w_in_l
    kr = w[:, 1344:1360]
    z = lambda n: jnp.zeros((D_MODEL, n), w.dtype)
    kr_wide = jnp.concatenate([jnp.concatenate([z(MLA_NOPE), kr, z(MLA_HEAD_PAD - MLA_NOPE - MLA_ROPE)], 1)] * MLA_HEADS, 1)
    return jnp.concatenate([w[:, 0:1024], w[:, 1024:1216], z(64), w[:, 1216:1344], kr_wide, w[:, 1360:2384]], axis=1).astype(BF16)


def kernel(x, c, ctx, c_ctx, w_ada, b_ada, norm_mix_g, norm_ffn_g, w_in, da_qk_norm_g, da_lambda, da_subln_g, s5_lam_re, s5_lam_im, s5_log_dt, s5_b_re, s5_b_im, s5_c_re, s5_c_im, s5_d, s5_w_glu, s5_b_glu, mla_cq_norm_g, mla_ckv_norm_g, mla_w_uq, mla_w_ukv, mla_qk_norm_g, rw_mu, rw_w0, rw_w1, rw_w2, rw_a0, rw_a1, rw_a2, rw_g1, rw_g2, rw_k_k, rw_k_a, rw_r_k, rw_ln_g, rw_ln_b, w_branch, w_out, router_w, router_bias, exp_w_gate, exp_w_up, exp_w_down):
    b, n_lat, dm = x.shape
    n_ctx = ctx.shape[1]
    depth = w_ada.shape[0]
    assert dm == D_MODEL
    lay = _Layout(b, n_ctx, n_lat)
    t_all = b * lay.n_tot
    tm_big = 2 * lay.t

    g32 = _block_ones(MIX_W, DA_DIM)
    g64 = _block_ones(MIX_W, RW_DIM)
    tabs = _rope_tables(n_ctx, n_lat)
    row = lambda v: v.reshape(1, -1).astype(F32)
    bf = lambda a: a.astype(BF16)

    cc = jnp.zeros((16, dm), F32).at[:b].set(c).at[b].set(c_ctx)
    mod_all = _ada_call(cc, w_ada, b_ada)
    x_all = jnp.concatenate([ctx, x], axis=1).reshape(t_all, dm)

    s5_mats = jax.vmap(_s5_mats)(s5_lam_re, s5_lam_im, s5_log_dt, s5_b_re, s5_b_im, s5_c_re, s5_c_im)

    wr_hi = router_w.T.astype(BF16)
    wr_lo = (router_w.T - wr_hi.astype(F32)).astype(BF16)
    r_bias = router_bias.reshape(N_EXPERTS, 1).astype(F32)

    for l in range(depth):
        need_ctx = l < depth - 1
        lambda_init = 0.8 - 0.6 * math.exp(-0.3 * l)
        mod = mod_all[l, :b + 1].reshape(b + 1, 6, dm)
        g_mix = row(norm_mix_g[l])
        da, s5a, s5b, mla, rw = _inproj_call(lay, x_all, mod, g_mix, _mix_weight(w_in[l]))

        log2e = math.log2(math.e)
        gda = jnp.stack([jnp.tile(da_qk_norm_g[l, 0], 2 * DA_HEADS) * (DA_DIM ** -0.5 * log2e), jnp.tile(da_qk_norm_g[l, 1], 2 * DA_HEADS)])
        mla_pad = MLA_HEAD_PAD - MLA_NOPE - MLA_ROPE
        gml = jnp.stack([jnp.tile(jnp.pad(mla_qk_norm_g[l, 0], (0, mla_pad)), MLA_HEADS) * ((MLA_NOPE + MLA_ROPE) ** -0.5 * log2e),
                         jnp.tile(jnp.pad(mla_qk_norm_g[l, 1], (0, mla_pad)), MLA_HEADS)])
        wuq = bf(jnp.pad(_pad_heads(mla_w_uq[l], MLA_HEADS, MLA_NOPE + MLA_ROPE, 0, MLA_NOPE + MLA_ROPE), ((0, 64), (0, 0))))
        wuk = bf(_pad_heads(mla_w_ukv[l], MLA_HEADS, MLA_NOPE + MLA_VDIM, 0, MLA_NOPE))
        wuv = bf(_pad_heads(mla_w_ukv[l], MLA_HEADS, MLA_NOPE + MLA_VDIM, MLA_NOPE, MLA_NOPE + MLA_VDIM))
        consts = (g32, g64, gda.astype(F32), gml.astype(F32), row(jnp.pad(mla_cq_norm_g[l], (0, 64))), row(mla_ckv_norm_g[l]),
                  wuq, wuk, wuv)
        qd, kdt, vd, qm, kmt, vm = _qkprep_call(lay, da, mla, tabs, consts)

        lam32 = da_lambda[l].astype(F32)
        lmbda = (jnp.exp(jnp.sum(lam32[0] * lam32[1])) - jnp.exp(jnp.sum(lam32[2] * lam32[3])) + lambda_init).reshape(1, 1)
        subln = row(jnp.tile(da_subln_g[l], DA_HEADS) * (1.0 - lambda_init))
        ya = _attention(lay, qd, kdt, vd, (lmbda, subln, g64), True, need_ctx, "diff_attn")
        yc = _attention(lay, qm, kmt, vm, (lmbda, subln, g64), False, need_ctx, "mla_attn")

        ys_a, ys_b = _s5_scan(lay, s5a, s5b, s5_mats, l)
        yb = _s5_glu_call(s5a, s5b, ys_a, ys_b, row(s5_d[l]), bf(s5_w_glu[l]), row(s5_b_glu[l]), tm_big)

        pre_consts = (row(rw_mu[l]), g64, row(rw_k_k[l]), row(rw_k_a[l]), row(rw_r_k[l]),
                      rw_w0[l].reshape(2, 1, MIX_W), bf(rw_w1[l]), bf(rw_w2[l]),
                      rw_a0[l].reshape(2, 1, MIX_W), bf(rw_a1[l]), bf(rw_a2[l]), bf(rw_g1[l]), bf(rw_g2[l]))
        yd = _rwkv_branch(lay, rw, pre_consts, g64, row(rw_ln_g[l]), row(rw_ln_b[l]))

        x_mid = _merge_call(lay, need_ctx, x_all, mod, g_mix, bf(w_in[l][:, 2384:]), ya, yb, yc, yd,
                            bf(w_branch[l]), bf(w_out[l]))
        f, comb_t, gid = _router_call(lay, need_ctx, x_mid, mod, row(norm_ffn_g[l]), wr_hi, wr_lo, r_bias)
        x_all = _moe_call(lay, need_ctx, f, comb_t.T, gid, bf(exp_w_gate[l]), bf(exp_w_up[l]), bf(exp_w_down[l]), x_mid, mod)
    return x_all.reshape(b, n_lat, dm)
```

```python
import functools
import math

import numpy as np
import jax
import jax.numpy as jnp
from jax import lax
from jax.experimental import pallas as pl
from jax.experimental.pallas import tpu as pltpu

F32 = jnp.float32
BF16 = jnp.bfloat16

D_MODEL = 1024
GRID_W = 64
ROPE_BASE = 10000.0
EPS = 1e-6
DA_HEADS, DA_DIM, DA_VDIM = 4, 32, 64
S5_GROUPS, S5_CH, S5_STATE = 16, 16, 64
MLA_HEADS, MLA_NOPE, MLA_ROPE, MLA_VDIM = 4, 32, 16, 64
MLA_Q_RANK, MLA_KV_RANK = 192, 128
MLA_HEAD_PAD = 64
RW_HEADS, RW_DIM = 4, 64
RW_LN_EPS = 64e-5
N_BRANCH = 4
N_EXPERTS, N_GROUPS, EXPERTS_PER_GROUP = 16, 4, 4
D_FF = 512
MIX_W = 256

S5_CHUNK = 8
S5_FLAT = S5_CHUNK * MIX_W
S5_STATE_W = S5_GROUPS * S5_STATE
RW_CHUNK = 16
RW_TILE = 128
RW_PREP_TILES = 2
_NT = (((1,), (1,)), ((), ()))
TOKEN_TILE = 256
PROJ_TILE = 768
MOE_TILE = 1152
MOE_BLOCK = 384
MOE_OVER = 128

_DA_W, _S5_W, _MLA_W, _RW_W = 768, 256, 640, 1024
_MIX_COLS = _DA_W + _S5_W + _MLA_W + _RW_W

V7X_VMEM_BYTES = 64 * 2**20
_VMEM_LIMIT = V7X_VMEM_BYTES - 8 * 2**20


def _cparams(*sem):
    return pltpu.CompilerParams(dimension_semantics=sem, vmem_limit_bytes=_VMEM_LIMIT)


def _full(a):
    return pl.BlockSpec(a.shape, lambda *_, nd=a.ndim: (0,) * nd)


def _split_dot(x, w, terms=2):
    acc = None
    rem = x
    for i in range(terms):
        part = rem.astype(BF16)
        d = jnp.dot(part, w, preferred_element_type=F32)
        acc = d if acc is None else acc + d
        if i + 1 < terms:
            rem = rem - part.astype(F32)
    return acc


def _split_dot_rhs(w, x):
    hi = x.astype(BF16)
    lo = (x - hi.astype(F32)).astype(BF16)
    return jnp.dot(w, hi, preferred_element_type=F32) + jnp.dot(w, lo, preferred_element_type=F32)


def _modulate(x, g, shift, scale):
    xn = x * lax.rsqrt(jnp.mean(x * x, axis=-1, keepdims=True) + EPS)
    return xn * g * (1.0 + scale) + shift


def _sigmoid(x):
    return 1.0 / (1.0 + jnp.exp(-x))


def _group_rms(x, ones_bd, inv_n, gain):
    ms = _split_dot(x * x, ones_bd) * inv_n
    return x * lax.rsqrt(ms + EPS) * gain


def _lane_partner(x, half, period, first_end):
    n = x.shape[1]
    lane = lax.broadcasted_iota(jnp.int32, x.shape, 1)
    up = pltpu.roll(x, n - half, axis=1)
    down = pltpu.roll(x, half, axis=1)
    return jnp.where((lane & (period - 1)) < first_end, up, down)


def _rope(x, cos_t, sin_t, half, period, first_end):
    return x * cos_t + _lane_partner(x, half, period, first_end) * sin_t


def _ada_kernel(c_ref, w_ref, b_ref, o_ref):
    c = c_ref[...]
    s = c * _sigmoid(c)
    o_ref[0] = jnp.dot(s.astype(BF16), w_ref[0].astype(BF16), preferred_element_type=F32) + b_ref[0]


def _ada_call(cc, w_ada, b_ada):
    depth, dm, n = w_ada.shape
    tn = n // 4
    return pl.pallas_call(
        _ada_kernel,
        grid=(depth, n // tn),
        in_specs=[
            pl.BlockSpec(cc.shape, lambda l, j: (0, 0)),
            pl.BlockSpec((1, dm, tn), lambda l, j: (l, 0, j)),
            pl.BlockSpec((1, 1, tn), lambda l, j: (l, 0, j)),
        ],
        out_specs=pl.BlockSpec((1, cc.shape[0], tn), lambda l, j: (l, 0, j)),
        out_shape=jax.ShapeDtypeStruct((depth, cc.shape[0], n), F32),
        compiler_params=_cparams("parallel", "parallel"),
        name="ada_mod",
    )(cc, w_ada, b_ada.reshape(depth, 1, n))


class _Layout:
    def __init__(self, n_batch, n_ctx, n_lat):
        t = TOKEN_TILE
        assert n_ctx % t == 0 and n_lat % t == 0
        self.b, self.n_ctx, self.n_lat, self.n_tot = n_batch, n_ctx, n_lat, n_ctx + n_lat
        self.t = t
        self.ctx_tiles, self.lat_tiles, self.seq_tiles = n_ctx // t, n_lat // t, (n_ctx + n_lat) // t

    def rows(self, with_ctx):
        return self.b * (self.n_tot if with_ctx else self.n_lat)

    def n_tiles(self, with_ctx):
        return self.b * (self.seq_tiles if with_ctx else self.lat_tiles)

    def src_tile(self, with_ctx):
        if with_ctx:
            return lambda i: i
        return lambda i: (i // self.lat_tiles) * self.seq_tiles + i % self.lat_tiles + self.ctx_tiles


def _mod_row(modb_ref, modc_ref, r, n_rows, ctx_rows, tiles_per_seq):
    per_batch = modb_ref[0, r:r + 1, :]
    if not ctx_rows:
        return per_batch
    row = lax.broadcasted_iota(jnp.int32, (n_rows, 1), 0)
    first = pl.program_id(0) % tiles_per_seq == 0
    return jnp.where(jnp.logical_and(first, row < ctx_rows), modc_ref[0, r:r + 1, :], per_batch)


def _inproj_kernel(x_ref, modb_ref, modc_ref, g_ref, w_ref, da_ref, s5a_ref, s5b_ref, mla_ref, rw_ref,
                   *, ctx_rows, tiles_per_seq):
    x = x_ref[...]
    mrow = lambda r: _mod_row(modb_ref, modc_ref, r, x.shape[0], ctx_rows, tiles_per_seq)
    h = _modulate(x, g_ref[...], mrow(0), mrow(1))
    acc = jnp.dot(h.astype(BF16), w_ref[...], preferred_element_type=F32)
    da_ref[...] = acc[:, 0:_DA_W]
    s5a_ref[...] = acc[:, _DA_W:_DA_W + _S5_W // 2]
    s5b_ref[...] = acc[:, _DA_W + _S5_W // 2:_DA_W + _S5_W]
    mla_ref[...] = acc[:, _DA_W + _S5_W:_DA_W + _S5_W + _MLA_W]
    rw_ref[...] = acc[:, _DA_W + _S5_W + _MLA_W:_MIX_COLS]


def _seq_tile(lay):
    tm = PROJ_TILE if lay.n_tot % PROJ_TILE == 0 and lay.n_ctx <= PROJ_TILE else lay.t
    return tm, lay.n_tot // tm


def _inproj_call(lay, x_all, mod, g, w_mix):
    t = x_all.shape[0]
    tm, tps = _seq_tile(lay)
    widths = (_DA_W, _S5_W // 2, _S5_W // 2, _MLA_W, _RW_W)
    return pl.pallas_call(
        functools.partial(_inproj_kernel, ctx_rows=lay.n_ctx, tiles_per_seq=tps),
        grid=(t // tm,),
        in_specs=[
            pl.BlockSpec((tm, D_MODEL), lambda i: (i, 0)),
            pl.BlockSpec((1, 6, D_MODEL), lambda i: (i // tps, 0, 0)),
            pl.BlockSpec((1, 6, D_MODEL), lambda i: (lay.b, 0, 0)),
            _full(g), _full(w_mix),
        ],
        out_specs=[pl.BlockSpec((tm, w), lambda i: (i, 0)) for w in widths],
        out_shape=[jax.ShapeDtypeStruct((t, w), F32) for w in widths],
        compiler_params=_cparams("parallel"),
        name="in_proj",
    )(x_all, mod, mod, g, w_mix)


def _qkprep_kernel(da_ref, mla_ref, cda_ref, sda_ref, cml_ref, sml_ref, g32_ref, g64_ref,
                   gda_ref, gml_ref, cqg_ref, ckvg_ref, wuq_ref, wuk_ref, wuv_ref,
                   qd_ref, kd_ref, vd_ref, qm_ref, km_ref, vm_ref):
    g32 = g32_ref[...]
    g64 = g64_ref[...]
    cda, sda = cda_ref[...], sda_ref[...]
    q = _group_rms(da_ref[:, 0:MIX_W], g32, 1.0 / DA_DIM, gda_ref[0:1, :])
    qd_ref[...] = _rope(q, cda, sda, DA_DIM // 2, DA_DIM, DA_DIM // 2).astype(BF16)
    k = _group_rms(da_ref[:, MIX_W:2 * MIX_W], g32, 1.0 / DA_DIM, gda_ref[1:2, :])
    kd_ref[0] = _rope(k, cda, sda, DA_DIM // 2, DA_DIM, DA_DIM // 2).T.astype(BF16)
    vd_ref[...] = da_ref[:, 2 * MIX_W:3 * MIX_W].astype(BF16)

    cml, sml = cml_ref[...], sml_ref[...]
    cq = mla_ref[:, 0:256]
    cqn = cq * lax.rsqrt(jnp.sum(cq * cq, axis=-1, keepdims=True) * (1.0 / MLA_Q_RANK) + EPS) * cqg_ref[...]
    q = jnp.dot(cqn.astype(BF16), wuq_ref[...], preferred_element_type=F32)
    ckv = mla_ref[:, 256:384]
    ckvn = ckv * lax.rsqrt(jnp.mean(ckv * ckv, axis=-1, keepdims=True) + EPS) * ckvg_ref[...]
    ckvb = ckvn.astype(BF16)
    k = jnp.dot(ckvb, wuk_ref[...], preferred_element_type=F32) + mla_ref[:, 384:640]
    vm_ref[...] = jnp.dot(ckvb, wuv_ref[...], preferred_element_type=F32).astype(BF16)
    inv_n = 1.0 / (MLA_NOPE + MLA_ROPE)
    half = MLA_ROPE // 2
    q = _group_rms(q, g64, inv_n, gml_ref[0:1, :])
    qm_ref[...] = _rope(q, cml, sml, half, MLA_HEAD_PAD, MLA_NOPE + half).astype(BF16)
    k = _group_rms(k, g64, inv_n, gml_ref[1:2, :])
    km_ref[0] = _rope(k, cml, sml, half, MLA_HEAD_PAD, MLA_NOPE + half).T.astype(BF16)


def _qkprep_call(lay, da, mla, tabs, consts):
    t = da.shape[0]
    tm, st = _seq_tile(lay)
    row = pl.BlockSpec((tm, MIX_W), lambda i: (i, 0))
    key_t = pl.BlockSpec((1, MIX_W, tm), lambda i: (i // st, 0, i % st))
    in_specs = [pl.BlockSpec((tm, _DA_W), lambda i: (i, 0)), pl.BlockSpec((tm, _MLA_W), lambda i: (i, 0))]
    in_specs += [pl.BlockSpec((tm, MIX_W), lambda i: (i % st, 0)) for _ in tabs]
    in_specs += [_full(a) for a in consts]
    tok = jax.ShapeDtypeStruct((t, MIX_W), BF16)
    keys = jax.ShapeDtypeStruct((lay.b, MIX_W, lay.n_tot), BF16)
    return pl.pallas_call(
        _qkprep_kernel,
        grid=(t // tm,),
        in_specs=in_specs,
        out_specs=[row, key_t, row, row, key_t, row],
        out_shape=[tok, keys, tok, tok, keys, tok],
        compiler_params=_cparams("parallel"),
        name="qk_prep",
    )(da, mla, *tabs, *consts)


def _softmax_parts(s):
    p = jnp.exp2(s - jnp.max(s, axis=-1, keepdims=True))
    return p, 1.0 / jnp.sum(p, axis=-1, keepdims=True)


def _attn_heads(q, kt_ref, v_ref, nk, diff, lam):
    lane = lax.broadcasted_iota(jnp.int32, (q.shape[0], MIX_W), 1)
    v = v_ref[0, 0:nk, :]
    acc = jnp.zeros((q.shape[0], MIX_W), F32)
    dk = DA_DIM if diff else MLA_HEAD_PAD
    per_head = 2 if diff else 1

    def scores(h):
        return [jnp.dot(q[:, e * dk:(e + 1) * dk], kt_ref[0, e * dk:(e + 1) * dk, 0:nk], preferred_element_type=F32)
                for e in range(per_head * h, per_head * (h + 1))]

    ahead = scores(0)
    for h in range(DA_HEADS):
        s = ahead
        if h + 1 < DA_HEADS:
            ahead = scores(h + 1)
        if diff:
            p0, r0 = _softmax_parts(s[0])
            p1, r1 = _softmax_parts(s[1])
            o = jnp.dot((p0 * r0 - p1 * (r1 * lam)).astype(BF16), v, preferred_element_type=F32)
        else:
            p, r = _softmax_parts(s[0])
            o = jnp.dot(p.astype(BF16), v, preferred_element_type=F32) * r
        in_head = jnp.logical_and(lane >= h * DA_VDIM, lane < (h + 1) * DA_VDIM)
        acc = jnp.where(in_head, o, acc)
    return acc


def _attn_kernel(q_ref, kt_ref, v_ref, lam_ref, gain_ref, g64_ref, o_ref, *, diff, n_ctx, n_tot, ctx_tiles):
    q = q_ref[...]
    lam = lam_ref[...]

    def run(nk):
        o = _attn_heads(q, kt_ref, v_ref, nk, diff, lam)
        if diff:
            o = _group_rms(o, g64_ref[...], 1.0 / DA_VDIM, gain_ref[...])
        o_ref[...] = o.astype(BF16)

    if ctx_tiles:
        is_ctx = pl.program_id(1) < ctx_tiles
        pl.when(is_ctx)(lambda: run(n_ctx))
        pl.when(jnp.logical_not(is_ctx))(lambda: run(n_tot))
    else:
        run(n_tot)


def _attention(lay, q, kt, v, extra, diff, with_ctx, name):
    tq = lay.t
    tiles = lay.seq_tiles if with_ctx else lay.lat_tiles
    off = 0 if with_ctx else lay.ctx_tiles
    v3 = v.reshape(lay.b, lay.n_tot, MIX_W)
    kern = functools.partial(_attn_kernel, diff=diff, n_ctx=lay.n_ctx, n_tot=lay.n_tot,
                             ctx_tiles=lay.ctx_tiles if with_ctx else 0)
    return pl.pallas_call(
        kern,
        grid=(lay.b, tiles),
        in_specs=[
            pl.BlockSpec((tq, MIX_W), lambda b, j: (b * lay.seq_tiles + j + off, 0)),
            pl.BlockSpec((1, MIX_W, lay.n_tot), lambda b, j: (b, 0, 0)),
            pl.BlockSpec((1, lay.n_tot, MIX_W), lambda b, j: (b, 0, 0)),
        ] + [_full(a) for a in extra],
        out_specs=pl.BlockSpec((tq, MIX_W), lambda b, j: (b * tiles + j, 0)),
        out_shape=jax.ShapeDtypeStruct((lay.rows(with_ctx), MIX_W), BF16),
        compiler_params=_cparams("parallel", "parallel"),
        name=name,
    )(q, kt, v3, *extra)


def _chunk_rows(ua_ref, ub_ref):
    n = ua_ref.shape[0] // S5_CHUNK
    parts = []
    for s in range(S5_CHUNK):
        rows = pl.ds(s, n, stride=S5_CHUNK)
        parts += [ua_ref[rows, :], ub_ref[rows, :]]
    return jnp.concatenate(parts, axis=1).astype(BF16)


def _s5_proj_kernel(ua_ref, ub_ref, bre_ref, bim_ref, sre_ref, sim_ref):
    u = _chunk_rows(ua_ref, ub_ref)
    sre_ref[0] = jnp.dot(u, bre_ref[0, 0], preferred_element_type=F32)
    sim_ref[0] = jnp.dot(u, bim_ref[0, 0], preferred_element_type=F32)


def _s5_rec_kernel(sre_ref, sim_ref, are_ref, aim_ref, hre_ref, him_ref, *, n_batch, n_chunks, ctx_chunks):
    rev = pl.program_id(0) == 1
    ar, ai = are_ref[0, 0], aim_ref[0, 0]
    sre, sim, hre, him = sre_ref.at[0], sim_ref.at[0], hre_ref.at[0], him_ref.at[0]

    def step(i, carry):
        hr, hi = carry
        k_rev = jnp.where(i < ctx_chunks, ctx_chunks - 1 - i, n_chunks - 1 + ctx_chunks - i)
        k = jnp.where(rev, k_rev, i)
        rows = pl.ds(k, n_batch, stride=n_chunks)
        hre[rows, :] = hr
        him[rows, :] = hi
        return ar * hr - ai * hi + sre[rows, :], ar * hi + ai * hr + sim[rows, :]

    zero = jnp.zeros((n_batch, 128), F32)
    lax.fori_loop(0, n_chunks, step, (zero, zero), unroll=2)


def _s5_out_kernel(ua_ref, ub_ref, hre_ref, him_ref, m_ref, cre_ref, cim_ref, ya_ref, yb_ref):
    y = jnp.dot(_chunk_rows(ua_ref, ub_ref), m_ref[0, 0], preferred_element_type=F32)
    y = y + jnp.dot(hre_ref[0].astype(BF16), cre_ref[0, 0], preferred_element_type=F32)
    y = y + jnp.dot(him_ref[0].astype(BF16), cim_ref[0, 0], preferred_element_type=F32)
    n = y.shape[0]
    ya, yb = ya_ref.at[0], yb_ref.at[0]
    for s in range(S5_CHUNK):
        rows = pl.ds(s, n, stride=S5_CHUNK)
        ya[rows, :] = y[:, s * MIX_W:s * MIX_W + 128]
        yb[rows, :] = y[:, s * MIX_W + 128:(s + 1) * MIX_W]


def _s5_mats(lam_re, lam_im, log_dt, b_re, b_im, c_re, c_im):
    hp = lax.Precision.HIGHEST
    L, G, P, CH = S5_CHUNK, S5_GROUPS, S5_STATE, S5_CH
    lr, li = lam_re.astype(F32), lam_im.astype(F32)
    dt = jnp.exp(log_dt.astype(F32))[..., None]
    zr, zi = lr * dt, li * dt
    j = jnp.arange(L + 1, dtype=F32)[:, None, None, None]
    mag = jnp.exp(zr[None] * j)
    pw_re, pw_im = mag * jnp.cos(zi[None] * j), mag * jnp.sin(zi[None] * j)
    nr, ni = pw_re[1] - 1.0, pw_im[1]
    den = lr * lr + li * li
    cr, ci = (nr * lr + ni * li) / den, (ni * lr - nr * li) / den
    bre, bim = b_re.astype(F32), b_im.astype(F32)
    bb_re = cr[..., None] * bre - ci[..., None] * bim
    bb_im = cr[..., None] * bim + ci[..., None] * bre
    x_re = pw_re[..., None] * bb_re[None] - pw_im[..., None] * bb_im[None]
    x_im = pw_re[..., None] * bb_im[None] + pw_im[..., None] * bb_re[None]
    cre, cim = c_re.astype(F32), c_im.astype(F32)
    kern = (jnp.einsum('dgcp,jdgpe->dgjce', cre, x_re[:L], precision=hp)
            - jnp.einsum('dgcp,jdgpe->dgjce', cim, x_im[:L], precision=hp))
    def spread_mask(a, b):
        spread = jnp.asarray(np.tile(np.eye(b, dtype=np.float32), (1, G)))
        mask = jnp.asarray(np.kron(np.eye(G, dtype=np.float32), np.ones((a, b), np.float32)))
        return spread, mask

    kt = kern.transpose(0, 2, 1, 4, 3)
    xt_re, xt_im = x_re.transpose(1, 0, 2, 4, 3), x_im.transpose(1, 0, 2, 4, 3)
    pwt_re, pwt_im = pw_re.transpose(1, 0, 2, 3)[:, :, :, :, None], pw_im.transpose(1, 0, 2, 3)[:, :, :, :, None]
    cret, cimt = cre.transpose(0, 1, 3, 2)[:, None], cim.transpose(0, 1, 3, 2)[:, None]
    ca_re, ca_im = cret * pwt_re - cimt * pwt_im, -(cret * pwt_im + cimt * pwt_re)
    s_idx, t_idx = np.arange(L)[:, None], np.arange(L)[None, :]
    k_st, xb_re, xb_im, cq_re, cq_im = [], [], [], [], []
    for d in range(2):
        lag = (t_idx - s_idx) if d == 0 else (s_idx - t_idx)
        k_st.append(jnp.where(jnp.asarray(lag >= 0)[:, :, None, None, None], kt[d][np.clip(lag, 0, L - 1)], 0.0))
        pw = np.arange(L - 1, -1, -1) if d == 0 else np.arange(L)
        xb_re.append(xt_re[d][pw])
        xb_im.append(xt_im[d][pw])
        q = np.arange(1, L + 1) if d == 0 else np.arange(L, 0, -1)
        cq_re.append(ca_re[d][q])
        cq_im.append(ca_im[d][q])
    sp, mk = spread_mask(CH, CH)
    m = jnp.einsum('dstrb,bc->dsrtc', jnp.stack(k_st).reshape(2, L, L, G * CH, CH), sp, precision=hp) * mk[:, None, :]
    m = m.astype(BF16).reshape(2, L * G * CH, L * G * CH)
    sp, mk = spread_mask(CH, P)
    to_b = lambda x: (jnp.einsum('dsrb,bc->dsrc', jnp.stack(x).reshape(2, L, G * CH, P), sp, precision=hp) * mk
                      ).astype(BF16).reshape(2, L * G * CH, G * P)
    sp_c, mk_c = spread_mask(P, CH)
    to_c = lambda x: (jnp.einsum('dtrb,bc->drtc', jnp.stack(x).reshape(2, L, G * P, CH), sp_c, precision=hp)
                      * mk_c[:, None, :]).astype(BF16).reshape(2, G * P, L * G * CH)
    a_re, a_im = pw_re[L].reshape(2, 1, G * P), pw_im[L].reshape(2, 1, G * P)
    return m, to_b(xb_re), to_b(xb_im), to_c(cq_re), to_c(cq_im), a_re, a_im


def _s5_scan(lay, ua, ub, mats, layer):
    m, b_r, b_i, c_r, c_i, a_re, a_im = mats
    n_chunks = lay.n_tot // S5_CHUNK
    rows = lay.b * n_chunks
    tr = min(lay.t, rows)
    tok = tr * S5_CHUNK
    half = MIX_W // 2
    wspec = lambda a: pl.BlockSpec((1, 1) + a.shape[2:], lambda d, i: (layer, d, 0, 0))
    state = jax.ShapeDtypeStruct((2, rows, S5_STATE_W), F32)
    sblk = pl.BlockSpec((1, tr, S5_STATE_W), lambda d, i: (d, i, 0))
    ublk = pl.BlockSpec((tok, half), lambda d, i: (i, 0))
    s_re, s_im = pl.pallas_call(
        _s5_proj_kernel,
        grid=(2, rows // tr),
        in_specs=[ublk, ublk, wspec(b_r), wspec(b_i)],
        out_specs=[sblk, sblk],
        out_shape=[state, state],
        compiler_params=_cparams("parallel", "parallel"),
        name="s5_proj",
    )(ua, ub, b_r, b_i)
    col = pl.BlockSpec((1, rows, 128), lambda d, j: (d, 0, j))
    acol = pl.BlockSpec((1, 1, 1, 128), lambda d, j: (layer, d, 0, j))
    h_re, h_im = pl.pallas_call(
        functools.partial(_s5_rec_kernel, n_batch=lay.b, n_chunks=n_chunks, ctx_chunks=lay.n_ctx // S5_CHUNK),
        grid=(2, S5_STATE_W // 128),
        in_specs=[col, col, acol, acol],
        out_specs=[col, col],
        out_shape=[state, state],
        compiler_params=_cparams("parallel", "parallel"),
        name="s5_rec",
    )(s_re, s_im, a_re, a_im)
    yblk = pl.BlockSpec((1, tok, half), lambda d, i: (d, i, 0))
    yshape = jax.ShapeDtypeStruct((2, lay.b * lay.n_tot, half), F32)
    return pl.pallas_call(
        _s5_out_kernel,
        grid=(2, rows // tr),
        in_specs=[ublk, ublk, sblk, sblk, wspec(m), wspec(c_r), wspec(c_i)],
        out_specs=[yblk, yblk],
        out_shape=[yshape, yshape],
        compiler_params=_cparams("parallel", "parallel"),
        name="s5_out",
    )(ua, ub, h_re, h_im, m, c_r, c_i)


def _s5_glu_kernel(ua_ref, ub_ref, ya_ref, yb_ref, d_ref, w_ref, b_ref, o_ref):
    u = jnp.concatenate([ua_ref[...], ub_ref[...]], axis=1)
    y = d_ref[...] * u + jnp.concatenate([ya_ref[0] + ya_ref[1], yb_ref[0] + yb_ref[1]], axis=1)
    z = 0.5 * y * (1.0 + jnp.tanh(math.sqrt(2.0 / math.pi) * (y + 0.044715 * (y * y * y))))
    gate = _sigmoid(jnp.dot(z.astype(BF16), w_ref[...], preferred_element_type=F32) + b_ref[...])
    o_ref[...] = (z * gate).astype(BF16)


def _s5_glu_call(ua, ub, ya, yb, d, w, bias, tm):
    t, half = ua.shape
    urow = pl.BlockSpec((tm, half), lambda i: (i, 0))
    yrow = pl.BlockSpec((2, tm, half), lambda i: (0, i, 0))
    return pl.pallas_call(
        _s5_glu_kernel,
        grid=(t // tm,),
        in_specs=[urow, urow, yrow, yrow, _full(d), _full(w), _full(bias)],
        out_specs=pl.BlockSpec((tm, MIX_W), lambda i: (i, 0)),
        out_shape=jax.ShapeDtypeStruct((t, MIX_W), BF16),
        compiler_params=_cparams("parallel"),
        name="s5_glu",
    )(ua, ub, ya, yb, d, w, bias)


def _rw_pre_kernel(x_ref, prev_ref, next_ref, mu_ref, g64_ref, kk_g_ref, ka_ref, rk_ref,
                   w0_ref, w1_ref, w2_ref, a0_ref, a1_ref, a2_ref, g1_ref, g2_ref,
                   r_ref, v_ref, kk_ref, lw_ref, kka_ref, km_ref, bon_ref, gate_ref,
                   *, seq_tiles, n_ctx, n_tot):
    x = x_ref[...]
    n = x.shape[0]
    row = lax.broadcasted_iota(jnp.int32, (n, 1), 0)
    pos = (pl.program_id(0) % seq_tiles) * n + row
    left = jnp.where(row == 0, prev_ref[0, 7:8, :], pltpu.roll(x, 1, axis=0))
    left = jnp.where(jnp.logical_or(pos == 0, pos == n_ctx), 0.0, left)
    right = jnp.where(row == n - 1, next_ref[0, 0:1, :], pltpu.roll(x, n - 1, axis=0))
    right = jnp.where(jnp.logical_or(pos == n_ctx - 1, pos == n_tot - 1), 0.0, right)
    x = x + (0.5 * (left + right) - x) * mu_ref[...]
    r, k, v, xd = (x[:, i * MIX_W:(i + 1) * MIX_W] for i in range(4))
    g64 = g64_ref[...]
    kscaled = k * kk_g_ref[...]
    kk = kscaled / jnp.maximum(jnp.sqrt(_split_dot(kscaled * kscaled, g64)), 1e-12)
    xdb = xd.astype(BF16)
    r_ref[...] = r
    v_ref[...] = v
    kk_ref[...] = kk
    km_sum = None
    for d in range(2):
        lo = jnp.tanh(jnp.dot(xdb, w1_ref[d], preferred_element_type=F32))
        w_raw = w0_ref[d] + jnp.dot(lo.astype(BF16), w2_ref[d], preferred_element_type=F32)
        lw_ref[d] = -_sigmoid(w_raw) * math.exp(-0.5)
        ar = jnp.dot(xdb, a1_ref[d], preferred_element_type=F32)
        a = _sigmoid(a0_ref[d] + jnp.dot(ar.astype(BF16), a2_ref[d], preferred_element_type=F32))
        km = k * (1.0 + (a - 1.0) * ka_ref[...])
        kka_ref[d] = kk * a
        km_ref[d] = km
        km_sum = km if km_sum is None else km_sum + km
    bon_ref[...] = _split_dot(r * km_sum * rk_ref[...], g64) * v
    gr = _sigmoid(jnp.dot(xdb, g1_ref[...], preferred_element_type=F32))
    gate_ref[...] = jnp.dot(gr.astype(BF16), g2_ref[...], preferred_element_type=F32)


def _rw_pre_call(lay, rw, consts):
    t = rw.shape[0]
    tr, seq_tiles = _seq_tile(lay)
    nt = t // tr
    g8 = tr // 8
    rw8 = rw.reshape(t // 8, 8, _RW_W)
    row = pl.BlockSpec((tr, MIX_W), lambda i: (i, 0))
    row2 = pl.BlockSpec((2, tr, MIX_W), lambda i: (0, i, 0))
    sd = jax.ShapeDtypeStruct((t, MIX_W), F32)
    sd2 = jax.ShapeDtypeStruct((2, t, MIX_W), F32)
    return pl.pallas_call(
        functools.partial(_rw_pre_kernel, seq_tiles=seq_tiles, n_ctx=lay.n_ctx, n_tot=lay.n_tot),
        grid=(nt,),
        in_specs=[pl.BlockSpec((tr, _RW_W), lambda i: (i, 0)),
                  pl.BlockSpec((1, 8, _RW_W), lambda i: (jnp.maximum(i * g8 - 1, 0), 0, 0)),
                  pl.BlockSpec((1, 8, _RW_W), lambda i: (jnp.minimum((i + 1) * g8, t // 8 - 1), 0, 0))]
                 + [_full(a) for a in consts],
        out_specs=[row, row, row, row2, row2, row2, row, row],
        out_shape=[sd, sd, sd, sd2, sd2, sd2, sd, sd],
        compiler_params=_cparams("parallel"),
        name="rwkv_pre",
    )(rw, rw8, rw8, *consts)


def _head_masks(shape, lane_axis, seg):
    lane = lax.broadcasted_iota(jnp.int32, shape, lane_axis)
    return [jnp.logical_and(lane >= h * seg, lane < (h + 1) * seg) for h in range(RW_HEADS)]


def _rw_prep_kernel(*refs, rev):
    tiles = [_rw_prep_tile(sub, *refs, rev=rev) for sub in range(RW_PREP_TILES)]
    while tiles:
        tiles = [t for t in tiles if next(t, None) is not None]


def _rw_prep_tile(sub, r_ref, kk_ref, v_ref, lw_ref, ka_ref, km_ref, perm_ref, permt_ref, g_ref, eye_ref,
                  br_ref, ck_ref, uvt_ref, y0_ref, pc_ref, *, rev):
    C, NC = RW_CHUNK, RW_TILE // RW_CHUNK
    perm, permt, g64, eye4 = perm_ref[...], permt_ref[...], g_ref[...], eye_ref[...]
    tok = slice(sub * RW_TILE, (sub + 1) * RW_TILE)
    nat = jnp.concatenate([r_ref[0, tok, :], kk_ref[0, tok, :], v_ref[0, tok, :],
                           lw_ref[0, 0, tok, :], ka_ref[0, 0, tok, :], km_ref[0, 0, tok, :]], axis=1)
    hi = nat.astype(BF16)
    lo = (nat - hi.astype(F32)).astype(BF16)
    pm = jnp.dot(perm, jnp.concatenate([hi, lo], axis=0), preferred_element_type=F32)
    r, kk, v, lw, ka, km = (pm[:, i * MIX_W:(i + 1) * MIX_W] for i in range(6))
    slab = lambda x, j: x[j * NC:(j + 1) * NC, :]
    order = list(range(C))[::-1] if rev else list(range(C))
    pos = {j: i for i, j in enumerate(order)}
    cum, run = {}, None
    for j in order:
        run = slab(lw, j) if run is None else run + slab(lw, j)
        cum[j] = run
    tot = run
    yield True
    bh, ch, kh, rh, cp, kp, vv = {}, {}, {}, {}, {}, {}, {}
    for j in range(C):
        e_inv, e_end = jnp.exp(-cum[j]), jnp.exp(tot - cum[j])
        bh[j] = -slab(kk, j) * jnp.exp(cum[j] - slab(lw, j))
        ch[j], kh[j] = slab(ka, j) * e_inv, slab(km, j) * e_inv
        rh[j] = slab(r, j) * jnp.exp(cum[j])
        cp[j], kp[j] = slab(ka, j) * e_end, slab(km, j) * e_end
        vv[j] = slab(v, j)
    strict = [(t, s) for t in order for s in order if pos[s] < pos[t]]
    incl = [(t, s) for t in order for s in order if pos[s] <= pos[t]]
    def head_dots(lhs, rhs, pairs):
        prods = jnp.concatenate([lhs[t] * rhs[s] for t, s in pairs], axis=0).astype(BF16)
        gram = jnp.dot(prods, g64, preferred_element_type=F32)
        return {p: gram[i * NC:(i + 1) * NC, :] for i, p in enumerate(pairs)}

    yield True
    acb = head_dots(bh, ch, strict)
    yield True
    akb = head_dots(bh, kh, strict)
    yield True
    mcr = head_dots(rh, ch, incl)
    yield True
    mkr = head_dots(rh, kh, incl)
    yield True
    bt, u0 = {}, {}
    for t in order:
        b_acc, u_acc = bh[t], jnp.zeros_like(bh[t])
        for s in order:
            if pos[s] < pos[t]:
                b_acc = b_acc + acb[(t, s)] * bt[s]
                u_acc = u_acc + akb[(t, s)] * vv[s] + acb[(t, s)] * u0[s]
        bt[t], u0[t] = b_acc, u_acc
        yield True
    rt, y0 = {}, {}
    for t in order:
        r_acc, y_acc = rh[t], jnp.zeros_like(rh[t])
        for s in order:
            if pos[s] <= pos[t]:
                r_acc = r_acc + mcr[(t, s)] * bt[s]
                y_acc = y_acc + mcr[(t, s)] * u0[s] + mkr[(t, s)] * vv[s]
        rt[t], y0[t] = r_acc, y_acc
        yield True
    stackp = lambda dct: jnp.concatenate([dct[j] for j in range(C)], axis=0)
    b16 = lambda x: x.astype(BF16)
    y0p, u0p = stackp(y0), stackp(u0)
    y0h, u0h = b16(y0p), b16(u0p)
    cat = jnp.concatenate([b16(stackp(bt)), b16(stackp(rt)), b16(stackp(cp)), b16(stackp(kp)),
                           y0h, b16(y0p - y0h.astype(F32)), u0h, b16(u0p - u0h.astype(F32)), b16(stackp(vv))], axis=1)
    natural = jnp.dot(permt, cat, preferred_element_type=F32)
    seg = lambda i: natural[:, i * MIX_W:(i + 1) * MIX_W]
    yield True
    btn, rtn, cpn, kpn = b16(seg(0)), b16(seg(1)), b16(seg(2)), b16(seg(3))
    y0_ref[0, tok, :] = seg(4) + seg(5)
    u0h_n, u0l_n, vn = b16(seg(6)), b16(seg(7)), b16(seg(8))
    hm = _head_masks((C, MIX_W), 1, RW_DIM)
    zero = jnp.zeros((C, MIX_W), BF16)
    zh, zl = [], []
    for c in range(NC):
        rows = slice(c * C, (c + 1) * C)
        br_ref[0, sub * NC + c, 0:C, :] = btn[rows]
        br_ref[0, sub * NC + c, C:2 * C, :] = rtn[rows]
        ck_ref[0, sub * NC + c, 0:C, :] = cpn[rows]
        ck_ref[0, sub * NC + c, C:2 * C, :] = kpn[rows]
        for h in range(RW_HEADS):
            zh += [jnp.where(hm[h], u0h_n[rows], zero), jnp.where(hm[h], vn[rows], zero)]
            zl += [jnp.where(hm[h], u0l_n[rows], zero), zero]
    z = jnp.concatenate([jnp.concatenate(zh, axis=0), jnp.concatenate(zl, axis=0)], axis=1)
    uvt = lax.dot_general(eye4, z, _NT, preferred_element_type=F32)
    for c in range(NC):
        uvt_ref[0, sub * NC + c] = uvt[:, c * 2 * C * RW_HEADS:(c + 1) * 2 * C * RW_HEADS]
    pc_ref[0, sub * NC:(sub + 1) * NC, :] = jnp.exp(tot)


def _rw_prep_call(lay, shared, perdir, consts, rev):
    b, n_tot, tt = lay.b, lay.n_tot, RW_TILE * RW_PREP_TILES
    assert n_tot % tt == 0
    nck = n_tot // RW_CHUNK
    cpt = tt // RW_CHUNK
    d = 1 if rev else 0
    sh = [a.reshape(b, n_tot, MIX_W) for a in shared]
    pd = [a.reshape(2, b, n_tot, MIX_W) for a in perdir]
    tok = pl.BlockSpec((1, tt, MIX_W), lambda i, j: (i, j, 0))
    tok_d = pl.BlockSpec((1, 1, tt, MIX_W), lambda i, j: (d, i, j, 0))
    rows32 = pl.BlockSpec((1, cpt, 2 * RW_CHUNK, MIX_W), lambda i, j: (i, j, 0, 0))
    return pl.pallas_call(
        functools.partial(_rw_prep_kernel, rev=rev),
        grid=(b, n_tot // tt),
        in_specs=[tok] * 3 + [tok_d] * 3 + [_full(a) for a in consts],
        out_specs=[rows32, rows32,
                   pl.BlockSpec((1, cpt, RW_DIM, 2 * RW_CHUNK * RW_HEADS), lambda i, j: (i, j, 0, 0)),
                   tok,
                   pl.BlockSpec((1, cpt, MIX_W), lambda i, j: (i, j, 0))],
        out_shape=[jax.ShapeDtypeStruct((b, nck, 2 * RW_CHUNK, MIX_W), BF16),
                   jax.ShapeDtypeStruct((b, nck, 2 * RW_CHUNK, MIX_W), BF16),
                   jax.ShapeDtypeStruct((b, nck, RW_DIM, 2 * RW_CHUNK * RW_HEADS), F32),
                   jax.ShapeDtypeStruct((b, n_tot, MIX_W), F32),
                   jax.ShapeDtypeStruct((b, nck, MIX_W), F32)],
        compiler_params=_cparams("parallel", "parallel"),
        name="rwkv_prep_rev" if rev else "rwkv_prep_fwd",
    )(*sh, *pd, *consts)


def _rw_scan_kernel(brf, ckf, uvtf, pcf, brr, ckr, uvtr, pcr, ytf_ref, ytr_ref, s_scr, *, n_batch):
    @pl.when(pl.program_id(0) == 0)
    def _():
        s_scr[...] = jnp.zeros_like(s_scr)

    cpt = RW_TILE // RW_CHUNK
    hm = _head_masks((2 * RW_CHUNK, MIX_W), 1, RW_DIM)
    lane = lax.broadcasted_iota(jnp.int32, (RW_DIM, 2 * RW_CHUNK * RW_HEADS), 1)
    is_u = (lane & (2 * RW_CHUNK - 1)) < RW_CHUNK
    per_head = lambda x: jnp.concatenate([jnp.where(m, x, jnp.zeros_like(x)) for m in hm], axis=0)

    def refs_of(p, c):
        d, b = divmod(p, n_batch)
        refs = (brf, ckf, uvtf, pcf, ytf_ref) if d == 0 else (brr, ckr, uvtr, pcr, ytr_ref)
        return refs, b, (c if d == 0 else cpt - 1 - c)

    def step(c, carry):
        lhs = []
        for p in range(2 * n_batch):
            (br_ref, _, uvt_ref, _, yt_ref), b, cc = refs_of(p, c)
            s = s_scr[p]
            shi = s.astype(BF16)
            slo = (s - shi.astype(F32)).astype(BF16)
            w2 = lax.dot_general(jnp.concatenate([shi, slo], axis=0), per_head(br_ref[b, cc]), _NT, preferred_element_type=F32)
            w = w2[:RW_DIM] + w2[RW_DIM:]
            yt_ref[b, cc] = w
            uvt = uvt_ref[b, cc]
            lhs.append(jnp.where(is_u, w + uvt, uvt).astype(BF16))
        for p in range(2 * n_batch):
            (_, ck_ref, _, pc_ref, _), b, cc = refs_of(p, c)
            s_scr[p] = (s_scr[p] * pc_ref[b, pl.ds(cc, 1), :]
                        + jnp.dot(lhs[p], per_head(ck_ref[b, cc]), preferred_element_type=F32))
        return carry

    lax.fori_loop(0, cpt, step, 0)


def _rw_scan_call(lay, fwd, rev):
    b, n_tot, tt = lay.b, lay.n_tot, RW_TILE
    assert lay.n_ctx % tt == 0 and lay.n_lat % tt == 0
    nt, ct = n_tot // tt, lay.n_ctx // tt
    cpt = tt // RW_CHUNK
    rev_tile = lambda i: jnp.where(i < ct, ct - 1 - i, nt - 1 + ct - i)

    def specs(tile):
        return [pl.BlockSpec((b, cpt, 2 * RW_CHUNK, MIX_W), lambda i: (0, tile(i), 0, 0)),
                pl.BlockSpec((b, cpt, 2 * RW_CHUNK, MIX_W), lambda i: (0, tile(i), 0, 0)),
                pl.BlockSpec((b, cpt, RW_DIM, 2 * RW_CHUNK * RW_HEADS), lambda i: (0, tile(i), 0, 0)),
                pl.BlockSpec((b, cpt, MIX_W), lambda i: (0, tile(i), 0))]

    ident = lambda i: i
    yt = jax.ShapeDtypeStruct((b, n_tot // RW_CHUNK, RW_DIM, 2 * RW_CHUNK * RW_HEADS), F32)
    return pl.pallas_call(
        functools.partial(_rw_scan_kernel, n_batch=b),
        grid=(nt,),
        in_specs=specs(ident) + specs(rev_tile),
        out_specs=[specs(ident)[2], specs(rev_tile)[2]],
        out_shape=[yt, yt],
        scratch_shapes=[pltpu.VMEM((2 * b, RW_DIM, MIX_W), F32)],
        compiler_params=_cparams("arbitrary"),
        name="rwkv_scan",
    )(*fwd, *rev)


def _rw_fin_kernel(ytf_ref, ytr_ref, y0f_ref, y0r_ref, bon_ref, gate_ref, asel_ref, g64_ref, lng_ref, lnb_ref, o_ref):
    cpt = RW_TILE // RW_CHUNK
    asel = asel_ref[...]
    width = cpt * 2 * RW_CHUNK * RW_HEADS
    lane = lax.broadcasted_iota(jnp.int32, (RW_DIM, width), 1)
    lane_head = jnp.bitwise_and(jnp.right_shift(lane, 5), RW_HEADS - 1)

    def base(yt_ref, sub):
        yt = jnp.concatenate([yt_ref[0, sub * cpt + c] for c in range(cpt)], axis=1).astype(BF16)
        rows = jnp.concatenate([jnp.where(lane_head == h, yt, jnp.zeros_like(yt)) for h in range(RW_HEADS)], axis=0)
        return lax.dot_general(asel, rows, _NT, preferred_element_type=F32)

    g64 = g64_ref[...]
    bases = [(base(ytf_ref, sub), base(ytr_ref, sub)) for sub in range(RW_PREP_TILES)]
    for sub, (yb_f, yb_r) in enumerate(bases):
        tok = slice(sub * RW_TILE, (sub + 1) * RW_TILE)
        y = yb_f + y0f_ref[0, tok, :] + yb_r + y0r_ref[0, tok, :]
        mean = _split_dot(y, g64) * (1.0 / RW_DIM)
        c = y - mean
        var = _split_dot(c * c, g64) * (1.0 / RW_DIM)
        out = c * lax.rsqrt(var + RW_LN_EPS) * lng_ref[...] + lnb_ref[...] + bon_ref[tok, :]
        o_ref[tok, :] = (out * gate_ref[tok, :]).astype(BF16)


def _rw_fin_call(lay, ytf, ytr, y0f, y0r, bon, gate, consts):
    b, n_tot, tt = lay.b, lay.n_tot, RW_TILE * RW_PREP_TILES
    nt = n_tot // tt
    cpt = tt // RW_CHUNK
    ytb = pl.BlockSpec((1, cpt, RW_DIM, 2 * RW_CHUNK * RW_HEADS), lambda i, j: (i, j, 0, 0))
    y0b = pl.BlockSpec((1, tt, MIX_W), lambda i, j: (i, j, 0))
    row = pl.BlockSpec((tt, MIX_W), lambda i, j: (i * nt + j, 0))
    return pl.pallas_call(
        _rw_fin_kernel,
        grid=(b, nt),
        in_specs=[ytb, ytb, y0b, y0b, row, row] + [_full(a) for a in consts],
        out_specs=row,
        out_shape=jax.ShapeDtypeStruct((b * n_tot, MIX_W), BF16),
        compiler_params=_cparams("parallel", "parallel"),
        name="rwkv_finish",
    )(ytf, ytr, y0f, y0r, bon, gate, *consts)


def _rw_constants():
    c, nc = RW_CHUNK, RW_TILE // RW_CHUNK
    perm = np.zeros((RW_TILE, RW_TILE), np.float32)
    for ci in range(nc):
        for j in range(c):
            perm[j * nc + ci, ci * c + j] = 1.0
    lane = np.arange(MIX_W) % RW_DIM
    eye4 = (lane[None, :] == np.arange(RW_DIM)[:, None]).astype(np.float32)
    lanes = np.arange(nc * 2 * c * RW_HEADS)
    lane_chunk, lane_tok = lanes // (2 * c * RW_HEADS), lanes % (2 * c)
    t = np.arange(RW_TILE)
    asel = ((lane_chunk[None, :] == (t // c)[:, None]) & (lane_tok[None, :] == (c + t % c)[:, None])).astype(np.float32)
    as16 = lambda a: jnp.asarray(a, BF16)
    twice = lambda a: np.concatenate([a, a], axis=1)
    return as16(twice(perm)), as16(perm.T), as16(twice(eye4)), as16(asel)


def _rwkv_branch(lay, rw, pre_consts, g64, lng, lnb):
    r_, v_, kk_, lw_, kka_, km_, bon, gate = _rw_pre_call(lay, rw, pre_consts)
    perm, permt, eye4, asel = _rw_constants()
    prep_consts = (perm, permt, g64, eye4)
    fwd = _rw_prep_call(lay, (r_, kk_, v_), (lw_, kka_, km_), prep_consts, False)
    rev = _rw_prep_call(lay, (r_, kk_, v_), (lw_, kka_, km_), prep_consts, True)
    pick = lambda o: (o[0], o[1], o[2], o[4])
    ytf, ytr = _rw_scan_call(lay, pick(fwd), pick(rev))
    return _rw_fin_call(lay, ytf, ytr, fwd[3], rev[3], bon, gate, (asel, g64, lng, lnb))


def _merge_kernel(x_ref, modb_ref, modc_ref, g_ref, wg_ref, ya_ref, yb_ref, yc_ref, yd_ref, wb_ref, wo_ref, o_ref,
                  *, ctx_rows, tiles_per_seq):
    x = x_ref[...]
    mrow = lambda r: _mod_row(modb_ref, modc_ref, r, x.shape[0], ctx_rows, tiles_per_seq)
    h = _modulate(x, g_ref[...], mrow(0), mrow(1)).astype(BF16)
    merged = None
    for i, y_ref in enumerate((ya_ref, yb_ref, yc_ref, yd_ref)):
        gate = _sigmoid(jnp.dot(h, wg_ref[:, i * D_MODEL:(i + 1) * D_MODEL], preferred_element_type=F32))
        term = gate * jnp.dot(y_ref[...], wb_ref[i], preferred_element_type=F32)
        merged = term if merged is None else merged + term
    out = jnp.dot(merged.astype(BF16), wo_ref[...], preferred_element_type=F32)
    o_ref[...] = x + mrow(2) * out


def _merge_call(lay, with_ctx, x_all, mod, g, w_gate, ya, yb, yc, yd, w_branch, w_out):
    if with_ctx:
        tm, tps = _seq_tile(lay)
        src, n_tiles, ctx_rows = (lambda i: i), lay.b * tps, lay.n_ctx
    else:
        tm, tps = lay.t, lay.lat_tiles
        src, n_tiles, ctx_rows = lay.src_tile(False), lay.n_tiles(False), 0
    full_row = lambda w: pl.BlockSpec((tm, w), lambda i: (src(i), 0))
    out_row = lambda w: pl.BlockSpec((tm, w), lambda i: (i, 0))
    return pl.pallas_call(
        functools.partial(_merge_kernel, ctx_rows=ctx_rows, tiles_per_seq=tps),
        grid=(n_tiles,),
        in_specs=[full_row(D_MODEL), pl.BlockSpec((1, 6, D_MODEL), lambda i: (i // tps, 0, 0)),
                  pl.BlockSpec((1, 6, D_MODEL), lambda i: (lay.b, 0, 0)), _full(g), _full(w_gate),
                  out_row(MIX_W), full_row(MIX_W), out_row(MIX_W), full_row(MIX_W), _full(w_branch), _full(w_out)],
        out_specs=out_row(D_MODEL),
        out_shape=jax.ShapeDtypeStruct((lay.rows(with_ctx), D_MODEL), F32),
        compiler_params=_cparams("parallel"),
        name="merge_out",
    )(x_all, mod, mod, g, w_gate, ya, yb, yc, yd, w_branch, w_out)


def _router_kernel(x_ref, modb_ref, modc_ref, g_ref, wh_ref, wl_ref, bias_ref, f_ref, comb_ref, gid_ref,
                   *, ctx_rows, tiles_per_seq):
    x = x_ref[...]
    mrow = lambda r: _mod_row(modb_ref, modc_ref, r, x.shape[0], ctx_rows, tiles_per_seq)
    f = _modulate(x, g_ref[...], mrow(3), mrow(4))
    fh = f.astype(BF16)
    f_ref[...] = fh
    fl = (f - fh.astype(F32)).astype(BF16)
    nt = (((1,), (1,)), ((), ()))
    wh, wl = wh_ref[...], wl_ref[...]
    logits = (lax.dot_general(wh, fh, nt, preferred_element_type=F32)
              + lax.dot_general(wh, fl, nt, preferred_element_type=F32)
              + lax.dot_general(wl, fh, nt, preferred_element_type=F32))
    scores = _sigmoid(logits)
    biased = scores + bias_ref[...]
    sc = [scores[e:e + 1, :] for e in range(N_EXPERTS)]
    bi = [biased[e:e + 1, :] for e in range(N_EXPERTS)]
    group_score = []
    for g in range(N_GROUPS):
        a, b, c, d = bi[4 * g:4 * g + 4]
        m1, n1, m2, n2 = jnp.maximum(a, b), jnp.minimum(a, b), jnp.maximum(c, d), jnp.minimum(c, d)
        group_score.append(jnp.maximum(m1, m2) + jnp.maximum(jnp.minimum(m1, m2), jnp.maximum(n1, n2)))

    def first_argmax(vals):
        top = functools.reduce(jnp.maximum, vals)
        seen, hot = None, []
        for v in vals:
            h = v == top
            if seen is not None:
                h = jnp.logical_and(h, jnp.logical_not(seen))
            seen = h if seen is None else jnp.logical_or(seen, h)
            hot.append(h)
        return hot

    in_group = first_argmax(group_score)
    masked = [jnp.where(in_group[e // EXPERTS_PER_GROUP], bi[e], -jnp.inf) for e in range(N_EXPERTS)]
    hot1 = first_argmax(masked)
    hot2 = first_argmax([jnp.where(h, -jnp.inf, v) for h, v in zip(hot1, masked)])
    w1 = functools.reduce(jnp.add, [jnp.where(h, s, 0.0) for h, s in zip(hot1, sc)])
    w2 = functools.reduce(jnp.add, [jnp.where(h, s, 0.0) for h, s in zip(hot2, sc)])
    inv_tot = 1.0 / (w1 + w2)
    for e in range(N_EXPERTS):
        comb_ref[e:e + 1, :] = (jnp.where(hot1[e], w1, 0.0) + jnp.where(hot2[e], w2, 0.0)) * inv_tot
    gid_ref[...] = functools.reduce(jnp.add, [jnp.where(in_group[g], g, 0) for g in range(1, N_GROUPS)])


def _router_call(lay, with_ctx, x, mod, g, wh, wl, bias):
    t = x.shape[0]
    if with_ctx:
        (tm, tps), ctx_rows = _seq_tile(lay), lay.n_ctx
    else:
        pair = 2 if lay.lat_tiles % 2 == 0 else 1
        tm, tps, ctx_rows = pair * lay.t, lay.lat_tiles // pair, 0
    return pl.pallas_call(
        functools.partial(_router_kernel, ctx_rows=ctx_rows, tiles_per_seq=tps),
        grid=(t // tm,),
        in_specs=[pl.BlockSpec((tm, D_MODEL), lambda i: (i, 0)),
                  pl.BlockSpec((1, 6, D_MODEL), lambda i: (i // tps, 0, 0)),
                  pl.BlockSpec((1, 6, D_MODEL), lambda i: (lay.b, 0, 0)), _full(g), _full(wh), _full(wl), _full(bias)],
        out_specs=[pl.BlockSpec((tm, D_MODEL), lambda i: (i, 0)), pl.BlockSpec((N_EXPERTS, tm), lambda i: (0, i)),
                   pl.BlockSpec((1, tm), lambda i: (0, i))],
        out_shape=[jax.ShapeDtypeStruct((t, D_MODEL), BF16), jax.ShapeDtypeStruct((N_EXPERTS, t), F32),
                   jax.ShapeDtypeStruct((1, t), jnp.int32)],
        compiler_params=_cparams("parallel"),
        name="moe_router",
    )(x, mod, mod, g, wh, wl, bias)


def _moe_plan(gid, n_tiles, tm):
    g = gid.reshape(n_tiles, tm)
    onehot = (g[..., None] == jnp.arange(N_GROUPS, dtype=jnp.int32)).astype(jnp.int32)
    rank = jnp.cumsum(onehot, axis=1) - onehot
    counts = jnp.sum(onehot, axis=1)
    padded = (counts + 15) // 16 * 16
    offs = jnp.cumsum(padded, axis=1) - padded
    pos = jnp.sum(onehot * (offs[:, None, :] + rank), axis=-1)
    n_over = (jnp.maximum(padded - MOE_BLOCK, 0) + MOE_OVER - 1) // MOE_OVER
    return pos.astype(jnp.int32), offs.astype(jnp.int32), n_over.astype(jnp.int32)


def _moe_kernel(offs_ref, nover_ref, f_ref, posr_ref, posc_ref, comb_ref, wg_ref, wu_ref, wd_ref, x_ref, modb_ref,
                modc_ref, o_ref, xs_scr, cs_scr, ys_scr, *, ctx_rows, tiles_per_seq):
    i, e = pl.program_id(0), pl.program_id(1)
    n_slots, tm = xs_scr.shape[0], f_ref.shape[0]
    n_live = min(n_slots, -(-(tm + 16 * N_GROUPS) // 256) * 256)

    @pl.when(e == 0)
    def _():
        slot = lax.broadcasted_iota(jnp.int32, (n_live, tm), 0)
        place = (slot == posr_ref[0]).astype(BF16)
        xs_scr[0:n_live, :] = jnp.dot(place, f_ref[...], preferred_element_type=F32).astype(BF16)
        xs_scr[n_live:n_slots, :] = jnp.zeros((n_slots - n_live, D_MODEL), BF16)
        cs_scr[0:n_live, :] = _split_dot_rhs(place, comb_ref[...])
        cs_scr[n_live:n_slots, :] = jnp.zeros((n_slots - n_live, N_EXPERTS), F32)
        ys_scr[...] = jnp.zeros_like(ys_scr)

    grp = lax.shift_right_logical(e, 2)
    start = offs_ref[i, grp]
    lane = lax.broadcasted_iota(jnp.int32, (1, N_EXPERTS), 1)

    def run(rows):
        xb = xs_scr[rows, :]
        gate = jnp.dot(xb, wg_ref[0], preferred_element_type=F32)
        up = jnp.dot(xb, wu_ref[0], preferred_element_type=F32)
        act = (gate * _sigmoid(gate) * up).astype(BF16)
        down = jnp.dot(act, wd_ref[0], preferred_element_type=F32)
        c_e = jnp.sum(jnp.where(lane == e, cs_scr[rows, :], 0.0), axis=1, keepdims=True)
        ys_scr[rows, :] += c_e * down

    run(pl.ds(pl.multiple_of(start, 16), MOE_BLOCK))

    def overflow(k, carry):
        run(pl.ds(pl.multiple_of(start + MOE_BLOCK + k * MOE_OVER, 16), MOE_OVER))
        return carry

    lax.fori_loop(0, nover_ref[i, grp], overflow, 0)

    @pl.when(e == N_EXPERTS - 1)
    def _():
        slot = lax.broadcasted_iota(jnp.int32, (tm, n_live), 1)
        fetch = (slot == posc_ref[...]).astype(BF16)
        y = jnp.dot(fetch, ys_scr[0:n_live, :].astype(BF16), preferred_element_type=F32)
        res_gate = modb_ref[0, 5:6, :]
        if ctx_rows:
            row = lax.broadcasted_iota(jnp.int32, y.shape, 0)
            first = i % tiles_per_seq == 0
            res_gate = jnp.where(jnp.logical_and(first, row < ctx_rows), modc_ref[0, 5:6, :], res_gate)
        o_ref[...] = x_ref[...] + res_gate * y


def _moe_call(lay, with_ctx, f, comb, gid, wg, wu, wd, x, mod):
    t = f.shape[0]
    seq = lay.n_tot if with_ctx else lay.n_lat
    tm = MOE_TILE if seq % MOE_TILE == 0 else math.gcd(seq, 1024)
    tps, n_tiles = seq // tm, t // tm
    ctx_rows = lay.n_ctx if with_ctx else 0
    assert ctx_rows <= tm
    n_slots = -(-(tm + 16 * N_GROUPS + MOE_BLOCK + MOE_OVER) // 256) * 256
    pos, offs, n_over = _moe_plan(gid, n_tiles, tm)
    wspec = lambda a: pl.BlockSpec((1,) + a.shape[1:], lambda i, e, *_: (e, 0, 0))
    tok = lambda w: pl.BlockSpec((tm, w), lambda i, e, *_: (i, 0))
    grid_spec = pltpu.PrefetchScalarGridSpec(
        num_scalar_prefetch=2,
        grid=(n_tiles, N_EXPERTS),
        in_specs=[tok(D_MODEL), pl.BlockSpec((1, 1, tm), lambda i, e, *_: (i, 0, 0)), tok(1), tok(N_EXPERTS),
                  wspec(wg), wspec(wu), wspec(wd), tok(D_MODEL),
                  pl.BlockSpec((1, 6, D_MODEL), lambda i, e, *_: (i // tps, 0, 0)),
                  pl.BlockSpec((1, 6, D_MODEL), lambda i, e, *_: (lay.b, 0, 0))],
        out_specs=tok(D_MODEL),
        scratch_shapes=[pltpu.VMEM((n_slots, D_MODEL), BF16), pltpu.VMEM((n_slots, N_EXPERTS), F32),
                        pltpu.VMEM((n_slots, D_MODEL), F32)])
    return pl.pallas_call(
        functools.partial(_moe_kernel, ctx_rows=ctx_rows, tiles_per_seq=tps),
        grid_spec=grid_spec,
        out_shape=jax.ShapeDtypeStruct((t, D_MODEL), F32),
        compiler_params=_cparams("parallel", "arbitrary"),
        name="moe_experts",
    )(offs, n_over, f, pos.reshape(n_tiles, 1, tm), pos.reshape(t, 1), comb, wg, wu, wd, x, mod, mod)


def _block_ones(n, group):
    i = np.arange(n) // group
    return jnp.asarray(i[:, None] == i[None, :], dtype=BF16)


def _rope_tables(n_ctx, n_lat):
    rows = n_lat // GRID_W
    row = jnp.repeat(jnp.arange(rows, dtype=F32), GRID_W)
    col = jnp.tile(jnp.arange(GRID_W, dtype=F32), rows)

    def angles(rot_dim):
        n_freq = rot_dim // 4
        inv_freq = ROPE_BASE ** (-jnp.arange(n_freq, dtype=F32) / n_freq)
        ang = jnp.concatenate([row[:, None] * inv_freq, col[:, None] * inv_freq], axis=-1)
        return jnp.cos(ang), jnp.sin(ang)

    c, s = angles(DA_DIM)
    cda = jnp.tile(jnp.concatenate([c, c], -1), (1, 2 * DA_HEADS))
    sda = jnp.tile(jnp.concatenate([-s, s], -1), (1, 2 * DA_HEADS))
    c, s = angles(MLA_ROPE)
    one = jnp.ones((n_lat, MLA_NOPE), F32)
    pad = MLA_HEAD_PAD - MLA_NOPE - MLA_ROPE
    cml = jnp.tile(jnp.concatenate([one, c, c, jnp.ones((n_lat, pad), F32)], -1), (1, MLA_HEADS))
    sml = jnp.tile(jnp.concatenate([0 * one, -s, s, jnp.zeros((n_lat, pad), F32)], -1), (1, MLA_HEADS))
    ident = lambda t, v: jnp.concatenate([jnp.full((n_ctx, MIX_W), v, F32), t], axis=0)
    return ident(cda, 1.0), ident(sda, 0.0), ident(cml, 1.0), ident(sml, 0.0)


def _pad_heads(w, n_heads, src_w, lo, hi, dst_w=MLA_HEAD_PAD):
    w = w.reshape(w.shape[0], n_heads, src_w)[:, :, lo:hi]
    w = jnp.pad(w, ((0, 0), (0, 0), (0, dst_w - (hi - lo))))
    return w.reshape(w.shape[0], n_heads * dst_w)


def _mix_weight(w_in_l):
    w = w_in_l
    kr = w[:, 1344:1360]
    z = lambda n: jnp.zeros((D_MODEL, n), w.dtype)
    kr_wide = jnp.concatenate([jnp.concatenate([z(MLA_NOPE), kr, z(MLA_HEAD_PAD - MLA_NOPE - MLA_ROPE)], 1)] * MLA_HEADS, 1)
    return jnp.concatenate([w[:, 0:1024], w[:, 1024:1216], z(64), w[:, 1216:1344], kr_wide, w[:, 1360:2384]], axis=1).astype(BF16)


def kernel(x, c, ctx, c_ctx, w_ada, b_ada, norm_mix_g, norm_ffn_g, w_in, da_qk_norm_g, da_lambda, da_subln_g, s5_lam_re, s5_lam_im, s5_log_dt, s5_b_re, s5_b_im, s5_c_re, s5_c_im, s5_d, s5_w_glu, s5_b_glu, mla_cq_norm_g, mla_ckv_norm_g, mla_w_uq, mla_w_ukv, mla_qk_norm_g, rw_mu, rw_w0, rw_w1, rw_w2, rw_a0, rw_a1, rw_a2, rw_g1, rw_g2, rw_k_k, rw_k_a, rw_r_k, rw_ln_g, rw_ln_b, w_branch, w_out, router_w, router_bias, exp_w_gate, exp_w_up, exp_w_down):
    b, n_lat, dm = x.shape
    n_ctx = ctx.shape[1]
    depth = w_ada.shape[0]
    assert dm == D_MODEL
    lay = _Layout(b, n_ctx, n_lat)
    t_all = b * lay.n_tot
    tm_big = 2 * lay.t

    g32 = _block_ones(MIX_W, DA_DIM)
    g64 = _block_ones(MIX_W, RW_DIM)
    tabs = _rope_tables(n_ctx, n_lat)
    row = lambda v: v.reshape(1, -1).astype(F32)
    bf = lambda a: a.astype(BF16)

    cc = jnp.zeros((16, dm), F32).at[:b].set(c).at[b].set(c_ctx)
    mod_all = _ada_call(cc, w_ada, b_ada)
    x_all = jnp.concatenate([ctx, x], axis=1).reshape(t_all, dm)

    s5_mats = jax.vmap(_s5_mats)(s5_lam_re, s5_lam_im, s5_log_dt, s5_b_re, s5_b_im, s5_c_re, s5_c_im)

    wr_hi = router_w.T.astype(BF16)
    wr_lo = (router_w.T - wr_hi.astype(F32)).astype(BF16)
    r_bias = router_bias.reshape(N_EXPERTS, 1).astype(F32)

    for l in range(depth):
        need_ctx = l < depth - 1
        lambda_init = 0.8 - 0.6 * math.exp(-0.3 * l)
        mod = mod_all[l, :b + 1].reshape(b + 1, 6, dm)
        g_mix = row(norm_mix_g[l])
        da, s5a, s5b, mla, rw = _inproj_call(lay, x_all, mod, g_mix, _mix_weight(w_in[l]))

        log2e = math.log2(math.e)
        gda = jnp.stack([jnp.tile(da_qk_norm_g[l, 0], 2 * DA_HEADS) * (DA_DIM ** -0.5 * log2e), jnp.tile(da_qk_norm_g[l, 1], 2 * DA_HEADS)])
        mla_pad = MLA_HEAD_PAD - MLA_NOPE - MLA_ROPE
        gml = jnp.stack([jnp.tile(jnp.pad(mla_qk_norm_g[l, 0], (0, mla_pad)), MLA_HEADS) * ((MLA_NOPE + MLA_ROPE) ** -0.5 * log2e),
                         jnp.tile(jnp.pad(mla_qk_norm_g[l, 1], (0, mla_pad)), MLA_HEADS)])
        wuq = bf(jnp.pad(_pad_heads(mla_w_uq[l], MLA_HEADS, MLA_NOPE + MLA_ROPE, 0, MLA_NOPE + MLA_ROPE), ((0, 64), (0, 0))))
        wuk = bf(_pad_heads(mla_w_ukv[l], MLA_HEADS, MLA_NOPE + MLA_VDIM, 0, MLA_NOPE))
        wuv = bf(_pad_heads(mla_w_ukv[l], MLA_HEADS, MLA_NOPE + MLA_VDIM, MLA_NOPE, MLA_NOPE + MLA_VDIM))
        consts = (g32, g64, gda.astype(F32), gml.astype(F32), row(jnp.pad(mla_cq_norm_g[l], (0, 64))), row(mla_ckv_norm_g[l]),
                  wuq, wuk, wuv)
        qd, kdt, vd, qm, kmt, vm = _qkprep_call(lay, da, mla, tabs, consts)

        lam32 = da_lambda[l].astype(F32)
        lmbda = (jnp.exp(jnp.sum(lam32[0] * lam32[1])) - jnp.exp(jnp.sum(lam32[2] * lam32[3])) + lambda_init).reshape(1, 1)
        subln = row(jnp.tile(da_subln_g[l], DA_HEADS) * (1.0 - lambda_init))
        ya = _attention(lay, qd, kdt, vd, (lmbda, subln, g64), True, need_ctx, "diff_attn")
        yc = _attention(lay, qm, kmt, vm, (lmbda, subln, g64), False, need_ctx, "mla_attn")

        ys_a, ys_b = _s5_scan(lay, s5a, s5b, s5_mats, l)
        yb = _s5_glu_call(s5a, s5b, ys_a, ys_b, row(s5_d[l]), bf(s5_w_glu[l]), row(s5_b_glu[l]), tm_big)

        pre_consts = (row(rw_mu[l]), g64, row(rw_k_k[l]), row(rw_k_a[l]), row(rw_r_k[l]),
                      rw_w0[l].reshape(2, 1, MIX_W), bf(rw_w1[l]), bf(rw_w2[l]),
                      rw_a0[l].reshape(2, 1, MIX_W), bf(rw_a1[l]), bf(rw_a2[l]), bf(rw_g1[l]), bf(rw_g2[l]))
        yd = _rwkv_branch(lay, rw, pre_consts, g64, row(rw_ln_g[l]), row(rw_ln_b[l]))

        x_mid = _merge_call(lay, need_ctx, x_all, mod, g_mix, bf(w_in[l][:, 2384:]), ya, yb, yc, yd,
                            bf(w_branch[l]), bf(w_out[l]))
        f, comb_t, gid = _router_call(lay, need_ctx, x_mid, mod, row(norm_ffn_g[l]), wr_hi, wr_lo, r_bias)
        x_all = _moe_call(lay, need_ctx, f, comb_t.T, gid, bf(exp_w_gate[l]), bf(exp_w_up[l]), bf(exp_w_down[l]), x_mid, mod)
    return x_all.reshape(b, n_lat, dm)
```

```python
import functools
import math

import numpy as np
import jax
import jax.numpy as jnp
from jax import lax
from jax.experimental import pallas as pl
from jax.experimental.pallas import tpu as pltpu

F32 = jnp.float32
BF16 = jnp.bfloat16

D_MODEL = 1024
GRID_W = 64
ROPE_BASE = 10000.0
EPS = 1e-6
DA_HEADS, DA_DIM, DA_VDIM = 4, 32, 64
S5_GROUPS, S5_CH, S5_STATE = 16, 16, 64
MLA_HEADS, MLA_NOPE, MLA_ROPE, MLA_VDIM = 4, 32, 16, 64
MLA_Q_RANK = 192
MLA_HEAD_PAD = 64
RW_HEADS, RW_DIM = 4, 64
RW_LN_EPS = 64e-5
N_EXPERTS, N_GROUPS, EXPERTS_PER_GROUP = 16, 4, 4
D_FF = 512
MIX_W = 256

S5_CHUNK = 8
S5_FLAT = S5_CHUNK * MIX_W
S5_STATE_W = S5_GROUPS * S5_STATE
RW_CHUNK = 16
RW_TILE = 128
RW_PREP_TILES = 2
_NT = (((1,), (1,)), ((), ()))
TOKEN_TILE = 256
PROJ_TILE = 768
MOE_TILE = 1152
MOE_BLOCK = 384
MOE_OVER = 128

_DA_W, _S5_W, _MLA_W, _RW_W = 768, 256, 640, 1024
_MIX_COLS = _DA_W + _S5_W + _MLA_W + _RW_W

V7X_VMEM_BYTES = 64 * 2**20
_VMEM_LIMIT = V7X_VMEM_BYTES - 8 * 2**20


def _cparams(*sem):
    return pltpu.CompilerParams(dimension_semantics=sem, vmem_limit_bytes=_VMEM_LIMIT)


def _full(a):
    return pl.BlockSpec(a.shape, lambda *_, nd=a.ndim: (0,) * nd)


def _split_dot(x, w, terms=2):
    acc = None
    rem = x
    for i in range(terms):
        part = rem.astype(BF16)
        d = jnp.dot(part, w, preferred_element_type=F32)
        acc = d if acc is None else acc + d
        if i + 1 < terms:
            rem = rem - part.astype(F32)
    return acc


def _split_dot_rhs(w, x):
    hi = x.astype(BF16)
    lo = (x - hi.astype(F32)).astype(BF16)
    return jnp.dot(w, hi, preferred_element_type=F32) + jnp.dot(w, lo, preferred_element_type=F32)


def _modulate(x, g, shift, scale):
    xn = x * lax.rsqrt(jnp.mean(x * x, axis=-1, keepdims=True) + EPS)
    return xn * g * (1.0 + scale) + shift


def _sigmoid(x):
    return 1.0 / (1.0 + jnp.exp(-x))


def _group_rms(x, ones_bd, inv_n, gain):
    ms = _split_dot(x * x, ones_bd) * inv_n
    return x * lax.rsqrt(ms + EPS) * gain


def _lane_partner(x, half, period, first_end):
    n = x.shape[1]
    lane = lax.broadcasted_iota(jnp.int32, x.shape, 1)
    up = pltpu.roll(x, n - half, axis=1)
    down = pltpu.roll(x, half, axis=1)
    return jnp.where((lane & (period - 1)) < first_end, up, down)


def _rope(x, cos_t, sin_t, half, period, first_end):
    return x * cos_t + _lane_partner(x, half, period, first_end) * sin_t


def _ada_kernel(c_ref, w_ref, b_ref, o_ref):
    c = c_ref[...]
    s = c * _sigmoid(c)
    o_ref[0] = jnp.dot(s.astype(BF16), w_ref[0].astype(BF16), preferred_element_type=F32) + b_ref[0]


def _ada_call(cc, w_ada, b_ada):
    depth, dm, n = w_ada.shape
    tn = n // 4
    return pl.pallas_call(
        _ada_kernel,
        grid=(depth, n // tn),
        in_specs=[
            pl.BlockSpec(cc.shape, lambda l, j: (0, 0)),
            pl.BlockSpec((1, dm, tn), lambda l, j: (l, 0, j)),
            pl.BlockSpec((1, 1, tn), lambda l, j: (l, 0, j)),
        ],
        out_specs=pl.BlockSpec((1, cc.shape[0], tn), lambda l, j: (l, 0, j)),
        out_shape=jax.ShapeDtypeStruct((depth, cc.shape[0], n), F32),
        compiler_params=_cparams("parallel", "parallel"),
        name="ada_mod",
    )(cc, w_ada, b_ada.reshape(depth, 1, n))


class _Layout:
    def __init__(self, n_batch, n_ctx, n_lat):
        t = TOKEN_TILE
        assert n_ctx % t == 0 and n_lat % t == 0
        self.b, self.n_ctx, self.n_lat, self.n_tot = n_batch, n_ctx, n_lat, n_ctx + n_lat
        self.t = t
        self.ctx_tiles, self.lat_tiles, self.seq_tiles = n_ctx // t, n_lat // t, (n_ctx + n_lat) // t

    def rows(self, with_ctx):
        return self.b * (self.n_tot if with_ctx else self.n_lat)

    def n_tiles(self, with_ctx):
        return self.b * (self.seq_tiles if with_ctx else self.lat_tiles)

    def src_tile(self, with_ctx):
        if with_ctx:
            return lambda i: i
        return lambda i: (i // self.lat_tiles) * self.seq_tiles + i % self.lat_tiles + self.ctx_tiles


def _mod_row(modb_ref, modc_ref, r, n_rows, ctx_rows, tiles_per_seq):
    per_batch = modb_ref[0, r:r + 1, :]
    if not ctx_rows:
        return per_batch
    row = lax.broadcasted_iota(jnp.int32, (n_rows, 1), 0)
    first = pl.program_id(0) % tiles_per_seq == 0
    return jnp.where(jnp.logical_and(first, row < ctx_rows), modc_ref[0, r:r + 1, :], per_batch)


def _inproj_kernel(x_ref, modb_ref, modc_ref, g_ref, w_ref, da_ref, s5a_ref, s5b_ref, mla_ref, rw_ref,
                   *, ctx_rows, tiles_per_seq):
    x = x_ref[...]
    mrow = lambda r: _mod_row(modb_ref, modc_ref, r, x.shape[0], ctx_rows, tiles_per_seq)
    h = _modulate(x, g_ref[...], mrow(0), mrow(1))
    acc = jnp.dot(h.astype(BF16), w_ref[...], preferred_element_type=F32)
    da_ref[...] = acc[:, 0:_DA_W]
    s5a_ref[...] = acc[:, _DA_W:_DA_W + _S5_W // 2]
    s5b_ref[...] = acc[:, _DA_W + _S5_W // 2:_DA_W + _S5_W]
    mla_ref[...] = acc[:, _DA_W + _S5_W:_DA_W + _S5_W + _MLA_W]
    rw_ref[...] = acc[:, _DA_W + _S5_W + _MLA_W:_MIX_COLS]


def _seq_tile(lay):
    tm = PROJ_TILE if lay.n_tot % PROJ_TILE == 0 and lay.n_ctx <= PROJ_TILE else lay.t
    return tm, lay.n_tot // tm


def _inproj_call(lay, x_all, mod, g, w_mix):
    t = x_all.shape[0]
    tm, tps = _seq_tile(lay)
    widths = (_DA_W, _S5_W // 2, _S5_W // 2, _MLA_W, _RW_W)
    return pl.pallas_call(
        functools.partial(_inproj_kernel, ctx_rows=lay.n_ctx, tiles_per_seq=tps),
        grid=(t // tm,),
        in_specs=[
            pl.BlockSpec((tm, D_MODEL), lambda i: (i, 0)),
            pl.BlockSpec((1, 6, D_MODEL), lambda i: (i // tps, 0, 0)),
            pl.BlockSpec((1, 6, D_MODEL), lambda i: (lay.b, 0, 0)),
            _full(g), _full(w_mix),
        ],
        out_specs=[pl.BlockSpec((tm, w), lambda i: (i, 0)) for w in widths],
        out_shape=[jax.ShapeDtypeStruct((t, w), F32) for w in widths],
        compiler_params=_cparams("parallel"),
        name="in_proj",
    )(x_all, mod, mod, g, w_mix)


def _qkprep_kernel(da_ref, mla_ref, cda_ref, sda_ref, cml_ref, sml_ref, g32_ref, g64_ref,
                   gda_ref, gml_ref, cqg_ref, ckvg_ref, wuq_ref, wuk_ref, wuv_ref,
                   qd_ref, kd_ref, vd_ref, qm_ref, km_ref, vm_ref):
    g32 = g32_ref[...]
    g64 = g64_ref[...]
    cda, sda = cda_ref[...], sda_ref[...]
    q = _group_rms(da_ref[:, 0:MIX_W], g32, 1.0 / DA_DIM, gda_ref[0:1, :])
    qd_ref[...] = _rope(q, cda, sda, DA_DIM // 2, DA_DIM, DA_DIM // 2).astype(BF16)
    k = _group_rms(da_ref[:, MIX_W:2 * MIX_W], g32, 1.0 / DA_DIM, gda_ref[1:2, :])
    kd_ref[0] = _rope(k, cda, sda, DA_DIM // 2, DA_DIM, DA_DIM // 2).T.astype(BF16)
    vd_ref[...] = da_ref[:, 2 * MIX_W:3 * MIX_W].astype(BF16)

    cml, sml = cml_ref[...], sml_ref[...]
    cq = mla_ref[:, 0:256]
    cqn = cq * lax.rsqrt(jnp.sum(cq * cq, axis=-1, keepdims=True) * (1.0 / MLA_Q_RANK) + EPS) * cqg_ref[...]
    q = jnp.dot(cqn.astype(BF16), wuq_ref[...], preferred_element_type=F32)
    ckv = mla_ref[:, 256:384]
    ckvn = ckv * lax.rsqrt(jnp.mean(ckv * ckv, axis=-1, keepdims=True) + EPS) * ckvg_ref[...]
    ckvb = ckvn.astype(BF16)
    k = jnp.dot(ckvb, wuk_ref[...], preferred_element_type=F32) + mla_ref[:, 384:640]
    vm_ref[...] = jnp.dot(ckvb, wuv_ref[...], preferred_element_type=F32).astype(BF16)
    inv_n = 1.0 / (MLA_NOPE + MLA_ROPE)
    half = MLA_ROPE // 2
    q = _group_rms(q, g64, inv_n, gml_ref[0:1, :])
    qm_ref[...] = _rope(q, cml, sml, half, MLA_HEAD_PAD, MLA_NOPE + half).astype(BF16)
    k = _group_rms(k, g64, inv_n, gml_ref[1:2, :])
    km_ref[0] = _rope(k, cml, sml, half, MLA_HEAD_PAD, MLA_NOPE + half).T.astype(BF16)


def _qkprep_call(lay, da, mla, tabs, consts):
    t = da.shape[0]
    tm, st = _seq_tile(lay)
    row = pl.BlockSpec((tm, MIX_W), lambda i: (i, 0))
    key_t = pl.BlockSpec((1, MIX_W, tm), lambda i: (i // st, 0, i % st))
    in_specs = [pl.BlockSpec((tm, _DA_W), lambda i: (i, 0)), pl.BlockSpec((tm, _MLA_W), lambda i: (i, 0))]
    in_specs += [pl.BlockSpec((tm, MIX_W), lambda i: (i % st, 0)) for _ in tabs]
    in_specs += [_full(a) for a in consts]
    tok = jax.ShapeDtypeStruct((t, MIX_W), BF16)
    keys = jax.ShapeDtypeStruct((lay.b, MIX_W, lay.n_tot), BF16)
    return pl.pallas_call(
        _qkprep_kernel,
        grid=(t // tm,),
        in_specs=in_specs,
        out_specs=[row, key_t, row, row, key_t, row],
        out_shape=[tok, keys, tok, tok, keys, tok],
        compiler_params=_cparams("parallel"),
        name="qk_prep",
    )(da, mla, *tabs, *consts)


def _softmax_parts(s):
    p = jnp.exp2(s - jnp.max(s, axis=-1, keepdims=True))
    return p, 1.0 / jnp.sum(p, axis=-1, keepdims=True)


def _attn_heads(q, kt_ref, v_ref, nk, diff, lam):
    lane = lax.broadcasted_iota(jnp.int32, (q.shape[0], MIX_W), 1)
    v = v_ref[0, 0:nk, :]
    acc = jnp.zeros((q.shape[0], MIX_W), F32)
    dk = DA_DIM if diff else MLA_HEAD_PAD
    per_head = 2 if diff else 1

    def scores(h):
        return [jnp.dot(q[:, e * dk:(e + 1) * dk], kt_ref[0, e * dk:(e + 1) * dk, 0:nk], preferred_element_type=F32)
                for e in range(per_head * h, per_head * (h + 1))]

    ahead = scores(0)
    for h in range(DA_HEADS):
        s = ahead
        if h + 1 < DA_HEADS:
            ahead = scores(h + 1)
        if diff:
            p0, r0 = _softmax_parts(s[0])
            p1, r1 = _softmax_parts(s[1])
            o = jnp.dot((p0 * r0 - p1 * (r1 * lam)).astype(BF16), v, preferred_element_type=F32)
        else:
            p, r = _softmax_parts(s[0])
            o = jnp.dot(p.astype(BF16), v, preferred_element_type=F32) * r
        in_head = jnp.logical_and(lane >= h * DA_VDIM, lane < (h + 1) * DA_VDIM)
        acc = jnp.where(in_head, o, acc)
    return acc


def _attn_kernel(q_ref, kt_ref, v_ref, lam_ref, gain_ref, g64_ref, o_ref, *, diff, n_ctx, n_tot, ctx_tiles):
    q = q_ref[...]
    lam = lam_ref[...]

    def run(nk):
        o = _attn_heads(q, kt_ref, v_ref, nk, diff, lam)
        if diff:
            o = _group_rms(o, g64_ref[...], 1.0 / DA_VDIM, gain_ref[...])
        o_ref[...] = o.astype(BF16)

    if ctx_tiles:
        is_ctx = pl.program_id(1) < ctx_tiles
        pl.when(is_ctx)(lambda: run(n_ctx))
        pl.when(jnp.logical_not(is_ctx))(lambda: run(n_tot))
    else:
        run(n_tot)


def _attention(lay, q, kt, v, extra, diff, with_ctx, name):
    tq = lay.t
    tiles = lay.seq_tiles if with_ctx else lay.lat_tiles
    off = 0 if with_ctx else lay.ctx_tiles
    v3 = v.reshape(lay.b, lay.n_tot, MIX_W)
    kern = functools.partial(_attn_kernel, diff=diff, n_ctx=lay.n_ctx, n_tot=lay.n_tot,
                             ctx_tiles=lay.ctx_tiles if with_ctx else 0)
    return pl.pallas_call(
        kern,
        grid=(lay.b, tiles),
        in_specs=[
            pl.BlockSpec((tq, MIX_W), lambda b, j: (b * lay.seq_tiles + j + off, 0)),
            pl.BlockSpec((1, MIX_W, lay.n_tot), lambda b, j: (b, 0, 0)),
            pl.BlockSpec((1, lay.n_tot, MIX_W), lambda b, j: (b, 0, 0)),
        ] + [_full(a) for a in extra],
        out_specs=pl.BlockSpec((tq, MIX_W), lambda b, j: (b * tiles + j, 0)),
        out_shape=jax.ShapeDtypeStruct((lay.rows(with_ctx), MIX_W), BF16),
        compiler_params=_cparams("parallel", "parallel"),
        name=name,
    )(q, kt, v3, *extra)


def _chunk_rows(ua_ref, ub_ref):
    n = ua_ref.shape[0] // S5_CHUNK
    parts = []
    for s in range(S5_CHUNK):
        rows = pl.ds(s, n, stride=S5_CHUNK)
        parts += [ua_ref[rows, :], ub_ref[rows, :]]
    return jnp.concatenate(parts, axis=1).astype(BF16)


def _s5_proj_kernel(ua_ref, ub_ref, bre_ref, bim_ref, sre_ref, sim_ref):
    u = _chunk_rows(ua_ref, ub_ref)
    sre_ref[0] = jnp.dot(u, bre_ref[0, 0], preferred_element_type=F32)
    sim_ref[0] = jnp.dot(u, bim_ref[0, 0], preferred_element_type=F32)


def _s5_rec_kernel(sre_ref, sim_ref, are_ref, aim_ref, hre_ref, him_ref, *, n_batch, n_chunks, ctx_chunks):
    rev = pl.program_id(0) == 1
    ar, ai = are_ref[0, 0], aim_ref[0, 0]
    sre, sim, hre, him = sre_ref.at[0], sim_ref.at[0], hre_ref.at[0], him_ref.at[0]

    def step(i, carry):
        hr, hi = carry
        k_rev = jnp.where(i < ctx_chunks, ctx_chunks - 1 - i, n_chunks - 1 + ctx_chunks - i)
        k = jnp.where(rev, k_rev, i)
        rows = pl.ds(k, n_batch, stride=n_chunks)
        hre[rows, :] = hr
        him[rows, :] = hi
        return ar * hr - ai * hi + sre[rows, :], ar * hi + ai * hr + sim[rows, :]

    zero = jnp.zeros((n_batch, 128), F32)
    lax.fori_loop(0, n_chunks, step, (zero, zero), unroll=2)


def _s5_out_kernel(ua_ref, ub_ref, hre_ref, him_ref, m_ref, cre_ref, cim_ref, ya_ref, yb_ref):
    y = jnp.dot(_chunk_rows(ua_ref, ub_ref), m_ref[0, 0], preferred_element_type=F32)
    y = y + jnp.dot(hre_ref[0].astype(BF16), cre_ref[0, 0], preferred_element_type=F32)
    y = y + jnp.dot(him_ref[0].astype(BF16), cim_ref[0, 0], preferred_element_type=F32)
    n = y.shape[0]
    ya, yb = ya_ref.at[0], yb_ref.at[0]
    for s in range(S5_CHUNK):
        rows = pl.ds(s, n, stride=S5_CHUNK)
        ya[rows, :] = y[:, s * MIX_W:s * MIX_W + 128]
        yb[rows, :] = y[:, s * MIX_W + 128:(s + 1) * MIX_W]


def _s5_mats(lam_re, lam_im, log_dt, b_re, b_im, c_re, c_im):
    hp = lax.Precision.HIGHEST
    L, G, P, CH = S5_CHUNK, S5_GROUPS, S5_STATE, S5_CH
    lr, li = lam_re.astype(F32), lam_im.astype(F32)
    dt = jnp.exp(log_dt.astype(F32))[..., None]
    zr, zi = lr * dt, li * dt
    j = jnp.arange(L + 1, dtype=F32)[:, None, None, None]
    mag = jnp.exp(zr[None] * j)
    pw_re, pw_im = mag * jnp.cos(zi[None] * j), mag * jnp.sin(zi[None] * j)
    nr, ni = pw_re[1] - 1.0, pw_im[1]
    den = lr * lr + li * li
    cr, ci = (nr * lr + ni * li) / den, (ni * lr - nr * li) / den
    bre, bim = b_re.astype(F32), b_im.astype(F32)
    bb_re = cr[..., None] * bre - ci[..., None] * bim
    bb_im = cr[..., None] * bim + ci[..., None] * bre
    x_re = pw_re[..., None] * bb_re[None] - pw_im[..., None] * bb_im[None]
    x_im = pw_re[..., None] * bb_im[None] + pw_im[..., None] * bb_re[None]
    cre, cim = c_re.astype(F32), c_im.astype(F32)
    kern = (jnp.einsum('dgcp,jdgpe->dgjce', cre, x_re[:L], precision=hp)
            - jnp.einsum('dgcp,jdgpe->dgjce', cim, x_im[:L], precision=hp))
    def spread_mask(a, b):
        spread = jnp.asarray(np.tile(np.eye(b, dtype=np.float32), (1, G)))
        mask = jnp.asarray(np.kron(np.eye(G, dtype=np.float32), np.ones((a, b), np.float32)))
        return spread, mask

    kt = kern.transpose(0, 2, 1, 4, 3)
    xt_re, xt_im = x_re.transpose(1, 0, 2, 4, 3), x_im.transpose(1, 0, 2, 4, 3)
    pwt_re, pwt_im = pw_re.transpose(1, 0, 2, 3)[:, :, :, :, None], pw_im.transpose(1, 0, 2, 3)[:, :, :, :, None]
    cret, cimt = cre.transpose(0, 1, 3, 2)[:, None], cim.transpose(0, 1, 3, 2)[:, None]
    ca_re, ca_im = cret * pwt_re - cimt * pwt_im, -(cret * pwt_im + cimt * pwt_re)
    s_idx, t_idx = np.arange(L)[:, None], np.arange(L)[None, :]
    k_st, xb_re, xb_im, cq_re, cq_im = [], [], [], [], []
    for d in range(2):
        lag = (t_idx - s_idx) if d == 0 else (s_idx - t_idx)
        k_st.append(jnp.where(jnp.asarray(lag >= 0)[:, :, None, None, None], kt[d][np.clip(lag, 0, L - 1)], 0.0))
        pw = np.arange(L - 1, -1, -1) if d == 0 else np.arange(L)
        xb_re.append(xt_re[d][pw])
        xb_im.append(xt_im[d][pw])
        q = np.arange(1, L + 1) if d == 0 else np.arange(L, 0, -1)
        cq_re.append(ca_re[d][q])
        cq_im.append(ca_im[d][q])
    sp, mk = spread_mask(CH, CH)
    m = jnp.einsum('dstrb,bc->dsrtc', jnp.stack(k_st).reshape(2, L, L, G * CH, CH), sp, precision=hp) * mk[:, None, :]
    m = m.astype(BF16).reshape(2, L * G * CH, L * G * CH)
    sp, mk = spread_mask(CH, P)
    to_b = lambda x: (jnp.einsum('dsrb,bc->dsrc', jnp.stack(x).reshape(2, L, G * CH, P), sp, precision=hp) * mk
                      ).astype(BF16).reshape(2, L * G * CH, G * P)
    sp_c, mk_c = spread_mask(P, CH)
    to_c = lambda x: (jnp.einsum('dtrb,bc->drtc', jnp.stack(x).reshape(2, L, G * P, CH), sp_c, precision=hp)
                      * mk_c[:, None, :]).astype(BF16).reshape(2, G * P, L * G * CH)
    a_re, a_im = pw_re[L].reshape(2, 1, G * P), pw_im[L].reshape(2, 1, G * P)
    return m, to_b(xb_re), to_b(xb_im), to_c(cq_re), to_c(cq_im), a_re, a_im


def _s5_scan(lay, ua, ub, mats, layer):
    m, b_r, b_i, c_r, c_i, a_re, a_im = mats
    n_chunks = lay.n_tot // S5_CHUNK
    rows = lay.b * n_chunks
    tr = min(lay.t, rows)
    tok = tr * S5_CHUNK
    half = MIX_W // 2
    wspec = lambda a: pl.BlockSpec((1, 1) + a.shape[2:], lambda d, i: (layer, d, 0, 0))
    state = jax.ShapeDtypeStruct((2, rows, S5_STATE_W), F32)
    sblk = pl.BlockSpec((1, tr, S5_STATE_W), lambda d, i: (d, i, 0))
    ublk = pl.BlockSpec((tok, half), lambda d, i: (i, 0))
    s_re, s_im = pl.pallas_call(
        _s5_proj_kernel,
        grid=(2, rows // tr),
        in_specs=[ublk, ublk, wspec(b_r), wspec(b_i)],
        out_specs=[sblk, sblk],
        out_shape=[state, state],
        compiler_params=_cparams("parallel", "parallel"),
        name="s5_proj",
    )(ua, ub, b_r, b_i)
    col = pl.BlockSpec((1, rows, 128), lambda d, j: (d, 0, j))
    acol = pl.BlockSpec((1, 1, 1, 128), lambda d, j: (layer, d, 0, j))
    h_re, h_im = pl.pallas_call(
        functools.partial(_s5_rec_kernel, n_batch=lay.b, n_chunks=n_chunks, ctx_chunks=lay.n_ctx // S5_CHUNK),
        grid=(2, S5_STATE_W // 128),
        in_specs=[col, col, acol, acol],
        out_specs=[col, col],
        out_shape=[state, state],
        compiler_params=_cparams("parallel", "parallel"),
        name="s5_rec",
    )(s_re, s_im, a_re, a_im)
    yblk = pl.BlockSpec((1, tok, half), lambda d, i: (d, i, 0))
    yshape = jax.ShapeDtypeStruct((2, lay.b * lay.n_tot, half), F32)
    return pl.pallas_call(
        _s5_out_kernel,
        grid=(2, rows // tr),
        in_specs=[ublk, ublk, sblk, sblk, wspec(m), wspec(c_r), wspec(c_i)],
        out_specs=[yblk, yblk],
        out_shape=[yshape, yshape],
        compiler_params=_cparams("parallel", "parallel"),
        name="s5_out",
    )(ua, ub, h_re, h_im, m, c_r, c_i)


def _s5_glu_kernel(ua_ref, ub_ref, ya_ref, yb_ref, d_ref, w_ref, b_ref, o_ref):
    u = jnp.concatenate([ua_ref[...], ub_ref[...]], axis=1)
    y = d_ref[...] * u + jnp.concatenate([ya_ref[0] + ya_ref[1], yb_ref[0] + yb_ref[1]], axis=1)
    z = 0.5 * y * (1.0 + jnp.tanh(math.sqrt(2.0 / math.pi) * (y + 0.044715 * (y * y * y))))
    gate = _sigmoid(jnp.dot(z.astype(BF16), w_ref[...], preferred_element_type=F32) + b_ref[...])
    o_ref[...] = (z * gate).astype(BF16)


def _s5_glu_call(ua, ub, ya, yb, d, w, bias, tm):
    t, half = ua.shape
    urow = pl.BlockSpec((tm, half), lambda i: (i, 0))
    yrow = pl.BlockSpec((2, tm, half), lambda i: (0, i, 0))
    return pl.pallas_call(
        _s5_glu_kernel,
        grid=(t // tm,),
        in_specs=[urow, urow, yrow, yrow, _full(d), _full(w), _full(bias)],
        out_specs=pl.BlockSpec((tm, MIX_W), lambda i: (i, 0)),
        out_shape=jax.ShapeDtypeStruct((t, MIX_W), BF16),
        compiler_params=_cparams("parallel"),
        name="s5_glu",
    )(ua, ub, ya, yb, d, w, bias)


def _rw_pre_kernel(x_ref, prev_ref, next_ref, mu_ref, g64_ref, kk_g_ref, ka_ref, rk_ref,
                   w0_ref, w1_ref, w2_ref, a0_ref, a1_ref, a2_ref, g1_ref, g2_ref,
                   r_ref, v_ref, kk_ref, lw_ref, kka_ref, km_ref, bon_ref, gate_ref,
                   *, seq_tiles, n_ctx, n_tot):
    x = x_ref[...]
    n = x.shape[0]
    row = lax.broadcasted_iota(jnp.int32, (n, 1), 0)
    pos = (pl.program_id(0) % seq_tiles) * n + row
    left = jnp.where(row == 0, prev_ref[0, 7:8, :], pltpu.roll(x, 1, axis=0))
    left = jnp.where(jnp.logical_or(pos == 0, pos == n_ctx), 0.0, left)
    right = jnp.where(row == n - 1, next_ref[0, 0:1, :], pltpu.roll(x, n - 1, axis=0))
    right = jnp.where(jnp.logical_or(pos == n_ctx - 1, pos == n_tot - 1), 0.0, right)
    x = x + (0.5 * (left + right) - x) * mu_ref[...]
    r, k, v, xd = (x[:, i * MIX_W:(i + 1) * MIX_W] for i in range(4))
    g64 = g64_ref[...]
    kscaled = k * kk_g_ref[...]
    kk = kscaled / jnp.maximum(jnp.sqrt(_split_dot(kscaled * kscaled, g64)), 1e-12)
    xdb = xd.astype(BF16)
    r_ref[...] = r
    v_ref[...] = v
    kk_ref[...] = kk
    km_sum = None
    for d in range(2):
        lo = jnp.tanh(jnp.dot(xdb, w1_ref[d], preferred_element_type=F32))
        w_raw = w0_ref[d] + jnp.dot(lo.astype(BF16), w2_ref[d], preferred_element_type=F32)
        lw_ref[d] = -_sigmoid(w_raw) * math.exp(-0.5)
        ar = jnp.dot(xdb, a1_ref[d], preferred_element_type=F32)
        a = _sigmoid(a0_ref[d] + jnp.dot(ar.astype(BF16), a2_ref[d], preferred_element_type=F32))
        km = k * (1.0 + (a - 1.0) * ka_ref[...])
        kka_ref[d] = kk * a
        km_ref[d] = km
        km_sum = km if km_sum is None else km_sum + km
    bon_ref[...] = _split_dot(r * km_sum * rk_ref[...], g64) * v
    gr = _sigmoid(jnp.dot(xdb, g1_ref[...], preferred_element_type=F32))
    gate_ref[...] = jnp.dot(gr.astype(BF16), g2_ref[...], preferred_element_type=F32)


def _rw_pre_call(lay, rw, consts):
    t = rw.shape[0]
    tr, seq_tiles = _seq_tile(lay)
    nt = t // tr
    g8 = tr // 8
    rw8 = rw.reshape(t // 8, 8, _RW_W)
    row = pl.BlockSpec((tr, MIX_W), lambda i: (i, 0))
    row2 = pl.BlockSpec((2, tr, MIX_W), lambda i: (0, i, 0))
    sd = jax.ShapeDtypeStruct((t, MIX_W), F32)
    sd2 = jax.ShapeDtypeStruct((2, t, MIX_W), F32)
    return pl.pallas_call(
        functools.partial(_rw_pre_kernel, seq_tiles=seq_tiles, n_ctx=lay.n_ctx, n_tot=lay.n_tot),
        grid=(nt,),
        in_specs=[pl.BlockSpec((tr, _RW_W), lambda i: (i, 0)),
                  pl.BlockSpec((1, 8, _RW_W), lambda i: (jnp.maximum(i * g8 - 1, 0), 0, 0)),
                  pl.BlockSpec((1, 8, _RW_W), lambda i: (jnp.minimum((i + 1) * g8, t // 8 - 1), 0, 0))]
                 + [_full(a) for a in consts],
        out_specs=[row, row, row, row2, row2, row2, row, row],
        out_shape=[sd, sd, sd, sd2, sd2, sd2, sd, sd],
        compiler_params=_cparams("parallel"),
        name="rwkv_pre",
    )(rw, rw8, rw8, *consts)


def _head_masks(shape, lane_axis, seg):
    lane = lax.broadcasted_iota(jnp.int32, shape, lane_axis)
    return [jnp.logical_and(lane >= h * seg, lane < (h + 1) * seg) for h in range(RW_HEADS)]


def _rw_prep_kernel(*refs, rev):
    tiles = [_rw_prep_tile(sub, *refs, rev=rev) for sub in range(RW_PREP_TILES)]
    while tiles:
        tiles = [t for t in tiles if next(t, None) is not None]


def _rw_prep_tile(sub, r_ref, kk_ref, v_ref, lw_ref, ka_ref, km_ref, perm_ref, permt_ref, g_ref, eye_ref,
                  br_ref, ck_ref, uvt_ref, y0_ref, pc_ref, *, rev):
    C, NC = RW_CHUNK, RW_TILE // RW_CHUNK
    perm, permt, g64, eye4 = perm_ref[...], permt_ref[...], g_ref[...], eye_ref[...]
    tok = slice(sub * RW_TILE, (sub + 1) * RW_TILE)
    nat = jnp.concatenate([r_ref[0, tok, :], kk_ref[0, tok, :], v_ref[0, tok, :],
                           lw_ref[0, 0, tok, :], ka_ref[0, 0, tok, :], km_ref[0, 0, tok, :]], axis=1)
    hi = nat.astype(BF16)
    lo = (nat - hi.astype(F32)).astype(BF16)
    pm = jnp.dot(perm, jnp.concatenate([hi, lo], axis=0), preferred_element_type=F32)
    r, kk, v, lw, ka, km = (pm[:, i * MIX_W:(i + 1) * MIX_W] for i in range(6))
    slab = lambda x, j: x[j * NC:(j + 1) * NC, :]
    order = list(range(C))[::-1] if rev else list(range(C))
    pos = {j: i for i, j in enumerate(order)}
    cum, run = {}, None
    for j in order:
        run = slab(lw, j) if run is None else run + slab(lw, j)
        cum[j] = run
    tot = run
    yield True
    bh, ch, kh, rh, cp, kp, vv = {}, {}, {}, {}, {}, {}, {}
    for j in range(C):
        e_inv, e_end = jnp.exp(-cum[j]), jnp.exp(tot - cum[j])
        bh[j] = -slab(kk, j) * jnp.exp(cum[j] - slab(lw, j))
        ch[j], kh[j] = slab(ka, j) * e_inv, slab(km, j) * e_inv
        rh[j] = slab(r, j) * jnp.exp(cum[j])
        cp[j], kp[j] = slab(ka, j) * e_end, slab(km, j) * e_end
        vv[j] = slab(v, j)
    strict = [(t, s) for t in order for s in order if pos[s] < pos[t]]
    incl = [(t, s) for t in order for s in order if pos[s] <= pos[t]]
    def head_dots(lhs, rhs, pairs):
        prods = jnp.concatenate([lhs[t] * rhs[s] for t, s in pairs], axis=0).astype(BF16)
        gram = jnp.dot(prods, g64, preferred_element_type=F32)
        return {p: gram[i * NC:(i + 1) * NC, :] for i, p in enumerate(pairs)}

    yield True
    acb = head_dots(bh, ch, strict)
    yield True
    akb = head_dots(bh, kh, strict)
    yield True
    mcr = head_dots(rh, ch, incl)
    yield True
    mkr = head_dots(rh, kh, incl)
    yield True
    bt, u0 = {}, {}
    for t in order:
        b_acc, u_acc = bh[t], jnp.zeros_like(bh[t])
        for s in order:
            if pos[s] < pos[t]:
                b_acc = b_acc + acb[(t, s)] * bt[s]
                u_acc = u_acc + akb[(t, s)] * vv[s] + acb[(t, s)] * u0[s]
        bt[t], u0[t] = b_acc, u_acc
        yield True
    rt, y0 = {}, {}
    for t in order:
        r_acc, y_acc = rh[t], jnp.zeros_like(rh[t])
        for s in order:
            if pos[s] <= pos[t]:
                r_acc = r_acc + mcr[(t, s)] * bt[s]
                y_acc = y_acc + mcr[(t, s)] * u0[s] + mkr[(t, s)] * vv[s]
        rt[t], y0[t] = r_acc, y_acc
        yield True
    stackp = lambda dct: jnp.concatenate([dct[j] for j in range(C)], axis=0)
    b16 = lambda x: x.astype(BF16)
    y0p, u0p = stackp(y0), stackp(u0)
    y0h, u0h = b16(y0p), b16(u0p)
    cat = jnp.concatenate([b16(stackp(bt)), b16(stackp(rt)), b16(stackp(cp)), b16(stackp(kp)),
                           y0h, b16(y0p - y0h.astype(F32)), u0h, b16(u0p - u0h.astype(F32)), b16(stackp(vv))], axis=1)
    natural = jnp.dot(permt, cat, preferred_element_type=F32)
    seg = lambda i: natural[:, i * MIX_W:(i + 1) * MIX_W]
    yield True
    btn, rtn, cpn, kpn = b16(seg(0)), b16(seg(1)), b16(seg(2)), b16(seg(3))
    y0_ref[0, tok, :] = seg(4) + seg(5)
    u0h_n, u0l_n, vn = b16(seg(6)), b16(seg(7)), b16(seg(8))
    hm = _head_masks((C, MIX_W), 1, RW_DIM)
    zero = jnp.zeros((C, MIX_W), BF16)
    zh, zl = [], []
    for c in range(NC):
        rows = slice(c * C, (c + 1) * C)
        br_ref[0, sub * NC + c, 0:C, :] = btn[rows]
        br_ref[0, sub * NC + c, C:2 * C, :] = rtn[rows]
        ck_ref[0, sub * NC + c, 0:C, :] = cpn[rows]
        ck_ref[0, sub * NC + c, C:2 * C, :] = kpn[rows]
        for h in range(RW_HEADS):
            zh += [jnp.where(hm[h], u0h_n[rows], zero), jnp.where(hm[h], vn[rows], zero)]
            zl += [jnp.where(hm[h], u0l_n[rows], zero), zero]
    z = jnp.concatenate([jnp.concatenate(zh, axis=0), jnp.concatenate(zl, axis=0)], axis=1)
    uvt = lax.dot_general(eye4, z, _NT, preferred_element_type=F32)
    for c in range(NC):
        uvt_ref[0, sub * NC + c] = uvt[:, c * 2 * C * RW_HEADS:(c + 1) * 2 * C * RW_HEADS]
    pc_ref[0, sub * NC:(sub + 1) * NC, :] = jnp.exp(tot)


def _rw_prep_call(lay, shared, perdir, consts, rev):
    b, n_tot, tt = lay.b, lay.n_tot, RW_TILE * RW_PREP_TILES
    assert n_tot % tt == 0
    nck = n_tot // RW_CHUNK
    cpt = tt // RW_CHUNK
    d = 1 if rev else 0
    sh = [a.reshape(b, n_tot, MIX_W) for a in shared]
    pd = [a.reshape(2, b, n_tot, MIX_W) for a in perdir]
    tok = pl.BlockSpec((1, tt, MIX_W), lambda i, j: (i, j, 0))
    tok_d = pl.BlockSpec((1, 1, tt, MIX_W), lambda i, j: (d, i, j, 0))
    rows32 = pl.BlockSpec((1, cpt, 2 * RW_CHUNK, MIX_W), lambda i, j: (i, j, 0, 0))
    return pl.pallas_call(
        functools.partial(_rw_prep_kernel, rev=rev),
        grid=(b, n_tot // tt),
        in_specs=[tok] * 3 + [tok_d] * 3 + [_full(a) for a in consts],
        out_specs=[rows32, rows32,
                   pl.BlockSpec((1, cpt, RW_DIM, 2 * RW_CHUNK * RW_HEADS), lambda i, j: (i, j, 0, 0)),
                   tok,
                   pl.BlockSpec((1, cpt, MIX_W), lambda i, j: (i, j, 0))],
        out_shape=[jax.ShapeDtypeStruct((b, nck, 2 * RW_CHUNK, MIX_W), BF16),
                   jax.ShapeDtypeStruct((b, nck, 2 * RW_CHUNK, MIX_W), BF16),
                   jax.ShapeDtypeStruct((b, nck, RW_DIM, 2 * RW_CHUNK * RW_HEADS), F32),
                   jax.ShapeDtypeStruct((b, n_tot, MIX_W), F32),
                   jax.ShapeDtypeStruct((b, nck, MIX_W), F32)],
        compiler_params=_cparams("parallel", "parallel"),
        name="rwkv_prep_rev" if rev else "rwkv_prep_fwd",
    )(*sh, *pd, *consts)


def _rw_scan_kernel(brf, ckf, uvtf, pcf, brr, ckr, uvtr, pcr, ytf_ref, ytr_ref, s_scr, *, n_batch):
    @pl.when(pl.program_id(0) == 0)
    def _():
        s_scr[...] = jnp.zeros_like(s_scr)

    cpt = RW_TILE // RW_CHUNK
    hm = _head_masks((2 * RW_CHUNK, MIX_W), 1, RW_DIM)
    lane = lax.broadcasted_iota(jnp.int32, (RW_DIM, 2 * RW_CHUNK * RW_HEADS), 1)
    is_u = (lane & (2 * RW_CHUNK - 1)) < RW_CHUNK
    per_head = lambda x: jnp.concatenate([jnp.where(m, x, jnp.zeros_like(x)) for m in hm], axis=0)

    def refs_of(p, c):
        d, b = divmod(p, n_batch)
        refs = (brf, ckf, uvtf, pcf, ytf_ref) if d == 0 else (brr, ckr, uvtr, pcr, ytr_ref)
        return refs, b, (c if d == 0 else cpt - 1 - c)

    def step(c, carry):
        lhs = []
        for p in range(2 * n_batch):
            (br_ref, _, uvt_ref, _, yt_ref), b, cc = refs_of(p, c)
            s = s_scr[p]
            shi = s.astype(BF16)
            slo = (s - shi.astype(F32)).astype(BF16)
            w2 = lax.dot_general(jnp.concatenate([shi, slo], axis=0), per_head(br_ref[b, cc]), _NT, preferred_element_type=F32)
            w = w2[:RW_DIM] + w2[RW_DIM:]
            yt_ref[b, cc] = w
            uvt = uvt_ref[b, cc]
            lhs.append(jnp.where(is_u, w + uvt, uvt).astype(BF16))
        for p in range(2 * n_batch):
            (_, ck_ref, _, pc_ref, _), b, cc = refs_of(p, c)
            s_scr[p] = (s_scr[p] * pc_ref[b, pl.ds(cc, 1), :]
                        + jnp.dot(lhs[p], per_head(ck_ref[b, cc]), preferred_element_type=F32))
        return carry

    lax.fori_loop(0, cpt, step, 0)


def _rw_scan_call(lay, fwd, rev):
    b, n_tot, tt = lay.b, lay.n_tot, RW_TILE
    assert lay.n_ctx % tt == 0 and lay.n_lat % tt == 0
    nt, ct = n_tot // tt, lay.n_ctx // tt
    cpt = tt // RW_CHUNK
    rev_tile = lambda i: jnp.where(i < ct, ct - 1 - i, nt - 1 + ct - i)

    def specs(tile):
        return [pl.BlockSpec((b, cpt, 2 * RW_CHUNK, MIX_W), lambda i: (0, tile(i), 0, 0)),
                pl.BlockSpec((b, cpt, 2 * RW_CHUNK, MIX_W), lambda i: (0, tile(i), 0, 0)),
                pl.BlockSpec((b, cpt, RW_DIM, 2 * RW_CHUNK * RW_HEADS), lambda i: (0, tile(i), 0, 0)),
                pl.BlockSpec((b, cpt, MIX_W), lambda i: (0, tile(i), 0))]

    ident = lambda i: i
    yt = jax.ShapeDtypeStruct((b, n_tot // RW_CHUNK, RW_DIM, 2 * RW_CHUNK * RW_HEADS), F32)
    return pl.pallas_call(
        functools.partial(_rw_scan_kernel, n_batch=b),
        grid=(nt,),
        in_specs=specs(ident) + specs(rev_tile),
        out_specs=[specs(ident)[2], specs(rev_tile)[2]],
        out_shape=[yt, yt],
        scratch_shapes=[pltpu.VMEM((2 * b, RW_DIM, MIX_W), F32)],
        compiler_params=_cparams("arbitrary"),
        name="rwkv_scan",
    )(*fwd, *rev)


def _rw_fin_kernel(ytf_ref, ytr_ref, y0f_ref, y0r_ref, bon_ref, gate_ref, asel_ref, g64_ref, lng_ref, lnb_ref, o_ref):
    cpt = RW_TILE // RW_CHUNK
    asel = asel_ref[...]
    width = cpt * 2 * RW_CHUNK * RW_HEADS
    lane = lax.broadcasted_iota(jnp.int32, (RW_DIM, width), 1)
    lane_head = jnp.bitwise_and(jnp.right_shift(lane, 5), RW_HEADS - 1)

    def base(yt_ref, sub):
        yt = jnp.concatenate([yt_ref[0, sub * cpt + c] for c in range(cpt)], axis=1).astype(BF16)
        rows = jnp.concatenate([jnp.where(lane_head == h, yt, jnp.zeros_like(yt)) for h in range(RW_HEADS)], axis=0)
        return lax.dot_general(asel, rows, _NT, preferred_element_type=F32)

    g64 = g64_ref[...]
    bases = [(base(ytf_ref, sub), base(ytr_ref, sub)) for sub in range(RW_PREP_TILES)]
    for sub, (yb_f, yb_r) in enumerate(bases):
        tok = slice(sub * RW_TILE, (sub + 1) * RW_TILE)
        y = yb_f + y0f_ref[0, tok, :] + yb_r + y0r_ref[0, tok, :]
        mean = _split_dot(y, g64) * (1.0 / RW_DIM)
        c = y - mean
        var = _split_dot(c * c, g64) * (1.0 / RW_DIM)
        out = c * lax.rsqrt(var + RW_LN_EPS) * lng_ref[...] + lnb_ref[...] + bon_ref[tok, :]
        o_ref[tok, :] = (out * gate_ref[tok, :]).astype(BF16)


def _rw_fin_call(lay, ytf, ytr, y0f, y0r, bon, gate, consts):
    b, n_tot, tt = lay.b, lay.n_tot, RW_TILE * RW_PREP_TILES
    nt = n_tot // tt
    cpt = tt // RW_CHUNK
    ytb = pl.BlockSpec((1, cpt, RW_DIM, 2 * RW_CHUNK * RW_HEADS), lambda i, j: (i, j, 0, 0))
    y0b = pl.BlockSpec((1, tt, MIX_W), lambda i, j: (i, j, 0))
    row = pl.BlockSpec((tt, MIX_W), lambda i, j: (i * nt + j, 0))
    return pl.pallas_call(
        _rw_fin_kernel,
        grid=(b, nt),
        in_specs=[ytb, ytb, y0b, y0b, row, row] + [_full(a) for a in consts],
        out_specs=row,
        out_shape=jax.ShapeDtypeStruct((b * n_tot, MIX_W), BF16),
        compiler_params=_cparams("parallel", "parallel"),
        name="rwkv_finish",
    )(ytf, ytr, y0f, y0r, bon, gate, *consts)


def _rw_constants():
    c, nc = RW_CHUNK, RW_TILE // RW_CHUNK
    perm = np.zeros((RW_TILE, RW_TILE), np.float32)
    for ci in range(nc):
        for j in range(c):
            perm[j * nc + ci, ci * c + j] = 1.0
    lane = np.arange(MIX_W) % RW_DIM
    eye4 = (lane[None, :] == np.arange(RW_DIM)[:, None]).astype(np.float32)
    lanes = np.arange(nc * 2 * c * RW_HEADS)
    lane_chunk, lane_tok = lanes // (2 * c * RW_HEADS), lanes % (2 * c)
    t = np.arange(RW_TILE)
    asel = ((lane_chunk[None, :] == (t // c)[:, None]) & (lane_tok[None, :] == (c + t % c)[:, None])).astype(np.float32)
    as16 = lambda a: jnp.asarray(a, BF16)
    twice = lambda a: np.concatenate([a, a], axis=1)
    return as16(twice(perm)), as16(perm.T), as16(twice(eye4)), as16(asel)


def _rwkv_branch(lay, rw, pre_consts, g64, lng, lnb):
    r_, v_, kk_, lw_, kka_, km_, bon, gate = _rw_pre_call(lay, rw, pre_consts)
    perm, permt, eye4, asel = _rw_constants()
    prep_consts = (perm, permt, g64, eye4)
    fwd = _rw_prep_call(lay, (r_, kk_, v_), (lw_, kka_, km_), prep_consts, False)
    rev = _rw_prep_call(lay, (r_, kk_, v_), (lw_, kka_, km_), prep_consts, True)
    pick = lambda o: (o[0], o[1], o[2], o[4])
    ytf, ytr = _rw_scan_call(lay, pick(fwd), pick(rev))
    return _rw_fin_call(lay, ytf, ytr, fwd[3], rev[3], bon, gate, (asel, g64, lng, lnb))


def _merge_kernel(x_ref, modb_ref, modc_ref, g_ref, wg_ref, ya_ref, yb_ref, yc_ref, yd_ref, wb_ref, wo_ref, o_ref,
                  *, ctx_rows, tiles_per_seq):
    x = x_ref[...]
    mrow = lambda r: _mod_row(modb_ref, modc_ref, r, x.shape[0], ctx_rows, tiles_per_seq)
    h = _modulate(x, g_ref[...], mrow(0), mrow(1)).astype(BF16)
    merged = None
    for i, y_ref in enumerate((ya_ref, yb_ref, yc_ref, yd_ref)):
        gate = _sigmoid(jnp.dot(h, wg_ref[:, i * D_MODEL:(i + 1) * D_MODEL], preferred_element_type=F32))
        term = gate * jnp.dot(y_ref[...], wb_ref[i], preferred_element_type=F32)
        merged = term if merged is None else merged + term
    out = jnp.dot(merged.astype(BF16), wo_ref[...], preferred_element_type=F32)
    o_ref[...] = x + mrow(2) * out


def _merge_call(lay, with_ctx, x_all, mod, g, w_gate, ya, yb, yc, yd, w_branch, w_out):
    if with_ctx:
        tm, tps = _seq_tile(lay)
        src, n_tiles, ctx_rows = (lambda i: i), lay.b * tps, lay.n_ctx
    else:
        tm, tps = lay.t, lay.lat_tiles
        src, n_tiles, ctx_rows = lay.src_tile(False), lay.n_tiles(False), 0
    full_row = lambda w: pl.BlockSpec((tm, w), lambda i: (src(i), 0))
    out_row = lambda w: pl.BlockSpec((tm, w), lambda i: (i, 0))
    return pl.pallas_call(
        functools.partial(_merge_kernel, ctx_rows=ctx_rows, tiles_per_seq=tps),
        grid=(n_tiles,),
        in_specs=[full_row(D_MODEL), pl.BlockSpec((1, 6, D_MODEL), lambda i: (i // tps, 0, 0)),
                  pl.BlockSpec((1, 6, D_MODEL), lambda i: (lay.b, 0, 0)), _full(g), _full(w_gate),
                  out_row(MIX_W), full_row(MIX_W), out_row(MIX_W), full_row(MIX_W), _full(w_branch), _full(w_out)],
        out_specs=out_row(D_MODEL),
        out_shape=jax.ShapeDtypeStruct((lay.rows(with_ctx), D_MODEL), F32),
        compiler_params=_cparams("parallel"),
        name="merge_out",
    )(x_all, mod, mod, g, w_gate, ya, yb, yc, yd, w_branch, w_out)


def _router_kernel(x_ref, modb_ref, modc_ref, g_ref, wh_ref, wl_ref, bias_ref, f_ref, comb_ref, gid_ref,
                   *, ctx_rows, tiles_per_seq):
    x = x_ref[...]
    mrow = lambda r: _mod_row(modb_ref, modc_ref, r, x.shape[0], ctx_rows, tiles_per_seq)
    f = _modulate(x, g_ref[...], mrow(3), mrow(4))
    fh = f.astype(BF16)
    f_ref[...] = fh
    fl = (f - fh.astype(F32)).astype(BF16)
    nt = (((1,), (1,)), ((), ()))
    wh, wl = wh_ref[...], wl_ref[...]
    logits = (lax.dot_general(wh, fh, nt, preferred_element_type=F32)
              + lax.dot_general(wh, fl, nt, preferred_element_type=F32)
              + lax.dot_general(wl, fh, nt, preferred_element_type=F32))
    scores = _sigmoid(logits)
    biased = scores + bias_ref[...]
    sc = [scores[e:e + 1, :] for e in range(N_EXPERTS)]
    bi = [biased[e:e + 1, :] for e in range(N_EXPERTS)]
    group_score = []
    for g in range(N_GROUPS):
        a, b, c, d = bi[4 * g:4 * g + 4]
        m1, n1, m2, n2 = jnp.maximum(a, b), jnp.minimum(a, b), jnp.maximum(c, d), jnp.minimum(c, d)
        group_score.append(jnp.maximum(m1, m2) + jnp.maximum(jnp.minimum(m1, m2), jnp.maximum(n1, n2)))

    def first_argmax(vals):
        top = functools.reduce(jnp.maximum, vals)
        seen, hot = None, []
        for v in vals:
            h = v == top
            if seen is not None:
                h = jnp.logical_and(h, jnp.logical_not(seen))
            seen = h if seen is None else jnp.logical_or(seen, h)
            hot.append(h)
        return hot

    in_group = first_argmax(group_score)
    masked = [jnp.where(in_group[e // EXPERTS_PER_GROUP], bi[e], -jnp.inf) for e in range(N_EXPERTS)]
    hot1 = first_argmax(masked)
    hot2 = first_argmax([jnp.where(h, -jnp.inf, v) for h, v in zip(hot1, masked)])
    w1 = functools.reduce(jnp.add, [jnp.where(h, s, 0.0) for h, s in zip(hot1, sc)])
    w2 = functools.reduce(jnp.add, [jnp.where(h, s, 0.0) for h, s in zip(hot2, sc)])
    inv_tot = 1.0 / (w1 + w2)
    for e in range(N_EXPERTS):
        comb_ref[e:e + 1, :] = (jnp.where(hot1[e], w1, 0.0) + jnp.where(hot2[e], w2, 0.0)) * inv_tot
    gid_ref[...] = functools.reduce(jnp.add, [jnp.where(in_group[g], g, 0) for g in range(1, N_GROUPS)])


def _router_call(lay, with_ctx, x, mod, g, wh, wl, bias):
    t = x.shape[0]
    if with_ctx:
        (tm, tps), ctx_rows = _seq_tile(lay), lay.n_ctx
    else:
        pair = 2 if lay.lat_tiles % 2 == 0 else 1
        tm, tps, ctx_rows = pair * lay.t, lay.lat_tiles // pair, 0
    return pl.pallas_call(
        functools.partial(_router_kernel, ctx_rows=ctx_rows, tiles_per_seq=tps),
        grid=(t // tm,),
        in_specs=[pl.BlockSpec((tm, D_MODEL), lambda i: (i, 0)),
                  pl.BlockSpec((1, 6, D_MODEL), lambda i: (i // tps, 0, 0)),
                  pl.BlockSpec((1, 6, D_MODEL), lambda i: (lay.b, 0, 0)), _full(g), _full(wh), _full(wl), _full(bias)],
        out_specs=[pl.BlockSpec((tm, D_MODEL), lambda i: (i, 0)), pl.BlockSpec((N_EXPERTS, tm), lambda i: (0, i)),
                   pl.BlockSpec((1, tm), lambda i: (0, i))],
        out_shape=[jax.ShapeDtypeStruct((t, D_MODEL), BF16), jax.ShapeDtypeStruct((N_EXPERTS, t), F32),
                   jax.ShapeDtypeStruct((1, t), jnp.int32)],
        compiler_params=_cparams("parallel"),
        name="moe_router",
    )(x, mod, mod, g, wh, wl, bias)


def _moe_plan(gid, n_tiles, tm):
    g = gid.reshape(n_tiles, tm)
    onehot = (g[..., None] == jnp.arange(N_GROUPS, dtype=jnp.int32)).astype(jnp.int32)
    rank = jnp.cumsum(onehot, axis=1) - onehot
    counts = jnp.sum(onehot, axis=1)
    padded = (counts + 15) // 16 * 16
    offs = jnp.cumsum(padded, axis=1) - padded
    pos = jnp.sum(onehot * (offs[:, None, :] + rank), axis=-1)
    n_over = (jnp.maximum(padded - MOE_BLOCK, 0) + MOE_OVER - 1) // MOE_OVER
    return pos.astype(jnp.int32), offs.astype(jnp.int32), n_over.astype(jnp.int32)


def _moe_kernel(offs_ref, nover_ref, f_ref, posr_ref, posc_ref, comb_ref, wg_ref, wu_ref, wd_ref, x_ref, modb_ref,
                modc_ref, o_ref, xs_scr, cs_scr, ys_scr, *, ctx_rows, tiles_per_seq):
    i, e = pl.program_id(0), pl.program_id(1)
    n_slots, tm = xs_scr.shape[0], f_ref.shape[0]
    n_live = min(n_slots, -(-(tm + 16 * N_GROUPS) // 256) * 256)

    @pl.when(e == 0)
    def _():
        slot = lax.broadcasted_iota(jnp.int32, (n_live, tm), 0)
        place = (slot == posr_ref[0]).astype(BF16)
        xs_scr[0:n_live, :] = jnp.dot(place, f_ref[...], preferred_element_type=F32).astype(BF16)
        xs_scr[n_live:n_slots, :] = jnp.zeros((n_slots - n_live, D_MODEL), BF16)
        cs_scr[0:n_live, :] = _split_dot_rhs(place, comb_ref[...])
        cs_scr[n_live:n_slots, :] = jnp.zeros((n_slots - n_live, N_EXPERTS), F32)
        ys_scr[...] = jnp.zeros_like(ys_scr)

    grp = lax.shift_right_logical(e, 2)
    start = offs_ref[i, grp]
    lane = lax.broadcasted_iota(jnp.int32, (1, N_EXPERTS), 1)

    def run(rows):
        xb = xs_scr[rows, :]
        gate = jnp.dot(xb, wg_ref[0], preferred_element_type=F32)
        up = jnp.dot(xb, wu_ref[0], preferred_element_type=F32)
        act = (gate * _sigmoid(gate) * up).astype(BF16)
        down = jnp.dot(act, wd_ref[0], preferred_element_type=F32)
        c_e = jnp.sum(jnp.where(lane == e, cs_scr[rows, :], 0.0), axis=1, keepdims=True)
        ys_scr[rows, :] += c_e * down

    run(pl.ds(pl.multiple_of(start, 16), MOE_BLOCK))

    def overflow(k, carry):
        run(pl.ds(pl.multiple_of(start + MOE_BLOCK + k * MOE_OVER, 16), MOE_OVER))
        return carry

    lax.fori_loop(0, nover_ref[i, grp], overflow, 0)

    @pl.when(e == N_EXPERTS - 1)
    def _():
        slot = lax.broadcasted_iota(jnp.int32, (tm, n_live), 1)
        fetch = (slot == posc_ref[...]).astype(BF16)
        y = jnp.dot(fetch, ys_scr[0:n_live, :].astype(BF16), preferred_element_type=F32)
        res_gate = modb_ref[0, 5:6, :]
        if ctx_rows:
            row = lax.broadcasted_iota(jnp.int32, y.shape, 0)
            first = i % tiles_per_seq == 0
            res_gate = jnp.where(jnp.logical_and(first, row < ctx_rows), modc_ref[0, 5:6, :], res_gate)
        o_ref[...] = x_ref[...] + res_gate * y


def _moe_call(lay, with_ctx, f, comb, gid, wg, wu, wd, x, mod):
    t = f.shape[0]
    seq = lay.n_tot if with_ctx else lay.n_lat
    tm = MOE_TILE if seq % MOE_TILE == 0 else math.gcd(seq, 1024)
    tps, n_tiles = seq // tm, t // tm
    ctx_rows = lay.n_ctx if with_ctx else 0
    assert ctx_rows <= tm
    n_slots = -(-(tm + 16 * N_GROUPS + MOE_BLOCK + MOE_OVER) // 256) * 256
    pos, offs, n_over = _moe_plan(gid, n_tiles, tm)
    wspec = lambda a: pl.BlockSpec((1,) + a.shape[1:], lambda i, e, *_: (e, 0, 0))
    tok = lambda w: pl.BlockSpec((tm, w), lambda i, e, *_: (i, 0))
    grid_spec = pltpu.PrefetchScalarGridSpec(
        num_scalar_prefetch=2,
        grid=(n_tiles, N_EXPERTS),
        in_specs=[tok(D_MODEL), pl.BlockSpec((1, 1, tm), lambda i, e, *_: (i, 0, 0)), tok(1), tok(N_EXPERTS),
                  wspec(wg), wspec(wu), wspec(wd), tok(D_MODEL),
                  pl.BlockSpec((1, 6, D_MODEL), lambda i, e, *_: (i // tps, 0, 0)),
                  pl.BlockSpec((1, 6, D_MODEL), lambda i, e, *_: (lay.b, 0, 0))],
        out_specs=tok(D_MODEL),
        scratch_shapes=[pltpu.VMEM((n_slots, D_MODEL), BF16), pltpu.VMEM((n_slots, N_EXPERTS), F32),
                        pltpu.VMEM((n_slots, D_MODEL), F32)])
    return pl.pallas_call(
        functools.partial(_moe_kernel, ctx_rows=ctx_rows, tiles_per_seq=tps),
        grid_spec=grid_spec,
        out_shape=jax.ShapeDtypeStruct((t, D_MODEL), F32),
        compiler_params=_cparams("parallel", "arbitrary"),
        name="moe_experts",
    )(offs, n_over, f, pos.reshape(n_tiles, 1, tm), pos.reshape(t, 1), comb, wg, wu, wd, x, mod, mod)


def _block_ones(n, group):
    i = np.arange(n) // group
    return jnp.asarray(i[:, None] == i[None, :], dtype=BF16)


def _rope_tables(n_ctx, n_lat):
    rows = n_lat // GRID_W
    row = jnp.repeat(jnp.arange(rows, dtype=F32), GRID_W)
    col = jnp.tile(jnp.arange(GRID_W, dtype=F32), rows)

    def angles(rot_dim):
        n_freq = rot_dim // 4
        inv_freq = ROPE_BASE ** (-jnp.arange(n_freq, dtype=F32) / n_freq)
        ang = jnp.concatenate([row[:, None] * inv_freq, col[:, None] * inv_freq], axis=-1)
        return jnp.cos(ang), jnp.sin(ang)

    c, s = angles(DA_DIM)
    cda = jnp.tile(jnp.concatenate([c, c], -1), (1, 2 * DA_HEADS))
    sda = jnp.tile(jnp.concatenate([-s, s], -1), (1, 2 * DA_HEADS))
    c, s = angles(MLA_ROPE)
    one = jnp.ones((n_lat, MLA_NOPE), F32)
    pad = MLA_HEAD_PAD - MLA_NOPE - MLA_ROPE
    cml = jnp.tile(jnp.concatenate([one, c, c, jnp.ones((n_lat, pad), F32)], -1), (1, MLA_HEADS))
    sml = jnp.tile(jnp.concatenate([0 * one, -s, s, jnp.zeros((n_lat, pad), F32)], -1), (1, MLA_HEADS))
    ident = lambda t, v: jnp.concatenate([jnp.full((n_ctx, MIX_W), v, F32), t], axis=0)
    return ident(cda, 1.0), ident(sda, 0.0), ident(cml, 1.0), ident(sml, 0.0)


def _pad_heads(w, n_heads, src_w, lo, hi, dst_w=MLA_HEAD_PAD):
    w = w.reshape(w.shape[0], n_heads, src_w)[:, :, lo:hi]
    w = jnp.pad(w, ((0, 0), (0, 0), (0, dst_w - (hi - lo))))
    return w.reshape(w.shape[0], n_heads * dst_w)


def _mix_weight(w_in_l):
    w = w_in_l
    kr = w[:, 1344:1360]
    z = lambda n: jnp.zeros((D_MODEL, n), w.dtype)
    kr_wide = jnp.concatenate([jnp.concatenate([z(MLA_NOPE), kr, z(MLA_HEAD_PAD - MLA_NOPE - MLA_ROPE)], 1)] * MLA_HEADS, 1)
    return jnp.concatenate([w[:, 0:1024], w[:, 1024:1216], z(64), w[:, 1216:1344], kr_wide, w[:, 1360:2384]], axis=1).astype(BF16)


def kernel(x, c, ctx, c_ctx, w_ada, b_ada, norm_mix_g, norm_ffn_g, w_in, da_qk_norm_g, da_lambda, da_subln_g, s5_lam_re, s5_lam_im, s5_log_dt, s5_b_re, s5_b_im, s5_c_re, s5_c_im, s5_d, s5_w_glu, s5_b_glu, mla_cq_norm_g, mla_ckv_norm_g, mla_w_uq, mla_w_ukv, mla_qk_norm_g, rw_mu, rw_w0, rw_w1, rw_w2, rw_a0, rw_a1, rw_a2, rw_g1, rw_g2, rw_k_k, rw_k_a, rw_r_k, rw_ln_g, rw_ln_b, w_branch, w_out, router_w, router_bias, exp_w_gate, exp_w_up, exp_w_down):
    b, n_lat, dm = x.shape
    n_ctx = ctx.shape[1]
    depth = w_ada.shape[0]
    assert dm == D_MODEL
    lay = _Layout(b, n_ctx, n_lat)
    t_all = b * lay.n_tot
    tm_big = _seq_tile(lay)[0]

    g32 = _block_ones(MIX_W, DA_DIM)
    g64 = _block_ones(MIX_W, RW_DIM)
    tabs = _rope_tables(n_ctx, n_lat)
    row = lambda v: v.reshape(1, -1).astype(F32)
    bf = lambda a: a.astype(BF16)

    cc = jnp.zeros((16, dm), F32).at[:b].set(c).at[b].set(c_ctx)
    mod_all = _ada_call(cc, w_ada, b_ada)
    x_all = jnp.concatenate([ctx, x], axis=1).reshape(t_all, dm)

    s5_mats = jax.vmap(_s5_mats)(s5_lam_re, s5_lam_im, s5_log_dt, s5_b_re, s5_b_im, s5_c_re, s5_c_im)

    wr_hi = router_w.T.astype(BF16)
    wr_lo = (router_w.T - wr_hi.astype(F32)).astype(BF16)
    r_bias = router_bias.reshape(N_EXPERTS, 1).astype(F32)

    for l in range(depth):
        need_ctx = l < depth - 1
        lambda_init = 0.8 - 0.6 * math.exp(-0.3 * l)
        mod = mod_all[l, :b + 1].reshape(b + 1, 6, dm)
        g_mix = row(norm_mix_g[l])
        da, s5a, s5b, mla, rw = _inproj_call(lay, x_all, mod, g_mix, _mix_weight(w_in[l]))

        log2e = math.log2(math.e)
        gda = jnp.stack([jnp.tile(da_qk_norm_g[l, 0], 2 * DA_HEADS) * (DA_DIM ** -0.5 * log2e), jnp.tile(da_qk_norm_g[l, 1], 2 * DA_HEADS)])
        mla_pad = MLA_HEAD_PAD - MLA_NOPE - MLA_ROPE
        gml = jnp.stack([jnp.tile(jnp.pad(mla_qk_norm_g[l, 0], (0, mla_pad)), MLA_HEADS) * ((MLA_NOPE + MLA_ROPE) ** -0.5 * log2e),
                         jnp.tile(jnp.pad(mla_qk_norm_g[l, 1], (0, mla_pad)), MLA_HEADS)])
        wuq = bf(jnp.pad(_pad_heads(mla_w_uq[l], MLA_HEADS, MLA_NOPE + MLA_ROPE, 0, MLA_NOPE + MLA_ROPE), ((0, 64), (0, 0))))
        wuk = bf(_pad_heads(mla_w_ukv[l], MLA_HEADS, MLA_NOPE + MLA_VDIM, 0, MLA_NOPE))
        wuv = bf(_pad_heads(mla_w_ukv[l], MLA_HEADS, MLA_NOPE + MLA_VDIM, MLA_NOPE, MLA_NOPE + MLA_VDIM))
        consts = (g32, g64, gda.astype(F32), gml.astype(F32), row(jnp.pad(mla_cq_norm_g[l], (0, 64))), row(mla_ckv_norm_g[l]),
                  wuq, wuk, wuv)
        qd, kdt, vd, qm, kmt, vm = _qkprep_call(lay, da, mla, tabs, consts)

        lam32 = da_lambda[l].astype(F32)
        lmbda = (jnp.exp(jnp.sum(lam32[0] * lam32[1])) - jnp.exp(jnp.sum(lam32[2] * lam32[3])) + lambda_init).reshape(1, 1)
        subln = row(jnp.tile(da_subln_g[l], DA_HEADS) * (1.0 - lambda_init))
        ya = _attention(lay, qd, kdt, vd, (lmbda, subln, g64), True, need_ctx, "diff_attn")
        yc = _attention(lay, qm, kmt, vm, (lmbda, subln, g64), False, need_ctx, "mla_attn")

        ys_a, ys_b = _s5_scan(lay, s5a, s5b, s5_mats, l)
        yb = _s5_glu_call(s5a, s5b, ys_a, ys_b, row(s5_d[l]), bf(s5_w_glu[l]), row(s5_b_glu[l]), tm_big)

        pre_consts = (row(rw_mu[l]), g64, row(rw_k_k[l]), row(rw_k_a[l]), row(rw_r_k[l]),
                      rw_w0[l].reshape(2, 1, MIX_W), bf(rw_w1[l]), bf(rw_w2[l]),
                      rw_a0[l].reshape(2, 1, MIX_W), bf(rw_a1[l]), bf(rw_a2[l]), bf(rw_g1[l]), bf(rw_g2[l]))
        yd = _rwkv_branch(lay, rw, pre_consts, g64, row(rw_ln_g[l]), row(rw_ln_b[l]))

        x_mid = _merge_call(lay, need_ctx, x_all, mod, g_mix, bf(w_in[l][:, 2384:]), ya, yb, yc, yd,
                            bf(w_branch[l]), bf(w_out[l]))
        f, comb_t, gid = _router_call(lay, need_ctx, x_mid, mod, row(norm_ffn_g[l]), wr_hi, wr_lo, r_bias)
        x_all = _moe_call(lay, need_ctx, f, comb_t.T, gid, bf(exp_w_gate[l]), bf(exp_w_up[l]), bf(exp_w_down[l]), x_mid, mod)
    return x_all.reshape(b, n_lat, dm)
```

```python
import functools
import math

import numpy as np
import jax
import jax.numpy as jnp
from jax import lax
from jax.experimental import pallas as pl
from jax.experimental.pallas import tpu as pltpu

F32 = jnp.float32
BF16 = jnp.bfloat16

D_MODEL = 1024
GRID_W = 64
ROPE_BASE = 10000.0
EPS = 1e-6
DA_HEADS, DA_DIM, DA_VDIM = 4, 32, 64
S5_GROUPS, S5_CH, S5_STATE = 16, 16, 64
MLA_HEADS, MLA_NOPE, MLA_ROPE, MLA_VDIM = 4, 32, 16, 64
MLA_Q_RANK = 192
MLA_HEAD_PAD = 64
RW_HEADS, RW_DIM = 4, 64
RW_LN_EPS = 64e-5
N_EXPERTS, N_GROUPS, EXPERTS_PER_GROUP = 16, 4, 4
D_FF = 512
MIX_W = 256

S5_CHUNK = 8
S5_FLAT = S5_CHUNK * MIX_W
S5_STATE_W = S5_GROUPS * S5_STATE
RW_CHUNK = 16
RW_TILE = 128
RW_PREP_TILES = 2
_NT = (((1,), (1,)), ((), ()))
TOKEN_TILE = 256
PROJ_TILE = 768
MOE_TILE = 1152
MOE_BLOCK = 384
MOE_OVER = 128

_DA_W, _S5_W, _MLA_W, _RW_W = 768, 256, 640, 1024
_MIX_COLS = _DA_W + _S5_W + _MLA_W + _RW_W

V7X_VMEM_BYTES = 64 * 2**20
_VMEM_LIMIT = V7X_VMEM_BYTES - 8 * 2**20


def _cparams(*sem):
    return pltpu.CompilerParams(dimension_semantics=sem, vmem_limit_bytes=_VMEM_LIMIT)


def _full(a):
    return pl.BlockSpec(a.shape, lambda *_, nd=a.ndim: (0,) * nd)


def _split_dot(x, w, terms=2):
    acc = None
    rem = x
    for i in range(terms):
        part = rem.astype(BF16)
        d = jnp.dot(part, w, preferred_element_type=F32)
        acc = d if acc is None else acc + d
        if i + 1 < terms:
            rem = rem - part.astype(F32)
    return acc


def _split_dot_rhs(w, x):
    hi = x.astype(BF16)
    lo = (x - hi.astype(F32)).astype(BF16)
    return jnp.dot(w, hi, preferred_element_type=F32) + jnp.dot(w, lo, preferred_element_type=F32)


def _modulate(x, g, shift, scale):
    xn = x * lax.rsqrt(jnp.mean(x * x, axis=-1, keepdims=True) + EPS)
    return xn * g * (1.0 + scale) + shift


def _sigmoid(x):
    return 1.0 / (1.0 + jnp.exp(-x))


def _group_rms(x, ones_bd, inv_n, gain):
    ms = _split_dot(x * x, ones_bd) * inv_n
    return x * lax.rsqrt(ms + EPS) * gain


def _lane_partner(x, half, period, first_end):
    n = x.shape[1]
    lane = lax.broadcasted_iota(jnp.int32, x.shape, 1)
    up = pltpu.roll(x, n - half, axis=1)
    down = pltpu.roll(x, half, axis=1)
    return jnp.where((lane & (period - 1)) < first_end, up, down)


def _rope(x, cos_t, sin_t, half, period, first_end):
    return x * cos_t + _lane_partner(x, half, period, first_end) * sin_t


def _ada_kernel(c_ref, w_ref, b_ref, o_ref):
    c = c_ref[...]
    s = c * _sigmoid(c)
    o_ref[0] = jnp.dot(s.astype(BF16), w_ref[0].astype(BF16), preferred_element_type=F32) + b_ref[0]


def _ada_call(cc, w_ada, b_ada):
    depth, dm, n = w_ada.shape
    tn = n // 4
    return pl.pallas_call(
        _ada_kernel,
        grid=(depth, n // tn),
        in_specs=[
            pl.BlockSpec(cc.shape, lambda l, j: (0, 0)),
            pl.BlockSpec((1, dm, tn), lambda l, j: (l, 0, j)),
            pl.BlockSpec((1, 1, tn), lambda l, j: (l, 0, j)),
        ],
        out_specs=pl.BlockSpec((1, cc.shape[0], tn), lambda l, j: (l, 0, j)),
        out_shape=jax.ShapeDtypeStruct((depth, cc.shape[0], n), F32),
        compiler_params=_cparams("parallel", "parallel"),
        name="ada_mod",
    )(cc, w_ada, b_ada.reshape(depth, 1, n))


class _Layout:
    def __init__(self, n_batch, n_ctx, n_lat):
        t = TOKEN_TILE
        assert n_ctx % t == 0 and n_lat % t == 0
        self.b, self.n_ctx, self.n_lat, self.n_tot = n_batch, n_ctx, n_lat, n_ctx + n_lat
        self.t = t
        self.ctx_tiles, self.lat_tiles, self.seq_tiles = n_ctx // t, n_lat // t, (n_ctx + n_lat) // t

    def rows(self, with_ctx):
        return self.b * (self.n_tot if with_ctx else self.n_lat)

    def n_tiles(self, with_ctx):
        return self.b * (self.seq_tiles if with_ctx else self.lat_tiles)

    def src_tile(self, with_ctx):
        if with_ctx:
            return lambda i: i
        return lambda i: (i // self.lat_tiles) * self.seq_tiles + i % self.lat_tiles + self.ctx_tiles


def _mod_row(modb_ref, modc_ref, r, n_rows, ctx_rows, tiles_per_seq):
    per_batch = modb_ref[0, r:r + 1, :]
    if not ctx_rows:
        return per_batch
    row = lax.broadcasted_iota(jnp.int32, (n_rows, 1), 0)
    first = pl.program_id(0) % tiles_per_seq == 0
    return jnp.where(jnp.logical_and(first, row < ctx_rows), modc_ref[0, r:r + 1, :], per_batch)


def _inproj_kernel(x_ref, modb_ref, modc_ref, g_ref, w_ref, da_ref, s5a_ref, s5b_ref, mla_ref, rw_ref,
                   *, ctx_rows, tiles_per_seq):
    x = x_ref[...]
    mrow = lambda r: _mod_row(modb_ref, modc_ref, r, x.shape[0], ctx_rows, tiles_per_seq)
    h = _modulate(x, g_ref[...], mrow(0), mrow(1))
    acc = jnp.dot(h.astype(BF16), w_ref[...], preferred_element_type=F32)
    da_ref[...] = acc[:, 0:_DA_W]
    s5a_ref[...] = acc[:, _DA_W:_DA_W + _S5_W // 2]
    s5b_ref[...] = acc[:, _DA_W + _S5_W // 2:_DA_W + _S5_W]
    mla_ref[...] = acc[:, _DA_W + _S5_W:_DA_W + _S5_W + _MLA_W]
    rw_ref[...] = acc[:, _DA_W + _S5_W + _MLA_W:_MIX_COLS]


def _seq_tile(lay):
    tm = PROJ_TILE if lay.n_tot % PROJ_TILE == 0 and lay.n_ctx <= PROJ_TILE else lay.t
    return tm, lay.n_tot // tm


def _inproj_call(lay, x_all, mod, g, w_mix):
    t = x_all.shape[0]
    tm, tps = _seq_tile(lay)
    widths = (_DA_W, _S5_W // 2, _S5_W // 2, _MLA_W, _RW_W)
    return pl.pallas_call(
        functools.partial(_inproj_kernel, ctx_rows=lay.n_ctx, tiles_per_seq=tps),
        grid=(t // tm,),
        in_specs=[
            pl.BlockSpec((tm, D_MODEL), lambda i: (i, 0)),
            pl.BlockSpec((1, 6, D_MODEL), lambda i: (i // tps, 0, 0)),
            pl.BlockSpec((1, 6, D_MODEL), lambda i: (lay.b, 0, 0)),
            _full(g), _full(w_mix),
        ],
        out_specs=[pl.BlockSpec((tm, w), lambda i: (i, 0)) for w in widths],
        out_shape=[jax.ShapeDtypeStruct((t, w), F32) for w in widths],
        compiler_params=_cparams("parallel"),
        name="in_proj",
    )(x_all, mod, mod, g, w_mix)


def _qkprep_kernel(da_ref, mla_ref, cda_ref, sda_ref, cml_ref, sml_ref, g32_ref, g64_ref,
                   gda_ref, gml_ref, cqg_ref, ckvg_ref, wuq_ref, wuk_ref, wuv_ref,
                   qd_ref, kd_ref, vd_ref, qm_ref, km_ref, vm_ref):
    g32 = g32_ref[...]
    g64 = g64_ref[...]
    cda, sda = cda_ref[...], sda_ref[...]
    q = _group_rms(da_ref[:, 0:MIX_W], g32, 1.0 / DA_DIM, gda_ref[0:1, :])
    qd_ref[...] = _rope(q, cda, sda, DA_DIM // 2, DA_DIM, DA_DIM // 2).astype(BF16)
    k = _group_rms(da_ref[:, MIX_W:2 * MIX_W], g32, 1.0 / DA_DIM, gda_ref[1:2, :])
    kd_ref[0] = _rope(k, cda, sda, DA_DIM // 2, DA_DIM, DA_DIM // 2).T.astype(BF16)
    vd_ref[...] = da_ref[:, 2 * MIX_W:3 * MIX_W].astype(BF16)

    cml, sml = cml_ref[...], sml_ref[...]
    cq = mla_ref[:, 0:256]
    cqn = cq * lax.rsqrt(jnp.sum(cq * cq, axis=-1, keepdims=True) * (1.0 / MLA_Q_RANK) + EPS) * cqg_ref[...]
    q = jnp.dot(cqn.astype(BF16), wuq_ref[...], preferred_element_type=F32)
    ckv = mla_ref[:, 256:384]
    ckvn = ckv * lax.rsqrt(jnp.mean(ckv * ckv, axis=-1, keepdims=True) + EPS) * ckvg_ref[...]
    ckvb = ckvn.astype(BF16)
    k = jnp.dot(ckvb, wuk_ref[...], preferred_element_type=F32) + mla_ref[:, 384:640]
    vm_ref[...] = jnp.dot(ckvb, wuv_ref[...], preferred_element_type=F32).astype(BF16)
    inv_n = 1.0 / (MLA_NOPE + MLA_ROPE)
    half = MLA_ROPE // 2
    q = _group_rms(q, g64, inv_n, gml_ref[0:1, :])
    qm_ref[...] = _rope(q, cml, sml, half, MLA_HEAD_PAD, MLA_NOPE + half).astype(BF16)
    k = _group_rms(k, g64, inv_n, gml_ref[1:2, :])
    km_ref[0] = _rope(k, cml, sml, half, MLA_HEAD_PAD, MLA_NOPE + half).T.astype(BF16)


def _qkprep_call(lay, da, mla, tabs, consts):
    t = da.shape[0]
    tm, st = _seq_tile(lay)
    row = pl.BlockSpec((tm, MIX_W), lambda i: (i, 0))
    key_t = pl.BlockSpec((1, MIX_W, tm), lambda i: (i // st, 0, i % st))
    in_specs = [pl.BlockSpec((tm, _DA_W), lambda i: (i, 0)), pl.BlockSpec((tm, _MLA_W), lambda i: (i, 0))]
    in_specs += [pl.BlockSpec((tm, MIX_W), lambda i: (i % st, 0)) for _ in tabs]
    in_specs += [_full(a) for a in consts]
    tok = jax.ShapeDtypeStruct((t, MIX_W), BF16)
    keys = jax.ShapeDtypeStruct((lay.b, MIX_W, lay.n_tot), BF16)
    return pl.pallas_call(
        _qkprep_kernel,
        grid=(t // tm,),
        in_specs=in_specs,
        out_specs=[row, key_t, row, row, key_t, row],
        out_shape=[tok, keys, tok, tok, keys, tok],
        compiler_params=_cparams("parallel"),
        name="qk_prep",
    )(da, mla, *tabs, *consts)


def _softmax_parts(s):
    p = jnp.exp2(s - jnp.max(s, axis=-1, keepdims=True))
    return p, 1.0 / jnp.sum(p, axis=-1, keepdims=True)


def _attn_heads(q, kt_ref, v_ref, nk, diff, lam):
    lane = lax.broadcasted_iota(jnp.int32, (q.shape[0], MIX_W), 1)
    v = v_ref[0, 0:nk, :]
    acc = jnp.zeros((q.shape[0], MIX_W), F32)
    dk = DA_DIM if diff else MLA_HEAD_PAD
    per_head = 2 if diff else 1

    def scores(h):
        return [jnp.dot(q[:, e * dk:(e + 1) * dk], kt_ref[0, e * dk:(e + 1) * dk, 0:nk], preferred_element_type=F32)
                for e in range(per_head * h, per_head * (h + 1))]

    ahead = scores(0)
    for h in range(DA_HEADS):
        s = ahead
        if h + 1 < DA_HEADS:
            ahead = scores(h + 1)
        if diff:
            p0, r0 = _softmax_parts(s[0])
            p1, r1 = _softmax_parts(s[1])
            o = jnp.dot((p0 * r0 - p1 * (r1 * lam)).astype(BF16), v, preferred_element_type=F32)
        else:
            p, r = _softmax_parts(s[0])
            o = jnp.dot(p.astype(BF16), v, preferred_element_type=F32) * r
        in_head = jnp.logical_and(lane >= h * DA_VDIM, lane < (h + 1) * DA_VDIM)
        acc = jnp.where(in_head, o, acc)
    return acc


def _attn_kernel(q_ref, kt_ref, v_ref, lam_ref, gain_ref, g64_ref, o_ref, *, diff, n_ctx, n_tot, ctx_tiles):
    q = q_ref[...]
    lam = lam_ref[...]

    def run(nk):
        o = _attn_heads(q, kt_ref, v_ref, nk, diff, lam)
        if diff:
            o = _group_rms(o, g64_ref[...], 1.0 / DA_VDIM, gain_ref[...])
        o_ref[...] = o.astype(BF16)

    if ctx_tiles:
        is_ctx = pl.program_id(1) < ctx_tiles
        pl.when(is_ctx)(lambda: run(n_ctx))
        pl.when(jnp.logical_not(is_ctx))(lambda: run(n_tot))
    else:
        run(n_tot)


def _attention(lay, q, kt, v, extra, diff, with_ctx, name):
    tq = lay.t
    tiles = lay.seq_tiles if with_ctx else lay.lat_tiles
    off = 0 if with_ctx else lay.ctx_tiles
    v3 = v.reshape(lay.b, lay.n_tot, MIX_W)
    kern = functools.partial(_attn_kernel, diff=diff, n_ctx=lay.n_ctx, n_tot=lay.n_tot,
                             ctx_tiles=lay.ctx_tiles if with_ctx else 0)
    return pl.pallas_call(
        kern,
        grid=(lay.b, tiles),
        in_specs=[
            pl.BlockSpec((tq, MIX_W), lambda b, j: (b * lay.seq_tiles + j + off, 0)),
            pl.BlockSpec((1, MIX_W, lay.n_tot), lambda b, j: (b, 0, 0)),
            pl.BlockSpec((1, lay.n_tot, MIX_W), lambda b, j: (b, 0, 0)),
        ] + [_full(a) for a in extra],
        out_specs=pl.BlockSpec((tq, MIX_W), lambda b, j: (b * tiles + j, 0)),
        out_shape=jax.ShapeDtypeStruct((lay.rows(with_ctx), MIX_W), BF16),
        compiler_params=_cparams("parallel", "parallel"),
        name=name,
    )(q, kt, v3, *extra)


def _chunk_rows(ua_ref, ub_ref):
    n = ua_ref.shape[0] // S5_CHUNK
    parts = []
    for s in range(S5_CHUNK):
        rows = pl.ds(s, n, stride=S5_CHUNK)
        parts += [ua_ref[rows, :], ub_ref[rows, :]]
    return jnp.concatenate(parts, axis=1).astype(BF16)


def _s5_proj_kernel(ua_ref, ub_ref, bre_ref, bim_ref, sre_ref, sim_ref):
    u = _chunk_rows(ua_ref, ub_ref)
    sre_ref[0] = jnp.dot(u, bre_ref[0, 0], preferred_element_type=F32)
    sim_ref[0] = jnp.dot(u, bim_ref[0, 0], preferred_element_type=F32)


def _s5_rec_kernel(sre_ref, sim_ref, are_ref, aim_ref, hre_ref, him_ref, *, n_batch, n_chunks, ctx_chunks):
    rev = pl.program_id(0) == 1
    ar, ai = are_ref[0, 0], aim_ref[0, 0]
    sre, sim, hre, him = sre_ref.at[0], sim_ref.at[0], hre_ref.at[0], him_ref.at[0]

    def step(i, carry):
        hr, hi = carry
        k_rev = jnp.where(i < ctx_chunks, ctx_chunks - 1 - i, n_chunks - 1 + ctx_chunks - i)
        k = jnp.where(rev, k_rev, i)
        rows = pl.ds(k, n_batch, stride=n_chunks)
        hre[rows, :] = hr
        him[rows, :] = hi
        return ar * hr - ai * hi + sre[rows, :], ar * hi + ai * hr + sim[rows, :]

    zero = jnp.zeros((n_batch, 128), F32)
    lax.fori_loop(0, n_chunks, step, (zero, zero), unroll=4)


def _s5_out_kernel(ua_ref, ub_ref, hre_ref, him_ref, m_ref, cre_ref, cim_ref, ya_ref, yb_ref):
    y = jnp.dot(_chunk_rows(ua_ref, ub_ref), m_ref[0, 0], preferred_element_type=F32)
    y = y + jnp.dot(hre_ref[0].astype(BF16), cre_ref[0, 0], preferred_element_type=F32)
    y = y + jnp.dot(him_ref[0].astype(BF16), cim_ref[0, 0], preferred_element_type=F32)
    n = y.shape[0]
    ya, yb = ya_ref.at[0], yb_ref.at[0]
    for s in range(S5_CHUNK):
        rows = pl.ds(s, n, stride=S5_CHUNK)
        ya[rows, :] = y[:, s * MIX_W:s * MIX_W + 128]
        yb[rows, :] = y[:, s * MIX_W + 128:(s + 1) * MIX_W]


def _s5_mats(lam_re, lam_im, log_dt, b_re, b_im, c_re, c_im):
    hp = lax.Precision.HIGHEST
    L, G, P, CH = S5_CHUNK, S5_GROUPS, S5_STATE, S5_CH
    lr, li = lam_re.astype(F32), lam_im.astype(F32)
    dt = jnp.exp(log_dt.astype(F32))[..., None]
    zr, zi = lr * dt, li * dt
    j = jnp.arange(L + 1, dtype=F32)[:, None, None, None]
    mag = jnp.exp(zr[None] * j)
    pw_re, pw_im = mag * jnp.cos(zi[None] * j), mag * jnp.sin(zi[None] * j)
    nr, ni = pw_re[1] - 1.0, pw_im[1]
    den = lr * lr + li * li
    cr, ci = (nr * lr + ni * li) / den, (ni * lr - nr * li) / den
    bre, bim = b_re.astype(F32), b_im.astype(F32)
    bb_re = cr[..., None] * bre - ci[..., None] * bim
    bb_im = cr[..., None] * bim + ci[..., None] * bre
    x_re = pw_re[..., None] * bb_re[None] - pw_im[..., None] * bb_im[None]
    x_im = pw_re[..., None] * bb_im[None] + pw_im[..., None] * bb_re[None]
    cre, cim = c_re.astype(F32), c_im.astype(F32)
    kern = (jnp.einsum('dgcp,jdgpe->dgjce', cre, x_re[:L], precision=hp)
            - jnp.einsum('dgcp,jdgpe->dgjce', cim, x_im[:L], precision=hp))
    def spread_mask(a, b):
        spread = jnp.asarray(np.tile(np.eye(b, dtype=np.float32), (1, G)))
        mask = jnp.asarray(np.kron(np.eye(G, dtype=np.float32), np.ones((a, b), np.float32)))
        return spread, mask

    kt = kern.transpose(0, 2, 1, 4, 3)
    xt_re, xt_im = x_re.transpose(1, 0, 2, 4, 3), x_im.transpose(1, 0, 2, 4, 3)
    pwt_re, pwt_im = pw_re.transpose(1, 0, 2, 3)[:, :, :, :, None], pw_im.transpose(1, 0, 2, 3)[:, :, :, :, None]
    cret, cimt = cre.transpose(0, 1, 3, 2)[:, None], cim.transpose(0, 1, 3, 2)[:, None]
    ca_re, ca_im = cret * pwt_re - cimt * pwt_im, -(cret * pwt_im + cimt * pwt_re)
    s_idx, t_idx = np.arange(L)[:, None], np.arange(L)[None, :]
    k_st, xb_re, xb_im, cq_re, cq_im = [], [], [], [], []
    for d in range(2):
        lag = (t_idx - s_idx) if d == 0 else (s_idx - t_idx)
        k_st.append(jnp.where(jnp.asarray(lag >= 0)[:, :, None, None, None], kt[d][np.clip(lag, 0, L - 1)], 0.0))
        pw = np.arange(L - 1, -1, -1) if d == 0 else np.arange(L)
        xb_re.append(xt_re[d][pw])
        xb_im.append(xt_im[d][pw])
        q = np.arange(1, L + 1) if d == 0 else np.arange(L, 0, -1)
        cq_re.append(ca_re[d][q])
        cq_im.append(ca_im[d][q])
    sp, mk = spread_mask(CH, CH)
    m = jnp.einsum('dstrb,bc->dsrtc', jnp.stack(k_st).reshape(2, L, L, G * CH, CH), sp, precision=hp) * mk[:, None, :]
    m = m.astype(BF16).reshape(2, L * G * CH, L * G * CH)
    sp, mk = spread_mask(CH, P)
    to_b = lambda x: (jnp.einsum('dsrb,bc->dsrc', jnp.stack(x).reshape(2, L, G * CH, P), sp, precision=hp) * mk
                      ).astype(BF16).reshape(2, L * G * CH, G * P)
    sp_c, mk_c = spread_mask(P, CH)
    to_c = lambda x: (jnp.einsum('dtrb,bc->drtc', jnp.stack(x).reshape(2, L, G * P, CH), sp_c, precision=hp)
                      * mk_c[:, None, :]).astype(BF16).reshape(2, G * P, L * G * CH)
    a_re, a_im = pw_re[L].reshape(2, 1, G * P), pw_im[L].reshape(2, 1, G * P)
    return m, to_b(xb_re), to_b(xb_im), to_c(cq_re), to_c(cq_im), a_re, a_im


def _s5_scan(lay, ua, ub, mats, layer):
    m, b_r, b_i, c_r, c_i, a_re, a_im = mats
    n_chunks = lay.n_tot // S5_CHUNK
    rows = lay.b * n_chunks
    tr = min(lay.t, rows)
    tok = tr * S5_CHUNK
    half = MIX_W // 2
    wspec = lambda a: pl.BlockSpec((1, 1) + a.shape[2:], lambda d, i: (layer, d, 0, 0))
    state = jax.ShapeDtypeStruct((2, rows, S5_STATE_W), F32)
    sblk = pl.BlockSpec((1, tr, S5_STATE_W), lambda d, i: (d, i, 0))
    ublk = pl.BlockSpec((tok, half), lambda d, i: (i, 0))
    s_re, s_im = pl.pallas_call(
        _s5_proj_kernel,
        grid=(2, rows // tr),
        in_specs=[ublk, ublk, wspec(b_r), wspec(b_i)],
        out_specs=[sblk, sblk],
        out_shape=[state, state],
        compiler_params=_cparams("parallel", "parallel"),
        name="s5_proj",
    )(ua, ub, b_r, b_i)
    col = pl.BlockSpec((1, rows, 128), lambda d, j: (d, 0, j))
    acol = pl.BlockSpec((1, 1, 1, 128), lambda d, j: (layer, d, 0, j))
    h_re, h_im = pl.pallas_call(
        functools.partial(_s5_rec_kernel, n_batch=lay.b, n_chunks=n_chunks, ctx_chunks=lay.n_ctx // S5_CHUNK),
        grid=(2, S5_STATE_W // 128),
        in_specs=[col, col, acol, acol],
        out_specs=[col, col],
        out_shape=[state, state],
        compiler_params=_cparams("parallel", "parallel"),
        name="s5_rec",
    )(s_re, s_im, a_re, a_im)
    yblk = pl.BlockSpec((1, tok, half), lambda d, i: (d, i, 0))
    yshape = jax.ShapeDtypeStruct((2, lay.b * lay.n_tot, half), F32)
    return pl.pallas_call(
        _s5_out_kernel,
        grid=(2, rows // tr),
        in_specs=[ublk, ublk, sblk, sblk, wspec(m), wspec(c_r), wspec(c_i)],
        out_specs=[yblk, yblk],
        out_shape=[yshape, yshape],
        compiler_params=_cparams("parallel", "parallel"),
        name="s5_out",
    )(ua, ub, h_re, h_im, m, c_r, c_i)


def _s5_glu_kernel(ua_ref, ub_ref, ya_ref, yb_ref, d_ref, w_ref, b_ref, o_ref):
    u = jnp.concatenate([ua_ref[...], ub_ref[...]], axis=1)
    y = d_ref[...] * u + jnp.concatenate([ya_ref[0] + ya_ref[1], yb_ref[0] + yb_ref[1]], axis=1)
    z = 0.5 * y * (1.0 + jnp.tanh(math.sqrt(2.0 / math.pi) * (y + 0.044715 * (y * y * y))))
    gate = _sigmoid(jnp.dot(z.astype(BF16), w_ref[...], preferred_element_type=F32) + b_ref[...])
    o_ref[...] = (z * gate).astype(BF16)


def _s5_glu_call(ua, ub, ya, yb, d, w, bias, tm):
    t, half = ua.shape
    urow = pl.BlockSpec((tm, half), lambda i: (i, 0))
    yrow = pl.BlockSpec((2, tm, half), lambda i: (0, i, 0))
    return pl.pallas_call(
        _s5_glu_kernel,
        grid=(t // tm,),
        in_specs=[urow, urow, yrow, yrow, _full(d), _full(w), _full(bias)],
        out_specs=pl.BlockSpec((tm, MIX_W), lambda i: (i, 0)),
        out_shape=jax.ShapeDtypeStruct((t, MIX_W), BF16),
        compiler_params=_cparams("parallel"),
        name="s5_glu",
    )(ua, ub, ya, yb, d, w, bias)


def _rw_pre_kernel(x_ref, prev_ref, next_ref, mu_ref, g64_ref, kk_g_ref, ka_ref, rk_ref,
                   w0_ref, w1_ref, w2_ref, a0_ref, a1_ref, a2_ref, g1_ref, g2_ref,
                   r_ref, v_ref, kk_ref, lw_ref, kka_ref, km_ref, bon_ref, gate_ref,
                   *, seq_tiles, n_ctx, n_tot):
    x = x_ref[...]
    n = x.shape[0]
    row = lax.broadcasted_iota(jnp.int32, (n, 1), 0)
    pos = (pl.program_id(0) % seq_tiles) * n + row
    left = jnp.where(row == 0, prev_ref[0, 7:8, :], pltpu.roll(x, 1, axis=0))
    left = jnp.where(jnp.logical_or(pos == 0, pos == n_ctx), 0.0, left)
    right = jnp.where(row == n - 1, next_ref[0, 0:1, :], pltpu.roll(x, n - 1, axis=0))
    right = jnp.where(jnp.logical_or(pos == n_ctx - 1, pos == n_tot - 1), 0.0, right)
    x = x + (0.5 * (left + right) - x) * mu_ref[...]
    r, k, v, xd = (x[:, i * MIX_W:(i + 1) * MIX_W] for i in range(4))
    g64 = g64_ref[...]
    kscaled = k * kk_g_ref[...]
    kk = kscaled / jnp.maximum(jnp.sqrt(_split_dot(kscaled * kscaled, g64)), 1e-12)
    xdb = xd.astype(BF16)
    r_ref[...] = r
    v_ref[...] = v
    kk_ref[...] = kk
    km_sum = None
    for d in range(2):
        lo = jnp.tanh(jnp.dot(xdb, w1_ref[d], preferred_element_type=F32))
        w_raw = w0_ref[d] + jnp.dot(lo.astype(BF16), w2_ref[d], preferred_element_type=F32)
        lw_ref[d] = -_sigmoid(w_raw) * math.exp(-0.5)
        ar = jnp.dot(xdb, a1_ref[d], preferred_element_type=F32)
        a = _sigmoid(a0_ref[d] + jnp.dot(ar.astype(BF16), a2_ref[d], preferred_element_type=F32))
        km = k * (1.0 + (a - 1.0) * ka_ref[...])
        kka_ref[d] = kk * a
        km_ref[d] = km
        km_sum = km if km_sum is None else km_sum + km
    bon_ref[...] = _split_dot(r * km_sum * rk_ref[...], g64) * v
    gr = _sigmoid(jnp.dot(xdb, g1_ref[...], preferred_element_type=F32))
    gate_ref[...] = jnp.dot(gr.astype(BF16), g2_ref[...], preferred_element_type=F32)


def _rw_pre_call(lay, rw, consts):
    t = rw.shape[0]
    tr, seq_tiles = _seq_tile(lay)
    nt = t // tr
    g8 = tr // 8
    rw8 = rw.reshape(t // 8, 8, _RW_W)
    row = pl.BlockSpec((tr, MIX_W), lambda i: (i, 0))
    row2 = pl.BlockSpec((2, tr, MIX_W), lambda i: (0, i, 0))
    sd = jax.ShapeDtypeStruct((t, MIX_W), F32)
    sd2 = jax.ShapeDtypeStruct((2, t, MIX_W), F32)
    return pl.pallas_call(
        functools.partial(_rw_pre_kernel, seq_tiles=seq_tiles, n_ctx=lay.n_ctx, n_tot=lay.n_tot),
        grid=(nt,),
        in_specs=[pl.BlockSpec((tr, _RW_W), lambda i: (i, 0)),
                  pl.BlockSpec((1, 8, _RW_W), lambda i: (jnp.maximum(i * g8 - 1, 0), 0, 0)),
                  pl.BlockSpec((1, 8, _RW_W), lambda i: (jnp.minimum((i + 1) * g8, t // 8 - 1), 0, 0))]
                 + [_full(a) for a in consts],
        out_specs=[row, row, row, row2, row2, row2, row, row],
        out_shape=[sd, sd, sd, sd2, sd2, sd2, sd, sd],
        compiler_params=_cparams("parallel"),
        name="rwkv_pre",
    )(rw, rw8, rw8, *consts)


def _head_masks(shape, lane_axis, seg):
    lane = lax.broadcasted_iota(jnp.int32, shape, lane_axis)
    return [jnp.logical_and(lane >= h * seg, lane < (h + 1) * seg) for h in range(RW_HEADS)]


def _rw_prep_kernel(*refs, rev):
    tiles = [_rw_prep_tile(sub, *refs, rev=rev) for sub in range(RW_PREP_TILES)]
    while tiles:
        tiles = [t for t in tiles if next(t, None) is not None]


def _rw_prep_tile(sub, r_ref, kk_ref, v_ref, lw_ref, ka_ref, km_ref, perm_ref, permt_ref, g_ref, eye_ref,
                  br_ref, ck_ref, uvt_ref, y0_ref, pc_ref, *, rev):
    C, NC = RW_CHUNK, RW_TILE // RW_CHUNK
    perm, permt, g64, eye4 = perm_ref[...], permt_ref[...], g_ref[...], eye_ref[...]
    tok = slice(sub * RW_TILE, (sub + 1) * RW_TILE)
    nat = jnp.concatenate([r_ref[0, tok, :], kk_ref[0, tok, :], v_ref[0, tok, :],
                           lw_ref[0, 0, tok, :], ka_ref[0, 0, tok, :], km_ref[0, 0, tok, :]], axis=1)
    hi = nat.astype(BF16)
    lo = (nat - hi.astype(F32)).astype(BF16)
    pm = jnp.dot(perm, jnp.concatenate([hi, lo], axis=0), preferred_element_type=F32)
    r, kk, v, lw, ka, km = (pm[:, i * MIX_W:(i + 1) * MIX_W] for i in range(6))
    slab = lambda x, j: x[j * NC:(j + 1) * NC, :]
    order = list(range(C))[::-1] if rev else list(range(C))
    pos = {j: i for i, j in enumerate(order)}
    cum, run = {}, None
    for j in order:
        run = slab(lw, j) if run is None else run + slab(lw, j)
        cum[j] = run
    tot = run
    yield True
    bh, ch, kh, rh, cp, kp, vv = {}, {}, {}, {}, {}, {}, {}
    for j in range(C):
        e_inv, e_end = jnp.exp(-cum[j]), jnp.exp(tot - cum[j])
        bh[j] = -slab(kk, j) * jnp.exp(cum[j] - slab(lw, j))
        ch[j], kh[j] = slab(ka, j) * e_inv, slab(km, j) * e_inv
        rh[j] = slab(r, j) * jnp.exp(cum[j])
        cp[j], kp[j] = slab(ka, j) * e_end, slab(km, j) * e_end
        vv[j] = slab(v, j)
    strict = [(t, s) for t in order for s in order if pos[s] < pos[t]]
    incl = [(t, s) for t in order for s in order if pos[s] <= pos[t]]
    def head_dots(lhs, rhs, pairs):
        prods = jnp.concatenate([lhs[t] * rhs[s] for t, s in pairs], axis=0).astype(BF16)
        gram = jnp.dot(prods, g64, preferred_element_type=F32)
        return {p: gram[i * NC:(i + 1) * NC, :] for i, p in enumerate(pairs)}

    yield True
    acb = head_dots(bh, ch, strict)
    yield True
    akb = head_dots(bh, kh, strict)
    yield True
    mcr = head_dots(rh, ch, incl)
    yield True
    mkr = head_dots(rh, kh, incl)
    yield True
    bt, u0 = {}, {}
    for t in order:
        b_acc, u_acc = bh[t], jnp.zeros_like(bh[t])
        for s in order:
            if pos[s] < pos[t]:
                b_acc = b_acc + acb[(t, s)] * bt[s]
                u_acc = u_acc + akb[(t, s)] * vv[s] + acb[(t, s)] * u0[s]
        bt[t], u0[t] = b_acc, u_acc
        yield True
    rt, y0 = {}, {}
    for t in order:
        r_acc, y_acc = rh[t], jnp.zeros_like(rh[t])
        for s in order:
            if pos[s] <= pos[t]:
                r_acc = r_acc + mcr[(t, s)] * bt[s]
                y_acc = y_acc + mcr[(t, s)] * u0[s] + mkr[(t, s)] * vv[s]
        rt[t], y0[t] = r_acc, y_acc
        yield True
    stackp = lambda dct: jnp.concatenate([dct[j] for j in range(C)], axis=0)
    b16 = lambda x: x.astype(BF16)
    y0p, u0p = stackp(y0), stackp(u0)
    y0h, u0h = b16(y0p), b16(u0p)
    cat = jnp.concatenate([b16(stackp(bt)), b16(stackp(rt)), b16(stackp(cp)), b16(stackp(kp)),
                           y0h, b16(y0p - y0h.astype(F32)), u0h, b16(u0p - u0h.astype(F32)), b16(stackp(vv))], axis=1)
    natural = jnp.dot(permt, cat, preferred_element_type=F32)
    seg = lambda i: natural[:, i * MIX_W:(i + 1) * MIX_W]
    yield True
    btn, rtn, cpn, kpn = b16(seg(0)), b16(seg(1)), b16(seg(2)), b16(seg(3))
    y0_ref[0, tok, :] = seg(4) + seg(5)
    u0h_n, u0l_n, vn = b16(seg(6)), b16(seg(7)), b16(seg(8))
    hm = _head_masks((C, MIX_W), 1, RW_DIM)
    zero = jnp.zeros((C, MIX_W), BF16)
    zh, zl = [], []
    for c in range(NC):
        rows = slice(c * C, (c + 1) * C)
        br_ref[0, sub * NC + c, 0:C, :] = btn[rows]
        br_ref[0, sub * NC + c, C:2 * C, :] = rtn[rows]
        ck_ref[0, sub * NC + c, 0:C, :] = cpn[rows]
        ck_ref[0, sub * NC + c, C:2 * C, :] = kpn[rows]
        for h in range(RW_HEADS):
            zh += [jnp.where(hm[h], u0h_n[rows], zero), jnp.where(hm[h], vn[rows], zero)]
            zl += [jnp.where(hm[h], u0l_n[rows], zero), zero]
    z = jnp.concatenate([jnp.concatenate(zh, axis=0), jnp.concatenate(zl, axis=0)], axis=1)
    uvt = lax.dot_general(eye4, z, _NT, preferred_element_type=F32)
    for c in range(NC):
        uvt_ref[0, sub * NC + c] = uvt[:, c * 2 * C * RW_HEADS:(c + 1) * 2 * C * RW_HEADS]
    pc_ref[0, sub * NC:(sub + 1) * NC, :] = jnp.exp(tot)


def _rw_prep_call(lay, shared, perdir, consts, rev):
    b, n_tot, tt = lay.b, lay.n_tot, RW_TILE * RW_PREP_TILES
    assert n_tot % tt == 0
    nck = n_tot // RW_CHUNK
    cpt = tt // RW_CHUNK
    d = 1 if rev else 0
    sh = [a.reshape(b, n_tot, MIX_W) for a in shared]
    pd = [a.reshape(2, b, n_tot, MIX_W) for a in perdir]
    tok = pl.BlockSpec((1, tt, MIX_W), lambda i, j: (i, j, 0))
    tok_d = pl.BlockSpec((1, 1, tt, MIX_W), lambda i, j: (d, i, j, 0))
    rows32 = pl.BlockSpec((1, cpt, 2 * RW_CHUNK, MIX_W), lambda i, j: (i, j, 0, 0))
    return pl.pallas_call(
        functools.partial(_rw_prep_kernel, rev=rev),
        grid=(b, n_tot // tt),
        in_specs=[tok] * 3 + [tok_d] * 3 + [_full(a) for a in consts],
        out_specs=[rows32, rows32,
                   pl.BlockSpec((1, cpt, RW_DIM, 2 * RW_CHUNK * RW_HEADS), lambda i, j: (i, j, 0, 0)),
                   tok,
                   pl.BlockSpec((1, cpt, MIX_W), lambda i, j: (i, j, 0))],
        out_shape=[jax.ShapeDtypeStruct((b, nck, 2 * RW_CHUNK, MIX_W), BF16),
                   jax.ShapeDtypeStruct((b, nck, 2 * RW_CHUNK, MIX_W), BF16),
                   jax.ShapeDtypeStruct((b, nck, RW_DIM, 2 * RW_CHUNK * RW_HEADS), F32),
                   jax.ShapeDtypeStruct((b, n_tot, MIX_W), F32),
                   jax.ShapeDtypeStruct((b, nck, MIX_W), F32)],
        compiler_params=_cparams("parallel", "parallel"),
        name="rwkv_prep_rev" if rev else "rwkv_prep_fwd",
    )(*sh, *pd, *consts)


def _rw_scan_kernel(brf, ckf, uvtf, pcf, brr, ckr, uvtr, pcr, ytf_ref, ytr_ref, s_scr, *, n_batch):
    @pl.when(pl.program_id(0) == 0)
    def _():
        s_scr[...] = jnp.zeros_like(s_scr)

    cpt = RW_TILE // RW_CHUNK
    hm = _head_masks((2 * RW_CHUNK, MIX_W), 1, RW_DIM)
    lane = lax.broadcasted_iota(jnp.int32, (RW_DIM, 2 * RW_CHUNK * RW_HEADS), 1)
    is_u = (lane & (2 * RW_CHUNK - 1)) < RW_CHUNK
    per_head = lambda x: jnp.concatenate([jnp.where(m, x, jnp.zeros_like(x)) for m in hm], axis=0)

    def refs_of(p, c):
        d, b = divmod(p, n_batch)
        refs = (brf, ckf, uvtf, pcf, ytf_ref) if d == 0 else (brr, ckr, uvtr, pcr, ytr_ref)
        return refs, b, (c if d == 0 else cpt - 1 - c)

    def step(c, carry):
        lhs = []
        for p in range(2 * n_batch):
            (br_ref, _, uvt_ref, _, yt_ref), b, cc = refs_of(p, c)
            s = s_scr[p]
            shi = s.astype(BF16)
            slo = (s - shi.astype(F32)).astype(BF16)
            w2 = lax.dot_general(jnp.concatenate([shi, slo], axis=0), per_head(br_ref[b, cc]), _NT, preferred_element_type=F32)
            w = w2[:RW_DIM] + w2[RW_DIM:]
            yt_ref[b, cc] = w
            uvt = uvt_ref[b, cc]
            lhs.append(jnp.where(is_u, w + uvt, uvt).astype(BF16))
        for p in range(2 * n_batch):
            (_, ck_ref, _, pc_ref, _), b, cc = refs_of(p, c)
            s_scr[p] = (s_scr[p] * pc_ref[b, pl.ds(cc, 1), :]
                        + jnp.dot(lhs[p], per_head(ck_ref[b, cc]), preferred_element_type=F32))
        return carry

    lax.fori_loop(0, cpt, step, 0)


def _rw_scan_call(lay, fwd, rev):
    b, n_tot, tt = lay.b, lay.n_tot, RW_TILE
    assert lay.n_ctx % tt == 0 and lay.n_lat % tt == 0
    nt, ct = n_tot // tt, lay.n_ctx // tt
    cpt = tt // RW_CHUNK
    rev_tile = lambda i: jnp.where(i < ct, ct - 1 - i, nt - 1 + ct - i)

    def specs(tile):
        return [pl.BlockSpec((b, cpt, 2 * RW_CHUNK, MIX_W), lambda i: (0, tile(i), 0, 0)),
                pl.BlockSpec((b, cpt, 2 * RW_CHUNK, MIX_W), lambda i: (0, tile(i), 0, 0)),
                pl.BlockSpec((b, cpt, RW_DIM, 2 * RW_CHUNK * RW_HEADS), lambda i: (0, tile(i), 0, 0)),
                pl.BlockSpec((b, cpt, MIX_W), lambda i: (0, tile(i), 0))]

    ident = lambda i: i
    yt = jax.ShapeDtypeStruct((b, n_tot // RW_CHUNK, RW_DIM, 2 * RW_CHUNK * RW_HEADS), F32)
    return pl.pallas_call(
        functools.partial(_rw_scan_kernel, n_batch=b),
        grid=(nt,),
        in_specs=specs(ident) + specs(rev_tile),
        out_specs=[specs(ident)[2], specs(rev_tile)[2]],
        out_shape=[yt, yt],
        scratch_shapes=[pltpu.VMEM((2 * b, RW_DIM, MIX_W), F32)],
        compiler_params=_cparams("arbitrary"),
        name="rwkv_scan",
    )(*fwd, *rev)


def _rw_fin_kernel(ytf_ref, ytr_ref, y0f_ref, y0r_ref, bon_ref, gate_ref, asel_ref, g64_ref, lng_ref, lnb_ref, o_ref):
    cpt = RW_TILE // RW_CHUNK
    asel = asel_ref[...]
    width = cpt * 2 * RW_CHUNK * RW_HEADS
    lane = lax.broadcasted_iota(jnp.int32, (RW_DIM, width), 1)
    lane_head = jnp.bitwise_and(jnp.right_shift(lane, 5), RW_HEADS - 1)

    def base(yt_ref, sub):
        yt = jnp.concatenate([yt_ref[0, sub * cpt + c] for c in range(cpt)], axis=1).astype(BF16)
        rows = jnp.concatenate([jnp.where(lane_head == h, yt, jnp.zeros_like(yt)) for h in range(RW_HEADS)], axis=0)
        return lax.dot_general(asel, rows, _NT, preferred_element_type=F32)

    g64 = g64_ref[...]
    bases = [(base(ytf_ref, sub), base(ytr_ref, sub)) for sub in range(RW_PREP_TILES)]
    for sub, (yb_f, yb_r) in enumerate(bases):
        tok = slice(sub * RW_TILE, (sub + 1) * RW_TILE)
        y = yb_f + y0f_ref[0, tok, :] + yb_r + y0r_ref[0, tok, :]
        mean = _split_dot(y, g64) * (1.0 / RW_DIM)
        c = y - mean
        var = _split_dot(c * c, g64) * (1.0 / RW_DIM)
        out = c * lax.rsqrt(var + RW_LN_EPS) * lng_ref[...] + lnb_ref[...] + bon_ref[tok, :]
        o_ref[tok, :] = (out * gate_ref[tok, :]).astype(BF16)


def _rw_fin_call(lay, ytf, ytr, y0f, y0r, bon, gate, consts):
    b, n_tot, tt = lay.b, lay.n_tot, RW_TILE * RW_PREP_TILES
    nt = n_tot // tt
    cpt = tt // RW_CHUNK
    ytb = pl.BlockSpec((1, cpt, RW_DIM, 2 * RW_CHUNK * RW_HEADS), lambda i, j: (i, j, 0, 0))
    y0b = pl.BlockSpec((1, tt, MIX_W), lambda i, j: (i, j, 0))
    row = pl.BlockSpec((tt, MIX_W), lambda i, j: (i * nt + j, 0))
    return pl.pallas_call(
        _rw_fin_kernel,
        grid=(b, nt),
        in_specs=[ytb, ytb, y0b, y0b, row, row] + [_full(a) for a in consts],
        out_specs=row,
        out_shape=jax.ShapeDtypeStruct((b * n_tot, MIX_W), BF16),
        compiler_params=_cparams("parallel", "parallel"),
        name="rwkv_finish",
    )(ytf, ytr, y0f, y0r, bon, gate, *consts)


def _rw_constants():
    c, nc = RW_CHUNK, RW_TILE // RW_CHUNK
    perm = np.zeros((RW_TILE, RW_TILE), np.float32)
    for ci in range(nc):
        for j in range(c):
            perm[j * nc + ci, ci * c + j] = 1.0
    lane = np.arange(MIX_W) % RW_DIM
    eye4 = (lane[None, :] == np.arange(RW_DIM)[:, None]).astype(np.float32)
    lanes = np.arange(nc * 2 * c * RW_HEADS)
    lane_chunk, lane_tok = lanes // (2 * c * RW_HEADS), lanes % (2 * c)
    t = np.arange(RW_TILE)
    asel = ((lane_chunk[None, :] == (t // c)[:, None]) & (lane_tok[None, :] == (c + t % c)[:, None])).astype(np.float32)
    as16 = lambda a: jnp.asarray(a, BF16)
    twice = lambda a: np.concatenate([a, a], axis=1)
    return as16(twice(perm)), as16(perm.T), as16(twice(eye4)), as16(asel)


def _rwkv_branch(lay, rw, pre_consts, g64, lng, lnb):
    r_, v_, kk_, lw_, kka_, km_, bon, gate = _rw_pre_call(lay, rw, pre_consts)
    perm, permt, eye4, asel = _rw_constants()
    prep_consts = (perm, permt, g64, eye4)
    fwd = _rw_prep_call(lay, (r_, kk_, v_), (lw_, kka_, km_), prep_consts, False)
    rev = _rw_prep_call(lay, (r_, kk_, v_), (lw_, kka_, km_), prep_consts, True)
    pick = lambda o: (o[0], o[1], o[2], o[4])
    ytf, ytr = _rw_scan_call(lay, pick(fwd), pick(rev))
    return _rw_fin_call(lay, ytf, ytr, fwd[3], rev[3], bon, gate, (asel, g64, lng, lnb))


def _merge_kernel(x_ref, modb_ref, modc_ref, g_ref, wg_ref, ya_ref, yb_ref, yc_ref, yd_ref, wb_ref, wo_ref, o_ref,
                  *, ctx_rows, tiles_per_seq):
    x = x_ref[...]
    mrow = lambda r: _mod_row(modb_ref, modc_ref, r, x.shape[0], ctx_rows, tiles_per_seq)
    h = _modulate(x, g_ref[...], mrow(0), mrow(1)).astype(BF16)
    merged = None
    for i, y_ref in enumerate((ya_ref, yb_ref, yc_ref, yd_ref)):
        gate = _sigmoid(jnp.dot(h, wg_ref[:, i * D_MODEL:(i + 1) * D_MODEL], preferred_element_type=F32))
        term = gate * jnp.dot(y_ref[...], wb_ref[i], preferred_element_type=F32)
        merged = term if merged is None else merged + term
    out = jnp.dot(merged.astype(BF16), wo_ref[...], preferred_element_type=F32)
    o_ref[...] = x + mrow(2) * out


def _merge_call(lay, with_ctx, x_all, mod, g, w_gate, ya, yb, yc, yd, w_branch, w_out):
    if with_ctx:
        tm, tps = _seq_tile(lay)
        src, n_tiles, ctx_rows = (lambda i: i), lay.b * tps, lay.n_ctx
    else:
        tm, tps = lay.t, lay.lat_tiles
        src, n_tiles, ctx_rows = lay.src_tile(False), lay.n_tiles(False), 0
    full_row = lambda w: pl.BlockSpec((tm, w), lambda i: (src(i), 0))
    out_row = lambda w: pl.BlockSpec((tm, w), lambda i: (i, 0))
    return pl.pallas_call(
        functools.partial(_merge_kernel, ctx_rows=ctx_rows, tiles_per_seq=tps),
        grid=(n_tiles,),
        in_specs=[full_row(D_MODEL), pl.BlockSpec((1, 6, D_MODEL), lambda i: (i // tps, 0, 0)),
                  pl.BlockSpec((1, 6, D_MODEL), lambda i: (lay.b, 0, 0)), _full(g), _full(w_gate),
                  out_row(MIX_W), full_row(MIX_W), out_row(MIX_W), full_row(MIX_W), _full(w_branch), _full(w_out)],
        out_specs=out_row(D_MODEL),
        out_shape=jax.ShapeDtypeStruct((lay.rows(with_ctx), D_MODEL), F32),
        compiler_params=_cparams("parallel"),
        name="merge_out",
    )(x_all, mod, mod, g, w_gate, ya, yb, yc, yd, w_branch, w_out)


def _router_kernel(x_ref, modb_ref, modc_ref, g_ref, wh_ref, wl_ref, bias_ref, f_ref, comb_ref, gid_ref,
                   *, ctx_rows, tiles_per_seq):
    x = x_ref[...]
    mrow = lambda r: _mod_row(modb_ref, modc_ref, r, x.shape[0], ctx_rows, tiles_per_seq)
    f = _modulate(x, g_ref[...], mrow(3), mrow(4))
    fh = f.astype(BF16)
    f_ref[...] = fh
    fl = (f - fh.astype(F32)).astype(BF16)
    nt = (((1,), (1,)), ((), ()))
    wh, wl = wh_ref[...], wl_ref[...]
    logits = (lax.dot_general(wh, fh, nt, preferred_element_type=F32)
              + lax.dot_general(wh, fl, nt, preferred_element_type=F32)
              + lax.dot_general(wl, fh, nt, preferred_element_type=F32))
    scores = _sigmoid(logits)
    biased = scores + bias_ref[...]
    sc = [scores[e:e + 1, :] for e in range(N_EXPERTS)]
    bi = [biased[e:e + 1, :] for e in range(N_EXPERTS)]
    group_score = []
    for g in range(N_GROUPS):
        a, b, c, d = bi[4 * g:4 * g + 4]
        m1, n1, m2, n2 = jnp.maximum(a, b), jnp.minimum(a, b), jnp.maximum(c, d), jnp.minimum(c, d)
        group_score.append(jnp.maximum(m1, m2) + jnp.maximum(jnp.minimum(m1, m2), jnp.maximum(n1, n2)))

    def first_argmax(vals):
        top = functools.reduce(jnp.maximum, vals)
        seen, hot = None, []
        for v in vals:
            h = v == top
            if seen is not None:
                h = jnp.logical_and(h, jnp.logical_not(seen))
            seen = h if seen is None else jnp.logical_or(seen, h)
            hot.append(h)
        return hot

    in_group = first_argmax(group_score)
    masked = [jnp.where(in_group[e // EXPERTS_PER_GROUP], bi[e], -jnp.inf) for e in range(N_EXPERTS)]
    hot1 = first_argmax(masked)
    hot2 = first_argmax([jnp.where(h, -jnp.inf, v) for h, v in zip(hot1, masked)])
    w1 = functools.reduce(jnp.add, [jnp.where(h, s, 0.0) for h, s in zip(hot1, sc)])
    w2 = functools.reduce(jnp.add, [jnp.where(h, s, 0.0) for h, s in zip(hot2, sc)])
    inv_tot = 1.0 / (w1 + w2)
    for e in range(N_EXPERTS):
        comb_ref[e:e + 1, :] = (jnp.where(hot1[e], w1, 0.0) + jnp.where(hot2[e], w2, 0.0)) * inv_tot
    gid_ref[...] = functools.reduce(jnp.add, [jnp.where(in_group[g], g, 0) for g in range(1, N_GROUPS)])


def _router_call(lay, with_ctx, x, mod, g, wh, wl, bias):
    t = x.shape[0]
    if with_ctx:
        (tm, tps), ctx_rows = _seq_tile(lay), lay.n_ctx
    else:
        pair = 2 if lay.lat_tiles % 2 == 0 else 1
        tm, tps, ctx_rows = pair * lay.t, lay.lat_tiles // pair, 0
    return pl.pallas_call(
        functools.partial(_router_kernel, ctx_rows=ctx_rows, tiles_per_seq=tps),
        grid=(t // tm,),
        in_specs=[pl.BlockSpec((tm, D_MODEL), lambda i: (i, 0)),
                  pl.BlockSpec((1, 6, D_MODEL), lambda i: (i // tps, 0, 0)),
                  pl.BlockSpec((1, 6, D_MODEL), lambda i: (lay.b, 0, 0)), _full(g), _full(wh), _full(wl), _full(bias)],
        out_specs=[pl.BlockSpec((tm, D_MODEL), lambda i: (i, 0)), pl.BlockSpec((N_EXPERTS, tm), lambda i: (0, i)),
                   pl.BlockSpec((1, tm), lambda i: (0, i))],
        out_shape=[jax.ShapeDtypeStruct((t, D_MODEL), BF16), jax.ShapeDtypeStruct((N_EXPERTS, t), F32),
                   jax.ShapeDtypeStruct((1, t), jnp.int32)],
        compiler_params=_cparams("parallel"),
        name="moe_router",
    )(x, mod, mod, g, wh, wl, bias)


def _moe_plan(gid, n_tiles, tm):
    g = gid.reshape(n_tiles, tm)
    onehot = (g[..., None] == jnp.arange(N_GROUPS, dtype=jnp.int32)).astype(jnp.int32)
    rank = jnp.cumsum(onehot, axis=1) - onehot
    counts = jnp.sum(onehot, axis=1)
    padded = (counts + 15) // 16 * 16
    offs = jnp.cumsum(padded, axis=1) - padded
    pos = jnp.sum(onehot * (offs[:, None, :] + rank), axis=-1)
    n_over = (jnp.maximum(padded - MOE_BLOCK, 0) + MOE_OVER - 1) // MOE_OVER
    return pos.astype(jnp.int32), offs.astype(jnp.int32), n_over.astype(jnp.int32)


def _moe_kernel(offs_ref, nover_ref, f_ref, posr_ref, posc_ref, comb_ref, wg_ref, wu_ref, wd_ref, x_ref, modb_ref,
                modc_ref, o_ref, xs_scr, cs_scr, ys_scr, *, ctx_rows, tiles_per_seq):
    i, e = pl.program_id(0), pl.program_id(1)
    n_slots, tm = xs_scr.shape[0], f_ref.shape[0]
    n_live = min(n_slots, -(-(tm + 16 * N_GROUPS) // 256) * 256)

    @pl.when(e == 0)
    def _():
        slot = lax.broadcasted_iota(jnp.int32, (n_live, tm), 0)
        place = (slot == posr_ref[0]).astype(BF16)
        xs_scr[0:n_live, :] = jnp.dot(place, f_ref[...], preferred_element_type=F32).astype(BF16)
        xs_scr[n_live:n_slots, :] = jnp.zeros((n_slots - n_live, D_MODEL), BF16)
        cs_scr[0:n_live, :] = _split_dot_rhs(place, comb_ref[...])
        cs_scr[n_live:n_slots, :] = jnp.zeros((n_slots - n_live, N_EXPERTS), F32)
        ys_scr[...] = jnp.zeros_like(ys_scr)

    grp = lax.shift_right_logical(e, 2)
    start = offs_ref[i, grp]
    lane = lax.broadcasted_iota(jnp.int32, (1, N_EXPERTS), 1)

    def run(rows):
        xb = xs_scr[rows, :]
        gate = jnp.dot(xb, wg_ref[0], preferred_element_type=F32)
        up = jnp.dot(xb, wu_ref[0], preferred_element_type=F32)
        act = (gate * _sigmoid(gate) * up).astype(BF16)
        down = jnp.dot(act, wd_ref[0], preferred_element_type=F32)
        c_e = jnp.sum(jnp.where(lane == e, cs_scr[rows, :], 0.0), axis=1, keepdims=True)
        ys_scr[rows, :] += c_e * down

    run(pl.ds(pl.multiple_of(start, 16), MOE_BLOCK))

    def overflow(k, carry):
        run(pl.ds(pl.multiple_of(start + MOE_BLOCK + k * MOE_OVER, 16), MOE_OVER))
        return carry

    lax.fori_loop(0, nover_ref[i, grp], overflow, 0)

    @pl.when(e == N_EXPERTS - 1)
    def _():
        slot = lax.broadcasted_iota(jnp.int32, (tm, n_live), 1)
        fetch = (slot == posc_ref[...]).astype(BF16)
        y = jnp.dot(fetch, ys_scr[0:n_live, :].astype(BF16), preferred_element_type=F32)
        res_gate = modb_ref[0, 5:6, :]
        if ctx_rows:
            row = lax.broadcasted_iota(jnp.int32, y.shape, 0)
            first = i % tiles_per_seq == 0
            res_gate = jnp.where(jnp.logical_and(first, row < ctx_rows), modc_ref[0, 5:6, :], res_gate)
        o_ref[...] = x_ref[...] + res_gate * y


def _moe_call(lay, with_ctx, f, comb, gid, wg, wu, wd, x, mod):
    t = f.shape[0]
    seq = lay.n_tot if with_ctx else lay.n_lat
    tm = MOE_TILE if seq % MOE_TILE == 0 else math.gcd(seq, 1024)
    tps, n_tiles = seq // tm, t // tm
    ctx_rows = lay.n_ctx if with_ctx else 0
    assert ctx_rows <= tm
    n_slots = -(-(tm + 16 * N_GROUPS + MOE_BLOCK + MOE_OVER) // 256) * 256
    pos, offs, n_over = _moe_plan(gid, n_tiles, tm)
    wspec = lambda a: pl.BlockSpec((1,) + a.shape[1:], lambda i, e, *_: (e, 0, 0))
    tok = lambda w: pl.BlockSpec((tm, w), lambda i, e, *_: (i, 0))
    grid_spec = pltpu.PrefetchScalarGridSpec(
        num_scalar_prefetch=2,
        grid=(n_tiles, N_EXPERTS),
        in_specs=[tok(D_MODEL), pl.BlockSpec((1, 1, tm), lambda i, e, *_: (i, 0, 0)), tok(1), tok(N_EXPERTS),
                  wspec(wg), wspec(wu), wspec(wd), tok(D_MODEL),
                  pl.BlockSpec((1, 6, D_MODEL), lambda i, e, *_: (i // tps, 0, 0)),
                  pl.BlockSpec((1, 6, D_MODEL), lambda i, e, *_: (lay.b, 0, 0))],
        out_specs=tok(D_MODEL),
        scratch_shapes=[pltpu.VMEM((n_slots, D_MODEL), BF16), pltpu.VMEM((n_slots, N_EXPERTS), F32),
                        pltpu.VMEM((n_slots, D_MODEL), F32)])
    return pl.pallas_call(
        functools.partial(_moe_kernel, ctx_rows=ctx_rows, tiles_per_seq=tps),
        grid_spec=grid_spec,
        out_shape=jax.ShapeDtypeStruct((t, D_MODEL), F32),
        compiler_params=_cparams("parallel", "arbitrary"),
        name="moe_experts",
    )(offs, n_over, f, pos.reshape(n_tiles, 1, tm), pos.reshape(t, 1), comb, wg, wu, wd, x, mod, mod)


def _block_ones(n, group):
    i = np.arange(n) // group
    return jnp.asarray(i[:, None] == i[None, :], dtype=BF16)


def _rope_tables(n_ctx, n_lat):
    rows = n_lat // GRID_W
    row = jnp.repeat(jnp.arange(rows, dtype=F32), GRID_W)
    col = jnp.tile(jnp.arange(GRID_W, dtype=F32), rows)

    def angles(rot_dim):
        n_freq = rot_dim // 4
        inv_freq = ROPE_BASE ** (-jnp.arange(n_freq, dtype=F32) / n_freq)
        ang = jnp.concatenate([row[:, None] * inv_freq, col[:, None] * inv_freq], axis=-1)
        return jnp.cos(ang), jnp.sin(ang)

    c, s = angles(DA_DIM)
    cda = jnp.tile(jnp.concatenate([c, c], -1), (1, 2 * DA_HEADS))
    sda = jnp.tile(jnp.concatenate([-s, s], -1), (1, 2 * DA_HEADS))
    c, s = angles(MLA_ROPE)
    one = jnp.ones((n_lat, MLA_NOPE), F32)
    pad = MLA_HEAD_PAD - MLA_NOPE - MLA_ROPE
    cml = jnp.tile(jnp.concatenate([one, c, c, jnp.ones((n_lat, pad), F32)], -1), (1, MLA_HEADS))
    sml = jnp.tile(jnp.concatenate([0 * one, -s, s, jnp.zeros((n_lat, pad), F32)], -1), (1, MLA_HEADS))
    ident = lambda t, v: jnp.concatenate([jnp.full((n_ctx, MIX_W), v, F32), t], axis=0)
    return ident(cda, 1.0), ident(sda, 0.0), ident(cml, 1.0), ident(sml, 0.0)


def _pad_heads(w, n_heads, src_w, lo, hi, dst_w=MLA_HEAD_PAD):
    w = w.reshape(w.shape[0], n_heads, src_w)[:, :, lo:hi]
    w = jnp.pad(w, ((0, 0), (0, 0), (0, dst_w - (hi - lo))))
    return w.reshape(w.shape[0], n_heads * dst_w)


def _mix_weight(w_in_l):
    w = w_in_l
    kr = w[:, 1344:1360]
    z = lambda n: jnp.zeros((D_MODEL, n), w.dtype)
    kr_wide = jnp.concatenate([jnp.concatenate([z(MLA_NOPE), kr, z(MLA_HEAD_PAD - MLA_NOPE - MLA_ROPE)], 1)] * MLA_HEADS, 1)
    return jnp.concatenate([w[:, 0:1024], w[:, 1024:1216], z(64), w[:, 1216:1344], kr_wide, w[:, 1360:2384]], axis=1).astype(BF16)


def kernel(x, c, ctx, c_ctx, w_ada, b_ada, norm_mix_g, norm_ffn_g, w_in, da_qk_norm_g, da_lambda, da_subln_g, s5_lam_re, s5_lam_im, s5_log_dt, s5_b_re, s5_b_im, s5_c_re, s5_c_im, s5_d, s5_w_glu, s5_b_glu, mla_cq_norm_g, mla_ckv_norm_g, mla_w_uq, mla_w_ukv, mla_qk_norm_g, rw_mu, rw_w0, rw_w1, rw_w2, rw_a0, rw_a1, rw_a2, rw_g1, rw_g2, rw_k_k, rw_k_a, rw_r_k, rw_ln_g, rw_ln_b, w_branch, w_out, router_w, router_bias, exp_w_gate, exp_w_up, exp_w_down):
    b, n_lat, dm = x.shape
    n_ctx = ctx.shape[1]
    depth = w_ada.shape[0]
    assert dm == D_MODEL
    lay = _Layout(b, n_ctx, n_lat)
    t_all = b * lay.n_tot
    tm_big = _seq_tile(lay)[0]

    g32 = _block_ones(MIX_W, DA_DIM)
    g64 = _block_ones(MIX_W, RW_DIM)
    tabs = _rope_tables(n_ctx, n_lat)
    row = lambda v: v.reshape(1, -1).astype(F32)
    bf = lambda a: a.astype(BF16)

    cc = jnp.zeros((16, dm), F32).at[:b].set(c).at[b].set(c_ctx)
    mod_all = _ada_call(cc, w_ada, b_ada)
    x_all = jnp.concatenate([ctx, x], axis=1).reshape(t_all, dm)

    s5_mats = jax.vmap(_s5_mats)(s5_lam_re, s5_lam_im, s5_log_dt, s5_b_re, s5_b_im, s5_c_re, s5_c_im)

    wr_hi = router_w.T.astype(BF16)
    wr_lo = (router_w.T - wr_hi.astype(F32)).astype(BF16)
    r_bias = router_bias.reshape(N_EXPERTS, 1).astype(F32)

    for l in range(depth):
        need_ctx = l < depth - 1
        lambda_init = 0.8 - 0.6 * math.exp(-0.3 * l)
        mod = mod_all[l, :b + 1].reshape(b + 1, 6, dm)
        g_mix = row(norm_mix_g[l])
        da, s5a, s5b, mla, rw = _inproj_call(lay, x_all, mod, g_mix, _mix_weight(w_in[l]))

        log2e = math.log2(math.e)
        gda = jnp.stack([jnp.tile(da_qk_norm_g[l, 0], 2 * DA_HEADS) * (DA_DIM ** -0.5 * log2e), jnp.tile(da_qk_norm_g[l, 1], 2 * DA_HEADS)])
        mla_pad = MLA_HEAD_PAD - MLA_NOPE - MLA_ROPE
        gml = jnp.stack([jnp.tile(jnp.pad(mla_qk_norm_g[l, 0], (0, mla_pad)), MLA_HEADS) * ((MLA_NOPE + MLA_ROPE) ** -0.5 * log2e),
                         jnp.tile(jnp.pad(mla_qk_norm_g[l, 1], (0, mla_pad)), MLA_HEADS)])
        wuq = bf(jnp.pad(_pad_heads(mla_w_uq[l], MLA_HEADS, MLA_NOPE + MLA_ROPE, 0, MLA_NOPE + MLA_ROPE), ((0, 64), (0, 0))))
        wuk = bf(_pad_heads(mla_w_ukv[l], MLA_HEADS, MLA_NOPE + MLA_VDIM, 0, MLA_NOPE))
        wuv = bf(_pad_heads(mla_w_ukv[l], MLA_HEADS, MLA_NOPE + MLA_VDIM, MLA_NOPE, MLA_NOPE + MLA_VDIM))
        consts = (g32, g64, gda.astype(F32), gml.astype(F32), row(jnp.pad(mla_cq_norm_g[l], (0, 64))), row(mla_ckv_norm_g[l]),
                  wuq, wuk, wuv)
        qd, kdt, vd, qm, kmt, vm = _qkprep_call(lay, da, mla, tabs, consts)

        lam32 = da_lambda[l].astype(F32)
        lmbda = (jnp.exp(jnp.sum(lam32[0] * lam32[1])) - jnp.exp(jnp.sum(lam32[2] * lam32[3])) + lambda_init).reshape(1, 1)
        subln = row(jnp.tile(da_subln_g[l], DA_HEADS) * (1.0 - lambda_init))
        ya = _attention(lay, qd, kdt, vd, (lmbda, subln, g64), True, need_ctx, "diff_attn")
        yc = _attention(lay, qm, kmt, vm, (lmbda, subln, g64), False, need_ctx, "mla_attn")

        ys_a, ys_b = _s5_scan(lay, s5a, s5b, s5_mats, l)
        yb = _s5_glu_call(s5a, s5b, ys_a, ys_b, row(s5_d[l]), bf(s5_w_glu[l]), row(s5_b_glu[l]), tm_big)

        pre_consts = (row(rw_mu[l]), g64, row(rw_k_k[l]), row(rw_k_a[l]), row(rw_r_k[l]),
                      rw_w0[l].reshape(2, 1, MIX_W), bf(rw_w1[l]), bf(rw_w2[l]),
                      rw_a0[l].reshape(2, 1, MIX_W), bf(rw_a1[l]), bf(rw_a2[l]), bf(rw_g1[l]), bf(rw_g2[l]))
        yd = _rwkv_branch(lay, rw, pre_consts, g64, row(rw_ln_g[l]), row(rw_ln_b[l]))

        x_mid = _merge_call(lay, need_ctx, x_all, mod, g_mix, bf(w_in[l][:, 2384:]), ya, yb, yc, yd,
                            bf(w_branch[l]), bf(w_out[l]))
        f, comb_t, gid = _router_call(lay, need_ctx, x_mid, mod, row(norm_ffn_g[l]), wr_hi, wr_lo, r_bias)
        x_all = _moe_call(lay, need_ctx, f, comb_t.T, gid, bf(exp_w_gate[l]), bf(exp_w_up[l]), bf(exp_w_down[l]), x_mid, mod)
    return x_all.reshape(b, n_lat, dm)
```

```python
import functools
import math

import numpy as np
import jax
import jax.numpy as jnp
from jax import lax
from jax.experimental import pallas as pl
from jax.experimental.pallas import tpu as pltpu

F32 = jnp.float32
BF16 = jnp.bfloat16

D_MODEL = 1024
GRID_W = 64
ROPE_BASE = 10000.0
EPS = 1e-6
DA_HEADS, DA_DIM, DA_VDIM = 4, 32, 64
S5_GROUPS, S5_CH, S5_STATE = 16, 16, 64
MLA_HEADS, MLA_NOPE, MLA_ROPE, MLA_VDIM = 4, 32, 16, 64
MLA_Q_RANK = 192
MLA_HEAD_PAD = 64
RW_HEADS, RW_DIM = 4, 64
RW_LN_EPS = 64e-5
N_EXPERTS, N_GROUPS, EXPERTS_PER_GROUP = 16, 4, 4
D_FF = 512
MIX_W = 256

S5_CHUNK = 8
S5_FLAT = S5_CHUNK * MIX_W
S5_STATE_W = S5_GROUPS * S5_STATE
RW_CHUNK = 16
RW_TILE = 128
RW_PREP_TILES = 2
_NT = (((1,), (1,)), ((), ()))
TOKEN_TILE = 256
PROJ_TILE = 768
MOE_TILE = 1152
MOE_BLOCK = 384
MOE_OVER = 128

_DA_W, _S5_W, _MLA_W, _RW_W = 768, 256, 640, 1024
_MIX_COLS = _DA_W + _S5_W + _MLA_W + _RW_W

V7X_VMEM_BYTES = 64 * 2**20
_VMEM_LIMIT = V7X_VMEM_BYTES - 8 * 2**20


def _cparams(*sem):
    return pltpu.CompilerParams(dimension_semantics=sem, vmem_limit_bytes=_VMEM_LIMIT)


def _full(a):
    return pl.BlockSpec(a.shape, lambda *_, nd=a.ndim: (0,) * nd)


def _split_dot(x, w, terms=2):
    acc = None
    rem = x
    for i in range(terms):
        part = rem.astype(BF16)
        d = jnp.dot(part, w, preferred_element_type=F32)
        acc = d if acc is None else acc + d
        if i + 1 < terms:
            rem = rem - part.astype(F32)
    return acc


def _split_dot_rhs(w, x):
    hi = x.astype(BF16)
    lo = (x - hi.astype(F32)).astype(BF16)
    return jnp.dot(w, hi, preferred_element_type=F32) + jnp.dot(w, lo, preferred_element_type=F32)


def _modulate(x, g, shift, scale):
    xn = x * lax.rsqrt(jnp.mean(x * x, axis=-1, keepdims=True) + EPS)
    return xn * g * (1.0 + scale) + shift


def _sigmoid(x):
    return 1.0 / (1.0 + jnp.exp(-x))


def _group_rms(x, ones_bd, inv_n, gain):
    ms = _split_dot(x * x, ones_bd) * inv_n
    return x * lax.rsqrt(ms + EPS) * gain


def _lane_partner(x, half, period, first_end):
    n = x.shape[1]
    lane = lax.broadcasted_iota(jnp.int32, x.shape, 1)
    up = pltpu.roll(x, n - half, axis=1)
    down = pltpu.roll(x, half, axis=1)
    return jnp.where((lane & (period - 1)) < first_end, up, down)


def _rope(x, cos_t, sin_t, half, period, first_end):
    return x * cos_t + _lane_partner(x, half, period, first_end) * sin_t


def _ada_kernel(c_ref, w_ref, b_ref, o_ref):
    c = c_ref[...]
    s = c * _sigmoid(c)
    o_ref[0] = jnp.dot(s.astype(BF16), w_ref[0].astype(BF16), preferred_element_type=F32) + b_ref[0]


def _ada_call(cc, w_ada, b_ada):
    depth, dm, n = w_ada.shape
    tn = n // 4
    return pl.pallas_call(
        _ada_kernel,
        grid=(depth, n // tn),
        in_specs=[
            pl.BlockSpec(cc.shape, lambda l, j: (0, 0)),
            pl.BlockSpec((1, dm, tn), lambda l, j: (l, 0, j)),
            pl.BlockSpec((1, 1, tn), lambda l, j: (l, 0, j)),
        ],
        out_specs=pl.BlockSpec((1, cc.shape[0], tn), lambda l, j: (l, 0, j)),
        out_shape=jax.ShapeDtypeStruct((depth, cc.shape[0], n), F32),
        compiler_params=_cparams("parallel", "parallel"),
        name="ada_mod",
    )(cc, w_ada, b_ada.reshape(depth, 1, n))


class _Layout:
    def __init__(self, n_batch, n_ctx, n_lat):
        t = TOKEN_TILE
        assert n_ctx % t == 0 and n_lat % t == 0
        self.b, self.n_ctx, self.n_lat, self.n_tot = n_batch, n_ctx, n_lat, n_ctx + n_lat
        self.t = t
        self.ctx_tiles, self.lat_tiles, self.seq_tiles = n_ctx // t, n_lat // t, (n_ctx + n_lat) // t

    def rows(self, with_ctx):
        return self.b * (self.n_tot if with_ctx else self.n_lat)

    def n_tiles(self, with_ctx):
        return self.b * (self.seq_tiles if with_ctx else self.lat_tiles)

    def src_tile(self, with_ctx):
        if with_ctx:
            return lambda i: i
        return lambda i: (i // self.lat_tiles) * self.seq_tiles + i % self.lat_tiles + self.ctx_tiles


def _mod_row(modb_ref, modc_ref, r, n_rows, ctx_rows, tiles_per_seq):
    per_batch = modb_ref[0, r:r + 1, :]
    if not ctx_rows:
        return per_batch
    row = lax.broadcasted_iota(jnp.int32, (n_rows, 1), 0)
    first = pl.program_id(0) % tiles_per_seq == 0
    return jnp.where(jnp.logical_and(first, row < ctx_rows), modc_ref[0, r:r + 1, :], per_batch)


def _inproj_kernel(x_ref, modb_ref, modc_ref, g_ref, w_ref, da_ref, s5a_ref, s5b_ref, mla_ref, rw_ref,
                   *, ctx_rows, tiles_per_seq):
    x = x_ref[...]
    mrow = lambda r: _mod_row(modb_ref, modc_ref, r, x.shape[0], ctx_rows, tiles_per_seq)
    h = _modulate(x, g_ref[...], mrow(0), mrow(1))
    acc = jnp.dot(h.astype(BF16), w_ref[...], preferred_element_type=F32)
    da_ref[...] = acc[:, 0:_DA_W]
    s5a_ref[...] = acc[:, _DA_W:_DA_W + _S5_W // 2]
    s5b_ref[...] = acc[:, _DA_W + _S5_W // 2:_DA_W + _S5_W]
    mla_ref[...] = acc[:, _DA_W + _S5_W:_DA_W + _S5_W + _MLA_W]
    rw_ref[...] = acc[:, _DA_W + _S5_W + _MLA_W:_MIX_COLS]


def _seq_tile(lay):
    tm = PROJ_TILE if lay.n_tot % PROJ_TILE == 0 and lay.n_ctx <= PROJ_TILE else lay.t
    return tm, lay.n_tot // tm


def _inproj_call(lay, x_all, mod, g, w_mix):
    t = x_all.shape[0]
    tm, tps = _seq_tile(lay)
    widths = (_DA_W, _S5_W // 2, _S5_W // 2, _MLA_W, _RW_W)
    return pl.pallas_call(
        functools.partial(_inproj_kernel, ctx_rows=lay.n_ctx, tiles_per_seq=tps),
        grid=(t // tm,),
        in_specs=[
            pl.BlockSpec((tm, D_MODEL), lambda i: (i, 0)),
            pl.BlockSpec((1, 6, D_MODEL), lambda i: (i // tps, 0, 0)),
            pl.BlockSpec((1, 6, D_MODEL), lambda i: (lay.b, 0, 0)),
            _full(g), _full(w_mix),
        ],
        out_specs=[pl.BlockSpec((tm, w), lambda i: (i, 0)) for w in widths],
        out_shape=[jax.ShapeDtypeStruct((t, w), F32) for w in widths],
        compiler_params=_cparams("parallel"),
        name="in_proj",
    )(x_all, mod, mod, g, w_mix)


def _qkprep_kernel(da_ref, mla_ref, cda_ref, sda_ref, cml_ref, sml_ref, g32_ref, g64_ref,
                   gda_ref, gml_ref, cqg_ref, ckvg_ref, wuq_ref, wuk_ref, wuv_ref,
                   qd_ref, kd_ref, vd_ref, qm_ref, km_ref, vm_ref):
    g32 = g32_ref[...]
    g64 = g64_ref[...]
    cda, sda = cda_ref[...], sda_ref[...]
    q = _group_rms(da_ref[:, 0:MIX_W], g32, 1.0 / DA_DIM, gda_ref[0:1, :])
    qd_ref[...] = _rope(q, cda, sda, DA_DIM // 2, DA_DIM, DA_DIM // 2).astype(BF16)
    k = _group_rms(da_ref[:, MIX_W:2 * MIX_W], g32, 1.0 / DA_DIM, gda_ref[1:2, :])
    kd_ref[0] = _rope(k, cda, sda, DA_DIM // 2, DA_DIM, DA_DIM // 2).T.astype(BF16)
    vd_ref[...] = da_ref[:, 2 * MIX_W:3 * MIX_W].astype(BF16)

    cml, sml = cml_ref[...], sml_ref[...]
    cq = mla_ref[:, 0:256]
    cqn = cq * lax.rsqrt(jnp.sum(cq * cq, axis=-1, keepdims=True) * (1.0 / MLA_Q_RANK) + EPS) * cqg_ref[...]
    q = jnp.dot(cqn.astype(BF16), wuq_ref[...], preferred_element_type=F32)
    ckv = mla_ref[:, 256:384]
    ckvn = ckv * lax.rsqrt(jnp.mean(ckv * ckv, axis=-1, keepdims=True) + EPS) * ckvg_ref[...]
    ckvb = ckvn.astype(BF16)
    k = jnp.dot(ckvb, wuk_ref[...], preferred_element_type=F32) + mla_ref[:, 384:640]
    vm_ref[...] = jnp.dot(ckvb, wuv_ref[...], preferred_element_type=F32).astype(BF16)
    inv_n = 1.0 / (MLA_NOPE + MLA_ROPE)
    half = MLA_ROPE // 2
    q = _group_rms(q, g64, inv_n, gml_ref[0:1, :])
    qm_ref[...] = _rope(q, cml, sml, half, MLA_HEAD_PAD, MLA_NOPE + half).astype(BF16)
    k = _group_rms(k, g64, inv_n, gml_ref[1:2, :])
    km_ref[0] = _rope(k, cml, sml, half, MLA_HEAD_PAD, MLA_NOPE + half).T.astype(BF16)


def _qkprep_call(lay, da, mla, tabs, consts):
    t = da.shape[0]
    tm, st = _seq_tile(lay)
    row = pl.BlockSpec((tm, MIX_W), lambda i: (i, 0))
    key_t = pl.BlockSpec((1, MIX_W, tm), lambda i: (i // st, 0, i % st))
    in_specs = [pl.BlockSpec((tm, _DA_W), lambda i: (i, 0)), pl.BlockSpec((tm, _MLA_W), lambda i: (i, 0))]
    in_specs += [pl.BlockSpec((tm, MIX_W), lambda i: (i % st, 0)) for _ in tabs]
    in_specs += [_full(a) for a in consts]
    tok = jax.ShapeDtypeStruct((t, MIX_W), BF16)
    keys = jax.ShapeDtypeStruct((lay.b, MIX_W, lay.n_tot), BF16)
    return pl.pallas_call(
        _qkprep_kernel,
        grid=(t // tm,),
        in_specs=in_specs,
        out_specs=[row, key_t, row, row, key_t, row],
        out_shape=[tok, keys, tok, tok, keys, tok],
        compiler_params=_cparams("parallel"),
        name="qk_prep",
    )(da, mla, *tabs, *consts)


def _softmax_parts(s):
    p = jnp.exp2(s - jnp.max(s, axis=-1, keepdims=True))
    return p, 1.0 / jnp.sum(p, axis=-1, keepdims=True)


def _attn_heads(q, kt_ref, v_ref, nk, diff, lam):
    lane = lax.broadcasted_iota(jnp.int32, (q.shape[0], MIX_W), 1)
    v = v_ref[0, 0:nk, :]
    acc = jnp.zeros((q.shape[0], MIX_W), F32)
    dk = DA_DIM if diff else MLA_HEAD_PAD
    per_head = 2 if diff else 1

    def scores(h):
        return [jnp.dot(q[:, e * dk:(e + 1) * dk], kt_ref[0, e * dk:(e + 1) * dk, 0:nk], preferred_element_type=F32)
                for e in range(per_head * h, per_head * (h + 1))]

    ahead = scores(0)
    for h in range(DA_HEADS):
        s = ahead
        if h + 1 < DA_HEADS:
            ahead = scores(h + 1)
        if diff:
            p0, r0 = _softmax_parts(s[0])
            p1, r1 = _softmax_parts(s[1])
            o = jnp.dot((p0 - p1 * (r1 * lam / r0)).astype(BF16), v, preferred_element_type=F32) * r0
        else:
            p, r = _softmax_parts(s[0])
            o = jnp.dot(p.astype(BF16), v, preferred_element_type=F32) * r
        in_head = jnp.logical_and(lane >= h * DA_VDIM, lane < (h + 1) * DA_VDIM)
        acc = jnp.where(in_head, o, acc)
    return acc


def _attn_kernel(q_ref, kt_ref, v_ref, lam_ref, gain_ref, g64_ref, o_ref, *, diff, n_ctx, n_tot, ctx_tiles):
    q = q_ref[...]
    lam = lam_ref[...]

    def run(nk):
        o = _attn_heads(q, kt_ref, v_ref, nk, diff, lam)
        if diff:
            o = _group_rms(o, g64_ref[...], 1.0 / DA_VDIM, gain_ref[...])
        o_ref[...] = o.astype(BF16)

    if ctx_tiles:
        is_ctx = pl.program_id(1) < ctx_tiles
        pl.when(is_ctx)(lambda: run(n_ctx))
        pl.when(jnp.logical_not(is_ctx))(lambda: run(n_tot))
    else:
        run(n_tot)


def _attention(lay, q, kt, v, extra, diff, with_ctx, name):
    tq = lay.t
    tiles = lay.seq_tiles if with_ctx else lay.lat_tiles
    off = 0 if with_ctx else lay.ctx_tiles
    v3 = v.reshape(lay.b, lay.n_tot, MIX_W)
    kern = functools.partial(_attn_kernel, diff=diff, n_ctx=lay.n_ctx, n_tot=lay.n_tot,
                             ctx_tiles=lay.ctx_tiles if with_ctx else 0)
    return pl.pallas_call(
        kern,
        grid=(lay.b, tiles),
        in_specs=[
            pl.BlockSpec((tq, MIX_W), lambda b, j: (b * lay.seq_tiles + j + off, 0)),
            pl.BlockSpec((1, MIX_W, lay.n_tot), lambda b, j: (b, 0, 0)),
            pl.BlockSpec((1, lay.n_tot, MIX_W), lambda b, j: (b, 0, 0)),
        ] + [_full(a) for a in extra],
        out_specs=pl.BlockSpec((tq, MIX_W), lambda b, j: (b * tiles + j, 0)),
        out_shape=jax.ShapeDtypeStruct((lay.rows(with_ctx), MIX_W), BF16),
        compiler_params=_cparams("parallel", "parallel"),
        name=name,
    )(q, kt, v3, *extra)


def _chunk_rows(ua_ref, ub_ref):
    n = ua_ref.shape[0] // S5_CHUNK
    parts = []
    for s in range(S5_CHUNK):
        rows = pl.ds(s, n, stride=S5_CHUNK)
        parts += [ua_ref[rows, :], ub_ref[rows, :]]
    return jnp.concatenate(parts, axis=1).astype(BF16)


def _s5_proj_kernel(ua_ref, ub_ref, bre_ref, bim_ref, sre_ref, sim_ref):
    u = _chunk_rows(ua_ref, ub_ref)
    sre_ref[0] = jnp.dot(u, bre_ref[0, 0], preferred_element_type=F32)
    sim_ref[0] = jnp.dot(u, bim_ref[0, 0], preferred_element_type=F32)


def _s5_rec_kernel(sre_ref, sim_ref, are_ref, aim_ref, hre_ref, him_ref, *, n_batch, n_chunks, ctx_chunks):
    rev = pl.program_id(0) == 1
    ar, ai = are_ref[0, 0], aim_ref[0, 0]
    sre, sim, hre, him = sre_ref.at[0], sim_ref.at[0], hre_ref.at[0], him_ref.at[0]

    def step(i, carry):
        hr, hi = carry
        k_rev = jnp.where(i < ctx_chunks, ctx_chunks - 1 - i, n_chunks - 1 + ctx_chunks - i)
        k = jnp.where(rev, k_rev, i)
        rows = pl.ds(k, n_batch, stride=n_chunks)
        hre[rows, :] = hr
        him[rows, :] = hi
        return ar * hr - ai * hi + sre[rows, :], ar * hi + ai * hr + sim[rows, :]

    zero = jnp.zeros((n_batch, 128), F32)
    lax.fori_loop(0, n_chunks, step, (zero, zero), unroll=2)


def _s5_out_kernel(ua_ref, ub_ref, hre_ref, him_ref, m_ref, cre_ref, cim_ref, ya_ref, yb_ref):
    y = jnp.dot(_chunk_rows(ua_ref, ub_ref), m_ref[0, 0], preferred_element_type=F32)
    y = y + jnp.dot(hre_ref[0].astype(BF16), cre_ref[0, 0], preferred_element_type=F32)
    y = y + jnp.dot(him_ref[0].astype(BF16), cim_ref[0, 0], preferred_element_type=F32)
    n = y.shape[0]
    ya, yb = ya_ref.at[0], yb_ref.at[0]
    for s in range(S5_CHUNK):
        rows = pl.ds(s, n, stride=S5_CHUNK)
        ya[rows, :] = y[:, s * MIX_W:s * MIX_W + 128]
        yb[rows, :] = y[:, s * MIX_W + 128:(s + 1) * MIX_W]


def _s5_mats(lam_re, lam_im, log_dt, b_re, b_im, c_re, c_im):
    hp = lax.Precision.HIGHEST
    L, G, P, CH = S5_CHUNK, S5_GROUPS, S5_STATE, S5_CH
    lr, li = lam_re.astype(F32), lam_im.astype(F32)
    dt = jnp.exp(log_dt.astype(F32))[..., None]
    zr, zi = lr * dt, li * dt
    j = jnp.arange(L + 1, dtype=F32)[:, None, None, None]
    mag = jnp.exp(zr[None] * j)
    pw_re, pw_im = mag * jnp.cos(zi[None] * j), mag * jnp.sin(zi[None] * j)
    nr, ni = pw_re[1] - 1.0, pw_im[1]
    den = lr * lr + li * li
    cr, ci = (nr * lr + ni * li) / den, (ni * lr - nr * li) / den
    bre, bim = b_re.astype(F32), b_im.astype(F32)
    bb_re = cr[..., None] * bre - ci[..., None] * bim
    bb_im = cr[..., None] * bim + ci[..., None] * bre
    x_re = pw_re[..., None] * bb_re[None] - pw_im[..., None] * bb_im[None]
    x_im = pw_re[..., None] * bb_im[None] + pw_im[..., None] * bb_re[None]
    cre, cim = c_re.astype(F32), c_im.astype(F32)
    kern = (jnp.einsum('dgcp,jdgpe->dgjce', cre, x_re[:L], precision=hp)
            - jnp.einsum('dgcp,jdgpe->dgjce', cim, x_im[:L], precision=hp))
    def spread_mask(a, b):
        spread = jnp.asarray(np.tile(np.eye(b, dtype=np.float32), (1, G)))
        mask = jnp.asarray(np.kron(np.eye(G, dtype=np.float32), np.ones((a, b), np.float32)))
        return spread, mask

    kt = kern.transpose(0, 2, 1, 4, 3)
    xt_re, xt_im = x_re.transpose(1, 0, 2, 4, 3), x_im.transpose(1, 0, 2, 4, 3)
    pwt_re, pwt_im = pw_re.transpose(1, 0, 2, 3)[:, :, :, :, None], pw_im.transpose(1, 0, 2, 3)[:, :, :, :, None]
    cret, cimt = cre.transpose(0, 1, 3, 2)[:, None], cim.transpose(0, 1, 3, 2)[:, None]
    ca_re, ca_im = cret * pwt_re - cimt * pwt_im, -(cret * pwt_im + cimt * pwt_re)
    s_idx, t_idx = np.arange(L)[:, None], np.arange(L)[None, :]
    k_st, xb_re, xb_im, cq_re, cq_im = [], [], [], [], []
    for d in range(2):
        lag = (t_idx - s_idx) if d == 0 else (s_idx - t_idx)
        k_st.append(jnp.where(jnp.asarray(lag >= 0)[:, :, None, None, None], kt[d][np.clip(lag, 0, L - 1)], 0.0))
        pw = np.arange(L - 1, -1, -1) if d == 0 else np.arange(L)
        xb_re.append(xt_re[d][pw])
        xb_im.append(xt_im[d][pw])
        q = np.arange(1, L + 1) if d == 0 else np.arange(L, 0, -1)
        cq_re.append(ca_re[d][q])
        cq_im.append(ca_im[d][q])
    sp, mk = spread_mask(CH, CH)
    m = jnp.einsum('dstrb,bc->dsrtc', jnp.stack(k_st).reshape(2, L, L, G * CH, CH), sp, precision=hp) * mk[:, None, :]
    m = m.astype(BF16).reshape(2, L * G * CH, L * G * CH)
    sp, mk = spread_mask(CH, P)
    to_b = lambda x: (jnp.einsum('dsrb,bc->dsrc', jnp.stack(x).reshape(2, L, G * CH, P), sp, precision=hp) * mk
                      ).astype(BF16).reshape(2, L * G * CH, G * P)
    sp_c, mk_c = spread_mask(P, CH)
    to_c = lambda x: (jnp.einsum('dtrb,bc->drtc', jnp.stack(x).reshape(2, L, G * P, CH), sp_c, precision=hp)
                      * mk_c[:, None, :]).astype(BF16).reshape(2, G * P, L * G * CH)
    a_re, a_im = pw_re[L].reshape(2, 1, G * P), pw_im[L].reshape(2, 1, G * P)
    return m, to_b(xb_re), to_b(xb_im), to_c(cq_re), to_c(cq_im), a_re, a_im


def _s5_scan(lay, ua, ub, mats, layer):
    m, b_r, b_i, c_r, c_i, a_re, a_im = mats
    n_chunks = lay.n_tot // S5_CHUNK
    rows = lay.b * n_chunks
    tr = min(lay.t, rows)
    tok = tr * S5_CHUNK
    half = MIX_W // 2
    wspec = lambda a: pl.BlockSpec((1, 1) + a.shape[2:], lambda d, i: (layer, d, 0, 0))
    state = jax.ShapeDtypeStruct((2, rows, S5_STATE_W), F32)
    sblk = pl.BlockSpec((1, tr, S5_STATE_W), lambda d, i: (d, i, 0))
    ublk = pl.BlockSpec((tok, half), lambda d, i: (i, 0))
    s_re, s_im = pl.pallas_call(
        _s5_proj_kernel,
        grid=(2, rows // tr),
        in_specs=[ublk, ublk, wspec(b_r), wspec(b_i)],
        out_specs=[sblk, sblk],
        out_shape=[state, state],
        compiler_params=_cparams("parallel", "parallel"),
        name="s5_proj",
    )(ua, ub, b_r, b_i)
    col = pl.BlockSpec((1, rows, 128), lambda d, j: (d, 0, j))
    acol = pl.BlockSpec((1, 1, 1, 128), lambda d, j: (layer, d, 0, j))
    h_re, h_im = pl.pallas_call(
        functools.partial(_s5_rec_kernel, n_batch=lay.b, n_chunks=n_chunks, ctx_chunks=lay.n_ctx // S5_CHUNK),
        grid=(2, S5_STATE_W // 128),
        in_specs=[col, col, acol, acol],
        out_specs=[col, col],
        out_shape=[state, state],
        compiler_params=_cparams("parallel", "parallel"),
        name="s5_rec",
    )(s_re, s_im, a_re, a_im)
    yblk = pl.BlockSpec((1, tok, half), lambda d, i: (d, i, 0))
    yshape = jax.ShapeDtypeStruct((2, lay.b * lay.n_tot, half), F32)
    return pl.pallas_call(
        _s5_out_kernel,
        grid=(2, rows // tr),
        in_specs=[ublk, ublk, sblk, sblk, wspec(m), wspec(c_r), wspec(c_i)],
        out_specs=[yblk, yblk],
        out_shape=[yshape, yshape],
        compiler_params=_cparams("parallel", "parallel"),
        name="s5_out",
    )(ua, ub, h_re, h_im, m, c_r, c_i)


def _s5_glu_kernel(ua_ref, ub_ref, ya_ref, yb_ref, d_ref, w_ref, b_ref, o_ref):
    u = jnp.concatenate([ua_ref[...], ub_ref[...]], axis=1)
    y = d_ref[...] * u + jnp.concatenate([ya_ref[0] + ya_ref[1], yb_ref[0] + yb_ref[1]], axis=1)
    z = 0.5 * y * (1.0 + jnp.tanh(math.sqrt(2.0 / math.pi) * (y + 0.044715 * (y * y * y))))
    gate = _sigmoid(jnp.dot(z.astype(BF16), w_ref[...], preferred_element_type=F32) + b_ref[...])
    o_ref[...] = (z * gate).astype(BF16)


def _s5_glu_call(ua, ub, ya, yb, d, w, bias, tm):
    t, half = ua.shape
    urow = pl.BlockSpec((tm, half), lambda i: (i, 0))
    yrow = pl.BlockSpec((2, tm, half), lambda i: (0, i, 0))
    return pl.pallas_call(
        _s5_glu_kernel,
        grid=(t // tm,),
        in_specs=[urow, urow, yrow, yrow, _full(d), _full(w), _full(bias)],
        out_specs=pl.BlockSpec((tm, MIX_W), lambda i: (i, 0)),
        out_shape=jax.ShapeDtypeStruct((t, MIX_W), BF16),
        compiler_params=_cparams("parallel"),
        name="s5_glu",
    )(ua, ub, ya, yb, d, w, bias)


def _rw_pre_kernel(x_ref, prev_ref, next_ref, mu_ref, g64_ref, kk_g_ref, ka_ref, rk_ref,
                   w0_ref, w1_ref, w2_ref, a0_ref, a1_ref, a2_ref, g1_ref, g2_ref,
                   r_ref, v_ref, kk_ref, lw_ref, kka_ref, km_ref, bon_ref, gate_ref,
                   *, seq_tiles, n_ctx, n_tot):
    x = x_ref[...]
    n = x.shape[0]
    row = lax.broadcasted_iota(jnp.int32, (n, 1), 0)
    pos = (pl.program_id(0) % seq_tiles) * n + row
    left = jnp.where(row == 0, prev_ref[0, 7:8, :], pltpu.roll(x, 1, axis=0))
    left = jnp.where(jnp.logical_or(pos == 0, pos == n_ctx), 0.0, left)
    right = jnp.where(row == n - 1, next_ref[0, 0:1, :], pltpu.roll(x, n - 1, axis=0))
    right = jnp.where(jnp.logical_or(pos == n_ctx - 1, pos == n_tot - 1), 0.0, right)
    x = x + (0.5 * (left + right) - x) * mu_ref[...]
    r, k, v, xd = (x[:, i * MIX_W:(i + 1) * MIX_W] for i in range(4))
    g64 = g64_ref[...]
    kscaled = k * kk_g_ref[...]
    kk = kscaled / jnp.maximum(jnp.sqrt(_split_dot(kscaled * kscaled, g64)), 1e-12)
    xdb = xd.astype(BF16)
    r_ref[...] = r
    v_ref[...] = v
    kk_ref[...] = kk
    km_sum = None
    for d in range(2):
        lo = jnp.tanh(jnp.dot(xdb, w1_ref[d], preferred_element_type=F32))
        w_raw = w0_ref[d] + jnp.dot(lo.astype(BF16), w2_ref[d], preferred_element_type=F32)
        lw_ref[d] = -_sigmoid(w_raw) * math.exp(-0.5)
        ar = jnp.dot(xdb, a1_ref[d], preferred_element_type=F32)
        a = _sigmoid(a0_ref[d] + jnp.dot(ar.astype(BF16), a2_ref[d], preferred_element_type=F32))
        km = k * (1.0 + (a - 1.0) * ka_ref[...])
        kka_ref[d] = kk * a
        km_ref[d] = km
        km_sum = km if km_sum is None else km_sum + km
    bon_ref[...] = _split_dot(r * km_sum * rk_ref[...], g64) * v
    gr = _sigmoid(jnp.dot(xdb, g1_ref[...], preferred_element_type=F32))
    gate_ref[...] = jnp.dot(gr.astype(BF16), g2_ref[...], preferred_element_type=F32)


def _rw_pre_call(lay, rw, consts):
    t = rw.shape[0]
    tr, seq_tiles = _seq_tile(lay)
    nt = t // tr
    g8 = tr // 8
    rw8 = rw.reshape(t // 8, 8, _RW_W)
    row = pl.BlockSpec((tr, MIX_W), lambda i: (i, 0))
    row2 = pl.BlockSpec((2, tr, MIX_W), lambda i: (0, i, 0))
    sd = jax.ShapeDtypeStruct((t, MIX_W), F32)
    sd2 = jax.ShapeDtypeStruct((2, t, MIX_W), F32)
    return pl.pallas_call(
        functools.partial(_rw_pre_kernel, seq_tiles=seq_tiles, n_ctx=lay.n_ctx, n_tot=lay.n_tot),
        grid=(nt,),
        in_specs=[pl.BlockSpec((tr, _RW_W), lambda i: (i, 0)),
                  pl.BlockSpec((1, 8, _RW_W), lambda i: (jnp.maximum(i * g8 - 1, 0), 0, 0)),
                  pl.BlockSpec((1, 8, _RW_W), lambda i: (jnp.minimum((i + 1) * g8, t // 8 - 1), 0, 0))]
                 + [_full(a) for a in consts],
        out_specs=[row, row, row, row2, row2, row2, row, row],
        out_shape=[sd, sd, sd, sd2, sd2, sd2, sd, sd],
        compiler_params=_cparams("parallel"),
        name="rwkv_pre",
    )(rw, rw8, rw8, *consts)


def _head_masks(shape, lane_axis, seg):
    lane = lax.broadcasted_iota(jnp.int32, shape, lane_axis)
    return [jnp.logical_and(lane >= h * seg, lane < (h + 1) * seg) for h in range(RW_HEADS)]


def _rw_prep_kernel(*refs, rev):
    tiles = [_rw_prep_tile(sub, *refs, rev=rev) for sub in range(RW_PREP_TILES)]
    while tiles:
        tiles = [t for t in tiles if next(t, None) is not None]


def _rw_prep_tile(sub, r_ref, kk_ref, v_ref, lw_ref, ka_ref, km_ref, perm_ref, permt_ref, g_ref, eye_ref,
                  br_ref, ck_ref, uvt_ref, y0_ref, pc_ref, *, rev):
    C, NC = RW_CHUNK, RW_TILE // RW_CHUNK
    perm, permt, g64, eye4 = perm_ref[...], permt_ref[...], g_ref[...], eye_ref[...]
    tok = slice(sub * RW_TILE, (sub + 1) * RW_TILE)
    nat = jnp.concatenate([r_ref[0, tok, :], kk_ref[0, tok, :], v_ref[0, tok, :],
                           lw_ref[0, 0, tok, :], ka_ref[0, 0, tok, :], km_ref[0, 0, tok, :]], axis=1)
    hi = nat.astype(BF16)
    lo = (nat - hi.astype(F32)).astype(BF16)
    pm = jnp.dot(perm, jnp.concatenate([hi, lo], axis=0), preferred_element_type=F32)
    r, kk, v, lw, ka, km = (pm[:, i * MIX_W:(i + 1) * MIX_W] for i in range(6))
    slab = lambda x, j: x[j * NC:(j + 1) * NC, :]
    order = list(range(C))[::-1] if rev else list(range(C))
    pos = {j: i for i, j in enumerate(order)}
    cum, run = {}, None
    for j in order:
        run = slab(lw, j) if run is None else run + slab(lw, j)
        cum[j] = run
    tot = run
    yield True
    bh, ch, kh, rh, cp, kp, vv = {}, {}, {}, {}, {}, {}, {}
    for j in range(C):
        e_inv, e_end = jnp.exp(-cum[j]), jnp.exp(tot - cum[j])
        bh[j] = -slab(kk, j) * jnp.exp(cum[j] - slab(lw, j))
        ch[j], kh[j] = slab(ka, j) * e_inv, slab(km, j) * e_inv
        rh[j] = slab(r, j) * jnp.exp(cum[j])
        cp[j], kp[j] = slab(ka, j) * e_end, slab(km, j) * e_end
        vv[j] = slab(v, j)
    strict = [(t, s) for t in order for s in order if pos[s] < pos[t]]
    incl = [(t, s) for t in order for s in order if pos[s] <= pos[t]]
    def head_dots(lhs, rhs, pairs):
        prods = jnp.concatenate([lhs[t] * rhs[s] for t, s in pairs], axis=0).astype(BF16)
        gram = jnp.dot(prods, g64, preferred_element_type=F32)
        return {p: gram[i * NC:(i + 1) * NC, :] for i, p in enumerate(pairs)}

    yield True
    acb = head_dots(bh, ch, strict)
    yield True
    akb = head_dots(bh, kh, strict)
    yield True
    mcr = head_dots(rh, ch, incl)
    yield True
    mkr = head_dots(rh, kh, incl)
    yield True
    bt, u0 = {}, {}
    for t in order:
        b_acc, u_acc = bh[t], jnp.zeros_like(bh[t])
        for s in order:
            if pos[s] < pos[t]:
                b_acc = b_acc + acb[(t, s)] * bt[s]
                u_acc = u_acc + akb[(t, s)] * vv[s] + acb[(t, s)] * u0[s]
        bt[t], u0[t] = b_acc, u_acc
        yield True
    rt, y0 = {}, {}
    for t in order:
        r_acc, y_acc = rh[t], jnp.zeros_like(rh[t])
        for s in order:
            if pos[s] <= pos[t]:
                r_acc = r_acc + mcr[(t, s)] * bt[s]
                y_acc = y_acc + mcr[(t, s)] * u0[s] + mkr[(t, s)] * vv[s]
        rt[t], y0[t] = r_acc, y_acc
        yield True
    stackp = lambda dct: jnp.concatenate([dct[j] for j in range(C)], axis=0)
    b16 = lambda x: x.astype(BF16)
    y0p, u0p = stackp(y0), stackp(u0)
    y0h, u0h = b16(y0p), b16(u0p)
    cat = jnp.concatenate([b16(stackp(bt)), b16(stackp(rt)), b16(stackp(cp)), b16(stackp(kp)),
                           y0h, b16(y0p - y0h.astype(F32)), u0h, b16(u0p - u0h.astype(F32)), b16(stackp(vv))], axis=1)
    natural = jnp.dot(permt, cat, preferred_element_type=F32)
    seg = lambda i: natural[:, i * MIX_W:(i + 1) * MIX_W]
    yield True
    btn, rtn, cpn, kpn = b16(seg(0)), b16(seg(1)), b16(seg(2)), b16(seg(3))
    y0_ref[0, tok, :] = seg(4) + seg(5)
    u0h_n, u0l_n, vn = b16(seg(6)), b16(seg(7)), b16(seg(8))
    hm = _head_masks((C, MIX_W), 1, RW_DIM)
    zero = jnp.zeros((C, MIX_W), BF16)
    zh, zl = [], []
    for c in range(NC):
        rows = slice(c * C, (c + 1) * C)
        br_ref[0, sub * NC + c, 0:C, :] = btn[rows]
        br_ref[0, sub * NC + c, C:2 * C, :] = rtn[rows]
        ck_ref[0, sub * NC + c, 0:C, :] = cpn[rows]
        ck_ref[0, sub * NC + c, C:2 * C, :] = kpn[rows]
        for h in range(RW_HEADS):
            zh += [jnp.where(hm[h], u0h_n[rows], zero), jnp.where(hm[h], vn[rows], zero)]
            zl += [jnp.where(hm[h], u0l_n[rows], zero), zero]
    z = jnp.concatenate([jnp.concatenate(zh, axis=0), jnp.concatenate(zl, axis=0)], axis=1)
    uvt = lax.dot_general(eye4, z, _NT, preferred_element_type=F32)
    for c in range(NC):
        uvt_ref[0, sub * NC + c] = uvt[:, c * 2 * C * RW_HEADS:(c + 1) * 2 * C * RW_HEADS]
    pc_ref[0, sub * NC:(sub + 1) * NC, :] = jnp.exp(tot)


def _rw_prep_call(lay, shared, perdir, consts, rev):
    b, n_tot, tt = lay.b, lay.n_tot, RW_TILE * RW_PREP_TILES
    assert n_tot % tt == 0
    nck = n_tot // RW_CHUNK
    cpt = tt // RW_CHUNK
    d = 1 if rev else 0
    sh = [a.reshape(b, n_tot, MIX_W) for a in shared]
    pd = [a.reshape(2, b, n_tot, MIX_W) for a in perdir]
    tok = pl.BlockSpec((1, tt, MIX_W), lambda i, j: (i, j, 0))
    tok_d = pl.BlockSpec((1, 1, tt, MIX_W), lambda i, j: (d, i, j, 0))
    rows32 = pl.BlockSpec((1, cpt, 2 * RW_CHUNK, MIX_W), lambda i, j: (i, j, 0, 0))
    return pl.pallas_call(
        functools.partial(_rw_prep_kernel, rev=rev),
        grid=(b, n_tot // tt),
        in_specs=[tok] * 3 + [tok_d] * 3 + [_full(a) for a in consts],
        out_specs=[rows32, rows32,
                   pl.BlockSpec((1, cpt, RW_DIM, 2 * RW_CHUNK * RW_HEADS), lambda i, j: (i, j, 0, 0)),
                   tok,
                   pl.BlockSpec((1, cpt, MIX_W), lambda i, j: (i, j, 0))],
        out_shape=[jax.ShapeDtypeStruct((b, nck, 2 * RW_CHUNK, MIX_W), BF16),
                   jax.ShapeDtypeStruct((b, nck, 2 * RW_CHUNK, MIX_W), BF16),
                   jax.ShapeDtypeStruct((b, nck, RW_DIM, 2 * RW_CHUNK * RW_HEADS), F32),
                   jax.ShapeDtypeStruct((b, n_tot, MIX_W), F32),
                   jax.ShapeDtypeStruct((b, nck, MIX_W), F32)],
        compiler_params=_cparams("parallel", "parallel"),
        name="rwkv_prep_rev" if rev else "rwkv_prep_fwd",
    )(*sh, *pd, *consts)


def _rw_scan_kernel(brf, ckf, uvtf, pcf, brr, ckr, uvtr, pcr, ytf_ref, ytr_ref, s_scr, *, n_batch):
    @pl.when(pl.program_id(0) == 0)
    def _():
        s_scr[...] = jnp.zeros_like(s_scr)

    cpt = RW_TILE // RW_CHUNK
    hm = _head_masks((2 * RW_CHUNK, MIX_W), 1, RW_DIM)
    lane = lax.broadcasted_iota(jnp.int32, (RW_DIM, 2 * RW_CHUNK * RW_HEADS), 1)
    is_u = (lane & (2 * RW_CHUNK - 1)) < RW_CHUNK
    per_head = lambda x: jnp.concatenate([jnp.where(m, x, jnp.zeros_like(x)) for m in hm], axis=0)

    def refs_of(p, c):
        d, b = divmod(p, n_batch)
        refs = (brf, ckf, uvtf, pcf, ytf_ref) if d == 0 else (brr, ckr, uvtr, pcr, ytr_ref)
        return refs, b, (c if d == 0 else cpt - 1 - c)

    def step(c, carry):
        lhs = []
        for p in range(2 * n_batch):
            (br_ref, _, uvt_ref, _, yt_ref), b, cc = refs_of(p, c)
            w = lax.dot_general(s_scr[p].astype(BF16), per_head(br_ref[b, cc]), _NT, preferred_element_type=F32)
            yt_ref[b, cc] = w
            uvt = uvt_ref[b, cc]
            lhs.append(jnp.where(is_u, w + uvt, uvt).astype(BF16))
        for p in range(2 * n_batch):
            (_, ck_ref, _, pc_ref, _), b, cc = refs_of(p, c)
            s_scr[p] = (s_scr[p] * pc_ref[b, pl.ds(cc, 1), :]
                        + jnp.dot(lhs[p], per_head(ck_ref[b, cc]), preferred_element_type=F32))
        return carry

    lax.fori_loop(0, cpt, step, 0)


def _rw_scan_call(lay, fwd, rev):
    b, n_tot, tt = lay.b, lay.n_tot, RW_TILE
    assert lay.n_ctx % tt == 0 and lay.n_lat % tt == 0
    nt, ct = n_tot // tt, lay.n_ctx // tt
    cpt = tt // RW_CHUNK
    rev_tile = lambda i: jnp.where(i < ct, ct - 1 - i, nt - 1 + ct - i)

    def specs(tile):
        return [pl.BlockSpec((b, cpt, 2 * RW_CHUNK, MIX_W), lambda i: (0, tile(i), 0, 0)),
                pl.BlockSpec((b, cpt, 2 * RW_CHUNK, MIX_W), lambda i: (0, tile(i), 0, 0)),
                pl.BlockSpec((b, cpt, RW_DIM, 2 * RW_CHUNK * RW_HEADS), lambda i: (0, tile(i), 0, 0)),
                pl.BlockSpec((b, cpt, MIX_W), lambda i: (0, tile(i), 0))]

    ident = lambda i: i
    yt = jax.ShapeDtypeStruct((b, n_tot // RW_CHUNK, RW_DIM, 2 * RW_CHUNK * RW_HEADS), F32)
    return pl.pallas_call(
        functools.partial(_rw_scan_kernel, n_batch=b),
        grid=(nt,),
        in_specs=specs(ident) + specs(rev_tile),
        out_specs=[specs(ident)[2], specs(rev_tile)[2]],
        out_shape=[yt, yt],
        scratch_shapes=[pltpu.VMEM((2 * b, RW_DIM, MIX_W), F32)],
        compiler_params=_cparams("arbitrary"),
        name="rwkv_scan",
    )(*fwd, *rev)


def _rw_fin_kernel(ytf_ref, ytr_ref, y0f_ref, y0r_ref, bon_ref, gate_ref, asel_ref, g64_ref, lng_ref, lnb_ref, o_ref):
    cpt = RW_TILE // RW_CHUNK
    asel = asel_ref[...]
    width = cpt * 2 * RW_CHUNK * RW_HEADS
    lane = lax.broadcasted_iota(jnp.int32, (RW_DIM, width), 1)
    lane_head = jnp.bitwise_and(jnp.right_shift(lane, 5), RW_HEADS - 1)

    def base(yt_ref, sub):
        yt = jnp.concatenate([yt_ref[0, sub * cpt + c] for c in range(cpt)], axis=1).astype(BF16)
        rows = jnp.concatenate([jnp.where(lane_head == h, yt, jnp.zeros_like(yt)) for h in range(RW_HEADS)], axis=0)
        return lax.dot_general(asel, rows, _NT, preferred_element_type=F32)

    g64 = g64_ref[...]
    bases = [(base(ytf_ref, sub), base(ytr_ref, sub)) for sub in range(RW_PREP_TILES)]
    for sub, (yb_f, yb_r) in enumerate(bases):
        tok = slice(sub * RW_TILE, (sub + 1) * RW_TILE)
        y = yb_f + y0f_ref[0, tok, :] + yb_r + y0r_ref[0, tok, :]
        mean = _split_dot(y, g64) * (1.0 / RW_DIM)
        c = y - mean
        var = _split_dot(c * c, g64) * (1.0 / RW_DIM)
        out = c * lax.rsqrt(var + RW_LN_EPS) * lng_ref[...] + lnb_ref[...] + bon_ref[tok, :]
        o_ref[tok, :] = (out * gate_ref[tok, :]).astype(BF16)


def _rw_fin_call(lay, ytf, ytr, y0f, y0r, bon, gate, consts):
    b, n_tot, tt = lay.b, lay.n_tot, RW_TILE * RW_PREP_TILES
    nt = n_tot // tt
    cpt = tt // RW_CHUNK
    ytb = pl.BlockSpec((1, cpt, RW_DIM, 2 * RW_CHUNK * RW_HEADS), lambda i, j: (i, j, 0, 0))
    y0b = pl.BlockSpec((1, tt, MIX_W), lambda i, j: (i, j, 0))
    row = pl.BlockSpec((tt, MIX_W), lambda i, j: (i * nt + j, 0))
    return pl.pallas_call(
        _rw_fin_kernel,
        grid=(b, nt),
        in_specs=[ytb, ytb, y0b, y0b, row, row] + [_full(a) for a in consts],
        out_specs=row,
        out_shape=jax.ShapeDtypeStruct((b * n_tot, MIX_W), BF16),
        compiler_params=_cparams("parallel", "parallel"),
        name="rwkv_finish",
    )(ytf, ytr, y0f, y0r, bon, gate, *consts)


def _rw_constants():
    c, nc = RW_CHUNK, RW_TILE // RW_CHUNK
    perm = np.zeros((RW_TILE, RW_TILE), np.float32)
    for ci in range(nc):
        for j in range(c):
            perm[j * nc + ci, ci * c + j] = 1.0
    lane = np.arange(MIX_W) % RW_DIM
    eye4 = (lane[None, :] == np.arange(RW_DIM)[:, None]).astype(np.float32)
    lanes = np.arange(nc * 2 * c * RW_HEADS)
    lane_chunk, lane_tok = lanes // (2 * c * RW_HEADS), lanes % (2 * c)
    t = np.arange(RW_TILE)
    asel = ((lane_chunk[None, :] == (t // c)[:, None]) & (lane_tok[None, :] == (c + t % c)[:, None])).astype(np.float32)
    as16 = lambda a: jnp.asarray(a, BF16)
    twice = lambda a: np.concatenate([a, a], axis=1)
    return as16(twice(perm)), as16(perm.T), as16(twice(eye4)), as16(asel)


def _rwkv_branch(lay, rw, pre_consts, g64, lng, lnb):
    r_, v_, kk_, lw_, kka_, km_, bon, gate = _rw_pre_call(lay, rw, pre_consts)
    perm, permt, eye4, asel = _rw_constants()
    prep_consts = (perm, permt, g64, eye4)
    fwd = _rw_prep_call(lay, (r_, kk_, v_), (lw_, kka_, km_), prep_consts, False)
    rev = _rw_prep_call(lay, (r_, kk_, v_), (lw_, kka_, km_), prep_consts, True)
    pick = lambda o: (o[0], o[1], o[2], o[4])
    ytf, ytr = _rw_scan_call(lay, pick(fwd), pick(rev))
    return _rw_fin_call(lay, ytf, ytr, fwd[3], rev[3], bon, gate, (asel, g64, lng, lnb))


def _merge_kernel(x_ref, modb_ref, modc_ref, g_ref, wg_ref, ya_ref, yb_ref, yc_ref, yd_ref, wb_ref, wo_ref, o_ref,
                  *, ctx_rows, tiles_per_seq):
    x = x_ref[...]
    mrow = lambda r: _mod_row(modb_ref, modc_ref, r, x.shape[0], ctx_rows, tiles_per_seq)
    h = _modulate(x, g_ref[...], mrow(0), mrow(1)).astype(BF16)
    merged = None
    for i, y_ref in enumerate((ya_ref, yb_ref, yc_ref, yd_ref)):
        gate = _sigmoid(jnp.dot(h, wg_ref[:, i * D_MODEL:(i + 1) * D_MODEL], preferred_element_type=F32))
        term = gate * jnp.dot(y_ref[...], wb_ref[i], preferred_element_type=F32)
        merged = term if merged is None else merged + term
    out = jnp.dot(merged.astype(BF16), wo_ref[...], preferred_element_type=F32)
    o_ref[...] = x + mrow(2) * out


def _merge_call(lay, with_ctx, x_all, mod, g, w_gate, ya, yb, yc, yd, w_branch, w_out):
    if with_ctx:
        tm, tps = _seq_tile(lay)
        src, n_tiles, ctx_rows = (lambda i: i), lay.b * tps, lay.n_ctx
    else:
        tm, tps = lay.t, lay.lat_tiles
        src, n_tiles, ctx_rows = lay.src_tile(False), lay.n_tiles(False), 0
    full_row = lambda w: pl.BlockSpec((tm, w), lambda i: (src(i), 0))
    out_row = lambda w: pl.BlockSpec((tm, w), lambda i: (i, 0))
    return pl.pallas_call(
        functools.partial(_merge_kernel, ctx_rows=ctx_rows, tiles_per_seq=tps),
        grid=(n_tiles,),
        in_specs=[full_row(D_MODEL), pl.BlockSpec((1, 6, D_MODEL), lambda i: (i // tps, 0, 0)),
                  pl.BlockSpec((1, 6, D_MODEL), lambda i: (lay.b, 0, 0)), _full(g), _full(w_gate),
                  out_row(MIX_W), full_row(MIX_W), out_row(MIX_W), full_row(MIX_W), _full(w_branch), _full(w_out)],
        out_specs=out_row(D_MODEL),
        out_shape=jax.ShapeDtypeStruct((lay.rows(with_ctx), D_MODEL), F32),
        compiler_params=_cparams("parallel"),
        name="merge_out",
    )(x_all, mod, mod, g, w_gate, ya, yb, yc, yd, w_branch, w_out)


def _router_kernel(x_ref, modb_ref, modc_ref, g_ref, wh_ref, wl_ref, bias_ref, f_ref, comb_ref, gid_ref,
                   *, ctx_rows, tiles_per_seq):
    x = x_ref[...]
    mrow = lambda r: _mod_row(modb_ref, modc_ref, r, x.shape[0], ctx_rows, tiles_per_seq)
    f = _modulate(x, g_ref[...], mrow(3), mrow(4))
    fh = f.astype(BF16)
    f_ref[...] = fh
    fl = (f - fh.astype(F32)).astype(BF16)
    nt = (((1,), (1,)), ((), ()))
    wh, wl = wh_ref[...], wl_ref[...]
    logits = (lax.dot_general(wh, fh, nt, preferred_element_type=F32)
              + lax.dot_general(wh, fl, nt, preferred_element_type=F32)
              + lax.dot_general(wl, fh, nt, preferred_element_type=F32))
    scores = _sigmoid(logits)
    biased = scores + bias_ref[...]
    sc = [scores[e:e + 1, :] for e in range(N_EXPERTS)]
    bi = [biased[e:e + 1, :] for e in range(N_EXPERTS)]
    group_score = []
    for g in range(N_GROUPS):
        a, b, c, d = bi[4 * g:4 * g + 4]
        m1, n1, m2, n2 = jnp.maximum(a, b), jnp.minimum(a, b), jnp.maximum(c, d), jnp.minimum(c, d)
        group_score.append(jnp.maximum(m1, m2) + jnp.maximum(jnp.minimum(m1, m2), jnp.maximum(n1, n2)))

    def first_argmax(vals):
        top = functools.reduce(jnp.maximum, vals)
        seen, hot = None, []
        for v in vals:
            h = v == top
            if seen is not None:
                h = jnp.logical_and(h, jnp.logical_not(seen))
            seen = h if seen is None else jnp.logical_or(seen, h)
            hot.append(h)
        return hot

    in_group = first_argmax(group_score)
    masked = [jnp.where(in_group[e // EXPERTS_PER_GROUP], bi[e], -jnp.inf) for e in range(N_EXPERTS)]
    hot1 = first_argmax(masked)
    hot2 = first_argmax([jnp.where(h, -jnp.inf, v) for h, v in zip(hot1, masked)])
    w1 = functools.reduce(jnp.add, [jnp.where(h, s, 0.0) for h, s in zip(hot1, sc)])
    w2 = functools.reduce(jnp.add, [jnp.where(h, s, 0.0) for h, s in zip(hot2, sc)])
    inv_tot = 1.0 / (w1 + w2)
    for e in range(N_EXPERTS):
        comb_ref[e:e + 1, :] = (jnp.where(hot1[e], w1, 0.0) + jnp.where(hot2[e], w2, 0.0)) * inv_tot
    gid_ref[...] = functools.reduce(jnp.add, [jnp.where(in_group[g], g, 0) for g in range(1, N_GROUPS)])


def _router_call(lay, with_ctx, x, mod, g, wh, wl, bias):
    t = x.shape[0]
    if with_ctx:
        (tm, tps), ctx_rows = _seq_tile(lay), lay.n_ctx
    else:
        pair = 2 if lay.lat_tiles % 2 == 0 else 1
        tm, tps, ctx_rows = pair * lay.t, lay.lat_tiles // pair, 0
    return pl.pallas_call(
        functools.partial(_router_kernel, ctx_rows=ctx_rows, tiles_per_seq=tps),
        grid=(t // tm,),
        in_specs=[pl.BlockSpec((tm, D_MODEL), lambda i: (i, 0)),
                  pl.BlockSpec((1, 6, D_MODEL), lambda i: (i // tps, 0, 0)),
                  pl.BlockSpec((1, 6, D_MODEL), lambda i: (lay.b, 0, 0)), _full(g), _full(wh), _full(wl), _full(bias)],
        out_specs=[pl.BlockSpec((tm, D_MODEL), lambda i: (i, 0)), pl.BlockSpec((N_EXPERTS, tm), lambda i: (0, i)),
                   pl.BlockSpec((1, tm), lambda i: (0, i))],
        out_shape=[jax.ShapeDtypeStruct((t, D_MODEL), BF16), jax.ShapeDtypeStruct((N_EXPERTS, t), F32),
                   jax.ShapeDtypeStruct((1, t), jnp.int32)],
        compiler_params=_cparams("parallel"),
        name="moe_router",
    )(x, mod, mod, g, wh, wl, bias)


def _moe_plan(gid, n_tiles, tm):
    g = gid.reshape(n_tiles, tm)
    onehot = (g[..., None] == jnp.arange(N_GROUPS, dtype=jnp.int32)).astype(jnp.int32)
    rank = jnp.cumsum(onehot, axis=1) - onehot
    counts = jnp.sum(onehot, axis=1)
    padded = (counts + 15) // 16 * 16
    offs = jnp.cumsum(padded, axis=1) - padded
    pos = jnp.sum(onehot * (offs[:, None, :] + rank), axis=-1)
    n_over = (jnp.maximum(padded - MOE_BLOCK, 0) + MOE_OVER - 1) // MOE_OVER
    return pos.astype(jnp.int32), offs.astype(jnp.int32), n_over.astype(jnp.int32)


def _moe_kernel(offs_ref, nover_ref, f_ref, posr_ref, posc_ref, comb_ref, wg_ref, wu_ref, wd_ref, x_ref, modb_ref,
                modc_ref, o_ref, xs_scr, cs_scr, ys_scr, *, ctx_rows, tiles_per_seq):
    i, e = pl.program_id(0), pl.program_id(1)
    n_slots, tm = xs_scr.shape[0], f_ref.shape[0]
    n_live = min(n_slots, -(-(tm + 16 * N_GROUPS) // 256) * 256)

    @pl.when(e == 0)
    def _():
        slot = lax.broadcasted_iota(jnp.int32, (n_live, tm), 0)
        place = (slot == posr_ref[0]).astype(BF16)
        xs_scr[0:n_live, :] = jnp.dot(place, f_ref[...], preferred_element_type=F32).astype(BF16)
        xs_scr[n_live:n_slots, :] = jnp.zeros((n_slots - n_live, D_MODEL), BF16)
        cs_scr[0:n_live, :] = _split_dot_rhs(place, comb_ref[...])
        cs_scr[n_live:n_slots, :] = jnp.zeros((n_slots - n_live, N_EXPERTS), F32)
        ys_scr[...] = jnp.zeros_like(ys_scr)

    grp = lax.shift_right_logical(e, 2)
    start = offs_ref[i, grp]
    lane = lax.broadcasted_iota(jnp.int32, (1, N_EXPERTS), 1)

    def run(rows):
        xb = xs_scr[rows, :]
        gate = jnp.dot(xb, wg_ref[0], preferred_element_type=F32)
        up = jnp.dot(xb, wu_ref[0], preferred_element_type=F32)
        act = (gate * _sigmoid(gate) * up).astype(BF16)
        down = jnp.dot(act, wd_ref[0], preferred_element_type=F32)
        c_e = jnp.sum(jnp.where(lane == e, cs_scr[rows, :], 0.0), axis=1, keepdims=True)
        ys_scr[rows, :] += c_e * down

    run(pl.ds(pl.multiple_of(start, 16), MOE_BLOCK))

    def overflow(k, carry):
        run(pl.ds(pl.multiple_of(start + MOE_BLOCK + k * MOE_OVER, 16), MOE_OVER))
        return carry

    lax.fori_loop(0, nover_ref[i, grp], overflow, 0)

    @pl.when(e == N_EXPERTS - 1)
    def _():
        slot = lax.broadcasted_iota(jnp.int32, (tm, n_live), 1)
        fetch = (slot == posc_ref[...]).astype(BF16)
        y = jnp.dot(fetch, ys_scr[0:n_live, :].astype(BF16), preferred_element_type=F32)
        res_gate = modb_ref[0, 5:6, :]
        if ctx_rows:
            row = lax.broadcasted_iota(jnp.int32, y.shape, 0)
            first = i % tiles_per_seq == 0
            res_gate = jnp.where(jnp.logical_and(first, row < ctx_rows), modc_ref[0, 5:6, :], res_gate)
        o_ref[...] = x_ref[...] + res_gate * y


def _moe_call(lay, with_ctx, f, comb, gid, wg, wu, wd, x, mod):
    t = f.shape[0]
    seq = lay.n_tot if with_ctx else lay.n_lat
    tm = MOE_TILE if seq % MOE_TILE == 0 else math.gcd(seq, 1024)
    tps, n_tiles = seq // tm, t // tm
    ctx_rows = lay.n_ctx if with_ctx else 0
    assert ctx_rows <= tm
    n_slots = -(-(tm + 16 * N_GROUPS + MOE_BLOCK + MOE_OVER) // 256) * 256
    pos, offs, n_over = _moe_plan(gid, n_tiles, tm)
    wspec = lambda a: pl.BlockSpec((1,) + a.shape[1:], lambda i, e, *_: (e, 0, 0))
    tok = lambda w: pl.BlockSpec((tm, w), lambda i, e, *_: (i, 0))
    grid_spec = pltpu.PrefetchScalarGridSpec(
        num_scalar_prefetch=2,
        grid=(n_tiles, N_EXPERTS),
        in_specs=[tok(D_MODEL), pl.BlockSpec((1, 1, tm), lambda i, e, *_: (i, 0, 0)), tok(1), tok(N_EXPERTS),
                  wspec(wg), wspec(wu), wspec(wd), tok(D_MODEL),
                  pl.BlockSpec((1, 6, D_MODEL), lambda i, e, *_: (i // tps, 0, 0)),
                  pl.BlockSpec((1, 6, D_MODEL), lambda i, e, *_: (lay.b, 0, 0))],
        out_specs=tok(D_MODEL),
        scratch_shapes=[pltpu.VMEM((n_slots, D_MODEL), BF16), pltpu.VMEM((n_slots, N_EXPERTS), F32),
                        pltpu.VMEM((n_slots, D_MODEL), F32)])
    return pl.pallas_call(
        functools.partial(_moe_kernel, ctx_rows=ctx_rows, tiles_per_seq=tps),
        grid_spec=grid_spec,
        out_shape=jax.ShapeDtypeStruct((t, D_MODEL), F32),
        compiler_params=_cparams("parallel", "arbitrary"),
        name="moe_experts",
    )(offs, n_over, f, pos.reshape(n_tiles, 1, tm), pos.reshape(t, 1), comb, wg, wu, wd, x, mod, mod)


def _block_ones(n, group):
    i = np.arange(n) // group
    return jnp.asarray(i[:, None] == i[None, :], dtype=BF16)


def _rope_tables(n_ctx, n_lat):
    rows = n_lat // GRID_W
    row = jnp.repeat(jnp.arange(rows, dtype=F32), GRID_W)
    col = jnp.tile(jnp.arange(GRID_W, dtype=F32), rows)

    def angles(rot_dim):
        n_freq = rot_dim // 4
        inv_freq = ROPE_BASE ** (-jnp.arange(n_freq, dtype=F32) / n_freq)
        ang = jnp.concatenate([row[:, None] * inv_freq, col[:, None] * inv_freq], axis=-1)
        return jnp.cos(ang), jnp.sin(ang)

    c, s = angles(DA_DIM)
    cda = jnp.tile(jnp.concatenate([c, c], -1), (1, 2 * DA_HEADS))
    sda = jnp.tile(jnp.concatenate([-s, s], -1), (1, 2 * DA_HEADS))
    c, s = angles(MLA_ROPE)
    one = jnp.ones((n_lat, MLA_NOPE), F32)
    pad = MLA_HEAD_PAD - MLA_NOPE - MLA_ROPE
    cml = jnp.tile(jnp.concatenate([one, c, c, jnp.ones((n_lat, pad), F32)], -1), (1, MLA_HEADS))
    sml = jnp.tile(jnp.concatenate([0 * one, -s, s, jnp.zeros((n_lat, pad), F32)], -1), (1, MLA_HEADS))
    ident = lambda t, v: jnp.concatenate([jnp.full((n_ctx, MIX_W), v, F32), t], axis=0)
    return ident(cda, 1.0), ident(sda, 0.0), ident(cml, 1.0), ident(sml, 0.0)


def _pad_heads(w, n_heads, src_w, lo, hi, dst_w=MLA_HEAD_PAD):
    w = w.reshape(w.shape[0], n_heads, src_w)[:, :, lo:hi]
    w = jnp.pad(w, ((0, 0), (0, 0), (0, dst_w - (hi - lo))))
    return w.reshape(w.shape[0], n_heads * dst_w)


def _mix_weight(w_in_l):
    w = w_in_l
    kr = w[:, 1344:1360]
    z = lambda n: jnp.zeros((D_MODEL, n), w.dtype)
    kr_wide = jnp.concatenate([jnp.concatenate([z(MLA_NOPE), kr, z(MLA_HEAD_PAD - MLA_NOPE - MLA_ROPE)], 1)] * MLA_HEADS, 1)
    return jnp.concatenate([w[:, 0:1024], w[:, 1024:1216], z(64), w[:, 1216:1344], kr_wide, w[:, 1360:2384]], axis=1).astype(BF16)


def kernel(x, c, ctx, c_ctx, w_ada, b_ada, norm_mix_g, norm_ffn_g, w_in, da_qk_norm_g, da_lambda, da_subln_g, s5_lam_re, s5_lam_im, s5_log_dt, s5_b_re, s5_b_im, s5_c_re, s5_c_im, s5_d, s5_w_glu, s5_b_glu, mla_cq_norm_g, mla_ckv_norm_g, mla_w_uq, mla_w_ukv, mla_qk_norm_g, rw_mu, rw_w0, rw_w1, rw_w2, rw_a0, rw_a1, rw_a2, rw_g1, rw_g2, rw_k_k, rw_k_a, rw_r_k, rw_ln_g, rw_ln_b, w_branch, w_out, router_w, router_bias, exp_w_gate, exp_w_up, exp_w_down):
    b, n_lat, dm = x.shape
    n_ctx = ctx.shape[1]
    depth = w_ada.shape[0]
    assert dm == D_MODEL
    lay = _Layout(b, n_ctx, n_lat)
    t_all = b * lay.n_tot
    tm_big = _seq_tile(lay)[0]

    g32 = _block_ones(MIX_W, DA_DIM)
    g64 = _block_ones(MIX_W, RW_DIM)
    tabs = _rope_tables(n_ctx, n_lat)
    row = lambda v: v.reshape(1, -1).astype(F32)
    bf = lambda a: a.astype(BF16)

    cc = jnp.zeros((16, dm), F32).at[:b].set(c).at[b].set(c_ctx)
    mod_all = _ada_call(cc, w_ada, b_ada)
    x_all = jnp.concatenate([ctx, x], axis=1).reshape(t_all, dm)

    s5_mats = jax.vmap(_s5_mats)(s5_lam_re, s5_lam_im, s5_log_dt, s5_b_re, s5_b_im, s5_c_re, s5_c_im)

    wr_hi = router_w.T.astype(BF16)
    wr_lo = (router_w.T - wr_hi.astype(F32)).astype(BF16)
    r_bias = router_bias.reshape(N_EXPERTS, 1).astype(F32)

    for l in range(depth):
        need_ctx = l < depth - 1
        lambda_init = 0.8 - 0.6 * math.exp(-0.3 * l)
        mod = mod_all[l, :b + 1].reshape(b + 1, 6, dm)
        g_mix = row(norm_mix_g[l])
        da, s5a, s5b, mla, rw = _inproj_call(lay, x_all, mod, g_mix, _mix_weight(w_in[l]))

        log2e = math.log2(math.e)
        gda = jnp.stack([jnp.tile(da_qk_norm_g[l, 0], 2 * DA_HEADS) * (DA_DIM ** -0.5 * log2e), jnp.tile(da_qk_norm_g[l, 1], 2 * DA_HEADS)])
        mla_pad = MLA_HEAD_PAD - MLA_NOPE - MLA_ROPE
        gml = jnp.stack([jnp.tile(jnp.pad(mla_qk_norm_g[l, 0], (0, mla_pad)), MLA_HEADS) * ((MLA_NOPE + MLA_ROPE) ** -0.5 * log2e),
                         jnp.tile(jnp.pad(mla_qk_norm_g[l, 1], (0, mla_pad)), MLA_HEADS)])
        wuq = bf(jnp.pad(_pad_heads(mla_w_uq[l], MLA_HEADS, MLA_NOPE + MLA_ROPE, 0, MLA_NOPE + MLA_ROPE), ((0, 64), (0, 0))))
        wuk = bf(_pad_heads(mla_w_ukv[l], MLA_HEADS, MLA_NOPE + MLA_VDIM, 0, MLA_NOPE))
        wuv = bf(_pad_heads(mla_w_ukv[l], MLA_HEADS, MLA_NOPE + MLA_VDIM, MLA_NOPE, MLA_NOPE + MLA_VDIM))
        consts = (g32, g64, gda.astype(F32), gml.astype(F32), row(jnp.pad(mla_cq_norm_g[l], (0, 64))), row(mla_ckv_norm_g[l]),
                  wuq, wuk, wuv)
        qd, kdt, vd, qm, kmt, vm = _qkprep_call(lay, da, mla, tabs, consts)

        lam32 = da_lambda[l].astype(F32)
        lmbda = (jnp.exp(jnp.sum(lam32[0] * lam32[1])) - jnp.exp(jnp.sum(lam32[2] * lam32[3])) + lambda_init).reshape(1, 1)
        subln = row(jnp.tile(da_subln_g[l], DA_HEADS) * (1.0 - lambda_init))
        ya = _attention(lay, qd, kdt, vd, (lmbda, subln, g64), True, need_ctx, "diff_attn")
        yc = _attention(lay, qm, kmt, vm, (lmbda, subln, g64), False, need_ctx, "mla_attn")

        ys_a, ys_b = _s5_scan(lay, s5a, s5b, s5_mats, l)
        yb = _s5_glu_call(s5a, s5b, ys_a, ys_b, row(s5_d[l]), bf(s5_w_glu[l]), row(s5_b_glu[l]), tm_big)

        pre_consts = (row(rw_mu[l]), g64, row(rw_k_k[l]), row(rw_k_a[l]), row(rw_r_k[l]),
                      rw_w0[l].reshape(2, 1, MIX_W), bf(rw_w1[l]), bf(rw_w2[l]),
                      rw_a0[l].reshape(2, 1, MIX_W), bf(rw_a1[l]), bf(rw_a2[l]), bf(rw_g1[l]), bf(rw_g2[l]))
        yd = _rwkv_branch(lay, rw, pre_consts, g64, row(rw_ln_g[l]), row(rw_ln_b[l]))

        x_mid = _merge_call(lay, need_ctx, x_all, mod, g_mix, bf(w_in[l][:, 2384:]), ya, yb, yc, yd,
                            bf(w_branch[l]), bf(w_out[l]))
        f, comb_t, gid = _router_call(lay, need_ctx, x_mid, mod, row(norm_ffn_g[l]), wr_hi, wr_lo, r_bias)
        x_all = _moe_call(lay, need_ctx, f, comb_t.T, gid, bf(exp_w_gate[l]), bf(exp_w_up[l]), bf(exp_w_down[l]), x_mid, mod)
    return x_all.reshape(b, n_lat, dm)
```
